```python
import math
import jax
import jax.numpy as jnp
from jax import lax
import numpy as np

D_MODEL = 1024
BATCH = 2
SEQ = 8192
DEPTH = 1

CHUNK = 64
N_META = 16
D_MIX = D_MODEL
EPS = 1e-6
CONV_CH = D_MIX // 2
CONV_GROUPS = 8
CONV_K_MIX = 3
GDN_HEADS = 4
GDN_HEAD_DIM = 128
GDN_W = GDN_HEADS * GDN_HEAD_DIM
CONV_K_QKV = 4
GDN_CHUNK = CHUNK
PROJ_COLS = 3 * CONV_CH + 3 * GDN_W + GDN_W + 2 * GDN_HEADS
SPLIT_AT = [CONV_CH, 2 * CONV_CH, 3 * CONV_CH, 3 * CONV_CH + 3 * GDN_W,
            3 * CONV_CH + 4 * GDN_W, 3 * CONV_CH + 4 * GDN_W + GDN_HEADS]
N_GROUPS = 4
EXPERTS_PER_GROUP = 8
N_EXPERTS = N_GROUPS * EXPERTS_PER_GROUP
TOP_K = 2
D_EXPERT = D_MODEL // 2
EXPERT_BLOCK = 128

kernel_name = "hymba_conv_gdn_hmoe_layer"


def rms_norm(x, w):
    xf = x.astype(jnp.float32)
    y = xf * lax.rsqrt(jnp.mean(xf * xf, axis=-1, keepdims=True) + EPS)
    return (y * w.astype(jnp.float32)).astype(x.dtype)


def l2_normalize(x):
    xf = x.astype(jnp.float32)
    return xf * lax.rsqrt(jnp.sum(xf * xf, axis=-1, keepdims=True) + EPS)


def causal_depthwise_conv(x, w):
    K = w.shape[0]
    L = x.shape[1]
    xp = jnp.pad(x, ((0, 0), (K - 1, 0), (0, 0)))
    acc = xp[:, 0:L] * w[0]
    for j in range(1, K):
        acc = acc + xp[:, j:j + L] * w[j]
    return acc


def gated_delta_rule(q, k, v, g, beta):
    B, H, L, DK = q.shape
    DV = v.shape[-1]
    n_chunks = L // GDN_CHUNK

    def blocks(t):
        return t.reshape(B, H, n_chunks, GDN_CHUNK, *t.shape[3:])

    q, k, v, g, beta = (blocks(t) for t in (q * (DK ** -0.5), k, v, g, beta))
    g = jnp.cumsum(g, axis=-1)
    idx = jnp.arange(GDN_CHUNK)
    incl = idx[:, None] >= idx[None, :]
    strict = idx[:, None] > idx[None, :]
    decay = jnp.exp(jnp.where(incl, g[..., :, None] - g[..., None, :], -jnp.inf))
    kb = k * beta[..., None]
    lower = jnp.where(strict, jnp.einsum('bhnid,bhnjd->bhnij', kb, k) * decay, 0.0)
    eye = jnp.eye(GDN_CHUNK, dtype=q.dtype)
    rhs = jnp.concatenate([v * beta[..., None], kb * jnp.exp(g)[..., None]], axis=-1)
    sol = lax.linalg.triangular_solve(lower + eye, rhs, left_side=True, lower=True,
                                      unit_diagonal=True)
    u, w = sol[..., :DV], sol[..., DV:]
    intra = jnp.einsum('bhnid,bhnjd->bhnij', q, k) * decay
    g_last = g[..., -1]
    qg = q * jnp.exp(g)[..., None]
    kd = k * jnp.exp(g_last[..., None] - g)[..., None]
    a_last = jnp.exp(g_last)
    xs = tuple(jnp.moveaxis(t, 2, 0) for t in (qg, kd, u, w, intra, a_last))

    def step(S, inp):
        qg_c, kd_c, u_c, w_c, intra_c, a_c = inp
        v_new = u_c - jnp.einsum('bhik,bhkv->bhiv', w_c, S)
        o_c = (jnp.einsum('bhik,bhkv->bhiv', qg_c, S)
               + jnp.einsum('bhij,bhjv->bhiv', intra_c, v_new))
        S = S * a_c[..., None, None] + jnp.einsum('bhik,bhiv->bhkv', kd_c, v_new)
        return S, o_c

    S0 = jnp.zeros((B, H, DK, DV), q.dtype)
    _, o = lax.scan(step, S0, xs)
    return jnp.moveaxis(o, 0, 2).reshape(B, H, L, DV)


def hybrid_mixer(xn, w_in, conv_mix_w, conv_mix_norm_w, qkv_conv_w, a_log, dt_bias,
                 gdn_norm_w, w_out):
    B, L, _ = xn.shape
    proj = xn @ w_in
    gate_b, gate_c, u, qkv, z, a, b = jnp.split(proj, SPLIT_AT, axis=-1)

    y_a = gate_b * causal_depthwise_conv(gate_c * u, conv_mix_w)
    y_a = rms_norm(y_a.reshape(B, L, CONV_GROUPS, CONV_CH // CONV_GROUPS),
                   conv_mix_norm_w.reshape(CONV_GROUPS, CONV_CH // CONV_GROUPS))
    y_a = y_a.reshape(B, L, CONV_CH)

    qkv = jax.nn.silu(causal_depthwise_conv(qkv, qkv_conv_w))
    q, k, v = jnp.split(qkv, 3, axis=-1)
    q = l2_normalize(q.reshape(B, L, GDN_HEADS, GDN_HEAD_DIM))
    k = l2_normalize(k.reshape(B, L, GDN_HEADS, GDN_HEAD_DIM))
    v = v.reshape(B, L, GDN_HEADS, GDN_HEAD_DIM).astype(jnp.float32)
    beta = jax.nn.sigmoid(b.astype(jnp.float32))
    g = -jnp.exp(a_log.astype(jnp.float32)) * jax.nn.softplus(
        a.astype(jnp.float32) + dt_bias.astype(jnp.float32))
    pad = (-L) % GDN_CHUNK
    to_bhl = lambda t: jnp.pad(jnp.swapaxes(t, 1, 2),
                               [(0, 0), (0, 0), (0, pad)] + [(0, 0)] * (t.ndim - 3))
    o = gated_delta_rule(to_bhl(q), to_bhl(k), to_bhl(v), to_bhl(g), to_bhl(beta))
    o = jnp.swapaxes(o[:, :, :L], 1, 2).astype(xn.dtype)
    o = rms_norm(o, gdn_norm_w) * jax.nn.silu(z.reshape(B, L, GDN_HEADS, GDN_HEAD_DIM))
    y_b = o.reshape(B, L, GDN_W)

    return jnp.concatenate([y_a, y_b], axis=-1) @ w_out


def grouped_expert_ffn(xf, expert_idx, expert_w, w_gate, w_up, w_down):
    T, D = xf.shape
    n_assign = T * TOP_K
    n_blocks = (n_assign + N_EXPERTS * (EXPERT_BLOCK - 1) + EXPERT_BLOCK - 1) // EXPERT_BLOCK
    n_rows = n_blocks * EXPERT_BLOCK
    flat_e = expert_idx.reshape(-1).astype(jnp.int32)
    flat_w = expert_w.reshape(-1)
    flat_tok = jnp.arange(n_assign, dtype=jnp.int32) // TOP_K
    order = jnp.argsort(flat_e)
    sorted_e = flat_e[order]
    counts = jnp.bincount(flat_e, length=N_EXPERTS).astype(jnp.int32)
    padded = (counts + EXPERT_BLOCK - 1) // EXPERT_BLOCK * EXPERT_BLOCK
    pad_end = jnp.cumsum(padded)
    pad_start = pad_end - padded
    start = jnp.cumsum(counts) - counts
    dest = pad_start[sorted_e] + jnp.arange(n_assign, dtype=jnp.int32) - start[sorted_e]
    row_tok = jnp.full((n_rows,), T, jnp.int32).at[dest].set(flat_tok[order])
    row_w = jnp.zeros((n_rows,), xf.dtype).at[dest].set(flat_w[order].astype(xf.dtype))
    block_start = jnp.arange(n_blocks, dtype=jnp.int32) * EXPERT_BLOCK
    block_expert = jnp.minimum(jnp.searchsorted(pad_end, block_start, side='right'),
                               N_EXPERTS - 1).astype(jnp.int32)
    x_rows = jnp.concatenate([xf, jnp.zeros((1, D), xf.dtype)], axis=0)[row_tok]
    x_rows = x_rows.reshape(n_blocks, EXPERT_BLOCK, D)

    def expert_block(args):
        xb, e = args
        hid = jax.nn.silu(xb @ w_gate[e]) * (xb @ w_up[e])
        return hid @ w_down[e]

    y_rows = lax.map(expert_block, (x_rows, block_expert)).reshape(n_rows, D)
    out = jnp.zeros((T + 1, D), xf.dtype).at[row_tok].add(y_rows * row_w[:, None])
    return out[:T]


def hierarchical_moe(xn, w_group, b_group, w_router, b_router, w_gate, w_up, w_down):
    B, L, D = xn.shape
    xf = xn.reshape(-1, D)
    xr = xf.astype(jnp.float32)
    grp_prob = jax.nn.softmax(xr @ w_group.astype(jnp.float32) + b_group, axis=-1)
    p_grp, g_sel = lax.top_k(grp_prob, 1)
    exp_logits = jnp.einsum('td,dge->tge', xr, w_router.astype(jnp.float32)) + b_router
    exp_logits = jnp.take_along_axis(exp_logits, g_sel[:, :, None], axis=1)[:, 0]
    w_top, e_top = lax.top_k(jax.nn.softmax(exp_logits, axis=-1), TOP_K)
    w_top = w_top / jnp.sum(w_top, axis=-1, keepdims=True) * p_grp
    expert_idx = g_sel * EXPERTS_PER_GROUP + e_top
    y = grouped_expert_ffn(xf, expert_idx, w_top, w_gate, w_up, w_down)
    return y.reshape(B, L, D)


def setup_inputs(seed: int = 0) -> dict:
    key = jax.random.key(seed)
    ks = jax.random.split(key, 24)
    f32 = jnp.float32
    nrm = lambda k, shape, scale: jax.random.normal(k, shape, f32) * scale
    gain = lambda k, shape: 1.0 + 0.02 * jax.random.normal(k, shape, f32)
    dt = jnp.exp(jax.random.uniform(ks[8], (DEPTH, GDN_HEADS), f32,
                                    math.log(1e-3), math.log(1e-1)))
    dt_bias = dt + jnp.log(-jnp.expm1(-dt))
    a_log = jnp.log(jax.random.uniform(ks[7], (DEPTH, GDN_HEADS), f32, 1.0, 16.0))
    return {
        'x': jax.random.normal(ks[0], (BATCH, SEQ, D_MODEL), f32),
        'meta_tokens': nrm(ks[1], (N_META, D_MODEL), 1.0),
        'mix_norm_w': gain(ks[2], (DEPTH, D_MODEL)),
        'w_in': nrm(ks[3], (DEPTH, D_MODEL, PROJ_COLS), D_MODEL ** -0.5),
        'conv_mix_w': nrm(ks[4], (DEPTH, CONV_K_MIX, CONV_CH), CONV_K_MIX ** -0.5),
        'conv_mix_norm_w': gain(ks[5], (DEPTH, CONV_CH)),
        'qkv_conv_w': nrm(ks[6], (DEPTH, CONV_K_QKV, 3 * GDN_W), CONV_K_QKV ** -0.5),
        'a_log': a_log,
        'dt_bias': dt_bias,
        'gdn_norm_w': gain(ks[9], (DEPTH, GDN_HEAD_DIM)),
        'w_out': nrm(ks[10], (DEPTH, D_MIX, D_MODEL), D_MIX ** -0.5),
        'ffn_norm_w': gain(ks[11], (DEPTH, D_MODEL)),
        'w_group': nrm(ks[12], (DEPTH, D_MODEL, N_GROUPS), D_MODEL ** -0.5),
        'b_group': nrm(ks[13], (DEPTH, N_GROUPS), 0.01),
        'w_router': nrm(ks[14], (DEPTH, D_MODEL, N_GROUPS, EXPERTS_PER_GROUP), D_MODEL ** -0.5),
        'b_router': nrm(ks[15], (DEPTH, N_GROUPS, EXPERTS_PER_GROUP), 0.01),
        'w_gate': nrm(ks[16], (DEPTH, N_EXPERTS, D_MODEL, D_EXPERT), D_MODEL ** -0.5),
        'w_up': nrm(ks[17], (DEPTH, N_EXPERTS, D_MODEL, D_EXPERT), D_MODEL ** -0.5),
        'w_down': nrm(ks[18], (DEPTH, N_EXPERTS, D_EXPERT, D_MODEL), D_EXPERT ** -0.5),
        'final_norm_w': gain(ks[19], (D_MODEL,)),
    }


def reference(x, meta_tokens, mix_norm_w, w_in, conv_mix_w, conv_mix_norm_w, qkv_conv_w,
              a_log, dt_bias, gdn_norm_w, w_out, ffn_norm_w, w_group, b_group, w_router,
              b_router, w_gate, w_up, w_down, final_norm_w):
    B = x.shape[0]
    meta = jnp.broadcast_to(meta_tokens.astype(x.dtype)[None], (B, N_META, D_MODEL))
    h = jnp.concatenate([meta, x], axis=1)
    for layer in range(DEPTH):
        xn = rms_norm(h, mix_norm_w[layer])
        h = h + hybrid_mixer(xn, w_in[layer], conv_mix_w[layer], conv_mix_norm_w[layer],
                             qkv_conv_w[layer], a_log[layer], dt_bias[layer],
                             gdn_norm_w[layer], w_out[layer])
        xn = rms_norm(h, ffn_norm_w[layer])
        h = h + hierarchical_moe(xn, w_group[layer], b_group[layer], w_router[layer],
                                 b_router[layer], w_gate[layer], w_up[layer], w_down[layer])
    out = rms_norm(h, final_norm_w)
    return out[:, N_META:]
```

```python
import functools

import jax
import jax.numpy as jnp
from jax import lax
from jax.experimental import pallas as pl
from jax.experimental.pallas import tpu as pltpu

F32 = jnp.float32
BF16 = jnp.bfloat16
EPS = 1e-6

D_MODEL = 1024
N_META = 16
CONV_CH = 512
CONV_GROUP_W = 64
HEADS = 4
HEAD_DIM = 128
GDN_W = HEADS * HEAD_DIM
CHUNK = 64
N_GROUPS = 4
EXPERTS_PER_GROUP = 8
N_EXPERTS = N_GROUPS * EXPERTS_PER_GROUP
TOP_K = 2
D_EXPERT = 512
LANES = 128
HIST = 8

PROJ_ROWS = 512
GDN_CHUNKS = 8
MIX_ROWS = 512
EXPERT_ROWS = 256
COMBINE_ROWS = 256
VMEM_LIMIT = 56 * 1024 * 1024


def _dot(a, b):
    return jnp.dot(a, b, preferred_element_type=F32)


def _dot_nt(a, b):
    return lax.dot_general(a, b, (((1,), (1,)), ((), ())), preferred_element_type=F32)


def _dot_tn(a, b):
    return lax.dot_general(a, b, (((0,), (0,)), ((), ())), preferred_element_type=F32)


def _split_bf16(x):
    hi = x.astype(BF16)
    lo = (x - hi.astype(F32)).astype(BF16)
    return hi, lo


def _sigmoid(x):
    return 1.0 / (1.0 + jnp.exp(-x))


def _proj_kernel(x_ref, hcu_ref, hqkv_ref, nw_ref, wa_ref, wq_ref, wz_ref, wab_ref, cmw_ref,
                 cmn_ref, gmat_ref, qcw_ref, alog_ref, dtb_ref,
                 ya_ref, q_ref, k_ref, v_ref, z_ref, gb_ref, tcu_ref, tqkv_ref,
                 cu_s, qkv_s):
    rows = x_ref.shape[0]

    @pl.when(pl.program_id(1) == 0)
    def _():
        cu_s[0:HIST, :] = hcu_ref[...]
        qkv_s[0:HIST, :] = hqkv_ref[...]

    x = x_ref[...]
    ms = jnp.mean(x * x, axis=-1, keepdims=True)
    xn = (x * lax.rsqrt(ms + EPS) * nw_ref[...]).astype(BF16)

    pa = _dot(xn, wa_ref[...])
    gate_b = pa[:, :CONV_CH]
    cu = pa[:, CONV_CH:2 * CONV_CH] * pa[:, 2 * CONV_CH:]
    cu_s[HIST:HIST + rows, :] = cu
    conv = (cu_s[pl.ds(HIST - 2, rows), :] * cmw_ref[0:1, :]
            + cu_s[pl.ds(HIST - 1, rows), :] * cmw_ref[1:2, :]
            + cu * cmw_ref[2:3, :])
    ya = gate_b * conv
    sq_hi, sq_lo = _split_bf16(ya * ya)
    msg = _dot(sq_hi, gmat_ref[...]) + _dot(sq_lo, gmat_ref[...])
    ya_ref[...] = (ya * lax.rsqrt(msg + EPS) * cmn_ref[...]).astype(BF16)
    tail_cu = cu_s[rows:rows + HIST, :]
    cu_s[0:HIST, :] = tail_cu
    tcu_ref[...] = tail_cu

    pq = _dot(xn, wq_ref[...])
    qkv_s[HIST:HIST + rows, :] = pq
    c = (qkv_s[pl.ds(HIST - 3, rows), :] * qcw_ref[0:1, :]
         + qkv_s[pl.ds(HIST - 2, rows), :] * qcw_ref[1:2, :]
         + qkv_s[pl.ds(HIST - 1, rows), :] * qcw_ref[2:3, :]
         + pq * qcw_ref[3:4, :])
    c = c * _sigmoid(c)
    tail_qkv = qkv_s[rows:rows + HIST, :]
    qkv_s[0:HIST, :] = tail_qkv
    tqkv_ref[...] = tail_qkv
    for h in range(HEADS):
        sl = slice(h * HEAD_DIM, (h + 1) * HEAD_DIM)
        qh = c[:, h * HEAD_DIM:(h + 1) * HEAD_DIM]
        kh = c[:, GDN_W + h * HEAD_DIM:GDN_W + (h + 1) * HEAD_DIM]
        qn = qh * lax.rsqrt(jnp.sum(qh * qh, axis=-1, keepdims=True) + EPS)
        q_ref[:, sl] = qn * (HEAD_DIM ** -0.5)
        k_ref[:, sl] = kh * lax.rsqrt(jnp.sum(kh * kh, axis=-1, keepdims=True) + EPS)
    v_ref[...] = c[:, 2 * GDN_W:]
    z_ref[...] = _dot(xn, wz_ref[...])

    ab = _dot(xn, wab_ref[...])
    sp_in = ab + dtb_ref[...]
    softplus = jnp.maximum(sp_in, 0.0) + jnp.log1p(jnp.exp(-jnp.abs(sp_in)))
    g = -jnp.exp(alog_ref[...]) * softplus
    lane = lax.broadcasted_iota(jnp.int32, ab.shape, 1)
    gb_ref[...] = jnp.where(lane < HEADS, g, jnp.where(lane < 2 * HEADS, _sigmoid(ab), 0.0))


def _proj_call(x3, hist_cu, hist_qkv, w, rows):
    nb, seq, _ = x3.shape
    nt = seq // rows
    tok = lambda width: pl.BlockSpec((None, rows, width), lambda b, t: (b, t, 0))
    full = lambda a: pl.BlockSpec(a.shape, lambda b, t: (0,) * a.ndim)
    tail = lambda width: pl.BlockSpec((None, HIST, width), lambda b, t: (b, 0, 0))
    consts = (hist_cu, hist_qkv, w['mix_norm_w'], w['w_a'], w['w_qkv'], w['w_z'], w['w_ab'],
              w['conv_mix_w'], w['conv_mix_norm_w'], w['gmat'], w['qkv_conv_w'], w['a_log'],
              w['dt_bias'])
    out_shape = (
        jax.ShapeDtypeStruct((nb, seq, CONV_CH), BF16),
        jax.ShapeDtypeStruct((nb, seq, GDN_W), F32),
        jax.ShapeDtypeStruct((nb, seq, GDN_W), F32),
        jax.ShapeDtypeStruct((nb, seq, GDN_W), F32),
        jax.ShapeDtypeStruct((nb, seq, GDN_W), F32),
        jax.ShapeDtypeStruct((nb, seq, LANES), F32),
        jax.ShapeDtypeStruct((nb, HIST, CONV_CH), F32),
        jax.ShapeDtypeStruct((nb, HIST, 3 * GDN_W), F32),
    )
    return pl.pallas_call(
        _proj_kernel,
        grid=(nb, nt),
        in_specs=[tok(D_MODEL)] + [full(a) for a in consts],
        out_specs=(tok(CONV_CH), tok(GDN_W), tok(GDN_W), tok(GDN_W), tok(GDN_W), tok(LANES),
                   tail(CONV_CH), tail(3 * GDN_W)),
        out_shape=out_shape,
        scratch_shapes=[pltpu.VMEM((rows + HIST, CONV_CH), F32),
                        pltpu.VMEM((rows + HIST, 3 * GDN_W), F32)],
        compiler_params=pltpu.CompilerParams(
            dimension_semantics=("arbitrary", "arbitrary"), vmem_limit_bytes=VMEM_LIMIT),
        name="proj",
    )(x3, *consts)


def _chunk_masks():
    row = lax.broadcasted_iota(jnp.int32, (CHUNK, CHUNK), 0)
    col = lax.broadcasted_iota(jnp.int32, (CHUNK, CHUNK), 1)
    incl = row >= col
    strict = row > col
    levels = []
    n = 1
    while n < CHUNK:
        levels.append((row // (2 * n) == col // (2 * n)) & ((row // n) % 2 == 1) & ((col // n) % 2 == 0))
        n *= 2
    return incl, strict, levels


def _chunk_cumsum(gb_blk, incl):
    tri = incl.astype(BF16)
    hi, lo = _split_bf16(gb_blk)
    return _dot(tri, hi) + _dot(tri, lo)


def _chunk_transform(qh, kh, vh, beta, gc_col, gc_row, g_last, masks, state_only):
    incl, strict, levels = masks
    decay = jnp.exp(jnp.where(incl, gc_col - gc_row, -jnp.inf))
    k_bf = kh.astype(BF16)
    kb = kh * beta
    a_mat = jnp.where(strict, _dot_nt(kb.astype(BF16), k_bf) * decay, 0.0)
    eye = (lax.broadcasted_iota(jnp.int32, (CHUNK, CHUNK), 0)
           == lax.broadcasted_iota(jnp.int32, (CHUNK, CHUNK), 1)).astype(F32)
    t_inv = eye - jnp.where(levels[0], a_mat, 0.0)
    for lvl in levels[1:]:
        t_bf = t_inv.astype(BF16)
        m1 = _dot(jnp.where(lvl, a_mat, 0.0).astype(BF16), t_bf)
        t_inv = t_inv - _dot(t_bf, m1.astype(BF16))
    rhs = jnp.concatenate([vh * beta, kb * jnp.exp(gc_col)], axis=1)
    uw = _dot(t_inv.astype(BF16), rhs.astype(BF16)).astype(BF16)
    kd = kh * jnp.exp(g_last - gc_col)
    pn = _dot_tn(kd.astype(BF16), uw)
    n_mat = pn[:, :HEAD_DIM]
    if state_only:
        return n_mat
    intra = jnp.where(incl, _dot_nt(qh.astype(BF16), k_bf) * decay, 0.0)
    iuw = _dot(intra.astype(BF16), uw)
    o_part = iuw[:, :HEAD_DIM]
    q_part = qh * jnp.exp(gc_col) - iuw[:, HEAD_DIM:]
    return q_part, pn[:, HEAD_DIM:], o_part, n_mat, jnp.exp(g_last)


def _gdn_kernel(pq_ref, pk_ref, pv_ref, pgb_ref, q_ref, k_ref, v_ref, gb_ref, o_ref,
                s_s, qp_s, op_s, n_s, a_s):
    nb = q_ref.shape[0]
    n_chunks = q_ref.shape[1] // CHUNK
    masks = _chunk_masks()

    def head_inputs(gb_blk, h):
        gc = _chunk_cumsum(gb_blk, masks[0])
        gc_t = gc.T
        return (gb_blk[:, HEADS + h:HEADS + h + 1], gc[:, h:h + 1], gc_t[h:h + 1, :CHUNK],
                gc[CHUNK - 1:CHUNK, h:h + 1])

    @pl.when(pl.program_id(0) == 0)
    def _():
        gb_blk = pgb_ref[...]
        for h in range(HEADS):
            sl = slice(h * HEAD_DIM, (h + 1) * HEAD_DIM)
            beta, gc_col, gc_row, g_last = head_inputs(gb_blk, h)
            n_mat = _chunk_transform(pq_ref[:, sl], pk_ref[:, sl], pv_ref[:, sl], beta, gc_col,
                                     gc_row, g_last, masks, True)
            for b in range(nb):
                s_s[b * HEADS + h] = n_mat

    def transform_chunk(c, carry):
        r0 = pl.multiple_of(c * CHUNK, CHUNK)
        for b in range(nb):
            gb_blk = gb_ref[b, pl.ds(r0, CHUNK), :]
            for h in range(HEADS):
                sl = slice(h * HEAD_DIM, (h + 1) * HEAD_DIM)
                beta, gc_col, gc_row, g_last = head_inputs(gb_blk, h)
                q_part, p_mat, o_part, n_mat, a = _chunk_transform(
                    q_ref[b, pl.ds(r0, CHUNK), sl], k_ref[b, pl.ds(r0, CHUNK), sl],
                    v_ref[b, pl.ds(r0, CHUNK), sl], beta, gc_col, gc_row, g_last, masks, False)
                ch = b * HEADS + h
                qp_s[c, ch, 0:CHUNK, :] = q_part.astype(BF16)
                qp_s[c, ch, CHUNK:, :] = p_mat.astype(BF16)
                op_s[c, ch] = o_part
                n_s[c, ch] = n_mat
                a_s[c, ch] = jnp.broadcast_to(a, (8, HEAD_DIM))
        return carry

    lax.fori_loop(0, n_chunks, transform_chunk, 0)

    def scan_chunk(c, carry):
        r0 = pl.multiple_of(c * CHUNK, CHUNK)
        for b in range(nb):
            for h in range(HEADS):
                ch = b * HEADS + h
                s = s_s[ch]
                r = _dot(qp_s[c, ch], s.astype(BF16))
                o_ref[b, pl.ds(r0, CHUNK), h * HEAD_DIM:(h + 1) * HEAD_DIM] = r[:CHUNK] + op_s[c, ch]
                s_s[ch] = a_s[c, ch][0:1, :] * s - r[CHUNK:] + n_s[c, ch]
        return carry

    lax.fori_loop(0, n_chunks, scan_chunk, 0)


def _gdn_call(pq, pk, pv, pgb, q, k, v, gb, chunks_per_step):
    nb, seq, _ = q.shape
    rows = chunks_per_step * CHUNK
    steps = seq // rows
    tok = lambda width: pl.BlockSpec((nb, rows, width), lambda i: (0, i, 0))
    pre = lambda width: pl.BlockSpec((None, CHUNK, width), lambda i: (0, 0, 0))
    nch = nb * HEADS
    return pl.pallas_call(
        _gdn_kernel,
        grid=(steps,),
        in_specs=[pre(GDN_W), pre(GDN_W), pre(GDN_W), pre(LANES),
                  tok(GDN_W), tok(GDN_W), tok(GDN_W), tok(LANES)],
        out_specs=tok(GDN_W),
        out_shape=jax.ShapeDtypeStruct((nb, seq, GDN_W), F32),
        scratch_shapes=[
            pltpu.VMEM((nch, HEAD_DIM, HEAD_DIM), F32),
            pltpu.VMEM((chunks_per_step, nch, CHUNK + HEAD_DIM, HEAD_DIM), BF16),
            pltpu.VMEM((chunks_per_step, nch, CHUNK, HEAD_DIM), F32),
            pltpu.VMEM((chunks_per_step, nch, HEAD_DIM, HEAD_DIM), F32),
            pltpu.VMEM((chunks_per_step, nch, 8, HEAD_DIM), F32),
        ],
        compiler_params=pltpu.CompilerParams(
            dimension_semantics=("arbitrary",), vmem_limit_bytes=VMEM_LIMIT),
        name="gdn",
    )(pq, pk, pv, pgb, q, k, v, gb)


def _mix_out_kernel(x_ref, ya_ref, o_ref, z_ref, gnw_ref, woa_ref, wob_ref, fnw_ref, wrh_ref,
                    wrl_ref, br_ref, h_ref, xn_ref, route_ref):
    yb = []
    for h in range(HEADS):
        sl = slice(h * HEAD_DIM, (h + 1) * HEAD_DIM)
        oh = o_ref[:, sl]
        zh = z_ref[:, sl]
        on = oh * lax.rsqrt(jnp.mean(oh * oh, axis=-1, keepdims=True) + EPS) * gnw_ref[...]
        yb.append((on * (zh * _sigmoid(zh))).astype(BF16))
    yb = jnp.concatenate(yb, axis=1)
    h1 = x_ref[...] + (_dot(ya_ref[...], woa_ref[...]) + _dot(yb, wob_ref[...]))
    h_ref[...] = h1
    xn = h1 * lax.rsqrt(jnp.mean(h1 * h1, axis=-1, keepdims=True) + EPS) * fnw_ref[...]
    xn_ref[...] = xn

    x_hi, x_lo = _split_bf16(xn)
    logits = (_dot(x_hi, wrh_ref[...]) + _dot(x_lo, wrh_ref[...]) + _dot(x_hi, wrl_ref[...])
              + br_ref[...])
    lane = lax.broadcasted_iota(jnp.int32, logits.shape, 1)
    neg = -jnp.inf
    big = jnp.int32(1 << 20)

    def argmax_first(vals):
        m = jnp.max(vals, axis=-1, keepdims=True)
        idx = jnp.min(jnp.where(vals == m, lane, big), axis=-1, keepdims=True)
        return m, idx

    grp = jnp.where(lane < N_GROUPS, logits, neg)
    g_max, g_sel = argmax_first(grp)
    p_grp = 1.0 / jnp.sum(jnp.exp(grp - g_max), axis=-1, keepdims=True)
    lo_lane = N_GROUPS + g_sel * EXPERTS_PER_GROUP
    ex = jnp.where((lane >= lo_lane) & (lane < lo_lane + EXPERTS_PER_GROUP), logits, neg)
    m1, i1 = argmax_first(ex)
    m2, i2 = argmax_first(jnp.where(lane == i1, neg, ex))
    e2 = jnp.exp(m2 - m1)
    w1 = 1.0 / (1.0 + e2) * p_grp
    w2 = e2 / (1.0 + e2) * p_grp
    route = jnp.where(lane == 0, (i1 - N_GROUPS).astype(F32),
                      jnp.where(lane == 1, (i2 - N_GROUPS).astype(F32),
                                jnp.where(lane == 2, w1, jnp.where(lane == 3, w2, 0.0))))
    route_ref[...] = route


def _mix_out_call(x2, ya, o, z, w, rows):
    tokens = x2.shape[0]
    tok = lambda width: pl.BlockSpec((rows, width), lambda i: (i, 0))
    full = lambda a: pl.BlockSpec(a.shape, lambda i: (0,) * a.ndim)
    consts = (w['gdn_norm_w'], w['w_out_a'], w['w_out_b'], w['ffn_norm_w'], w['w_route_hi'],
              w['w_route_lo'], w['b_route'])
    return pl.pallas_call(
        _mix_out_kernel,
        grid=(tokens // rows,),
        in_specs=[tok(D_MODEL), tok(CONV_CH), tok(GDN_W), tok(GDN_W)] + [full(a) for a in consts],
        out_specs=(tok(D_MODEL), tok(D_MODEL), tok(LANES)),
        out_shape=(jax.ShapeDtypeStruct((tokens, D_MODEL), F32),
                   jax.ShapeDtypeStruct((tokens, D_MODEL), F32),
                   jax.ShapeDtypeStruct((tokens, LANES), F32)),
        compiler_params=pltpu.CompilerParams(
            dimension_semantics=("arbitrary",), vmem_limit_bytes=VMEM_LIMIT),
        name="mix_out",
    )(x2, ya, o, z, *consts)


def _row_gather(src_hbm, idx_ref, idx_base, dst, sem, n_rows):
    def body(r, carry):
        pltpu.make_async_copy(src_hbm.at[pl.ds(idx_ref[idx_base + r], 1)], dst.at[pl.ds(r, 1)],
                              sem).start()
        return carry
    lax.fori_loop(0, n_rows, body, 0)


def _row_gather_wait(src_hbm, dst, sem, n_rows):
    pltpu.make_async_copy(src_hbm.at[pl.ds(0, n_rows)], dst, sem).wait()


def _experts_kernel(bexp_ref, nused_ref, rtok_ref, xn_hbm, wg_ref, wu_ref, wd_ref, y_ref,
                    xbuf, sems, wg_s, wu_s, wd_s):
    i = pl.program_id(0)
    rows = y_ref.shape[0]
    n_used = nused_ref[0]
    slot = i % 2

    def gather(blk, s):
        _row_gather(xn_hbm, rtok_ref, blk * rows, xbuf.at[s], sems.at[s], rows)

    @pl.when((i == 0) & (n_used > 0))
    def _():
        gather(0, 0)

    @pl.when(i + 1 < n_used)
    def _():
        gather(i + 1, 1 - slot)

    first_of_expert = (i == 0) | (bexp_ref[i] != bexp_ref[jnp.maximum(i - 1, 0)])

    @pl.when((i < n_used) & first_of_expert)
    def _():
        wg_s[...] = wg_ref[...].astype(BF16)
        wu_s[...] = wu_ref[...].astype(BF16)
        wd_s[...] = wd_ref[...].astype(BF16)

    @pl.when(i < n_used)
    def _():
        _row_gather_wait(xn_hbm, xbuf.at[slot], sems.at[slot], rows)
        xb = xbuf[slot].astype(BF16)
        gate = _dot(xb, wg_s[...])
        hid = (gate * _sigmoid(gate)) * _dot(xb, wu_s[...])
        y_ref[...] = _dot(hid.astype(BF16), wd_s[...])

    @pl.when(i >= n_used)
    def _():
        y_ref[...] = jnp.zeros(y_ref.shape, F32)


def _experts_call(block_expert, n_used, row_tok, xn, w_gate, w_up, w_down, rows):
    n_blocks = block_expert.shape[0]
    wspec = lambda shape: pl.BlockSpec((None,) + shape, lambda i, be, nu, rt: (be[i], 0, 0))
    grid_spec = pltpu.PrefetchScalarGridSpec(
        num_scalar_prefetch=3,
        grid=(n_blocks,),
        in_specs=[pl.BlockSpec(memory_space=pl.ANY),
                  wspec((D_MODEL, D_EXPERT)), wspec((D_MODEL, D_EXPERT)),
                  wspec((D_EXPERT, D_MODEL))],
        out_specs=pl.BlockSpec((rows, D_MODEL), lambda i, be, nu, rt: (i, 0)),
        scratch_shapes=[pltpu.VMEM((2, rows, D_MODEL), F32),
                        pltpu.SemaphoreType.DMA((2,)),
                        pltpu.VMEM((D_MODEL, D_EXPERT), BF16),
                        pltpu.VMEM((D_MODEL, D_EXPERT), BF16),
                        pltpu.VMEM((D_EXPERT, D_MODEL), BF16)],
    )
    return pl.pallas_call(
        _experts_kernel,
        grid_spec=grid_spec,
        out_shape=jax.ShapeDtypeStruct((n_blocks * rows, D_MODEL), F32),
        compiler_params=pltpu.CompilerParams(
            dimension_semantics=("arbitrary",), vmem_limit_bytes=VMEM_LIMIT),
        name="experts",
    )(block_expert, n_used, row_tok, xn, w_gate, w_up, w_down)


def _combine_kernel(pos_ref, y_hbm, h_ref, route_ref, fw_ref, out_ref, ybuf, sems):
    i = pl.program_id(0)
    n = pl.num_programs(0)
    rows = h_ref.shape[0]
    slot = i % 2

    def gather(blk, s):
        _row_gather(y_hbm, pos_ref, blk * (TOP_K * rows), ybuf.at[s], sems.at[s], TOP_K * rows)

    @pl.when(i == 0)
    def _():
        gather(0, 0)

    @pl.when(i + 1 < n)
    def _():
        gather(i + 1, 1 - slot)

    _row_gather_wait(y_hbm, ybuf.at[slot], sems.at[slot], TOP_K * rows)
    route = route_ref[...]
    moe = route[:, 2:3] * ybuf[slot, 0:rows, :] + route[:, 3:4] * ybuf[slot, rows:, :]
    h2 = h_ref[...] + moe
    out_ref[...] = h2 * lax.rsqrt(jnp.mean(h2 * h2, axis=-1, keepdims=True) + EPS) * fw_ref[...]


def _combine_call(pos, y_rows, h1, route, final_w, rows):
    tokens = h1.shape[0]
    grid_spec = pltpu.PrefetchScalarGridSpec(
        num_scalar_prefetch=1,
        grid=(tokens // rows,),
        in_specs=[pl.BlockSpec(memory_space=pl.ANY),
                  pl.BlockSpec((rows, D_MODEL), lambda i, p: (i, 0)),
                  pl.BlockSpec((rows, LANES), lambda i, p: (i, 0)),
                  pl.BlockSpec((1, D_MODEL), lambda i, p: (0, 0))],
        out_specs=pl.BlockSpec((rows, D_MODEL), lambda i, p: (i, 0)),
        scratch_shapes=[pltpu.VMEM((2, TOP_K * rows, D_MODEL), F32),
                        pltpu.SemaphoreType.DMA((2,))],
    )
    return pl.pallas_call(
        _combine_kernel,
        grid_spec=grid_spec,
        out_shape=jax.ShapeDtypeStruct((tokens, D_MODEL), F32),
        compiler_params=pltpu.CompilerParams(
            dimension_semantics=("arbitrary",), vmem_limit_bytes=VMEM_LIMIT),
        name="combine",
    )(pos, y_rows, h1, route, final_w)


def _dispatch_tables(route, tokens, rows):
    n_assign = tokens * TOP_K
    n_blocks = (n_assign + N_EXPERTS * (rows - 1)) // rows
    e1 = route[:, 0].astype(jnp.int32)
    e2 = route[:, 1].astype(jnp.int32)
    flat_e = jnp.stack([e1, e2], axis=0).reshape(-1)
    onehot = (flat_e[:, None] == jnp.arange(N_EXPERTS, dtype=jnp.int32)[None, :]).astype(jnp.int32)
    ranks = jnp.cumsum(onehot, axis=0) - onehot
    rank = jnp.sum(ranks * onehot, axis=1)
    counts = jnp.sum(onehot, axis=0)
    padded = (counts + rows - 1) // rows * rows
    pad_end = jnp.cumsum(padded)
    pad_start = pad_end - padded
    dest = pad_start[flat_e] + rank
    flat_tok = jnp.tile(jnp.arange(tokens, dtype=jnp.int32), TOP_K)
    row_tok = jnp.zeros((n_blocks * rows,), jnp.int32).at[dest].set(flat_tok)
    block_start = jnp.arange(n_blocks, dtype=jnp.int32) * rows
    block_expert = jnp.minimum(jnp.searchsorted(pad_end, block_start, side='right'),
                               N_EXPERTS - 1).astype(jnp.int32)
    n_used = (pad_end[-1] // rows).astype(jnp.int32).reshape(1)
    return dest.astype(jnp.int32), row_tok, block_expert, n_used, n_blocks


def _combine_positions(dest, tokens, rows):
    d = dest.reshape(TOP_K, tokens // rows, rows)
    return jnp.transpose(d, (1, 0, 2)).reshape(-1)


def _prepare_weights(mix_norm_w, w_in, conv_mix_w, conv_mix_norm_w, qkv_conv_w, a_log, dt_bias,
                     gdn_norm_w, w_out, ffn_norm_w, w_group, b_group, w_router, b_router):
    pad_lanes = lambda v: jnp.pad(v.reshape(1, -1), ((0, 0), (0, LANES - v.size)))
    c3 = 3 * CONV_CH
    w_ab = jnp.pad(w_in[:, c3 + 4 * GDN_W:], ((0, 0), (0, LANES - 2 * HEADS)))
    grp = jnp.arange(CONV_CH) // CONV_GROUP_W
    gmat = jnp.where(grp[:, None] == grp[None, :], 1.0 / CONV_GROUP_W, 0.0).astype(BF16)
    w_route = jnp.concatenate([w_group, w_router.reshape(D_MODEL, N_EXPERTS)], axis=1)
    w_route = jnp.pad(w_route, ((0, 0), (0, LANES - w_route.shape[1])))
    w_route_hi = w_route.astype(BF16)
    return dict(
        mix_norm_w=mix_norm_w.reshape(1, -1),
        w_a=w_in[:, :c3].astype(BF16),
        w_qkv=w_in[:, c3:c3 + 3 * GDN_W].astype(BF16),
        w_z=w_in[:, c3 + 3 * GDN_W:c3 + 4 * GDN_W].astype(BF16),
        w_ab=w_ab.astype(BF16),
        conv_mix_w=conv_mix_w,
        conv_mix_norm_w=conv_mix_norm_w.reshape(1, -1),
        gmat=gmat,
        qkv_conv_w=qkv_conv_w,
        a_log=pad_lanes(a_log),
        dt_bias=pad_lanes(dt_bias),
        gdn_norm_w=gdn_norm_w.reshape(1, -1),
        w_out_a=w_out[:CONV_CH].astype(BF16),
        w_out_b=w_out[CONV_CH:].astype(BF16),
        ffn_norm_w=ffn_norm_w.reshape(1, -1),
        w_route_hi=w_route_hi,
        w_route_lo=(w_route - w_route_hi.astype(F32)).astype(BF16),
        b_route=pad_lanes(jnp.concatenate([b_group, b_router.reshape(-1)])),
    )


def _tile(n, preferred):
    return preferred if n % preferred == 0 else n


def kernel(x, meta_tokens, mix_norm_w, w_in, conv_mix_w, conv_mix_norm_w, qkv_conv_w, a_log,
           dt_bias, gdn_norm_w, w_out, ffn_norm_w, w_group, b_group, w_router, b_router, w_gate,
           w_up, w_down, final_norm_w):
    assert mix_norm_w.shape[0] == 1, "single-layer kernel"
    batch, seq, _ = x.shape
    assert seq % CHUNK == 0
    w = _prepare_weights(mix_norm_w[0], w_in[0], conv_mix_w[0], conv_mix_norm_w[0], qkv_conv_w[0],
                         a_log[0], dt_bias[0], gdn_norm_w[0], w_out[0], ffn_norm_w[0], w_group[0],
                         b_group[0], w_router[0], b_router[0])

    prefix = jnp.concatenate([jnp.zeros((CHUNK - N_META, D_MODEL), x.dtype),
                              meta_tokens.astype(x.dtype)], axis=0)[None]
    zero_cu = jnp.zeros((HIST, CONV_CH), F32)
    zero_qkv = jnp.zeros((HIST, 3 * GDN_W), F32)
    _, pq, pk, pv, _, pgb, tail_cu, tail_qkv = _proj_call(prefix, zero_cu, zero_qkv, w, CHUNK)

    ya, q, k, v, z, gb, _, _ = _proj_call(x, tail_cu[0], tail_qkv[0], w, _tile(seq, PROJ_ROWS))
    o = _gdn_call(pq, pk, pv, pgb, q, k, v, gb, _tile(seq // CHUNK, GDN_CHUNKS))

    tokens = batch * seq
    flat = lambda a: a.reshape(tokens, a.shape[-1])
    h1, xn2, route = _mix_out_call(flat(x), flat(ya), flat(o), flat(z), w, _tile(tokens, MIX_ROWS))

    c_rows = _tile(tokens, COMBINE_ROWS)
    dest, row_tok, block_expert, n_used, _ = _dispatch_tables(route, tokens, EXPERT_ROWS)
    y_rows = _experts_call(block_expert, n_used, row_tok, xn2, w_gate[0], w_up[0], w_down[0],
                           EXPERT_ROWS)
    out = _combine_call(_combine_positions(dest, tokens, c_rows), y_rows, h1, route,
                        final_norm_w.reshape(1, -1), c_rows)
    return out.reshape(batch, seq, D_MODEL)
```

```python
import functools

import jax
import jax.numpy as jnp
from jax import lax
from jax.experimental import pallas as pl
from jax.experimental.pallas import tpu as pltpu

F32 = jnp.float32
BF16 = jnp.bfloat16
EPS = 1e-6

D_MODEL = 1024
N_META = 16
CONV_CH = 512
CONV_GROUP_W = 64
HEADS = 4
HEAD_DIM = 128
GDN_W = HEADS * HEAD_DIM
CHUNK = 64
N_GROUPS = 4
EXPERTS_PER_GROUP = 8
N_EXPERTS = N_GROUPS * EXPERTS_PER_GROUP
TOP_K = 2
D_EXPERT = 512
LANES = 128
HIST = 8

PROJ_ROWS = 512
GDN_CHUNKS = 8
GDN_GROUP = 2
MIX_ROWS = 512
EXPERT_ROWS = 256
COMBINE_ROWS = 256
VMEM_LIMIT = 56 * 1024 * 1024


def _dot(a, b):
    return jnp.dot(a, b, preferred_element_type=F32)


def _dot_nt(a, b):
    return lax.dot_general(a, b, (((1,), (1,)), ((), ())), preferred_element_type=F32)


def _dot_tn(a, b):
    return lax.dot_general(a, b, (((0,), (0,)), ((), ())), preferred_element_type=F32)


def _split_bf16(x):
    hi = x.astype(BF16)
    lo = (x - hi.astype(F32)).astype(BF16)
    return hi, lo


def _sigmoid(x):
    return 1.0 / (1.0 + jnp.exp(-x))


def _proj_kernel(x_ref, hcu_ref, hqkv_ref, nw_ref, wa_ref, wq_ref, wz_ref, wab_ref, cmw_ref,
                 cmn_ref, gmat_ref, qcw_ref, alog_ref, dtb_ref,
                 ya_ref, q_ref, k_ref, v_ref, z_ref, gb_ref, tcu_ref, tqkv_ref,
                 cu_s, qkv_s):
    rows = x_ref.shape[0]

    @pl.when(pl.program_id(1) == 0)
    def _():
        cu_s[0:HIST, :] = hcu_ref[...]
        qkv_s[0:HIST, :] = hqkv_ref[...]

    x = x_ref[...]
    ms = jnp.mean(x * x, axis=-1, keepdims=True)
    xn = (x * lax.rsqrt(ms + EPS) * nw_ref[...]).astype(BF16)

    pa = _dot(xn, wa_ref[...])
    gate_b = pa[:, :CONV_CH]
    cu = pa[:, CONV_CH:2 * CONV_CH] * pa[:, 2 * CONV_CH:]
    cu_s[HIST:HIST + rows, :] = cu
    conv = (cu_s[pl.ds(HIST - 2, rows), :] * cmw_ref[0:1, :]
            + cu_s[pl.ds(HIST - 1, rows), :] * cmw_ref[1:2, :]
            + cu * cmw_ref[2:3, :])
    ya = gate_b * conv
    sq_hi, sq_lo = _split_bf16(ya * ya)
    msg = _dot(sq_hi, gmat_ref[...]) + _dot(sq_lo, gmat_ref[...])
    ya_ref[...] = (ya * lax.rsqrt(msg + EPS) * cmn_ref[...]).astype(BF16)
    tail_cu = cu_s[rows:rows + HIST, :]
    cu_s[0:HIST, :] = tail_cu
    tcu_ref[...] = tail_cu

    pq = _dot(xn, wq_ref[...])
    qkv_s[HIST:HIST + rows, :] = pq
    c = (qkv_s[pl.ds(HIST - 3, rows), :] * qcw_ref[0:1, :]
         + qkv_s[pl.ds(HIST - 2, rows), :] * qcw_ref[1:2, :]
         + qkv_s[pl.ds(HIST - 1, rows), :] * qcw_ref[2:3, :]
         + pq * qcw_ref[3:4, :])
    c = c * _sigmoid(c)
    tail_qkv = qkv_s[rows:rows + HIST, :]
    qkv_s[0:HIST, :] = tail_qkv
    tqkv_ref[...] = tail_qkv
    for h in range(HEADS):
        sl = slice(h * HEAD_DIM, (h + 1) * HEAD_DIM)
        qh = c[:, h * HEAD_DIM:(h + 1) * HEAD_DIM]
        kh = c[:, GDN_W + h * HEAD_DIM:GDN_W + (h + 1) * HEAD_DIM]
        qn = qh * lax.rsqrt(jnp.sum(qh * qh, axis=-1, keepdims=True) + EPS)
        q_ref[:, sl] = qn * (HEAD_DIM ** -0.5)
        k_ref[:, sl] = kh * lax.rsqrt(jnp.sum(kh * kh, axis=-1, keepdims=True) + EPS)
    v_ref[...] = c[:, 2 * GDN_W:]
    z_ref[...] = _dot(xn, wz_ref[...])

    ab = _dot(xn, wab_ref[...])
    sp_in = ab + dtb_ref[...]
    softplus = jnp.maximum(sp_in, 0.0) + jnp.log1p(jnp.exp(-jnp.abs(sp_in)))
    g = -jnp.exp(alog_ref[...]) * softplus
    lane = lax.broadcasted_iota(jnp.int32, ab.shape, 1)
    gb_ref[...] = jnp.where(lane < HEADS, g, jnp.where(lane < 2 * HEADS, _sigmoid(ab), 0.0))


def _proj_call(x3, hist_cu, hist_qkv, w, rows):
    nb, seq, _ = x3.shape
    nt = seq // rows
    tok = lambda width: pl.BlockSpec((None, rows, width), lambda b, t: (b, t, 0))
    full = lambda a: pl.BlockSpec(a.shape, lambda b, t: (0,) * a.ndim)
    tail = lambda width: pl.BlockSpec((None, HIST, width), lambda b, t: (b, 0, 0))
    consts = (hist_cu, hist_qkv, w['mix_norm_w'], w['w_a'], w['w_qkv'], w['w_z'], w['w_ab'],
              w['conv_mix_w'], w['conv_mix_norm_w'], w['gmat'], w['qkv_conv_w'], w['a_log'],
              w['dt_bias'])
    out_shape = (
        jax.ShapeDtypeStruct((nb, seq, CONV_CH), BF16),
        jax.ShapeDtypeStruct((nb, seq, GDN_W), F32),
        jax.ShapeDtypeStruct((nb, seq, GDN_W), F32),
        jax.ShapeDtypeStruct((nb, seq, GDN_W), F32),
        jax.ShapeDtypeStruct((nb, seq, GDN_W), F32),
        jax.ShapeDtypeStruct((nb, seq, LANES), F32),
        jax.ShapeDtypeStruct((nb, HIST, CONV_CH), F32),
        jax.ShapeDtypeStruct((nb, HIST, 3 * GDN_W), F32),
    )
    return pl.pallas_call(
        _proj_kernel,
        grid=(nb, nt),
        in_specs=[tok(D_MODEL)] + [full(a) for a in consts],
        out_specs=(tok(CONV_CH), tok(GDN_W), tok(GDN_W), tok(GDN_W), tok(GDN_W), tok(LANES),
                   tail(CONV_CH), tail(3 * GDN_W)),
        out_shape=out_shape,
        scratch_shapes=[pltpu.VMEM((rows + HIST, CONV_CH), F32),
                        pltpu.VMEM((rows + HIST, 3 * GDN_W), F32)],
        compiler_params=pltpu.CompilerParams(
            dimension_semantics=("arbitrary", "arbitrary"), vmem_limit_bytes=VMEM_LIMIT),
        name="proj",
    )(x3, *consts)


def _chunk_masks():
    row = lax.broadcasted_iota(jnp.int32, (CHUNK, CHUNK), 0)
    col = lax.broadcasted_iota(jnp.int32, (CHUNK, CHUNK), 1)
    incl = row >= col
    strict = row > col
    levels = []
    n = 1
    while n < CHUNK:
        levels.append((row // (2 * n) == col // (2 * n)) & ((row // n) % 2 == 1) & ((col // n) % 2 == 0))
        n *= 2
    return incl, strict, levels


def _chunk_cumsum(gb_blk, incl):
    tri = incl.astype(BF16)
    hi, lo = _split_bf16(gb_blk)
    return _dot(tri, hi) + _dot(tri, lo)


def _chunk_transforms(chains, masks, state_only):
    incl, strict, levels = masks
    eye = (lax.broadcasted_iota(jnp.int32, (CHUNK, CHUNK), 0)
           == lax.broadcasted_iota(jnp.int32, (CHUNK, CHUNK), 1)).astype(F32)
    decay = [jnp.exp(jnp.where(incl, gc_col - gc_row, -jnp.inf))
             for (_, _, _, _, gc_col, gc_row, _) in chains]
    kb = [kh * beta for (_, kh, _, beta, _, _, _) in chains]
    k_bf = [kh.astype(BF16) for (_, kh, _, _, _, _, _) in chains]
    a_mat = [jnp.where(strict, _dot_nt(kb_i.astype(BF16), k_i) * d_i, 0.0)
             for kb_i, k_i, d_i in zip(kb, k_bf, decay)]
    t_inv = [eye - jnp.where(levels[0], a_i, 0.0) for a_i in a_mat]
    for lvl in levels[1:]:
        t_bf = [t_i.astype(BF16) for t_i in t_inv]
        m1 = [_dot(jnp.where(lvl, a_i, 0.0).astype(BF16), t_i) for a_i, t_i in zip(a_mat, t_bf)]
        t_inv = [t_i - _dot(tb_i, m_i.astype(BF16)) for t_i, tb_i, m_i in zip(t_inv, t_bf, m1)]
    rhs = [jnp.concatenate([vh * beta, kb_i * jnp.exp(gc_col)], axis=1)
           for (_, _, vh, beta, gc_col, _, _), kb_i in zip(chains, kb)]
    uw = [_dot(t_i.astype(BF16), r_i.astype(BF16)).astype(BF16)
          for t_i, r_i in zip(t_inv, rhs)]
    kd = [kh * jnp.exp(g_last - gc_col) for (_, kh, _, _, gc_col, _, g_last) in chains]
    pn = [_dot_tn(kd_i.astype(BF16), uw_i) for kd_i, uw_i in zip(kd, uw)]
    if state_only:
        return [pn_i[:, :HEAD_DIM] for pn_i in pn]
    intra = [jnp.where(incl, _dot_nt(qh.astype(BF16), k_i) * d_i, 0.0)
             for (qh, _, _, _, _, _, _), k_i, d_i in zip(chains, k_bf, decay)]
    iuw = [_dot(in_i.astype(BF16), uw_i) for in_i, uw_i in zip(intra, uw)]
    out = []
    for (qh, _, _, _, gc_col, _, g_last), pn_i, iuw_i in zip(chains, pn, iuw):
        q_part = qh * jnp.exp(gc_col) - iuw_i[:, HEAD_DIM:]
        out.append((q_part, pn_i[:, HEAD_DIM:], iuw_i[:, :HEAD_DIM], pn_i[:, :HEAD_DIM],
                    jnp.exp(g_last)))
    return out


def _gdn_kernel(pq_ref, pk_ref, pv_ref, pgb_ref, q_ref, k_ref, v_ref, gb_ref, o_ref,
                s_s, qp_s, op_s, n_s, a_s):
    nb = q_ref.shape[0]
    n_chunks = q_ref.shape[1] // CHUNK
    group = GDN_GROUP if n_chunks % GDN_GROUP == 0 else 1
    masks = _chunk_masks()

    def chains_of(gb_blk, q_blk, k_blk, v_blk):
        gc = _chunk_cumsum(gb_blk, masks[0])
        gc_t = gc.T
        res = []
        for h in range(HEADS):
            sl = slice(h * HEAD_DIM, (h + 1) * HEAD_DIM)
            res.append((q_blk(sl), k_blk(sl), v_blk(sl), gb_blk[:, HEADS + h:HEADS + h + 1],
                        gc[:, h:h + 1], gc_t[h:h + 1, :CHUNK], gc[CHUNK - 1:CHUNK, h:h + 1]))
        return res

    @pl.when(pl.program_id(0) == 0)
    def _():
        chains = chains_of(pgb_ref[...], lambda sl: pq_ref[:, sl], lambda sl: pk_ref[:, sl],
                           lambda sl: pv_ref[:, sl])
        for h, n_mat in enumerate(_chunk_transforms(chains, masks, True)):
            for b in range(nb):
                s_s[b * HEADS + h] = n_mat

    def transform_group(gi, carry):
        chains, where = [], []
        for cc in range(group):
            c = gi * group + cc
            rows = pl.ds(pl.multiple_of(c * CHUNK, CHUNK), CHUNK)
            for b in range(nb):
                chains += chains_of(gb_ref[b, rows, :], lambda sl: q_ref[b, rows, sl],
                                    lambda sl: k_ref[b, rows, sl], lambda sl: v_ref[b, rows, sl])
                where += [(c, b * HEADS + h) for h in range(HEADS)]
        for (c, ch), (q_part, p_mat, o_part, n_mat, a) in zip(
                where, _chunk_transforms(chains, masks, False)):
            qp_s[c, ch, 0:CHUNK, :] = q_part.astype(BF16)
            qp_s[c, ch, CHUNK:, :] = p_mat.astype(BF16)
            op_s[c, ch] = o_part
            n_s[c, ch] = n_mat
            a_s[c, ch] = jnp.broadcast_to(a, (8, HEAD_DIM))
        return carry

    lax.fori_loop(0, n_chunks // group, transform_group, 0)

    def scan_chunk(c, carry):
        r0 = pl.multiple_of(c * CHUNK, CHUNK)
        for b in range(nb):
            for h in range(HEADS):
                ch = b * HEADS + h
                s = s_s[ch]
                r = _dot(qp_s[c, ch], s.astype(BF16))
                o_ref[b, pl.ds(r0, CHUNK), h * HEAD_DIM:(h + 1) * HEAD_DIM] = r[:CHUNK] + op_s[c, ch]
                s_s[ch] = a_s[c, ch][0:1, :] * s - r[CHUNK:] + n_s[c, ch]
        return carry

    lax.fori_loop(0, n_chunks, scan_chunk, 0)


def _gdn_call(pq, pk, pv, pgb, q, k, v, gb, chunks_per_step):
    nb, seq, _ = q.shape
    rows = chunks_per_step * CHUNK
    steps = seq // rows
    tok = lambda width: pl.BlockSpec((nb, rows, width), lambda i: (0, i, 0))
    pre = lambda width: pl.BlockSpec((None, CHUNK, width), lambda i: (0, 0, 0))
    nch = nb * HEADS
    return pl.pallas_call(
        _gdn_kernel,
        grid=(steps,),
        in_specs=[pre(GDN_W), pre(GDN_W), pre(GDN_W), pre(LANES),
                  tok(GDN_W), tok(GDN_W), tok(GDN_W), tok(LANES)],
        out_specs=tok(GDN_W),
        out_shape=jax.ShapeDtypeStruct((nb, seq, GDN_W), F32),
        scratch_shapes=[
            pltpu.VMEM((nch, HEAD_DIM, HEAD_DIM), F32),
            pltpu.VMEM((chunks_per_step, nch, CHUNK + HEAD_DIM, HEAD_DIM), BF16),
            pltpu.VMEM((chunks_per_step, nch, CHUNK, HEAD_DIM), F32),
            pltpu.VMEM((chunks_per_step, nch, HEAD_DIM, HEAD_DIM), F32),
            pltpu.VMEM((chunks_per_step, nch, 8, HEAD_DIM), F32),
        ],
        compiler_params=pltpu.CompilerParams(
            dimension_semantics=("arbitrary",), vmem_limit_bytes=VMEM_LIMIT),
        name="gdn",
    )(pq, pk, pv, pgb, q, k, v, gb)


def _mix_out_kernel(x_ref, ya_ref, o_ref, z_ref, gnw_ref, woa_ref, wob_ref, fnw_ref, wrh_ref,
                    wrl_ref, br_ref, h_ref, xn_ref, route_ref):
    yb = []
    for h in range(HEADS):
        sl = slice(h * HEAD_DIM, (h + 1) * HEAD_DIM)
        oh = o_ref[:, sl]
        zh = z_ref[:, sl]
        on = oh * lax.rsqrt(jnp.mean(oh * oh, axis=-1, keepdims=True) + EPS) * gnw_ref[...]
        yb.append((on * (zh * _sigmoid(zh))).astype(BF16))
    yb = jnp.concatenate(yb, axis=1)
    h1 = x_ref[...] + (_dot(ya_ref[...], woa_ref[...]) + _dot(yb, wob_ref[...]))
    h_ref[...] = h1
    xn = h1 * lax.rsqrt(jnp.mean(h1 * h1, axis=-1, keepdims=True) + EPS) * fnw_ref[...]
    xn_ref[...] = xn

    x_hi, x_lo = _split_bf16(xn)
    logits = (_dot(x_hi, wrh_ref[...]) + _dot(x_lo, wrh_ref[...]) + _dot(x_hi, wrl_ref[...])
              + br_ref[...])
    lane = lax.broadcasted_iota(jnp.int32, logits.shape, 1)
    neg = -jnp.inf
    big = jnp.int32(1 << 20)

    def argmax_first(vals):
        m = jnp.max(vals, axis=-1, keepdims=True)
        idx = jnp.min(jnp.where(vals == m, lane, big), axis=-1, keepdims=True)
        return m, idx

    grp = jnp.where(lane < N_GROUPS, logits, neg)
    g_max, g_sel = argmax_first(grp)
    p_grp = 1.0 / jnp.sum(jnp.exp(grp - g_max), axis=-1, keepdims=True)
    lo_lane = N_GROUPS + g_sel * EXPERTS_PER_GROUP
    ex = jnp.where((lane >= lo_lane) & (lane < lo_lane + EXPERTS_PER_GROUP), logits, neg)
    m1, i1 = argmax_first(ex)
    m2, i2 = argmax_first(jnp.where(lane == i1, neg, ex))
    e2 = jnp.exp(m2 - m1)
    w1 = 1.0 / (1.0 + e2) * p_grp
    w2 = e2 / (1.0 + e2) * p_grp
    route = jnp.where(lane == 0, (i1 - N_GROUPS).astype(F32),
                      jnp.where(lane == 1, (i2 - N_GROUPS).astype(F32),
                                jnp.where(lane == 2, w1, jnp.where(lane == 3, w2, 0.0))))
    route_ref[...] = route


def _mix_out_call(x2, ya, o, z, w, rows):
    tokens = x2.shape[0]
    tok = lambda width: pl.BlockSpec((rows, width), lambda i: (i, 0))
    full = lambda a: pl.BlockSpec(a.shape, lambda i: (0,) * a.ndim)
    consts = (w['gdn_norm_w'], w['w_out_a'], w['w_out_b'], w['ffn_norm_w'], w['w_route_hi'],
              w['w_route_lo'], w['b_route'])
    return pl.pallas_call(
        _mix_out_kernel,
        grid=(tokens // rows,),
        in_specs=[tok(D_MODEL), tok(CONV_CH), tok(GDN_W), tok(GDN_W)] + [full(a) for a in consts],
        out_specs=(tok(D_MODEL), tok(D_MODEL), tok(LANES)),
        out_shape=(jax.ShapeDtypeStruct((tokens, D_MODEL), F32),
                   jax.ShapeDtypeStruct((tokens, D_MODEL), F32),
                   jax.ShapeDtypeStruct((tokens, LANES), F32)),
        compiler_params=pltpu.CompilerParams(
            dimension_semantics=("arbitrary",), vmem_limit_bytes=VMEM_LIMIT),
        name="mix_out",
    )(x2, ya, o, z, *consts)


def _row_gather(src_hbm, idx_ref, idx_base, dst, sem, n_rows):
    def body(r, carry):
        pltpu.make_async_copy(src_hbm.at[pl.ds(idx_ref[idx_base + r], 1)], dst.at[pl.ds(r, 1)],
                              sem).start()
        return carry
    lax.fori_loop(0, n_rows, body, 0)


def _row_gather_wait(src_hbm, dst, sem, n_rows):
    pltpu.make_async_copy(src_hbm.at[pl.ds(0, n_rows)], dst, sem).wait()


def _experts_kernel(bexp_ref, nused_ref, rtok_ref, xn_hbm, wg_ref, wu_ref, wd_ref, y_ref,
                    xbuf, sems, wg_s, wu_s, wd_s):
    i = pl.program_id(0)
    rows = y_ref.shape[0]
    n_used = nused_ref[0]
    slot = i % 2

    def gather(blk, s):
        _row_gather(xn_hbm, rtok_ref, blk * rows, xbuf.at[s], sems.at[s], rows)

    @pl.when((i == 0) & (n_used > 0))
    def _():
        gather(0, 0)

    @pl.when(i + 1 < n_used)
    def _():
        gather(i + 1, 1 - slot)

    first_of_expert = (i == 0) | (bexp_ref[i] != bexp_ref[jnp.maximum(i - 1, 0)])

    @pl.when((i < n_used) & first_of_expert)
    def _():
        wg_s[...] = wg_ref[...].astype(BF16)
        wu_s[...] = wu_ref[...].astype(BF16)
        wd_s[...] = wd_ref[...].astype(BF16)

    @pl.when(i < n_used)
    def _():
        _row_gather_wait(xn_hbm, xbuf.at[slot], sems.at[slot], rows)
        xb = xbuf[slot].astype(BF16)
        gate = _dot(xb, wg_s[...])
        hid = (gate * _sigmoid(gate)) * _dot(xb, wu_s[...])
        y_ref[...] = _dot(hid.astype(BF16), wd_s[...])

    @pl.when(i >= n_used)
    def _():
        y_ref[...] = jnp.zeros(y_ref.shape, F32)


def _experts_call(block_expert, n_used, row_tok, xn, w_gate, w_up, w_down, rows):
    n_blocks = block_expert.shape[0]
    wspec = lambda shape: pl.BlockSpec((None,) + shape, lambda i, be, nu, rt: (be[i], 0, 0))
    grid_spec = pltpu.PrefetchScalarGridSpec(
        num_scalar_prefetch=3,
        grid=(n_blocks,),
        in_specs=[pl.BlockSpec(memory_space=pl.ANY),
                  wspec((D_MODEL, D_EXPERT)), wspec((D_MODEL, D_EXPERT)),
                  wspec((D_EXPERT, D_MODEL))],
        out_specs=pl.BlockSpec((rows, D_MODEL), lambda i, be, nu, rt: (i, 0)),
        scratch_shapes=[pltpu.VMEM((2, rows, D_MODEL), F32),
                        pltpu.SemaphoreType.DMA((2,)),
                        pltpu.VMEM((D_MODEL, D_EXPERT), BF16),
                        pltpu.VMEM((D_MODEL, D_EXPERT), BF16),
                        pltpu.VMEM((D_EXPERT, D_MODEL), BF16)],
    )
    return pl.pallas_call(
        _experts_kernel,
        grid_spec=grid_spec,
        out_shape=jax.ShapeDtypeStruct((n_blocks * rows, D_MODEL), F32),
        compiler_params=pltpu.CompilerParams(
            dimension_semantics=("arbitrary",), vmem_limit_bytes=VMEM_LIMIT),
        name="experts",
    )(block_expert, n_used, row_tok, xn, w_gate, w_up, w_down)


def _combine_kernel(pos_ref, y_hbm, h_ref, route_ref, fw_ref, out_ref, ybuf, sems):
    i = pl.program_id(0)
    n = pl.num_programs(0)
    rows = h_ref.shape[0]
    slot = i % 2

    def gather(blk, s):
        _row_gather(y_hbm, pos_ref, blk * (TOP_K * rows), ybuf.at[s], sems.at[s], TOP_K * rows)

    @pl.when(i == 0)
    def _():
        gather(0, 0)

    @pl.when(i + 1 < n)
    def _():
        gather(i + 1, 1 - slot)

    _row_gather_wait(y_hbm, ybuf.at[slot], sems.at[slot], TOP_K * rows)
    route = route_ref[...]
    moe = route[:, 2:3] * ybuf[slot, 0:rows, :] + route[:, 3:4] * ybuf[slot, rows:, :]
    h2 = h_ref[...] + moe
    out_ref[...] = h2 * lax.rsqrt(jnp.mean(h2 * h2, axis=-1, keepdims=True) + EPS) * fw_ref[...]


def _combine_call(pos, y_rows, h1, route, final_w, rows):
    tokens = h1.shape[0]
    grid_spec = pltpu.PrefetchScalarGridSpec(
        num_scalar_prefetch=1,
        grid=(tokens // rows,),
        in_specs=[pl.BlockSpec(memory_space=pl.ANY),
                  pl.BlockSpec((rows, D_MODEL), lambda i, p: (i, 0)),
                  pl.BlockSpec((rows, LANES), lambda i, p: (i, 0)),
                  pl.BlockSpec((1, D_MODEL), lambda i, p: (0, 0))],
        out_specs=pl.BlockSpec((rows, D_MODEL), lambda i, p: (i, 0)),
        scratch_shapes=[pltpu.VMEM((2, TOP_K * rows, D_MODEL), F32),
                        pltpu.SemaphoreType.DMA((2,))],
    )
    return pl.pallas_call(
        _combine_kernel,
        grid_spec=grid_spec,
        out_shape=jax.ShapeDtypeStruct((tokens, D_MODEL), F32),
        compiler_params=pltpu.CompilerParams(
            dimension_semantics=("arbitrary",), vmem_limit_bytes=VMEM_LIMIT),
        name="combine",
    )(pos, y_rows, h1, route, final_w)


def _dispatch_tables(route, tokens, rows):
    n_assign = tokens * TOP_K
    n_blocks = (n_assign + N_EXPERTS * (rows - 1)) // rows
    e1 = route[:, 0].astype(jnp.int32)
    e2 = route[:, 1].astype(jnp.int32)
    flat_e = jnp.stack([e1, e2], axis=0).reshape(-1)
    onehot = (flat_e[:, None] == jnp.arange(N_EXPERTS, dtype=jnp.int32)[None, :]).astype(jnp.int32)
    ranks = jnp.cumsum(onehot, axis=0) - onehot
    rank = jnp.sum(ranks * onehot, axis=1)
    counts = jnp.sum(onehot, axis=0)
    padded = (counts + rows - 1) // rows * rows
    pad_end = jnp.cumsum(padded)
    pad_start = pad_end - padded
    dest = pad_start[flat_e] + rank
    flat_tok = jnp.tile(jnp.arange(tokens, dtype=jnp.int32), TOP_K)
    row_tok = jnp.zeros((n_blocks * rows,), jnp.int32).at[dest].set(flat_tok)
    block_start = jnp.arange(n_blocks, dtype=jnp.int32) * rows
    block_expert = jnp.minimum(jnp.searchsorted(pad_end, block_start, side='right'),
                               N_EXPERTS - 1).astype(jnp.int32)
    n_used = (pad_end[-1] // rows).astype(jnp.int32).reshape(1)
    return dest.astype(jnp.int32), row_tok, block_expert, n_used, n_blocks


def _combine_positions(dest, tokens, rows):
    d = dest.reshape(TOP_K, tokens // rows, rows)
    return jnp.transpose(d, (1, 0, 2)).reshape(-1)


def _prepare_weights(mix_norm_w, w_in, conv_mix_w, conv_mix_norm_w, qkv_conv_w, a_log, dt_bias,
                     gdn_norm_w, w_out, ffn_norm_w, w_group, b_group, w_router, b_router):
    pad_lanes = lambda v: jnp.pad(v.reshape(1, -1), ((0, 0), (0, LANES - v.size)))
    c3 = 3 * CONV_CH
    w_ab = jnp.pad(w_in[:, c3 + 4 * GDN_W:], ((0, 0), (0, LANES - 2 * HEADS)))
    grp = jnp.arange(CONV_CH) // CONV_GROUP_W
    gmat = jnp.where(grp[:, None] == grp[None, :], 1.0 / CONV_GROUP_W, 0.0).astype(BF16)
    w_route = jnp.concatenate([w_group, w_router.reshape(D_MODEL, N_EXPERTS)], axis=1)
    w_route = jnp.pad(w_route, ((0, 0), (0, LANES - w_route.shape[1])))
    w_route_hi = w_route.astype(BF16)
    return dict(
        mix_norm_w=mix_norm_w.reshape(1, -1),
        w_a=w_in[:, :c3].astype(BF16),
        w_qkv=w_in[:, c3:c3 + 3 * GDN_W].astype(BF16),
        w_z=w_in[:, c3 + 3 * GDN_W:c3 + 4 * GDN_W].astype(BF16),
        w_ab=w_ab.astype(BF16),
        conv_mix_w=conv_mix_w,
        conv_mix_norm_w=conv_mix_norm_w.reshape(1, -1),
        gmat=gmat,
        qkv_conv_w=qkv_conv_w,
        a_log=pad_lanes(a_log),
        dt_bias=pad_lanes(dt_bias),
        gdn_norm_w=gdn_norm_w.reshape(1, -1),
        w_out_a=w_out[:CONV_CH].astype(BF16),
        w_out_b=w_out[CONV_CH:].astype(BF16),
        ffn_norm_w=ffn_norm_w.reshape(1, -1),
        w_route_hi=w_route_hi,
        w_route_lo=(w_route - w_route_hi.astype(F32)).astype(BF16),
        b_route=pad_lanes(jnp.concatenate([b_group, b_router.reshape(-1)])),
    )


def _tile(n, preferred):
    return preferred if n % preferred == 0 else n


def kernel(x, meta_tokens, mix_norm_w, w_in, conv_mix_w, conv_mix_norm_w, qkv_conv_w, a_log,
           dt_bias, gdn_norm_w, w_out, ffn_norm_w, w_group, b_group, w_router, b_router, w_gate,
           w_up, w_down, final_norm_w):
    assert mix_norm_w.shape[0] == 1, "single-layer kernel"
    batch, seq, _ = x.shape
    assert seq % CHUNK == 0
    w = _prepare_weights(mix_norm_w[0], w_in[0], conv_mix_w[0], conv_mix_norm_w[0], qkv_conv_w[0],
                         a_log[0], dt_bias[0], gdn_norm_w[0], w_out[0], ffn_norm_w[0], w_group[0],
                         b_group[0], w_router[0], b_router[0])

    prefix = jnp.concatenate([jnp.zeros((CHUNK - N_META, D_MODEL), x.dtype),
                              meta_tokens.astype(x.dtype)], axis=0)[None]
    zero_cu = jnp.zeros((HIST, CONV_CH), F32)
    zero_qkv = jnp.zeros((HIST, 3 * GDN_W), F32)
    _, pq, pk, pv, _, pgb, tail_cu, tail_qkv = _proj_call(prefix, zero_cu, zero_qkv, w, CHUNK)

    ya, q, k, v, z, gb, _, _ = _proj_call(x, tail_cu[0], tail_qkv[0], w, _tile(seq, PROJ_ROWS))
    o = _gdn_call(pq, pk, pv, pgb, q, k, v, gb, _tile(seq // CHUNK, GDN_CHUNKS))

    tokens = batch * seq
    flat = lambda a: a.reshape(tokens, a.shape[-1])
    h1, xn2, route = _mix_out_call(flat(x), flat(ya), flat(o), flat(z), w, _tile(tokens, MIX_ROWS))

    c_rows = _tile(tokens, COMBINE_ROWS)
    dest, row_tok, block_expert, n_used, _ = _dispatch_tables(route, tokens, EXPERT_ROWS)
    y_rows = _experts_call(block_expert, n_used, row_tok, xn2, w_gate[0], w_up[0], w_down[0],
                           EXPERT_ROWS)
    out = _combine_call(_combine_positions(dest, tokens, c_rows), y_rows, h1, route,
                        final_norm_w.reshape(1, -1), c_rows)
    return out.reshape(batch, seq, D_MODEL)
```

```python
import jax
import jax.numpy as jnp
from jax import lax
from jax.experimental import pallas as pl
from jax.experimental.pallas import tpu as pltpu
from jax.experimental.pallas import tpu_sc as plsc

F32 = jnp.float32
BF16 = jnp.bfloat16
EPS = 1e-6

D_MODEL = 1024
N_META = 16
CONV_CH = 512
CONV_GROUP_W = 64
HEADS = 4
HEAD_DIM = 128
GDN_W = HEADS * HEAD_DIM
CHUNK = 64
N_GROUPS = 4
EXPERTS_PER_GROUP = 8
N_EXPERTS = N_GROUPS * EXPERTS_PER_GROUP
TOP_K = 2
D_EXPERT = 512
LANES = 128
HIST = 8

PROJ_ROWS = 512
GDN_CHUNKS = 8
GDN_GROUP = 2
MIX_ROWS = 512
EXPERT_ROWS = 256
COMBINE_ROWS = 256
VMEM_LIMIT = 56 * 1024 * 1024
SC_CORES = 2
SC_SUBCORES = 16
SC_DISPATCH_CHUNK = 64
SC_COMBINE_CHUNK = 32


def _dot(a, b):
    return jnp.dot(a, b, preferred_element_type=F32)


def _dot_nt(a, b):
    return lax.dot_general(a, b, (((1,), (1,)), ((), ())), preferred_element_type=F32)


def _dot_tn(a, b):
    return lax.dot_general(a, b, (((0,), (0,)), ((), ())), preferred_element_type=F32)


def _split_bf16(x):
    hi = x.astype(BF16)
    lo = (x - hi.astype(F32)).astype(BF16)
    return hi, lo


def _sigmoid(x):
    return 1.0 / (1.0 + jnp.exp(-x))


def _proj_kernel(x_ref, hcu_ref, hqkv_ref, nw_ref, wa_ref, wq_ref, wz_ref, wab_ref, cmw_ref,
                 cmn_ref, gmat_ref, qcw_ref, alog_ref, dtb_ref,
                 ya_ref, q_ref, k_ref, v_ref, z_ref, gb_ref, tcu_ref, tqkv_ref,
                 cu_s, qkv_s):
    rows = x_ref.shape[0]

    @pl.when(pl.program_id(1) == 0)
    def _():
        cu_s[0:HIST, :] = hcu_ref[...]
        qkv_s[0:HIST, :] = hqkv_ref[...]

    x = x_ref[...]
    ms = jnp.mean(x * x, axis=-1, keepdims=True)
    xn = (x * lax.rsqrt(ms + EPS) * nw_ref[...]).astype(BF16)

    pa = _dot(xn, wa_ref[...])
    gate_b = pa[:, :CONV_CH]
    cu = pa[:, CONV_CH:2 * CONV_CH] * pa[:, 2 * CONV_CH:]
    cu_s[HIST:HIST + rows, :] = cu
    conv = (cu_s[pl.ds(HIST - 2, rows), :] * cmw_ref[0:1, :]
            + cu_s[pl.ds(HIST - 1, rows), :] * cmw_ref[1:2, :]
            + cu * cmw_ref[2:3, :])
    ya = gate_b * conv
    sq_hi, sq_lo = _split_bf16(ya * ya)
    msg = _dot(sq_hi, gmat_ref[...]) + _dot(sq_lo, gmat_ref[...])
    ya_ref[...] = (ya * lax.rsqrt(msg + EPS) * cmn_ref[...]).astype(BF16)
    tail_cu = cu_s[rows:rows + HIST, :]
    cu_s[0:HIST, :] = tail_cu
    tcu_ref[...] = tail_cu

    pq = _dot(xn, wq_ref[...])
    qkv_s[HIST:HIST + rows, :] = pq
    c = (qkv_s[pl.ds(HIST - 3, rows), :] * qcw_ref[0:1, :]
         + qkv_s[pl.ds(HIST - 2, rows), :] * qcw_ref[1:2, :]
         + qkv_s[pl.ds(HIST - 1, rows), :] * qcw_ref[2:3, :]
         + pq * qcw_ref[3:4, :])
    c = c * _sigmoid(c)
    tail_qkv = qkv_s[rows:rows + HIST, :]
    qkv_s[0:HIST, :] = tail_qkv
    tqkv_ref[...] = tail_qkv
    for h in range(HEADS):
        sl = slice(h * HEAD_DIM, (h + 1) * HEAD_DIM)
        qh = c[:, h * HEAD_DIM:(h + 1) * HEAD_DIM]
        kh = c[:, GDN_W + h * HEAD_DIM:GDN_W + (h + 1) * HEAD_DIM]
        qn = qh * lax.rsqrt(jnp.sum(qh * qh, axis=-1, keepdims=True) + EPS)
        q_ref[:, sl] = qn * (HEAD_DIM ** -0.5)
        k_ref[:, sl] = kh * lax.rsqrt(jnp.sum(kh * kh, axis=-1, keepdims=True) + EPS)
    v_ref[...] = c[:, 2 * GDN_W:]
    z_ref[...] = _dot(xn, wz_ref[...])

    ab = _dot(xn, wab_ref[...])
    sp_in = ab + dtb_ref[...]
    softplus = jnp.maximum(sp_in, 0.0) + jnp.log1p(jnp.exp(-jnp.abs(sp_in)))
    g = -jnp.exp(alog_ref[...]) * softplus
    lane = lax.broadcasted_iota(jnp.int32, ab.shape, 1)
    gb_ref[...] = jnp.where(lane < HEADS, g, jnp.where(lane < 2 * HEADS, _sigmoid(ab), 0.0))


def _proj_call(x3, hist_cu, hist_qkv, w, rows):
    nb, seq, _ = x3.shape
    nt = seq // rows
    tok = lambda width: pl.BlockSpec((None, rows, width), lambda b, t: (b, t, 0))
    full = lambda a: pl.BlockSpec(a.shape, lambda b, t: (0,) * a.ndim)
    tail = lambda width: pl.BlockSpec((None, HIST, width), lambda b, t: (b, 0, 0))
    consts = (hist_cu, hist_qkv, w['mix_norm_w'], w['w_a'], w['w_qkv'], w['w_z'], w['w_ab'],
              w['conv_mix_w'], w['conv_mix_norm_w'], w['gmat'], w['qkv_conv_w'], w['a_log'],
              w['dt_bias'])
    out_shape = (
        jax.ShapeDtypeStruct((nb, seq, CONV_CH), BF16),
        jax.ShapeDtypeStruct((nb, seq, GDN_W), F32),
        jax.ShapeDtypeStruct((nb, seq, GDN_W), F32),
        jax.ShapeDtypeStruct((nb, seq, GDN_W), F32),
        jax.ShapeDtypeStruct((nb, seq, GDN_W), F32),
        jax.ShapeDtypeStruct((nb, seq, LANES), F32),
        jax.ShapeDtypeStruct((nb, HIST, CONV_CH), F32),
        jax.ShapeDtypeStruct((nb, HIST, 3 * GDN_W), F32),
    )
    return pl.pallas_call(
        _proj_kernel,
        grid=(nb, nt),
        in_specs=[tok(D_MODEL)] + [full(a) for a in consts],
        out_specs=(tok(CONV_CH), tok(GDN_W), tok(GDN_W), tok(GDN_W), tok(GDN_W), tok(LANES),
                   tail(CONV_CH), tail(3 * GDN_W)),
        out_shape=out_shape,
        scratch_shapes=[pltpu.VMEM((rows + HIST, CONV_CH), F32),
                        pltpu.VMEM((rows + HIST, 3 * GDN_W), F32)],
        compiler_params=pltpu.CompilerParams(
            dimension_semantics=("arbitrary", "arbitrary"), vmem_limit_bytes=VMEM_LIMIT),
        name="proj",
    )(x3, *consts)


def _chunk_masks():
    row = lax.broadcasted_iota(jnp.int32, (CHUNK, CHUNK), 0)
    col = lax.broadcasted_iota(jnp.int32, (CHUNK, CHUNK), 1)
    incl = row >= col
    strict = row > col
    levels = []
    n = 1
    while n < CHUNK:
        levels.append((row // (2 * n) == col // (2 * n)) & ((row // n) % 2 == 1) & ((col // n) % 2 == 0))
        n *= 2
    return incl, strict, levels


def _chunk_cumsum(gb_blk, incl):
    tri = incl.astype(BF16)
    hi, lo = _split_bf16(gb_blk)
    return _dot(tri, hi) + _dot(tri, lo)


def _chunk_transforms(chains, masks, state_only):
    incl, strict, levels = masks
    eye = (lax.broadcasted_iota(jnp.int32, (CHUNK, CHUNK), 0)
           == lax.broadcasted_iota(jnp.int32, (CHUNK, CHUNK), 1)).astype(F32)
    decay = [jnp.exp(jnp.where(incl, gc_col - gc_row, -jnp.inf))
             for (_, _, _, _, gc_col, gc_row, _) in chains]
    kb = [kh * beta for (_, kh, _, beta, _, _, _) in chains]
    k_bf = [kh.astype(BF16) for (_, kh, _, _, _, _, _) in chains]
    a_mat = [jnp.where(strict, _dot_nt(kb_i.astype(BF16), k_i) * d_i, 0.0)
             for kb_i, k_i, d_i in zip(kb, k_bf, decay)]
    t_inv = [eye - jnp.where(levels[0], a_i, 0.0) for a_i in a_mat]
    for lvl in levels[1:]:
        t_bf = [t_i.astype(BF16) for t_i in t_inv]
        m1 = [_dot(jnp.where(lvl, a_i, 0.0).astype(BF16), t_i) for a_i, t_i in zip(a_mat, t_bf)]
        t_inv = [t_i - _dot(tb_i, m_i.astype(BF16)) for t_i, tb_i, m_i in zip(t_inv, t_bf, m1)]
    rhs = [jnp.concatenate([vh * beta, kb_i * jnp.exp(gc_col)], axis=1)
           for (_, _, vh, beta, gc_col, _, _), kb_i in zip(chains, kb)]
    uw = [_dot(t_i.astype(BF16), r_i.astype(BF16)).astype(BF16)
          for t_i, r_i in zip(t_inv, rhs)]
    kd = [kh * jnp.exp(g_last - gc_col) for (_, kh, _, _, gc_col, _, g_last) in chains]
    pn = [_dot_tn(kd_i.astype(BF16), uw_i) for kd_i, uw_i in zip(kd, uw)]
    if state_only:
        return [pn_i[:, :HEAD_DIM] for pn_i in pn]
    intra = [jnp.where(incl, _dot_nt(qh.astype(BF16), k_i) * d_i, 0.0)
             for (qh, _, _, _, _, _, _), k_i, d_i in zip(chains, k_bf, decay)]
    iuw = [_dot(in_i.astype(BF16), uw_i) for in_i, uw_i in zip(intra, uw)]
    out = []
    for (qh, _, _, _, gc_col, _, g_last), pn_i, iuw_i in zip(chains, pn, iuw):
        q_part = qh * jnp.exp(gc_col) - iuw_i[:, HEAD_DIM:]
        out.append((q_part, pn_i[:, HEAD_DIM:], iuw_i[:, :HEAD_DIM], pn_i[:, :HEAD_DIM],
                    jnp.exp(g_last)))
    return out


def _gdn_kernel(pq_ref, pk_ref, pv_ref, pgb_ref, q_ref, k_ref, v_ref, gb_ref, o_ref,
                s_s, qp_s, op_s, n_s, a_s):
    nb = q_ref.shape[0]
    n_chunks = q_ref.shape[1] // CHUNK
    group = GDN_GROUP if n_chunks % GDN_GROUP == 0 else 1
    masks = _chunk_masks()

    def chains_of(gb_blk, q_blk, k_blk, v_blk):
        gc = _chunk_cumsum(gb_blk, masks[0])
        gc_t = gc.T
        res = []
        for h in range(HEADS):
            sl = slice(h * HEAD_DIM, (h + 1) * HEAD_DIM)
            res.append((q_blk(sl), k_blk(sl), v_blk(sl), gb_blk[:, HEADS + h:HEADS + h + 1],
                        gc[:, h:h + 1], gc_t[h:h + 1, :CHUNK], gc[CHUNK - 1:CHUNK, h:h + 1]))
        return res

    @pl.when(pl.program_id(0) == 0)
    def _():
        chains = chains_of(pgb_ref[...], lambda sl: pq_ref[:, sl], lambda sl: pk_ref[:, sl],
                           lambda sl: pv_ref[:, sl])
        for h, n_mat in enumerate(_chunk_transforms(chains, masks, True)):
            for b in range(nb):
                s_s[b * HEADS + h] = n_mat

    def transform_group(gi, carry):
        chains, where = [], []
        for cc in range(group):
            c = gi * group + cc
            rows = pl.ds(pl.multiple_of(c * CHUNK, CHUNK), CHUNK)
            for b in range(nb):
                chains += chains_of(gb_ref[b, rows, :], lambda sl: q_ref[b, rows, sl],
                                    lambda sl: k_ref[b, rows, sl], lambda sl: v_ref[b, rows, sl])
                where += [(c, b * HEADS + h) for h in range(HEADS)]
        for (c, ch), (q_part, p_mat, o_part, n_mat, a) in zip(
                where, _chunk_transforms(chains, masks, False)):
            qp_s[c, ch, 0:CHUNK, :] = q_part.astype(BF16)
            qp_s[c, ch, CHUNK:, :] = p_mat.astype(BF16)
            op_s[c, ch] = o_part
            n_s[c, ch] = n_mat
            a_s[c, ch] = jnp.broadcast_to(a, (8, HEAD_DIM))
        return carry

    lax.fori_loop(0, n_chunks // group, transform_group, 0)

    def scan_chunk(c, carry):
        r0 = pl.multiple_of(c * CHUNK, CHUNK)
        for b in range(nb):
            for h in range(HEADS):
                ch = b * HEADS + h
                s = s_s[ch]
                r = _dot(qp_s[c, ch], s.astype(BF16))
                o_ref[b, pl.ds(r0, CHUNK), h * HEAD_DIM:(h + 1) * HEAD_DIM] = r[:CHUNK] + op_s[c, ch]
                s_s[ch] = a_s[c, ch][0:1, :] * s - r[CHUNK:] + n_s[c, ch]
        return carry

    lax.fori_loop(0, n_chunks, scan_chunk, 0)


def _gdn_call(pq, pk, pv, pgb, q, k, v, gb, chunks_per_step):
    nb, seq, _ = q.shape
    rows = chunks_per_step * CHUNK
    steps = seq // rows
    tok = lambda width: pl.BlockSpec((nb, rows, width), lambda i: (0, i, 0))
    pre = lambda width: pl.BlockSpec((None, CHUNK, width), lambda i: (0, 0, 0))
    nch = nb * HEADS
    return pl.pallas_call(
        _gdn_kernel,
        grid=(steps,),
        in_specs=[pre(GDN_W), pre(GDN_W), pre(GDN_W), pre(LANES),
                  tok(GDN_W), tok(GDN_W), tok(GDN_W), tok(LANES)],
        out_specs=tok(GDN_W),
        out_shape=jax.ShapeDtypeStruct((nb, seq, GDN_W), F32),
        scratch_shapes=[
            pltpu.VMEM((nch, HEAD_DIM, HEAD_DIM), F32),
            pltpu.VMEM((chunks_per_step, nch, CHUNK + HEAD_DIM, HEAD_DIM), BF16),
            pltpu.VMEM((chunks_per_step, nch, CHUNK, HEAD_DIM), F32),
            pltpu.VMEM((chunks_per_step, nch, HEAD_DIM, HEAD_DIM), F32),
            pltpu.VMEM((chunks_per_step, nch, 8, HEAD_DIM), F32),
        ],
        compiler_params=pltpu.CompilerParams(
            dimension_semantics=("arbitrary",), vmem_limit_bytes=VMEM_LIMIT),
        name="gdn",
    )(pq, pk, pv, pgb, q, k, v, gb)


def _mix_out_kernel(x_ref, ya_ref, o_ref, z_ref, gnw_ref, woa_ref, wob_ref, fnw_ref, wrh_ref,
                    wrl_ref, br_ref, h_ref, xn_ref, route_ref):
    yb = []
    for h in range(HEADS):
        sl = slice(h * HEAD_DIM, (h + 1) * HEAD_DIM)
        oh = o_ref[:, sl]
        zh = z_ref[:, sl]
        on = oh * lax.rsqrt(jnp.mean(oh * oh, axis=-1, keepdims=True) + EPS) * gnw_ref[...]
        yb.append((on * (zh * _sigmoid(zh))).astype(BF16))
    yb = jnp.concatenate(yb, axis=1)
    h1 = x_ref[...] + (_dot(ya_ref[...], woa_ref[...]) + _dot(yb, wob_ref[...]))
    h_ref[...] = h1
    xn = h1 * lax.rsqrt(jnp.mean(h1 * h1, axis=-1, keepdims=True) + EPS) * fnw_ref[...]
    x_hi, x_lo = _split_bf16(xn)
    bits = pltpu.bitcast(x_hi.astype(F32), jnp.uint32)
    half = D_MODEL // 2
    xn_ref[...] = (bits[:, :half] >> 16) | (bits[:, half:] & jnp.uint32(0xFFFF0000))
    logits = (_dot(x_hi, wrh_ref[...]) + _dot(x_lo, wrh_ref[...]) + _dot(x_hi, wrl_ref[...])
              + br_ref[...])
    lane = lax.broadcasted_iota(jnp.int32, logits.shape, 1)
    neg = -jnp.inf
    big = jnp.int32(1 << 20)

    def argmax_first(vals):
        m = jnp.max(vals, axis=-1, keepdims=True)
        idx = jnp.min(jnp.where(vals == m, lane, big), axis=-1, keepdims=True)
        return m, idx

    grp = jnp.where(lane < N_GROUPS, logits, neg)
    g_max, g_sel = argmax_first(grp)
    p_grp = 1.0 / jnp.sum(jnp.exp(grp - g_max), axis=-1, keepdims=True)
    lo_lane = N_GROUPS + g_sel * EXPERTS_PER_GROUP
    ex = jnp.where((lane >= lo_lane) & (lane < lo_lane + EXPERTS_PER_GROUP), logits, neg)
    m1, i1 = argmax_first(ex)
    m2, i2 = argmax_first(jnp.where(lane == i1, neg, ex))
    e2 = jnp.exp(m2 - m1)
    w1 = 1.0 / (1.0 + e2) * p_grp
    w2 = e2 / (1.0 + e2) * p_grp
    route = jnp.where(lane == 0, (i1 - N_GROUPS).astype(F32),
                      jnp.where(lane == 1, (i2 - N_GROUPS).astype(F32),
                                jnp.where(lane == 2, w1, jnp.where(lane == 3, w2, 0.0))))
    route_ref[...] = route


def _mix_out_call(x2, ya, o, z, w, rows):
    tokens = x2.shape[0]
    tok = lambda width: pl.BlockSpec((rows, width), lambda i: (i, 0))
    full = lambda a: pl.BlockSpec(a.shape, lambda i: (0,) * a.ndim)
    consts = (w['gdn_norm_w'], w['w_out_a'], w['w_out_b'], w['ffn_norm_w'], w['w_route_hi'],
              w['w_route_lo'], w['b_route'])
    return pl.pallas_call(
        _mix_out_kernel,
        grid=(tokens // rows,),
        in_specs=[tok(D_MODEL), tok(CONV_CH), tok(GDN_W), tok(GDN_W)] + [full(a) for a in consts],
        out_specs=(tok(D_MODEL), tok(D_MODEL // 2), tok(LANES)),
        out_shape=(jax.ShapeDtypeStruct((tokens, D_MODEL), F32),
                   jax.ShapeDtypeStruct((tokens, D_MODEL // 2), jnp.uint32),
                   jax.ShapeDtypeStruct((tokens, LANES), F32)),
        compiler_params=pltpu.CompilerParams(
            dimension_semantics=("arbitrary",), vmem_limit_bytes=VMEM_LIMIT),
        name="mix_out",
    )(x2, ya, o, z, *consts)


def _sc_gather(table, idx, chunk, name):
    n = idx.shape[0]
    width = table.shape[1]
    workers = SC_CORES * SC_SUBCORES
    per_w = n // workers
    assert n % workers == 0 and per_w % (2 * chunk) == 0
    pairs = per_w // (2 * chunk)
    mesh = plsc.VectorSubcoreMesh(core_axis_name="c", subcore_axis_name="s",
                                  num_cores=SC_CORES, num_subcores=SC_SUBCORES)

    def body(table_hbm, idx_hbm, out_hbm, idx_v, buf_a, buf_b, sem_ga, sem_gb, sem_wa, sem_wb):
        base = (lax.axis_index("s") * SC_CORES + lax.axis_index("c")) * per_w
        pltpu.sync_copy(idx_hbm.at[pl.ds(base, per_w)], idx_v)

        def gather(c, buf, sem):
            off = pl.multiple_of(c * chunk, chunk)
            return pltpu.make_async_copy(table_hbm.at[idx_v.at[pl.ds(off, chunk)]], buf, sem)

        def write(c, buf, sem):
            off = pl.multiple_of(c * chunk, chunk)
            return pltpu.make_async_copy(buf, out_hbm.at[pl.ds(base + off, chunk)], sem)

        gather(0, buf_a, sem_ga).start()

        @pl.loop(0, pairs)
        def _(j):
            ca = 2 * j
            cb = ca + 1
            gather(cb, buf_b, sem_gb).start()
            gather(ca, buf_a, sem_ga).wait()
            write(ca, buf_a, sem_wa).start()
            gather(cb, buf_b, sem_gb).wait()
            write(cb, buf_b, sem_wb).start()
            write(ca, buf_a, sem_wa).wait()

            @pl.when(j + 1 < pairs)
            def _():
                gather(ca + 2, buf_a, sem_ga).start()

            write(cb, buf_b, sem_wb).wait()

    return pl.kernel(
        body,
        out_type=jax.ShapeDtypeStruct((n, width), table.dtype),
        mesh=mesh,
        scratch_types=[pltpu.VMEM((per_w,), jnp.int32),
                       pltpu.VMEM((chunk, width), table.dtype),
                       pltpu.VMEM((chunk, width), table.dtype),
                       pltpu.SemaphoreType.DMA, pltpu.SemaphoreType.DMA,
                       pltpu.SemaphoreType.DMA, pltpu.SemaphoreType.DMA],
        name=name,
    )(table, idx)


def _experts_kernel(bexp_ref, nused_ref, x_ref, wg_ref, wu_ref, wd_ref, y_ref, wg_s, wu_s, wd_s):
    i = pl.program_id(0)
    n_used = nused_ref[0]
    first_of_expert = (i == 0) | (bexp_ref[i] != bexp_ref[jnp.maximum(i - 1, 0)])

    @pl.when((i < n_used) & first_of_expert)
    def _():
        wg_s[...] = wg_ref[...].astype(BF16)
        wu_s[...] = wu_ref[...].astype(BF16)
        wd_s[...] = wd_ref[...].astype(BF16)

    @pl.when(i < n_used)
    def _():
        packed = x_ref[...]
        x_lo = pltpu.bitcast(packed << 16, F32).astype(BF16)
        x_hi = pltpu.bitcast(packed & jnp.uint32(0xFFFF0000), F32).astype(BF16)
        xb = jnp.concatenate([x_lo, x_hi], axis=1)
        gate = _dot(xb, wg_s[...])
        hid = (gate * _sigmoid(gate)) * _dot(xb, wu_s[...])
        y_ref[...] = _dot(hid.astype(BF16), wd_s[...])

    @pl.when(i >= n_used)
    def _():
        y_ref[...] = jnp.zeros(y_ref.shape, F32)


def _experts_call(block_expert, n_used, x_rows, w_gate, w_up, w_down, rows):
    n_blocks = block_expert.shape[0]
    wspec = lambda shape: pl.BlockSpec((None,) + shape, lambda i, be, nu: (be[i], 0, 0))
    grid_spec = pltpu.PrefetchScalarGridSpec(
        num_scalar_prefetch=2,
        grid=(n_blocks,),
        in_specs=[pl.BlockSpec((rows, D_MODEL // 2), lambda i, be, nu: (i, 0)),
                  wspec((D_MODEL, D_EXPERT)), wspec((D_MODEL, D_EXPERT)),
                  wspec((D_EXPERT, D_MODEL))],
        out_specs=pl.BlockSpec((rows, D_MODEL), lambda i, be, nu: (i, 0)),
        scratch_shapes=[pltpu.VMEM((D_MODEL, D_EXPERT), BF16),
                        pltpu.VMEM((D_MODEL, D_EXPERT), BF16),
                        pltpu.VMEM((D_EXPERT, D_MODEL), BF16)],
    )
    return pl.pallas_call(
        _experts_kernel,
        grid_spec=grid_spec,
        out_shape=jax.ShapeDtypeStruct((n_blocks * rows, D_MODEL), F32),
        compiler_params=pltpu.CompilerParams(
            dimension_semantics=("arbitrary",), vmem_limit_bytes=VMEM_LIMIT),
        name="experts",
    )(block_expert, n_used, x_rows, w_gate, w_up, w_down)


def _combine_kernel(y1_ref, y2_ref, h_ref, route_ref, fw_ref, out_ref):
    route = route_ref[...]
    moe = route[:, 2:3] * y1_ref[...] + route[:, 3:4] * y2_ref[...]
    h2 = h_ref[...] + moe
    out_ref[...] = h2 * lax.rsqrt(jnp.mean(h2 * h2, axis=-1, keepdims=True) + EPS) * fw_ref[...]


def _combine_call(y_tok, h1, route, final_w, rows):
    tokens = h1.shape[0]
    steps = tokens // rows
    tok = lambda width: pl.BlockSpec((rows, width), lambda i: (i, 0))
    return pl.pallas_call(
        _combine_kernel,
        grid=(steps,),
        in_specs=[tok(D_MODEL), pl.BlockSpec((rows, D_MODEL), lambda i: (i + steps, 0)),
                  tok(D_MODEL), tok(LANES), pl.BlockSpec((1, D_MODEL), lambda i: (0, 0))],
        out_specs=tok(D_MODEL),
        out_shape=jax.ShapeDtypeStruct((tokens, D_MODEL), F32),
        compiler_params=pltpu.CompilerParams(
            dimension_semantics=("arbitrary",), vmem_limit_bytes=VMEM_LIMIT),
        name="combine",
    )(y_tok, y_tok, h1, route, final_w)


def _dispatch_tables(route, tokens, rows):
    n_assign = tokens * TOP_K
    n_blocks = (n_assign + N_EXPERTS * (rows - 1)) // rows
    e1 = route[:, 0].astype(jnp.int32)
    e2 = route[:, 1].astype(jnp.int32)
    flat_e = jnp.concatenate([e1, e2])
    onehot = (flat_e[:, None] == jnp.arange(N_EXPERTS, dtype=jnp.int32)[None, :]).astype(jnp.int32)
    ranks = jnp.cumsum(onehot, axis=0) - onehot
    rank = jnp.sum(ranks * onehot, axis=1)
    counts = jnp.sum(onehot, axis=0)
    padded = (counts + rows - 1) // rows * rows
    pad_end = jnp.cumsum(padded)
    pad_start = pad_end - padded
    dest = (pad_start[flat_e] + rank).astype(jnp.int32)
    flat_tok = jnp.tile(jnp.arange(tokens, dtype=jnp.int32), TOP_K)
    unit = SC_CORES * SC_SUBCORES * 2 * SC_DISPATCH_CHUNK
    row_tok = jnp.zeros((pl.cdiv(n_blocks * rows, unit) * unit,), jnp.int32).at[dest].set(flat_tok)
    block_start = jnp.arange(n_blocks, dtype=jnp.int32) * rows
    block_expert = jnp.minimum(jnp.searchsorted(pad_end, block_start, side='right'),
                               N_EXPERTS - 1).astype(jnp.int32)
    n_used = (pad_end[-1] // rows).astype(jnp.int32).reshape(1)
    return dest, row_tok, block_expert, n_used


def _prepare_weights(mix_norm_w, w_in, conv_mix_w, conv_mix_norm_w, qkv_conv_w, a_log, dt_bias,
                     gdn_norm_w, w_out, ffn_norm_w, w_group, b_group, w_router, b_router):
    pad_lanes = lambda v: jnp.pad(v.reshape(1, -1), ((0, 0), (0, LANES - v.size)))
    c3 = 3 * CONV_CH
    w_ab = jnp.pad(w_in[:, c3 + 4 * GDN_W:], ((0, 0), (0, LANES - 2 * HEADS)))
    grp = jnp.arange(CONV_CH) // CONV_GROUP_W
    gmat = jnp.where(grp[:, None] == grp[None, :], 1.0 / CONV_GROUP_W, 0.0).astype(BF16)
    w_route = jnp.concatenate([w_group, w_router.reshape(D_MODEL, N_EXPERTS)], axis=1)
    w_route = jnp.pad(w_route, ((0, 0), (0, LANES - w_route.shape[1])))
    w_route_hi = w_route.astype(BF16)
    return dict(
        mix_norm_w=mix_norm_w.reshape(1, -1),
        w_a=w_in[:, :c3].astype(BF16),
        w_qkv=w_in[:, c3:c3 + 3 * GDN_W].astype(BF16),
        w_z=w_in[:, c3 + 3 * GDN_W:c3 + 4 * GDN_W].astype(BF16),
        w_ab=w_ab.astype(BF16),
        conv_mix_w=conv_mix_w,
        conv_mix_norm_w=conv_mix_norm_w.reshape(1, -1),
        gmat=gmat,
        qkv_conv_w=qkv_conv_w,
        a_log=pad_lanes(a_log),
        dt_bias=pad_lanes(dt_bias),
        gdn_norm_w=gdn_norm_w.reshape(1, -1),
        w_out_a=w_out[:CONV_CH].astype(BF16),
        w_out_b=w_out[CONV_CH:].astype(BF16),
        ffn_norm_w=ffn_norm_w.reshape(1, -1),
        w_route_hi=w_route_hi,
        w_route_lo=(w_route - w_route_hi.astype(F32)).astype(BF16),
        b_route=pad_lanes(jnp.concatenate([b_group, b_router.reshape(-1)])),
    )


def _tile(n, preferred):
    return preferred if n % preferred == 0 else n


def kernel(x, meta_tokens, mix_norm_w, w_in, conv_mix_w, conv_mix_norm_w, qkv_conv_w, a_log,
           dt_bias, gdn_norm_w, w_out, ffn_norm_w, w_group, b_group, w_router, b_router, w_gate,
           w_up, w_down, final_norm_w):
    assert mix_norm_w.shape[0] == 1, "single-layer kernel"
    batch, seq, _ = x.shape
    assert seq % CHUNK == 0
    w = _prepare_weights(mix_norm_w[0], w_in[0], conv_mix_w[0], conv_mix_norm_w[0], qkv_conv_w[0],
                         a_log[0], dt_bias[0], gdn_norm_w[0], w_out[0], ffn_norm_w[0], w_group[0],
                         b_group[0], w_router[0], b_router[0])

    prefix = jnp.concatenate([jnp.zeros((CHUNK - N_META, D_MODEL), x.dtype),
                              meta_tokens.astype(x.dtype)], axis=0)[None]
    zero_cu = jnp.zeros((HIST, CONV_CH), F32)
    zero_qkv = jnp.zeros((HIST, 3 * GDN_W), F32)
    _, pq, pk, pv, _, pgb, tail_cu, tail_qkv = _proj_call(prefix, zero_cu, zero_qkv, w, CHUNK)

    ya, q, k, v, z, gb, _, _ = _proj_call(x, tail_cu[0], tail_qkv[0], w, _tile(seq, PROJ_ROWS))
    o = _gdn_call(pq, pk, pv, pgb, q, k, v, gb, _tile(seq // CHUNK, GDN_CHUNKS))

    tokens = batch * seq
    flat = lambda a: a.reshape(tokens, a.shape[-1])
    h1, xn2, route = _mix_out_call(flat(x), flat(ya), flat(o), flat(z), w, _tile(tokens, MIX_ROWS))

    dest, row_tok, block_expert, n_used = _dispatch_tables(route, tokens, EXPERT_ROWS)
    x_rows = _sc_gather(xn2, row_tok, SC_DISPATCH_CHUNK, "dispatch_gather")
    y_rows = _experts_call(block_expert, n_used, x_rows, w_gate[0], w_up[0], w_down[0], EXPERT_ROWS)
    y_tok = _sc_gather(y_rows, dest, SC_COMBINE_CHUNK, "combine_gather")
    out = _combine_call(y_tok, h1, route, final_norm_w.reshape(1, -1), _tile(tokens, COMBINE_ROWS))
    return out.reshape(batch, seq, D_MODEL)
```

```python
import jax
import jax.numpy as jnp
from jax import lax
from jax.experimental import pallas as pl
from jax.experimental.pallas import tpu as pltpu
from jax.experimental.pallas import tpu_sc as plsc

F32 = jnp.float32
BF16 = jnp.bfloat16
EPS = 1e-6

D_MODEL = 1024
N_META = 16
CONV_CH = 512
CONV_GROUP_W = 64
HEADS = 4
HEAD_DIM = 128
GDN_W = HEADS * HEAD_DIM
CHUNK = 64
N_GROUPS = 4
EXPERTS_PER_GROUP = 8
N_EXPERTS = N_GROUPS * EXPERTS_PER_GROUP
TOP_K = 2
D_EXPERT = 512
LANES = 128
HIST = 8

PROJ_ROWS = 512
GDN_CHUNKS = 8
GDN_GROUP = 2
MIX_ROWS = 512
EXPERT_ROWS = 256
COMBINE_ROWS = 256
VMEM_LIMIT = 56 * 1024 * 1024
SC_CORES = 2
SC_SUBCORES = 16
SC_DISPATCH_CHUNK = 64
SC_COMBINE_CHUNK = 32


def _dot(a, b):
    return jnp.dot(a, b, preferred_element_type=F32)


def _dot_nt(a, b):
    return lax.dot_general(a, b, (((1,), (1,)), ((), ())), preferred_element_type=F32)


def _dot_tn(a, b):
    return lax.dot_general(a, b, (((0,), (0,)), ((), ())), preferred_element_type=F32)


def _split_bf16(x):
    hi = x.astype(BF16)
    lo = (x - hi.astype(F32)).astype(BF16)
    return hi, lo


def _sigmoid(x):
    return 1.0 / (1.0 + jnp.exp(-x))


def _proj_kernel(x_ref, hcu_ref, hqkv_ref, nw_ref, wa_ref, wq_ref, wz_ref, wab_ref, cmw_ref,
                 cmn_ref, gmat_ref, qcw_ref, alog_ref, dtb_ref,
                 ya_ref, q_ref, k_ref, v_ref, z_ref, gb_ref, tcu_ref, tqkv_ref,
                 cu_s, qkv_s):
    rows = x_ref.shape[0]

    @pl.when(pl.program_id(1) == 0)
    def _():
        cu_s[0:HIST, :] = hcu_ref[...]
        qkv_s[0:HIST, :] = hqkv_ref[...]

    x = x_ref[...]
    ms = jnp.mean(x * x, axis=-1, keepdims=True)
    xn = (x * lax.rsqrt(ms + EPS) * nw_ref[...]).astype(BF16)

    pa = _dot(xn, wa_ref[...])
    gate_b = pa[:, :CONV_CH]
    cu = pa[:, CONV_CH:2 * CONV_CH] * pa[:, 2 * CONV_CH:]
    cu_s[HIST:HIST + rows, :] = cu
    conv = (cu_s[pl.ds(HIST - 2, rows), :] * cmw_ref[0:1, :]
            + cu_s[pl.ds(HIST - 1, rows), :] * cmw_ref[1:2, :]
            + cu * cmw_ref[2:3, :])
    ya = gate_b * conv
    sq_hi, sq_lo = _split_bf16(ya * ya)
    msg = _dot(sq_hi, gmat_ref[...]) + _dot(sq_lo, gmat_ref[...])
    ya_ref[...] = (ya * lax.rsqrt(msg + EPS) * cmn_ref[...]).astype(BF16)
    tail_cu = cu_s[rows:rows + HIST, :]
    cu_s[0:HIST, :] = tail_cu
    tcu_ref[...] = tail_cu

    pq = _dot(xn, wq_ref[...])
    qkv_s[HIST:HIST + rows, :] = pq
    c = (qkv_s[pl.ds(HIST - 3, rows), :] * qcw_ref[0:1, :]
         + qkv_s[pl.ds(HIST - 2, rows), :] * qcw_ref[1:2, :]
         + qkv_s[pl.ds(HIST - 1, rows), :] * qcw_ref[2:3, :]
         + pq * qcw_ref[3:4, :])
    c = c * _sigmoid(c)
    tail_qkv = qkv_s[rows:rows + HIST, :]
    qkv_s[0:HIST, :] = tail_qkv
    tqkv_ref[...] = tail_qkv
    for h in range(HEADS):
        sl = slice(h * HEAD_DIM, (h + 1) * HEAD_DIM)
        qh = c[:, h * HEAD_DIM:(h + 1) * HEAD_DIM]
        kh = c[:, GDN_W + h * HEAD_DIM:GDN_W + (h + 1) * HEAD_DIM]
        qn = qh * lax.rsqrt(jnp.sum(qh * qh, axis=-1, keepdims=True) + EPS)
        q_ref[:, sl] = qn * (HEAD_DIM ** -0.5)
        k_ref[:, sl] = kh * lax.rsqrt(jnp.sum(kh * kh, axis=-1, keepdims=True) + EPS)
    v_ref[...] = c[:, 2 * GDN_W:]
    z_ref[...] = _dot(xn, wz_ref[...])

    ab = _dot(xn, wab_ref[...])
    sp_in = ab + dtb_ref[...]
    softplus = jnp.maximum(sp_in, 0.0) + jnp.log1p(jnp.exp(-jnp.abs(sp_in)))
    g = -jnp.exp(alog_ref[...]) * softplus
    lane = lax.broadcasted_iota(jnp.int32, ab.shape, 1)
    gb_ref[...] = jnp.where(lane < HEADS, g, jnp.where(lane < 2 * HEADS, _sigmoid(ab), 0.0))


def _proj_call(x3, hist_cu, hist_qkv, w, rows):
    nb, seq, _ = x3.shape
    nt = seq // rows
    tok = lambda width: pl.BlockSpec((None, rows, width), lambda b, t: (b, t, 0))
    full = lambda a: pl.BlockSpec(a.shape, lambda b, t: (0,) * a.ndim)
    tail = lambda width: pl.BlockSpec((None, HIST, width), lambda b, t: (b, 0, 0))
    consts = (hist_cu, hist_qkv, w['mix_norm_w'], w['w_a'], w['w_qkv'], w['w_z'], w['w_ab'],
              w['conv_mix_w'], w['conv_mix_norm_w'], w['gmat'], w['qkv_conv_w'], w['a_log'],
              w['dt_bias'])
    out_shape = (
        jax.ShapeDtypeStruct((nb, seq, CONV_CH), BF16),
        jax.ShapeDtypeStruct((nb, seq, GDN_W), F32),
        jax.ShapeDtypeStruct((nb, seq, GDN_W), F32),
        jax.ShapeDtypeStruct((nb, seq, GDN_W), F32),
        jax.ShapeDtypeStruct((nb, seq, GDN_W), F32),
        jax.ShapeDtypeStruct((nb, seq, LANES), F32),
        jax.ShapeDtypeStruct((nb, HIST, CONV_CH), F32),
        jax.ShapeDtypeStruct((nb, HIST, 3 * GDN_W), F32),
    )
    return pl.pallas_call(
        _proj_kernel,
        grid=(nb, nt),
        in_specs=[tok(D_MODEL)] + [full(a) for a in consts],
        out_specs=(tok(CONV_CH), tok(GDN_W), tok(GDN_W), tok(GDN_W), tok(GDN_W), tok(LANES),
                   tail(CONV_CH), tail(3 * GDN_W)),
        out_shape=out_shape,
        scratch_shapes=[pltpu.VMEM((rows + HIST, CONV_CH), F32),
                        pltpu.VMEM((rows + HIST, 3 * GDN_W), F32)],
        compiler_params=pltpu.CompilerParams(
            dimension_semantics=("arbitrary", "arbitrary"), vmem_limit_bytes=VMEM_LIMIT),
        name="proj",
    )(x3, *consts)


def _chunk_masks():
    row = lax.broadcasted_iota(jnp.int32, (CHUNK, CHUNK), 0)
    col = lax.broadcasted_iota(jnp.int32, (CHUNK, CHUNK), 1)
    incl = row >= col
    strict = row > col
    levels = []
    n = 1
    while n < CHUNK:
        levels.append((row // (2 * n) == col // (2 * n)) & ((row // n) % 2 == 1) & ((col // n) % 2 == 0))
        n *= 2
    return incl, strict, levels


def _chunk_cumsum(gb_blk, incl):
    tri = incl.astype(BF16)
    hi, lo = _split_bf16(gb_blk)
    return _dot(tri, hi) + _dot(tri, lo)


def _chunk_transforms(chains, masks, state_only):
    incl, strict, levels = masks
    eye = (lax.broadcasted_iota(jnp.int32, (CHUNK, CHUNK), 0)
           == lax.broadcasted_iota(jnp.int32, (CHUNK, CHUNK), 1)).astype(F32)
    decay = [jnp.exp(jnp.where(incl, gc_col - gc_row, -jnp.inf))
             for (_, _, _, _, gc_col, gc_row, _) in chains]
    kb = [kh * beta for (_, kh, _, beta, _, _, _) in chains]
    k_bf = [kh.astype(BF16) for (_, kh, _, _, _, _, _) in chains]
    a_mat = [jnp.where(strict, _dot_nt(kb_i.astype(BF16), k_i) * d_i, 0.0)
             for kb_i, k_i, d_i in zip(kb, k_bf, decay)]
    t_inv = [eye - jnp.where(levels[0], a_i, 0.0) for a_i in a_mat]
    for lvl in levels[1:]:
        t_bf = [t_i.astype(BF16) for t_i in t_inv]
        m1 = [_dot(jnp.where(lvl, a_i, 0.0).astype(BF16), t_i) for a_i, t_i in zip(a_mat, t_bf)]
        t_inv = [t_i - _dot(tb_i, m_i.astype(BF16)) for t_i, tb_i, m_i in zip(t_inv, t_bf, m1)]
    rhs = [jnp.concatenate([vh * beta, kb_i * jnp.exp(gc_col)], axis=1)
           for (_, _, vh, beta, gc_col, _, _), kb_i in zip(chains, kb)]
    uw = [_dot(t_i.astype(BF16), r_i.astype(BF16)).astype(BF16)
          for t_i, r_i in zip(t_inv, rhs)]
    kd = [kh * jnp.exp(g_last - gc_col) for (_, kh, _, _, gc_col, _, g_last) in chains]
    pn = [_dot_tn(kd_i.astype(BF16), uw_i) for kd_i, uw_i in zip(kd, uw)]
    if state_only:
        return [pn_i[:, :HEAD_DIM] for pn_i in pn]
    intra = [jnp.where(incl, _dot_nt(qh.astype(BF16), k_i) * d_i, 0.0)
             for (qh, _, _, _, _, _, _), k_i, d_i in zip(chains, k_bf, decay)]
    iuw = [_dot(in_i.astype(BF16), uw_i) for in_i, uw_i in zip(intra, uw)]
    out = []
    for (qh, _, _, _, gc_col, _, g_last), pn_i, iuw_i in zip(chains, pn, iuw):
        q_part = qh * jnp.exp(gc_col) - iuw_i[:, HEAD_DIM:]
        out.append((q_part, pn_i[:, HEAD_DIM:], iuw_i[:, :HEAD_DIM], pn_i[:, :HEAD_DIM],
                    jnp.exp(g_last)))
    return out


def _gdn_kernel(pq_ref, pk_ref, pv_ref, pgb_ref, q_ref, k_ref, v_ref, gb_ref, o_ref,
                s_s, qp_s, op_s, n_s, a_s):
    nb = q_ref.shape[0]
    n_chunks = q_ref.shape[1] // CHUNK
    group = GDN_GROUP if n_chunks % GDN_GROUP == 0 else 1
    masks = _chunk_masks()

    def chains_of(gb_blk, q_blk, k_blk, v_blk):
        gc = _chunk_cumsum(gb_blk, masks[0])
        gc_t = gc.T
        res = []
        for h in range(HEADS):
            sl = slice(h * HEAD_DIM, (h + 1) * HEAD_DIM)
            res.append((q_blk(sl), k_blk(sl), v_blk(sl), gb_blk[:, HEADS + h:HEADS + h + 1],
                        gc[:, h:h + 1], gc_t[h:h + 1, :CHUNK], gc[CHUNK - 1:CHUNK, h:h + 1]))
        return res

    @pl.when(pl.program_id(0) == 0)
    def _():
        chains = chains_of(pgb_ref[...], lambda sl: pq_ref[:, sl], lambda sl: pk_ref[:, sl],
                           lambda sl: pv_ref[:, sl])
        for h, n_mat in enumerate(_chunk_transforms(chains, masks, True)):
            for b in range(nb):
                s_s[b * HEADS + h] = n_mat

    def transform_group(gi, carry):
        chains, where = [], []
        for cc in range(group):
            c = gi * group + cc
            rows = pl.ds(pl.multiple_of(c * CHUNK, CHUNK), CHUNK)
            for b in range(nb):
                chains += chains_of(gb_ref[b, rows, :], lambda sl: q_ref[b, rows, sl],
                                    lambda sl: k_ref[b, rows, sl], lambda sl: v_ref[b, rows, sl])
                where += [(c, b * HEADS + h) for h in range(HEADS)]
        for (c, ch), (q_part, p_mat, o_part, n_mat, a) in zip(
                where, _chunk_transforms(chains, masks, False)):
            qp_s[c, ch, 0:CHUNK, :] = q_part.astype(BF16)
            qp_s[c, ch, CHUNK:, :] = p_mat.astype(BF16)
            op_s[c, ch] = o_part
            n_s[c, ch] = n_mat
            a_s[c, ch] = jnp.broadcast_to(a, (8, HEAD_DIM))
        return carry

    lax.fori_loop(0, n_chunks // group, transform_group, 0)

    def scan_chunk(c, carry):
        r0 = pl.multiple_of(c * CHUNK, CHUNK)
        for b in range(nb):
            for h in range(HEADS):
                ch = b * HEADS + h
                s = s_s[ch]
                r = _dot(qp_s[c, ch], s.astype(BF16))
                o_ref[b, pl.ds(r0, CHUNK), h * HEAD_DIM:(h + 1) * HEAD_DIM] = r[:CHUNK] + op_s[c, ch]
                s_s[ch] = a_s[c, ch][0:1, :] * s - r[CHUNK:] + n_s[c, ch]
        return carry

    lax.fori_loop(0, n_chunks, scan_chunk, 0)


def _gdn_call(pq, pk, pv, pgb, q, k, v, gb, chunks_per_step):
    nb, seq, _ = q.shape
    rows = chunks_per_step * CHUNK
    steps = seq // rows
    tok = lambda width: pl.BlockSpec((nb, rows, width), lambda i: (0, i, 0))
    pre = lambda width: pl.BlockSpec((None, CHUNK, width), lambda i: (0, 0, 0))
    nch = nb * HEADS
    return pl.pallas_call(
        _gdn_kernel,
        grid=(steps,),
        in_specs=[pre(GDN_W), pre(GDN_W), pre(GDN_W), pre(LANES),
                  tok(GDN_W), tok(GDN_W), tok(GDN_W), tok(LANES)],
        out_specs=tok(GDN_W),
        out_shape=jax.ShapeDtypeStruct((nb, seq, GDN_W), F32),
        scratch_shapes=[
            pltpu.VMEM((nch, HEAD_DIM, HEAD_DIM), F32),
            pltpu.VMEM((chunks_per_step, nch, CHUNK + HEAD_DIM, HEAD_DIM), BF16),
            pltpu.VMEM((chunks_per_step, nch, CHUNK, HEAD_DIM), F32),
            pltpu.VMEM((chunks_per_step, nch, HEAD_DIM, HEAD_DIM), F32),
            pltpu.VMEM((chunks_per_step, nch, 8, HEAD_DIM), F32),
        ],
        compiler_params=pltpu.CompilerParams(
            dimension_semantics=("arbitrary",), vmem_limit_bytes=VMEM_LIMIT),
        name="gdn",
    )(pq, pk, pv, pgb, q, k, v, gb)


def _mix_out_kernel(x_ref, ya_ref, o_ref, z_ref, gnw_ref, woa_ref, wob_ref, fnw_ref, wrh_ref,
                    wrl_ref, br_ref, h_ref, xn_ref, route_ref):
    yb = []
    for h in range(HEADS):
        sl = slice(h * HEAD_DIM, (h + 1) * HEAD_DIM)
        oh = o_ref[:, sl]
        zh = z_ref[:, sl]
        on = oh * lax.rsqrt(jnp.mean(oh * oh, axis=-1, keepdims=True) + EPS) * gnw_ref[...]
        yb.append((on * (zh * _sigmoid(zh))).astype(BF16))
    yb = jnp.concatenate(yb, axis=1)
    h1 = x_ref[...] + (_dot(ya_ref[...], woa_ref[...]) + _dot(yb, wob_ref[...]))
    h_ref[...] = h1
    xn = h1 * lax.rsqrt(jnp.mean(h1 * h1, axis=-1, keepdims=True) + EPS) * fnw_ref[...]
    x_hi, x_lo = _split_bf16(xn)
    bits = pltpu.bitcast(x_hi.astype(F32), jnp.uint32)
    half = D_MODEL // 2
    xn_ref[...] = (bits[:, :half] >> 16) | (bits[:, half:] & jnp.uint32(0xFFFF0000))
    logits = (_dot(x_hi, wrh_ref[...]) + _dot(x_lo, wrh_ref[...]) + _dot(x_hi, wrl_ref[...])
              + br_ref[...])
    lane = lax.broadcasted_iota(jnp.int32, logits.shape, 1)
    neg = -jnp.inf
    big = jnp.int32(1 << 20)

    def argmax_first(vals):
        m = jnp.max(vals, axis=-1, keepdims=True)
        idx = jnp.min(jnp.where(vals == m, lane, big), axis=-1, keepdims=True)
        return m, idx

    grp = jnp.where(lane < N_GROUPS, logits, neg)
    g_max, g_sel = argmax_first(grp)
    p_grp = 1.0 / jnp.sum(jnp.exp(grp - g_max), axis=-1, keepdims=True)
    lo_lane = N_GROUPS + g_sel * EXPERTS_PER_GROUP
    ex = jnp.where((lane >= lo_lane) & (lane < lo_lane + EXPERTS_PER_GROUP), logits, neg)
    m1, i1 = argmax_first(ex)
    m2, i2 = argmax_first(jnp.where(lane == i1, neg, ex))
    e2 = jnp.exp(m2 - m1)
    w1 = 1.0 / (1.0 + e2) * p_grp
    w2 = e2 / (1.0 + e2) * p_grp
    route = jnp.where(lane == 0, (i1 - N_GROUPS).astype(F32),
                      jnp.where(lane == 1, (i2 - N_GROUPS).astype(F32),
                                jnp.where(lane == 2, w1, jnp.where(lane == 3, w2, 0.0))))
    route_ref[...] = route


def _mix_out_call(x2, ya, o, z, w, rows):
    tokens = x2.shape[0]
    tok = lambda width: pl.BlockSpec((rows, width), lambda i: (i, 0))
    full = lambda a: pl.BlockSpec(a.shape, lambda i: (0,) * a.ndim)
    consts = (w['gdn_norm_w'], w['w_out_a'], w['w_out_b'], w['ffn_norm_w'], w['w_route_hi'],
              w['w_route_lo'], w['b_route'])
    return pl.pallas_call(
        _mix_out_kernel,
        grid=(tokens // rows,),
        in_specs=[tok(D_MODEL), tok(CONV_CH), tok(GDN_W), tok(GDN_W)] + [full(a) for a in consts],
        out_specs=(tok(D_MODEL), tok(D_MODEL // 2), tok(LANES)),
        out_shape=(jax.ShapeDtypeStruct((tokens, D_MODEL), F32),
                   jax.ShapeDtypeStruct((tokens, D_MODEL // 2), jnp.uint32),
                   jax.ShapeDtypeStruct((tokens, LANES), F32)),
        compiler_params=pltpu.CompilerParams(
            dimension_semantics=("arbitrary",), vmem_limit_bytes=VMEM_LIMIT),
        name="mix_out",
    )(x2, ya, o, z, *consts)


def _sc_row_move(table, idx, out_rows, chunk, scatter, name):
    n = idx.shape[0]
    n_src, width = table.shape
    workers = SC_CORES * SC_SUBCORES
    per_w = n // workers
    assert n % workers == 0 and per_w % (2 * chunk) == 0
    assert n_src % per_w == 0 or not scatter
    pairs = per_w // (2 * chunk)
    mesh = plsc.VectorSubcoreMesh(core_axis_name="c", subcore_axis_name="s",
                                  num_cores=SC_CORES, num_subcores=SC_SUBCORES)

    def body(table_hbm, idx_hbm, out_hbm, idx_v, buf_a, buf_b, sem_ra, sem_rb, sem_wa, sem_wb):
        base = (lax.axis_index("s") * SC_CORES + lax.axis_index("c")) * per_w
        src_base = lax.rem(base, n_src)
        pltpu.sync_copy(idx_hbm.at[pl.ds(base, per_w)], idx_v)

        def read(c, buf, sem):
            off = pl.multiple_of(c * chunk, chunk)
            src = (table_hbm.at[pl.ds(src_base + off, chunk)] if scatter
                   else table_hbm.at[idx_v.at[pl.ds(off, chunk)]])
            return pltpu.make_async_copy(src, buf, sem)

        def write(c, buf, sem):
            off = pl.multiple_of(c * chunk, chunk)
            dst = (out_hbm.at[idx_v.at[pl.ds(off, chunk)]] if scatter
                   else out_hbm.at[pl.ds(base + off, chunk)])
            return pltpu.make_async_copy(buf, dst, sem)

        read(0, buf_a, sem_ra).start()

        @pl.loop(0, pairs)
        def _(j):
            ca = 2 * j
            cb = ca + 1
            read(cb, buf_b, sem_rb).start()
            read(ca, buf_a, sem_ra).wait()
            write(ca, buf_a, sem_wa).start()
            read(cb, buf_b, sem_rb).wait()
            write(cb, buf_b, sem_wb).start()
            write(ca, buf_a, sem_wa).wait()

            @pl.when(j + 1 < pairs)
            def _():
                read(ca + 2, buf_a, sem_ra).start()

            write(cb, buf_b, sem_wb).wait()

    return pl.kernel(
        body,
        out_type=jax.ShapeDtypeStruct((out_rows, width), table.dtype),
        mesh=mesh,
        scratch_types=[pltpu.VMEM((per_w,), jnp.int32),
                       pltpu.VMEM((chunk, width), table.dtype),
                       pltpu.VMEM((chunk, width), table.dtype),
                       pltpu.SemaphoreType.DMA, pltpu.SemaphoreType.DMA,
                       pltpu.SemaphoreType.DMA, pltpu.SemaphoreType.DMA],
        name=name,
    )(table, idx)


def _experts_kernel(bexp_ref, bvalid_ref, nused_ref, x_ref, wg_ref, wu_ref, wd_ref, y_ref,
                    wg_s, wu_s, wd_s):
    i = pl.program_id(0)
    n_used = nused_ref[0]
    first_of_expert = (i == 0) | (bexp_ref[i] != bexp_ref[jnp.maximum(i - 1, 0)])

    @pl.when((i < n_used) & first_of_expert)
    def _():
        wg_s[...] = wg_ref[...].astype(BF16)
        wu_s[...] = wu_ref[...].astype(BF16)
        wd_s[...] = wd_ref[...].astype(BF16)

    @pl.when(i < n_used)
    def _():
        row = lax.broadcasted_iota(jnp.int32, x_ref.shape, 0)
        packed = jnp.where(row < bvalid_ref[i], x_ref[...], jnp.uint32(0))
        x_lo = pltpu.bitcast(packed << 16, F32).astype(BF16)
        x_hi = pltpu.bitcast(packed & jnp.uint32(0xFFFF0000), F32).astype(BF16)
        xb = jnp.concatenate([x_lo, x_hi], axis=1)
        gate = _dot(xb, wg_s[...])
        hid = (gate * _sigmoid(gate)) * _dot(xb, wu_s[...])
        y_ref[...] = _dot(hid.astype(BF16), wd_s[...])

    @pl.when(i >= n_used)
    def _():
        y_ref[...] = jnp.zeros(y_ref.shape, F32)


def _experts_call(block_expert, block_valid, n_used, x_rows, w_gate, w_up, w_down, rows):
    n_blocks = block_expert.shape[0]
    wspec = lambda shape: pl.BlockSpec((None,) + shape, lambda i, be, bv, nu: (be[i], 0, 0))
    grid_spec = pltpu.PrefetchScalarGridSpec(
        num_scalar_prefetch=3,
        grid=(n_blocks,),
        in_specs=[pl.BlockSpec((rows, D_MODEL // 2), lambda i, be, bv, nu: (i, 0)),
                  wspec((D_MODEL, D_EXPERT)), wspec((D_MODEL, D_EXPERT)),
                  wspec((D_EXPERT, D_MODEL))],
        out_specs=pl.BlockSpec((rows, D_MODEL), lambda i, be, bv, nu: (i, 0)),
        scratch_shapes=[pltpu.VMEM((D_MODEL, D_EXPERT), BF16),
                        pltpu.VMEM((D_MODEL, D_EXPERT), BF16),
                        pltpu.VMEM((D_EXPERT, D_MODEL), BF16)],
    )
    return pl.pallas_call(
        _experts_kernel,
        grid_spec=grid_spec,
        out_shape=jax.ShapeDtypeStruct((n_blocks * rows, D_MODEL), F32),
        compiler_params=pltpu.CompilerParams(
            dimension_semantics=("arbitrary",), vmem_limit_bytes=VMEM_LIMIT),
        name="experts",
    )(block_expert, block_valid, n_used, x_rows, w_gate, w_up, w_down)


def _combine_kernel(y1_ref, y2_ref, h_ref, route_ref, fw_ref, out_ref):
    route = route_ref[...]
    moe = route[:, 2:3] * y1_ref[...] + route[:, 3:4] * y2_ref[...]
    h2 = h_ref[...] + moe
    out_ref[...] = h2 * lax.rsqrt(jnp.mean(h2 * h2, axis=-1, keepdims=True) + EPS) * fw_ref[...]


def _combine_call(y_tok, h1, route, final_w, rows):
    tokens = h1.shape[0]
    steps = tokens // rows
    tok = lambda width: pl.BlockSpec((rows, width), lambda i: (i, 0))
    return pl.pallas_call(
        _combine_kernel,
        grid=(steps,),
        in_specs=[tok(D_MODEL), pl.BlockSpec((rows, D_MODEL), lambda i: (i + steps, 0)),
                  tok(D_MODEL), tok(LANES), pl.BlockSpec((1, D_MODEL), lambda i: (0, 0))],
        out_specs=tok(D_MODEL),
        out_shape=jax.ShapeDtypeStruct((tokens, D_MODEL), F32),
        compiler_params=pltpu.CompilerParams(
            dimension_semantics=("arbitrary",), vmem_limit_bytes=VMEM_LIMIT),
        name="combine",
    )(y_tok, y_tok, h1, route, final_w)


def _dispatch_tables(route, tokens, rows):
    n_assign = tokens * TOP_K
    n_blocks = (n_assign + N_EXPERTS * (rows - 1)) // rows
    e1 = route[:, 0].astype(jnp.int32)
    e2 = route[:, 1].astype(jnp.int32)
    flat_e = jnp.concatenate([e1, e2])
    onehot = (flat_e[:, None] == jnp.arange(N_EXPERTS, dtype=jnp.int32)[None, :]).astype(jnp.int32)
    ranks = jnp.cumsum(onehot, axis=0) - onehot
    rank = jnp.sum(ranks * onehot, axis=1)
    counts = jnp.sum(onehot, axis=0)
    padded = (counts + rows - 1) // rows * rows
    pad_end = jnp.cumsum(padded)
    pad_start = pad_end - padded
    dest = (pad_start[flat_e] + rank).astype(jnp.int32)
    block_start = jnp.arange(n_blocks, dtype=jnp.int32) * rows
    block_expert = jnp.minimum(jnp.searchsorted(pad_end, block_start, side='right'),
                               N_EXPERTS - 1).astype(jnp.int32)
    block_valid = jnp.clip(pad_start[block_expert] + counts[block_expert] - block_start, 0, rows)
    n_used = (pad_end[-1] // rows).astype(jnp.int32).reshape(1)
    return dest, block_expert, block_valid.astype(jnp.int32), n_used, n_blocks


def _prepare_weights(mix_norm_w, w_in, conv_mix_w, conv_mix_norm_w, qkv_conv_w, a_log, dt_bias,
                     gdn_norm_w, w_out, ffn_norm_w, w_group, b_group, w_router, b_router):
    pad_lanes = lambda v: jnp.pad(v.reshape(1, -1), ((0, 0), (0, LANES - v.size)))
    c3 = 3 * CONV_CH
    w_ab = jnp.pad(w_in[:, c3 + 4 * GDN_W:], ((0, 0), (0, LANES - 2 * HEADS)))
    grp = jnp.arange(CONV_CH) // CONV_GROUP_W
    gmat = jnp.where(grp[:, None] == grp[None, :], 1.0 / CONV_GROUP_W, 0.0).astype(BF16)
    w_route = jnp.concatenate([w_group, w_router.reshape(D_MODEL, N_EXPERTS)], axis=1)
    w_route = jnp.pad(w_route, ((0, 0), (0, LANES - w_route.shape[1])))
    w_route_hi = w_route.astype(BF16)
    return dict(
        mix_norm_w=mix_norm_w.reshape(1, -1),
        w_a=w_in[:, :c3].astype(BF16),
        w_qkv=w_in[:, c3:c3 + 3 * GDN_W].astype(BF16),
        w_z=w_in[:, c3 + 3 * GDN_W:c3 + 4 * GDN_W].astype(BF16),
        w_ab=w_ab.astype(BF16),
        conv_mix_w=conv_mix_w,
        conv_mix_norm_w=conv_mix_norm_w.reshape(1, -1),
        gmat=gmat,
        qkv_conv_w=qkv_conv_w,
        a_log=pad_lanes(a_log),
        dt_bias=pad_lanes(dt_bias),
        gdn_norm_w=gdn_norm_w.reshape(1, -1),
        w_out_a=w_out[:CONV_CH].astype(BF16),
        w_out_b=w_out[CONV_CH:].astype(BF16),
        ffn_norm_w=ffn_norm_w.reshape(1, -1),
        w_route_hi=w_route_hi,
        w_route_lo=(w_route - w_route_hi.astype(F32)).astype(BF16),
        b_route=pad_lanes(jnp.concatenate([b_group, b_router.reshape(-1)])),
    )


def _tile(n, preferred):
    return preferred if n % preferred == 0 else n


def kernel(x, meta_tokens, mix_norm_w, w_in, conv_mix_w, conv_mix_norm_w, qkv_conv_w, a_log,
           dt_bias, gdn_norm_w, w_out, ffn_norm_w, w_group, b_group, w_router, b_router, w_gate,
           w_up, w_down, final_norm_w):
    assert mix_norm_w.shape[0] == 1, "single-layer kernel"
    batch, seq, _ = x.shape
    assert seq % CHUNK == 0
    w = _prepare_weights(mix_norm_w[0], w_in[0], conv_mix_w[0], conv_mix_norm_w[0], qkv_conv_w[0],
                         a_log[0], dt_bias[0], gdn_norm_w[0], w_out[0], ffn_norm_w[0], w_group[0],
                         b_group[0], w_router[0], b_router[0])

    prefix = jnp.concatenate([jnp.zeros((CHUNK - N_META, D_MODEL), x.dtype),
                              meta_tokens.astype(x.dtype)], axis=0)[None]
    zero_cu = jnp.zeros((HIST, CONV_CH), F32)
    zero_qkv = jnp.zeros((HIST, 3 * GDN_W), F32)
    _, pq, pk, pv, _, pgb, tail_cu, tail_qkv = _proj_call(prefix, zero_cu, zero_qkv, w, CHUNK)

    ya, q, k, v, z, gb, _, _ = _proj_call(x, tail_cu[0], tail_qkv[0], w, _tile(seq, PROJ_ROWS))
    o = _gdn_call(pq, pk, pv, pgb, q, k, v, gb, _tile(seq // CHUNK, GDN_CHUNKS))

    tokens = batch * seq
    flat = lambda a: a.reshape(tokens, a.shape[-1])
    h1, xn2, route = _mix_out_call(flat(x), flat(ya), flat(o), flat(z), w, _tile(tokens, MIX_ROWS))

    dest, block_expert, block_valid, n_used, n_blocks = _dispatch_tables(route, tokens, EXPERT_ROWS)
    x_rows = _sc_row_move(xn2, dest, n_blocks * EXPERT_ROWS, SC_DISPATCH_CHUNK, True,
                          "dispatch_scatter")
    y_rows = _experts_call(block_expert, block_valid, n_used, x_rows, w_gate[0], w_up[0],
                           w_down[0], EXPERT_ROWS)
    y_tok = _sc_row_move(y_rows, dest, TOP_K * tokens, SC_COMBINE_CHUNK, False, "combine_gather")
    out = _combine_call(y_tok, h1, route, final_norm_w.reshape(1, -1), _tile(tokens, COMBINE_ROWS))
    return out.reshape(batch, seq, D_MODEL)
```

```python
import functools

import jax
import jax.numpy as jnp
from jax import lax
from jax.experimental import pallas as pl
from jax.experimental.pallas import tpu as pltpu
from jax.experimental.pallas import tpu_sc as plsc

F32 = jnp.float32
BF16 = jnp.bfloat16
EPS = 1e-6

D_MODEL = 1024
N_META = 16
CONV_CH = 512
CONV_GROUP_W = 64
HEADS = 4
HEAD_DIM = 128
GDN_W = HEADS * HEAD_DIM
CHUNK = 64
N_GROUPS = 4
EXPERTS_PER_GROUP = 8
N_EXPERTS = N_GROUPS * EXPERTS_PER_GROUP
TOP_K = 2
D_EXPERT = 512
LANES = 128
ROUTE_E1, ROUTE_E2, ROUTE_W1, ROUTE_W2, ROUTE_RANK1, ROUTE_RANK2 = range(6)
HIST = 8

PROJ_ROWS = 512
PROJ_COLS = 256
GDN_CHUNKS = 8
GDN_GROUP = 2
MIX_ROWS = 512
EXPERT_ROWS = 256
COMBINE_ROWS = 256
VMEM_LIMIT = 56 * 1024 * 1024
SC_CORES = 2
SC_SUBCORES = 16
SC_DISPATCH_CHUNK = 64
SC_COMBINE_CHUNK = 32


def _dot(a, b):
    return jnp.dot(a, b, preferred_element_type=F32)


def _dot_nt(a, b):
    return lax.dot_general(a, b, (((1,), (1,)), ((), ())), preferred_element_type=F32)


def _dot_tn(a, b):
    return lax.dot_general(a, b, (((0,), (0,)), ((), ())), preferred_element_type=F32)


def _split_bf16(x):
    hi = x.astype(BF16)
    lo = (x - hi.astype(F32)).astype(BF16)
    return hi, lo


def _sigmoid(x):
    return 1.0 / (1.0 + jnp.exp(-x))


def _proj_kernel(x_ref, hcu_ref, hqkv_ref, nw_ref, wa_ref, wq_ref, wz_ref, wab_ref, cmw_ref,
                 cmn_ref, gmat_ref, qcw_ref, alog_ref, dtb_ref,
                 ya_ref, q_ref, k_ref, v_ref, z_ref, gb_ref, tcu_ref, tqkv_ref,
                 cu_s, qkv_s, xn_s):
    rows = x_ref.shape[0]

    @pl.when(pl.program_id(1) == 0)
    def _():
        cu_s[0:HIST, :] = hcu_ref[...]
        qkv_s[0:HIST, :] = hqkv_ref[...]

    x = x_ref[...]
    ms = jnp.mean(x * x, axis=-1, keepdims=True)
    xn_s[...] = (x * lax.rsqrt(ms + EPS) * nw_ref[...]).astype(BF16)

    def causal_conv(buf, cur, w_ref, cols, tail_ref):
        taps = w_ref.shape[0]
        buf[HIST:HIST + rows, cols] = cur
        acc = buf[pl.ds(HIST - taps + 1, rows), cols] * w_ref[0:1, cols]
        for j in range(1, taps - 1):
            acc = acc + buf[pl.ds(HIST - taps + 1 + j, rows), cols] * w_ref[j:j + 1, cols]
        acc = acc + cur * w_ref[taps - 1:taps, cols]
        tail = buf[rows:rows + HIST, cols]
        buf[0:HIST, cols] = tail
        tail_ref[:, cols] = tail
        return acc

    def mixer_a_tail(i, pa):
        cols = slice(i * PROJ_COLS, (i + 1) * PROJ_COLS)
        cu = pa[:, PROJ_COLS:2 * PROJ_COLS] * pa[:, 2 * PROJ_COLS:]
        ya = pa[:, :PROJ_COLS] * causal_conv(cu_s, cu, cmw_ref, cols, tcu_ref)
        sq_hi, sq_lo = _split_bf16(ya * ya)
        msg = _dot(sq_hi, gmat_ref[...]) + _dot(sq_lo, gmat_ref[...])
        ya_ref[:, cols] = (ya * lax.rsqrt(msg + EPS) * cmn_ref[:, cols]).astype(BF16)

    heads_per_chunk = PROJ_COLS // HEAD_DIM

    def qkv_tail(i, pq):
        cols = slice(i * PROJ_COLS, (i + 1) * PROJ_COLS)
        c = causal_conv(qkv_s, pq, qcw_ref, cols, tqkv_ref)
        c = c * _sigmoid(c)
        part, first_head = divmod(i * heads_per_chunk, HEADS)
        for j in range(heads_per_chunk):
            ch = c[:, j * HEAD_DIM:(j + 1) * HEAD_DIM]
            sl = slice((first_head + j) * HEAD_DIM, (first_head + j + 1) * HEAD_DIM)
            if part == 0:
                norm = lax.rsqrt(jnp.sum(ch * ch, axis=-1, keepdims=True) + EPS)
                q_ref[:, sl] = ch * norm * (HEAD_DIM ** -0.5)
            elif part == 1:
                k_ref[:, sl] = ch * lax.rsqrt(jnp.sum(ch * ch, axis=-1, keepdims=True) + EPS)
            else:
                v_ref[:, sl] = ch

    def z_tail(pz):
        z_ref[...] = pz

    def decay_beta_tail(ab):
        sp_in = ab + dtb_ref[...]
        softplus = jnp.maximum(sp_in, 0.0) + jnp.log1p(jnp.exp(-jnp.abs(sp_in)))
        g = -jnp.exp(alog_ref[...]) * softplus
        lane = lax.broadcasted_iota(jnp.int32, ab.shape, 1)
        gb_ref[...] = jnp.where(lane < HEADS, g, jnp.where(lane < 2 * HEADS, _sigmoid(ab), 0.0))

    stages = []
    for i in range(CONV_CH // PROJ_COLS):
        w_cols = slice(3 * i * PROJ_COLS, 3 * (i + 1) * PROJ_COLS)
        stages.append((functools.partial(lambda s: _dot(xn_s[...], wa_ref[:, s]), w_cols),
                       functools.partial(mixer_a_tail, i)))
    for i in range(3 * GDN_W // PROJ_COLS):
        w_cols = slice(i * PROJ_COLS, (i + 1) * PROJ_COLS)
        stages.append((functools.partial(lambda s: _dot(xn_s[...], wq_ref[:, s]), w_cols),
                       functools.partial(qkv_tail, i)))
    stages.append((lambda: _dot(xn_s[...], wz_ref[...]), z_tail))
    stages.append((lambda: _dot(xn_s[...], wab_ref[...]), decay_beta_tail))
    pending = None
    for matmul, tail in stages:
        res = matmul()
        if pending is not None:
            pending()
        pending = functools.partial(tail, res)
    pending()


def _proj_call(x3, hist_cu, hist_qkv, w, rows):
    nb, seq, _ = x3.shape
    nt = seq // rows
    tok = lambda width: pl.BlockSpec((None, rows, width), lambda b, t: (b, t, 0))
    full = lambda a: pl.BlockSpec(a.shape, lambda b, t: (0,) * a.ndim)
    tail = lambda width: pl.BlockSpec((None, HIST, width), lambda b, t: (b, 0, 0))
    consts = (hist_cu, hist_qkv, w['mix_norm_w'], w['w_a'], w['w_qkv'], w['w_z'], w['w_ab'],
              w['conv_mix_w'], w['conv_mix_norm_w'], w['gmat'], w['qkv_conv_w'], w['a_log'],
              w['dt_bias'])
    out_shape = (
        jax.ShapeDtypeStruct((nb, seq, CONV_CH), BF16),
        jax.ShapeDtypeStruct((nb, seq, GDN_W), F32),
        jax.ShapeDtypeStruct((nb, seq, GDN_W), F32),
        jax.ShapeDtypeStruct((nb, seq, GDN_W), F32),
        jax.ShapeDtypeStruct((nb, seq, GDN_W), F32),
        jax.ShapeDtypeStruct((nb, seq, LANES), F32),
        jax.ShapeDtypeStruct((nb, HIST, CONV_CH), F32),
        jax.ShapeDtypeStruct((nb, HIST, 3 * GDN_W), F32),
    )
    return pl.pallas_call(
        _proj_kernel,
        grid=(nb, nt),
        in_specs=[tok(D_MODEL)] + [full(a) for a in consts],
        out_specs=(tok(CONV_CH), tok(GDN_W), tok(GDN_W), tok(GDN_W), tok(GDN_W), tok(LANES),
                   tail(CONV_CH), tail(3 * GDN_W)),
        out_shape=out_shape,
        scratch_shapes=[pltpu.VMEM((rows + HIST, CONV_CH), F32),
                        pltpu.VMEM((rows + HIST, 3 * GDN_W), F32),
                        pltpu.VMEM((rows, D_MODEL), BF16)],
        compiler_params=pltpu.CompilerParams(
            dimension_semantics=("arbitrary", "arbitrary"), vmem_limit_bytes=VMEM_LIMIT),
        name="proj",
    )(x3, *consts)


def _chunk_masks():
    row = lax.broadcasted_iota(jnp.int32, (CHUNK, CHUNK), 0)
    col = lax.broadcasted_iota(jnp.int32, (CHUNK, CHUNK), 1)
    incl = row >= col
    strict = row > col
    levels = []
    n = 1
    while n < CHUNK:
        levels.append((row // (2 * n) == col // (2 * n)) & ((row // n) % 2 == 1) & ((col // n) % 2 == 0))
        n *= 2
    return incl, strict, levels


def _chunk_cumsum(gb_blk, incl):
    tri = incl.astype(BF16)
    hi, lo = _split_bf16(gb_blk)
    return _dot(tri, hi) + _dot(tri, lo)


def _chunk_transforms(chains, masks, state_only):
    incl, strict, levels = masks
    eye = (lax.broadcasted_iota(jnp.int32, (CHUNK, CHUNK), 0)
           == lax.broadcasted_iota(jnp.int32, (CHUNK, CHUNK), 1)).astype(F32)
    decay = [jnp.exp(jnp.where(incl, gc_col - gc_row, -jnp.inf))
             for (_, _, _, _, gc_col, gc_row, _) in chains]
    kb = [kh * beta for (_, kh, _, beta, _, _, _) in chains]
    k_bf = [kh.astype(BF16) for (_, kh, _, _, _, _, _) in chains]
    a_mat = [jnp.where(strict, _dot_nt(kb_i.astype(BF16), k_i) * d_i, 0.0)
             for kb_i, k_i, d_i in zip(kb, k_bf, decay)]
    t_inv = [eye - jnp.where(levels[0], a_i, 0.0) for a_i in a_mat]
    for lvl in levels[1:]:
        t_bf = [t_i.astype(BF16) for t_i in t_inv]
        m1 = [_dot(jnp.where(lvl, a_i, 0.0).astype(BF16), t_i) for a_i, t_i in zip(a_mat, t_bf)]
        t_inv = [t_i - _dot(tb_i, m_i.astype(BF16)) for t_i, tb_i, m_i in zip(t_inv, t_bf, m1)]
    rhs = [jnp.concatenate([vh * beta, kb_i * jnp.exp(gc_col)], axis=1)
           for (_, _, vh, beta, gc_col, _, _), kb_i in zip(chains, kb)]
    uw = [_dot(t_i.astype(BF16), r_i.astype(BF16)).astype(BF16)
          for t_i, r_i in zip(t_inv, rhs)]
    kd = [kh * jnp.exp(g_last - gc_col) for (_, kh, _, _, gc_col, _, g_last) in chains]
    pn = [_dot_tn(kd_i.astype(BF16), uw_i) for kd_i, uw_i in zip(kd, uw)]
    if state_only:
        return [pn_i[:, :HEAD_DIM] for pn_i in pn]
    intra = [jnp.where(incl, _dot_nt(qh.astype(BF16), k_i) * d_i, 0.0)
             for (qh, _, _, _, _, _, _), k_i, d_i in zip(chains, k_bf, decay)]
    iuw = [_dot(in_i.astype(BF16), uw_i) for in_i, uw_i in zip(intra, uw)]
    out = []
    for (qh, _, _, _, gc_col, _, g_last), pn_i, iuw_i in zip(chains, pn, iuw):
        q_part = qh * jnp.exp(gc_col) - iuw_i[:, HEAD_DIM:]
        out.append((q_part, pn_i[:, HEAD_DIM:], iuw_i[:, :HEAD_DIM], pn_i[:, :HEAD_DIM],
                    jnp.exp(g_last)))
    return out


def _gdn_kernel(pq_ref, pk_ref, pv_ref, pgb_ref, q_ref, k_ref, v_ref, gb_ref, o_ref,
                s_s, qp_s, op_s, n_s, a_s):
    nb = q_ref.shape[0]
    n_chunks = q_ref.shape[1] // CHUNK
    group = GDN_GROUP if n_chunks % GDN_GROUP == 0 else 1
    masks = _chunk_masks()

    def chains_of(gb_blk, q_blk, k_blk, v_blk):
        gc = _chunk_cumsum(gb_blk, masks[0])
        gc_t = gc.T
        res = []
        for h in range(HEADS):
            sl = slice(h * HEAD_DIM, (h + 1) * HEAD_DIM)
            res.append((q_blk(sl), k_blk(sl), v_blk(sl), gb_blk[:, HEADS + h:HEADS + h + 1],
                        gc[:, h:h + 1], gc_t[h:h + 1, :CHUNK], gc[CHUNK - 1:CHUNK, h:h + 1]))
        return res

    @pl.when(pl.program_id(0) == 0)
    def _():
        chains = chains_of(pgb_ref[...], lambda sl: pq_ref[:, sl], lambda sl: pk_ref[:, sl],
                           lambda sl: pv_ref[:, sl])
        for h, n_mat in enumerate(_chunk_transforms(chains, masks, True)):
            for b in range(nb):
                s_s[b * HEADS + h] = n_mat

    def transform_group(gi, carry):
        chains, where = [], []
        for cc in range(group):
            c = gi * group + cc
            rows = pl.ds(pl.multiple_of(c * CHUNK, CHUNK), CHUNK)
            for b in range(nb):
                chains += chains_of(gb_ref[b, rows, :], lambda sl: q_ref[b, rows, sl],
                                    lambda sl: k_ref[b, rows, sl], lambda sl: v_ref[b, rows, sl])
                where += [(c, b * HEADS + h) for h in range(HEADS)]
        for (c, ch), (q_part, p_mat, o_part, n_mat, a) in zip(
                where, _chunk_transforms(chains, masks, False)):
            qp_s[c, ch, 0:CHUNK, :] = q_part.astype(BF16)
            qp_s[c, ch, CHUNK:, :] = p_mat.astype(BF16)
            op_s[c, ch] = o_part
            n_s[c, ch] = n_mat
            a_s[c, ch] = jnp.broadcast_to(a, (8, HEAD_DIM))
        return carry

    lax.fori_loop(0, n_chunks // group, transform_group, 0)

    def scan_chunk(c, carry):
        r0 = pl.multiple_of(c * CHUNK, CHUNK)
        for b in range(nb):
            for h in range(HEADS):
                ch = b * HEADS + h
                s = s_s[ch]
                r = _dot(qp_s[c, ch], s.astype(BF16))
                o_ref[b, pl.ds(r0, CHUNK), h * HEAD_DIM:(h + 1) * HEAD_DIM] = r[:CHUNK] + op_s[c, ch]
                s_s[ch] = a_s[c, ch][0:1, :] * s - r[CHUNK:] + n_s[c, ch]
        return carry

    lax.fori_loop(0, n_chunks, scan_chunk, 0)


def _gdn_call(pq, pk, pv, pgb, q, k, v, gb, chunks_per_step):
    nb, seq, _ = q.shape
    rows = chunks_per_step * CHUNK
    steps = seq // rows
    tok = lambda width: pl.BlockSpec((nb, rows, width), lambda i: (0, i, 0))
    pre = lambda width: pl.BlockSpec((None, CHUNK, width), lambda i: (0, 0, 0))
    nch = nb * HEADS
    return pl.pallas_call(
        _gdn_kernel,
        grid=(steps,),
        in_specs=[pre(GDN_W), pre(GDN_W), pre(GDN_W), pre(LANES),
                  tok(GDN_W), tok(GDN_W), tok(GDN_W), tok(LANES)],
        out_specs=tok(GDN_W),
        out_shape=jax.ShapeDtypeStruct((nb, seq, GDN_W), F32),
        scratch_shapes=[
            pltpu.VMEM((nch, HEAD_DIM, HEAD_DIM), F32),
            pltpu.VMEM((chunks_per_step, nch, CHUNK + HEAD_DIM, HEAD_DIM), BF16),
            pltpu.VMEM((chunks_per_step, nch, CHUNK, HEAD_DIM), F32),
            pltpu.VMEM((chunks_per_step, nch, HEAD_DIM, HEAD_DIM), F32),
            pltpu.VMEM((chunks_per_step, nch, 8, HEAD_DIM), F32),
        ],
        compiler_params=pltpu.CompilerParams(
            dimension_semantics=("arbitrary",), vmem_limit_bytes=VMEM_LIMIT),
        name="gdn",
    )(pq, pk, pv, pgb, q, k, v, gb)


def _mix_out_kernel(x_ref, ya_ref, o_ref, z_ref, gnw_ref, woa_ref, wob_ref, fnw_ref, wrh_ref,
                    wrl_ref, br_ref, h_ref, xn_ref, route_ref, counts_ref, cnt_s):
    yb = []
    for h in range(HEADS):
        sl = slice(h * HEAD_DIM, (h + 1) * HEAD_DIM)
        oh = o_ref[:, sl]
        zh = z_ref[:, sl]
        on = oh * lax.rsqrt(jnp.mean(oh * oh, axis=-1, keepdims=True) + EPS) * gnw_ref[...]
        yb.append((on * (zh * _sigmoid(zh))).astype(BF16))
    yb = jnp.concatenate(yb, axis=1)
    h1 = x_ref[...] + (_dot(ya_ref[...], woa_ref[...]) + _dot(yb, wob_ref[...]))
    h_ref[...] = h1
    xn = h1 * lax.rsqrt(jnp.mean(h1 * h1, axis=-1, keepdims=True) + EPS) * fnw_ref[...]
    x_hi, x_lo = _split_bf16(xn)
    bits = pltpu.bitcast(x_hi.astype(F32), jnp.uint32)
    half = D_MODEL // 2
    xn_ref[...] = (bits[:, :half] >> 16) | (bits[:, half:] & jnp.uint32(0xFFFF0000))
    logits = (_dot(x_hi, wrh_ref[...]) + _dot(x_lo, wrh_ref[...]) + _dot(x_hi, wrl_ref[...])
              + br_ref[...])
    lane = lax.broadcasted_iota(jnp.int32, logits.shape, 1)
    neg = -jnp.inf
    big = jnp.int32(1 << 20)

    def argmax_first(vals):
        m = jnp.max(vals, axis=-1, keepdims=True)
        idx = jnp.min(jnp.where(vals == m, lane, big), axis=-1, keepdims=True)
        return m, idx

    grp = jnp.where(lane < N_GROUPS, logits, neg)
    g_max, g_sel = argmax_first(grp)
    p_grp = 1.0 / jnp.sum(jnp.exp(grp - g_max), axis=-1, keepdims=True)
    lo_lane = N_GROUPS + g_sel * EXPERTS_PER_GROUP
    ex = jnp.where((lane >= lo_lane) & (lane < lo_lane + EXPERTS_PER_GROUP), logits, neg)
    m1, i1 = argmax_first(ex)
    m2, i2 = argmax_first(jnp.where(lane == i1, neg, ex))
    e2 = jnp.exp(m2 - m1)
    w1 = 1.0 / (1.0 + e2) * p_grp
    w2 = e2 / (1.0 + e2) * p_grp
    @pl.when(pl.program_id(0) == 0)
    def _():
        cnt_s[...] = jnp.zeros(cnt_s.shape, F32)

    rows = logits.shape[0]
    oh1 = (lane == i1 - N_GROUPS).astype(F32)
    oh2 = (lane == i2 - N_GROUPS).astype(F32)
    oh = oh1 + oh2
    earlier = (lax.broadcasted_iota(jnp.int32, (rows, rows), 0)
               > lax.broadcasted_iota(jnp.int32, (rows, rows), 1)).astype(BF16)
    before = _dot(earlier, oh.astype(BF16)) + cnt_s[...]
    rank1 = jnp.sum(before * oh1, axis=-1, keepdims=True)
    rank2 = jnp.sum(before * oh2, axis=-1, keepdims=True)
    counts = cnt_s[...] + jnp.sum(oh, axis=0, keepdims=True)
    cnt_s[...] = counts
    counts_ref[...] = counts

    route = jnp.zeros(logits.shape, F32)
    for k, val in ((ROUTE_E1, (i1 - N_GROUPS).astype(F32)), (ROUTE_E2, (i2 - N_GROUPS).astype(F32)),
                   (ROUTE_W1, w1), (ROUTE_W2, w2), (ROUTE_RANK1, rank1), (ROUTE_RANK2, rank2)):
        route = jnp.where(lane == k, val, route)
    route_ref[...] = route


def _mix_out_call(x2, ya, o, z, w, rows):
    tokens = x2.shape[0]
    tok = lambda width: pl.BlockSpec((rows, width), lambda i: (i, 0))
    full = lambda a: pl.BlockSpec(a.shape, lambda i: (0,) * a.ndim)
    consts = (w['gdn_norm_w'], w['w_out_a'], w['w_out_b'], w['ffn_norm_w'], w['w_route_hi'],
              w['w_route_lo'], w['b_route'])
    return pl.pallas_call(
        _mix_out_kernel,
        grid=(tokens // rows,),
        in_specs=[tok(D_MODEL), tok(CONV_CH), tok(GDN_W), tok(GDN_W)] + [full(a) for a in consts],
        out_specs=(tok(D_MODEL), tok(D_MODEL // 2), tok(LANES),
                   pl.BlockSpec((1, LANES), lambda i: (0, 0))),
        out_shape=(jax.ShapeDtypeStruct((tokens, D_MODEL), F32),
                   jax.ShapeDtypeStruct((tokens, D_MODEL // 2), jnp.uint32),
                   jax.ShapeDtypeStruct((tokens, LANES), F32),
                   jax.ShapeDtypeStruct((1, LANES), F32)),
        scratch_shapes=[pltpu.VMEM((1, LANES), F32)],
        compiler_params=pltpu.CompilerParams(
            dimension_semantics=("arbitrary",), vmem_limit_bytes=VMEM_LIMIT),
        name="mix_out",
    )(x2, ya, o, z, *consts)


def _sc_row_move(table, idx, out_rows, chunk, scatter, name):
    n = idx.shape[0]
    n_src, width = table.shape
    workers = SC_CORES * SC_SUBCORES
    per_w = n // workers
    assert n % workers == 0 and per_w % (2 * chunk) == 0
    assert n_src % per_w == 0 or not scatter
    pairs = per_w // (2 * chunk)
    mesh = plsc.VectorSubcoreMesh(core_axis_name="c", subcore_axis_name="s",
                                  num_cores=SC_CORES, num_subcores=SC_SUBCORES)

    def body(table_hbm, idx_hbm, out_hbm, idx_v, buf_a, buf_b, sem_ra, sem_rb, sem_wa, sem_wb):
        base = (lax.axis_index("s") * SC_CORES + lax.axis_index("c")) * per_w
        src_base = lax.rem(base, n_src)
        pltpu.sync_copy(idx_hbm.at[pl.ds(base, per_w)], idx_v)

        def read(c, buf, sem):
            off = pl.multiple_of(c * chunk, chunk)
            src = (table_hbm.at[pl.ds(src_base + off, chunk)] if scatter
                   else table_hbm.at[idx_v.at[pl.ds(off, chunk)]])
            return pltpu.make_async_copy(src, buf, sem)

        def write(c, buf, sem):
            off = pl.multiple_of(c * chunk, chunk)
            dst = (out_hbm.at[idx_v.at[pl.ds(off, chunk)]] if scatter
                   else out_hbm.at[pl.ds(base + off, chunk)])
            return pltpu.make_async_copy(buf, dst, sem)

        read(0, buf_a, sem_ra).start()

        @pl.loop(0, pairs)
        def _(j):
            ca = 2 * j
            cb = ca + 1
            read(cb, buf_b, sem_rb).start()
            read(ca, buf_a, sem_ra).wait()
            write(ca, buf_a, sem_wa).start()
            read(cb, buf_b, sem_rb).wait()
            write(cb, buf_b, sem_wb).start()
            write(ca, buf_a, sem_wa).wait()

            @pl.when(j + 1 < pairs)
            def _():
                read(ca + 2, buf_a, sem_ra).start()

            write(cb, buf_b, sem_wb).wait()

    return pl.kernel(
        body,
        out_type=jax.ShapeDtypeStruct((out_rows, width), table.dtype),
        mesh=mesh,
        scratch_types=[pltpu.VMEM((per_w,), jnp.int32),
                       pltpu.VMEM((chunk, width), table.dtype),
                       pltpu.VMEM((chunk, width), table.dtype),
                       pltpu.SemaphoreType.DMA, pltpu.SemaphoreType.DMA,
                       pltpu.SemaphoreType.DMA, pltpu.SemaphoreType.DMA],
        name=name,
    )(table, idx)


def _experts_kernel(bexp_ref, bvalid_ref, nused_ref, x_ref, wg_ref, wu_ref, wd_ref, y_ref,
                    wg_s, wu_s, wd_s):
    i = pl.program_id(0)
    n_used = nused_ref[0]
    first_of_expert = (i == 0) | (bexp_ref[i] != bexp_ref[jnp.maximum(i - 1, 0)])

    @pl.when((i < n_used) & first_of_expert)
    def _():
        wg_s[...] = wg_ref[...].astype(BF16)
        wu_s[...] = wu_ref[...].astype(BF16)
        wd_s[...] = wd_ref[...].astype(BF16)

    @pl.when(i < n_used)
    def _():
        row = lax.broadcasted_iota(jnp.int32, x_ref.shape, 0)
        packed = jnp.where(row < bvalid_ref[i], x_ref[...], jnp.uint32(0))
        x_lo = pltpu.bitcast(packed << 16, F32).astype(BF16)
        x_hi = pltpu.bitcast(packed & jnp.uint32(0xFFFF0000), F32).astype(BF16)
        xb = jnp.concatenate([x_lo, x_hi], axis=1)
        gate = _dot(xb, wg_s[...])
        hid = (gate * _sigmoid(gate)) * _dot(xb, wu_s[...])
        y_ref[...] = _dot(hid.astype(BF16), wd_s[...])

    @pl.when(i >= n_used)
    def _():
        y_ref[...] = jnp.zeros(y_ref.shape, F32)


def _experts_call(block_expert, block_valid, n_used, x_rows, w_gate, w_up, w_down, rows):
    n_blocks = block_expert.shape[0]
    wspec = lambda shape: pl.BlockSpec((None,) + shape, lambda i, be, bv, nu: (be[i], 0, 0))
    grid_spec = pltpu.PrefetchScalarGridSpec(
        num_scalar_prefetch=3,
        grid=(n_blocks,),
        in_specs=[pl.BlockSpec((rows, D_MODEL // 2), lambda i, be, bv, nu: (i, 0)),
                  wspec((D_MODEL, D_EXPERT)), wspec((D_MODEL, D_EXPERT)),
                  wspec((D_EXPERT, D_MODEL))],
        out_specs=pl.BlockSpec((rows, D_MODEL), lambda i, be, bv, nu: (i, 0)),
        scratch_shapes=[pltpu.VMEM((D_MODEL, D_EXPERT), BF16),
                        pltpu.VMEM((D_MODEL, D_EXPERT), BF16),
                        pltpu.VMEM((D_EXPERT, D_MODEL), BF16)],
    )
    return pl.pallas_call(
        _experts_kernel,
        grid_spec=grid_spec,
        out_shape=jax.ShapeDtypeStruct((n_blocks * rows, D_MODEL), F32),
        compiler_params=pltpu.CompilerParams(
            dimension_semantics=("arbitrary",), vmem_limit_bytes=VMEM_LIMIT),
        name="experts",
    )(block_expert, block_valid, n_used, x_rows, w_gate, w_up, w_down)


def _combine_kernel(y1_ref, y2_ref, h_ref, route_ref, fw_ref, out_ref):
    route = route_ref[...]
    moe = (route[:, ROUTE_W1:ROUTE_W1 + 1] * y1_ref[...]
           + route[:, ROUTE_W2:ROUTE_W2 + 1] * y2_ref[...])
    h2 = h_ref[...] + moe
    out_ref[...] = h2 * lax.rsqrt(jnp.mean(h2 * h2, axis=-1, keepdims=True) + EPS) * fw_ref[...]


def _combine_call(y_tok, h1, route, final_w, rows):
    tokens = h1.shape[0]
    steps = tokens // rows
    tok = lambda width: pl.BlockSpec((rows, width), lambda i: (i, 0))
    return pl.pallas_call(
        _combine_kernel,
        grid=(steps,),
        in_specs=[tok(D_MODEL), pl.BlockSpec((rows, D_MODEL), lambda i: (i + steps, 0)),
                  tok(D_MODEL), tok(LANES), pl.BlockSpec((1, D_MODEL), lambda i: (0, 0))],
        out_specs=tok(D_MODEL),
        out_shape=jax.ShapeDtypeStruct((tokens, D_MODEL), F32),
        compiler_params=pltpu.CompilerParams(
            dimension_semantics=("arbitrary",), vmem_limit_bytes=VMEM_LIMIT),
        name="combine",
    )(y_tok, y_tok, h1, route, final_w)


def _dispatch_tables(route, counts, tokens, rows):
    n_blocks = (tokens * TOP_K + N_EXPERTS * (rows - 1)) // rows
    counts = counts[0, :N_EXPERTS].astype(jnp.int32)
    padded = (counts + rows - 1) // rows * rows
    pad_end = jnp.cumsum(padded)
    pad_start = pad_end - padded
    experts = route[:, ROUTE_E1:ROUTE_E2 + 1].astype(jnp.int32)
    ranks = route[:, ROUTE_RANK1:ROUTE_RANK2 + 1].astype(jnp.int32)
    dest = (pad_start[experts] + ranks).T.reshape(-1)
    block_start = jnp.arange(n_blocks, dtype=jnp.int32) * rows
    block_expert = jnp.minimum(jnp.sum(block_start[:, None] >= pad_end[None, :], axis=1),
                               N_EXPERTS - 1).astype(jnp.int32)
    block_valid = jnp.clip(pad_start[block_expert] + counts[block_expert] - block_start, 0, rows)
    n_used = (pad_end[-1] // rows).astype(jnp.int32).reshape(1)
    return dest, block_expert, block_valid.astype(jnp.int32), n_used, n_blocks


def _prepare_weights(mix_norm_w, w_in, conv_mix_w, conv_mix_norm_w, qkv_conv_w, a_log, dt_bias,
                     gdn_norm_w, w_out, ffn_norm_w, w_group, b_group, w_router, b_router):
    pad_lanes = lambda v: jnp.pad(v.reshape(1, -1), ((0, 0), (0, LANES - v.size)))
    c3 = 3 * CONV_CH
    w_ab = jnp.pad(w_in[:, c3 + 4 * GDN_W:], ((0, 0), (0, LANES - 2 * HEADS)))
    grp = jnp.arange(PROJ_COLS) // CONV_GROUP_W
    gmat = jnp.where(grp[:, None] == grp[None, :], 1.0 / CONV_GROUP_W, 0.0).astype(BF16)
    w_route = jnp.concatenate([w_group, w_router.reshape(D_MODEL, N_EXPERTS)], axis=1)
    w_route = jnp.pad(w_route, ((0, 0), (0, LANES - w_route.shape[1])))
    w_route_hi = w_route.astype(BF16)
    return dict(
        mix_norm_w=mix_norm_w.reshape(1, -1),
        w_a=w_in[:, :c3].reshape(D_MODEL, 3, CONV_CH // PROJ_COLS, PROJ_COLS).transpose(0, 2, 1, 3)
        .reshape(D_MODEL, c3).astype(BF16),
        w_qkv=w_in[:, c3:c3 + 3 * GDN_W].astype(BF16),
        w_z=w_in[:, c3 + 3 * GDN_W:c3 + 4 * GDN_W].astype(BF16),
        w_ab=w_ab.astype(BF16),
        conv_mix_w=conv_mix_w,
        conv_mix_norm_w=conv_mix_norm_w.reshape(1, -1),
        gmat=gmat,
        qkv_conv_w=qkv_conv_w,
        a_log=pad_lanes(a_log),
        dt_bias=pad_lanes(dt_bias),
        gdn_norm_w=gdn_norm_w.reshape(1, -1),
        w_out_a=w_out[:CONV_CH].astype(BF16),
        w_out_b=w_out[CONV_CH:].astype(BF16),
        ffn_norm_w=ffn_norm_w.reshape(1, -1),
        w_route_hi=w_route_hi,
        w_route_lo=(w_route - w_route_hi.astype(F32)).astype(BF16),
        b_route=pad_lanes(jnp.concatenate([b_group, b_router.reshape(-1)])),
    )


def _tile(n, preferred):
    return preferred if n % preferred == 0 else n


def kernel(x, meta_tokens, mix_norm_w, w_in, conv_mix_w, conv_mix_norm_w, qkv_conv_w, a_log,
           dt_bias, gdn_norm_w, w_out, ffn_norm_w, w_group, b_group, w_router, b_router, w_gate,
           w_up, w_down, final_norm_w):
    assert mix_norm_w.shape[0] == 1, "single-layer kernel"
    batch, seq, _ = x.shape
    assert seq % CHUNK == 0
    w = _prepare_weights(mix_norm_w[0], w_in[0], conv_mix_w[0], conv_mix_norm_w[0], qkv_conv_w[0],
                         a_log[0], dt_bias[0], gdn_norm_w[0], w_out[0], ffn_norm_w[0], w_group[0],
                         b_group[0], w_router[0], b_router[0])

    prefix = jnp.concatenate([jnp.zeros((CHUNK - N_META, D_MODEL), x.dtype),
                              meta_tokens.astype(x.dtype)], axis=0)[None]
    zero_cu = jnp.zeros((HIST, CONV_CH), F32)
    zero_qkv = jnp.zeros((HIST, 3 * GDN_W), F32)
    _, pq, pk, pv, _, pgb, tail_cu, tail_qkv = _proj_call(prefix, zero_cu, zero_qkv, w, CHUNK)

    ya, q, k, v, z, gb, _, _ = _proj_call(x, tail_cu[0], tail_qkv[0], w, _tile(seq, PROJ_ROWS))
    o = _gdn_call(pq, pk, pv, pgb, q, k, v, gb, _tile(seq // CHUNK, GDN_CHUNKS))

    tokens = batch * seq
    flat = lambda a: a.reshape(tokens, a.shape[-1])
    h1, xn2, route, counts = _mix_out_call(flat(x), flat(ya), flat(o), flat(z), w,
                                           _tile(tokens, MIX_ROWS))

    dest, block_expert, block_valid, n_used, n_blocks = _dispatch_tables(route, counts, tokens,
                                                                         EXPERT_ROWS)
    x_rows = _sc_row_move(xn2, dest, n_blocks * EXPERT_ROWS, SC_DISPATCH_CHUNK, True,
                          "dispatch_scatter")
    y_rows = _experts_call(block_expert, block_valid, n_used, x_rows, w_gate[0], w_up[0],
                           w_down[0], EXPERT_ROWS)
    y_tok = _sc_row_move(y_rows, dest, TOP_K * tokens, SC_COMBINE_CHUNK, False, "combine_gather")
    out = _combine_call(y_tok, h1, route, final_norm_w.reshape(1, -1), _tile(tokens, COMBINE_ROWS))
    return out.reshape(batch, seq, D_MODEL)
```

```python
import functools

import jax
import jax.numpy as jnp
from jax import lax
from jax.experimental import pallas as pl
from jax.experimental.pallas import tpu as pltpu
from jax.experimental.pallas import tpu_sc as plsc

F32 = jnp.float32
BF16 = jnp.bfloat16
EPS = 1e-6

D_MODEL = 1024
N_META = 16
CONV_CH = 512
CONV_GROUP_W = 64
HEADS = 4
HEAD_DIM = 128
GDN_W = HEADS * HEAD_DIM
CHUNK = 64
N_GROUPS = 4
EXPERTS_PER_GROUP = 8
N_EXPERTS = N_GROUPS * EXPERTS_PER_GROUP
TOP_K = 2
D_EXPERT = 512
LANES = 128
ROUTE_E1, ROUTE_E2, ROUTE_RANK1, ROUTE_RANK2, ROUTE_W1, ROUTE_W2 = range(6)
ASSIGN_ROWS = 8
HIST = 8

PROJ_ROWS = 512
PROJ_COLS = 256
GDN_CHUNKS = 8
GDN_GROUP = 2
MIX_ROWS = 512
EXPERT_ROWS = 256
COMBINE_ROWS = 256
VMEM_LIMIT = 56 * 1024 * 1024
SC_CORES = 2
SC_SUBCORES = 16
SC_DISPATCH_CHUNK = 64
SC_COMBINE_CHUNK = 32


def _dot(a, b):
    return jnp.dot(a, b, preferred_element_type=F32)


def _dot_nt(a, b):
    return lax.dot_general(a, b, (((1,), (1,)), ((), ())), preferred_element_type=F32)


def _dot_tn(a, b):
    return lax.dot_general(a, b, (((0,), (0,)), ((), ())), preferred_element_type=F32)


def _split_bf16(x):
    hi = x.astype(BF16)
    lo = (x - hi.astype(F32)).astype(BF16)
    return hi, lo


def _sigmoid(x):
    return 1.0 / (1.0 + jnp.exp(-x))


def _proj_kernel(x_ref, hcu_ref, hqkv_ref, nw_ref, wa_ref, wq_ref, wz_ref, wab_ref, cmw_ref,
                 cmn_ref, gmat_ref, qcw_ref, alog_ref, dtb_ref,
                 ya_ref, q_ref, k_ref, v_ref, z_ref, gb_ref, tcu_ref, tqkv_ref,
                 cu_s, qkv_s, xn_s):
    rows = x_ref.shape[0]

    @pl.when(pl.program_id(1) == 0)
    def _():
        cu_s[0:HIST, :] = hcu_ref[...]
        qkv_s[0:HIST, :] = hqkv_ref[...]

    x = x_ref[...]
    ms = jnp.mean(x * x, axis=-1, keepdims=True)
    xn_s[...] = (x * lax.rsqrt(ms + EPS) * nw_ref[...]).astype(BF16)

    def causal_conv(buf, cur, w_ref, cols, tail_ref):
        taps = w_ref.shape[0]
        buf[HIST:HIST + rows, cols] = cur
        acc = buf[pl.ds(HIST - taps + 1, rows), cols] * w_ref[0:1, cols]
        for j in range(1, taps - 1):
            acc = acc + buf[pl.ds(HIST - taps + 1 + j, rows), cols] * w_ref[j:j + 1, cols]
        acc = acc + cur * w_ref[taps - 1:taps, cols]
        tail = buf[rows:rows + HIST, cols]
        buf[0:HIST, cols] = tail
        tail_ref[:, cols] = tail
        return acc

    def mixer_a_tail(i, pa):
        cols = slice(i * PROJ_COLS, (i + 1) * PROJ_COLS)
        cu = pa[:, PROJ_COLS:2 * PROJ_COLS] * pa[:, 2 * PROJ_COLS:]
        ya = pa[:, :PROJ_COLS] * causal_conv(cu_s, cu, cmw_ref, cols, tcu_ref)
        sq_hi, sq_lo = _split_bf16(ya * ya)
        msg = _dot(sq_hi, gmat_ref[...]) + _dot(sq_lo, gmat_ref[...])
        ya_ref[:, cols] = (ya * lax.rsqrt(msg + EPS) * cmn_ref[:, cols]).astype(BF16)

    heads_per_chunk = PROJ_COLS // HEAD_DIM

    def qkv_tail(i, pq):
        cols = slice(i * PROJ_COLS, (i + 1) * PROJ_COLS)
        c = causal_conv(qkv_s, pq, qcw_ref, cols, tqkv_ref)
        c = c * _sigmoid(c)
        part, first_head = divmod(i * heads_per_chunk, HEADS)
        for j in range(heads_per_chunk):
            ch = c[:, j * HEAD_DIM:(j + 1) * HEAD_DIM]
            sl = slice((first_head + j) * HEAD_DIM, (first_head + j + 1) * HEAD_DIM)
            if part == 0:
                norm = lax.rsqrt(jnp.sum(ch * ch, axis=-1, keepdims=True) + EPS)
                q_ref[:, sl] = ch * norm * (HEAD_DIM ** -0.5)
            elif part == 1:
                k_ref[:, sl] = ch * lax.rsqrt(jnp.sum(ch * ch, axis=-1, keepdims=True) + EPS)
            else:
                v_ref[:, sl] = ch

    def z_tail(pz):
        z_ref[...] = pz

    def decay_beta_tail(ab):
        sp_in = ab + dtb_ref[...]
        softplus = jnp.maximum(sp_in, 0.0) + jnp.log1p(jnp.exp(-jnp.abs(sp_in)))
        g = -jnp.exp(alog_ref[...]) * softplus
        lane = lax.broadcasted_iota(jnp.int32, ab.shape, 1)
        gb_ref[...] = jnp.where(lane < HEADS, g, jnp.where(lane < 2 * HEADS, _sigmoid(ab), 0.0))

    stages = []
    for i in range(CONV_CH // PROJ_COLS):
        w_cols = slice(3 * i * PROJ_COLS, 3 * (i + 1) * PROJ_COLS)
        stages.append((functools.partial(lambda s: _dot(xn_s[...], wa_ref[:, s]), w_cols),
                       functools.partial(mixer_a_tail, i)))
    for i in range(3 * GDN_W // PROJ_COLS):
        w_cols = slice(i * PROJ_COLS, (i + 1) * PROJ_COLS)
        stages.append((functools.partial(lambda s: _dot(xn_s[...], wq_ref[:, s]), w_cols),
                       functools.partial(qkv_tail, i)))
    stages.append((lambda: _dot(xn_s[...], wz_ref[...]), z_tail))
    stages.append((lambda: _dot(xn_s[...], wab_ref[...]), decay_beta_tail))
    pending = None
    for matmul, tail in stages:
        res = matmul()
        if pending is not None:
            pending()
        pending = functools.partial(tail, res)
    pending()


def _proj_call(x3, hist_cu, hist_qkv, w, rows):
    nb, seq, _ = x3.shape
    nt = seq // rows
    tok = lambda width: pl.BlockSpec((None, rows, width), lambda b, t: (b, t, 0))
    full = lambda a: pl.BlockSpec(a.shape, lambda b, t: (0,) * a.ndim)
    tail = lambda width: pl.BlockSpec((None, HIST, width), lambda b, t: (b, 0, 0))
    consts = (hist_cu, hist_qkv, w['mix_norm_w'], w['w_a'], w['w_qkv'], w['w_z'], w['w_ab'],
              w['conv_mix_w'], w['conv_mix_norm_w'], w['gmat'], w['qkv_conv_w'], w['a_log'],
              w['dt_bias'])
    out_shape = (
        jax.ShapeDtypeStruct((nb, seq, CONV_CH), BF16),
        jax.ShapeDtypeStruct((nb, seq, GDN_W), F32),
        jax.ShapeDtypeStruct((nb, seq, GDN_W), F32),
        jax.ShapeDtypeStruct((nb, seq, GDN_W), F32),
        jax.ShapeDtypeStruct((nb, seq, GDN_W), F32),
        jax.ShapeDtypeStruct((nb, seq, LANES), F32),
        jax.ShapeDtypeStruct((nb, HIST, CONV_CH), F32),
        jax.ShapeDtypeStruct((nb, HIST, 3 * GDN_W), F32),
    )
    return pl.pallas_call(
        _proj_kernel,
        grid=(nb, nt),
        in_specs=[tok(D_MODEL)] + [full(a) for a in consts],
        out_specs=(tok(CONV_CH), tok(GDN_W), tok(GDN_W), tok(GDN_W), tok(GDN_W), tok(LANES),
                   tail(CONV_CH), tail(3 * GDN_W)),
        out_shape=out_shape,
        scratch_shapes=[pltpu.VMEM((rows + HIST, CONV_CH), F32),
                        pltpu.VMEM((rows + HIST, 3 * GDN_W), F32),
                        pltpu.VMEM((rows, D_MODEL), BF16)],
        compiler_params=pltpu.CompilerParams(
            dimension_semantics=("arbitrary", "arbitrary"), vmem_limit_bytes=VMEM_LIMIT),
        name="proj",
    )(x3, *consts)


def _chunk_masks():
    row = lax.broadcasted_iota(jnp.int32, (CHUNK, CHUNK), 0)
    col = lax.broadcasted_iota(jnp.int32, (CHUNK, CHUNK), 1)
    incl = row >= col
    strict = row > col
    levels = []
    n = 1
    while n < CHUNK:
        levels.append((row // (2 * n) == col // (2 * n)) & ((row // n) % 2 == 1) & ((col // n) % 2 == 0))
        n *= 2
    return incl, strict, levels


def _chunk_cumsum(gb_blk, incl):
    tri = incl.astype(BF16)
    hi, lo = _split_bf16(gb_blk)
    return _dot(tri, hi) + _dot(tri, lo)


def _chunk_transforms(chains, masks, state_only):
    incl, strict, levels = masks
    eye = (lax.broadcasted_iota(jnp.int32, (CHUNK, CHUNK), 0)
           == lax.broadcasted_iota(jnp.int32, (CHUNK, CHUNK), 1)).astype(F32)
    decay = [jnp.exp(jnp.where(incl, gc_col - gc_row, -jnp.inf))
             for (_, _, _, _, gc_col, gc_row, _) in chains]
    kb = [kh * beta for (_, kh, _, beta, _, _, _) in chains]
    k_bf = [kh.astype(BF16) for (_, kh, _, _, _, _, _) in chains]
    a_mat = [jnp.where(strict, _dot_nt(kb_i.astype(BF16), k_i) * d_i, 0.0)
             for kb_i, k_i, d_i in zip(kb, k_bf, decay)]
    t_inv = [eye - jnp.where(levels[0], a_i, 0.0) for a_i in a_mat]
    for lvl in levels[1:]:
        t_bf = [t_i.astype(BF16) for t_i in t_inv]
        m1 = [_dot(jnp.where(lvl, a_i, 0.0).astype(BF16), t_i) for a_i, t_i in zip(a_mat, t_bf)]
        t_inv = [t_i - _dot(tb_i, m_i.astype(BF16)) for t_i, tb_i, m_i in zip(t_inv, t_bf, m1)]
    rhs = [jnp.concatenate([vh * beta, kb_i * jnp.exp(gc_col)], axis=1)
           for (_, _, vh, beta, gc_col, _, _), kb_i in zip(chains, kb)]
    uw = [_dot(t_i.astype(BF16), r_i.astype(BF16)).astype(BF16)
          for t_i, r_i in zip(t_inv, rhs)]
    kd = [kh * jnp.exp(g_last - gc_col) for (_, kh, _, _, gc_col, _, g_last) in chains]
    pn = [_dot_tn(kd_i.astype(BF16), uw_i) for kd_i, uw_i in zip(kd, uw)]
    if state_only:
        return [pn_i[:, :HEAD_DIM] for pn_i in pn]
    intra = [jnp.where(incl, _dot_nt(qh.astype(BF16), k_i) * d_i, 0.0)
             for (qh, _, _, _, _, _, _), k_i, d_i in zip(chains, k_bf, decay)]
    iuw = [_dot(in_i.astype(BF16), uw_i) for in_i, uw_i in zip(intra, uw)]
    out = []
    for (qh, _, _, _, gc_col, _, g_last), pn_i, iuw_i in zip(chains, pn, iuw):
        q_part = qh * jnp.exp(gc_col) - iuw_i[:, HEAD_DIM:]
        out.append((q_part, pn_i[:, HEAD_DIM:], iuw_i[:, :HEAD_DIM], pn_i[:, :HEAD_DIM],
                    jnp.exp(g_last)))
    return out


def _gdn_kernel(pq_ref, pk_ref, pv_ref, pgb_ref, q_ref, k_ref, v_ref, gb_ref, o_ref,
                s_s, qp_s, op_s, n_s, a_s):
    nb = q_ref.shape[0]
    n_chunks = q_ref.shape[1] // CHUNK
    group = GDN_GROUP if n_chunks % GDN_GROUP == 0 else 1
    masks = _chunk_masks()

    def chains_of(gb_blk, q_blk, k_blk, v_blk):
        gc = _chunk_cumsum(gb_blk, masks[0])
        gc_t = gc.T
        res = []
        for h in range(HEADS):
            sl = slice(h * HEAD_DIM, (h + 1) * HEAD_DIM)
            res.append((q_blk(sl), k_blk(sl), v_blk(sl), gb_blk[:, HEADS + h:HEADS + h + 1],
                        gc[:, h:h + 1], gc_t[h:h + 1, :CHUNK], gc[CHUNK - 1:CHUNK, h:h + 1]))
        return res

    @pl.when(pl.program_id(0) == 0)
    def _():
        chains = chains_of(pgb_ref[...], lambda sl: pq_ref[:, sl], lambda sl: pk_ref[:, sl],
                           lambda sl: pv_ref[:, sl])
        for h, n_mat in enumerate(_chunk_transforms(chains, masks, True)):
            for b in range(nb):
                s_s[b * HEADS + h] = n_mat

    def transform_group(gi, carry):
        chains, where = [], []
        for cc in range(group):
            c = gi * group + cc
            rows = pl.ds(pl.multiple_of(c * CHUNK, CHUNK), CHUNK)
            for b in range(nb):
                chains += chains_of(gb_ref[b, rows, :], lambda sl: q_ref[b, rows, sl],
                                    lambda sl: k_ref[b, rows, sl], lambda sl: v_ref[b, rows, sl])
                where += [(c, b * HEADS + h) for h in range(HEADS)]
        for (c, ch), (q_part, p_mat, o_part, n_mat, a) in zip(
                where, _chunk_transforms(chains, masks, False)):
            qp_s[c, ch, 0:CHUNK, :] = q_part.astype(BF16)
            qp_s[c, ch, CHUNK:, :] = p_mat.astype(BF16)
            op_s[c, ch] = o_part
            n_s[c, ch] = n_mat
            a_s[c, ch] = jnp.broadcast_to(a, (8, HEAD_DIM))
        return carry

    lax.fori_loop(0, n_chunks // group, transform_group, 0)

    def scan_chunk(c, carry):
        r0 = pl.multiple_of(c * CHUNK, CHUNK)
        for b in range(nb):
            for h in range(HEADS):
                ch = b * HEADS + h
                s = s_s[ch]
                r = _dot(qp_s[c, ch], s.astype(BF16))
                o_ref[b, pl.ds(r0, CHUNK), h * HEAD_DIM:(h + 1) * HEAD_DIM] = r[:CHUNK] + op_s[c, ch]
                s_s[ch] = a_s[c, ch][0:1, :] * s - r[CHUNK:] + n_s[c, ch]
        return carry

    lax.fori_loop(0, n_chunks, scan_chunk, 0)


def _gdn_call(pq, pk, pv, pgb, q, k, v, gb, chunks_per_step):
    nb, seq, _ = q.shape
    rows = chunks_per_step * CHUNK
    steps = seq // rows
    tok = lambda width: pl.BlockSpec((nb, rows, width), lambda i: (0, i, 0))
    pre = lambda width: pl.BlockSpec((None, CHUNK, width), lambda i: (0, 0, 0))
    nch = nb * HEADS
    return pl.pallas_call(
        _gdn_kernel,
        grid=(steps,),
        in_specs=[pre(GDN_W), pre(GDN_W), pre(GDN_W), pre(LANES),
                  tok(GDN_W), tok(GDN_W), tok(GDN_W), tok(LANES)],
        out_specs=tok(GDN_W),
        out_shape=jax.ShapeDtypeStruct((nb, seq, GDN_W), F32),
        scratch_shapes=[
            pltpu.VMEM((nch, HEAD_DIM, HEAD_DIM), F32),
            pltpu.VMEM((chunks_per_step, nch, CHUNK + HEAD_DIM, HEAD_DIM), BF16),
            pltpu.VMEM((chunks_per_step, nch, CHUNK, HEAD_DIM), F32),
            pltpu.VMEM((chunks_per_step, nch, HEAD_DIM, HEAD_DIM), F32),
            pltpu.VMEM((chunks_per_step, nch, 8, HEAD_DIM), F32),
        ],
        compiler_params=pltpu.CompilerParams(
            dimension_semantics=("arbitrary",), vmem_limit_bytes=VMEM_LIMIT),
        name="gdn",
    )(pq, pk, pv, pgb, q, k, v, gb)


def _mix_out_kernel(x_ref, ya_ref, o_ref, z_ref, gnw_ref, woa_ref, wob_ref, fnw_ref, wrh_ref,
                    wrl_ref, br_ref, h_ref, xn_ref, route_ref, assign_ref, counts_ref, cnt_s):
    yb = []
    for h in range(HEADS):
        sl = slice(h * HEAD_DIM, (h + 1) * HEAD_DIM)
        oh = o_ref[:, sl]
        zh = z_ref[:, sl]
        on = oh * lax.rsqrt(jnp.mean(oh * oh, axis=-1, keepdims=True) + EPS) * gnw_ref[...]
        yb.append((on * (zh * _sigmoid(zh))).astype(BF16))
    yb = jnp.concatenate(yb, axis=1)
    h1 = x_ref[...] + (_dot(ya_ref[...], woa_ref[...]) + _dot(yb, wob_ref[...]))
    h_ref[...] = h1
    xn = h1 * lax.rsqrt(jnp.mean(h1 * h1, axis=-1, keepdims=True) + EPS) * fnw_ref[...]
    x_hi, x_lo = _split_bf16(xn)
    bits = pltpu.bitcast(x_hi.astype(F32), jnp.uint32)
    half = D_MODEL // 2
    xn_ref[...] = (bits[:, :half] >> 16) | (bits[:, half:] & jnp.uint32(0xFFFF0000))
    logits = (_dot(x_hi, wrh_ref[...]) + _dot(x_lo, wrh_ref[...]) + _dot(x_hi, wrl_ref[...])
              + br_ref[...])
    lane = lax.broadcasted_iota(jnp.int32, logits.shape, 1)
    neg = -jnp.inf
    big = jnp.int32(1 << 20)

    def argmax_first(vals):
        m = jnp.max(vals, axis=-1, keepdims=True)
        idx = jnp.min(jnp.where(vals == m, lane, big), axis=-1, keepdims=True)
        return m, idx

    grp = jnp.where(lane < N_GROUPS, logits, neg)
    g_max, g_sel = argmax_first(grp)
    p_grp = 1.0 / jnp.sum(jnp.exp(grp - g_max), axis=-1, keepdims=True)
    lo_lane = N_GROUPS + g_sel * EXPERTS_PER_GROUP
    ex = jnp.where((lane >= lo_lane) & (lane < lo_lane + EXPERTS_PER_GROUP), logits, neg)
    m1, i1 = argmax_first(ex)
    m2, i2 = argmax_first(jnp.where(lane == i1, neg, ex))
    e2 = jnp.exp(m2 - m1)
    w1 = 1.0 / (1.0 + e2) * p_grp
    w2 = e2 / (1.0 + e2) * p_grp
    @pl.when(pl.program_id(0) == 0)
    def _():
        cnt_s[...] = jnp.zeros(cnt_s.shape, F32)

    rows = logits.shape[0]
    oh1 = (lane == i1 - N_GROUPS).astype(F32)
    oh2 = (lane == i2 - N_GROUPS).astype(F32)
    oh = oh1 + oh2
    earlier = (lax.broadcasted_iota(jnp.int32, (rows, rows), 0)
               > lax.broadcasted_iota(jnp.int32, (rows, rows), 1)).astype(BF16)
    before = _dot(earlier, oh.astype(BF16)) + cnt_s[...]
    rank1 = jnp.sum(before * oh1, axis=-1, keepdims=True)
    rank2 = jnp.sum(before * oh2, axis=-1, keepdims=True)
    counts = cnt_s[...] + jnp.sum(oh, axis=0, keepdims=True)
    cnt_s[...] = counts
    counts_ref[...] = counts

    route = jnp.zeros(logits.shape, F32)
    for k, val in ((ROUTE_E1, (i1 - N_GROUPS).astype(F32)), (ROUTE_E2, (i2 - N_GROUPS).astype(F32)),
                   (ROUTE_W1, w1), (ROUTE_W2, w2), (ROUTE_RANK1, rank1), (ROUTE_RANK2, rank2)):
        route = jnp.where(lane == k, val, route)
    route_ref[...] = route
    assign_ref[...] = route.T[:ASSIGN_ROWS].astype(jnp.int32)


def _mix_out_call(x2, ya, o, z, w, rows):
    tokens = x2.shape[0]
    tok = lambda width: pl.BlockSpec((rows, width), lambda i: (i, 0))
    full = lambda a: pl.BlockSpec(a.shape, lambda i: (0,) * a.ndim)
    consts = (w['gdn_norm_w'], w['w_out_a'], w['w_out_b'], w['ffn_norm_w'], w['w_route_hi'],
              w['w_route_lo'], w['b_route'])
    return pl.pallas_call(
        _mix_out_kernel,
        grid=(tokens // rows,),
        in_specs=[tok(D_MODEL), tok(CONV_CH), tok(GDN_W), tok(GDN_W)] + [full(a) for a in consts],
        out_specs=(tok(D_MODEL), tok(D_MODEL // 2), tok(LANES),
                   pl.BlockSpec((ASSIGN_ROWS, rows), lambda i: (0, i)),
                   pl.BlockSpec((1, LANES), lambda i: (0, 0))),
        out_shape=(jax.ShapeDtypeStruct((tokens, D_MODEL), F32),
                   jax.ShapeDtypeStruct((tokens, D_MODEL // 2), jnp.uint32),
                   jax.ShapeDtypeStruct((tokens, LANES), F32),
                   jax.ShapeDtypeStruct((ASSIGN_ROWS, tokens), jnp.int32),
                   jax.ShapeDtypeStruct((1, LANES), F32)),
        scratch_shapes=[pltpu.VMEM((1, LANES), F32)],
        compiler_params=pltpu.CompilerParams(
            dimension_semantics=("arbitrary",), vmem_limit_bytes=VMEM_LIMIT),
        name="mix_out",
    )(x2, ya, o, z, *consts)


def _sc_row_move(table, idx, out_rows, chunk, scatter, name):
    n = idx.shape[0]
    n_src, width = table.shape
    workers = SC_CORES * SC_SUBCORES
    per_w = n // workers
    assert n % workers == 0 and per_w % (2 * chunk) == 0
    assert n_src % per_w == 0 or not scatter
    pairs = per_w // (2 * chunk)
    mesh = plsc.VectorSubcoreMesh(core_axis_name="c", subcore_axis_name="s",
                                  num_cores=SC_CORES, num_subcores=SC_SUBCORES)

    def body(table_hbm, idx_hbm, out_hbm, idx_v, buf_a, buf_b, sem_ra, sem_rb, sem_wa, sem_wb):
        base = (lax.axis_index("s") * SC_CORES + lax.axis_index("c")) * per_w
        src_base = lax.rem(base, n_src)
        pltpu.sync_copy(idx_hbm.at[pl.ds(base, per_w)], idx_v)

        def read(c, buf, sem):
            off = pl.multiple_of(c * chunk, chunk)
            src = (table_hbm.at[pl.ds(src_base + off, chunk)] if scatter
                   else table_hbm.at[idx_v.at[pl.ds(off, chunk)]])
            return pltpu.make_async_copy(src, buf, sem)

        def write(c, buf, sem):
            off = pl.multiple_of(c * chunk, chunk)
            dst = (out_hbm.at[idx_v.at[pl.ds(off, chunk)]] if scatter
                   else out_hbm.at[pl.ds(base + off, chunk)])
            return pltpu.make_async_copy(buf, dst, sem)

        read(0, buf_a, sem_ra).start()

        @pl.loop(0, pairs)
        def _(j):
            ca = 2 * j
            cb = ca + 1
            read(cb, buf_b, sem_rb).start()
            read(ca, buf_a, sem_ra).wait()
            write(ca, buf_a, sem_wa).start()
            read(cb, buf_b, sem_rb).wait()
            write(cb, buf_b, sem_wb).start()
            write(ca, buf_a, sem_wa).wait()

            @pl.when(j + 1 < pairs)
            def _():
                read(ca + 2, buf_a, sem_ra).start()

            write(cb, buf_b, sem_wb).wait()

    return pl.kernel(
        body,
        out_type=jax.ShapeDtypeStruct((out_rows, width), table.dtype),
        mesh=mesh,
        scratch_types=[pltpu.VMEM((per_w,), jnp.int32),
                       pltpu.VMEM((chunk, width), table.dtype),
                       pltpu.VMEM((chunk, width), table.dtype),
                       pltpu.SemaphoreType.DMA, pltpu.SemaphoreType.DMA,
                       pltpu.SemaphoreType.DMA, pltpu.SemaphoreType.DMA],
        name=name,
    )(table, idx)


def _experts_kernel(bexp_ref, bvalid_ref, nused_ref, x_ref, wg_ref, wu_ref, wd_ref, y_ref,
                    wg_s, wu_s, wd_s):
    i = pl.program_id(0)
    n_used = nused_ref[0]
    first_of_expert = (i == 0) | (bexp_ref[i] != bexp_ref[jnp.maximum(i - 1, 0)])

    @pl.when((i < n_used) & first_of_expert)
    def _():
        wg_s[...] = wg_ref[...].astype(BF16)
        wu_s[...] = wu_ref[...].astype(BF16)
        wd_s[...] = wd_ref[...].astype(BF16)

    @pl.when(i < n_used)
    def _():
        row = lax.broadcasted_iota(jnp.int32, x_ref.shape, 0)
        packed = jnp.where(row < bvalid_ref[i], x_ref[...], jnp.uint32(0))
        x_lo = pltpu.bitcast(packed << 16, F32).astype(BF16)
        x_hi = pltpu.bitcast(packed & jnp.uint32(0xFFFF0000), F32).astype(BF16)
        xb = jnp.concatenate([x_lo, x_hi], axis=1)
        gate = _dot(xb, wg_s[...])
        hid = (gate * _sigmoid(gate)) * _dot(xb, wu_s[...])
        y_ref[...] = _dot(hid.astype(BF16), wd_s[...])

    @pl.when(i >= n_used)
    def _():
        y_ref[...] = jnp.zeros(y_ref.shape, F32)


def _experts_call(block_expert, block_valid, n_used, x_rows, w_gate, w_up, w_down, rows):
    n_blocks = block_expert.shape[0]
    wspec = lambda shape: pl.BlockSpec((None,) + shape, lambda i, be, bv, nu: (be[i], 0, 0))
    grid_spec = pltpu.PrefetchScalarGridSpec(
        num_scalar_prefetch=3,
        grid=(n_blocks,),
        in_specs=[pl.BlockSpec((rows, D_MODEL // 2), lambda i, be, bv, nu: (i, 0)),
                  wspec((D_MODEL, D_EXPERT)), wspec((D_MODEL, D_EXPERT)),
                  wspec((D_EXPERT, D_MODEL))],
        out_specs=pl.BlockSpec((rows, D_MODEL), lambda i, be, bv, nu: (i, 0)),
        scratch_shapes=[pltpu.VMEM((D_MODEL, D_EXPERT), BF16),
                        pltpu.VMEM((D_MODEL, D_EXPERT), BF16),
                        pltpu.VMEM((D_EXPERT, D_MODEL), BF16)],
    )
    return pl.pallas_call(
        _experts_kernel,
        grid_spec=grid_spec,
        out_shape=jax.ShapeDtypeStruct((n_blocks * rows, D_MODEL), F32),
        compiler_params=pltpu.CompilerParams(
            dimension_semantics=("arbitrary",), vmem_limit_bytes=VMEM_LIMIT),
        name="experts",
    )(block_expert, block_valid, n_used, x_rows, w_gate, w_up, w_down)


def _combine_kernel(y1_ref, y2_ref, h_ref, route_ref, fw_ref, out_ref):
    route = route_ref[...]
    moe = (route[:, ROUTE_W1:ROUTE_W1 + 1] * y1_ref[...]
           + route[:, ROUTE_W2:ROUTE_W2 + 1] * y2_ref[...])
    h2 = h_ref[...] + moe
    out_ref[...] = h2 * lax.rsqrt(jnp.mean(h2 * h2, axis=-1, keepdims=True) + EPS) * fw_ref[...]


def _combine_call(y_tok, h1, route, final_w, rows):
    tokens = h1.shape[0]
    steps = tokens // rows
    tok = lambda width: pl.BlockSpec((rows, width), lambda i: (i, 0))
    return pl.pallas_call(
        _combine_kernel,
        grid=(steps,),
        in_specs=[tok(D_MODEL), pl.BlockSpec((rows, D_MODEL), lambda i: (i + steps, 0)),
                  tok(D_MODEL), tok(LANES), pl.BlockSpec((1, D_MODEL), lambda i: (0, 0))],
        out_specs=tok(D_MODEL),
        out_shape=jax.ShapeDtypeStruct((tokens, D_MODEL), F32),
        compiler_params=pltpu.CompilerParams(
            dimension_semantics=("arbitrary",), vmem_limit_bytes=VMEM_LIMIT),
        name="combine",
    )(y_tok, y_tok, h1, route, final_w)


def _dispatch_tables(assign, counts, tokens, rows):
    n_blocks = (tokens * TOP_K + N_EXPERTS * (rows - 1)) // rows
    counts = counts[0, :N_EXPERTS].astype(jnp.int32)
    padded = (counts + rows - 1) // rows * rows
    pad_end = jnp.cumsum(padded)
    pad_start = pad_end - padded
    experts = assign[ROUTE_E1:ROUTE_E2 + 1]
    ranks = assign[ROUTE_RANK1:ROUTE_RANK2 + 1]
    is_expert = experts[..., None] == jnp.arange(N_EXPERTS, dtype=jnp.int32)
    dest = (jnp.sum(jnp.where(is_expert, pad_start, 0), axis=-1) + ranks).reshape(-1)
    block_start = jnp.arange(n_blocks, dtype=jnp.int32) * rows
    block_expert = jnp.minimum(jnp.sum(block_start[:, None] >= pad_end[None, :], axis=1),
                               N_EXPERTS - 1).astype(jnp.int32)
    block_valid = jnp.clip(pad_start[block_expert] + counts[block_expert] - block_start, 0, rows)
    n_used = (pad_end[-1] // rows).astype(jnp.int32).reshape(1)
    return dest, block_expert, block_valid.astype(jnp.int32), n_used, n_blocks


def _prepare_weights(mix_norm_w, w_in, conv_mix_w, conv_mix_norm_w, qkv_conv_w, a_log, dt_bias,
                     gdn_norm_w, w_out, ffn_norm_w, w_group, b_group, w_router, b_router):
    pad_lanes = lambda v: jnp.pad(v.reshape(1, -1), ((0, 0), (0, LANES - v.size)))
    c3 = 3 * CONV_CH
    w_ab = jnp.pad(w_in[:, c3 + 4 * GDN_W:], ((0, 0), (0, LANES - 2 * HEADS)))
    grp = jnp.arange(PROJ_COLS) // CONV_GROUP_W
    gmat = jnp.where(grp[:, None] == grp[None, :], 1.0 / CONV_GROUP_W, 0.0).astype(BF16)
    w_route = jnp.concatenate([w_group, w_router.reshape(D_MODEL, N_EXPERTS)], axis=1)
    w_route = jnp.pad(w_route, ((0, 0), (0, LANES - w_route.shape[1])))
    w_route_hi = w_route.astype(BF16)
    return dict(
        mix_norm_w=mix_norm_w.reshape(1, -1),
        w_a=w_in[:, :c3].reshape(D_MODEL, 3, CONV_CH // PROJ_COLS, PROJ_COLS).transpose(0, 2, 1, 3)
        .reshape(D_MODEL, c3).astype(BF16),
        w_qkv=w_in[:, c3:c3 + 3 * GDN_W].astype(BF16),
        w_z=w_in[:, c3 + 3 * GDN_W:c3 + 4 * GDN_W].astype(BF16),
        w_ab=w_ab.astype(BF16),
        conv_mix_w=conv_mix_w,
        conv_mix_norm_w=conv_mix_norm_w.reshape(1, -1),
        gmat=gmat,
        qkv_conv_w=qkv_conv_w,
        a_log=pad_lanes(a_log),
        dt_bias=pad_lanes(dt_bias),
        gdn_norm_w=gdn_norm_w.reshape(1, -1),
        w_out_a=w_out[:CONV_CH].astype(BF16),
        w_out_b=w_out[CONV_CH:].astype(BF16),
        ffn_norm_w=ffn_norm_w.reshape(1, -1),
        w_route_hi=w_route_hi,
        w_route_lo=(w_route - w_route_hi.astype(F32)).astype(BF16),
        b_route=pad_lanes(jnp.concatenate([b_group, b_router.reshape(-1)])),
    )


def _tile(n, preferred):
    return preferred if n % preferred == 0 else n


def kernel(x, meta_tokens, mix_norm_w, w_in, conv_mix_w, conv_mix_norm_w, qkv_conv_w, a_log,
           dt_bias, gdn_norm_w, w_out, ffn_norm_w, w_group, b_group, w_router, b_router, w_gate,
           w_up, w_down, final_norm_w):
    assert mix_norm_w.shape[0] == 1, "single-layer kernel"
    batch, seq, _ = x.shape
    assert seq % CHUNK == 0
    w = _prepare_weights(mix_norm_w[0], w_in[0], conv_mix_w[0], conv_mix_norm_w[0], qkv_conv_w[0],
                         a_log[0], dt_bias[0], gdn_norm_w[0], w_out[0], ffn_norm_w[0], w_group[0],
                         b_group[0], w_router[0], b_router[0])

    prefix = jnp.concatenate([jnp.zeros((CHUNK - N_META, D_MODEL), x.dtype),
                              meta_tokens.astype(x.dtype)], axis=0)[None]
    zero_cu = jnp.zeros((HIST, CONV_CH), F32)
    zero_qkv = jnp.zeros((HIST, 3 * GDN_W), F32)
    _, pq, pk, pv, _, pgb, tail_cu, tail_qkv = _proj_call(prefix, zero_cu, zero_qkv, w, CHUNK)

    ya, q, k, v, z, gb, _, _ = _proj_call(x, tail_cu[0], tail_qkv[0], w, _tile(seq, PROJ_ROWS))
    o = _gdn_call(pq, pk, pv, pgb, q, k, v, gb, _tile(seq // CHUNK, GDN_CHUNKS))

    tokens = batch * seq
    flat = lambda a: a.reshape(tokens, a.shape[-1])
    h1, xn2, route, assign, counts = _mix_out_call(flat(x), flat(ya), flat(o), flat(z), w,
                                           _tile(tokens, MIX_ROWS))

    dest, block_expert, block_valid, n_used, n_blocks = _dispatch_tables(assign, counts, tokens,
                                                                         EXPERT_ROWS)
    x_rows = _sc_row_move(xn2, dest, n_blocks * EXPERT_ROWS, SC_DISPATCH_CHUNK, True,
                          "dispatch_scatter")
    y_rows = _experts_call(block_expert, block_valid, n_used, x_rows, w_gate[0], w_up[0],
                           w_down[0], EXPERT_ROWS)
    y_tok = _sc_row_move(y_rows, dest, TOP_K * tokens, SC_COMBINE_CHUNK, False, "combine_gather")
    out = _combine_call(y_tok, h1, route, final_norm_w.reshape(1, -1), _tile(tokens, COMBINE_ROWS))
    return out.reshape(batch, seq, D_MODEL)
```

```python
import functools

import jax
import jax.numpy as jnp
from jax import lax
from jax.experimental import pallas as pl
from jax.experimental.pallas import tpu as pltpu
from jax.experimental.pallas import tpu_sc as plsc

F32 = jnp.float32
BF16 = jnp.bfloat16
EPS = 1e-6

D_MODEL = 1024
N_META = 16
CONV_CH = 512
CONV_GROUP_W = 64
HEADS = 4
HEAD_DIM = 128
GDN_W = HEADS * HEAD_DIM
CHUNK = 64
N_GROUPS = 4
EXPERTS_PER_GROUP = 8
N_EXPERTS = N_GROUPS * EXPERTS_PER_GROUP
TOP_K = 2
D_EXPERT = 512
LANES = 128
ROUTE_E1, ROUTE_E2, ROUTE_RANK1, ROUTE_RANK2, ROUTE_W1, ROUTE_W2 = range(6)
ASSIGN_ROWS = 8
HIST = 8

PROJ_ROWS = 512
PROJ_COLS = 256
GDN_CHUNKS = 8
GDN_GROUP = 2
MIX_ROWS = 512
EXPERT_ROWS = 256
COMBINE_ROWS = 256
VMEM_LIMIT = 56 * 1024 * 1024
SC_CORES = 2
SC_SUBCORES = 16
SC_DISPATCH_CHUNK = 64
SC_COMBINE_CHUNK = 32


def _dot(a, b):
    return jnp.dot(a, b, preferred_element_type=F32)


def _dot_nt(a, b):
    return lax.dot_general(a, b, (((1,), (1,)), ((), ())), preferred_element_type=F32)


def _dot_tn(a, b):
    return lax.dot_general(a, b, (((0,), (0,)), ((), ())), preferred_element_type=F32)


def _split_bf16(x):
    hi = x.astype(BF16)
    lo = (x - hi.astype(F32)).astype(BF16)
    return hi, lo


def _sigmoid(x):
    return 1.0 / (1.0 + jnp.exp(-x))


def _proj_kernel(x_ref, hcu_ref, hqkv_ref, nw_ref, wa_ref, wq_ref, wz_ref, wab_ref, cmw_ref,
                 cmn_ref, gmat_ref, qcw_ref, alog_ref, dtb_ref,
                 ya_ref, q_ref, k_ref, v_ref, z_ref, gb_ref, tcu_ref, tqkv_ref,
                 cu_s, qkv_s, xn_s):
    rows = x_ref.shape[0]

    @pl.when(pl.program_id(1) == 0)
    def _():
        cu_s[0:HIST, :] = hcu_ref[...]
        qkv_s[0:HIST, :] = hqkv_ref[...]

    x = x_ref[...]
    ms = jnp.mean(x * x, axis=-1, keepdims=True)
    xn_s[...] = (x * lax.rsqrt(ms + EPS) * nw_ref[...]).astype(BF16)

    def causal_conv(buf, cur, w_ref, cols, tail_ref):
        taps = w_ref.shape[0]
        buf[HIST:HIST + rows, cols] = cur
        acc = buf[pl.ds(HIST - taps + 1, rows), cols] * w_ref[0:1, cols]
        for j in range(1, taps - 1):
            acc = acc + buf[pl.ds(HIST - taps + 1 + j, rows), cols] * w_ref[j:j + 1, cols]
        acc = acc + cur * w_ref[taps - 1:taps, cols]
        tail = buf[rows:rows + HIST, cols]
        buf[0:HIST, cols] = tail
        tail_ref[:, cols] = tail
        return acc

    def mixer_a_tail(i, pa):
        cols = slice(i * PROJ_COLS, (i + 1) * PROJ_COLS)
        cu = pa[:, PROJ_COLS:2 * PROJ_COLS] * pa[:, 2 * PROJ_COLS:]
        ya = pa[:, :PROJ_COLS] * causal_conv(cu_s, cu, cmw_ref, cols, tcu_ref)
        sq_hi, sq_lo = _split_bf16(ya * ya)
        msg = _dot(sq_hi, gmat_ref[...]) + _dot(sq_lo, gmat_ref[...])
        ya_ref[:, cols] = (ya * lax.rsqrt(msg + EPS) * cmn_ref[:, cols]).astype(BF16)

    heads_per_chunk = PROJ_COLS // HEAD_DIM

    def qkv_tail(i, pq):
        cols = slice(i * PROJ_COLS, (i + 1) * PROJ_COLS)
        c = causal_conv(qkv_s, pq, qcw_ref, cols, tqkv_ref)
        c = c * _sigmoid(c)
        part, first_head = divmod(i * heads_per_chunk, HEADS)
        for j in range(heads_per_chunk):
            ch = c[:, j * HEAD_DIM:(j + 1) * HEAD_DIM]
            sl = slice((first_head + j) * HEAD_DIM, (first_head + j + 1) * HEAD_DIM)
            if part == 0:
                norm = lax.rsqrt(jnp.sum(ch * ch, axis=-1, keepdims=True) + EPS)
                q_ref[:, sl] = ch * norm * (HEAD_DIM ** -0.5)
            elif part == 1:
                k_ref[:, sl] = ch * lax.rsqrt(jnp.sum(ch * ch, axis=-1, keepdims=True) + EPS)
            else:
                v_ref[:, sl] = ch

    def z_tail(pz):
        z_ref[...] = pz

    def decay_beta_tail(ab):
        sp_in = ab + dtb_ref[...]
        softplus = jnp.maximum(sp_in, 0.0) + jnp.log1p(jnp.exp(-jnp.abs(sp_in)))
        g = -jnp.exp(alog_ref[...]) * softplus
        lane = lax.broadcasted_iota(jnp.int32, ab.shape, 1)
        gb_ref[...] = jnp.where(lane < HEADS, g, jnp.where(lane < 2 * HEADS, _sigmoid(ab), 0.0))

    stages = []
    for i in range(CONV_CH // PROJ_COLS):
        w_cols = slice(3 * i * PROJ_COLS, 3 * (i + 1) * PROJ_COLS)
        stages.append((functools.partial(lambda s: _dot(xn_s[...], wa_ref[:, s]), w_cols),
                       functools.partial(mixer_a_tail, i)))
    for i in range(3 * GDN_W // PROJ_COLS):
        w_cols = slice(i * PROJ_COLS, (i + 1) * PROJ_COLS)
        stages.append((functools.partial(lambda s: _dot(xn_s[...], wq_ref[:, s]), w_cols),
                       functools.partial(qkv_tail, i)))
    stages.append((lambda: _dot(xn_s[...], wz_ref[...]), z_tail))
    stages.append((lambda: _dot(xn_s[...], wab_ref[...]), decay_beta_tail))
    pending = None
    for matmul, tail in stages:
        res = matmul()
        if pending is not None:
            pending()
        pending = functools.partial(tail, res)
    pending()


def _proj_call(x3, hist_cu, hist_qkv, w, rows):
    nb, seq, _ = x3.shape
    nt = seq // rows
    tok = lambda width: pl.BlockSpec((None, rows, width), lambda b, t: (b, t, 0))
    full = lambda a: pl.BlockSpec(a.shape, lambda b, t: (0,) * a.ndim)
    tail = lambda width: pl.BlockSpec((None, HIST, width), lambda b, t: (b, 0, 0))
    consts = (hist_cu, hist_qkv, w['mix_norm_w'], w['w_a'], w['w_qkv'], w['w_z'], w['w_ab'],
              w['conv_mix_w'], w['conv_mix_norm_w'], w['gmat'], w['qkv_conv_w'], w['a_log'],
              w['dt_bias'])
    out_shape = (
        jax.ShapeDtypeStruct((nb, seq, CONV_CH), BF16),
        jax.ShapeDtypeStruct((nb, seq, GDN_W), F32),
        jax.ShapeDtypeStruct((nb, seq, GDN_W), F32),
        jax.ShapeDtypeStruct((nb, seq, GDN_W), F32),
        jax.ShapeDtypeStruct((nb, seq, GDN_W), F32),
        jax.ShapeDtypeStruct((nb, seq, LANES), F32),
        jax.ShapeDtypeStruct((nb, HIST, CONV_CH), F32),
        jax.ShapeDtypeStruct((nb, HIST, 3 * GDN_W), F32),
    )
    return pl.pallas_call(
        _proj_kernel,
        grid=(nb, nt),
        in_specs=[tok(D_MODEL)] + [full(a) for a in consts],
        out_specs=(tok(CONV_CH), tok(GDN_W), tok(GDN_W), tok(GDN_W), tok(GDN_W), tok(LANES),
                   tail(CONV_CH), tail(3 * GDN_W)),
        out_shape=out_shape,
        scratch_shapes=[pltpu.VMEM((rows + HIST, CONV_CH), F32),
                        pltpu.VMEM((rows + HIST, 3 * GDN_W), F32),
                        pltpu.VMEM((rows, D_MODEL), BF16)],
        compiler_params=pltpu.CompilerParams(
            dimension_semantics=("arbitrary", "arbitrary"), vmem_limit_bytes=VMEM_LIMIT),
        name="proj",
    )(x3, *consts)


def _chunk_masks():
    row = lax.broadcasted_iota(jnp.int32, (CHUNK, CHUNK), 0)
    col = lax.broadcasted_iota(jnp.int32, (CHUNK, CHUNK), 1)
    incl = row >= col
    strict = row > col
    levels = []
    n = 1
    while n < CHUNK:
        levels.append((row // (2 * n) == col // (2 * n)) & ((row // n) % 2 == 1) & ((col // n) % 2 == 0))
        n *= 2
    return incl, strict, levels


def _chunk_cumsum(gb_blk, incl):
    tri = incl.astype(BF16)
    hi, lo = _split_bf16(gb_blk)
    return _dot(tri, hi) + _dot(tri, lo)


def _chunk_transforms(chains, masks, state_only):
    incl, strict, levels = masks
    eye = (lax.broadcasted_iota(jnp.int32, (CHUNK, CHUNK), 0)
           == lax.broadcasted_iota(jnp.int32, (CHUNK, CHUNK), 1)).astype(F32)
    decay = [jnp.exp(jnp.where(incl, gc_col - gc_row, -jnp.inf))
             for (_, _, _, _, gc_col, gc_row, _) in chains]
    kb = [kh * beta for (_, kh, _, beta, _, _, _) in chains]
    k_bf = [kh.astype(BF16) for (_, kh, _, _, _, _, _) in chains]
    a_mat = [jnp.where(strict, _dot_nt(kb_i.astype(BF16), k_i) * d_i, 0.0)
             for kb_i, k_i, d_i in zip(kb, k_bf, decay)]
    t_inv = [eye - jnp.where(levels[0], a_i, 0.0) for a_i in a_mat]
    for lvl in levels[1:]:
        t_bf = [t_i.astype(BF16) for t_i in t_inv]
        m1 = [_dot(jnp.where(lvl, a_i, 0.0).astype(BF16), t_i) for a_i, t_i in zip(a_mat, t_bf)]
        t_inv = [t_i - _dot(tb_i, m_i.astype(BF16)) for t_i, tb_i, m_i in zip(t_inv, t_bf, m1)]
    rhs = [jnp.concatenate([vh * beta, kb_i * jnp.exp(gc_col)], axis=1)
           for (_, _, vh, beta, gc_col, _, _), kb_i in zip(chains, kb)]
    uw = [_dot(t_i.astype(BF16), r_i.astype(BF16)).astype(BF16)
          for t_i, r_i in zip(t_inv, rhs)]
    kd = [kh * jnp.exp(g_last - gc_col) for (_, kh, _, _, gc_col, _, g_last) in chains]
    pn = [_dot_tn(kd_i.astype(BF16), uw_i) for kd_i, uw_i in zip(kd, uw)]
    if state_only:
        return [pn_i[:, :HEAD_DIM] for pn_i in pn]
    intra = [jnp.where(incl, _dot_nt(qh.astype(BF16), k_i) * d_i, 0.0)
             for (qh, _, _, _, _, _, _), k_i, d_i in zip(chains, k_bf, decay)]
    iuw = [_dot(in_i.astype(BF16), uw_i) for in_i, uw_i in zip(intra, uw)]
    out = []
    for (qh, _, _, _, gc_col, _, g_last), pn_i, iuw_i in zip(chains, pn, iuw):
        q_part = qh * jnp.exp(gc_col) - iuw_i[:, HEAD_DIM:]
        out.append((q_part, pn_i[:, HEAD_DIM:], iuw_i[:, :HEAD_DIM], pn_i[:, :HEAD_DIM],
                    jnp.exp(g_last)))
    return out


def _gdn_kernel(pq_ref, pk_ref, pv_ref, pgb_ref, q_ref, k_ref, v_ref, gb_ref, o_ref,
                s_s, qp_s, op_s, n_s, a_s):
    nb = q_ref.shape[0]
    n_chunks = q_ref.shape[1] // CHUNK
    group = GDN_GROUP if n_chunks % GDN_GROUP == 0 else 1
    masks = _chunk_masks()

    def chains_of(gb_blk, q_blk, k_blk, v_blk):
        gc = _chunk_cumsum(gb_blk, masks[0])
        gc_t = gc.T
        res = []
        for h in range(HEADS):
            sl = slice(h * HEAD_DIM, (h + 1) * HEAD_DIM)
            res.append((q_blk(sl), k_blk(sl), v_blk(sl), gb_blk[:, HEADS + h:HEADS + h + 1],
                        gc[:, h:h + 1], gc_t[h:h + 1, :CHUNK], gc[CHUNK - 1:CHUNK, h:h + 1]))
        return res

    @pl.when(pl.program_id(0) == 0)
    def _():
        chains = chains_of(pgb_ref[...], lambda sl: pq_ref[:, sl], lambda sl: pk_ref[:, sl],
                           lambda sl: pv_ref[:, sl])
        for h, n_mat in enumerate(_chunk_transforms(chains, masks, True)):
            for b in range(nb):
                s_s[b * HEADS + h] = n_mat

    def transform_group(gi, carry):
        chains, where = [], []
        for cc in range(group):
            c = gi * group + cc
            rows = pl.ds(pl.multiple_of(c * CHUNK, CHUNK), CHUNK)
            for b in range(nb):
                chains += chains_of(gb_ref[b, rows, :], lambda sl: q_ref[b, rows, sl],
                                    lambda sl: k_ref[b, rows, sl], lambda sl: v_ref[b, rows, sl])
                where += [(c, b * HEADS + h) for h in range(HEADS)]
        for (c, ch), (q_part, p_mat, o_part, n_mat, a) in zip(
                where, _chunk_transforms(chains, masks, False)):
            qp_s[c, ch, 0:CHUNK, :] = q_part.astype(BF16)
            qp_s[c, ch, CHUNK:, :] = p_mat.astype(BF16)
            op_s[c, ch] = o_part
            n_s[c, ch] = n_mat
            a_s[c, ch] = jnp.broadcast_to(a, (8, HEAD_DIM))
        return carry

    lax.fori_loop(0, n_chunks // group, transform_group, 0)

    def scan_chunk(c, carry):
        r0 = pl.multiple_of(c * CHUNK, CHUNK)
        for b in range(nb):
            for h in range(HEADS):
                ch = b * HEADS + h
                s = s_s[ch]
                r = _dot(qp_s[c, ch], s.astype(BF16))
                o_ref[b, pl.ds(r0, CHUNK), h * HEAD_DIM:(h + 1) * HEAD_DIM] = r[:CHUNK] + op_s[c, ch]
                s_s[ch] = a_s[c, ch][0:1, :] * s - r[CHUNK:] + n_s[c, ch]
        return carry

    lax.fori_loop(0, n_chunks, scan_chunk, 0)


def _gdn_call(pq, pk, pv, pgb, q, k, v, gb, chunks_per_step):
    nb, seq, _ = q.shape
    rows = chunks_per_step * CHUNK
    steps = seq // rows
    tok = lambda width: pl.BlockSpec((nb, rows, width), lambda i: (0, i, 0))
    pre = lambda width: pl.BlockSpec((None, CHUNK, width), lambda i: (0, 0, 0))
    nch = nb * HEADS
    return pl.pallas_call(
        _gdn_kernel,
        grid=(steps,),
        in_specs=[pre(GDN_W), pre(GDN_W), pre(GDN_W), pre(LANES),
                  tok(GDN_W), tok(GDN_W), tok(GDN_W), tok(LANES)],
        out_specs=tok(GDN_W),
        out_shape=jax.ShapeDtypeStruct((nb, seq, GDN_W), F32),
        scratch_shapes=[
            pltpu.VMEM((nch, HEAD_DIM, HEAD_DIM), F32),
            pltpu.VMEM((chunks_per_step, nch, CHUNK + HEAD_DIM, HEAD_DIM), BF16),
            pltpu.VMEM((chunks_per_step, nch, CHUNK, HEAD_DIM), F32),
            pltpu.VMEM((chunks_per_step, nch, HEAD_DIM, HEAD_DIM), F32),
            pltpu.VMEM((chunks_per_step, nch, 8, HEAD_DIM), F32),
        ],
        compiler_params=pltpu.CompilerParams(
            dimension_semantics=("arbitrary",), vmem_limit_bytes=VMEM_LIMIT),
        name="gdn",
    )(pq, pk, pv, pgb, q, k, v, gb)


def _mix_out_kernel(x_ref, ya_ref, o_ref, z_ref, gnw_ref, woa_ref, wob_ref, fnw_ref, wrh_ref,
                    wrl_ref, br_ref, h_ref, xn_ref, route_ref, assign_ref, counts_ref, cnt_s):
    yb = []
    for h in range(HEADS):
        sl = slice(h * HEAD_DIM, (h + 1) * HEAD_DIM)
        oh = o_ref[:, sl]
        zh = z_ref[:, sl]
        on = oh * lax.rsqrt(jnp.mean(oh * oh, axis=-1, keepdims=True) + EPS) * gnw_ref[...]
        yb.append((on * (zh * _sigmoid(zh))).astype(BF16))
    yb = jnp.concatenate(yb, axis=1)
    h1 = x_ref[...] + (_dot(ya_ref[...], woa_ref[...]) + _dot(yb, wob_ref[...]))
    h_ref[...] = h1
    xn = h1 * lax.rsqrt(jnp.mean(h1 * h1, axis=-1, keepdims=True) + EPS) * fnw_ref[...]
    x_hi, x_lo = _split_bf16(xn)
    bits = pltpu.bitcast(x_hi.astype(F32), jnp.uint32)
    half = D_MODEL // 2
    xn_ref[...] = (bits[:, :half] >> 16) | (bits[:, half:] & jnp.uint32(0xFFFF0000))
    logits = (_dot(x_hi, wrh_ref[...]) + _dot(x_lo, wrh_ref[...]) + _dot(x_hi, wrl_ref[...])
              + br_ref[...])
    lane = lax.broadcasted_iota(jnp.int32, logits.shape, 1)
    neg = -jnp.inf
    big = jnp.int32(1 << 20)

    def argmax_first(vals):
        m = jnp.max(vals, axis=-1, keepdims=True)
        idx = jnp.min(jnp.where(vals == m, lane, big), axis=-1, keepdims=True)
        return m, idx

    grp = jnp.where(lane < N_GROUPS, logits, neg)
    g_max, g_sel = argmax_first(grp)
    p_grp = 1.0 / jnp.sum(jnp.exp(grp - g_max), axis=-1, keepdims=True)
    lo_lane = N_GROUPS + g_sel * EXPERTS_PER_GROUP
    ex = jnp.where((lane >= lo_lane) & (lane < lo_lane + EXPERTS_PER_GROUP), logits, neg)
    m1, i1 = argmax_first(ex)
    m2, i2 = argmax_first(jnp.where(lane == i1, neg, ex))
    e2 = jnp.exp(m2 - m1)
    w1 = 1.0 / (1.0 + e2) * p_grp
    w2 = e2 / (1.0 + e2) * p_grp
    @pl.when(pl.program_id(0) == 0)
    def _():
        cnt_s[...] = jnp.zeros(cnt_s.shape, F32)

    rows = logits.shape[0]
    oh1 = (lane == i1 - N_GROUPS).astype(F32)
    oh2 = (lane == i2 - N_GROUPS).astype(F32)
    oh = oh1 + oh2
    earlier = (lax.broadcasted_iota(jnp.int32, (rows, rows), 0)
               > lax.broadcasted_iota(jnp.int32, (rows, rows), 1)).astype(BF16)
    before = _dot(earlier, oh.astype(BF16)) + cnt_s[...]
    rank1 = jnp.sum(before * oh1, axis=-1, keepdims=True)
    rank2 = jnp.sum(before * oh2, axis=-1, keepdims=True)
    counts = cnt_s[...] + jnp.sum(oh, axis=0, keepdims=True)
    cnt_s[...] = counts
    counts_ref[...] = counts

    route = jnp.zeros(logits.shape, F32)
    for k, val in ((ROUTE_E1, (i1 - N_GROUPS).astype(F32)), (ROUTE_E2, (i2 - N_GROUPS).astype(F32)),
                   (ROUTE_W1, w1), (ROUTE_W2, w2), (ROUTE_RANK1, rank1), (ROUTE_RANK2, rank2)):
        route = jnp.where(lane == k, val, route)
    route_ref[...] = route
    assign_ref[...] = route.T[:ASSIGN_ROWS].astype(jnp.int32)


def _mix_out_call(x2, ya, o, z, w, rows):
    tokens = x2.shape[0]
    tok = lambda width: pl.BlockSpec((rows, width), lambda i: (i, 0))
    full = lambda a: pl.BlockSpec(a.shape, lambda i: (0,) * a.ndim)
    consts = (w['gdn_norm_w'], w['w_out_a'], w['w_out_b'], w['ffn_norm_w'], w['w_route_hi'],
              w['w_route_lo'], w['b_route'])
    return pl.pallas_call(
        _mix_out_kernel,
        grid=(tokens // rows,),
        in_specs=[tok(D_MODEL), tok(CONV_CH), tok(GDN_W), tok(GDN_W)] + [full(a) for a in consts],
        out_specs=(tok(D_MODEL), tok(D_MODEL // 2), tok(LANES),
                   pl.BlockSpec((ASSIGN_ROWS, rows), lambda i: (0, i)),
                   pl.BlockSpec((1, LANES), lambda i: (0, 0))),
        out_shape=(jax.ShapeDtypeStruct((tokens, D_MODEL), F32),
                   jax.ShapeDtypeStruct((tokens, D_MODEL // 2), jnp.uint32),
                   jax.ShapeDtypeStruct((tokens, LANES), F32),
                   jax.ShapeDtypeStruct((ASSIGN_ROWS, tokens), jnp.int32),
                   jax.ShapeDtypeStruct((1, LANES), F32)),
        scratch_shapes=[pltpu.VMEM((1, LANES), F32)],
        compiler_params=pltpu.CompilerParams(
            dimension_semantics=("arbitrary",), vmem_limit_bytes=VMEM_LIMIT),
        name="mix_out",
    )(x2, ya, o, z, *consts)


def _sc_row_move(table, idx, out_rows, chunk, scatter, name):
    n = idx.shape[0]
    n_src, width = table.shape
    workers = SC_CORES * SC_SUBCORES
    per_w = n // workers
    assert n % workers == 0 and per_w % (2 * chunk) == 0
    assert n_src % per_w == 0 or not scatter
    pairs = per_w // (2 * chunk)
    mesh = plsc.VectorSubcoreMesh(core_axis_name="c", subcore_axis_name="s",
                                  num_cores=SC_CORES, num_subcores=SC_SUBCORES)

    def body(table_hbm, idx_hbm, out_hbm, idx_v, buf_a, buf_b, sem_ra, sem_rb, sem_wa, sem_wb):
        base = (lax.axis_index("s") * SC_CORES + lax.axis_index("c")) * per_w
        src_base = lax.rem(base, n_src)
        pltpu.sync_copy(idx_hbm.at[pl.ds(base, per_w)], idx_v)

        def read(c, buf, sem):
            off = pl.multiple_of(c * chunk, chunk)
            src = (table_hbm.at[pl.ds(src_base + off, chunk)] if scatter
                   else table_hbm.at[idx_v.at[pl.ds(off, chunk)]])
            return pltpu.make_async_copy(src, buf, sem)

        def write(c, buf, sem):
            off = pl.multiple_of(c * chunk, chunk)
            dst = (out_hbm.at[idx_v.at[pl.ds(off, chunk)]] if scatter
                   else out_hbm.at[pl.ds(base + off, chunk)])
            return pltpu.make_async_copy(buf, dst, sem)

        read(0, buf_a, sem_ra).start()

        @pl.loop(0, pairs)
        def _(j):
            ca = 2 * j
            cb = ca + 1
            read(cb, buf_b, sem_rb).start()
            read(ca, buf_a, sem_ra).wait()
            write(ca, buf_a, sem_wa).start()
            read(cb, buf_b, sem_rb).wait()
            write(cb, buf_b, sem_wb).start()
            write(ca, buf_a, sem_wa).wait()

            @pl.when(j + 1 < pairs)
            def _():
                read(ca + 2, buf_a, sem_ra).start()

            write(cb, buf_b, sem_wb).wait()

    return pl.kernel(
        body,
        out_type=jax.ShapeDtypeStruct((out_rows, width), table.dtype),
        mesh=mesh,
        scratch_types=[pltpu.VMEM((per_w,), jnp.int32),
                       pltpu.VMEM((chunk, width), table.dtype),
                       pltpu.VMEM((chunk, width), table.dtype),
                       pltpu.SemaphoreType.DMA, pltpu.SemaphoreType.DMA,
                       pltpu.SemaphoreType.DMA, pltpu.SemaphoreType.DMA],
        name=name,
    )(table, idx)


def _experts_kernel(bexp_ref, bvalid_ref, nused_ref, x_ref, wg_hbm, wu_hbm, wd_hbm, y_ref,
                    wg_s, wu_s, wd_s, stage_g, stage_u, stage_d, sems, seq_s):
    i = pl.program_id(0)
    n_used = nused_ref[0]
    n_last = bexp_ref.shape[0] - 1
    expert = bexp_ref[i]
    first_of_expert = (i == 0) | (expert != bexp_ref[jnp.maximum(i - 1, 0)])

    def weight_copies(e, slot):
        return (pltpu.make_async_copy(wg_hbm.at[e], stage_g.at[slot], sems.at[slot, 0]),
                pltpu.make_async_copy(wu_hbm.at[e], stage_u.at[slot], sems.at[slot, 1]),
                pltpu.make_async_copy(wd_hbm.at[e], stage_d.at[slot], sems.at[slot, 2]))

    @pl.when((i == 0) & (n_used > 0))
    def _():
        seq_s[0] = 0
        for copy in weight_copies(expert, 0):
            copy.start()

    @pl.when((i < n_used) & first_of_expert)
    def _():
        @pl.when(i > 0)
        def _():
            seq_s[0] = seq_s[0] + 1

        slot = seq_s[0] % 2
        nxt = lax.while_loop(lambda j: (j < n_used) & (bexp_ref[jnp.minimum(j, n_last)] == expert),
                             lambda j: j + 1, i + 1)

        @pl.when(nxt < n_used)
        def _():
            for copy in weight_copies(bexp_ref[jnp.minimum(nxt, n_last)], 1 - slot):
                copy.start()

        for copy in weight_copies(expert, slot):
            copy.wait()
        wg_s[...] = stage_g[slot].astype(BF16)
        wu_s[...] = stage_u[slot].astype(BF16)
        wd_s[...] = stage_d[slot].astype(BF16)

    @pl.when(i < n_used)
    def _():
        row = lax.broadcasted_iota(jnp.int32, x_ref.shape, 0)
        packed = jnp.where(row < bvalid_ref[i], x_ref[...], jnp.uint32(0))
        x_lo = pltpu.bitcast(packed << 16, F32).astype(BF16)
        x_hi = pltpu.bitcast(packed & jnp.uint32(0xFFFF0000), F32).astype(BF16)
        xb = jnp.concatenate([x_lo, x_hi], axis=1)
        gate = _dot(xb, wg_s[...])
        hid = (gate * _sigmoid(gate)) * _dot(xb, wu_s[...])
        y_ref[...] = _dot(hid.astype(BF16), wd_s[...])

    @pl.when(i >= n_used)
    def _():
        y_ref[...] = jnp.zeros(y_ref.shape, F32)


def _experts_call(block_expert, block_valid, n_used, x_rows, w_gate, w_up, w_down, rows):
    n_blocks = block_expert.shape[0]
    hbm = pl.BlockSpec(memory_space=pl.ANY)
    grid_spec = pltpu.PrefetchScalarGridSpec(
        num_scalar_prefetch=3,
        grid=(n_blocks,),
        in_specs=[pl.BlockSpec((rows, D_MODEL // 2), lambda i, be, bv, nu: (i, 0)), hbm, hbm, hbm],
        out_specs=pl.BlockSpec((rows, D_MODEL), lambda i, be, bv, nu: (i, 0)),
        scratch_shapes=[pltpu.VMEM((D_MODEL, D_EXPERT), BF16),
                        pltpu.VMEM((D_MODEL, D_EXPERT), BF16),
                        pltpu.VMEM((D_EXPERT, D_MODEL), BF16),
                        pltpu.VMEM((2, D_MODEL, D_EXPERT), F32),
                        pltpu.VMEM((2, D_MODEL, D_EXPERT), F32),
                        pltpu.VMEM((2, D_EXPERT, D_MODEL), F32),
                        pltpu.SemaphoreType.DMA((2, 3)),
                        pltpu.SMEM((1,), jnp.int32)],
    )
    return pl.pallas_call(
        _experts_kernel,
        grid_spec=grid_spec,
        out_shape=jax.ShapeDtypeStruct((n_blocks * rows, D_MODEL), F32),
        compiler_params=pltpu.CompilerParams(
            dimension_semantics=("arbitrary",), vmem_limit_bytes=VMEM_LIMIT),
        name="experts",
    )(block_expert, block_valid, n_used, x_rows, w_gate, w_up, w_down)


def _combine_kernel(y1_ref, y2_ref, h_ref, route_ref, fw_ref, out_ref):
    route = route_ref[...]
    moe = (route[:, ROUTE_W1:ROUTE_W1 + 1] * y1_ref[...]
           + route[:, ROUTE_W2:ROUTE_W2 + 1] * y2_ref[...])
    h2 = h_ref[...] + moe
    out_ref[...] = h2 * lax.rsqrt(jnp.mean(h2 * h2, axis=-1, keepdims=True) + EPS) * fw_ref[...]


def _combine_call(y_tok, h1, route, final_w, rows):
    tokens = h1.shape[0]
    steps = tokens // rows
    tok = lambda width: pl.BlockSpec((rows, width), lambda i: (i, 0))
    return pl.pallas_call(
        _combine_kernel,
        grid=(steps,),
        in_specs=[tok(D_MODEL), pl.BlockSpec((rows, D_MODEL), lambda i: (i + steps, 0)),
                  tok(D_MODEL), tok(LANES), pl.BlockSpec((1, D_MODEL), lambda i: (0, 0))],
        out_specs=tok(D_MODEL),
        out_shape=jax.ShapeDtypeStruct((tokens, D_MODEL), F32),
        compiler_params=pltpu.CompilerParams(
            dimension_semantics=("arbitrary",), vmem_limit_bytes=VMEM_LIMIT),
        name="combine",
    )(y_tok, y_tok, h1, route, final_w)


def _dispatch_tables(assign, counts, tokens, rows):
    n_blocks = (tokens * TOP_K + N_EXPERTS * (rows - 1)) // rows
    counts = counts[0, :N_EXPERTS].astype(jnp.int32)
    padded = (counts + rows - 1) // rows * rows
    pad_end = jnp.cumsum(padded)
    pad_start = pad_end - padded
    experts = assign[ROUTE_E1:ROUTE_E2 + 1]
    ranks = assign[ROUTE_RANK1:ROUTE_RANK2 + 1]
    is_expert = experts[..., None] == jnp.arange(N_EXPERTS, dtype=jnp.int32)
    dest = (jnp.sum(jnp.where(is_expert, pad_start, 0), axis=-1) + ranks).reshape(-1)
    block_start = jnp.arange(n_blocks, dtype=jnp.int32) * rows
    block_expert = jnp.minimum(jnp.sum(block_start[:, None] >= pad_end[None, :], axis=1),
                               N_EXPERTS - 1).astype(jnp.int32)
    block_valid = jnp.clip(pad_start[block_expert] + counts[block_expert] - block_start, 0, rows)
    n_used = (pad_end[-1] // rows).astype(jnp.int32).reshape(1)
    return dest, block_expert, block_valid.astype(jnp.int32), n_used, n_blocks


def _prepare_weights(mix_norm_w, w_in, conv_mix_w, conv_mix_norm_w, qkv_conv_w, a_log, dt_bias,
                     gdn_norm_w, w_out, ffn_norm_w, w_group, b_group, w_router, b_router):
    pad_lanes = lambda v: jnp.pad(v.reshape(1, -1), ((0, 0), (0, LANES - v.size)))
    c3 = 3 * CONV_CH
    w_ab = jnp.pad(w_in[:, c3 + 4 * GDN_W:], ((0, 0), (0, LANES - 2 * HEADS)))
    grp = jnp.arange(PROJ_COLS) // CONV_GROUP_W
    gmat = jnp.where(grp[:, None] == grp[None, :], 1.0 / CONV_GROUP_W, 0.0).astype(BF16)
    w_route = jnp.concatenate([w_group, w_router.reshape(D_MODEL, N_EXPERTS)], axis=1)
    w_route = jnp.pad(w_route, ((0, 0), (0, LANES - w_route.shape[1])))
    w_route_hi = w_route.astype(BF16)
    return dict(
        mix_norm_w=mix_norm_w.reshape(1, -1),
        w_a=w_in[:, :c3].reshape(D_MODEL, 3, CONV_CH // PROJ_COLS, PROJ_COLS).transpose(0, 2, 1, 3)
        .reshape(D_MODEL, c3).astype(BF16),
        w_qkv=w_in[:, c3:c3 + 3 * GDN_W].astype(BF16),
        w_z=w_in[:, c3 + 3 * GDN_W:c3 + 4 * GDN_W].astype(BF16),
        w_ab=w_ab.astype(BF16),
        conv_mix_w=conv_mix_w,
        conv_mix_norm_w=conv_mix_norm_w.reshape(1, -1),
        gmat=gmat,
        qkv_conv_w=qkv_conv_w,
        a_log=pad_lanes(a_log),
        dt_bias=pad_lanes(dt_bias),
        gdn_norm_w=gdn_norm_w.reshape(1, -1),
        w_out_a=w_out[:CONV_CH].astype(BF16),
        w_out_b=w_out[CONV_CH:].astype(BF16),
        ffn_norm_w=ffn_norm_w.reshape(1, -1),
        w_route_hi=w_route_hi,
        w_route_lo=(w_route - w_route_hi.astype(F32)).astype(BF16),
        b_route=pad_lanes(jnp.concatenate([b_group, b_router.reshape(-1)])),
    )


def _tile(n, preferred):
    return preferred if n % preferred == 0 else n


def kernel(x, meta_tokens, mix_norm_w, w_in, conv_mix_w, conv_mix_norm_w, qkv_conv_w, a_log,
           dt_bias, gdn_norm_w, w_out, ffn_norm_w, w_group, b_group, w_router, b_router, w_gate,
           w_up, w_down, final_norm_w):
    assert mix_norm_w.shape[0] == 1, "single-layer kernel"
    batch, seq, _ = x.shape
    assert seq % CHUNK == 0
    w = _prepare_weights(mix_norm_w[0], w_in[0], conv_mix_w[0], conv_mix_norm_w[0], qkv_conv_w[0],
                         a_log[0], dt_bias[0], gdn_norm_w[0], w_out[0], ffn_norm_w[0], w_group[0],
                         b_group[0], w_router[0], b_router[0])

    prefix = jnp.concatenate([jnp.zeros((CHUNK - N_META, D_MODEL), x.dtype),
                              meta_tokens.astype(x.dtype)], axis=0)[None]
    zero_cu = jnp.zeros((HIST, CONV_CH), F32)
    zero_qkv = jnp.zeros((HIST, 3 * GDN_W), F32)
    _, pq, pk, pv, _, pgb, tail_cu, tail_qkv = _proj_call(prefix, zero_cu, zero_qkv, w, CHUNK)

    ya, q, k, v, z, gb, _, _ = _proj_call(x, tail_cu[0], tail_qkv[0], w, _tile(seq, PROJ_ROWS))
    o = _gdn_call(pq, pk, pv, pgb, q, k, v, gb, _tile(seq // CHUNK, GDN_CHUNKS))

    tokens = batch * seq
    flat = lambda a: a.reshape(tokens, a.shape[-1])
    h1, xn2, route, assign, counts = _mix_out_call(flat(x), flat(ya), flat(o), flat(z), w,
                                           _tile(tokens, MIX_ROWS))

    dest, block_expert, block_valid, n_used, n_blocks = _dispatch_tables(assign, counts, tokens,
                                                                         EXPERT_ROWS)
    x_rows = _sc_row_move(xn2, dest, n_blocks * EXPERT_ROWS, SC_DISPATCH_CHUNK, True,
                          "dispatch_scatter")
    y_rows = _experts_call(block_expert, block_valid, n_used, x_rows, w_gate[0], w_up[0],
                           w_down[0], EXPERT_ROWS)
    y_tok = _sc_row_move(y_rows, dest, TOP_K * tokens, SC_COMBINE_CHUNK, False, "combine_gather")
    out = _combine_call(y_tok, h1, route, final_norm_w.reshape(1, -1), _tile(tokens, COMBINE_ROWS))
    return out.reshape(batch, seq, D_MODEL)
```

```python
import functools

import jax
import jax.numpy as jnp
from jax import lax
from jax.experimental import pallas as pl
from jax.experimental.pallas import tpu as pltpu
from jax.experimental.pallas import tpu_sc as plsc

F32 = jnp.float32
BF16 = jnp.bfloat16
EPS = 1e-6

D_MODEL = 1024
N_META = 16
CONV_CH = 512
CONV_GROUP_W = 64
HEADS = 4
HEAD_DIM = 128
GDN_W = HEADS * HEAD_DIM
QKV_COL0 = 3 * CONV_CH
Z_COL0 = QKV_COL0 + 3 * GDN_W
CHUNK = 64
N_GROUPS = 4
EXPERTS_PER_GROUP = 8
N_EXPERTS = N_GROUPS * EXPERTS_PER_GROUP
TOP_K = 2
D_EXPERT = 512
LANES = 128
ROUTE_E1, ROUTE_E2, ROUTE_RANK1, ROUTE_RANK2, ROUTE_W1, ROUTE_W2 = range(6)
ASSIGN_ROWS = 8
HIST = 8

PROJ_ROWS = 512
PROJ_COLS = 256
GDN_CHUNKS = 8
GDN_GROUP = 4
MIX_ROWS = 512
EXPERT_ROWS = 256
COMBINE_ROWS = 256
VMEM_LIMIT = 56 * 1024 * 1024
SC_CORES = 2
SC_SUBCORES = 16
SC_DISPATCH_CHUNK = 64
SC_COMBINE_CHUNK = 64


def _dot(a, b):
    return jnp.dot(a, b, preferred_element_type=F32)


def _dot_nt(a, b):
    return lax.dot_general(a, b, (((1,), (1,)), ((), ())), preferred_element_type=F32)


def _dot_tn(a, b):
    return lax.dot_general(a, b, (((0,), (0,)), ((), ())), preferred_element_type=F32)


def _split_bf16(x):
    hi = x.astype(BF16)
    lo = (x - hi.astype(F32)).astype(BF16)
    return hi, lo


def _sigmoid(x):
    return 1.0 / (1.0 + jnp.exp(-x))


def _pack_bf16_pairs(x_bf16):
    bits = pltpu.bitcast(x_bf16.astype(F32), jnp.uint32)
    n = x_bf16.shape[1] // 2
    return (bits[:, :n] >> 16) | (bits[:, n:] & jnp.uint32(0xFFFF0000))


def _unpack_bf16_pairs(packed):
    return (pltpu.bitcast(packed << 16, F32),
            pltpu.bitcast(packed & jnp.uint32(0xFFFF0000), F32))


def _proj_kernel(x_ref, hcu_ref, hqkv_ref, nw_ref, wa_ref, wi_ref, wab_ref, cmw_ref,
                 cmn_ref, gmat_ref, qcw_ref, alog_ref, dtb_ref,
                 ya_ref, q_ref, k_ref, v_ref, z_ref, gb_ref, tcu_ref, tqkv_ref,
                 cu_s, qkv_s, xn_s):
    rows = x_ref.shape[0]

    @pl.when(pl.program_id(1) == 0)
    def _():
        cu_s[0:HIST, :] = hcu_ref[...]
        qkv_s[0:HIST, :] = hqkv_ref[...]

    x = x_ref[...]
    ms = jnp.mean(x * x, axis=-1, keepdims=True)
    xn_s[...] = (x * lax.rsqrt(ms + EPS) * nw_ref[...]).astype(BF16)

    def causal_conv(buf, cur, w_ref, cols, tail_ref):
        taps = w_ref.shape[0]
        buf[HIST:HIST + rows, cols] = cur
        acc = buf[pl.ds(HIST - taps + 1, rows), cols] * w_ref[0:1, cols]
        for j in range(1, taps - 1):
            acc = acc + buf[pl.ds(HIST - taps + 1 + j, rows), cols] * w_ref[j:j + 1, cols]
        acc = acc + cur * w_ref[taps - 1:taps, cols]
        tail = buf[rows:rows + HIST, cols]
        buf[0:HIST, cols] = tail
        tail_ref[:, cols] = tail
        return acc

    def mixer_a_tail(i, pa):
        cols = slice(i * PROJ_COLS, (i + 1) * PROJ_COLS)
        cu = pa[:, PROJ_COLS:2 * PROJ_COLS] * pa[:, 2 * PROJ_COLS:]
        ya = pa[:, :PROJ_COLS] * causal_conv(cu_s, cu, cmw_ref, cols, tcu_ref)
        sq_hi, sq_lo = _split_bf16(ya * ya)
        msg = _dot(sq_hi, gmat_ref[...]) + _dot(sq_lo, gmat_ref[...])
        ya_ref[:, cols] = (ya * lax.rsqrt(msg + EPS) * cmn_ref[:, cols]).astype(BF16)

    heads_per_chunk = PROJ_COLS // HEAD_DIM

    def qkv_tail(i, pq):
        cols = slice(i * PROJ_COLS, (i + 1) * PROJ_COLS)
        c = causal_conv(qkv_s, pq, qcw_ref, cols, tqkv_ref)
        c = c * _sigmoid(c)
        part, first_head = divmod(i * heads_per_chunk, HEADS)
        for j in range(heads_per_chunk):
            ch = c[:, j * HEAD_DIM:(j + 1) * HEAD_DIM]
            sl = slice((first_head + j) * HEAD_DIM, (first_head + j + 1) * HEAD_DIM)
            if part == 0:
                norm = lax.rsqrt(jnp.sum(ch * ch, axis=-1, keepdims=True) + EPS)
                q_ref[:, sl] = ch * norm * (HEAD_DIM ** -0.5)
            elif part == 1:
                k_ref[:, sl] = ch * lax.rsqrt(jnp.sum(ch * ch, axis=-1, keepdims=True) + EPS)
            else:
                v_ref[:, sl] = ch

    def z_tail(pz):
        z_ref[...] = pz

    def decay_beta_tail(ab):
        sp_in = ab + dtb_ref[...]
        softplus = jnp.maximum(sp_in, 0.0) + jnp.log1p(jnp.exp(-jnp.abs(sp_in)))
        g = -jnp.exp(alog_ref[...]) * softplus
        lane = lax.broadcasted_iota(jnp.int32, ab.shape, 1)
        gb_ref[...] = jnp.where(lane < HEADS, g, jnp.where(lane < 2 * HEADS, _sigmoid(ab), 0.0))

    stages = []
    for i in range(CONV_CH // PROJ_COLS):
        w_cols = slice(3 * i * PROJ_COLS, 3 * (i + 1) * PROJ_COLS)
        stages.append((functools.partial(lambda s: _dot(xn_s[...], wa_ref[:, s]), w_cols),
                       functools.partial(mixer_a_tail, i)))
    for i in range(3 * GDN_W // PROJ_COLS):
        w_cols = slice(QKV_COL0 + i * PROJ_COLS, QKV_COL0 + (i + 1) * PROJ_COLS)
        stages.append((functools.partial(lambda s: _dot(xn_s[...], wi_ref[:, s]), w_cols),
                       functools.partial(qkv_tail, i)))
    stages.append((lambda: _dot(xn_s[...], wi_ref[:, Z_COL0:Z_COL0 + GDN_W]), z_tail))
    stages.append((lambda: _dot(xn_s[...], wab_ref[...]), decay_beta_tail))
    pending = None
    for matmul, tail in stages:
        res = matmul()
        if pending is not None:
            pending()
        pending = functools.partial(tail, res)
    pending()


def _proj_call(x3, hist_cu, hist_qkv, w, rows):
    nb, seq, _ = x3.shape
    nt = seq // rows
    tok = lambda width: pl.BlockSpec((None, rows, width), lambda b, t: (b, t, 0))
    full = lambda a: pl.BlockSpec(a.shape, lambda b, t: (0,) * a.ndim)
    tail = lambda width: pl.BlockSpec((None, HIST, width), lambda b, t: (b, 0, 0))
    consts = (hist_cu, hist_qkv, w['mix_norm_w'], w['w_a'], w['w_in'], w['w_ab'],
              w['conv_mix_w'], w['conv_mix_norm_w'], w['gmat'], w['qkv_conv_w'], w['a_log'],
              w['dt_bias'])
    out_shape = (
        jax.ShapeDtypeStruct((nb, seq, CONV_CH), BF16),
        jax.ShapeDtypeStruct((nb, seq, GDN_W), F32),
        jax.ShapeDtypeStruct((nb, seq, GDN_W), F32),
        jax.ShapeDtypeStruct((nb, seq, GDN_W), F32),
        jax.ShapeDtypeStruct((nb, seq, GDN_W), F32),
        jax.ShapeDtypeStruct((nb, seq, LANES), F32),
        jax.ShapeDtypeStruct((nb, HIST, CONV_CH), F32),
        jax.ShapeDtypeStruct((nb, HIST, 3 * GDN_W), F32),
    )
    return pl.pallas_call(
        _proj_kernel,
        grid=(nb, nt),
        in_specs=[tok(D_MODEL)] + [full(a) for a in consts],
        out_specs=(tok(CONV_CH), tok(GDN_W), tok(GDN_W), tok(GDN_W), tok(GDN_W), tok(LANES),
                   tail(CONV_CH), tail(3 * GDN_W)),
        out_shape=out_shape,
        scratch_shapes=[pltpu.VMEM((rows + HIST, CONV_CH), F32),
                        pltpu.VMEM((rows + HIST, 3 * GDN_W), F32),
                        pltpu.VMEM((rows, D_MODEL), BF16)],
        compiler_params=pltpu.CompilerParams(
            dimension_semantics=("arbitrary", "arbitrary"), vmem_limit_bytes=VMEM_LIMIT),
        name="proj",
    )(x3, *consts)


def _chunk_masks():
    row = lax.broadcasted_iota(jnp.int32, (CHUNK, CHUNK), 0)
    col = lax.broadcasted_iota(jnp.int32, (CHUNK, CHUNK), 1)
    incl = row >= col
    strict = row > col
    levels = []
    n = 1
    while n < CHUNK:
        levels.append((row // (2 * n) == col // (2 * n)) & ((row // n) % 2 == 1) & ((col // n) % 2 == 0))
        n *= 2
    return incl, strict, levels


def _chunk_cumsum(gb_blk, incl):
    tri = incl.astype(BF16)
    hi, lo = _split_bf16(gb_blk)
    return _dot(tri, hi) + _dot(tri, lo)


def _chunk_transforms(chains, masks, state_only):
    incl, strict, levels = masks
    eye = (lax.broadcasted_iota(jnp.int32, (CHUNK, CHUNK), 0)
           == lax.broadcasted_iota(jnp.int32, (CHUNK, CHUNK), 1)).astype(F32)
    decay = [jnp.exp(jnp.where(incl, gc_col - gc_row, -jnp.inf))
             for (_, _, _, _, gc_col, gc_row, _) in chains]
    kb = [kh * beta for (_, kh, _, beta, _, _, _) in chains]
    k_bf = [kh.astype(BF16) for (_, kh, _, _, _, _, _) in chains]
    a_mat = [jnp.where(strict, _dot_nt(kb_i.astype(BF16), k_i) * d_i, 0.0)
             for kb_i, k_i, d_i in zip(kb, k_bf, decay)]
    t_inv = [eye - jnp.where(levels[0], a_i, 0.0) for a_i in a_mat]
    for lvl in levels[1:]:
        t_bf = [t_i.astype(BF16) for t_i in t_inv]
        m1 = [_dot(jnp.where(lvl, a_i, 0.0).astype(BF16), t_i) for a_i, t_i in zip(a_mat, t_bf)]
        t_inv = [t_i - _dot(tb_i, m_i.astype(BF16)) for t_i, tb_i, m_i in zip(t_inv, t_bf, m1)]
    rhs = [jnp.concatenate([vh * beta, kb_i * jnp.exp(gc_col)], axis=1)
           for (_, _, vh, beta, gc_col, _, _), kb_i in zip(chains, kb)]
    uw = [_dot(t_i.astype(BF16), r_i.astype(BF16)).astype(BF16)
          for t_i, r_i in zip(t_inv, rhs)]
    kd = [kh * jnp.exp(g_last - gc_col) for (_, kh, _, _, gc_col, _, g_last) in chains]
    pn = [_dot_tn(kd_i.astype(BF16), uw_i) for kd_i, uw_i in zip(kd, uw)]
    if state_only:
        return [pn_i[:, :HEAD_DIM] for pn_i in pn]
    intra = [jnp.where(incl, _dot_nt(qh.astype(BF16), k_i) * d_i, 0.0)
             for (qh, _, _, _, _, _, _), k_i, d_i in zip(chains, k_bf, decay)]
    iuw = [_dot(in_i.astype(BF16), uw_i) for in_i, uw_i in zip(intra, uw)]
    out = []
    for (qh, _, _, _, gc_col, _, g_last), pn_i, iuw_i in zip(chains, pn, iuw):
        q_part = qh * jnp.exp(gc_col) - iuw_i[:, HEAD_DIM:]
        out.append((q_part, pn_i[:, HEAD_DIM:], iuw_i[:, :HEAD_DIM], pn_i[:, :HEAD_DIM],
                    jnp.exp(g_last)))
    return out


def _gdn_kernel(pq_ref, pk_ref, pv_ref, pgb_ref, q_ref, k_ref, v_ref, gb_ref, o_ref,
                s_s, qp_s, op_s, n_s, a_s):
    nb = q_ref.shape[0]
    n_chunks = q_ref.shape[1] // CHUNK
    group = GDN_GROUP if n_chunks % GDN_GROUP == 0 else 1
    masks = _chunk_masks()

    def chains_of(gb_blk, q_blk, k_blk, v_blk):
        gc = _chunk_cumsum(gb_blk, masks[0])
        gc_t = gc.T
        res = []
        for h in range(HEADS):
            sl = slice(h * HEAD_DIM, (h + 1) * HEAD_DIM)
            res.append((q_blk(sl), k_blk(sl), v_blk(sl), gb_blk[:, HEADS + h:HEADS + h + 1],
                        gc[:, h:h + 1], gc_t[h:h + 1, :CHUNK], gc[CHUNK - 1:CHUNK, h:h + 1]))
        return res

    @pl.when(pl.program_id(0) == 0)
    def _():
        chains = chains_of(pgb_ref[...], lambda sl: pq_ref[:, sl], lambda sl: pk_ref[:, sl],
                           lambda sl: pv_ref[:, sl])
        for h, n_mat in enumerate(_chunk_transforms(chains, masks, True)):
            for b in range(nb):
                s_s[b * HEADS + h] = n_mat

    def transform_group(gi, carry):
        chains, where = [], []
        for cc in range(group):
            c = gi * group + cc
            rows = pl.ds(pl.multiple_of(c * CHUNK, CHUNK), CHUNK)
            for b in range(nb):
                chains += chains_of(gb_ref[b, rows, :], lambda sl: q_ref[b, rows, sl],
                                    lambda sl: k_ref[b, rows, sl], lambda sl: v_ref[b, rows, sl])
                where += [(c, b * HEADS + h) for h in range(HEADS)]
        for (c, ch), (q_part, p_mat, o_part, n_mat, a) in zip(
                where, _chunk_transforms(chains, masks, False)):
            qp_s[c, ch, 0:CHUNK, :] = q_part.astype(BF16)
            qp_s[c, ch, CHUNK:, :] = p_mat.astype(BF16)
            op_s[c, ch] = o_part
            n_s[c, ch] = n_mat
            a_s[c, ch] = jnp.broadcast_to(a, (8, HEAD_DIM))
        return carry

    lax.fori_loop(0, n_chunks // group, transform_group, 0)

    def scan_chunk(c, carry):
        r0 = pl.multiple_of(c * CHUNK, CHUNK)
        for b in range(nb):
            for h in range(HEADS):
                ch = b * HEADS + h
                s = s_s[ch]
                r = _dot(qp_s[c, ch], s.astype(BF16))
                o_ref[b, pl.ds(r0, CHUNK), h * HEAD_DIM:(h + 1) * HEAD_DIM] = r[:CHUNK] + op_s[c, ch]
                s_s[ch] = a_s[c, ch][0:1, :] * s - r[CHUNK:] + n_s[c, ch]
        return carry

    lax.fori_loop(0, n_chunks, scan_chunk, 0)


def _gdn_call(pq, pk, pv, pgb, q, k, v, gb, chunks_per_step):
    nb, seq, _ = q.shape
    rows = chunks_per_step * CHUNK
    steps = seq // rows
    tok = lambda width: pl.BlockSpec((nb, rows, width), lambda i: (0, i, 0))
    pre = lambda width: pl.BlockSpec((None, CHUNK, width), lambda i: (0, 0, 0))
    nch = nb * HEADS
    return pl.pallas_call(
        _gdn_kernel,
        grid=(steps,),
        in_specs=[pre(GDN_W), pre(GDN_W), pre(GDN_W), pre(LANES),
                  tok(GDN_W), tok(GDN_W), tok(GDN_W), tok(LANES)],
        out_specs=tok(GDN_W),
        out_shape=jax.ShapeDtypeStruct((nb, seq, GDN_W), F32),
        scratch_shapes=[
            pltpu.VMEM((nch, HEAD_DIM, HEAD_DIM), F32),
            pltpu.VMEM((chunks_per_step, nch, CHUNK + HEAD_DIM, HEAD_DIM), BF16),
            pltpu.VMEM((chunks_per_step, nch, CHUNK, HEAD_DIM), F32),
            pltpu.VMEM((chunks_per_step, nch, HEAD_DIM, HEAD_DIM), F32),
            pltpu.VMEM((chunks_per_step, nch, 8, HEAD_DIM), F32),
        ],
        compiler_params=pltpu.CompilerParams(
            dimension_semantics=("arbitrary",), vmem_limit_bytes=VMEM_LIMIT),
        name="gdn",
    )(pq, pk, pv, pgb, q, k, v, gb)


def _mix_out_kernel(x_ref, ya_ref, o_ref, z_ref, gnw_ref, woa_ref, wob_ref, fnw_ref, wrh_ref,
                    wrl_ref, br_ref, h_ref, xn_ref, route_ref, assign_ref, counts_ref, cnt_s):
    yb = []
    for h in range(HEADS):
        sl = slice(h * HEAD_DIM, (h + 1) * HEAD_DIM)
        oh = o_ref[:, sl]
        zh = z_ref[:, sl]
        on = oh * lax.rsqrt(jnp.mean(oh * oh, axis=-1, keepdims=True) + EPS) * gnw_ref[...]
        yb.append((on * (zh * _sigmoid(zh))).astype(BF16))
    yb = jnp.concatenate(yb, axis=1)
    h1 = x_ref[...] + (_dot(ya_ref[...], woa_ref[...]) + _dot(yb, wob_ref[...]))
    h_ref[...] = h1
    xn = h1 * lax.rsqrt(jnp.mean(h1 * h1, axis=-1, keepdims=True) + EPS) * fnw_ref[...]
    x_hi, x_lo = _split_bf16(xn)
    xn_ref[...] = _pack_bf16_pairs(x_hi)
    logits = (_dot(x_hi, wrh_ref[...]) + _dot(x_lo, wrh_ref[...]) + _dot(x_hi, wrl_ref[...])
              + br_ref[...])
    lane = lax.broadcasted_iota(jnp.int32, logits.shape, 1)
    neg = -jnp.inf
    big = jnp.int32(1 << 20)

    def argmax_first(vals):
        m = jnp.max(vals, axis=-1, keepdims=True)
        idx = jnp.min(jnp.where(vals == m, lane, big), axis=-1, keepdims=True)
        return m, idx

    grp = jnp.where(lane < N_GROUPS, logits, neg)
    g_max, g_sel = argmax_first(grp)
    p_grp = 1.0 / jnp.sum(jnp.exp(grp - g_max), axis=-1, keepdims=True)
    lo_lane = N_GROUPS + g_sel * EXPERTS_PER_GROUP
    ex = jnp.where((lane >= lo_lane) & (lane < lo_lane + EXPERTS_PER_GROUP), logits, neg)
    m1, i1 = argmax_first(ex)
    m2, i2 = argmax_first(jnp.where(lane == i1, neg, ex))
    e2 = jnp.exp(m2 - m1)
    w1 = 1.0 / (1.0 + e2) * p_grp
    w2 = e2 / (1.0 + e2) * p_grp
    @pl.when(pl.program_id(0) == 0)
    def _():
        cnt_s[...] = jnp.zeros(cnt_s.shape, F32)

    rows = logits.shape[0]
    oh1 = (lane == i1 - N_GROUPS).astype(F32)
    oh2 = (lane == i2 - N_GROUPS).astype(F32)
    oh = oh1 + oh2
    earlier = (lax.broadcasted_iota(jnp.int32, (rows, rows), 0)
               > lax.broadcasted_iota(jnp.int32, (rows, rows), 1)).astype(BF16)
    before = _dot(earlier, oh.astype(BF16)) + cnt_s[...]
    rank1 = jnp.sum(before * oh1, axis=-1, keepdims=True)
    rank2 = jnp.sum(before * oh2, axis=-1, keepdims=True)
    counts = cnt_s[...] + jnp.sum(oh, axis=0, keepdims=True)
    cnt_s[...] = counts
    counts_ref[...] = counts

    route = jnp.zeros(logits.shape, F32)
    for k, val in ((ROUTE_E1, (i1 - N_GROUPS).astype(F32)), (ROUTE_E2, (i2 - N_GROUPS).astype(F32)),
                   (ROUTE_W1, w1), (ROUTE_W2, w2), (ROUTE_RANK1, rank1), (ROUTE_RANK2, rank2)):
        route = jnp.where(lane == k, val, route)
    route_ref[...] = route
    assign_ref[...] = route.T[:ASSIGN_ROWS].astype(jnp.int32)


def _mix_out_call(x2, ya, o, z, w, rows):
    tokens = x2.shape[0]
    tok = lambda width: pl.BlockSpec((rows, width), lambda i: (i, 0))
    full = lambda a: pl.BlockSpec(a.shape, lambda i: (0,) * a.ndim)
    consts = (w['gdn_norm_w'], w['w_out_a'], w['w_out_b'], w['ffn_norm_w'], w['w_route_hi'],
              w['w_route_lo'], w['b_route'])
    return pl.pallas_call(
        _mix_out_kernel,
        grid=(tokens // rows,),
        in_specs=[tok(D_MODEL), tok(CONV_CH), tok(GDN_W), tok(GDN_W)] + [full(a) for a in consts],
        out_specs=(tok(D_MODEL), tok(D_MODEL // 2), tok(LANES),
                   pl.BlockSpec((ASSIGN_ROWS, rows), lambda i: (0, i)),
                   pl.BlockSpec((1, LANES), lambda i: (0, 0))),
        out_shape=(jax.ShapeDtypeStruct((tokens, D_MODEL), F32),
                   jax.ShapeDtypeStruct((tokens, D_MODEL // 2), jnp.uint32),
                   jax.ShapeDtypeStruct((tokens, LANES), F32),
                   jax.ShapeDtypeStruct((ASSIGN_ROWS, tokens), jnp.int32),
                   jax.ShapeDtypeStruct((1, LANES), F32)),
        scratch_shapes=[pltpu.VMEM((1, LANES), F32)],
        compiler_params=pltpu.CompilerParams(
            dimension_semantics=("arbitrary",), vmem_limit_bytes=VMEM_LIMIT),
        name="mix_out",
    )(x2, ya, o, z, *consts)


def _sc_row_move(table, idx, out_rows, chunk, scatter, name):
    n = idx.shape[0]
    n_src, width = table.shape
    workers = SC_CORES * SC_SUBCORES
    per_w = n // workers
    assert n % workers == 0 and per_w % (2 * chunk) == 0
    assert n_src % per_w == 0 or not scatter
    pairs = per_w // (2 * chunk)
    mesh = plsc.VectorSubcoreMesh(core_axis_name="c", subcore_axis_name="s",
                                  num_cores=SC_CORES, num_subcores=SC_SUBCORES)

    def body(table_hbm, idx_hbm, out_hbm, idx_v, buf_a, buf_b, sem_ra, sem_rb, sem_wa, sem_wb):
        base = (lax.axis_index("s") * SC_CORES + lax.axis_index("c")) * per_w
        src_base = lax.rem(base, n_src)
        pltpu.sync_copy(idx_hbm.at[pl.ds(base, per_w)], idx_v)

        def read(c, buf, sem):
            off = pl.multiple_of(c * chunk, chunk)
            src = (table_hbm.at[pl.ds(src_base + off, chunk)] if scatter
                   else table_hbm.at[idx_v.at[pl.ds(off, chunk)]])
            return pltpu.make_async_copy(src, buf, sem)

        def write(c, buf, sem):
            off = pl.multiple_of(c * chunk, chunk)
            dst = (out_hbm.at[idx_v.at[pl.ds(off, chunk)]] if scatter
                   else out_hbm.at[pl.ds(base + off, chunk)])
            return pltpu.make_async_copy(buf, dst, sem)

        read(0, buf_a, sem_ra).start()

        @pl.loop(0, pairs)
        def _(j):
            ca = 2 * j
            cb = ca + 1
            read(cb, buf_b, sem_rb).start()
            read(ca, buf_a, sem_ra).wait()
            write(ca, buf_a, sem_wa).start()
            read(cb, buf_b, sem_rb).wait()
            write(cb, buf_b, sem_wb).start()
            write(ca, buf_a, sem_wa).wait()

            @pl.when(j + 1 < pairs)
            def _():
                read(ca + 2, buf_a, sem_ra).start()

            write(cb, buf_b, sem_wb).wait()

    return pl.kernel(
        body,
        out_type=jax.ShapeDtypeStruct((out_rows, width), table.dtype),
        mesh=mesh,
        scratch_types=[pltpu.VMEM((per_w,), jnp.int32),
                       pltpu.VMEM((chunk, width), table.dtype),
                       pltpu.VMEM((chunk, width), table.dtype),
                       pltpu.SemaphoreType.DMA, pltpu.SemaphoreType.DMA,
                       pltpu.SemaphoreType.DMA, pltpu.SemaphoreType.DMA],
        name=name,
    )(table, idx)


def _experts_kernel(bexp_ref, bvalid_ref, nused_ref, x_ref, wg_hbm, wu_hbm, wd_hbm, y_ref,
                    wg_s, wu_s, wd_s, stage_g, stage_u, stage_d, sems, seq_s):
    i = pl.program_id(0)
    n_used = nused_ref[0]
    n_last = bexp_ref.shape[0] - 1
    expert = bexp_ref[i]
    first_of_expert = (i == 0) | (expert != bexp_ref[jnp.maximum(i - 1, 0)])

    def weight_copies(e, slot):
        return (pltpu.make_async_copy(wg_hbm.at[e], stage_g.at[slot], sems.at[slot, 0]),
                pltpu.make_async_copy(wu_hbm.at[e], stage_u.at[slot], sems.at[slot, 1]),
                pltpu.make_async_copy(wd_hbm.at[e], stage_d.at[slot], sems.at[slot, 2]))

    @pl.when((i == 0) & (n_used > 0))
    def _():
        seq_s[0] = 0
        for copy in weight_copies(expert, 0):
            copy.start()

    @pl.when((i < n_used) & first_of_expert)
    def _():
        @pl.when(i > 0)
        def _():
            seq_s[0] = seq_s[0] + 1

        slot = seq_s[0] % 2
        nxt = lax.while_loop(lambda j: (j < n_used) & (bexp_ref[jnp.minimum(j, n_last)] == expert),
                             lambda j: j + 1, i + 1)

        @pl.when(nxt < n_used)
        def _():
            for copy in weight_copies(bexp_ref[jnp.minimum(nxt, n_last)], 1 - slot):
                copy.start()

        for copy in weight_copies(expert, slot):
            copy.wait()
        wg_s[...] = stage_g[slot].astype(BF16)
        wu_s[...] = stage_u[slot].astype(BF16)
        wd_s[...] = stage_d[slot].astype(BF16)

    @pl.when(i < n_used)
    def _():
        row = lax.broadcasted_iota(jnp.int32, x_ref.shape, 0)
        packed = jnp.where(row < bvalid_ref[i], x_ref[...], jnp.uint32(0))
        xb = jnp.concatenate(_unpack_bf16_pairs(packed), axis=1).astype(BF16)
        gate = _dot(xb, wg_s[...])
        hid = (gate * _sigmoid(gate)) * _dot(xb, wu_s[...])
        y_ref[...] = _pack_bf16_pairs(_dot(hid.astype(BF16), wd_s[...]).astype(BF16))

    @pl.when(i >= n_used)
    def _():
        y_ref[...] = jnp.zeros(y_ref.shape, jnp.uint32)


def _experts_call(block_expert, block_valid, n_used, x_rows, w_gate, w_up, w_down, rows):
    n_blocks = block_expert.shape[0]
    hbm = pl.BlockSpec(memory_space=pl.ANY)
    grid_spec = pltpu.PrefetchScalarGridSpec(
        num_scalar_prefetch=3,
        grid=(n_blocks,),
        in_specs=[pl.BlockSpec((rows, D_MODEL // 2), lambda i, be, bv, nu: (i, 0)), hbm, hbm, hbm],
        out_specs=pl.BlockSpec((rows, D_MODEL // 2), lambda i, be, bv, nu: (i, 0)),
        scratch_shapes=[pltpu.VMEM((D_MODEL, D_EXPERT), BF16),
                        pltpu.VMEM((D_MODEL, D_EXPERT), BF16),
                        pltpu.VMEM((D_EXPERT, D_MODEL), BF16),
                        pltpu.VMEM((2, D_MODEL, D_EXPERT), F32),
                        pltpu.VMEM((2, D_MODEL, D_EXPERT), F32),
                        pltpu.VMEM((2, D_EXPERT, D_MODEL), F32),
                        pltpu.SemaphoreType.DMA((2, 3)),
                        pltpu.SMEM((1,), jnp.int32)],
    )
    return pl.pallas_call(
        _experts_kernel,
        grid_spec=grid_spec,
        out_shape=jax.ShapeDtypeStruct((n_blocks * rows, D_MODEL // 2), jnp.uint32),
        compiler_params=pltpu.CompilerParams(
            dimension_semantics=("arbitrary",), vmem_limit_bytes=VMEM_LIMIT),
        name="experts",
    )(block_expert, block_valid, n_used, x_rows, w_gate, w_up, w_down)


def _combine_kernel(y1_ref, y2_ref, h_ref, route_ref, fw_ref, out_ref):
    route = route_ref[...]
    y1 = jnp.concatenate(_unpack_bf16_pairs(y1_ref[...]), axis=1)
    y2 = jnp.concatenate(_unpack_bf16_pairs(y2_ref[...]), axis=1)
    moe = route[:, ROUTE_W1:ROUTE_W1 + 1] * y1 + route[:, ROUTE_W2:ROUTE_W2 + 1] * y2
    h2 = h_ref[...] + moe
    out_ref[...] = h2 * lax.rsqrt(jnp.mean(h2 * h2, axis=-1, keepdims=True) + EPS) * fw_ref[...]


def _combine_call(y_tok, h1, route, final_w, rows):
    tokens = h1.shape[0]
    steps = tokens // rows
    tok = lambda width: pl.BlockSpec((rows, width), lambda i: (i, 0))
    return pl.pallas_call(
        _combine_kernel,
        grid=(steps,),
        in_specs=[tok(D_MODEL // 2), pl.BlockSpec((rows, D_MODEL // 2), lambda i: (i + steps, 0)),
                  tok(D_MODEL), tok(LANES), pl.BlockSpec((1, D_MODEL), lambda i: (0, 0))],
        out_specs=tok(D_MODEL),
        out_shape=jax.ShapeDtypeStruct((tokens, D_MODEL), F32),
        compiler_params=pltpu.CompilerParams(
            dimension_semantics=("arbitrary",), vmem_limit_bytes=VMEM_LIMIT),
        name="combine",
    )(y_tok, y_tok, h1, route, final_w)


def _dispatch_tables(assign, counts, tokens, rows):
    n_blocks = (tokens * TOP_K + N_EXPERTS * (rows - 1)) // rows
    counts = counts[0, :N_EXPERTS].astype(jnp.int32)
    padded = (counts + rows - 1) // rows * rows
    pad_end = jnp.cumsum(padded)
    pad_start = pad_end - padded
    experts = assign[ROUTE_E1:ROUTE_E2 + 1]
    ranks = assign[ROUTE_RANK1:ROUTE_RANK2 + 1]
    is_expert = experts[..., None] == jnp.arange(N_EXPERTS, dtype=jnp.int32)
    dest = (jnp.sum(jnp.where(is_expert, pad_start, 0), axis=-1) + ranks).reshape(-1)
    block_start = jnp.arange(n_blocks, dtype=jnp.int32) * rows
    block_expert = jnp.minimum(jnp.sum(block_start[:, None] >= pad_end[None, :], axis=1),
                               N_EXPERTS - 1).astype(jnp.int32)
    block_valid = jnp.clip(pad_start[block_expert] + counts[block_expert] - block_start, 0, rows)
    n_used = (pad_end[-1] // rows).astype(jnp.int32).reshape(1)
    return dest, block_expert, block_valid.astype(jnp.int32), n_used, n_blocks


def _prepare_weights(mix_norm_w, w_in, conv_mix_w, conv_mix_norm_w, qkv_conv_w, a_log, dt_bias,
                     gdn_norm_w, w_out, ffn_norm_w, w_group, b_group, w_router, b_router):
    pad_lanes = lambda v: jnp.pad(v.reshape(1, -1), ((0, 0), (0, LANES - v.size)))
    w_in_bf = w_in.astype(BF16)
    w_ab = jnp.pad(w_in_bf[:, Z_COL0 + GDN_W:], ((0, 0), (0, LANES - 2 * HEADS)))
    grp = jnp.arange(PROJ_COLS) // CONV_GROUP_W
    gmat = jnp.where(grp[:, None] == grp[None, :], 1.0 / CONV_GROUP_W, 0.0).astype(BF16)
    w_route = jnp.concatenate([w_group, w_router.reshape(D_MODEL, N_EXPERTS)], axis=1)
    w_route = jnp.pad(w_route, ((0, 0), (0, LANES - w_route.shape[1])))
    w_route_hi = w_route.astype(BF16)
    return dict(
        mix_norm_w=mix_norm_w.reshape(1, -1),
        w_a=w_in_bf[:, :QKV_COL0].reshape(D_MODEL, 3, CONV_CH // PROJ_COLS, PROJ_COLS)
        .transpose(0, 2, 1, 3).reshape(D_MODEL, QKV_COL0),
        w_in=w_in_bf,
        w_ab=w_ab,
        conv_mix_w=conv_mix_w,
        conv_mix_norm_w=conv_mix_norm_w.reshape(1, -1),
        gmat=gmat,
        qkv_conv_w=qkv_conv_w,
        a_log=pad_lanes(a_log),
        dt_bias=pad_lanes(dt_bias),
        gdn_norm_w=gdn_norm_w.reshape(1, -1),
        w_out_a=w_out[:CONV_CH].astype(BF16),
        w_out_b=w_out[CONV_CH:].astype(BF16),
        ffn_norm_w=ffn_norm_w.reshape(1, -1),
        w_route_hi=w_route_hi,
        w_route_lo=(w_route - w_route_hi.astype(F32)).astype(BF16),
        b_route=pad_lanes(jnp.concatenate([b_group, b_router.reshape(-1)])),
    )


def _tile(n, preferred):
    return preferred if n % preferred == 0 else n


def kernel(x, meta_tokens, mix_norm_w, w_in, conv_mix_w, conv_mix_norm_w, qkv_conv_w, a_log,
           dt_bias, gdn_norm_w, w_out, ffn_norm_w, w_group, b_group, w_router, b_router, w_gate,
           w_up, w_down, final_norm_w):
    assert mix_norm_w.shape[0] == 1, "single-layer kernel"
    batch, seq, _ = x.shape
    assert seq % CHUNK == 0
    w = _prepare_weights(mix_norm_w[0], w_in[0], conv_mix_w[0], conv_mix_norm_w[0], qkv_conv_w[0],
                         a_log[0], dt_bias[0], gdn_norm_w[0], w_out[0], ffn_norm_w[0], w_group[0],
                         b_group[0], w_router[0], b_router[0])

    prefix = jnp.concatenate([jnp.zeros((CHUNK - N_META, D_MODEL), x.dtype),
                              meta_tokens.astype(x.dtype)], axis=0)[None]
    zero_cu = jnp.zeros((HIST, CONV_CH), F32)
    zero_qkv = jnp.zeros((HIST, 3 * GDN_W), F32)
    _, pq, pk, pv, _, pgb, tail_cu, tail_qkv = _proj_call(prefix, zero_cu, zero_qkv, w, CHUNK)

    ya, q, k, v, z, gb, _, _ = _proj_call(x, tail_cu[0], tail_qkv[0], w, _tile(seq, PROJ_ROWS))
    o = _gdn_call(pq, pk, pv, pgb, q, k, v, gb, _tile(seq // CHUNK, GDN_CHUNKS))

    tokens = batch * seq
    flat = lambda a: a.reshape(tokens, a.shape[-1])
    h1, xn2, route, assign, counts = _mix_out_call(flat(x), flat(ya), flat(o), flat(z), w,
                                           _tile(tokens, MIX_ROWS))

    dest, block_expert, block_valid, n_used, n_blocks = _dispatch_tables(assign, counts, tokens,
                                                                         EXPERT_ROWS)
    x_rows = _sc_row_move(xn2, dest, n_blocks * EXPERT_ROWS, SC_DISPATCH_CHUNK, True,
                          "dispatch_scatter")
    y_rows = _experts_call(block_expert, block_valid, n_used, x_rows, w_gate[0], w_up[0],
                           w_down[0], EXPERT_ROWS)
    y_tok = _sc_row_move(y_rows, dest, TOP_K * tokens, SC_COMBINE_CHUNK, False, "combine_gather")
    out = _combine_call(y_tok, h1, route, final_norm_w.reshape(1, -1), _tile(tokens, COMBINE_ROWS))
    return out.reshape(batch, seq, D_MODEL)
```

```python
import functools

import jax
import jax.numpy as jnp
from jax import lax
from jax.experimental import pallas as pl
from jax.experimental.pallas import tpu as pltpu
from jax.experimental.pallas import tpu_sc as plsc

F32 = jnp.float32
BF16 = jnp.bfloat16
EPS = 1e-6

D_MODEL = 1024
N_META = 16
CONV_CH = 512
CONV_GROUP_W = 64
HEADS = 4
HEAD_DIM = 128
GDN_W = HEADS * HEAD_DIM
QKV_COL0 = 3 * CONV_CH
Z_COL0 = QKV_COL0 + 3 * GDN_W
CHUNK = 64
N_GROUPS = 4
EXPERTS_PER_GROUP = 8
N_EXPERTS = N_GROUPS * EXPERTS_PER_GROUP
TOP_K = 2
D_EXPERT = 512
LANES = 128
ROUTE_E1, ROUTE_E2, ROUTE_RANK1, ROUTE_RANK2, ROUTE_W1, ROUTE_W2 = range(6)
ASSIGN_ROWS = 8
HIST = 8

PROJ_ROWS = 512
PROJ_COLS = 256
GDN_CHUNKS = 8
GDN_GROUP = 4
MIX_ROWS = 512
MIX_SUB_ROWS = 128
EXPERT_ROWS = 256
COMBINE_ROWS = 1024
VMEM_LIMIT = 56 * 1024 * 1024
SC_CORES = 2
SC_SUBCORES = 16
SC_DISPATCH_CHUNK = 64
SC_COMBINE_CHUNK = 64


def _dot(a, b):
    return jnp.dot(a, b, preferred_element_type=F32)


def _dot_nt(a, b):
    return lax.dot_general(a, b, (((1,), (1,)), ((), ())), preferred_element_type=F32)


def _dot_tn(a, b):
    return lax.dot_general(a, b, (((0,), (0,)), ((), ())), preferred_element_type=F32)


def _split_bf16(x):
    hi = x.astype(BF16)
    lo = (x - hi.astype(F32)).astype(BF16)
    return hi, lo


def _sigmoid(x):
    return 1.0 / (1.0 + jnp.exp(-x))


def _pack_bf16_pairs(x_bf16):
    bits = pltpu.bitcast(x_bf16.astype(F32), jnp.uint32)
    n = x_bf16.shape[1] // 2
    return (bits[:, :n] >> 16) | (bits[:, n:] & jnp.uint32(0xFFFF0000))


def _unpack_bf16_pairs(packed):
    return (pltpu.bitcast(packed << 16, F32),
            pltpu.bitcast(packed & jnp.uint32(0xFFFF0000), F32))


def _proj_kernel(x_ref, hcu_ref, hqkv_ref, nw_ref, wa_ref, wi_ref, wab_ref, cmw_ref,
                 cmn_ref, gmat_ref, qcw_ref, alog_ref, dtb_ref,
                 ya_ref, q_ref, k_ref, v_ref, z_ref, gb_ref, tcu_ref, tqkv_ref,
                 cu_s, qkv_s, xn_s):
    rows = x_ref.shape[0]

    @pl.when(pl.program_id(1) == 0)
    def _():
        cu_s[0:HIST, :] = hcu_ref[...]
        qkv_s[0:HIST, :] = hqkv_ref[...]

    x = x_ref[...]
    ms = jnp.mean(x * x, axis=-1, keepdims=True)
    xn_s[...] = (x * lax.rsqrt(ms + EPS) * nw_ref[...]).astype(BF16)

    def causal_conv(buf, cur, w_ref, cols, tail_ref):
        taps = w_ref.shape[0]
        buf[HIST:HIST + rows, cols] = cur
        acc = buf[pl.ds(HIST - taps + 1, rows), cols] * w_ref[0:1, cols]
        for j in range(1, taps - 1):
            acc = acc + buf[pl.ds(HIST - taps + 1 + j, rows), cols] * w_ref[j:j + 1, cols]
        acc = acc + cur * w_ref[taps - 1:taps, cols]
        tail = buf[rows:rows + HIST, cols]
        buf[0:HIST, cols] = tail
        tail_ref[:, cols] = tail
        return acc

    def mixer_a_tail(i, pa):
        cols = slice(i * PROJ_COLS, (i + 1) * PROJ_COLS)
        cu = pa[:, PROJ_COLS:2 * PROJ_COLS] * pa[:, 2 * PROJ_COLS:]
        ya = pa[:, :PROJ_COLS] * causal_conv(cu_s, cu, cmw_ref, cols, tcu_ref)
        sq_hi, sq_lo = _split_bf16(ya * ya)
        msg = _dot(sq_hi, gmat_ref[...]) + _dot(sq_lo, gmat_ref[...])
        ya_ref[:, cols] = (ya * lax.rsqrt(msg + EPS) * cmn_ref[:, cols]).astype(BF16)

    heads_per_chunk = PROJ_COLS // HEAD_DIM

    def qkv_tail(i, pq):
        cols = slice(i * PROJ_COLS, (i + 1) * PROJ_COLS)
        c = causal_conv(qkv_s, pq, qcw_ref, cols, tqkv_ref)
        c = c * _sigmoid(c)
        part, first_head = divmod(i * heads_per_chunk, HEADS)
        for j in range(heads_per_chunk):
            ch = c[:, j * HEAD_DIM:(j + 1) * HEAD_DIM]
            sl = slice((first_head + j) * HEAD_DIM, (first_head + j + 1) * HEAD_DIM)
            if part == 0:
                norm = lax.rsqrt(jnp.sum(ch * ch, axis=-1, keepdims=True) + EPS)
                q_ref[:, sl] = ch * norm * (HEAD_DIM ** -0.5)
            elif part == 1:
                k_ref[:, sl] = ch * lax.rsqrt(jnp.sum(ch * ch, axis=-1, keepdims=True) + EPS)
            else:
                v_ref[:, sl] = ch

    def z_tail(pz):
        z_ref[...] = pz

    def decay_beta_tail(ab):
        sp_in = ab + dtb_ref[...]
        softplus = jnp.maximum(sp_in, 0.0) + jnp.log1p(jnp.exp(-jnp.abs(sp_in)))
        g = -jnp.exp(alog_ref[...]) * softplus
        lane = lax.broadcasted_iota(jnp.int32, ab.shape, 1)
        gb_ref[...] = jnp.where(lane < HEADS, g, jnp.where(lane < 2 * HEADS, _sigmoid(ab), 0.0))

    stages = []
    for i in range(CONV_CH // PROJ_COLS):
        w_cols = slice(3 * i * PROJ_COLS, 3 * (i + 1) * PROJ_COLS)
        stages.append((functools.partial(lambda s: _dot(xn_s[...], wa_ref[:, s]), w_cols),
                       functools.partial(mixer_a_tail, i)))
    for i in range(3 * GDN_W // PROJ_COLS):
        w_cols = slice(QKV_COL0 + i * PROJ_COLS, QKV_COL0 + (i + 1) * PROJ_COLS)
        stages.append((functools.partial(lambda s: _dot(xn_s[...], wi_ref[:, s]), w_cols),
                       functools.partial(qkv_tail, i)))
    stages.append((lambda: _dot(xn_s[...], wi_ref[:, Z_COL0:Z_COL0 + GDN_W]), z_tail))
    stages.append((lambda: _dot(xn_s[...], wab_ref[...]), decay_beta_tail))
    pending = None
    for matmul, tail in stages:
        res = matmul()
        if pending is not None:
            pending()
        pending = functools.partial(tail, res)
    pending()


def _proj_call(x3, hist_cu, hist_qkv, w, rows):
    nb, seq, _ = x3.shape
    nt = seq // rows
    tok = lambda width: pl.BlockSpec((None, rows, width), lambda b, t: (b, t, 0))
    full = lambda a: pl.BlockSpec(a.shape, lambda b, t: (0,) * a.ndim)
    tail = lambda width: pl.BlockSpec((None, HIST, width), lambda b, t: (b, 0, 0))
    consts = (hist_cu, hist_qkv, w['mix_norm_w'], w['w_a'], w['w_in'], w['w_ab'],
              w['conv_mix_w'], w['conv_mix_norm_w'], w['gmat'], w['qkv_conv_w'], w['a_log'],
              w['dt_bias'])
    out_shape = (
        jax.ShapeDtypeStruct((nb, seq, CONV_CH), BF16),
        jax.ShapeDtypeStruct((nb, seq, GDN_W), F32),
        jax.ShapeDtypeStruct((nb, seq, GDN_W), F32),
        jax.ShapeDtypeStruct((nb, seq, GDN_W), F32),
        jax.ShapeDtypeStruct((nb, seq, GDN_W), F32),
        jax.ShapeDtypeStruct((nb, seq, LANES), F32),
        jax.ShapeDtypeStruct((nb, HIST, CONV_CH), F32),
        jax.ShapeDtypeStruct((nb, HIST, 3 * GDN_W), F32),
    )
    return pl.pallas_call(
        _proj_kernel,
        grid=(nb, nt),
        in_specs=[tok(D_MODEL)] + [full(a) for a in consts],
        out_specs=(tok(CONV_CH), tok(GDN_W), tok(GDN_W), tok(GDN_W), tok(GDN_W), tok(LANES),
                   tail(CONV_CH), tail(3 * GDN_W)),
        out_shape=out_shape,
        scratch_shapes=[pltpu.VMEM((rows + HIST, CONV_CH), F32),
                        pltpu.VMEM((rows + HIST, 3 * GDN_W), F32),
                        pltpu.VMEM((rows, D_MODEL), BF16)],
        compiler_params=pltpu.CompilerParams(
            dimension_semantics=("arbitrary", "arbitrary"), vmem_limit_bytes=VMEM_LIMIT),
        name="proj",
    )(x3, *consts)


def _chunk_masks():
    row = lax.broadcasted_iota(jnp.int32, (CHUNK, CHUNK), 0)
    col = lax.broadcasted_iota(jnp.int32, (CHUNK, CHUNK), 1)
    incl = row >= col
    strict = row > col
    levels = []
    n = 1
    while n < CHUNK:
        levels.append((row // (2 * n) == col // (2 * n)) & ((row // n) % 2 == 1) & ((col // n) % 2 == 0))
        n *= 2
    return incl, strict, levels


def _chunk_cumsum(gb_blk, incl):
    tri = incl.astype(BF16)
    hi, lo = _split_bf16(gb_blk)
    return _dot(tri, hi) + _dot(tri, lo)


def _chunk_transforms(chains, masks, state_only):
    incl, strict, levels = masks
    eye = (lax.broadcasted_iota(jnp.int32, (CHUNK, CHUNK), 0)
           == lax.broadcasted_iota(jnp.int32, (CHUNK, CHUNK), 1)).astype(F32)
    decay = [jnp.exp(jnp.where(incl, gc_col - gc_row, -jnp.inf))
             for (_, _, _, _, gc_col, gc_row, _) in chains]
    kb = [kh * beta for (_, kh, _, beta, _, _, _) in chains]
    k_bf = [kh.astype(BF16) for (_, kh, _, _, _, _, _) in chains]
    a_mat = [jnp.where(strict, _dot_nt(kb_i.astype(BF16), k_i) * d_i, 0.0)
             for kb_i, k_i, d_i in zip(kb, k_bf, decay)]
    t_inv = [eye - jnp.where(levels[0], a_i, 0.0) for a_i in a_mat]
    for lvl in levels[1:]:
        t_bf = [t_i.astype(BF16) for t_i in t_inv]
        m1 = [_dot(jnp.where(lvl, a_i, 0.0).astype(BF16), t_i) for a_i, t_i in zip(a_mat, t_bf)]
        t_inv = [t_i - _dot(tb_i, m_i.astype(BF16)) for t_i, tb_i, m_i in zip(t_inv, t_bf, m1)]
    rhs = [jnp.concatenate([vh * beta, kb_i * jnp.exp(gc_col)], axis=1)
           for (_, _, vh, beta, gc_col, _, _), kb_i in zip(chains, kb)]
    uw = [_dot(t_i.astype(BF16), r_i.astype(BF16)).astype(BF16)
          for t_i, r_i in zip(t_inv, rhs)]
    kd = [kh * jnp.exp(g_last - gc_col) for (_, kh, _, _, gc_col, _, g_last) in chains]
    pn = [_dot_tn(kd_i.astype(BF16), uw_i) for kd_i, uw_i in zip(kd, uw)]
    if state_only:
        return [pn_i[:, :HEAD_DIM] for pn_i in pn]
    intra = [jnp.where(incl, _dot_nt(qh.astype(BF16), k_i) * d_i, 0.0)
             for (qh, _, _, _, _, _, _), k_i, d_i in zip(chains, k_bf, decay)]
    iuw = [_dot(in_i.astype(BF16), uw_i) for in_i, uw_i in zip(intra, uw)]
    out = []
    for (qh, _, _, _, gc_col, _, g_last), pn_i, iuw_i in zip(chains, pn, iuw):
        q_part = qh * jnp.exp(gc_col) - iuw_i[:, HEAD_DIM:]
        out.append((q_part, pn_i[:, HEAD_DIM:], iuw_i[:, :HEAD_DIM], pn_i[:, :HEAD_DIM],
                    jnp.exp(g_last)))
    return out


def _gdn_kernel(pq_ref, pk_ref, pv_ref, pgb_ref, q_ref, k_ref, v_ref, gb_ref, o_ref,
                s_s, qp_s, op_s, n_s, a_s):
    nb = q_ref.shape[0]
    n_chunks = q_ref.shape[1] // CHUNK
    group = GDN_GROUP if n_chunks % GDN_GROUP == 0 else 1
    masks = _chunk_masks()

    def chains_of(gb_blk, q_blk, k_blk, v_blk):
        gc = _chunk_cumsum(gb_blk, masks[0])
        gc_t = gc.T
        res = []
        for h in range(HEADS):
            sl = slice(h * HEAD_DIM, (h + 1) * HEAD_DIM)
            res.append((q_blk(sl), k_blk(sl), v_blk(sl), gb_blk[:, HEADS + h:HEADS + h + 1],
                        gc[:, h:h + 1], gc_t[h:h + 1, :CHUNK], gc[CHUNK - 1:CHUNK, h:h + 1]))
        return res

    @pl.when(pl.program_id(0) == 0)
    def _():
        chains = chains_of(pgb_ref[...], lambda sl: pq_ref[:, sl], lambda sl: pk_ref[:, sl],
                           lambda sl: pv_ref[:, sl])
        for h, n_mat in enumerate(_chunk_transforms(chains, masks, True)):
            for b in range(nb):
                s_s[b * HEADS + h] = n_mat

    def transform_group(gi, carry):
        chains, where = [], []
        for cc in range(group):
            c = gi * group + cc
            rows = pl.ds(pl.multiple_of(c * CHUNK, CHUNK), CHUNK)
            for b in range(nb):
                chains += chains_of(gb_ref[b, rows, :], lambda sl: q_ref[b, rows, sl],
                                    lambda sl: k_ref[b, rows, sl], lambda sl: v_ref[b, rows, sl])
                where += [(c, b * HEADS + h) for h in range(HEADS)]
        for (c, ch), (q_part, p_mat, o_part, n_mat, a) in zip(
                where, _chunk_transforms(chains, masks, False)):
            qp_s[c, ch, 0:CHUNK, :] = q_part.astype(BF16)
            qp_s[c, ch, CHUNK:, :] = p_mat.astype(BF16)
            op_s[c, ch] = o_part
            n_s[c, ch] = n_mat
            a_s[c, ch] = jnp.broadcast_to(a, (8, HEAD_DIM))
        return carry

    lax.fori_loop(0, n_chunks // group, transform_group, 0)

    def scan_chunk(c, carry):
        r0 = pl.multiple_of(c * CHUNK, CHUNK)
        for b in range(nb):
            for h in range(HEADS):
                ch = b * HEADS + h
                s = s_s[ch]
                r = _dot(qp_s[c, ch], s.astype(BF16))
                o_ref[b, pl.ds(r0, CHUNK), h * HEAD_DIM:(h + 1) * HEAD_DIM] = r[:CHUNK] + op_s[c, ch]
                s_s[ch] = a_s[c, ch][0:1, :] * s - r[CHUNK:] + n_s[c, ch]
        return carry

    lax.fori_loop(0, n_chunks, scan_chunk, 0)


def _gdn_call(pq, pk, pv, pgb, q, k, v, gb, chunks_per_step):
    nb, seq, _ = q.shape
    rows = chunks_per_step * CHUNK
    steps = seq // rows
    tok = lambda width: pl.BlockSpec((nb, rows, width), lambda i: (0, i, 0))
    pre = lambda width: pl.BlockSpec((None, CHUNK, width), lambda i: (0, 0, 0))
    nch = nb * HEADS
    return pl.pallas_call(
        _gdn_kernel,
        grid=(steps,),
        in_specs=[pre(GDN_W), pre(GDN_W), pre(GDN_W), pre(LANES),
                  tok(GDN_W), tok(GDN_W), tok(GDN_W), tok(LANES)],
        out_specs=tok(GDN_W),
        out_shape=jax.ShapeDtypeStruct((nb, seq, GDN_W), F32),
        scratch_shapes=[
            pltpu.VMEM((nch, HEAD_DIM, HEAD_DIM), F32),
            pltpu.VMEM((chunks_per_step, nch, CHUNK + HEAD_DIM, HEAD_DIM), BF16),
            pltpu.VMEM((chunks_per_step, nch, CHUNK, HEAD_DIM), F32),
            pltpu.VMEM((chunks_per_step, nch, HEAD_DIM, HEAD_DIM), F32),
            pltpu.VMEM((chunks_per_step, nch, 8, HEAD_DIM), F32),
        ],
        compiler_params=pltpu.CompilerParams(
            dimension_semantics=("arbitrary",), vmem_limit_bytes=VMEM_LIMIT),
        name="gdn",
    )(pq, pk, pv, pgb, q, k, v, gb)


def _mix_out_kernel(x_ref, ya_ref, o_ref, z_ref, gnw_ref, woa_ref, wob_ref, fnw_ref, wr_ref,
                    wrh_ref, br_ref, h_ref, xn_ref, route_ref, assign_ref, counts_ref, cnt_s):
    rows = x_ref.shape[0]
    sub = MIX_SUB_ROWS if rows % MIX_SUB_ROWS == 0 else rows
    blocks = [slice(r, r + sub) for r in range(0, rows, sub)]

    def gated_heads(rs):
        yb = []
        for h in range(HEADS):
            sl = slice(h * HEAD_DIM, (h + 1) * HEAD_DIM)
            oh = o_ref[rs, sl]
            zh = z_ref[rs, sl]
            on = oh * lax.rsqrt(jnp.mean(oh * oh, axis=-1, keepdims=True) + EPS) * gnw_ref[...]
            yb.append((on * (zh * _sigmoid(zh))).astype(BF16))
        return jnp.concatenate(yb, axis=1)

    yb = [gated_heads(rs) for rs in blocks]
    h1 = [x_ref[rs, :] + (_dot(ya_ref[rs, :], woa_ref[...]) + _dot(yb_i, wob_ref[...]))
          for rs, yb_i in zip(blocks, yb)]
    for rs, h1_i in zip(blocks, h1):
        h_ref[rs, :] = h1_i
    xn = [h1_i * lax.rsqrt(jnp.mean(h1_i * h1_i, axis=-1, keepdims=True) + EPS) * fnw_ref[...]
          for h1_i in h1]
    split = [_split_bf16(xn_i) for xn_i in xn]
    for rs, (x_hi, _) in zip(blocks, split):
        xn_ref[rs, :] = _pack_bf16_pairs(x_hi)
    hi_part = [_dot(x_hi, wr_ref[...]) for x_hi, _ in split]
    lo_part = [_dot(x_lo, wrh_ref[...]) for _, x_lo in split]
    logits = [hp[:, :LANES] + hp[:, LANES:] + lp + br_ref[...] for hp, lp in zip(hi_part, lo_part)]

    lane = lax.broadcasted_iota(jnp.int32, (sub, LANES), 1).astype(F32)
    neg = -jnp.inf
    big = float(1 << 20)

    def argmax_first(vals):
        m = jnp.max(vals, axis=-1, keepdims=True)
        idx = jnp.min(jnp.where(vals == m, lane, big), axis=-1, keepdims=True)
        return m, idx

    def top_k(lg):
        grp = jnp.where(lane < N_GROUPS, lg, neg)
        g_max, g_sel = argmax_first(grp)
        p_grp = 1.0 / jnp.sum(jnp.exp(grp - g_max), axis=-1, keepdims=True)
        lo_lane = N_GROUPS + g_sel * EXPERTS_PER_GROUP
        ex = jnp.where((lane >= lo_lane) & (lane < lo_lane + EXPERTS_PER_GROUP), lg, neg)
        m1, i1 = argmax_first(ex)
        m2, i2 = argmax_first(jnp.where(lane == i1, neg, ex))
        e2 = jnp.exp(m2 - m1)
        return i1 - N_GROUPS, i2 - N_GROUPS, 1.0 / (1.0 + e2) * p_grp, e2 / (1.0 + e2) * p_grp

    picks = [top_k(lg) for lg in logits]

    @pl.when(pl.program_id(0) == 0)
    def _():
        cnt_s[...] = jnp.zeros(cnt_s.shape, F32)

    onehots = [((lane == e1).astype(F32), (lane == e2).astype(F32)) for e1, e2, _, _ in picks]
    earlier = (lax.broadcasted_iota(jnp.int32, (sub, sub), 0)
               > lax.broadcasted_iota(jnp.int32, (sub, sub), 1)).astype(BF16)
    within = [_dot(earlier, (oh1 + oh2).astype(BF16)) for oh1, oh2 in onehots]
    counts = cnt_s[...]
    for rs, (e1, e2, w1, w2), (oh1, oh2), within_i in zip(blocks, picks, onehots, within):
        before = within_i + counts
        rank1 = jnp.sum(before * oh1, axis=-1, keepdims=True)
        rank2 = jnp.sum(before * oh2, axis=-1, keepdims=True)
        counts = counts + jnp.sum(oh1 + oh2, axis=0, keepdims=True)
        route = jnp.zeros((sub, LANES), F32)
        for k, val in ((ROUTE_E1, e1), (ROUTE_E2, e2), (ROUTE_W1, w1), (ROUTE_W2, w2),
                       (ROUTE_RANK1, rank1), (ROUTE_RANK2, rank2)):
            route = jnp.where(lane == k, val, route)
        route_ref[rs, :] = route
        assign_ref[:, rs] = route.T[:ASSIGN_ROWS].astype(jnp.int32)
    cnt_s[...] = counts
    counts_ref[...] = counts


def _mix_out_call(x2, ya, o, z, w, rows):
    tokens = x2.shape[0]
    tok = lambda width: pl.BlockSpec((rows, width), lambda i: (i, 0))
    full = lambda a: pl.BlockSpec(a.shape, lambda i: (0,) * a.ndim)
    consts = (w['gdn_norm_w'], w['w_out_a'], w['w_out_b'], w['ffn_norm_w'], w['w_route'],
              w['w_route_hi'], w['b_route'])
    return pl.pallas_call(
        _mix_out_kernel,
        grid=(tokens // rows,),
        in_specs=[tok(D_MODEL), tok(CONV_CH), tok(GDN_W), tok(GDN_W)] + [full(a) for a in consts],
        out_specs=(tok(D_MODEL), tok(D_MODEL // 2), tok(LANES),
                   pl.BlockSpec((ASSIGN_ROWS, rows), lambda i: (0, i)),
                   pl.BlockSpec((1, LANES), lambda i: (0, 0))),
        out_shape=(jax.ShapeDtypeStruct((tokens, D_MODEL), F32),
                   jax.ShapeDtypeStruct((tokens, D_MODEL // 2), jnp.uint32),
                   jax.ShapeDtypeStruct((tokens, LANES), F32),
                   jax.ShapeDtypeStruct((ASSIGN_ROWS, tokens), jnp.int32),
                   jax.ShapeDtypeStruct((1, LANES), F32)),
        scratch_shapes=[pltpu.VMEM((1, LANES), F32)],
        compiler_params=pltpu.CompilerParams(
            dimension_semantics=("arbitrary",), vmem_limit_bytes=VMEM_LIMIT),
        name="mix_out",
    )(x2, ya, o, z, *consts)


def _sc_row_move(table, idx, out_rows, chunk, scatter, name):
    n = idx.shape[0]
    n_src, width = table.shape
    workers = SC_CORES * SC_SUBCORES
    per_w = n // workers
    assert n % workers == 0 and per_w % (2 * chunk) == 0
    assert n_src % per_w == 0 or not scatter
    pairs = per_w // (2 * chunk)
    mesh = plsc.VectorSubcoreMesh(core_axis_name="c", subcore_axis_name="s",
                                  num_cores=SC_CORES, num_subcores=SC_SUBCORES)

    def body(table_hbm, idx_hbm, out_hbm, idx_v, buf_a, buf_b, sem_ra, sem_rb, sem_wa, sem_wb):
        base = (lax.axis_index("s") * SC_CORES + lax.axis_index("c")) * per_w
        src_base = lax.rem(base, n_src)
        pltpu.sync_copy(idx_hbm.at[pl.ds(base, per_w)], idx_v)

        def read(c, buf, sem):
            off = pl.multiple_of(c * chunk, chunk)
            src = (table_hbm.at[pl.ds(src_base + off, chunk)] if scatter
                   else table_hbm.at[idx_v.at[pl.ds(off, chunk)]])
            return pltpu.make_async_copy(src, buf, sem)

        def write(c, buf, sem):
            off = pl.multiple_of(c * chunk, chunk)
            dst = (out_hbm.at[idx_v.at[pl.ds(off, chunk)]] if scatter
                   else out_hbm.at[pl.ds(base + off, chunk)])
            return pltpu.make_async_copy(buf, dst, sem)

        read(0, buf_a, sem_ra).start()

        @pl.loop(0, pairs)
        def _(j):
            ca = 2 * j
            cb = ca + 1
            read(cb, buf_b, sem_rb).start()
            read(ca, buf_a, sem_ra).wait()
            write(ca, buf_a, sem_wa).start()
            read(cb, buf_b, sem_rb).wait()
            write(cb, buf_b, sem_wb).start()
            write(ca, buf_a, sem_wa).wait()

            @pl.when(j + 1 < pairs)
            def _():
                read(ca + 2, buf_a, sem_ra).start()

            write(cb, buf_b, sem_wb).wait()

    return pl.kernel(
        body,
        out_type=jax.ShapeDtypeStruct((out_rows, width), table.dtype),
        mesh=mesh,
        scratch_types=[pltpu.VMEM((per_w,), jnp.int32),
                       pltpu.VMEM((chunk, width), table.dtype),
                       pltpu.VMEM((chunk, width), table.dtype),
                       pltpu.SemaphoreType.DMA, pltpu.SemaphoreType.DMA,
                       pltpu.SemaphoreType.DMA, pltpu.SemaphoreType.DMA],
        name=name,
    )(table, idx)


def _experts_kernel(bexp_ref, bvalid_ref, nused_ref, x_ref, wg_hbm, wu_hbm, wd_hbm, y_ref,
                    wg_s, wu_s, wd_s, stage_g, stage_u, stage_d, sems, seq_s):
    i = pl.program_id(0)
    n_used = nused_ref[0]
    n_last = bexp_ref.shape[0] - 1
    expert = bexp_ref[i]
    first_of_expert = (i == 0) | (expert != bexp_ref[jnp.maximum(i - 1, 0)])

    def weight_copies(e, slot):
        return (pltpu.make_async_copy(wg_hbm.at[e], stage_g.at[slot], sems.at[slot, 0]),
                pltpu.make_async_copy(wu_hbm.at[e], stage_u.at[slot], sems.at[slot, 1]),
                pltpu.make_async_copy(wd_hbm.at[e], stage_d.at[slot], sems.at[slot, 2]))

    @pl.when((i == 0) & (n_used > 0))
    def _():
        seq_s[0] = 0
        for copy in weight_copies(expert, 0):
            copy.start()

    @pl.when((i < n_used) & first_of_expert)
    def _():
        @pl.when(i > 0)
        def _():
            seq_s[0] = seq_s[0] + 1

        slot = seq_s[0] % 2
        nxt = lax.while_loop(lambda j: (j < n_used) & (bexp_ref[jnp.minimum(j, n_last)] == expert),
                             lambda j: j + 1, i + 1)

        @pl.when(nxt < n_used)
        def _():
            for copy in weight_copies(bexp_ref[jnp.minimum(nxt, n_last)], 1 - slot):
                copy.start()

        for copy in weight_copies(expert, slot):
            copy.wait()
        wg_s[...] = stage_g[slot].astype(BF16)
        wu_s[...] = stage_u[slot].astype(BF16)
        wd_s[...] = stage_d[slot].astype(BF16)

    @pl.when(i < n_used)
    def _():
        row = lax.broadcasted_iota(jnp.int32, x_ref.shape, 0)
        packed = jnp.where(row < bvalid_ref[i], x_ref[...], jnp.uint32(0))
        xb = jnp.concatenate(_unpack_bf16_pairs(packed), axis=1).astype(BF16)
        gate = _dot(xb, wg_s[...])
        hid = (gate * _sigmoid(gate)) * _dot(xb, wu_s[...])
        y_ref[...] = _pack_bf16_pairs(_dot(hid.astype(BF16), wd_s[...]).astype(BF16))

    @pl.when(i >= n_used)
    def _():
        y_ref[...] = jnp.zeros(y_ref.shape, jnp.uint32)


def _experts_call(block_expert, block_valid, n_used, x_rows, w_gate, w_up, w_down, rows):
    n_blocks = block_expert.shape[0]
    hbm = pl.BlockSpec(memory_space=pl.ANY)
    grid_spec = pltpu.PrefetchScalarGridSpec(
        num_scalar_prefetch=3,
        grid=(n_blocks,),
        in_specs=[pl.BlockSpec((rows, D_MODEL // 2), lambda i, be, bv, nu: (i, 0)), hbm, hbm, hbm],
        out_specs=pl.BlockSpec((rows, D_MODEL // 2), lambda i, be, bv, nu: (i, 0)),
        scratch_shapes=[pltpu.VMEM((D_MODEL, D_EXPERT), BF16),
                        pltpu.VMEM((D_MODEL, D_EXPERT), BF16),
                        pltpu.VMEM((D_EXPERT, D_MODEL), BF16),
                        pltpu.VMEM((2, D_MODEL, D_EXPERT), F32),
                        pltpu.VMEM((2, D_MODEL, D_EXPERT), F32),
                        pltpu.VMEM((2, D_EXPERT, D_MODEL), F32),
                        pltpu.SemaphoreType.DMA((2, 3)),
                        pltpu.SMEM((1,), jnp.int32)],
    )
    return pl.pallas_call(
        _experts_kernel,
        grid_spec=grid_spec,
        out_shape=jax.ShapeDtypeStruct((n_blocks * rows, D_MODEL // 2), jnp.uint32),
        compiler_params=pltpu.CompilerParams(
            dimension_semantics=("arbitrary",), vmem_limit_bytes=VMEM_LIMIT),
        name="experts",
    )(block_expert, block_valid, n_used, x_rows, w_gate, w_up, w_down)


def _combine_kernel(y1_ref, y2_ref, h_ref, route_ref, fw_ref, out_ref):
    route = route_ref[...]
    y1 = jnp.concatenate(_unpack_bf16_pairs(y1_ref[...]), axis=1)
    y2 = jnp.concatenate(_unpack_bf16_pairs(y2_ref[...]), axis=1)
    moe = route[:, ROUTE_W1:ROUTE_W1 + 1] * y1 + route[:, ROUTE_W2:ROUTE_W2 + 1] * y2
    h2 = h_ref[...] + moe
    out_ref[...] = h2 * lax.rsqrt(jnp.mean(h2 * h2, axis=-1, keepdims=True) + EPS) * fw_ref[...]


def _combine_call(y_tok, h1, route, final_w, rows):
    tokens = h1.shape[0]
    steps = tokens // rows
    tok = lambda width: pl.BlockSpec((rows, width), lambda i: (i, 0))
    return pl.pallas_call(
        _combine_kernel,
        grid=(steps,),
        in_specs=[tok(D_MODEL // 2), pl.BlockSpec((rows, D_MODEL // 2), lambda i: (i + steps, 0)),
                  tok(D_MODEL), tok(LANES), pl.BlockSpec((1, D_MODEL), lambda i: (0, 0))],
        out_specs=tok(D_MODEL),
        out_shape=jax.ShapeDtypeStruct((tokens, D_MODEL), F32),
        compiler_params=pltpu.CompilerParams(
            dimension_semantics=("arbitrary",), vmem_limit_bytes=VMEM_LIMIT),
        name="combine",
    )(y_tok, y_tok, h1, route, final_w)


def _dispatch_tables(assign, counts, tokens, rows):
    n_blocks = (tokens * TOP_K + N_EXPERTS * (rows - 1)) // rows
    counts = counts[0, :N_EXPERTS].astype(jnp.int32)
    padded = (counts + rows - 1) // rows * rows
    pad_end = jnp.cumsum(padded)
    pad_start = pad_end - padded
    experts = assign[ROUTE_E1:ROUTE_E2 + 1]
    ranks = assign[ROUTE_RANK1:ROUTE_RANK2 + 1]
    is_expert = experts[..., None] == jnp.arange(N_EXPERTS, dtype=jnp.int32)
    dest = (jnp.sum(jnp.where(is_expert, pad_start, 0), axis=-1) + ranks).reshape(-1)
    block_start = jnp.arange(n_blocks, dtype=jnp.int32) * rows
    block_expert = jnp.minimum(jnp.sum(block_start[:, None] >= pad_end[None, :], axis=1),
                               N_EXPERTS - 1).astype(jnp.int32)
    block_valid = jnp.clip(pad_start[block_expert] + counts[block_expert] - block_start, 0, rows)
    n_used = (pad_end[-1] // rows).astype(jnp.int32).reshape(1)
    return dest, block_expert, block_valid.astype(jnp.int32), n_used, n_blocks


def _prepare_weights(mix_norm_w, w_in, conv_mix_w, conv_mix_norm_w, qkv_conv_w, a_log, dt_bias,
                     gdn_norm_w, w_out, ffn_norm_w, w_group, b_group, w_router, b_router):
    pad_lanes = lambda v: jnp.pad(v.reshape(1, -1), ((0, 0), (0, LANES - v.size)))
    w_in_bf = w_in.astype(BF16)
    w_ab = jnp.pad(w_in_bf[:, Z_COL0 + GDN_W:], ((0, 0), (0, LANES - 2 * HEADS)))
    grp = jnp.arange(PROJ_COLS) // CONV_GROUP_W
    gmat = jnp.where(grp[:, None] == grp[None, :], 1.0 / CONV_GROUP_W, 0.0).astype(BF16)
    w_route = jnp.concatenate([w_group, w_router.reshape(D_MODEL, N_EXPERTS)], axis=1)
    w_route = jnp.pad(w_route, ((0, 0), (0, LANES - w_route.shape[1])))
    w_route_hi = w_route.astype(BF16)
    return dict(
        mix_norm_w=mix_norm_w.reshape(1, -1),
        w_a=w_in_bf[:, :QKV_COL0].reshape(D_MODEL, 3, CONV_CH // PROJ_COLS, PROJ_COLS)
        .transpose(0, 2, 1, 3).reshape(D_MODEL, QKV_COL0),
        w_in=w_in_bf,
        w_ab=w_ab,
        conv_mix_w=conv_mix_w,
        conv_mix_norm_w=conv_mix_norm_w.reshape(1, -1),
        gmat=gmat,
        qkv_conv_w=qkv_conv_w,
        a_log=pad_lanes(a_log),
        dt_bias=pad_lanes(dt_bias),
        gdn_norm_w=gdn_norm_w.reshape(1, -1),
        w_out_a=w_out[:CONV_CH].astype(BF16),
        w_out_b=w_out[CONV_CH:].astype(BF16),
        ffn_norm_w=ffn_norm_w.reshape(1, -1),
        w_route=jnp.concatenate([w_route_hi, (w_route - w_route_hi.astype(F32)).astype(BF16)], axis=1),
        w_route_hi=w_route_hi,
        b_route=pad_lanes(jnp.concatenate([b_group, b_router.reshape(-1)])),
    )


def _tile(n, preferred):
    return preferred if n % preferred == 0 else n


def kernel(x, meta_tokens, mix_norm_w, w_in, conv_mix_w, conv_mix_norm_w, qkv_conv_w, a_log,
           dt_bias, gdn_norm_w, w_out, ffn_norm_w, w_group, b_group, w_router, b_router, w_gate,
           w_up, w_down, final_norm_w):
    assert mix_norm_w.shape[0] == 1, "single-layer kernel"
    batch, seq, _ = x.shape
    assert seq % CHUNK == 0
    w = _prepare_weights(mix_norm_w[0], w_in[0], conv_mix_w[0], conv_mix_norm_w[0], qkv_conv_w[0],
                         a_log[0], dt_bias[0], gdn_norm_w[0], w_out[0], ffn_norm_w[0], w_group[0],
                         b_group[0], w_router[0], b_router[0])

    prefix = jnp.concatenate([jnp.zeros((CHUNK - N_META, D_MODEL), x.dtype),
                              meta_tokens.astype(x.dtype)], axis=0)[None]
    zero_cu = jnp.zeros((HIST, CONV_CH), F32)
    zero_qkv = jnp.zeros((HIST, 3 * GDN_W), F32)
    _, pq, pk, pv, _, pgb, tail_cu, tail_qkv = _proj_call(prefix, zero_cu, zero_qkv, w, CHUNK)

    ya, q, k, v, z, gb, _, _ = _proj_call(x, tail_cu[0], tail_qkv[0], w, _tile(seq, PROJ_ROWS))
    o = _gdn_call(pq, pk, pv, pgb, q, k, v, gb, _tile(seq // CHUNK, GDN_CHUNKS))

    tokens = batch * seq
    flat = lambda a: a.reshape(tokens, a.shape[-1])
    h1, xn2, route, assign, counts = _mix_out_call(flat(x), flat(ya), flat(o), flat(z), w,
                                           _tile(tokens, MIX_ROWS))

    dest, block_expert, block_valid, n_used, n_blocks = _dispatch_tables(assign, counts, tokens,
                                                                         EXPERT_ROWS)
    x_rows = _sc_row_move(xn2, dest, n_blocks * EXPERT_ROWS, SC_DISPATCH_CHUNK, True,
                          "dispatch_scatter")
    y_rows = _experts_call(block_expert, block_valid, n_used, x_rows, w_gate[0], w_up[0],
                           w_down[0], EXPERT_ROWS)
    y_tok = _sc_row_move(y_rows, dest, TOP_K * tokens, SC_COMBINE_CHUNK, False, "combine_gather")
    out = _combine_call(y_tok, h1, route, final_norm_w.reshape(1, -1), _tile(tokens, COMBINE_ROWS))
    return out.reshape(batch, seq, D_MODEL)
```

```python
import functools

import jax
import jax.numpy as jnp
from jax import lax
from jax.experimental import pallas as pl
from jax.experimental.pallas import tpu as pltpu
from jax.experimental.pallas import tpu_sc as plsc

F32 = jnp.float32
BF16 = jnp.bfloat16
EPS = 1e-6

D_MODEL = 1024
N_META = 16
CONV_CH = 512
CONV_GROUP_W = 64
HEADS = 4
HEAD_DIM = 128
GDN_W = HEADS * HEAD_DIM
QKV_COL0 = 3 * CONV_CH
Z_COL0 = QKV_COL0 + 3 * GDN_W
CHUNK = 64
N_GROUPS = 4
EXPERTS_PER_GROUP = 8
N_EXPERTS = N_GROUPS * EXPERTS_PER_GROUP
TOP_K = 2
D_EXPERT = 512
LANES = 128
ROUTE_E1, ROUTE_E2, ROUTE_RANK1, ROUTE_RANK2, ROUTE_W1, ROUTE_W2 = range(6)
ASSIGN_ROWS = 8
HIST = 8

PROJ_ROWS = 512
PROJ_COLS = 256
PROJ_SUB_ROWS = 256
GDN_CHUNKS = 8
GDN_GROUP = 4
MIX_ROWS = 512
MIX_SUB_ROWS = 128
EXPERT_ROWS = 256
EXPERT_SUB_ROWS = 128
COMBINE_ROWS = 1024
VMEM_LIMIT = 56 * 1024 * 1024
SC_CORES = 2
SC_SUBCORES = 16
SC_DISPATCH_CHUNK = 64
SC_COMBINE_CHUNK = 64


def _dot(a, b):
    return jnp.dot(a, b, preferred_element_type=F32)


def _dot_nt(a, b):
    return lax.dot_general(a, b, (((1,), (1,)), ((), ())), preferred_element_type=F32)


def _dot_tn(a, b):
    return lax.dot_general(a, b, (((0,), (0,)), ((), ())), preferred_element_type=F32)


def _split_bf16(x):
    hi = x.astype(BF16)
    lo = (x - hi.astype(F32)).astype(BF16)
    return hi, lo


def _sigmoid(x):
    return 1.0 / (1.0 + jnp.exp(-x))


def _pack_bf16_pairs(x_bf16):
    bits = pltpu.bitcast(x_bf16.astype(F32), jnp.uint32)
    n = x_bf16.shape[1] // 2
    return (bits[:, :n] >> 16) | (bits[:, n:] & jnp.uint32(0xFFFF0000))


def _unpack_bf16_pairs(packed):
    return (pltpu.bitcast(packed << 16, F32),
            pltpu.bitcast(packed & jnp.uint32(0xFFFF0000), F32))


def _proj_kernel(x_ref, hcu_ref, hqkv_ref, nw_ref, wa_ref, wi_ref, wab_ref, cmw_ref,
                 cmn_ref, gmat_ref, qcw_ref, alog_ref, dtb_ref,
                 ya_ref, q_ref, k_ref, v_ref, z_ref, gb_ref, tcu_ref, tqkv_ref,
                 cu_s, qkv_s, xn_s):
    rows = x_ref.shape[0]
    sub = PROJ_SUB_ROWS if rows % PROJ_SUB_ROWS == 0 else rows
    blocks = [slice(r, r + sub) for r in range(0, rows, sub)]

    @pl.when(pl.program_id(1) == 0)
    def _():
        cu_s[0:HIST, :] = hcu_ref[...]
        qkv_s[0:HIST, :] = hqkv_ref[...]

    x = x_ref[...]
    ms = jnp.mean(x * x, axis=-1, keepdims=True)
    xn_s[...] = (x * lax.rsqrt(ms + EPS) * nw_ref[...]).astype(BF16)

    def causal_conv(buf, cur, w_ref, cols, tail_ref):
        taps = w_ref.shape[0]
        acc = pltpu.roll(cur, taps - 1, axis=0) * w_ref[0:1, cols]
        for j in range(1, taps - 1):
            acc = acc + pltpu.roll(cur, taps - 1 - j, axis=0) * w_ref[j:j + 1, cols]
        acc = acc + cur * w_ref[taps - 1:taps, cols]
        buf[HIST:2 * HIST, cols] = cur[:HIST]
        seam = buf[pl.ds(HIST - taps + 1, HIST), cols] * w_ref[0:1, cols]
        for j in range(1, taps):
            seam = seam + buf[pl.ds(HIST - taps + 1 + j, HIST), cols] * w_ref[j:j + 1, cols]
        tail = cur[sub - HIST:]
        buf[0:HIST, cols] = tail
        tail_ref[:, cols] = tail
        return jnp.concatenate([seam, acc[HIST:]], axis=0)

    def mixer_a_tail(i, rs, pa):
        cols = slice(i * PROJ_COLS, (i + 1) * PROJ_COLS)
        cu = pa[:, PROJ_COLS:2 * PROJ_COLS] * pa[:, 2 * PROJ_COLS:]
        ya = pa[:, :PROJ_COLS] * causal_conv(cu_s, cu, cmw_ref, cols, tcu_ref)
        sq_hi, sq_lo = _split_bf16(ya * ya)
        msg = _dot(sq_hi, gmat_ref[...]) + _dot(sq_lo, gmat_ref[...])
        ya_ref[rs, cols] = (ya * lax.rsqrt(msg + EPS) * cmn_ref[:, cols]).astype(BF16)

    heads_per_chunk = PROJ_COLS // HEAD_DIM

    def qkv_tail(i, rs, pq):
        cols = slice(i * PROJ_COLS, (i + 1) * PROJ_COLS)
        c = causal_conv(qkv_s, pq, qcw_ref, cols, tqkv_ref)
        c = c * _sigmoid(c)
        part, first_head = divmod(i * heads_per_chunk, HEADS)
        for j in range(heads_per_chunk):
            ch = c[:, j * HEAD_DIM:(j + 1) * HEAD_DIM]
            sl = slice((first_head + j) * HEAD_DIM, (first_head + j + 1) * HEAD_DIM)
            if part == 0:
                norm = lax.rsqrt(jnp.sum(ch * ch, axis=-1, keepdims=True) + EPS)
                q_ref[rs, sl] = ch * norm * (HEAD_DIM ** -0.5)
            elif part == 1:
                k_ref[rs, sl] = ch * lax.rsqrt(jnp.sum(ch * ch, axis=-1, keepdims=True) + EPS)
            else:
                v_ref[rs, sl] = ch

    def z_tail(rs, pz):
        z_ref[rs, :] = pz

    def decay_beta_tail(rs, ab):
        sp_in = ab + dtb_ref[...]
        softplus = jnp.maximum(sp_in, 0.0) + jnp.log1p(jnp.exp(-jnp.abs(sp_in)))
        g = -jnp.exp(alog_ref[...]) * softplus
        lane = lax.broadcasted_iota(jnp.int32, ab.shape, 1)
        gb_ref[rs, :] = jnp.where(lane < HEADS, g, jnp.where(lane < 2 * HEADS, _sigmoid(ab), 0.0))

    def matmul(w_ref, w_cols, rs):
        return _dot(xn_s[rs, :], w_ref[:, w_cols])

    stages = []
    for i in range(CONV_CH // PROJ_COLS):
        w_cols = slice(3 * i * PROJ_COLS, 3 * (i + 1) * PROJ_COLS)
        stages += [(functools.partial(matmul, wa_ref, w_cols, rs),
                    functools.partial(mixer_a_tail, i, rs)) for rs in blocks]
    for i in range(3 * GDN_W // PROJ_COLS):
        w_cols = slice(QKV_COL0 + i * PROJ_COLS, QKV_COL0 + (i + 1) * PROJ_COLS)
        stages += [(functools.partial(matmul, wi_ref, w_cols, rs),
                    functools.partial(qkv_tail, i, rs)) for rs in blocks]
    stages += [(functools.partial(matmul, wi_ref, slice(Z_COL0, Z_COL0 + GDN_W), rs),
                functools.partial(z_tail, rs)) for rs in blocks]
    stages += [(functools.partial(matmul, wab_ref, slice(None), rs),
                functools.partial(decay_beta_tail, rs)) for rs in blocks]
    pending = None
    for issue, tail in stages:
        res = issue()
        if pending is not None:
            pending()
        pending = functools.partial(tail, res)
    pending()


def _proj_call(x3, hist_cu, hist_qkv, w, rows):
    nb, seq, _ = x3.shape
    nt = seq // rows
    tok = lambda width: pl.BlockSpec((None, rows, width), lambda b, t: (b, t, 0))
    full = lambda a: pl.BlockSpec(a.shape, lambda b, t: (0,) * a.ndim)
    tail = lambda width: pl.BlockSpec((None, HIST, width), lambda b, t: (b, 0, 0))
    consts = (hist_cu, hist_qkv, w['mix_norm_w'], w['w_a'], w['w_in'], w['w_ab'],
              w['conv_mix_w'], w['conv_mix_norm_w'], w['gmat'], w['qkv_conv_w'], w['a_log'],
              w['dt_bias'])
    out_shape = (
        jax.ShapeDtypeStruct((nb, seq, CONV_CH), BF16),
        jax.ShapeDtypeStruct((nb, seq, GDN_W), F32),
        jax.ShapeDtypeStruct((nb, seq, GDN_W), F32),
        jax.ShapeDtypeStruct((nb, seq, GDN_W), F32),
        jax.ShapeDtypeStruct((nb, seq, GDN_W), F32),
        jax.ShapeDtypeStruct((nb, seq, LANES), F32),
        jax.ShapeDtypeStruct((nb, HIST, CONV_CH), F32),
        jax.ShapeDtypeStruct((nb, HIST, 3 * GDN_W), F32),
    )
    return pl.pallas_call(
        _proj_kernel,
        grid=(nb, nt),
        in_specs=[tok(D_MODEL)] + [full(a) for a in consts],
        out_specs=(tok(CONV_CH), tok(GDN_W), tok(GDN_W), tok(GDN_W), tok(GDN_W), tok(LANES),
                   tail(CONV_CH), tail(3 * GDN_W)),
        out_shape=out_shape,
        scratch_shapes=[pltpu.VMEM((2 * HIST, CONV_CH), F32),
                        pltpu.VMEM((2 * HIST, 3 * GDN_W), F32),
                        pltpu.VMEM((rows, D_MODEL), BF16)],
        compiler_params=pltpu.CompilerParams(
            dimension_semantics=("arbitrary", "arbitrary"), vmem_limit_bytes=VMEM_LIMIT),
        name="proj",
    )(x3, *consts)


def _chunk_masks():
    row = lax.broadcasted_iota(jnp.int32, (CHUNK, CHUNK), 0)
    col = lax.broadcasted_iota(jnp.int32, (CHUNK, CHUNK), 1)
    incl = row >= col
    strict = row > col
    levels = []
    n = 1
    while n < CHUNK:
        levels.append((row // (2 * n) == col // (2 * n)) & ((row // n) % 2 == 1) & ((col // n) % 2 == 0))
        n *= 2
    return incl, strict, levels


def _chunk_cumsum(gb_blk, incl):
    tri = incl.astype(BF16)
    hi, lo = _split_bf16(gb_blk)
    return _dot(tri, hi) + _dot(tri, lo)


def _chunk_transforms(chains, masks, state_only):
    incl, strict, levels = masks
    eye = (lax.broadcasted_iota(jnp.int32, (CHUNK, CHUNK), 0)
           == lax.broadcasted_iota(jnp.int32, (CHUNK, CHUNK), 1)).astype(F32)
    decay = [jnp.exp(jnp.where(incl, gc_col - gc_row, -jnp.inf))
             for (_, _, _, _, gc_col, gc_row, _) in chains]
    kb = [kh * beta for (_, kh, _, beta, _, _, _) in chains]
    k_bf = [kh.astype(BF16) for (_, kh, _, _, _, _, _) in chains]
    a_mat = [jnp.where(strict, _dot_nt(kb_i.astype(BF16), k_i) * d_i, 0.0)
             for kb_i, k_i, d_i in zip(kb, k_bf, decay)]
    t_inv = [eye - jnp.where(levels[0], a_i, 0.0) for a_i in a_mat]
    for lvl in levels[1:]:
        t_bf = [t_i.astype(BF16) for t_i in t_inv]
        m1 = [_dot(jnp.where(lvl, a_i, 0.0).astype(BF16), t_i) for a_i, t_i in zip(a_mat, t_bf)]
        t_inv = [t_i - _dot(tb_i, m_i.astype(BF16)) for t_i, tb_i, m_i in zip(t_inv, t_bf, m1)]
    rhs = [jnp.concatenate([vh * beta, kb_i * jnp.exp(gc_col)], axis=1)
           for (_, _, vh, beta, gc_col, _, _), kb_i in zip(chains, kb)]
    uw = [_dot(t_i.astype(BF16), r_i.astype(BF16)).astype(BF16)
          for t_i, r_i in zip(t_inv, rhs)]
    kd = [kh * jnp.exp(g_last - gc_col) for (_, kh, _, _, gc_col, _, g_last) in chains]
    pn = [_dot_tn(kd_i.astype(BF16), uw_i) for kd_i, uw_i in zip(kd, uw)]
    if state_only:
        return [pn_i[:, :HEAD_DIM] for pn_i in pn]
    intra = [jnp.where(incl, _dot_nt(qh.astype(BF16), k_i) * d_i, 0.0)
             for (qh, _, _, _, _, _, _), k_i, d_i in zip(chains, k_bf, decay)]
    iuw = [_dot(in_i.astype(BF16), uw_i) for in_i, uw_i in zip(intra, uw)]
    out = []
    for (qh, _, _, _, gc_col, _, g_last), pn_i, iuw_i in zip(chains, pn, iuw):
        q_part = qh * jnp.exp(gc_col) - iuw_i[:, HEAD_DIM:]
        out.append((q_part, pn_i[:, HEAD_DIM:], iuw_i[:, :HEAD_DIM], pn_i[:, :HEAD_DIM],
                    jnp.exp(g_last)))
    return out


def _gdn_kernel(pq_ref, pk_ref, pv_ref, pgb_ref, q_ref, k_ref, v_ref, gb_ref, o_ref,
                s_s, qp_s, op_s, n_s, a_s):
    nb = q_ref.shape[0]
    n_chunks = q_ref.shape[1] // CHUNK
    group = GDN_GROUP if n_chunks % GDN_GROUP == 0 else 1
    masks = _chunk_masks()

    def chains_of(gb_blk, q_blk, k_blk, v_blk):
        gc = _chunk_cumsum(gb_blk, masks[0])
        gc_t = gc.T
        res = []
        for h in range(HEADS):
            sl = slice(h * HEAD_DIM, (h + 1) * HEAD_DIM)
            res.append((q_blk(sl), k_blk(sl), v_blk(sl), gb_blk[:, HEADS + h:HEADS + h + 1],
                        gc[:, h:h + 1], gc_t[h:h + 1, :CHUNK], gc[CHUNK - 1:CHUNK, h:h + 1]))
        return res

    @pl.when(pl.program_id(0) == 0)
    def _():
        chains = chains_of(pgb_ref[...], lambda sl: pq_ref[:, sl], lambda sl: pk_ref[:, sl],
                           lambda sl: pv_ref[:, sl])
        for h, n_mat in enumerate(_chunk_transforms(chains, masks, True)):
            for b in range(nb):
                s_s[b * HEADS + h] = n_mat

    def transform_group(gi, carry):
        chains, where = [], []
        for cc in range(group):
            c = gi * group + cc
            rows = pl.ds(pl.multiple_of(c * CHUNK, CHUNK), CHUNK)
            for b in range(nb):
                chains += chains_of(gb_ref[b, rows, :], lambda sl: q_ref[b, rows, sl],
                                    lambda sl: k_ref[b, rows, sl], lambda sl: v_ref[b, rows, sl])
                where += [(c, b * HEADS + h) for h in range(HEADS)]
        for (c, ch), (q_part, p_mat, o_part, n_mat, a) in zip(
                where, _chunk_transforms(chains, masks, False)):
            qp_s[c, ch, 0:CHUNK, :] = q_part.astype(BF16)
            qp_s[c, ch, CHUNK:, :] = p_mat.astype(BF16)
            op_s[c, ch] = o_part
            n_s[c, ch] = n_mat
            a_s[c, ch] = jnp.broadcast_to(a, (8, HEAD_DIM))
        return carry

    lax.fori_loop(0, n_chunks // group, transform_group, 0)

    def scan_chunk(c, carry):
        r0 = pl.multiple_of(c * CHUNK, CHUNK)
        for b in range(nb):
            for h in range(HEADS):
                ch = b * HEADS + h
                s = s_s[ch]
                r = _dot(qp_s[c, ch], s.astype(BF16))
                o_ref[b, pl.ds(r0, CHUNK), h * HEAD_DIM:(h + 1) * HEAD_DIM] = r[:CHUNK] + op_s[c, ch]
                s_s[ch] = a_s[c, ch][0:1, :] * s - r[CHUNK:] + n_s[c, ch]
        return carry

    lax.fori_loop(0, n_chunks, scan_chunk, 0)


def _gdn_call(pq, pk, pv, pgb, q, k, v, gb, chunks_per_step):
    nb, seq, _ = q.shape
    rows = chunks_per_step * CHUNK
    steps = seq // rows
    tok = lambda width: pl.BlockSpec((nb, rows, width), lambda i: (0, i, 0))
    pre = lambda width: pl.BlockSpec((None, CHUNK, width), lambda i: (0, 0, 0))
    nch = nb * HEADS
    return pl.pallas_call(
        _gdn_kernel,
        grid=(steps,),
        in_specs=[pre(GDN_W), pre(GDN_W), pre(GDN_W), pre(LANES),
                  tok(GDN_W), tok(GDN_W), tok(GDN_W), tok(LANES)],
        out_specs=tok(GDN_W),
        out_shape=jax.ShapeDtypeStruct((nb, seq, GDN_W), F32),
        scratch_shapes=[
            pltpu.VMEM((nch, HEAD_DIM, HEAD_DIM), F32),
            pltpu.VMEM((chunks_per_step, nch, CHUNK + HEAD_DIM, HEAD_DIM), BF16),
            pltpu.VMEM((chunks_per_step, nch, CHUNK, HEAD_DIM), F32),
            pltpu.VMEM((chunks_per_step, nch, HEAD_DIM, HEAD_DIM), F32),
            pltpu.VMEM((chunks_per_step, nch, 8, HEAD_DIM), F32),
        ],
        compiler_params=pltpu.CompilerParams(
            dimension_semantics=("arbitrary",), vmem_limit_bytes=VMEM_LIMIT),
        name="gdn",
    )(pq, pk, pv, pgb, q, k, v, gb)


def _mix_out_kernel(x_ref, ya_ref, o_ref, z_ref, gnw_ref, woa_ref, wob_ref, fnw_ref, wr_ref,
                    wrh_ref, br_ref, h_ref, xn_ref, route_ref, assign_ref, counts_ref, cnt_s):
    rows = x_ref.shape[0]
    sub = MIX_SUB_ROWS if rows % MIX_SUB_ROWS == 0 else rows
    blocks = [slice(r, r + sub) for r in range(0, rows, sub)]

    def gated_heads(rs):
        yb = []
        for h in range(HEADS):
            sl = slice(h * HEAD_DIM, (h + 1) * HEAD_DIM)
            oh = o_ref[rs, sl]
            zh = z_ref[rs, sl]
            on = oh * lax.rsqrt(jnp.mean(oh * oh, axis=-1, keepdims=True) + EPS) * gnw_ref[...]
            yb.append((on * (zh * _sigmoid(zh))).astype(BF16))
        return jnp.concatenate(yb, axis=1)

    yb = [gated_heads(rs) for rs in blocks]
    h1 = [x_ref[rs, :] + (_dot(ya_ref[rs, :], woa_ref[...]) + _dot(yb_i, wob_ref[...]))
          for rs, yb_i in zip(blocks, yb)]
    for rs, h1_i in zip(blocks, h1):
        h_ref[rs, :] = h1_i
    xn = [h1_i * lax.rsqrt(jnp.mean(h1_i * h1_i, axis=-1, keepdims=True) + EPS) * fnw_ref[...]
          for h1_i in h1]
    split = [_split_bf16(xn_i) for xn_i in xn]
    for rs, (x_hi, _) in zip(blocks, split):
        xn_ref[rs, :] = _pack_bf16_pairs(x_hi)
    hi_part = [_dot(x_hi, wr_ref[...]) for x_hi, _ in split]
    lo_part = [_dot(x_lo, wrh_ref[...]) for _, x_lo in split]
    logits = [hp[:, :LANES] + hp[:, LANES:] + lp + br_ref[...] for hp, lp in zip(hi_part, lo_part)]

    lane = lax.broadcasted_iota(jnp.int32, (sub, LANES), 1).astype(F32)
    neg = -jnp.inf
    big = float(1 << 20)

    def argmax_first(vals):
        m = jnp.max(vals, axis=-1, keepdims=True)
        idx = jnp.min(jnp.where(vals == m, lane, big), axis=-1, keepdims=True)
        return m, idx

    def top_k(lg):
        grp = jnp.where(lane < N_GROUPS, lg, neg)
        g_max, g_sel = argmax_first(grp)
        p_grp = 1.0 / jnp.sum(jnp.exp(grp - g_max), axis=-1, keepdims=True)
        lo_lane = N_GROUPS + g_sel * EXPERTS_PER_GROUP
        ex = jnp.where((lane >= lo_lane) & (lane < lo_lane + EXPERTS_PER_GROUP), lg, neg)
        m1, i1 = argmax_first(ex)
        m2, i2 = argmax_first(jnp.where(lane == i1, neg, ex))
        e2 = jnp.exp(m2 - m1)
        return i1 - N_GROUPS, i2 - N_GROUPS, 1.0 / (1.0 + e2) * p_grp, e2 / (1.0 + e2) * p_grp

    picks = [top_k(lg) for lg in logits]

    @pl.when(pl.program_id(0) == 0)
    def _():
        cnt_s[...] = jnp.zeros(cnt_s.shape, F32)

    onehots = [((lane == e1).astype(F32), (lane == e2).astype(F32)) for e1, e2, _, _ in picks]
    earlier = (lax.broadcasted_iota(jnp.int32, (sub, sub), 0)
               > lax.broadcasted_iota(jnp.int32, (sub, sub), 1)).astype(BF16)
    within = [_dot(earlier, (oh1 + oh2).astype(BF16)) for oh1, oh2 in onehots]
    counts = cnt_s[...]
    for rs, (e1, e2, w1, w2), (oh1, oh2), within_i in zip(blocks, picks, onehots, within):
        before = within_i + counts
        rank1 = jnp.sum(before * oh1, axis=-1, keepdims=True)
        rank2 = jnp.sum(before * oh2, axis=-1, keepdims=True)
        counts = counts + jnp.sum(oh1 + oh2, axis=0, keepdims=True)
        route = jnp.zeros((sub, LANES), F32)
        for k, val in ((ROUTE_E1, e1), (ROUTE_E2, e2), (ROUTE_W1, w1), (ROUTE_W2, w2),
                       (ROUTE_RANK1, rank1), (ROUTE_RANK2, rank2)):
            route = jnp.where(lane == k, val, route)
        route_ref[rs, :] = route
        assign_ref[:, rs] = route.T[:ASSIGN_ROWS].astype(jnp.int32)
    cnt_s[...] = counts
    counts_ref[...] = counts


def _mix_out_call(x2, ya, o, z, w, rows):
    tokens = x2.shape[0]
    tok = lambda width: pl.BlockSpec((rows, width), lambda i: (i, 0))
    full = lambda a: pl.BlockSpec(a.shape, lambda i: (0,) * a.ndim)
    consts = (w['gdn_norm_w'], w['w_out_a'], w['w_out_b'], w['ffn_norm_w'], w['w_route'],
              w['w_route_hi'], w['b_route'])
    return pl.pallas_call(
        _mix_out_kernel,
        grid=(tokens // rows,),
        in_specs=[tok(D_MODEL), tok(CONV_CH), tok(GDN_W), tok(GDN_W)] + [full(a) for a in consts],
        out_specs=(tok(D_MODEL), tok(D_MODEL // 2), tok(LANES),
                   pl.BlockSpec((ASSIGN_ROWS, rows), lambda i: (0, i)),
                   pl.BlockSpec((1, LANES), lambda i: (0, 0))),
        out_shape=(jax.ShapeDtypeStruct((tokens, D_MODEL), F32),
                   jax.ShapeDtypeStruct((tokens, D_MODEL // 2), jnp.uint32),
                   jax.ShapeDtypeStruct((tokens, LANES), F32),
                   jax.ShapeDtypeStruct((ASSIGN_ROWS, tokens), jnp.int32),
                   jax.ShapeDtypeStruct((1, LANES), F32)),
        scratch_shapes=[pltpu.VMEM((1, LANES), F32)],
        compiler_params=pltpu.CompilerParams(
            dimension_semantics=("arbitrary",), vmem_limit_bytes=VMEM_LIMIT),
        name="mix_out",
    )(x2, ya, o, z, *consts)


def _sc_row_move(table, idx, out_rows, chunk, scatter, name):
    n = idx.shape[0]
    n_src, width = table.shape
    workers = SC_CORES * SC_SUBCORES
    per_w = n // workers
    assert n % workers == 0 and per_w % (2 * chunk) == 0
    assert n_src % per_w == 0 or not scatter
    pairs = per_w // (2 * chunk)
    mesh = plsc.VectorSubcoreMesh(core_axis_name="c", subcore_axis_name="s",
                                  num_cores=SC_CORES, num_subcores=SC_SUBCORES)

    def body(table_hbm, idx_hbm, out_hbm, idx_v, buf_a, buf_b, sem_ra, sem_rb, sem_wa, sem_wb):
        base = (lax.axis_index("s") * SC_CORES + lax.axis_index("c")) * per_w
        src_base = lax.rem(base, n_src)
        pltpu.sync_copy(idx_hbm.at[pl.ds(base, per_w)], idx_v)

        def read(c, buf, sem):
            off = pl.multiple_of(c * chunk, chunk)
            src = (table_hbm.at[pl.ds(src_base + off, chunk)] if scatter
                   else table_hbm.at[idx_v.at[pl.ds(off, chunk)]])
            return pltpu.make_async_copy(src, buf, sem)

        def write(c, buf, sem):
            off = pl.multiple_of(c * chunk, chunk)
            dst = (out_hbm.at[idx_v.at[pl.ds(off, chunk)]] if scatter
                   else out_hbm.at[pl.ds(base + off, chunk)])
            return pltpu.make_async_copy(buf, dst, sem)

        read(0, buf_a, sem_ra).start()

        @pl.loop(0, pairs)
        def _(j):
            ca = 2 * j
            cb = ca + 1
            read(cb, buf_b, sem_rb).start()
            read(ca, buf_a, sem_ra).wait()
            write(ca, buf_a, sem_wa).start()
            read(cb, buf_b, sem_rb).wait()
            write(cb, buf_b, sem_wb).start()
            write(ca, buf_a, sem_wa).wait()

            @pl.when(j + 1 < pairs)
            def _():
                read(ca + 2, buf_a, sem_ra).start()

            write(cb, buf_b, sem_wb).wait()

    return pl.kernel(
        body,
        out_type=jax.ShapeDtypeStruct((out_rows, width), table.dtype),
        mesh=mesh,
        scratch_types=[pltpu.VMEM((per_w,), jnp.int32),
                       pltpu.VMEM((chunk, width), table.dtype),
                       pltpu.VMEM((chunk, width), table.dtype),
                       pltpu.SemaphoreType.DMA, pltpu.SemaphoreType.DMA,
                       pltpu.SemaphoreType.DMA, pltpu.SemaphoreType.DMA],
        name=name,
    )(table, idx)


def _experts_kernel(bexp_ref, bvalid_ref, nused_ref, x_ref, wg_hbm, wu_hbm, wd_hbm, y_ref,
                    wg_s, wu_s, wd_s, stage_g, stage_u, stage_d, sems, seq_s):
    i = pl.program_id(0)
    n_used = nused_ref[0]
    n_last = bexp_ref.shape[0] - 1
    expert = bexp_ref[i]
    first_of_expert = (i == 0) | (expert != bexp_ref[jnp.maximum(i - 1, 0)])

    def weight_copies(e, slot):
        return (pltpu.make_async_copy(wg_hbm.at[e], stage_g.at[slot], sems.at[slot, 0]),
                pltpu.make_async_copy(wu_hbm.at[e], stage_u.at[slot], sems.at[slot, 1]),
                pltpu.make_async_copy(wd_hbm.at[e], stage_d.at[slot], sems.at[slot, 2]))

    @pl.when((i == 0) & (n_used > 0))
    def _():
        seq_s[0] = 0
        for copy in weight_copies(expert, 0):
            copy.start()

    @pl.when((i < n_used) & first_of_expert)
    def _():
        @pl.when(i > 0)
        def _():
            seq_s[0] = seq_s[0] + 1

        slot = seq_s[0] % 2
        nxt = lax.while_loop(lambda j: (j < n_used) & (bexp_ref[jnp.minimum(j, n_last)] == expert),
                             lambda j: j + 1, i + 1)

        @pl.when(nxt < n_used)
        def _():
            for copy in weight_copies(bexp_ref[jnp.minimum(nxt, n_last)], 1 - slot):
                copy.start()

        for copy in weight_copies(expert, slot):
            copy.wait()
        wg_s[...] = stage_g[slot].astype(BF16)
        wu_s[...] = stage_u[slot].astype(BF16)
        wd_s[...] = stage_d[slot].astype(BF16)

    @pl.when(i < n_used)
    def _():
        rows = x_ref.shape[0]
        sub = EXPERT_SUB_ROWS if rows % EXPERT_SUB_ROWS == 0 else rows
        blocks = [slice(r, r + sub) for r in range(0, rows, sub)]
        row = lax.broadcasted_iota(jnp.int32, (sub, x_ref.shape[1]), 0)
        xb = [jnp.concatenate(_unpack_bf16_pairs(
                  jnp.where(row + rs.start < bvalid_ref[i], x_ref[rs, :], jnp.uint32(0))),
                  axis=1).astype(BF16) for rs in blocks]
        gate = [_dot(xb_i, wg_s[...]) for xb_i in xb]
        up = [_dot(xb_i, wu_s[...]) for xb_i in xb]
        hid = [((g_i * _sigmoid(g_i)) * u_i).astype(BF16) for g_i, u_i in zip(gate, up)]
        y = [_dot(h_i, wd_s[...]) for h_i in hid]
        for rs, y_i in zip(blocks, y):
            y_ref[rs, :] = _pack_bf16_pairs(y_i.astype(BF16))

    @pl.when(i >= n_used)
    def _():
        y_ref[...] = jnp.zeros(y_ref.shape, jnp.uint32)


def _experts_call(block_expert, block_valid, n_used, x_rows, w_gate, w_up, w_down, rows):
    n_blocks = block_expert.shape[0]
    hbm = pl.BlockSpec(memory_space=pl.ANY)
    grid_spec = pltpu.PrefetchScalarGridSpec(
        num_scalar_prefetch=3,
        grid=(n_blocks,),
        in_specs=[pl.BlockSpec((rows, D_MODEL // 2), lambda i, be, bv, nu: (i, 0)), hbm, hbm, hbm],
        out_specs=pl.BlockSpec((rows, D_MODEL // 2), lambda i, be, bv, nu: (i, 0)),
        scratch_shapes=[pltpu.VMEM((D_MODEL, D_EXPERT), BF16),
                        pltpu.VMEM((D_MODEL, D_EXPERT), BF16),
                        pltpu.VMEM((D_EXPERT, D_MODEL), BF16),
                        pltpu.VMEM((2, D_MODEL, D_EXPERT), F32),
                        pltpu.VMEM((2, D_MODEL, D_EXPERT), F32),
                        pltpu.VMEM((2, D_EXPERT, D_MODEL), F32),
                        pltpu.SemaphoreType.DMA((2, 3)),
                        pltpu.SMEM((1,), jnp.int32)],
    )
    return pl.pallas_call(
        _experts_kernel,
        grid_spec=grid_spec,
        out_shape=jax.ShapeDtypeStruct((n_blocks * rows, D_MODEL // 2), jnp.uint32),
        compiler_params=pltpu.CompilerParams(
            dimension_semantics=("arbitrary",), vmem_limit_bytes=VMEM_LIMIT),
        name="experts",
    )(block_expert, block_valid, n_used, x_rows, w_gate, w_up, w_down)


def _combine_kernel(y1_ref, y2_ref, h_ref, route_ref, fw_ref, out_ref):
    route = route_ref[...]
    y1 = jnp.concatenate(_unpack_bf16_pairs(y1_ref[...]), axis=1)
    y2 = jnp.concatenate(_unpack_bf16_pairs(y2_ref[...]), axis=1)
    moe = route[:, ROUTE_W1:ROUTE_W1 + 1] * y1 + route[:, ROUTE_W2:ROUTE_W2 + 1] * y2
    h2 = h_ref[...] + moe
    out_ref[...] = h2 * lax.rsqrt(jnp.mean(h2 * h2, axis=-1, keepdims=True) + EPS) * fw_ref[...]


def _combine_call(y_tok, h1, route, final_w, rows):
    tokens = h1.shape[0]
    steps = tokens // rows
    tok = lambda width: pl.BlockSpec((rows, width), lambda i: (i, 0))
    return pl.pallas_call(
        _combine_kernel,
        grid=(steps,),
        in_specs=[tok(D_MODEL // 2), pl.BlockSpec((rows, D_MODEL // 2), lambda i: (i + steps, 0)),
                  tok(D_MODEL), tok(LANES), pl.BlockSpec((1, D_MODEL), lambda i: (0, 0))],
        out_specs=tok(D_MODEL),
        out_shape=jax.ShapeDtypeStruct((tokens, D_MODEL), F32),
        compiler_params=pltpu.CompilerParams(
            dimension_semantics=("arbitrary",), vmem_limit_bytes=VMEM_LIMIT),
        name="combine",
    )(y_tok, y_tok, h1, route, final_w)


def _dispatch_tables(assign, counts, tokens, rows):
    n_blocks = (tokens * TOP_K + N_EXPERTS * (rows - 1)) // rows
    counts = counts[0, :N_EXPERTS].astype(jnp.int32)
    padded = (counts + rows - 1) // rows * rows
    pad_end = jnp.cumsum(padded)
    pad_start = pad_end - padded
    experts = assign[ROUTE_E1:ROUTE_E2 + 1]
    ranks = assign[ROUTE_RANK1:ROUTE_RANK2 + 1]
    is_expert = experts[..., None] == jnp.arange(N_EXPERTS, dtype=jnp.int32)
    dest = (jnp.sum(jnp.where(is_expert, pad_start, 0), axis=-1) + ranks).reshape(-1)
    block_start = jnp.arange(n_blocks, dtype=jnp.int32) * rows
    block_expert = jnp.minimum(jnp.sum(block_start[:, None] >= pad_end[None, :], axis=1),
                               N_EXPERTS - 1).astype(jnp.int32)
    block_valid = jnp.clip(pad_start[block_expert] + counts[block_expert] - block_start, 0, rows)
    n_used = (pad_end[-1] // rows).astype(jnp.int32).reshape(1)
    return dest, block_expert, block_valid.astype(jnp.int32), n_used, n_blocks


def _prepare_weights(mix_norm_w, w_in, conv_mix_w, conv_mix_norm_w, qkv_conv_w, a_log, dt_bias,
                     gdn_norm_w, w_out, ffn_norm_w, w_group, b_group, w_router, b_router):
    pad_lanes = lambda v: jnp.pad(v.reshape(1, -1), ((0, 0), (0, LANES - v.size)))
    w_in_bf = w_in.astype(BF16)
    w_ab = jnp.pad(w_in_bf[:, Z_COL0 + GDN_W:], ((0, 0), (0, LANES - 2 * HEADS)))
    grp = jnp.arange(PROJ_COLS) // CONV_GROUP_W
    gmat = jnp.where(grp[:, None] == grp[None, :], 1.0 / CONV_GROUP_W, 0.0).astype(BF16)
    w_route = jnp.concatenate([w_group, w_router.reshape(D_MODEL, N_EXPERTS)], axis=1)
    w_route = jnp.pad(w_route, ((0, 0), (0, LANES - w_route.shape[1])))
    w_route_hi = w_route.astype(BF16)
    return dict(
        mix_norm_w=mix_norm_w.reshape(1, -1),
        w_a=w_in_bf[:, :QKV_COL0].reshape(D_MODEL, 3, CONV_CH // PROJ_COLS, PROJ_COLS)
        .transpose(0, 2, 1, 3).reshape(D_MODEL, QKV_COL0),
        w_in=w_in_bf,
        w_ab=w_ab,
        conv_mix_w=conv_mix_w,
        conv_mix_norm_w=conv_mix_norm_w.reshape(1, -1),
        gmat=gmat,
        qkv_conv_w=qkv_conv_w,
        a_log=pad_lanes(a_log),
        dt_bias=pad_lanes(dt_bias),
        gdn_norm_w=gdn_norm_w.reshape(1, -1),
        w_out_a=w_out[:CONV_CH].astype(BF16),
        w_out_b=w_out[CONV_CH:].astype(BF16),
        ffn_norm_w=ffn_norm_w.reshape(1, -1),
        w_route=jnp.concatenate([w_route_hi, (w_route - w_route_hi.astype(F32)).astype(BF16)], axis=1),
        w_route_hi=w_route_hi,
        b_route=pad_lanes(jnp.concatenate([b_group, b_router.reshape(-1)])),
    )


def _tile(n, preferred):
    return preferred if n % preferred == 0 else n


def kernel(x, meta_tokens, mix_norm_w, w_in, conv_mix_w, conv_mix_norm_w, qkv_conv_w, a_log,
           dt_bias, gdn_norm_w, w_out, ffn_norm_w, w_group, b_group, w_router, b_router, w_gate,
           w_up, w_down, final_norm_w):
    assert mix_norm_w.shape[0] == 1, "single-layer kernel"
    batch, seq, _ = x.shape
    assert seq % CHUNK == 0
    w = _prepare_weights(mix_norm_w[0], w_in[0], conv_mix_w[0], conv_mix_norm_w[0], qkv_conv_w[0],
                         a_log[0], dt_bias[0], gdn_norm_w[0], w_out[0], ffn_norm_w[0], w_group[0],
                         b_group[0], w_router[0], b_router[0])

    prefix = jnp.concatenate([jnp.zeros((CHUNK - N_META, D_MODEL), x.dtype),
                              meta_tokens.astype(x.dtype)], axis=0)[None]
    zero_cu = jnp.zeros((HIST, CONV_CH), F32)
    zero_qkv = jnp.zeros((HIST, 3 * GDN_W), F32)
    _, pq, pk, pv, _, pgb, tail_cu, tail_qkv = _proj_call(prefix, zero_cu, zero_qkv, w, CHUNK)

    ya, q, k, v, z, gb, _, _ = _proj_call(x, tail_cu[0], tail_qkv[0], w, _tile(seq, PROJ_ROWS))
    o = _gdn_call(pq, pk, pv, pgb, q, k, v, gb, _tile(seq // CHUNK, GDN_CHUNKS))

    tokens = batch * seq
    flat = lambda a: a.reshape(tokens, a.shape[-1])
    h1, xn2, route, assign, counts = _mix_out_call(flat(x), flat(ya), flat(o), flat(z), w,
                                           _tile(tokens, MIX_ROWS))

    dest, block_expert, block_valid, n_used, n_blocks = _dispatch_tables(assign, counts, tokens,
                                                                         EXPERT_ROWS)
    x_rows = _sc_row_move(xn2, dest, n_blocks * EXPERT_ROWS, SC_DISPATCH_CHUNK, True,
                          "dispatch_scatter")
    y_rows = _experts_call(block_expert, block_valid, n_used, x_rows, w_gate[0], w_up[0],
                           w_down[0], EXPERT_ROWS)
    y_tok = _sc_row_move(y_rows, dest, TOP_K * tokens, SC_COMBINE_CHUNK, False, "combine_gather")
    out = _combine_call(y_tok, h1, route, final_norm_w.reshape(1, -1), _tile(tokens, COMBINE_ROWS))
    return out.reshape(batch, seq, D_MODEL)
```

```python
import functools

import jax
import jax.numpy as jnp
from jax import lax
from jax.experimental import pallas as pl
from jax.experimental.pallas import tpu as pltpu
from jax.experimental.pallas import tpu_sc as plsc

F32 = jnp.float32
BF16 = jnp.bfloat16
EPS = 1e-6

D_MODEL = 1024
N_META = 16
CONV_CH = 512
CONV_GROUP_W = 64
HEADS = 4
HEAD_DIM = 128
GDN_W = HEADS * HEAD_DIM
QKV_COL0 = 3 * CONV_CH
Z_COL0 = QKV_COL0 + 3 * GDN_W
CHUNK = 64
N_GROUPS = 4
EXPERTS_PER_GROUP = 8
N_EXPERTS = N_GROUPS * EXPERTS_PER_GROUP
TOP_K = 2
D_EXPERT = 512
LANES = 128
ROUTE_E1, ROUTE_E2, ROUTE_RANK1, ROUTE_RANK2, ROUTE_W1, ROUTE_W2 = range(6)
ASSIGN_ROWS = 8
HIST = 8

PROJ_ROWS = 512
PROJ_COLS = 256
PROJ_SUB_ROWS = 256
GDN_CHUNKS = 8
GDN_GROUP = 4
MIX_ROWS = 512
MIX_SUB_ROWS = 128
EXPERT_ROWS = 256
EXPERT_SUB_ROWS = 128
EXPERT_STEP_BLOCKS = 2
COMBINE_ROWS = 1024
VMEM_LIMIT = 56 * 1024 * 1024
SC_CORES = 2
SC_SUBCORES = 16
SC_DISPATCH_CHUNK = 64
SC_COMBINE_CHUNK = 64


def _dot(a, b):
    return jnp.dot(a, b, preferred_element_type=F32)


def _dot_nt(a, b):
    return lax.dot_general(a, b, (((1,), (1,)), ((), ())), preferred_element_type=F32)


def _dot_tn(a, b):
    return lax.dot_general(a, b, (((0,), (0,)), ((), ())), preferred_element_type=F32)


def _split_bf16(x):
    hi = x.astype(BF16)
    lo = (x - hi.astype(F32)).astype(BF16)
    return hi, lo


def _sigmoid(x):
    return 1.0 / (1.0 + jnp.exp(-x))


def _pack_bf16_pairs(x_bf16):
    bits = pltpu.bitcast(x_bf16.astype(F32), jnp.uint32)
    n = x_bf16.shape[1] // 2
    return (bits[:, :n] >> 16) | (bits[:, n:] & jnp.uint32(0xFFFF0000))


def _unpack_bf16_pairs(packed):
    return (pltpu.bitcast(packed << 16, F32),
            pltpu.bitcast(packed & jnp.uint32(0xFFFF0000), F32))


def _proj_kernel(x_ref, hcu_ref, hqkv_ref, nw_ref, wa_ref, wi_ref, wab_ref, cmw_ref,
                 cmn_ref, gmat_ref, qcw_ref, alog_ref, dtb_ref,
                 ya_ref, q_ref, k_ref, v_ref, z_ref, gb_ref, tcu_ref, tqkv_ref,
                 cu_s, qkv_s, xn_s):
    rows = x_ref.shape[0]
    sub = PROJ_SUB_ROWS if rows % PROJ_SUB_ROWS == 0 else rows
    blocks = [slice(r, r + sub) for r in range(0, rows, sub)]

    @pl.when(pl.program_id(1) == 0)
    def _():
        cu_s[0:HIST, :] = hcu_ref[...]
        qkv_s[0:HIST, :] = hqkv_ref[...]

    x = x_ref[...]
    ms = jnp.mean(x * x, axis=-1, keepdims=True)
    xn_s[...] = (x * lax.rsqrt(ms + EPS) * nw_ref[...]).astype(BF16)

    def causal_conv(buf, cur, w_ref, cols, tail_ref):
        taps = w_ref.shape[0]
        acc = pltpu.roll(cur, taps - 1, axis=0) * w_ref[0:1, cols]
        for j in range(1, taps - 1):
            acc = acc + pltpu.roll(cur, taps - 1 - j, axis=0) * w_ref[j:j + 1, cols]
        acc = acc + cur * w_ref[taps - 1:taps, cols]
        buf[HIST:2 * HIST, cols] = cur[:HIST]
        seam = buf[pl.ds(HIST - taps + 1, HIST), cols] * w_ref[0:1, cols]
        for j in range(1, taps):
            seam = seam + buf[pl.ds(HIST - taps + 1 + j, HIST), cols] * w_ref[j:j + 1, cols]
        tail = cur[sub - HIST:]
        buf[0:HIST, cols] = tail
        tail_ref[:, cols] = tail
        return jnp.concatenate([seam, acc[HIST:]], axis=0)

    def mixer_a_tail(i, rs, pa):
        cols = slice(i * PROJ_COLS, (i + 1) * PROJ_COLS)
        cu = pa[:, PROJ_COLS:2 * PROJ_COLS] * pa[:, 2 * PROJ_COLS:]
        ya = pa[:, :PROJ_COLS] * causal_conv(cu_s, cu, cmw_ref, cols, tcu_ref)
        sq_hi, sq_lo = _split_bf16(ya * ya)
        msg = _dot(sq_hi, gmat_ref[...]) + _dot(sq_lo, gmat_ref[...])
        ya_ref[rs, cols] = (ya * lax.rsqrt(msg + EPS) * cmn_ref[:, cols]).astype(BF16)

    heads_per_chunk = PROJ_COLS // HEAD_DIM

    def qkv_tail(i, rs, pq):
        cols = slice(i * PROJ_COLS, (i + 1) * PROJ_COLS)
        c = causal_conv(qkv_s, pq, qcw_ref, cols, tqkv_ref)
        c = c * _sigmoid(c)
        part, first_head = divmod(i * heads_per_chunk, HEADS)
        for j in range(heads_per_chunk):
            ch = c[:, j * HEAD_DIM:(j + 1) * HEAD_DIM]
            sl = slice((first_head + j) * HEAD_DIM, (first_head + j + 1) * HEAD_DIM)
            if part == 0:
                norm = lax.rsqrt(jnp.sum(ch * ch, axis=-1, keepdims=True) + EPS)
                q_ref[rs, sl] = ch * norm * (HEAD_DIM ** -0.5)
            elif part == 1:
                k_ref[rs, sl] = ch * lax.rsqrt(jnp.sum(ch * ch, axis=-1, keepdims=True) + EPS)
            else:
                v_ref[rs, sl] = ch

    def z_tail(rs, pz):
        z_ref[rs, :] = pz

    def decay_beta_tail(rs, ab):
        sp_in = ab + dtb_ref[...]
        softplus = jnp.maximum(sp_in, 0.0) + jnp.log1p(jnp.exp(-jnp.abs(sp_in)))
        g = -jnp.exp(alog_ref[...]) * softplus
        lane = lax.broadcasted_iota(jnp.int32, ab.shape, 1)
        gb_ref[rs, :] = jnp.where(lane < HEADS, g, jnp.where(lane < 2 * HEADS, _sigmoid(ab), 0.0))

    def matmul(w_ref, w_cols, rs):
        return _dot(xn_s[rs, :], w_ref[:, w_cols])

    stages = []
    for i in range(CONV_CH // PROJ_COLS):
        w_cols = slice(3 * i * PROJ_COLS, 3 * (i + 1) * PROJ_COLS)
        stages += [(functools.partial(matmul, wa_ref, w_cols, rs),
                    functools.partial(mixer_a_tail, i, rs)) for rs in blocks]
    for i in range(3 * GDN_W // PROJ_COLS):
        w_cols = slice(QKV_COL0 + i * PROJ_COLS, QKV_COL0 + (i + 1) * PROJ_COLS)
        stages += [(functools.partial(matmul, wi_ref, w_cols, rs),
                    functools.partial(qkv_tail, i, rs)) for rs in blocks]
    stages += [(functools.partial(matmul, wi_ref, slice(Z_COL0, Z_COL0 + GDN_W), rs),
                functools.partial(z_tail, rs)) for rs in blocks]
    stages += [(functools.partial(matmul, wab_ref, slice(None), rs),
                functools.partial(decay_beta_tail, rs)) for rs in blocks]
    pending = None
    for issue, tail in stages:
        res = issue()
        if pending is not None:
            pending()
        pending = functools.partial(tail, res)
    pending()


def _proj_call(x3, hist_cu, hist_qkv, w, rows):
    nb, seq, _ = x3.shape
    nt = seq // rows
    tok = lambda width: pl.BlockSpec((None, rows, width), lambda b, t: (b, t, 0))
    full = lambda a: pl.BlockSpec(a.shape, lambda b, t: (0,) * a.ndim)
    tail = lambda width: pl.BlockSpec((None, HIST, width), lambda b, t: (b, 0, 0))
    consts = (hist_cu, hist_qkv, w['mix_norm_w'], w['w_a'], w['w_in'], w['w_ab'],
              w['conv_mix_w'], w['conv_mix_norm_w'], w['gmat'], w['qkv_conv_w'], w['a_log'],
              w['dt_bias'])
    out_shape = (
        jax.ShapeDtypeStruct((nb, seq, CONV_CH), BF16),
        jax.ShapeDtypeStruct((nb, seq, GDN_W), F32),
        jax.ShapeDtypeStruct((nb, seq, GDN_W), F32),
        jax.ShapeDtypeStruct((nb, seq, GDN_W), F32),
        jax.ShapeDtypeStruct((nb, seq, GDN_W), F32),
        jax.ShapeDtypeStruct((nb, seq, LANES), F32),
        jax.ShapeDtypeStruct((nb, HIST, CONV_CH), F32),
        jax.ShapeDtypeStruct((nb, HIST, 3 * GDN_W), F32),
    )
    return pl.pallas_call(
        _proj_kernel,
        grid=(nb, nt),
        in_specs=[tok(D_MODEL)] + [full(a) for a in consts],
        out_specs=(tok(CONV_CH), tok(GDN_W), tok(GDN_W), tok(GDN_W), tok(GDN_W), tok(LANES),
                   tail(CONV_CH), tail(3 * GDN_W)),
        out_shape=out_shape,
        scratch_shapes=[pltpu.VMEM((2 * HIST, CONV_CH), F32),
                        pltpu.VMEM((2 * HIST, 3 * GDN_W), F32),
                        pltpu.VMEM((rows, D_MODEL), BF16)],
        compiler_params=pltpu.CompilerParams(
            dimension_semantics=("arbitrary", "arbitrary"), vmem_limit_bytes=VMEM_LIMIT),
        name="proj",
    )(x3, *consts)


def _chunk_masks():
    row = lax.broadcasted_iota(jnp.int32, (CHUNK, CHUNK), 0)
    col = lax.broadcasted_iota(jnp.int32, (CHUNK, CHUNK), 1)
    incl = row >= col
    strict = row > col
    levels = []
    n = 1
    while n < CHUNK:
        levels.append((row // (2 * n) == col // (2 * n)) & ((row // n) % 2 == 1) & ((col // n) % 2 == 0))
        n *= 2
    return incl, strict, levels


def _chunk_cumsum(gb_blk, incl):
    tri = incl.astype(BF16)
    hi, lo = _split_bf16(gb_blk)
    return _dot(tri, hi) + _dot(tri, lo)


def _chunk_transforms(chains, masks, state_only):
    incl, strict, levels = masks
    eye = (lax.broadcasted_iota(jnp.int32, (CHUNK, CHUNK), 0)
           == lax.broadcasted_iota(jnp.int32, (CHUNK, CHUNK), 1)).astype(F32)
    decay = [jnp.exp(jnp.where(incl, gc_col - gc_row, -jnp.inf))
             for (_, _, _, _, gc_col, gc_row, _) in chains]
    kb = [kh * beta for (_, kh, _, beta, _, _, _) in chains]
    k_bf = [kh.astype(BF16) for (_, kh, _, _, _, _, _) in chains]
    a_mat = [jnp.where(strict, _dot_nt(kb_i.astype(BF16), k_i) * d_i, 0.0)
             for kb_i, k_i, d_i in zip(kb, k_bf, decay)]
    t_inv = [eye - jnp.where(levels[0], a_i, 0.0) for a_i in a_mat]
    for lvl in levels[1:]:
        t_bf = [t_i.astype(BF16) for t_i in t_inv]
        m1 = [_dot(jnp.where(lvl, a_i, 0.0).astype(BF16), t_i) for a_i, t_i in zip(a_mat, t_bf)]
        t_inv = [t_i - _dot(tb_i, m_i.astype(BF16)) for t_i, tb_i, m_i in zip(t_inv, t_bf, m1)]
    rhs = [jnp.concatenate([vh * beta, kb_i * jnp.exp(gc_col)], axis=1)
           for (_, _, vh, beta, gc_col, _, _), kb_i in zip(chains, kb)]
    uw = [_dot(t_i.astype(BF16), r_i.astype(BF16)).astype(BF16)
          for t_i, r_i in zip(t_inv, rhs)]
    kd = [kh * jnp.exp(g_last - gc_col) for (_, kh, _, _, gc_col, _, g_last) in chains]
    pn = [_dot_tn(kd_i.astype(BF16), uw_i) for kd_i, uw_i in zip(kd, uw)]
    if state_only:
        return [pn_i[:, :HEAD_DIM] for pn_i in pn]
    intra = [jnp.where(incl, _dot_nt(qh.astype(BF16), k_i) * d_i, 0.0)
             for (qh, _, _, _, _, _, _), k_i, d_i in zip(chains, k_bf, decay)]
    iuw = [_dot(in_i.astype(BF16), uw_i) for in_i, uw_i in zip(intra, uw)]
    out = []
    for (qh, _, _, _, gc_col, _, g_last), pn_i, iuw_i in zip(chains, pn, iuw):
        q_part = qh * jnp.exp(gc_col) - iuw_i[:, HEAD_DIM:]
        out.append((q_part, pn_i[:, HEAD_DIM:], iuw_i[:, :HEAD_DIM], pn_i[:, :HEAD_DIM],
                    jnp.exp(g_last)))
    return out


def _gdn_kernel(pq_ref, pk_ref, pv_ref, pgb_ref, q_ref, k_ref, v_ref, gb_ref, o_ref,
                s_s, qp_s, op_s, n_s, a_s):
    nb = q_ref.shape[0]
    n_chunks = q_ref.shape[1] // CHUNK
    group = GDN_GROUP if n_chunks % GDN_GROUP == 0 else 1
    masks = _chunk_masks()

    def chains_of(gb_blk, q_blk, k_blk, v_blk):
        gc = _chunk_cumsum(gb_blk, masks[0])
        gc_t = gc.T
        res = []
        for h in range(HEADS):
            sl = slice(h * HEAD_DIM, (h + 1) * HEAD_DIM)
            res.append((q_blk(sl), k_blk(sl), v_blk(sl), gb_blk[:, HEADS + h:HEADS + h + 1],
                        gc[:, h:h + 1], gc_t[h:h + 1, :CHUNK], gc[CHUNK - 1:CHUNK, h:h + 1]))
        return res

    @pl.when(pl.program_id(0) == 0)
    def _():
        chains = chains_of(pgb_ref[...], lambda sl: pq_ref[:, sl], lambda sl: pk_ref[:, sl],
                           lambda sl: pv_ref[:, sl])
        for h, n_mat in enumerate(_chunk_transforms(chains, masks, True)):
            for b in range(nb):
                s_s[b * HEADS + h] = n_mat

    def transform_group(gi, carry):
        chains, where = [], []
        for cc in range(group):
            c = gi * group + cc
            rows = pl.ds(pl.multiple_of(c * CHUNK, CHUNK), CHUNK)
            for b in range(nb):
                chains += chains_of(gb_ref[b, rows, :], lambda sl: q_ref[b, rows, sl],
                                    lambda sl: k_ref[b, rows, sl], lambda sl: v_ref[b, rows, sl])
                where += [(c, b * HEADS + h) for h in range(HEADS)]
        for (c, ch), (q_part, p_mat, o_part, n_mat, a) in zip(
                where, _chunk_transforms(chains, masks, False)):
            qp_s[c, ch, 0:CHUNK, :] = q_part.astype(BF16)
            qp_s[c, ch, CHUNK:, :] = p_mat.astype(BF16)
            op_s[c, ch] = o_part
            n_s[c, ch] = n_mat
            a_s[c, ch] = jnp.broadcast_to(a, (8, HEAD_DIM))
        return carry

    lax.fori_loop(0, n_chunks // group, transform_group, 0)

    def scan_chunk(c, carry):
        r0 = pl.multiple_of(c * CHUNK, CHUNK)
        for b in range(nb):
            for h in range(HEADS):
                ch = b * HEADS + h
                s = s_s[ch]
                r = _dot(qp_s[c, ch], s.astype(BF16))
                o_ref[b, pl.ds(r0, CHUNK), h * HEAD_DIM:(h + 1) * HEAD_DIM] = r[:CHUNK] + op_s[c, ch]
                s_s[ch] = a_s[c, ch][0:1, :] * s - r[CHUNK:] + n_s[c, ch]
        return carry

    lax.fori_loop(0, n_chunks, scan_chunk, 0)


def _gdn_call(pq, pk, pv, pgb, q, k, v, gb, chunks_per_step):
    nb, seq, _ = q.shape
    rows = chunks_per_step * CHUNK
    steps = seq // rows
    tok = lambda width: pl.BlockSpec((nb, rows, width), lambda i: (0, i, 0))
    pre = lambda width: pl.BlockSpec((None, CHUNK, width), lambda i: (0, 0, 0))
    nch = nb * HEADS
    return pl.pallas_call(
        _gdn_kernel,
        grid=(steps,),
        in_specs=[pre(GDN_W), pre(GDN_W), pre(GDN_W), pre(LANES),
                  tok(GDN_W), tok(GDN_W), tok(GDN_W), tok(LANES)],
        out_specs=tok(GDN_W),
        out_shape=jax.ShapeDtypeStruct((nb, seq, GDN_W), F32),
        scratch_shapes=[
            pltpu.VMEM((nch, HEAD_DIM, HEAD_DIM), F32),
            pltpu.VMEM((chunks_per_step, nch, CHUNK + HEAD_DIM, HEAD_DIM), BF16),
            pltpu.VMEM((chunks_per_step, nch, CHUNK, HEAD_DIM), F32),
            pltpu.VMEM((chunks_per_step, nch, HEAD_DIM, HEAD_DIM), F32),
            pltpu.VMEM((chunks_per_step, nch, 8, HEAD_DIM), F32),
        ],
        compiler_params=pltpu.CompilerParams(
            dimension_semantics=("arbitrary",), vmem_limit_bytes=VMEM_LIMIT),
        name="gdn",
    )(pq, pk, pv, pgb, q, k, v, gb)


def _mix_out_kernel(x_ref, ya_ref, o_ref, z_ref, gnw_ref, woa_ref, wob_ref, fnw_ref, wr_ref,
                    wrh_ref, br_ref, h_ref, xn_ref, route_ref, assign_ref, counts_ref, cnt_s):
    rows = x_ref.shape[0]
    sub = MIX_SUB_ROWS if rows % MIX_SUB_ROWS == 0 else rows
    blocks = [slice(r, r + sub) for r in range(0, rows, sub)]

    def gated_heads(rs):
        yb = []
        for h in range(HEADS):
            sl = slice(h * HEAD_DIM, (h + 1) * HEAD_DIM)
            oh = o_ref[rs, sl]
            zh = z_ref[rs, sl]
            on = oh * lax.rsqrt(jnp.mean(oh * oh, axis=-1, keepdims=True) + EPS) * gnw_ref[...]
            yb.append((on * (zh * _sigmoid(zh))).astype(BF16))
        return jnp.concatenate(yb, axis=1)

    yb = [gated_heads(rs) for rs in blocks]
    h1 = [x_ref[rs, :] + (_dot(ya_ref[rs, :], woa_ref[...]) + _dot(yb_i, wob_ref[...]))
          for rs, yb_i in zip(blocks, yb)]
    for rs, h1_i in zip(blocks, h1):
        h_ref[rs, :] = h1_i
    xn = [h1_i * lax.rsqrt(jnp.mean(h1_i * h1_i, axis=-1, keepdims=True) + EPS) * fnw_ref[...]
          for h1_i in h1]
    split = [_split_bf16(xn_i) for xn_i in xn]
    for rs, (x_hi, _) in zip(blocks, split):
        xn_ref[rs, :] = _pack_bf16_pairs(x_hi)
    hi_part = [_dot(x_hi, wr_ref[...]) for x_hi, _ in split]
    lo_part = [_dot(x_lo, wrh_ref[...]) for _, x_lo in split]
    logits = [hp[:, :LANES] + hp[:, LANES:] + lp + br_ref[...] for hp, lp in zip(hi_part, lo_part)]

    lane = lax.broadcasted_iota(jnp.int32, (sub, LANES), 1).astype(F32)
    neg = -jnp.inf
    big = float(1 << 20)

    def argmax_first(vals):
        m = jnp.max(vals, axis=-1, keepdims=True)
        idx = jnp.min(jnp.where(vals == m, lane, big), axis=-1, keepdims=True)
        return m, idx

    def top_k(lg):
        grp = jnp.where(lane < N_GROUPS, lg, neg)
        g_max, g_sel = argmax_first(grp)
        p_grp = 1.0 / jnp.sum(jnp.exp(grp - g_max), axis=-1, keepdims=True)
        lo_lane = N_GROUPS + g_sel * EXPERTS_PER_GROUP
        ex = jnp.where((lane >= lo_lane) & (lane < lo_lane + EXPERTS_PER_GROUP), lg, neg)
        m1, i1 = argmax_first(ex)
        m2, i2 = argmax_first(jnp.where(lane == i1, neg, ex))
        e2 = jnp.exp(m2 - m1)
        return i1 - N_GROUPS, i2 - N_GROUPS, 1.0 / (1.0 + e2) * p_grp, e2 / (1.0 + e2) * p_grp

    picks = [top_k(lg) for lg in logits]

    @pl.when(pl.program_id(0) == 0)
    def _():
        cnt_s[...] = jnp.zeros(cnt_s.shape, F32)

    onehots = [((lane == e1).astype(F32), (lane == e2).astype(F32)) for e1, e2, _, _ in picks]
    earlier = (lax.broadcasted_iota(jnp.int32, (sub, sub), 0)
               > lax.broadcasted_iota(jnp.int32, (sub, sub), 1)).astype(BF16)
    within = [_dot(earlier, (oh1 + oh2).astype(BF16)) for oh1, oh2 in onehots]
    counts = cnt_s[...]
    for rs, (e1, e2, w1, w2), (oh1, oh2), within_i in zip(blocks, picks, onehots, within):
        before = within_i + counts
        rank1 = jnp.sum(before * oh1, axis=-1, keepdims=True)
        rank2 = jnp.sum(before * oh2, axis=-1, keepdims=True)
        counts = counts + jnp.sum(oh1 + oh2, axis=0, keepdims=True)
        route = jnp.zeros((sub, LANES), F32)
        for k, val in ((ROUTE_E1, e1), (ROUTE_E2, e2), (ROUTE_W1, w1), (ROUTE_W2, w2),
                       (ROUTE_RANK1, rank1), (ROUTE_RANK2, rank2)):
            route = jnp.where(lane == k, val, route)
        route_ref[rs, :] = route
        assign_ref[:, rs] = route.T[:ASSIGN_ROWS].astype(jnp.int32)
    cnt_s[...] = counts
    counts_ref[...] = counts


def _mix_out_call(x2, ya, o, z, w, rows):
    tokens = x2.shape[0]
    tok = lambda width: pl.BlockSpec((rows, width), lambda i: (i, 0))
    full = lambda a: pl.BlockSpec(a.shape, lambda i: (0,) * a.ndim)
    consts = (w['gdn_norm_w'], w['w_out_a'], w['w_out_b'], w['ffn_norm_w'], w['w_route'],
              w['w_route_hi'], w['b_route'])
    return pl.pallas_call(
        _mix_out_kernel,
        grid=(tokens // rows,),
        in_specs=[tok(D_MODEL), tok(CONV_CH), tok(GDN_W), tok(GDN_W)] + [full(a) for a in consts],
        out_specs=(tok(D_MODEL), tok(D_MODEL // 2), tok(LANES),
                   pl.BlockSpec((ASSIGN_ROWS, rows), lambda i: (0, i)),
                   pl.BlockSpec((1, LANES), lambda i: (0, 0))),
        out_shape=(jax.ShapeDtypeStruct((tokens, D_MODEL), F32),
                   jax.ShapeDtypeStruct((tokens, D_MODEL // 2), jnp.uint32),
                   jax.ShapeDtypeStruct((tokens, LANES), F32),
                   jax.ShapeDtypeStruct((ASSIGN_ROWS, tokens), jnp.int32),
                   jax.ShapeDtypeStruct((1, LANES), F32)),
        scratch_shapes=[pltpu.VMEM((1, LANES), F32)],
        compiler_params=pltpu.CompilerParams(
            dimension_semantics=("arbitrary",), vmem_limit_bytes=VMEM_LIMIT),
        name="mix_out",
    )(x2, ya, o, z, *consts)


def _sc_row_move(table, idx, out_rows, chunk, scatter, name):
    n = idx.shape[0]
    n_src, width = table.shape
    workers = SC_CORES * SC_SUBCORES
    per_w = n // workers
    assert n % workers == 0 and per_w % (2 * chunk) == 0
    assert n_src % per_w == 0 or not scatter
    pairs = per_w // (2 * chunk)
    mesh = plsc.VectorSubcoreMesh(core_axis_name="c", subcore_axis_name="s",
                                  num_cores=SC_CORES, num_subcores=SC_SUBCORES)

    def body(table_hbm, idx_hbm, out_hbm, idx_v, buf_a, buf_b, sem_ra, sem_rb, sem_wa, sem_wb):
        base = (lax.axis_index("s") * SC_CORES + lax.axis_index("c")) * per_w
        src_base = lax.rem(base, n_src)
        pltpu.sync_copy(idx_hbm.at[pl.ds(base, per_w)], idx_v)

        def read(c, buf, sem):
            off = pl.multiple_of(c * chunk, chunk)
            src = (table_hbm.at[pl.ds(src_base + off, chunk)] if scatter
                   else table_hbm.at[idx_v.at[pl.ds(off, chunk)]])
            return pltpu.make_async_copy(src, buf, sem)

        def write(c, buf, sem):
            off = pl.multiple_of(c * chunk, chunk)
            dst = (out_hbm.at[idx_v.at[pl.ds(off, chunk)]] if scatter
                   else out_hbm.at[pl.ds(base + off, chunk)])
            return pltpu.make_async_copy(buf, dst, sem)

        read(0, buf_a, sem_ra).start()

        @pl.loop(0, pairs)
        def _(j):
            ca = 2 * j
            cb = ca + 1
            read(cb, buf_b, sem_rb).start()
            read(ca, buf_a, sem_ra).wait()
            write(ca, buf_a, sem_wa).start()
            read(cb, buf_b, sem_rb).wait()
            write(cb, buf_b, sem_wb).start()
            write(ca, buf_a, sem_wa).wait()

            @pl.when(j + 1 < pairs)
            def _():
                read(ca + 2, buf_a, sem_ra).start()

            write(cb, buf_b, sem_wb).wait()

    return pl.kernel(
        body,
        out_type=jax.ShapeDtypeStruct((out_rows, width), table.dtype),
        mesh=mesh,
        scratch_types=[pltpu.VMEM((per_w,), jnp.int32),
                       pltpu.VMEM((chunk, width), table.dtype),
                       pltpu.VMEM((chunk, width), table.dtype),
                       pltpu.SemaphoreType.DMA, pltpu.SemaphoreType.DMA,
                       pltpu.SemaphoreType.DMA, pltpu.SemaphoreType.DMA],
        name=name,
    )(table, idx)


def _experts_kernel(bexp_ref, bvalid_ref, nused_ref, x_ref, wg_hbm, wu_hbm, wd_hbm, y_ref,
                    wg_s, wu_s, wd_s, stage_g, stage_u, stage_d, sems, seq_s):
    n_used = nused_ref[0]
    n_last = bexp_ref.shape[0] - 1
    rows = x_ref.shape[0] // EXPERT_STEP_BLOCKS
    sub = EXPERT_SUB_ROWS if rows % EXPERT_SUB_ROWS == 0 else rows

    def weight_copies(e, slot):
        return (pltpu.make_async_copy(wg_hbm.at[e], stage_g.at[slot], sems.at[slot, 0]),
                pltpu.make_async_copy(wu_hbm.at[e], stage_u.at[slot], sems.at[slot, 1]),
                pltpu.make_async_copy(wd_hbm.at[e], stage_d.at[slot], sems.at[slot, 2]))

    def block(i, r0):
        expert = bexp_ref[i]
        first_of_expert = (i == 0) | (expert != bexp_ref[jnp.maximum(i - 1, 0)])

        @pl.when((i == 0) & (n_used > 0))
        def _():
            seq_s[0] = 0
            for copy in weight_copies(expert, 0):
                copy.start()

        @pl.when((i < n_used) & first_of_expert)
        def _():
            @pl.when(i > 0)
            def _():
                seq_s[0] = seq_s[0] + 1

            slot = seq_s[0] % 2
            nxt = lax.while_loop(
                lambda j: (j < n_used) & (bexp_ref[jnp.minimum(j, n_last)] == expert),
                lambda j: j + 1, i + 1)

            @pl.when(nxt < n_used)
            def _():
                for copy in weight_copies(bexp_ref[jnp.minimum(nxt, n_last)], 1 - slot):
                    copy.start()

            for copy in weight_copies(expert, slot):
                copy.wait()
            wg_s[...] = stage_g[slot].astype(BF16)
            wu_s[...] = stage_u[slot].astype(BF16)
            wd_s[...] = stage_d[slot].astype(BF16)

        @pl.when(i < n_used)
        def _():
            blocks = [slice(r, r + sub) for r in range(r0, r0 + rows, sub)]
            row = lax.broadcasted_iota(jnp.int32, (sub, x_ref.shape[1]), 0)
            xb = [jnp.concatenate(_unpack_bf16_pairs(
                      jnp.where(row + (rs.start - r0) < bvalid_ref[i], x_ref[rs, :], jnp.uint32(0))),
                      axis=1).astype(BF16) for rs in blocks]
            gate = [_dot(xb_i, wg_s[...]) for xb_i in xb]
            up = [_dot(xb_i, wu_s[...]) for xb_i in xb]
            hid = [((g_i * _sigmoid(g_i)) * u_i).astype(BF16) for g_i, u_i in zip(gate, up)]
            y = [_dot(h_i, wd_s[...]) for h_i in hid]
            for rs, y_i in zip(blocks, y):
                y_ref[rs, :] = _pack_bf16_pairs(y_i.astype(BF16))

        @pl.when(i >= n_used)
        def _():
            y_ref[r0:r0 + rows, :] = jnp.zeros((rows, y_ref.shape[1]), jnp.uint32)

    for j in range(EXPERT_STEP_BLOCKS):
        block(pl.program_id(0) * EXPERT_STEP_BLOCKS + j, j * rows)


def _experts_call(block_expert, block_valid, n_used, x_rows, w_gate, w_up, w_down, rows):
    n_blocks = block_expert.shape[0]
    step_rows = EXPERT_STEP_BLOCKS * rows
    hbm = pl.BlockSpec(memory_space=pl.ANY)
    grid_spec = pltpu.PrefetchScalarGridSpec(
        num_scalar_prefetch=3,
        grid=(n_blocks // EXPERT_STEP_BLOCKS,),
        in_specs=[pl.BlockSpec((step_rows, D_MODEL // 2), lambda i, be, bv, nu: (i, 0)),
                  hbm, hbm, hbm],
        out_specs=pl.BlockSpec((step_rows, D_MODEL // 2), lambda i, be, bv, nu: (i, 0)),
        scratch_shapes=[pltpu.VMEM((D_MODEL, D_EXPERT), BF16),
                        pltpu.VMEM((D_MODEL, D_EXPERT), BF16),
                        pltpu.VMEM((D_EXPERT, D_MODEL), BF16),
                        pltpu.VMEM((2, D_MODEL, D_EXPERT), F32),
                        pltpu.VMEM((2, D_MODEL, D_EXPERT), F32),
                        pltpu.VMEM((2, D_EXPERT, D_MODEL), F32),
                        pltpu.SemaphoreType.DMA((2, 3)),
                        pltpu.SMEM((1,), jnp.int32)],
    )
    return pl.pallas_call(
        _experts_kernel,
        grid_spec=grid_spec,
        out_shape=jax.ShapeDtypeStruct((n_blocks * rows, D_MODEL // 2), jnp.uint32),
        compiler_params=pltpu.CompilerParams(
            dimension_semantics=("arbitrary",), vmem_limit_bytes=VMEM_LIMIT),
        name="experts",
    )(block_expert, block_valid, n_used, x_rows, w_gate, w_up, w_down)


def _combine_kernel(y1_ref, y2_ref, h_ref, route_ref, fw_ref, out_ref):
    route = route_ref[...]
    y1 = jnp.concatenate(_unpack_bf16_pairs(y1_ref[...]), axis=1)
    y2 = jnp.concatenate(_unpack_bf16_pairs(y2_ref[...]), axis=1)
    moe = route[:, ROUTE_W1:ROUTE_W1 + 1] * y1 + route[:, ROUTE_W2:ROUTE_W2 + 1] * y2
    h2 = h_ref[...] + moe
    out_ref[...] = h2 * lax.rsqrt(jnp.mean(h2 * h2, axis=-1, keepdims=True) + EPS) * fw_ref[...]


def _combine_call(y_tok, h1, route, final_w, rows):
    tokens = h1.shape[0]
    steps = tokens // rows
    tok = lambda width: pl.BlockSpec((rows, width), lambda i: (i, 0))
    return pl.pallas_call(
        _combine_kernel,
        grid=(steps,),
        in_specs=[tok(D_MODEL // 2), pl.BlockSpec((rows, D_MODEL // 2), lambda i: (i + steps, 0)),
                  tok(D_MODEL), tok(LANES), pl.BlockSpec((1, D_MODEL), lambda i: (0, 0))],
        out_specs=tok(D_MODEL),
        out_shape=jax.ShapeDtypeStruct((tokens, D_MODEL), F32),
        compiler_params=pltpu.CompilerParams(
            dimension_semantics=("arbitrary",), vmem_limit_bytes=VMEM_LIMIT),
        name="combine",
    )(y_tok, y_tok, h1, route, final_w)


def _dispatch_tables(assign, counts, tokens, rows):
    n_blocks = pl.cdiv((tokens * TOP_K + N_EXPERTS * (rows - 1)) // rows,
                       EXPERT_STEP_BLOCKS) * EXPERT_STEP_BLOCKS
    counts = counts[0, :N_EXPERTS].astype(jnp.int32)
    padded = (counts + rows - 1) // rows * rows
    pad_end = jnp.cumsum(padded)
    pad_start = pad_end - padded
    experts = assign[ROUTE_E1:ROUTE_E2 + 1]
    ranks = assign[ROUTE_RANK1:ROUTE_RANK2 + 1]
    is_expert = experts[..., None] == jnp.arange(N_EXPERTS, dtype=jnp.int32)
    dest = (jnp.sum(jnp.where(is_expert, pad_start, 0), axis=-1) + ranks).reshape(-1)
    block_start = jnp.arange(n_blocks, dtype=jnp.int32) * rows
    block_expert = jnp.minimum(jnp.sum(block_start[:, None] >= pad_end[None, :], axis=1),
                               N_EXPERTS - 1).astype(jnp.int32)
    block_valid = jnp.clip(pad_start[block_expert] + counts[block_expert] - block_start, 0, rows)
    n_used = (pad_end[-1] // rows).astype(jnp.int32).reshape(1)
    return dest, block_expert, block_valid.astype(jnp.int32), n_used, n_blocks


def _prepare_weights(mix_norm_w, w_in, conv_mix_w, conv_mix_norm_w, qkv_conv_w, a_log, dt_bias,
                     gdn_norm_w, w_out, ffn_norm_w, w_group, b_group, w_router, b_router):
    pad_lanes = lambda v: jnp.pad(v.reshape(1, -1), ((0, 0), (0, LANES - v.size)))
    w_in_bf = w_in.astype(BF16)
    w_ab = jnp.pad(w_in_bf[:, Z_COL0 + GDN_W:], ((0, 0), (0, LANES - 2 * HEADS)))
    grp = jnp.arange(PROJ_COLS) // CONV_GROUP_W
    gmat = jnp.where(grp[:, None] == grp[None, :], 1.0 / CONV_GROUP_W, 0.0).astype(BF16)
    w_route = jnp.concatenate([w_group, w_router.reshape(D_MODEL, N_EXPERTS)], axis=1)
    w_route = jnp.pad(w_route, ((0, 0), (0, LANES - w_route.shape[1])))
    w_route_hi = w_route.astype(BF16)
    return dict(
        mix_norm_w=mix_norm_w.reshape(1, -1),
        w_a=w_in_bf[:, :QKV_COL0].reshape(D_MODEL, 3, CONV_CH // PROJ_COLS, PROJ_COLS)
        .transpose(0, 2, 1, 3).reshape(D_MODEL, QKV_COL0),
        w_in=w_in_bf,
        w_ab=w_ab,
        conv_mix_w=conv_mix_w,
        conv_mix_norm_w=conv_mix_norm_w.reshape(1, -1),
        gmat=gmat,
        qkv_conv_w=qkv_conv_w,
        a_log=pad_lanes(a_log),
        dt_bias=pad_lanes(dt_bias),
        gdn_norm_w=gdn_norm_w.reshape(1, -1),
        w_out_a=w_out[:CONV_CH].astype(BF16),
        w_out_b=w_out[CONV_CH:].astype(BF16),
        ffn_norm_w=ffn_norm_w.reshape(1, -1),
        w_route=jnp.concatenate([w_route_hi, (w_route - w_route_hi.astype(F32)).astype(BF16)], axis=1),
        w_route_hi=w_route_hi,
        b_route=pad_lanes(jnp.concatenate([b_group, b_router.reshape(-1)])),
    )


def _tile(n, preferred):
    return preferred if n % preferred == 0 else n


def kernel(x, meta_tokens, mix_norm_w, w_in, conv_mix_w, conv_mix_norm_w, qkv_conv_w, a_log,
           dt_bias, gdn_norm_w, w_out, ffn_norm_w, w_group, b_group, w_router, b_router, w_gate,
           w_up, w_down, final_norm_w):
    assert mix_norm_w.shape[0] == 1, "single-layer kernel"
    batch, seq, _ = x.shape
    assert seq % CHUNK == 0
    w = _prepare_weights(mix_norm_w[0], w_in[0], conv_mix_w[0], conv_mix_norm_w[0], qkv_conv_w[0],
                         a_log[0], dt_bias[0], gdn_norm_w[0], w_out[0], ffn_norm_w[0], w_group[0],
                         b_group[0], w_router[0], b_router[0])

    prefix = jnp.concatenate([jnp.zeros((CHUNK - N_META, D_MODEL), x.dtype),
                              meta_tokens.astype(x.dtype)], axis=0)[None]
    zero_cu = jnp.zeros((HIST, CONV_CH), F32)
    zero_qkv = jnp.zeros((HIST, 3 * GDN_W), F32)
    _, pq, pk, pv, _, pgb, tail_cu, tail_qkv = _proj_call(prefix, zero_cu, zero_qkv, w, CHUNK)

    ya, q, k, v, z, gb, _, _ = _proj_call(x, tail_cu[0], tail_qkv[0], w, _tile(seq, PROJ_ROWS))
    o = _gdn_call(pq, pk, pv, pgb, q, k, v, gb, _tile(seq // CHUNK, GDN_CHUNKS))

    tokens = batch * seq
    flat = lambda a: a.reshape(tokens, a.shape[-1])
    h1, xn2, route, assign, counts = _mix_out_call(flat(x), flat(ya), flat(o), flat(z), w,
                                           _tile(tokens, MIX_ROWS))

    dest, block_expert, block_valid, n_used, n_blocks = _dispatch_tables(assign, counts, tokens,
                                                                         EXPERT_ROWS)
    x_rows = _sc_row_move(xn2, dest, n_blocks * EXPERT_ROWS, SC_DISPATCH_CHUNK, True,
                          "dispatch_scatter")
    y_rows = _experts_call(block_expert, block_valid, n_used, x_rows, w_gate[0], w_up[0],
                           w_down[0], EXPERT_ROWS)
    y_tok = _sc_row_move(y_rows, dest, TOP_K * tokens, SC_COMBINE_CHUNK, False, "combine_gather")
    out = _combine_call(y_tok, h1, route, final_norm_w.reshape(1, -1), _tile(tokens, COMBINE_ROWS))
    return out.reshape(batch, seq, D_MODEL)
```

```python
import functools

import jax
import jax.numpy as jnp
from jax import lax
from jax.experimental import pallas as pl
from jax.experimental.pallas import tpu as pltpu
from jax.experimental.pallas import tpu_sc as plsc

F32 = jnp.float32
BF16 = jnp.bfloat16
EPS = 1e-6

D_MODEL = 1024
N_META = 16
CONV_CH = 512
CONV_GROUP_W = 64
HEADS = 4
HEAD_DIM = 128
GDN_W = HEADS * HEAD_DIM
QKV_COL0 = 3 * CONV_CH
Z_COL0 = QKV_COL0 + 3 * GDN_W
CHUNK = 64
N_GROUPS = 4
EXPERTS_PER_GROUP = 8
N_EXPERTS = N_GROUPS * EXPERTS_PER_GROUP
TOP_K = 2
D_EXPERT = 512
LANES = 128
ROUTE_E1, ROUTE_E2, ROUTE_RANK1, ROUTE_RANK2, ROUTE_W1, ROUTE_W2 = range(6)
ASSIGN_ROWS = 8
HIST = 8

PROJ_ROWS = 512
PROJ_COLS = 256
PROJ_SUB_ROWS = 256
GDN_CHUNKS = 8
GDN_GROUP = 4
MIX_ROWS = 1024
MIX_SUB_ROWS = 128
EXPERT_ROWS = 256
EXPERT_SUB_ROWS = 128
EXPERT_STEP_BLOCKS = 4
COMBINE_ROWS = 1024
VMEM_LIMIT = 56 * 1024 * 1024
SC_CORES = 2
SC_SUBCORES = 16
SC_DISPATCH_CHUNK = 64
SC_COMBINE_CHUNK = 64


def _dot(a, b):
    return jnp.dot(a, b, preferred_element_type=F32)


def _dot_nt(a, b):
    return lax.dot_general(a, b, (((1,), (1,)), ((), ())), preferred_element_type=F32)


def _dot_tn(a, b):
    return lax.dot_general(a, b, (((0,), (0,)), ((), ())), preferred_element_type=F32)


def _split_bf16(x):
    hi = x.astype(BF16)
    lo = (x - hi.astype(F32)).astype(BF16)
    return hi, lo


def _sigmoid(x):
    return 1.0 / (1.0 + jnp.exp(-x))


def _pack_bf16_pairs(x_bf16):
    bits = pltpu.bitcast(x_bf16.astype(F32), jnp.uint32)
    n = x_bf16.shape[1] // 2
    return (bits[:, :n] >> 16) | (bits[:, n:] & jnp.uint32(0xFFFF0000))


def _unpack_bf16_pairs(packed):
    return (pltpu.bitcast(packed << 16, F32),
            pltpu.bitcast(packed & jnp.uint32(0xFFFF0000), F32))


def _proj_kernel(x_ref, hcu_ref, hqkv_ref, nw_ref, wa_ref, wi_ref, wab_ref, cmw_ref,
                 cmn_ref, gmat_ref, qcw_ref, alog_ref, dtb_ref,
                 ya_ref, q_ref, k_ref, v_ref, z_ref, gb_ref, tcu_ref, tqkv_ref,
                 cu_s, qkv_s, xn_s):
    rows = x_ref.shape[0]
    sub = PROJ_SUB_ROWS if rows % PROJ_SUB_ROWS == 0 else rows
    blocks = [slice(r, r + sub) for r in range(0, rows, sub)]

    @pl.when(pl.program_id(1) == 0)
    def _():
        cu_s[0:HIST, :] = hcu_ref[...]
        qkv_s[0:HIST, :] = hqkv_ref[...]

    x = x_ref[...]
    ms = jnp.mean(x * x, axis=-1, keepdims=True)
    xn_s[...] = (x * lax.rsqrt(ms + EPS) * nw_ref[...]).astype(BF16)

    def causal_conv(buf, cur, w_ref, cols, tail_ref):
        taps = w_ref.shape[0]
        acc = pltpu.roll(cur, taps - 1, axis=0) * w_ref[0:1, cols]
        for j in range(1, taps - 1):
            acc = acc + pltpu.roll(cur, taps - 1 - j, axis=0) * w_ref[j:j + 1, cols]
        acc = acc + cur * w_ref[taps - 1:taps, cols]
        buf[HIST:2 * HIST, cols] = cur[:HIST]
        seam = buf[pl.ds(HIST - taps + 1, HIST), cols] * w_ref[0:1, cols]
        for j in range(1, taps):
            seam = seam + buf[pl.ds(HIST - taps + 1 + j, HIST), cols] * w_ref[j:j + 1, cols]
        tail = cur[sub - HIST:]
        buf[0:HIST, cols] = tail
        tail_ref[:, cols] = tail
        return jnp.concatenate([seam, acc[HIST:]], axis=0)

    def mixer_a_tail(i, rs, pa):
        cols = slice(i * PROJ_COLS, (i + 1) * PROJ_COLS)
        cu = pa[:, PROJ_COLS:2 * PROJ_COLS] * pa[:, 2 * PROJ_COLS:]
        ya = pa[:, :PROJ_COLS] * causal_conv(cu_s, cu, cmw_ref, cols, tcu_ref)
        sq_hi, sq_lo = _split_bf16(ya * ya)
        msg = _dot(sq_hi, gmat_ref[...]) + _dot(sq_lo, gmat_ref[...])
        ya_ref[rs, cols] = (ya * lax.rsqrt(msg + EPS) * cmn_ref[:, cols]).astype(BF16)

    heads_per_chunk = PROJ_COLS // HEAD_DIM

    def qkv_tail(i, rs, pq):
        cols = slice(i * PROJ_COLS, (i + 1) * PROJ_COLS)
        c = causal_conv(qkv_s, pq, qcw_ref, cols, tqkv_ref)
        c = c * _sigmoid(c)
        part, first_head = divmod(i * heads_per_chunk, HEADS)
        for j in range(heads_per_chunk):
            ch = c[:, j * HEAD_DIM:(j + 1) * HEAD_DIM]
            sl = slice((first_head + j) * HEAD_DIM, (first_head + j + 1) * HEAD_DIM)
            if part == 0:
                norm = lax.rsqrt(jnp.sum(ch * ch, axis=-1, keepdims=True) + EPS)
                q_ref[rs, sl] = ch * norm * (HEAD_DIM ** -0.5)
            elif part == 1:
                k_ref[rs, sl] = ch * lax.rsqrt(jnp.sum(ch * ch, axis=-1, keepdims=True) + EPS)
            else:
                v_ref[rs, sl] = ch

    def z_tail(rs, pz):
        z_ref[rs, :] = pz

    def decay_beta_tail(rs, ab):
        sp_in = ab + dtb_ref[...]
        softplus = jnp.maximum(sp_in, 0.0) + jnp.log1p(jnp.exp(-jnp.abs(sp_in)))
        g = -jnp.exp(alog_ref[...]) * softplus
        lane = lax.broadcasted_iota(jnp.int32, ab.shape, 1)
        gb_ref[rs, :] = jnp.where(lane < HEADS, g, jnp.where(lane < 2 * HEADS, _sigmoid(ab), 0.0))

    def matmul(w_ref, w_cols, rs):
        return _dot(xn_s[rs, :], w_ref[:, w_cols])

    stages = []
    for i in range(CONV_CH // PROJ_COLS):
        w_cols = slice(3 * i * PROJ_COLS, 3 * (i + 1) * PROJ_COLS)
        stages += [(functools.partial(matmul, wa_ref, w_cols, rs),
                    functools.partial(mixer_a_tail, i, rs)) for rs in blocks]
    for i in range(3 * GDN_W // PROJ_COLS):
        w_cols = slice(QKV_COL0 + i * PROJ_COLS, QKV_COL0 + (i + 1) * PROJ_COLS)
        stages += [(functools.partial(matmul, wi_ref, w_cols, rs),
                    functools.partial(qkv_tail, i, rs)) for rs in blocks]
    stages += [(functools.partial(matmul, wi_ref, slice(Z_COL0, Z_COL0 + GDN_W), rs),
                functools.partial(z_tail, rs)) for rs in blocks]
    stages += [(functools.partial(matmul, wab_ref, slice(None), rs),
                functools.partial(decay_beta_tail, rs)) for rs in blocks]
    pending = None
    for issue, tail in stages:
        res = issue()
        if pending is not None:
            pending()
        pending = functools.partial(tail, res)
    pending()


def _proj_call(x3, hist_cu, hist_qkv, w, rows):
    nb, seq, _ = x3.shape
    nt = seq // rows
    tok = lambda width: pl.BlockSpec((None, rows, width), lambda b, t: (b, t, 0))
    full = lambda a: pl.BlockSpec(a.shape, lambda b, t: (0,) * a.ndim)
    tail = lambda width: pl.BlockSpec((None, HIST, width), lambda b, t: (b, 0, 0))
    consts = (hist_cu, hist_qkv, w['mix_norm_w'], w['w_a'], w['w_in'], w['w_ab'],
              w['conv_mix_w'], w['conv_mix_norm_w'], w['gmat'], w['qkv_conv_w'], w['a_log'],
              w['dt_bias'])
    out_shape = (
        jax.ShapeDtypeStruct((nb, seq, CONV_CH), BF16),
        jax.ShapeDtypeStruct((nb, seq, GDN_W), F32),
        jax.ShapeDtypeStruct((nb, seq, GDN_W), F32),
        jax.ShapeDtypeStruct((nb, seq, GDN_W), F32),
        jax.ShapeDtypeStruct((nb, seq, GDN_W), F32),
        jax.ShapeDtypeStruct((nb, seq, LANES), F32),
        jax.ShapeDtypeStruct((nb, HIST, CONV_CH), F32),
        jax.ShapeDtypeStruct((nb, HIST, 3 * GDN_W), F32),
    )
    return pl.pallas_call(
        _proj_kernel,
        grid=(nb, nt),
        in_specs=[tok(D_MODEL)] + [full(a) for a in consts],
        out_specs=(tok(CONV_CH), tok(GDN_W), tok(GDN_W), tok(GDN_W), tok(GDN_W), tok(LANES),
                   tail(CONV_CH), tail(3 * GDN_W)),
        out_shape=out_shape,
        scratch_shapes=[pltpu.VMEM((2 * HIST, CONV_CH), F32),
                        pltpu.VMEM((2 * HIST, 3 * GDN_W), F32),
                        pltpu.VMEM((rows, D_MODEL), BF16)],
        compiler_params=pltpu.CompilerParams(
            dimension_semantics=("arbitrary", "arbitrary"), vmem_limit_bytes=VMEM_LIMIT),
        name="proj",
    )(x3, *consts)


def _chunk_masks():
    row = lax.broadcasted_iota(jnp.int32, (CHUNK, CHUNK), 0)
    col = lax.broadcasted_iota(jnp.int32, (CHUNK, CHUNK), 1)
    incl = row >= col
    strict = row > col
    levels = []
    n = 1
    while n < CHUNK:
        levels.append((row // (2 * n) == col // (2 * n)) & ((row // n) % 2 == 1) & ((col // n) % 2 == 0))
        n *= 2
    return incl, strict, levels


def _chunk_cumsum(gb_blk, incl):
    tri = incl.astype(BF16)
    hi, lo = _split_bf16(gb_blk)
    return _dot(tri, hi) + _dot(tri, lo)


def _chunk_transforms(chains, masks, state_only):
    incl, strict, levels = masks
    eye = (lax.broadcasted_iota(jnp.int32, (CHUNK, CHUNK), 0)
           == lax.broadcasted_iota(jnp.int32, (CHUNK, CHUNK), 1)).astype(F32)
    decay = [jnp.exp(jnp.where(incl, gc_col - gc_row, -jnp.inf))
             for (_, _, _, _, gc_col, gc_row, _) in chains]
    kb = [kh * beta for (_, kh, _, beta, _, _, _) in chains]
    k_bf = [kh.astype(BF16) for (_, kh, _, _, _, _, _) in chains]
    a_mat = [jnp.where(strict, _dot_nt(kb_i.astype(BF16), k_i) * d_i, 0.0)
             for kb_i, k_i, d_i in zip(kb, k_bf, decay)]
    t_inv = [eye - jnp.where(levels[0], a_i, 0.0) for a_i in a_mat]
    for lvl in levels[1:]:
        t_bf = [t_i.astype(BF16) for t_i in t_inv]
        m1 = [_dot(jnp.where(lvl, a_i, 0.0).astype(BF16), t_i) for a_i, t_i in zip(a_mat, t_bf)]
        t_inv = [t_i - _dot(tb_i, m_i.astype(BF16)) for t_i, tb_i, m_i in zip(t_inv, t_bf, m1)]
    rhs = [jnp.concatenate([vh * beta, kb_i * jnp.exp(gc_col)], axis=1)
           for (_, _, vh, beta, gc_col, _, _), kb_i in zip(chains, kb)]
    uw = [_dot(t_i.astype(BF16), r_i.astype(BF16)).astype(BF16)
          for t_i, r_i in zip(t_inv, rhs)]
    kd = [kh * jnp.exp(g_last - gc_col) for (_, kh, _, _, gc_col, _, g_last) in chains]
    pn = [_dot_tn(kd_i.astype(BF16), uw_i) for kd_i, uw_i in zip(kd, uw)]
    if state_only:
        return [pn_i[:, :HEAD_DIM] for pn_i in pn]
    intra = [jnp.where(incl, _dot_nt(qh.astype(BF16), k_i) * d_i, 0.0)
             for (qh, _, _, _, _, _, _), k_i, d_i in zip(chains, k_bf, decay)]
    iuw = [_dot(in_i.astype(BF16), uw_i) for in_i, uw_i in zip(intra, uw)]
    out = []
    for (qh, _, _, _, gc_col, _, g_last), pn_i, iuw_i in zip(chains, pn, iuw):
        q_part = qh * jnp.exp(gc_col) - iuw_i[:, HEAD_DIM:]
        out.append((q_part, pn_i[:, HEAD_DIM:], iuw_i[:, :HEAD_DIM], pn_i[:, :HEAD_DIM],
                    jnp.exp(g_last)))
    return out


def _gdn_kernel(pq_ref, pk_ref, pv_ref, pgb_ref, q_ref, k_ref, v_ref, gb_ref, o_ref,
                s_s, qp_s, op_s, n_s, a_s):
    nb = q_ref.shape[0]
    n_chunks = q_ref.shape[1] // CHUNK
    group = GDN_GROUP if n_chunks % GDN_GROUP == 0 else 1
    masks = _chunk_masks()

    def chains_of(gb_blk, q_blk, k_blk, v_blk):
        gc = _chunk_cumsum(gb_blk, masks[0])
        gc_t = gc.T
        res = []
        for h in range(HEADS):
            sl = slice(h * HEAD_DIM, (h + 1) * HEAD_DIM)
            res.append((q_blk(sl), k_blk(sl), v_blk(sl), gb_blk[:, HEADS + h:HEADS + h + 1],
                        gc[:, h:h + 1], gc_t[h:h + 1, :CHUNK], gc[CHUNK - 1:CHUNK, h:h + 1]))
        return res

    @pl.when(pl.program_id(0) == 0)
    def _():
        chains = chains_of(pgb_ref[...], lambda sl: pq_ref[:, sl], lambda sl: pk_ref[:, sl],
                           lambda sl: pv_ref[:, sl])
        for h, n_mat in enumerate(_chunk_transforms(chains, masks, True)):
            for b in range(nb):
                s_s[b * HEADS + h] = n_mat

    def transform_group(gi, carry):
        chains, where = [], []
        for cc in range(group):
            c = gi * group + cc
            rows = pl.ds(pl.multiple_of(c * CHUNK, CHUNK), CHUNK)
            for b in range(nb):
                chains += chains_of(gb_ref[b, rows, :], lambda sl: q_ref[b, rows, sl],
                                    lambda sl: k_ref[b, rows, sl], lambda sl: v_ref[b, rows, sl])
                where += [(c, b * HEADS + h) for h in range(HEADS)]
        for (c, ch), (q_part, p_mat, o_part, n_mat, a) in zip(
                where, _chunk_transforms(chains, masks, False)):
            qp_s[c, ch, 0:CHUNK, :] = q_part.astype(BF16)
            qp_s[c, ch, CHUNK:, :] = p_mat.astype(BF16)
            op_s[c, ch] = o_part
            n_s[c, ch] = n_mat
            a_s[c, ch] = jnp.broadcast_to(a, (8, HEAD_DIM))
        return carry

    lax.fori_loop(0, n_chunks // group, transform_group, 0)

    def scan_chunk(c, carry):
        r0 = pl.multiple_of(c * CHUNK, CHUNK)
        for b in range(nb):
            for h in range(HEADS):
                ch = b * HEADS + h
                s = s_s[ch]
                r = _dot(qp_s[c, ch], s.astype(BF16))
                o_ref[b, pl.ds(r0, CHUNK), h * HEAD_DIM:(h + 1) * HEAD_DIM] = r[:CHUNK] + op_s[c, ch]
                s_s[ch] = a_s[c, ch][0:1, :] * s - r[CHUNK:] + n_s[c, ch]
        return carry

    lax.fori_loop(0, n_chunks, scan_chunk, 0)


def _gdn_call(pq, pk, pv, pgb, q, k, v, gb, chunks_per_step):
    nb, seq, _ = q.shape
    rows = chunks_per_step * CHUNK
    steps = seq // rows
    tok = lambda width: pl.BlockSpec((nb, rows, width), lambda i: (0, i, 0))
    pre = lambda width: pl.BlockSpec((None, CHUNK, width), lambda i: (0, 0, 0))
    nch = nb * HEADS
    return pl.pallas_call(
        _gdn_kernel,
        grid=(steps,),
        in_specs=[pre(GDN_W), pre(GDN_W), pre(GDN_W), pre(LANES),
                  tok(GDN_W), tok(GDN_W), tok(GDN_W), tok(LANES)],
        out_specs=tok(GDN_W),
        out_shape=jax.ShapeDtypeStruct((nb, seq, GDN_W), F32),
        scratch_shapes=[
            pltpu.VMEM((nch, HEAD_DIM, HEAD_DIM), F32),
            pltpu.VMEM((chunks_per_step, nch, CHUNK + HEAD_DIM, HEAD_DIM), BF16),
            pltpu.VMEM((chunks_per_step, nch, CHUNK, HEAD_DIM), F32),
            pltpu.VMEM((chunks_per_step, nch, HEAD_DIM, HEAD_DIM), F32),
            pltpu.VMEM((chunks_per_step, nch, 8, HEAD_DIM), F32),
        ],
        compiler_params=pltpu.CompilerParams(
            dimension_semantics=("arbitrary",), vmem_limit_bytes=VMEM_LIMIT),
        name="gdn",
    )(pq, pk, pv, pgb, q, k, v, gb)


def _mix_out_kernel(x_ref, ya_ref, o_ref, z_ref, gnw_ref, woa_ref, wob_ref, fnw_ref, wr_ref,
                    wrh_ref, br_ref, h_ref, xn_ref, route_ref, assign_ref, counts_ref, cnt_s):
    rows = x_ref.shape[0]
    sub = MIX_SUB_ROWS if rows % MIX_SUB_ROWS == 0 else rows
    blocks = [slice(r, r + sub) for r in range(0, rows, sub)]

    def gated_heads(rs):
        yb = []
        for h in range(HEADS):
            sl = slice(h * HEAD_DIM, (h + 1) * HEAD_DIM)
            oh = o_ref[rs, sl]
            zh = z_ref[rs, sl]
            on = oh * lax.rsqrt(jnp.mean(oh * oh, axis=-1, keepdims=True) + EPS) * gnw_ref[...]
            yb.append((on * (zh * _sigmoid(zh))).astype(BF16))
        return jnp.concatenate(yb, axis=1)

    yb = [gated_heads(rs) for rs in blocks]
    h1 = [x_ref[rs, :] + (_dot(ya_ref[rs, :], woa_ref[...]) + _dot(yb_i, wob_ref[...]))
          for rs, yb_i in zip(blocks, yb)]
    for rs, h1_i in zip(blocks, h1):
        h_ref[rs, :] = h1_i
    xn = [h1_i * lax.rsqrt(jnp.mean(h1_i * h1_i, axis=-1, keepdims=True) + EPS) * fnw_ref[...]
          for h1_i in h1]
    split = [_split_bf16(xn_i) for xn_i in xn]
    for rs, (x_hi, _) in zip(blocks, split):
        xn_ref[rs, :] = _pack_bf16_pairs(x_hi)
    hi_part = [_dot(x_hi, wr_ref[...]) for x_hi, _ in split]
    lo_part = [_dot(x_lo, wrh_ref[...]) for _, x_lo in split]
    logits = [hp[:, :LANES] + hp[:, LANES:] + lp + br_ref[...] for hp, lp in zip(hi_part, lo_part)]

    lane = lax.broadcasted_iota(jnp.int32, (sub, LANES), 1).astype(F32)
    neg = -jnp.inf
    big = float(1 << 20)

    def argmax_first(vals):
        m = jnp.max(vals, axis=-1, keepdims=True)
        idx = jnp.min(jnp.where(vals == m, lane, big), axis=-1, keepdims=True)
        return m, idx

    def top_k(lg):
        grp = jnp.where(lane < N_GROUPS, lg, neg)
        g_max, g_sel = argmax_first(grp)
        p_grp = 1.0 / jnp.sum(jnp.exp(grp - g_max), axis=-1, keepdims=True)
        lo_lane = N_GROUPS + g_sel * EXPERTS_PER_GROUP
        ex = jnp.where((lane >= lo_lane) & (lane < lo_lane + EXPERTS_PER_GROUP), lg, neg)
        m1, i1 = argmax_first(ex)
        m2, i2 = argmax_first(jnp.where(lane == i1, neg, ex))
        e2 = jnp.exp(m2 - m1)
        return i1 - N_GROUPS, i2 - N_GROUPS, 1.0 / (1.0 + e2) * p_grp, e2 / (1.0 + e2) * p_grp

    picks = [top_k(lg) for lg in logits]

    @pl.when(pl.program_id(0) == 0)
    def _():
        cnt_s[...] = jnp.zeros(cnt_s.shape, F32)

    onehots = [((lane == e1).astype(F32), (lane == e2).astype(F32)) for e1, e2, _, _ in picks]
    earlier = (lax.broadcasted_iota(jnp.int32, (sub, sub), 0)
               > lax.broadcasted_iota(jnp.int32, (sub, sub), 1)).astype(BF16)
    within = [_dot(earlier, (oh1 + oh2).astype(BF16)) for oh1, oh2 in onehots]
    counts = cnt_s[...]
    for rs, (e1, e2, w1, w2), (oh1, oh2), within_i in zip(blocks, picks, onehots, within):
        before = within_i + counts
        rank1 = jnp.sum(before * oh1, axis=-1, keepdims=True)
        rank2 = jnp.sum(before * oh2, axis=-1, keepdims=True)
        counts = counts + jnp.sum(oh1 + oh2, axis=0, keepdims=True)
        route = jnp.zeros((sub, LANES), F32)
        for k, val in ((ROUTE_E1, e1), (ROUTE_E2, e2), (ROUTE_W1, w1), (ROUTE_W2, w2),
                       (ROUTE_RANK1, rank1), (ROUTE_RANK2, rank2)):
            route = jnp.where(lane == k, val, route)
        route_ref[rs, :] = route
        assign_ref[:, rs] = route.T[:ASSIGN_ROWS].astype(jnp.int32)
    cnt_s[...] = counts
    counts_ref[...] = counts


def _mix_out_call(x2, ya, o, z, w, rows):
    tokens = x2.shape[0]
    tok = lambda width: pl.BlockSpec((rows, width), lambda i: (i, 0))
    full = lambda a: pl.BlockSpec(a.shape, lambda i: (0,) * a.ndim)
    consts = (w['gdn_norm_w'], w['w_out_a'], w['w_out_b'], w['ffn_norm_w'], w['w_route'],
              w['w_route_hi'], w['b_route'])
    return pl.pallas_call(
        _mix_out_kernel,
        grid=(tokens // rows,),
        in_specs=[tok(D_MODEL), tok(CONV_CH), tok(GDN_W), tok(GDN_W)] + [full(a) for a in consts],
        out_specs=(tok(D_MODEL), tok(D_MODEL // 2), tok(LANES),
                   pl.BlockSpec((ASSIGN_ROWS, rows), lambda i: (0, i)),
                   pl.BlockSpec((1, LANES), lambda i: (0, 0))),
        out_shape=(jax.ShapeDtypeStruct((tokens, D_MODEL), F32),
                   jax.ShapeDtypeStruct((tokens, D_MODEL // 2), jnp.uint32),
                   jax.ShapeDtypeStruct((tokens, LANES), F32),
                   jax.ShapeDtypeStruct((ASSIGN_ROWS, tokens), jnp.int32),
                   jax.ShapeDtypeStruct((1, LANES), F32)),
        scratch_shapes=[pltpu.VMEM((1, LANES), F32)],
        compiler_params=pltpu.CompilerParams(
            dimension_semantics=("arbitrary",), vmem_limit_bytes=VMEM_LIMIT),
        name="mix_out",
    )(x2, ya, o, z, *consts)


def _sc_row_move(table, idx, out_rows, chunk, scatter, name):
    n = idx.shape[0]
    n_src, width = table.shape
    workers = SC_CORES * SC_SUBCORES
    per_w = n // workers
    assert n % workers == 0 and per_w % (2 * chunk) == 0
    assert n_src % per_w == 0 or not scatter
    pairs = per_w // (2 * chunk)
    mesh = plsc.VectorSubcoreMesh(core_axis_name="c", subcore_axis_name="s",
                                  num_cores=SC_CORES, num_subcores=SC_SUBCORES)

    def body(table_hbm, idx_hbm, out_hbm, idx_v, buf_a, buf_b, sem_ra, sem_rb, sem_wa, sem_wb):
        base = (lax.axis_index("s") * SC_CORES + lax.axis_index("c")) * per_w
        src_base = lax.rem(base, n_src)
        pltpu.sync_copy(idx_hbm.at[pl.ds(base, per_w)], idx_v)

        def read(c, buf, sem):
            off = pl.multiple_of(c * chunk, chunk)
            src = (table_hbm.at[pl.ds(src_base + off, chunk)] if scatter
                   else table_hbm.at[idx_v.at[pl.ds(off, chunk)]])
            return pltpu.make_async_copy(src, buf, sem)

        def write(c, buf, sem):
            off = pl.multiple_of(c * chunk, chunk)
            dst = (out_hbm.at[idx_v.at[pl.ds(off, chunk)]] if scatter
                   else out_hbm.at[pl.ds(base + off, chunk)])
            return pltpu.make_async_copy(buf, dst, sem)

        read(0, buf_a, sem_ra).start()

        @pl.loop(0, pairs)
        def _(j):
            ca = 2 * j
            cb = ca + 1
            read(cb, buf_b, sem_rb).start()
            read(ca, buf_a, sem_ra).wait()
            write(ca, buf_a, sem_wa).start()
            read(cb, buf_b, sem_rb).wait()
            write(cb, buf_b, sem_wb).start()
            write(ca, buf_a, sem_wa).wait()

            @pl.when(j + 1 < pairs)
            def _():
                read(ca + 2, buf_a, sem_ra).start()

            write(cb, buf_b, sem_wb).wait()

    return pl.kernel(
        body,
        out_type=jax.ShapeDtypeStruct((out_rows, width), table.dtype),
        mesh=mesh,
        scratch_types=[pltpu.VMEM((per_w,), jnp.int32),
                       pltpu.VMEM((chunk, width), table.dtype),
                       pltpu.VMEM((chunk, width), table.dtype),
                       pltpu.SemaphoreType.DMA, pltpu.SemaphoreType.DMA,
                       pltpu.SemaphoreType.DMA, pltpu.SemaphoreType.DMA],
        name=name,
    )(table, idx)


def _experts_kernel(bexp_ref, bvalid_ref, nused_ref, x_ref, wg_hbm, wu_hbm, wd_hbm, y_ref,
                    wg_s, wu_s, wd_s, stage_g, stage_u, stage_d, sems, seq_s):
    n_used = nused_ref[0]
    n_last = bexp_ref.shape[0] - 1
    rows = x_ref.shape[0] // EXPERT_STEP_BLOCKS
    sub = EXPERT_SUB_ROWS if rows % EXPERT_SUB_ROWS == 0 else rows

    def weight_copies(e, slot):
        return (pltpu.make_async_copy(wg_hbm.at[e], stage_g.at[slot], sems.at[slot, 0]),
                pltpu.make_async_copy(wu_hbm.at[e], stage_u.at[slot], sems.at[slot, 1]),
                pltpu.make_async_copy(wd_hbm.at[e], stage_d.at[slot], sems.at[slot, 2]))

    def block(i, r0):
        expert = bexp_ref[i]
        first_of_expert = (i == 0) | (expert != bexp_ref[jnp.maximum(i - 1, 0)])

        @pl.when((i == 0) & (n_used > 0))
        def _():
            seq_s[0] = 0
            for copy in weight_copies(expert, 0):
                copy.start()

        @pl.when((i < n_used) & first_of_expert)
        def _():
            @pl.when(i > 0)
            def _():
                seq_s[0] = seq_s[0] + 1

            slot = seq_s[0] % 2
            nxt = lax.while_loop(
                lambda j: (j < n_used) & (bexp_ref[jnp.minimum(j, n_last)] == expert),
                lambda j: j + 1, i + 1)

            @pl.when(nxt < n_used)
            def _():
                for copy in weight_copies(bexp_ref[jnp.minimum(nxt, n_last)], 1 - slot):
                    copy.start()

            for copy in weight_copies(expert, slot):
                copy.wait()
            wg_s[...] = stage_g[slot].astype(BF16)
            wu_s[...] = stage_u[slot].astype(BF16)
            wd_s[...] = stage_d[slot].astype(BF16)

        @pl.when(i < n_used)
        def _():
            blocks = [slice(r, r + sub) for r in range(r0, r0 + rows, sub)]
            row = lax.broadcasted_iota(jnp.int32, (sub, x_ref.shape[1]), 0)
            xb = [jnp.concatenate(_unpack_bf16_pairs(
                      jnp.where(row + (rs.start - r0) < bvalid_ref[i], x_ref[rs, :], jnp.uint32(0))),
                      axis=1).astype(BF16) for rs in blocks]
            gate = [_dot(xb_i, wg_s[...]) for xb_i in xb]
            up = [_dot(xb_i, wu_s[...]) for xb_i in xb]
            hid = [((g_i * _sigmoid(g_i)) * u_i).astype(BF16) for g_i, u_i in zip(gate, up)]
            y = [_dot(h_i, wd_s[...]) for h_i in hid]
            for rs, y_i in zip(blocks, y):
                y_ref[rs, :] = _pack_bf16_pairs(y_i.astype(BF16))

        @pl.when(i >= n_used)
        def _():
            y_ref[r0:r0 + rows, :] = jnp.zeros((rows, y_ref.shape[1]), jnp.uint32)

    for j in range(EXPERT_STEP_BLOCKS):
        block(pl.program_id(0) * EXPERT_STEP_BLOCKS + j, j * rows)


def _experts_call(block_expert, block_valid, n_used, x_rows, w_gate, w_up, w_down, rows):
    n_blocks = block_expert.shape[0]
    step_rows = EXPERT_STEP_BLOCKS * rows
    hbm = pl.BlockSpec(memory_space=pl.ANY)
    grid_spec = pltpu.PrefetchScalarGridSpec(
        num_scalar_prefetch=3,
        grid=(n_blocks // EXPERT_STEP_BLOCKS,),
        in_specs=[pl.BlockSpec((step_rows, D_MODEL // 2), lambda i, be, bv, nu: (i, 0)),
                  hbm, hbm, hbm],
        out_specs=pl.BlockSpec((step_rows, D_MODEL // 2), lambda i, be, bv, nu: (i, 0)),
        scratch_shapes=[pltpu.VMEM((D_MODEL, D_EXPERT), BF16),
                        pltpu.VMEM((D_MODEL, D_EXPERT), BF16),
                        pltpu.VMEM((D_EXPERT, D_MODEL), BF16),
                        pltpu.VMEM((2, D_MODEL, D_EXPERT), F32),
                        pltpu.VMEM((2, D_MODEL, D_EXPERT), F32),
                        pltpu.VMEM((2, D_EXPERT, D_MODEL), F32),
                        pltpu.SemaphoreType.DMA((2, 3)),
                        pltpu.SMEM((1,), jnp.int32)],
    )
    return pl.pallas_call(
        _experts_kernel,
        grid_spec=grid_spec,
        out_shape=jax.ShapeDtypeStruct((n_blocks * rows, D_MODEL // 2), jnp.uint32),
        compiler_params=pltpu.CompilerParams(
            dimension_semantics=("arbitrary",), vmem_limit_bytes=VMEM_LIMIT),
        name="experts",
    )(block_expert, block_valid, n_used, x_rows, w_gate, w_up, w_down)


def _combine_kernel(y1_ref, y2_ref, h_ref, route_ref, fw_ref, out_ref):
    route = route_ref[...]
    y1 = jnp.concatenate(_unpack_bf16_pairs(y1_ref[...]), axis=1)
    y2 = jnp.concatenate(_unpack_bf16_pairs(y2_ref[...]), axis=1)
    moe = route[:, ROUTE_W1:ROUTE_W1 + 1] * y1 + route[:, ROUTE_W2:ROUTE_W2 + 1] * y2
    h2 = h_ref[...] + moe
    out_ref[...] = h2 * lax.rsqrt(jnp.mean(h2 * h2, axis=-1, keepdims=True) + EPS) * fw_ref[...]


def _combine_call(y_tok, h1, route, final_w, rows):
    tokens = h1.shape[0]
    steps = tokens // rows
    tok = lambda width: pl.BlockSpec((rows, width), lambda i: (i, 0))
    return pl.pallas_call(
        _combine_kernel,
        grid=(steps,),
        in_specs=[tok(D_MODEL // 2), pl.BlockSpec((rows, D_MODEL // 2), lambda i: (i + steps, 0)),
                  tok(D_MODEL), tok(LANES), pl.BlockSpec((1, D_MODEL), lambda i: (0, 0))],
        out_specs=tok(D_MODEL),
        out_shape=jax.ShapeDtypeStruct((tokens, D_MODEL), F32),
        compiler_params=pltpu.CompilerParams(
            dimension_semantics=("arbitrary",), vmem_limit_bytes=VMEM_LIMIT),
        name="combine",
    )(y_tok, y_tok, h1, route, final_w)


def _dispatch_tables(assign, counts, tokens, rows):
    n_blocks = pl.cdiv((tokens * TOP_K + N_EXPERTS * (rows - 1)) // rows,
                       EXPERT_STEP_BLOCKS) * EXPERT_STEP_BLOCKS
    counts = counts[0, :N_EXPERTS].astype(jnp.int32)
    padded = (counts + rows - 1) // rows * rows
    pad_end = jnp.cumsum(padded)
    pad_start = pad_end - padded
    experts = assign[ROUTE_E1:ROUTE_E2 + 1]
    ranks = assign[ROUTE_RANK1:ROUTE_RANK2 + 1]
    is_expert = experts[..., None] == jnp.arange(N_EXPERTS, dtype=jnp.int32)
    dest = (jnp.sum(jnp.where(is_expert, pad_start, 0), axis=-1) + ranks).reshape(-1)
    block_start = jnp.arange(n_blocks, dtype=jnp.int32) * rows
    block_expert = jnp.minimum(jnp.sum(block_start[:, None] >= pad_end[None, :], axis=1),
                               N_EXPERTS - 1).astype(jnp.int32)
    block_valid = jnp.clip(pad_start[block_expert] + counts[block_expert] - block_start, 0, rows)
    n_used = (pad_end[-1] // rows).astype(jnp.int32).reshape(1)
    return dest, block_expert, block_valid.astype(jnp.int32), n_used, n_blocks


def _prepare_weights(mix_norm_w, w_in, conv_mix_w, conv_mix_norm_w, qkv_conv_w, a_log, dt_bias,
                     gdn_norm_w, w_out, ffn_norm_w, w_group, b_group, w_router, b_router):
    pad_lanes = lambda v: jnp.pad(v.reshape(1, -1), ((0, 0), (0, LANES - v.size)))
    w_in_bf = w_in.astype(BF16)
    w_ab = jnp.pad(w_in_bf[:, Z_COL0 + GDN_W:], ((0, 0), (0, LANES - 2 * HEADS)))
    grp = jnp.arange(PROJ_COLS) // CONV_GROUP_W
    gmat = jnp.where(grp[:, None] == grp[None, :], 1.0 / CONV_GROUP_W, 0.0).astype(BF16)
    w_route = jnp.concatenate([w_group, w_router.reshape(D_MODEL, N_EXPERTS)], axis=1)
    w_route = jnp.pad(w_route, ((0, 0), (0, LANES - w_route.shape[1])))
    w_route_hi = w_route.astype(BF16)
    return dict(
        mix_norm_w=mix_norm_w.reshape(1, -1),
        w_a=w_in_bf[:, :QKV_COL0].reshape(D_MODEL, 3, CONV_CH // PROJ_COLS, PROJ_COLS)
        .transpose(0, 2, 1, 3).reshape(D_MODEL, QKV_COL0),
        w_in=w_in_bf,
        w_ab=w_ab,
        conv_mix_w=conv_mix_w,
        conv_mix_norm_w=conv_mix_norm_w.reshape(1, -1),
        gmat=gmat,
        qkv_conv_w=qkv_conv_w,
        a_log=pad_lanes(a_log),
        dt_bias=pad_lanes(dt_bias),
        gdn_norm_w=gdn_norm_w.reshape(1, -1),
        w_out_a=w_out[:CONV_CH].astype(BF16),
        w_out_b=w_out[CONV_CH:].astype(BF16),
        ffn_norm_w=ffn_norm_w.reshape(1, -1),
        w_route=jnp.concatenate([w_route_hi, (w_route - w_route_hi.astype(F32)).astype(BF16)], axis=1),
        w_route_hi=w_route_hi,
        b_route=pad_lanes(jnp.concatenate([b_group, b_router.reshape(-1)])),
    )


def _tile(n, preferred):
    return preferred if n % preferred == 0 else n


def kernel(x, meta_tokens, mix_norm_w, w_in, conv_mix_w, conv_mix_norm_w, qkv_conv_w, a_log,
           dt_bias, gdn_norm_w, w_out, ffn_norm_w, w_group, b_group, w_router, b_router, w_gate,
           w_up, w_down, final_norm_w):
    assert mix_norm_w.shape[0] == 1, "single-layer kernel"
    batch, seq, _ = x.shape
    assert seq % CHUNK == 0
    w = _prepare_weights(mix_norm_w[0], w_in[0], conv_mix_w[0], conv_mix_norm_w[0], qkv_conv_w[0],
                         a_log[0], dt_bias[0], gdn_norm_w[0], w_out[0], ffn_norm_w[0], w_group[0],
                         b_group[0], w_router[0], b_router[0])

    prefix = jnp.concatenate([jnp.zeros((CHUNK - N_META, D_MODEL), x.dtype),
                              meta_tokens.astype(x.dtype)], axis=0)[None]
    zero_cu = jnp.zeros((HIST, CONV_CH), F32)
    zero_qkv = jnp.zeros((HIST, 3 * GDN_W), F32)
    _, pq, pk, pv, _, pgb, tail_cu, tail_qkv = _proj_call(prefix, zero_cu, zero_qkv, w, CHUNK)

    ya, q, k, v, z, gb, _, _ = _proj_call(x, tail_cu[0], tail_qkv[0], w, _tile(seq, PROJ_ROWS))
    o = _gdn_call(pq, pk, pv, pgb, q, k, v, gb, _tile(seq // CHUNK, GDN_CHUNKS))

    tokens = batch * seq
    flat = lambda a: a.reshape(tokens, a.shape[-1])
    h1, xn2, route, assign, counts = _mix_out_call(flat(x), flat(ya), flat(o), flat(z), w,
                                           _tile(tokens, MIX_ROWS))

    dest, block_expert, block_valid, n_used, n_blocks = _dispatch_tables(assign, counts, tokens,
                                                                         EXPERT_ROWS)
    x_rows = _sc_row_move(xn2, dest, n_blocks * EXPERT_ROWS, SC_DISPATCH_CHUNK, True,
                          "dispatch_scatter")
    y_rows = _experts_call(block_expert, block_valid, n_used, x_rows, w_gate[0], w_up[0],
                           w_down[0], EXPERT_ROWS)
    y_tok = _sc_row_move(y_rows, dest, TOP_K * tokens, SC_COMBINE_CHUNK, False, "combine_gather")
    out = _combine_call(y_tok, h1, route, final_norm_w.reshape(1, -1), _tile(tokens, COMBINE_ROWS))
    return out.reshape(batch, seq, D_MODEL)
```

```python
import functools

import jax
import jax.numpy as jnp
from jax import lax
from jax.experimental import pallas as pl
from jax.experimental.pallas import tpu as pltpu
from jax.experimental.pallas import tpu_sc as plsc

F32 = jnp.float32
BF16 = jnp.bfloat16
EPS = 1e-6

D_MODEL = 1024
N_META = 16
CONV_CH = 512
CONV_GROUP_W = 64
HEADS = 4
HEAD_DIM = 128
GDN_W = HEADS * HEAD_DIM
QKV_COL0 = 3 * CONV_CH
Z_COL0 = QKV_COL0 + 3 * GDN_W
CHUNK = 64
N_GROUPS = 4
EXPERTS_PER_GROUP = 8
N_EXPERTS = N_GROUPS * EXPERTS_PER_GROUP
TOP_K = 2
D_EXPERT = 512
LANES = 128
ROUTE_E1, ROUTE_E2, ROUTE_RANK1, ROUTE_RANK2, ROUTE_W1, ROUTE_W2 = range(6)
ASSIGN_ROWS = 8
HIST = 8

PROJ_ROWS = 1024
PROJ_COLS = 256
PROJ_SUB_ROWS = 256
GDN_CHUNKS = 8
GDN_GROUP = 4
MIX_ROWS = 1024
MIX_SUB_ROWS = 128
EXPERT_ROWS = 256
EXPERT_SUB_ROWS = 128
EXPERT_STEP_BLOCKS = 4
COMBINE_ROWS = 1024
VMEM_LIMIT = 56 * 1024 * 1024
SC_CORES = 2
SC_SUBCORES = 16
SC_DISPATCH_CHUNK = 64
SC_COMBINE_CHUNK = 64


def _dot(a, b):
    return jnp.dot(a, b, preferred_element_type=F32)


def _dot_nt(a, b):
    return lax.dot_general(a, b, (((1,), (1,)), ((), ())), preferred_element_type=F32)


def _dot_tn(a, b):
    return lax.dot_general(a, b, (((0,), (0,)), ((), ())), preferred_element_type=F32)


def _split_bf16(x):
    hi = x.astype(BF16)
    lo = (x - hi.astype(F32)).astype(BF16)
    return hi, lo


def _sigmoid(x):
    return 1.0 / (1.0 + jnp.exp(-x))


def _pack_bf16_pairs(x_bf16):
    bits = pltpu.bitcast(x_bf16.astype(F32), jnp.uint32)
    n = x_bf16.shape[1] // 2
    return (bits[:, :n] >> 16) | (bits[:, n:] & jnp.uint32(0xFFFF0000))


def _unpack_bf16_pairs(packed):
    return (pltpu.bitcast(packed << 16, F32),
            pltpu.bitcast(packed & jnp.uint32(0xFFFF0000), F32))


def _proj_kernel(x_ref, hcu_ref, hqkv_ref, nw_ref, wa_ref, wi_ref, wab_ref, cmw_ref,
                 cmn_ref, gmat_ref, qcw_ref, alog_ref, dtb_ref,
                 ya_ref, q_ref, k_ref, v_ref, z_ref, gb_ref, tcu_ref, tqkv_ref,
                 cu_s, qkv_s, xn_s):
    rows = x_ref.shape[0]
    sub = PROJ_SUB_ROWS if rows % PROJ_SUB_ROWS == 0 else rows
    blocks = [slice(r, r + sub) for r in range(0, rows, sub)]

    @pl.when(pl.program_id(1) == 0)
    def _():
        cu_s[0:HIST, :] = hcu_ref[...]
        qkv_s[0:HIST, :] = hqkv_ref[...]

    x = x_ref[...]
    ms = jnp.mean(x * x, axis=-1, keepdims=True)
    xn_s[...] = (x * lax.rsqrt(ms + EPS) * nw_ref[...]).astype(BF16)

    def causal_conv(buf, cur, w_ref, cols, tail_ref):
        taps = w_ref.shape[0]
        acc = pltpu.roll(cur, taps - 1, axis=0) * w_ref[0:1, cols]
        for j in range(1, taps - 1):
            acc = acc + pltpu.roll(cur, taps - 1 - j, axis=0) * w_ref[j:j + 1, cols]
        acc = acc + cur * w_ref[taps - 1:taps, cols]
        buf[HIST:2 * HIST, cols] = cur[:HIST]
        seam = buf[pl.ds(HIST - taps + 1, HIST), cols] * w_ref[0:1, cols]
        for j in range(1, taps):
            seam = seam + buf[pl.ds(HIST - taps + 1 + j, HIST), cols] * w_ref[j:j + 1, cols]
        tail = cur[sub - HIST:]
        buf[0:HIST, cols] = tail
        tail_ref[:, cols] = tail
        return jnp.concatenate([seam, acc[HIST:]], axis=0)

    def mixer_a_tail(i, rs, pa):
        cols = slice(i * PROJ_COLS, (i + 1) * PROJ_COLS)
        cu = pa[:, PROJ_COLS:2 * PROJ_COLS] * pa[:, 2 * PROJ_COLS:]
        ya = pa[:, :PROJ_COLS] * causal_conv(cu_s, cu, cmw_ref, cols, tcu_ref)
        sq_hi, sq_lo = _split_bf16(ya * ya)
        msg = _dot(sq_hi, gmat_ref[...]) + _dot(sq_lo, gmat_ref[...])
        ya_ref[rs, cols] = (ya * lax.rsqrt(msg + EPS) * cmn_ref[:, cols]).astype(BF16)

    heads_per_chunk = PROJ_COLS // HEAD_DIM

    def qkv_tail(i, rs, pq):
        cols = slice(i * PROJ_COLS, (i + 1) * PROJ_COLS)
        c = causal_conv(qkv_s, pq, qcw_ref, cols, tqkv_ref)
        c = c * _sigmoid(c)
        part, first_head = divmod(i * heads_per_chunk, HEADS)
        for j in range(heads_per_chunk):
            ch = c[:, j * HEAD_DIM:(j + 1) * HEAD_DIM]
            sl = slice((first_head + j) * HEAD_DIM, (first_head + j + 1) * HEAD_DIM)
            if part == 0:
                norm = lax.rsqrt(jnp.sum(ch * ch, axis=-1, keepdims=True) + EPS)
                q_ref[rs, sl] = ch * norm * (HEAD_DIM ** -0.5)
            elif part == 1:
                k_ref[rs, sl] = ch * lax.rsqrt(jnp.sum(ch * ch, axis=-1, keepdims=True) + EPS)
            else:
                v_ref[rs, sl] = ch

    def z_tail(rs, pz):
        z_ref[rs, :] = pz

    def decay_beta_tail(rs, ab):
        sp_in = ab + dtb_ref[...]
        softplus = jnp.maximum(sp_in, 0.0) + jnp.log1p(jnp.exp(-jnp.abs(sp_in)))
        g = -jnp.exp(alog_ref[...]) * softplus
        lane = lax.broadcasted_iota(jnp.int32, ab.shape, 1)
        gb_ref[rs, :] = jnp.where(lane < HEADS, g, jnp.where(lane < 2 * HEADS, _sigmoid(ab), 0.0))

    def matmul(w_ref, w_cols, rs):
        return _dot(xn_s[rs, :], w_ref[:, w_cols])

    stages = []
    for i in range(CONV_CH // PROJ_COLS):
        w_cols = slice(3 * i * PROJ_COLS, 3 * (i + 1) * PROJ_COLS)
        stages += [(functools.partial(matmul, wa_ref, w_cols, rs),
                    functools.partial(mixer_a_tail, i, rs)) for rs in blocks]
    for i in range(3 * GDN_W // PROJ_COLS):
        w_cols = slice(QKV_COL0 + i * PROJ_COLS, QKV_COL0 + (i + 1) * PROJ_COLS)
        stages += [(functools.partial(matmul, wi_ref, w_cols, rs),
                    functools.partial(qkv_tail, i, rs)) for rs in blocks]
    stages += [(functools.partial(matmul, wi_ref, slice(Z_COL0, Z_COL0 + GDN_W), rs),
                functools.partial(z_tail, rs)) for rs in blocks]
    stages += [(functools.partial(matmul, wab_ref, slice(None), rs),
                functools.partial(decay_beta_tail, rs)) for rs in blocks]
    pending = None
    for issue, tail in stages:
        res = issue()
        if pending is not None:
            pending()
        pending = functools.partial(tail, res)
    pending()


def _proj_call(x3, hist_cu, hist_qkv, w, rows):
    nb, seq, _ = x3.shape
    nt = seq // rows
    tok = lambda width: pl.BlockSpec((None, rows, width), lambda b, t: (b, t, 0))
    full = lambda a: pl.BlockSpec(a.shape, lambda b, t: (0,) * a.ndim,
                                  pipeline_mode=pl.Buffered(1))
    tail = lambda width: pl.BlockSpec((None, HIST, width), lambda b, t: (b, 0, 0))
    consts = (hist_cu, hist_qkv, w['mix_norm_w'], w['w_a'], w['w_in'], w['w_ab'],
              w['conv_mix_w'], w['conv_mix_norm_w'], w['gmat'], w['qkv_conv_w'], w['a_log'],
              w['dt_bias'])
    out_shape = (
        jax.ShapeDtypeStruct((nb, seq, CONV_CH), BF16),
        jax.ShapeDtypeStruct((nb, seq, GDN_W), F32),
        jax.ShapeDtypeStruct((nb, seq, GDN_W), F32),
        jax.ShapeDtypeStruct((nb, seq, GDN_W), F32),
        jax.ShapeDtypeStruct((nb, seq, GDN_W), F32),
        jax.ShapeDtypeStruct((nb, seq, LANES), F32),
        jax.ShapeDtypeStruct((nb, HIST, CONV_CH), F32),
        jax.ShapeDtypeStruct((nb, HIST, 3 * GDN_W), F32),
    )
    return pl.pallas_call(
        _proj_kernel,
        grid=(nb, nt),
        in_specs=[tok(D_MODEL)] + [full(a) for a in consts],
        out_specs=(tok(CONV_CH), tok(GDN_W), tok(GDN_W), tok(GDN_W), tok(GDN_W), tok(LANES),
                   tail(CONV_CH), tail(3 * GDN_W)),
        out_shape=out_shape,
        scratch_shapes=[pltpu.VMEM((2 * HIST, CONV_CH), F32),
                        pltpu.VMEM((2 * HIST, 3 * GDN_W), F32),
                        pltpu.VMEM((rows, D_MODEL), BF16)],
        compiler_params=pltpu.CompilerParams(
            dimension_semantics=("arbitrary", "arbitrary"), vmem_limit_bytes=VMEM_LIMIT),
        name="proj",
    )(x3, *consts)


def _chunk_masks():
    row = lax.broadcasted_iota(jnp.int32, (CHUNK, CHUNK), 0)
    col = lax.broadcasted_iota(jnp.int32, (CHUNK, CHUNK), 1)
    incl = row >= col
    strict = row > col
    levels = []
    n = 1
    while n < CHUNK:
        levels.append((row // (2 * n) == col // (2 * n)) & ((row // n) % 2 == 1) & ((col // n) % 2 == 0))
        n *= 2
    return incl, strict, levels


def _chunk_cumsum(gb_blk, incl):
    tri = incl.astype(BF16)
    hi, lo = _split_bf16(gb_blk)
    return _dot(tri, hi) + _dot(tri, lo)


def _chunk_transforms(chains, masks, state_only):
    incl, strict, levels = masks
    eye = (lax.broadcasted_iota(jnp.int32, (CHUNK, CHUNK), 0)
           == lax.broadcasted_iota(jnp.int32, (CHUNK, CHUNK), 1)).astype(F32)
    decay = [jnp.exp(jnp.where(incl, gc_col - gc_row, -jnp.inf))
             for (_, _, _, _, gc_col, gc_row, _) in chains]
    kb = [kh * beta for (_, kh, _, beta, _, _, _) in chains]
    k_bf = [kh.astype(BF16) for (_, kh, _, _, _, _, _) in chains]
    a_mat = [jnp.where(strict, _dot_nt(kb_i.astype(BF16), k_i) * d_i, 0.0)
             for kb_i, k_i, d_i in zip(kb, k_bf, decay)]
    t_inv = [eye - jnp.where(levels[0], a_i, 0.0) for a_i in a_mat]
    for lvl in levels[1:]:
        t_bf = [t_i.astype(BF16) for t_i in t_inv]
        m1 = [_dot(jnp.where(lvl, a_i, 0.0).astype(BF16), t_i) for a_i, t_i in zip(a_mat, t_bf)]
        t_inv = [t_i - _dot(tb_i, m_i.astype(BF16)) for t_i, tb_i, m_i in zip(t_inv, t_bf, m1)]
    rhs = [jnp.concatenate([vh * beta, kb_i * jnp.exp(gc_col)], axis=1)
           for (_, _, vh, beta, gc_col, _, _), kb_i in zip(chains, kb)]
    uw = [_dot(t_i.astype(BF16), r_i.astype(BF16)).astype(BF16)
          for t_i, r_i in zip(t_inv, rhs)]
    kd = [kh * jnp.exp(g_last - gc_col) for (_, kh, _, _, gc_col, _, g_last) in chains]
    pn = [_dot_tn(kd_i.astype(BF16), uw_i) for kd_i, uw_i in zip(kd, uw)]
    if state_only:
        return [pn_i[:, :HEAD_DIM] for pn_i in pn]
    intra = [jnp.where(incl, _dot_nt(qh.astype(BF16), k_i) * d_i, 0.0)
             for (qh, _, _, _, _, _, _), k_i, d_i in zip(chains, k_bf, decay)]
    iuw = [_dot(in_i.astype(BF16), uw_i) for in_i, uw_i in zip(intra, uw)]
    out = []
    for (qh, _, _, _, gc_col, _, g_last), pn_i, iuw_i in zip(chains, pn, iuw):
        q_part = qh * jnp.exp(gc_col) - iuw_i[:, HEAD_DIM:]
        out.append((q_part, pn_i[:, HEAD_DIM:], iuw_i[:, :HEAD_DIM], pn_i[:, :HEAD_DIM],
                    jnp.exp(g_last)))
    return out


def _gdn_kernel(pq_ref, pk_ref, pv_ref, pgb_ref, q_ref, k_ref, v_ref, gb_ref, o_ref,
                s_s, qp_s, op_s, n_s, a_s):
    nb = q_ref.shape[0]
    n_chunks = q_ref.shape[1] // CHUNK
    group = GDN_GROUP if n_chunks % GDN_GROUP == 0 else 1
    masks = _chunk_masks()

    def chains_of(gb_blk, q_blk, k_blk, v_blk):
        gc = _chunk_cumsum(gb_blk, masks[0])
        gc_t = gc.T
        res = []
        for h in range(HEADS):
            sl = slice(h * HEAD_DIM, (h + 1) * HEAD_DIM)
            res.append((q_blk(sl), k_blk(sl), v_blk(sl), gb_blk[:, HEADS + h:HEADS + h + 1],
                        gc[:, h:h + 1], gc_t[h:h + 1, :CHUNK], gc[CHUNK - 1:CHUNK, h:h + 1]))
        return res

    @pl.when(pl.program_id(0) == 0)
    def _():
        chains = chains_of(pgb_ref[...], lambda sl: pq_ref[:, sl], lambda sl: pk_ref[:, sl],
                           lambda sl: pv_ref[:, sl])
        for h, n_mat in enumerate(_chunk_transforms(chains, masks, True)):
            for b in range(nb):
                s_s[b * HEADS + h] = n_mat

    def transform_group(gi, carry):
        chains, where = [], []
        for cc in range(group):
            c = gi * group + cc
            rows = pl.ds(pl.multiple_of(c * CHUNK, CHUNK), CHUNK)
            for b in range(nb):
                chains += chains_of(gb_ref[b, rows, :], lambda sl: q_ref[b, rows, sl],
                                    lambda sl: k_ref[b, rows, sl], lambda sl: v_ref[b, rows, sl])
                where += [(c, b * HEADS + h) for h in range(HEADS)]
        for (c, ch), (q_part, p_mat, o_part, n_mat, a) in zip(
                where, _chunk_transforms(chains, masks, False)):
            qp_s[c, ch, 0:CHUNK, :] = q_part.astype(BF16)
            qp_s[c, ch, CHUNK:, :] = p_mat.astype(BF16)
            op_s[c, ch] = o_part
            n_s[c, ch] = n_mat
            a_s[c, ch] = jnp.broadcast_to(a, (8, HEAD_DIM))
        return carry

    lax.fori_loop(0, n_chunks // group, transform_group, 0)

    def scan_chunk(c, carry):
        r0 = pl.multiple_of(c * CHUNK, CHUNK)
        for b in range(nb):
            for h in range(HEADS):
                ch = b * HEADS + h
                s = s_s[ch]
                r = _dot(qp_s[c, ch], s.astype(BF16))
                o_ref[b, pl.ds(r0, CHUNK), h * HEAD_DIM:(h + 1) * HEAD_DIM] = r[:CHUNK] + op_s[c, ch]
                s_s[ch] = a_s[c, ch][0:1, :] * s - r[CHUNK:] + n_s[c, ch]
        return carry

    lax.fori_loop(0, n_chunks, scan_chunk, 0)


def _gdn_call(pq, pk, pv, pgb, q, k, v, gb, chunks_per_step):
    nb, seq, _ = q.shape
    rows = chunks_per_step * CHUNK
    steps = seq // rows
    tok = lambda width: pl.BlockSpec((nb, rows, width), lambda i: (0, i, 0))
    pre = lambda width: pl.BlockSpec((None, CHUNK, width), lambda i: (0, 0, 0))
    nch = nb * HEADS
    return pl.pallas_call(
        _gdn_kernel,
        grid=(steps,),
        in_specs=[pre(GDN_W), pre(GDN_W), pre(GDN_W), pre(LANES),
                  tok(GDN_W), tok(GDN_W), tok(GDN_W), tok(LANES)],
        out_specs=tok(GDN_W),
        out_shape=jax.ShapeDtypeStruct((nb, seq, GDN_W), F32),
        scratch_shapes=[
            pltpu.VMEM((nch, HEAD_DIM, HEAD_DIM), F32),
            pltpu.VMEM((chunks_per_step, nch, CHUNK + HEAD_DIM, HEAD_DIM), BF16),
            pltpu.VMEM((chunks_per_step, nch, CHUNK, HEAD_DIM), F32),
            pltpu.VMEM((chunks_per_step, nch, HEAD_DIM, HEAD_DIM), F32),
            pltpu.VMEM((chunks_per_step, nch, 8, HEAD_DIM), F32),
        ],
        compiler_params=pltpu.CompilerParams(
            dimension_semantics=("arbitrary",), vmem_limit_bytes=VMEM_LIMIT),
        name="gdn",
    )(pq, pk, pv, pgb, q, k, v, gb)


def _mix_out_kernel(x_ref, ya_ref, o_ref, z_ref, gnw_ref, woa_ref, wob_ref, fnw_ref, wr_ref,
                    wrh_ref, br_ref, h_ref, xn_ref, route_ref, assign_ref, counts_ref, cnt_s):
    rows = x_ref.shape[0]
    sub = MIX_SUB_ROWS if rows % MIX_SUB_ROWS == 0 else rows
    blocks = [slice(r, r + sub) for r in range(0, rows, sub)]

    def gated_heads(rs):
        yb = []
        for h in range(HEADS):
            sl = slice(h * HEAD_DIM, (h + 1) * HEAD_DIM)
            oh = o_ref[rs, sl]
            zh = z_ref[rs, sl]
            on = oh * lax.rsqrt(jnp.mean(oh * oh, axis=-1, keepdims=True) + EPS) * gnw_ref[...]
            yb.append((on * (zh * _sigmoid(zh))).astype(BF16))
        return jnp.concatenate(yb, axis=1)

    yb = [gated_heads(rs) for rs in blocks]
    h1 = [x_ref[rs, :] + (_dot(ya_ref[rs, :], woa_ref[...]) + _dot(yb_i, wob_ref[...]))
          for rs, yb_i in zip(blocks, yb)]
    for rs, h1_i in zip(blocks, h1):
        h_ref[rs, :] = h1_i
    xn = [h1_i * lax.rsqrt(jnp.mean(h1_i * h1_i, axis=-1, keepdims=True) + EPS) * fnw_ref[...]
          for h1_i in h1]
    split = [_split_bf16(xn_i) for xn_i in xn]
    for rs, (x_hi, _) in zip(blocks, split):
        xn_ref[rs, :] = _pack_bf16_pairs(x_hi)
    hi_part = [_dot(x_hi, wr_ref[...]) for x_hi, _ in split]
    lo_part = [_dot(x_lo, wrh_ref[...]) for _, x_lo in split]
    logits = [hp[:, :LANES] + hp[:, LANES:] + lp + br_ref[...] for hp, lp in zip(hi_part, lo_part)]

    lane = lax.broadcasted_iota(jnp.int32, (sub, LANES), 1).astype(F32)
    neg = -jnp.inf
    big = float(1 << 20)

    def argmax_first(vals):
        m = jnp.max(vals, axis=-1, keepdims=True)
        idx = jnp.min(jnp.where(vals == m, lane, big), axis=-1, keepdims=True)
        return m, idx

    def top_k(lg):
        grp = jnp.where(lane < N_GROUPS, lg, neg)
        g_max, g_sel = argmax_first(grp)
        p_grp = 1.0 / jnp.sum(jnp.exp(grp - g_max), axis=-1, keepdims=True)
        lo_lane = N_GROUPS + g_sel * EXPERTS_PER_GROUP
        ex = jnp.where((lane >= lo_lane) & (lane < lo_lane + EXPERTS_PER_GROUP), lg, neg)
        m1, i1 = argmax_first(ex)
        m2, i2 = argmax_first(jnp.where(lane == i1, neg, ex))
        e2 = jnp.exp(m2 - m1)
        return i1 - N_GROUPS, i2 - N_GROUPS, 1.0 / (1.0 + e2) * p_grp, e2 / (1.0 + e2) * p_grp

    picks = [top_k(lg) for lg in logits]

    @pl.when(pl.program_id(0) == 0)
    def _():
        cnt_s[...] = jnp.zeros(cnt_s.shape, F32)

    onehots = [((lane == e1).astype(F32), (lane == e2).astype(F32)) for e1, e2, _, _ in picks]
    earlier = (lax.broadcasted_iota(jnp.int32, (sub, sub), 0)
               > lax.broadcasted_iota(jnp.int32, (sub, sub), 1)).astype(BF16)
    within = [_dot(earlier, (oh1 + oh2).astype(BF16)) for oh1, oh2 in onehots]
    counts = cnt_s[...]
    for rs, (e1, e2, w1, w2), (oh1, oh2), within_i in zip(blocks, picks, onehots, within):
        before = within_i + counts
        rank1 = jnp.sum(before * oh1, axis=-1, keepdims=True)
        rank2 = jnp.sum(before * oh2, axis=-1, keepdims=True)
        counts = counts + jnp.sum(oh1 + oh2, axis=0, keepdims=True)
        route = jnp.zeros((sub, LANES), F32)
        for k, val in ((ROUTE_E1, e1), (ROUTE_E2, e2), (ROUTE_W1, w1), (ROUTE_W2, w2),
                       (ROUTE_RANK1, rank1), (ROUTE_RANK2, rank2)):
            route = jnp.where(lane == k, val, route)
        route_ref[rs, :] = route
        assign_ref[:, rs] = route.T[:ASSIGN_ROWS].astype(jnp.int32)
    cnt_s[...] = counts
    counts_ref[...] = counts


def _mix_out_call(x2, ya, o, z, w, rows):
    tokens = x2.shape[0]
    tok = lambda width: pl.BlockSpec((rows, width), lambda i: (i, 0))
    full = lambda a: pl.BlockSpec(a.shape, lambda i: (0,) * a.ndim)
    consts = (w['gdn_norm_w'], w['w_out_a'], w['w_out_b'], w['ffn_norm_w'], w['w_route'],
              w['w_route_hi'], w['b_route'])
    return pl.pallas_call(
        _mix_out_kernel,
        grid=(tokens // rows,),
        in_specs=[tok(D_MODEL), tok(CONV_CH), tok(GDN_W), tok(GDN_W)] + [full(a) for a in consts],
        out_specs=(tok(D_MODEL), tok(D_MODEL // 2), tok(LANES),
                   pl.BlockSpec((ASSIGN_ROWS, rows), lambda i: (0, i)),
                   pl.BlockSpec((1, LANES), lambda i: (0, 0))),
        out_shape=(jax.ShapeDtypeStruct((tokens, D_MODEL), F32),
                   jax.ShapeDtypeStruct((tokens, D_MODEL // 2), jnp.uint32),
                   jax.ShapeDtypeStruct((tokens, LANES), F32),
                   jax.ShapeDtypeStruct((ASSIGN_ROWS, tokens), jnp.int32),
                   jax.ShapeDtypeStruct((1, LANES), F32)),
        scratch_shapes=[pltpu.VMEM((1, LANES), F32)],
        compiler_params=pltpu.CompilerParams(
            dimension_semantics=("arbitrary",), vmem_limit_bytes=VMEM_LIMIT),
        name="mix_out",
    )(x2, ya, o, z, *consts)


def _sc_row_move(table, idx, out_rows, chunk, scatter, name):
    n = idx.shape[0]
    n_src, width = table.shape
    workers = SC_CORES * SC_SUBCORES
    per_w = n // workers
    assert n % workers == 0 and per_w % (2 * chunk) == 0
    assert n_src % per_w == 0 or not scatter
    pairs = per_w // (2 * chunk)
    mesh = plsc.VectorSubcoreMesh(core_axis_name="c", subcore_axis_name="s",
                                  num_cores=SC_CORES, num_subcores=SC_SUBCORES)

    def body(table_hbm, idx_hbm, out_hbm, idx_v, buf_a, buf_b, sem_ra, sem_rb, sem_wa, sem_wb):
        base = (lax.axis_index("s") * SC_CORES + lax.axis_index("c")) * per_w
        src_base = lax.rem(base, n_src)
        pltpu.sync_copy(idx_hbm.at[pl.ds(base, per_w)], idx_v)

        def read(c, buf, sem):
            off = pl.multiple_of(c * chunk, chunk)
            src = (table_hbm.at[pl.ds(src_base + off, chunk)] if scatter
                   else table_hbm.at[idx_v.at[pl.ds(off, chunk)]])
            return pltpu.make_async_copy(src, buf, sem)

        def write(c, buf, sem):
            off = pl.multiple_of(c * chunk, chunk)
            dst = (out_hbm.at[idx_v.at[pl.ds(off, chunk)]] if scatter
                   else out_hbm.at[pl.ds(base + off, chunk)])
            return pltpu.make_async_copy(buf, dst, sem)

        read(0, buf_a, sem_ra).start()

        @pl.loop(0, pairs)
        def _(j):
            ca = 2 * j
            cb = ca + 1
            read(cb, buf_b, sem_rb).start()
            read(ca, buf_a, sem_ra).wait()
            write(ca, buf_a, sem_wa).start()
            read(cb, buf_b, sem_rb).wait()
            write(cb, buf_b, sem_wb).start()
            write(ca, buf_a, sem_wa).wait()

            @pl.when(j + 1 < pairs)
            def _():
                read(ca + 2, buf_a, sem_ra).start()

            write(cb, buf_b, sem_wb).wait()

    return pl.kernel(
        body,
        out_type=jax.ShapeDtypeStruct((out_rows, width), table.dtype),
        mesh=mesh,
        scratch_types=[pltpu.VMEM((per_w,), jnp.int32),
                       pltpu.VMEM((chunk, width), table.dtype),
                       pltpu.VMEM((chunk, width), table.dtype),
                       pltpu.SemaphoreType.DMA, pltpu.SemaphoreType.DMA,
                       pltpu.SemaphoreType.DMA, pltpu.SemaphoreType.DMA],
        name=name,
    )(table, idx)


def _experts_kernel(bexp_ref, bvalid_ref, nused_ref, x_ref, wg_hbm, wu_hbm, wd_hbm, y_ref,
                    wg_s, wu_s, wd_s, stage_g, stage_u, stage_d, sems, seq_s):
    n_used = nused_ref[0]
    n_last = bexp_ref.shape[0] - 1
    rows = x_ref.shape[0] // EXPERT_STEP_BLOCKS
    sub = EXPERT_SUB_ROWS if rows % EXPERT_SUB_ROWS == 0 else rows

    def weight_copies(e, slot):
        return (pltpu.make_async_copy(wg_hbm.at[e], stage_g.at[slot], sems.at[slot, 0]),
                pltpu.make_async_copy(wu_hbm.at[e], stage_u.at[slot], sems.at[slot, 1]),
                pltpu.make_async_copy(wd_hbm.at[e], stage_d.at[slot], sems.at[slot, 2]))

    def block(i, r0):
        expert = bexp_ref[i]
        first_of_expert = (i == 0) | (expert != bexp_ref[jnp.maximum(i - 1, 0)])

        @pl.when((i == 0) & (n_used > 0))
        def _():
            seq_s[0] = 0
            for copy in weight_copies(expert, 0):
                copy.start()

        @pl.when((i < n_used) & first_of_expert)
        def _():
            @pl.when(i > 0)
            def _():
                seq_s[0] = seq_s[0] + 1

            slot = seq_s[0] % 2
            nxt = lax.while_loop(
                lambda j: (j < n_used) & (bexp_ref[jnp.minimum(j, n_last)] == expert),
                lambda j: j + 1, i + 1)

            @pl.when(nxt < n_used)
            def _():
                for copy in weight_copies(bexp_ref[jnp.minimum(nxt, n_last)], 1 - slot):
                    copy.start()

            for copy in weight_copies(expert, slot):
                copy.wait()
            wg_s[...] = stage_g[slot].astype(BF16)
            wu_s[...] = stage_u[slot].astype(BF16)
            wd_s[...] = stage_d[slot].astype(BF16)

        @pl.when(i < n_used)
        def _():
            blocks = [slice(r, r + sub) for r in range(r0, r0 + rows, sub)]
            row = lax.broadcasted_iota(jnp.int32, (sub, x_ref.shape[1]), 0)
            xb = [jnp.concatenate(_unpack_bf16_pairs(
                      jnp.where(row + (rs.start - r0) < bvalid_ref[i], x_ref[rs, :], jnp.uint32(0))),
                      axis=1).astype(BF16) for rs in blocks]
            gate = [_dot(xb_i, wg_s[...]) for xb_i in xb]
            up = [_dot(xb_i, wu_s[...]) for xb_i in xb]
            hid = [((g_i * _sigmoid(g_i)) * u_i).astype(BF16) for g_i, u_i in zip(gate, up)]
            y = [_dot(h_i, wd_s[...]) for h_i in hid]
            for rs, y_i in zip(blocks, y):
                y_ref[rs, :] = _pack_bf16_pairs(y_i.astype(BF16))

        @pl.when(i >= n_used)
        def _():
            y_ref[r0:r0 + rows, :] = jnp.zeros((rows, y_ref.shape[1]), jnp.uint32)

    for j in range(EXPERT_STEP_BLOCKS):
        block(pl.program_id(0) * EXPERT_STEP_BLOCKS + j, j * rows)


def _experts_call(block_expert, block_valid, n_used, x_rows, w_gate, w_up, w_down, rows):
    n_blocks = block_expert.shape[0]
    step_rows = EXPERT_STEP_BLOCKS * rows
    hbm = pl.BlockSpec(memory_space=pl.ANY)
    grid_spec = pltpu.PrefetchScalarGridSpec(
        num_scalar_prefetch=3,
        grid=(n_blocks // EXPERT_STEP_BLOCKS,),
        in_specs=[pl.BlockSpec((step_rows, D_MODEL // 2), lambda i, be, bv, nu: (i, 0)),
                  hbm, hbm, hbm],
        out_specs=pl.BlockSpec((step_rows, D_MODEL // 2), lambda i, be, bv, nu: (i, 0)),
        scratch_shapes=[pltpu.VMEM((D_MODEL, D_EXPERT), BF16),
                        pltpu.VMEM((D_MODEL, D_EXPERT), BF16),
                        pltpu.VMEM((D_EXPERT, D_MODEL), BF16),
                        pltpu.VMEM((2, D_MODEL, D_EXPERT), F32),
                        pltpu.VMEM((2, D_MODEL, D_EXPERT), F32),
                        pltpu.VMEM((2, D_EXPERT, D_MODEL), F32),
                        pltpu.SemaphoreType.DMA((2, 3)),
                        pltpu.SMEM((1,), jnp.int32)],
    )
    return pl.pallas_call(
        _experts_kernel,
        grid_spec=grid_spec,
        out_shape=jax.ShapeDtypeStruct((n_blocks * rows, D_MODEL // 2), jnp.uint32),
        compiler_params=pltpu.CompilerParams(
            dimension_semantics=("arbitrary",), vmem_limit_bytes=VMEM_LIMIT),
        name="experts",
    )(block_expert, block_valid, n_used, x_rows, w_gate, w_up, w_down)


def _combine_kernel(y1_ref, y2_ref, h_ref, route_ref, fw_ref, out_ref):
    route = route_ref[...]
    y1 = jnp.concatenate(_unpack_bf16_pairs(y1_ref[...]), axis=1)
    y2 = jnp.concatenate(_unpack_bf16_pairs(y2_ref[...]), axis=1)
    moe = route[:, ROUTE_W1:ROUTE_W1 + 1] * y1 + route[:, ROUTE_W2:ROUTE_W2 + 1] * y2
    h2 = h_ref[...] + moe
    out_ref[...] = h2 * lax.rsqrt(jnp.mean(h2 * h2, axis=-1, keepdims=True) + EPS) * fw_ref[...]


def _combine_call(y_tok, h1, route, final_w, rows):
    tokens = h1.shape[0]
    steps = tokens // rows
    tok = lambda width: pl.BlockSpec((rows, width), lambda i: (i, 0))
    return pl.pallas_call(
        _combine_kernel,
        grid=(steps,),
        in_specs=[tok(D_MODEL // 2), pl.BlockSpec((rows, D_MODEL // 2), lambda i: (i + steps, 0)),
                  tok(D_MODEL), tok(LANES), pl.BlockSpec((1, D_MODEL), lambda i: (0, 0))],
        out_specs=tok(D_MODEL),
        out_shape=jax.ShapeDtypeStruct((tokens, D_MODEL), F32),
        compiler_params=pltpu.CompilerParams(
            dimension_semantics=("arbitrary",), vmem_limit_bytes=VMEM_LIMIT),
        name="combine",
    )(y_tok, y_tok, h1, route, final_w)


def _dispatch_tables(assign, counts, tokens, rows):
    n_blocks = pl.cdiv((tokens * TOP_K + N_EXPERTS * (rows - 1)) // rows,
                       EXPERT_STEP_BLOCKS) * EXPERT_STEP_BLOCKS
    counts = counts[0, :N_EXPERTS].astype(jnp.int32)
    padded = (counts + rows - 1) // rows * rows
    pad_end = jnp.cumsum(padded)
    pad_start = pad_end - padded
    experts = assign[ROUTE_E1:ROUTE_E2 + 1]
    ranks = assign[ROUTE_RANK1:ROUTE_RANK2 + 1]
    is_expert = experts[..., None] == jnp.arange(N_EXPERTS, dtype=jnp.int32)
    dest = (jnp.sum(jnp.where(is_expert, pad_start, 0), axis=-1) + ranks).reshape(-1)
    block_start = jnp.arange(n_blocks, dtype=jnp.int32) * rows
    block_expert = jnp.minimum(jnp.sum(block_start[:, None] >= pad_end[None, :], axis=1),
                               N_EXPERTS - 1).astype(jnp.int32)
    block_valid = jnp.clip(pad_start[block_expert] + counts[block_expert] - block_start, 0, rows)
    n_used = (pad_end[-1] // rows).astype(jnp.int32).reshape(1)
    return dest, block_expert, block_valid.astype(jnp.int32), n_used, n_blocks


def _prepare_weights(mix_norm_w, w_in, conv_mix_w, conv_mix_norm_w, qkv_conv_w, a_log, dt_bias,
                     gdn_norm_w, w_out, ffn_norm_w, w_group, b_group, w_router, b_router):
    pad_lanes = lambda v: jnp.pad(v.reshape(1, -1), ((0, 0), (0, LANES - v.size)))
    w_in_bf = w_in.astype(BF16)
    w_ab = jnp.pad(w_in_bf[:, Z_COL0 + GDN_W:], ((0, 0), (0, LANES - 2 * HEADS)))
    grp = jnp.arange(PROJ_COLS) // CONV_GROUP_W
    gmat = jnp.where(grp[:, None] == grp[None, :], 1.0 / CONV_GROUP_W, 0.0).astype(BF16)
    w_route = jnp.concatenate([w_group, w_router.reshape(D_MODEL, N_EXPERTS)], axis=1)
    w_route = jnp.pad(w_route, ((0, 0), (0, LANES - w_route.shape[1])))
    w_route_hi = w_route.astype(BF16)
    return dict(
        mix_norm_w=mix_norm_w.reshape(1, -1),
        w_a=w_in_bf[:, :QKV_COL0].reshape(D_MODEL, 3, CONV_CH // PROJ_COLS, PROJ_COLS)
        .transpose(0, 2, 1, 3).reshape(D_MODEL, QKV_COL0),
        w_in=w_in_bf,
        w_ab=w_ab,
        conv_mix_w=conv_mix_w,
        conv_mix_norm_w=conv_mix_norm_w.reshape(1, -1),
        gmat=gmat,
        qkv_conv_w=qkv_conv_w,
        a_log=pad_lanes(a_log),
        dt_bias=pad_lanes(dt_bias),
        gdn_norm_w=gdn_norm_w.reshape(1, -1),
        w_out_a=w_out[:CONV_CH].astype(BF16),
        w_out_b=w_out[CONV_CH:].astype(BF16),
        ffn_norm_w=ffn_norm_w.reshape(1, -1),
        w_route=jnp.concatenate([w_route_hi, (w_route - w_route_hi.astype(F32)).astype(BF16)], axis=1),
        w_route_hi=w_route_hi,
        b_route=pad_lanes(jnp.concatenate([b_group, b_router.reshape(-1)])),
    )


def _tile(n, preferred):
    return preferred if n % preferred == 0 else n


def kernel(x, meta_tokens, mix_norm_w, w_in, conv_mix_w, conv_mix_norm_w, qkv_conv_w, a_log,
           dt_bias, gdn_norm_w, w_out, ffn_norm_w, w_group, b_group, w_router, b_router, w_gate,
           w_up, w_down, final_norm_w):
    assert mix_norm_w.shape[0] == 1, "single-layer kernel"
    batch, seq, _ = x.shape
    assert seq % CHUNK == 0
    w = _prepare_weights(mix_norm_w[0], w_in[0], conv_mix_w[0], conv_mix_norm_w[0], qkv_conv_w[0],
                         a_log[0], dt_bias[0], gdn_norm_w[0], w_out[0], ffn_norm_w[0], w_group[0],
                         b_group[0], w_router[0], b_router[0])

    prefix = jnp.concatenate([jnp.zeros((CHUNK - N_META, D_MODEL), x.dtype),
                              meta_tokens.astype(x.dtype)], axis=0)[None]
    zero_cu = jnp.zeros((HIST, CONV_CH), F32)
    zero_qkv = jnp.zeros((HIST, 3 * GDN_W), F32)
    _, pq, pk, pv, _, pgb, tail_cu, tail_qkv = _proj_call(prefix, zero_cu, zero_qkv, w, CHUNK)

    ya, q, k, v, z, gb, _, _ = _proj_call(x, tail_cu[0], tail_qkv[0], w, _tile(seq, PROJ_ROWS))
    o = _gdn_call(pq, pk, pv, pgb, q, k, v, gb, _tile(seq // CHUNK, GDN_CHUNKS))

    tokens = batch * seq
    flat = lambda a: a.reshape(tokens, a.shape[-1])
    h1, xn2, route, assign, counts = _mix_out_call(flat(x), flat(ya), flat(o), flat(z), w,
                                           _tile(tokens, MIX_ROWS))

    dest, block_expert, block_valid, n_used, n_blocks = _dispatch_tables(assign, counts, tokens,
                                                                         EXPERT_ROWS)
    x_rows = _sc_row_move(xn2, dest, n_blocks * EXPERT_ROWS, SC_DISPATCH_CHUNK, True,
                          "dispatch_scatter")
    y_rows = _experts_call(block_expert, block_valid, n_used, x_rows, w_gate[0], w_up[0],
                           w_down[0], EXPERT_ROWS)
    y_tok = _sc_row_move(y_rows, dest, TOP_K * tokens, SC_COMBINE_CHUNK, False, "combine_gather")
    out = _combine_call(y_tok, h1, route, final_norm_w.reshape(1, -1), _tile(tokens, COMBINE_ROWS))
    return out.reshape(batch, seq, D_MODEL)
```

```python
import functools

import jax
import jax.numpy as jnp
from jax import lax
from jax.experimental import pallas as pl
from jax.experimental.pallas import tpu as pltpu
from jax.experimental.pallas import tpu_sc as plsc

F32 = jnp.float32
BF16 = jnp.bfloat16
EPS = 1e-6

D_MODEL = 1024
N_META = 16
CONV_CH = 512
CONV_GROUP_W = 64
HEADS = 4
HEAD_DIM = 128
GDN_W = HEADS * HEAD_DIM
QKV_COL0 = 3 * CONV_CH
Z_COL0 = QKV_COL0 + 3 * GDN_W
CHUNK = 64
N_GROUPS = 4
EXPERTS_PER_GROUP = 8
N_EXPERTS = N_GROUPS * EXPERTS_PER_GROUP
TOP_K = 2
D_EXPERT = 512
LANES = 128
ROUTE_E1, ROUTE_E2, ROUTE_RANK1, ROUTE_RANK2, ROUTE_W1, ROUTE_W2 = range(6)
ASSIGN_ROWS = 8
HIST = 8

PROJ_ROWS = 512
PROJ_COLS = 256
PROJ_SUB_ROWS = 256
GDN_CHUNKS = 8
GDN_GROUP = 4
MIX_ROWS = 1024
MIX_SUB_ROWS = 128
EXPERT_ROWS = 256
EXPERT_SUB_ROWS = 128
EXPERT_STEP_BLOCKS = 4
COMBINE_ROWS = 1024
VMEM_LIMIT = 56 * 1024 * 1024
SC_CORES = 2
SC_SUBCORES = 16
SC_DISPATCH_CHUNK = 64
SC_COMBINE_CHUNK = 64


def _dot(a, b):
    return jnp.dot(a, b, preferred_element_type=F32)


def _dot_nt(a, b):
    return lax.dot_general(a, b, (((1,), (1,)), ((), ())), preferred_element_type=F32)


def _dot_tn(a, b):
    return lax.dot_general(a, b, (((0,), (0,)), ((), ())), preferred_element_type=F32)


def _split_bf16(x):
    hi = x.astype(BF16)
    lo = (x - hi.astype(F32)).astype(BF16)
    return hi, lo


def _sigmoid(x):
    return 1.0 / (1.0 + jnp.exp(-x))


def _pack_bf16_pairs(x_bf16):
    bits = pltpu.bitcast(x_bf16.astype(F32), jnp.uint32)
    n = x_bf16.shape[1] // 2
    return (bits[:, :n] >> 16) | (bits[:, n:] & jnp.uint32(0xFFFF0000))


def _unpack_bf16_pairs(packed):
    return (pltpu.bitcast(packed << 16, F32),
            pltpu.bitcast(packed & jnp.uint32(0xFFFF0000), F32))


def _proj_kernel(x_ref, hcu_ref, hqkv_ref, nw_ref, wa_ref, wi_ref, wab_ref, cmw_ref,
                 cmn_ref, gmat_ref, qcw_ref, alog_ref, dtb_ref,
                 ya_ref, q_ref, k_ref, v_ref, z_ref, gb_ref, tcu_ref, tqkv_ref,
                 cu_s, qkv_s, xn_s):
    rows = x_ref.shape[0]
    sub = PROJ_SUB_ROWS if rows % PROJ_SUB_ROWS == 0 else rows
    blocks = [slice(r, r + sub) for r in range(0, rows, sub)]

    @pl.when(pl.program_id(1) == 0)
    def _():
        cu_s[0:HIST, :] = hcu_ref[...]
        qkv_s[0:HIST, :] = hqkv_ref[...]

    x = x_ref[...]
    ms = jnp.mean(x * x, axis=-1, keepdims=True)
    xn_s[...] = (x * lax.rsqrt(ms + EPS) * nw_ref[...]).astype(BF16)

    def causal_conv(buf, cur, w_ref, cols, tail_ref):
        taps = w_ref.shape[0]
        acc = pltpu.roll(cur, taps - 1, axis=0) * w_ref[0:1, cols]
        for j in range(1, taps - 1):
            acc = acc + pltpu.roll(cur, taps - 1 - j, axis=0) * w_ref[j:j + 1, cols]
        acc = acc + cur * w_ref[taps - 1:taps, cols]
        buf[HIST:2 * HIST, cols] = cur[:HIST]
        seam = buf[pl.ds(HIST - taps + 1, HIST), cols] * w_ref[0:1, cols]
        for j in range(1, taps):
            seam = seam + buf[pl.ds(HIST - taps + 1 + j, HIST), cols] * w_ref[j:j + 1, cols]
        tail = cur[sub - HIST:]
        buf[0:HIST, cols] = tail
        tail_ref[:, cols] = tail
        return jnp.concatenate([seam, acc[HIST:]], axis=0)

    def mixer_a_tail(i, rs, pa):
        cols = slice(i * PROJ_COLS, (i + 1) * PROJ_COLS)
        cu = pa[:, PROJ_COLS:2 * PROJ_COLS] * pa[:, 2 * PROJ_COLS:]
        ya = pa[:, :PROJ_COLS] * causal_conv(cu_s, cu, cmw_ref, cols, tcu_ref)
        sq_hi, sq_lo = _split_bf16(ya * ya)
        msg = _dot(sq_hi, gmat_ref[...]) + _dot(sq_lo, gmat_ref[...])
        ya_ref[rs, cols] = (ya * lax.rsqrt(msg + EPS) * cmn_ref[:, cols]).astype(BF16)

    heads_per_chunk = PROJ_COLS // HEAD_DIM

    def qkv_tail(i, rs, pq):
        cols = slice(i * PROJ_COLS, (i + 1) * PROJ_COLS)
        c = causal_conv(qkv_s, pq, qcw_ref, cols, tqkv_ref)
        c = c * _sigmoid(c)
        part, first_head = divmod(i * heads_per_chunk, HEADS)
        for j in range(heads_per_chunk):
            ch = c[:, j * HEAD_DIM:(j + 1) * HEAD_DIM]
            sl = slice((first_head + j) * HEAD_DIM, (first_head + j + 1) * HEAD_DIM)
            if part == 0:
                norm = lax.rsqrt(jnp.sum(ch * ch, axis=-1, keepdims=True) + EPS)
                q_ref[rs, sl] = ch * norm * (HEAD_DIM ** -0.5)
            elif part == 1:
                k_ref[rs, sl] = ch * lax.rsqrt(jnp.sum(ch * ch, axis=-1, keepdims=True) + EPS)
            else:
                v_ref[rs, sl] = ch

    def z_tail(rs, pz):
        z_ref[rs, :] = pz

    def decay_beta_tail(rs, ab):
        sp_in = ab + dtb_ref[...]
        softplus = jnp.maximum(sp_in, 0.0) + jnp.log1p(jnp.exp(-jnp.abs(sp_in)))
        g = -jnp.exp(alog_ref[...]) * softplus
        lane = lax.broadcasted_iota(jnp.int32, ab.shape, 1)
        gb_ref[rs, :] = jnp.where(lane < HEADS, g, jnp.where(lane < 2 * HEADS, _sigmoid(ab), 0.0))

    def matmul(w_ref, w_cols, rs):
        return _dot(xn_s[rs, :], w_ref[:, w_cols])

    stages = []
    for i in range(CONV_CH // PROJ_COLS):
        w_cols = slice(3 * i * PROJ_COLS, 3 * (i + 1) * PROJ_COLS)
        stages += [(functools.partial(matmul, wa_ref, w_cols, rs),
                    functools.partial(mixer_a_tail, i, rs)) for rs in blocks]
    for i in range(3 * GDN_W // PROJ_COLS):
        w_cols = slice(QKV_COL0 + i * PROJ_COLS, QKV_COL0 + (i + 1) * PROJ_COLS)
        stages += [(functools.partial(matmul, wi_ref, w_cols, rs),
                    functools.partial(qkv_tail, i, rs)) for rs in blocks]
    stages += [(functools.partial(matmul, wi_ref, slice(Z_COL0, Z_COL0 + GDN_W), rs),
                functools.partial(z_tail, rs)) for rs in blocks]
    stages += [(functools.partial(matmul, wab_ref, slice(None), rs),
                functools.partial(decay_beta_tail, rs)) for rs in blocks]
    pending = None
    for issue, tail in stages:
        res = issue()
        if pending is not None:
            pending()
        pending = functools.partial(tail, res)
    pending()


def _proj_call(x3, hist_cu, hist_qkv, w, rows):
    nb, seq, _ = x3.shape
    nt = seq // rows
    tok = lambda width: pl.BlockSpec((None, rows, width), lambda b, t: (b, t, 0))
    full = lambda a: pl.BlockSpec(a.shape, lambda b, t: (0,) * a.ndim,
                                  pipeline_mode=pl.Buffered(1))
    tail = lambda width: pl.BlockSpec((None, HIST, width), lambda b, t: (b, 0, 0))
    consts = (hist_cu, hist_qkv, w['mix_norm_w'], w['w_a'], w['w_in'], w['w_ab'],
              w['conv_mix_w'], w['conv_mix_norm_w'], w['gmat'], w['qkv_conv_w'], w['a_log'],
              w['dt_bias'])
    out_shape = (
        jax.ShapeDtypeStruct((nb, seq, CONV_CH), BF16),
        jax.ShapeDtypeStruct((nb, seq, GDN_W), F32),
        jax.ShapeDtypeStruct((nb, seq, GDN_W), F32),
        jax.ShapeDtypeStruct((nb, seq, GDN_W), F32),
        jax.ShapeDtypeStruct((nb, seq, GDN_W), F32),
        jax.ShapeDtypeStruct((nb, seq, LANES), F32),
        jax.ShapeDtypeStruct((nb, HIST, CONV_CH), F32),
        jax.ShapeDtypeStruct((nb, HIST, 3 * GDN_W), F32),
    )
    return pl.pallas_call(
        _proj_kernel,
        grid=(nb, nt),
        in_specs=[tok(D_MODEL)] + [full(a) for a in consts],
        out_specs=(tok(CONV_CH), tok(GDN_W), tok(GDN_W), tok(GDN_W), tok(GDN_W), tok(LANES),
                   tail(CONV_CH), tail(3 * GDN_W)),
        out_shape=out_shape,
        scratch_shapes=[pltpu.VMEM((2 * HIST, CONV_CH), F32),
                        pltpu.VMEM((2 * HIST, 3 * GDN_W), F32),
                        pltpu.VMEM((rows, D_MODEL), BF16)],
        compiler_params=pltpu.CompilerParams(
            dimension_semantics=("arbitrary", "arbitrary"), vmem_limit_bytes=VMEM_LIMIT),
        name="proj",
    )(x3, *consts)


def _chunk_masks():
    row = lax.broadcasted_iota(jnp.int32, (CHUNK, CHUNK), 0)
    col = lax.broadcasted_iota(jnp.int32, (CHUNK, CHUNK), 1)
    incl = row >= col
    strict = row > col
    levels = []
    n = 1
    while n < CHUNK:
        levels.append((row // (2 * n) == col // (2 * n)) & ((row // n) % 2 == 1) & ((col // n) % 2 == 0))
        n *= 2
    return incl, strict, levels


def _chunk_cumsum(gb_blk, incl):
    tri = incl.astype(BF16)
    hi, lo = _split_bf16(gb_blk)
    return _dot(tri, hi) + _dot(tri, lo)


def _chunk_transforms(chains, masks, state_only):
    incl, strict, levels = masks
    eye = (lax.broadcasted_iota(jnp.int32, (CHUNK, CHUNK), 0)
           == lax.broadcasted_iota(jnp.int32, (CHUNK, CHUNK), 1)).astype(F32)
    decay = [jnp.exp(jnp.where(incl, gc_col - gc_row, -jnp.inf))
             for (_, _, _, _, gc_col, gc_row, _) in chains]
    kb = [kh * beta for (_, kh, _, beta, _, _, _) in chains]
    k_bf = [kh.astype(BF16) for (_, kh, _, _, _, _, _) in chains]
    a_mat = [jnp.where(strict, _dot_nt(kb_i.astype(BF16), k_i) * d_i, 0.0)
             for kb_i, k_i, d_i in zip(kb, k_bf, decay)]
    t_inv = [eye - jnp.where(levels[0], a_i, 0.0) for a_i in a_mat]
    for lvl in levels[1:]:
        t_bf = [t_i.astype(BF16) for t_i in t_inv]
        m1 = [_dot(jnp.where(lvl, a_i, 0.0).astype(BF16), t_i) for a_i, t_i in zip(a_mat, t_bf)]
        t_inv = [t_i - _dot(tb_i, m_i.astype(BF16)) for t_i, tb_i, m_i in zip(t_inv, t_bf, m1)]
    rhs = [jnp.concatenate([vh * beta, kb_i * jnp.exp(gc_col)], axis=1)
           for (_, _, vh, beta, gc_col, _, _), kb_i in zip(chains, kb)]
    uw = [_dot(t_i.astype(BF16), r_i.astype(BF16)).astype(BF16)
          for t_i, r_i in zip(t_inv, rhs)]
    kd = [kh * jnp.exp(g_last - gc_col) for (_, kh, _, _, gc_col, _, g_last) in chains]
    pn = [_dot_tn(kd_i.astype(BF16), uw_i) for kd_i, uw_i in zip(kd, uw)]
    if state_only:
        return [pn_i[:, :HEAD_DIM] for pn_i in pn]
    intra = [jnp.where(incl, _dot_nt(qh.astype(BF16), k_i) * d_i, 0.0)
             for (qh, _, _, _, _, _, _), k_i, d_i in zip(chains, k_bf, decay)]
    iuw = [_dot(in_i.astype(BF16), uw_i) for in_i, uw_i in zip(intra, uw)]
    out = []
    for (qh, _, _, _, gc_col, _, g_last), pn_i, iuw_i in zip(chains, pn, iuw):
        q_part = qh * jnp.exp(gc_col) - iuw_i[:, HEAD_DIM:]
        out.append((q_part, pn_i[:, HEAD_DIM:], iuw_i[:, :HEAD_DIM], pn_i[:, :HEAD_DIM],
                    jnp.exp(g_last)))
    return out


def _gdn_kernel(pq_ref, pk_ref, pv_ref, pgb_ref, q_ref, k_ref, v_ref, gb_ref, o_ref,
                s_s, qp_s, op_s, n_s, a_s):
    nb = q_ref.shape[0]
    n_chunks = q_ref.shape[1] // CHUNK
    group = GDN_GROUP if n_chunks % GDN_GROUP == 0 else 1
    masks = _chunk_masks()

    def chains_of(gb_blk, q_blk, k_blk, v_blk):
        gc = _chunk_cumsum(gb_blk, masks[0])
        gc_t = gc.T
        res = []
        for h in range(HEADS):
            sl = slice(h * HEAD_DIM, (h + 1) * HEAD_DIM)
            res.append((q_blk(sl), k_blk(sl), v_blk(sl), gb_blk[:, HEADS + h:HEADS + h + 1],
                        gc[:, h:h + 1], gc_t[h:h + 1, :CHUNK], gc[CHUNK - 1:CHUNK, h:h + 1]))
        return res

    @pl.when(pl.program_id(0) == 0)
    def _():
        chains = chains_of(pgb_ref[...], lambda sl: pq_ref[:, sl], lambda sl: pk_ref[:, sl],
                           lambda sl: pv_ref[:, sl])
        for h, n_mat in enumerate(_chunk_transforms(chains, masks, True)):
            for b in range(nb):
                s_s[b * HEADS + h] = n_mat

    def transform_group(gi, carry):
        chains, where = [], []
        for cc in range(group):
            c = gi * group + cc
            rows = pl.ds(pl.multiple_of(c * CHUNK, CHUNK), CHUNK)
            for b in range(nb):
                chains += chains_of(gb_ref[b, rows, :], lambda sl: q_ref[b, rows, sl],
                                    lambda sl: k_ref[b, rows, sl], lambda sl: v_ref[b, rows, sl])
                where += [(c, b * HEADS + h) for h in range(HEADS)]
        for (c, ch), (q_part, p_mat, o_part, n_mat, a) in zip(
                where, _chunk_transforms(chains, masks, False)):
            qp_s[c, ch, 0:CHUNK, :] = q_part.astype(BF16)
            qp_s[c, ch, CHUNK:, :] = p_mat.astype(BF16)
            op_s[c, ch] = o_part
            n_s[c, ch] = n_mat
            a_s[c, ch] = jnp.broadcast_to(a, (8, HEAD_DIM))
        return carry

    lax.fori_loop(0, n_chunks // group, transform_group, 0)

    def scan_chunk(c, carry):
        r0 = pl.multiple_of(c * CHUNK, CHUNK)
        for b in range(nb):
            for h in range(HEADS):
                ch = b * HEADS + h
                s = s_s[ch]
                r = _dot(qp_s[c, ch], s.astype(BF16))
                o_ref[b, pl.ds(r0, CHUNK), h * HEAD_DIM:(h + 1) * HEAD_DIM] = r[:CHUNK] + op_s[c, ch]
                s_s[ch] = a_s[c, ch][0:1, :] * s - r[CHUNK:] + n_s[c, ch]
        return carry

    lax.fori_loop(0, n_chunks, scan_chunk, 0)


def _gdn_call(pq, pk, pv, pgb, q, k, v, gb, chunks_per_step):
    nb, seq, _ = q.shape
    rows = chunks_per_step * CHUNK
    steps = seq // rows
    tok = lambda width: pl.BlockSpec((nb, rows, width), lambda i: (0, i, 0))
    pre = lambda width: pl.BlockSpec((None, CHUNK, width), lambda i: (0, 0, 0))
    nch = nb * HEADS
    return pl.pallas_call(
        _gdn_kernel,
        grid=(steps,),
        in_specs=[pre(GDN_W), pre(GDN_W), pre(GDN_W), pre(LANES),
                  tok(GDN_W), tok(GDN_W), tok(GDN_W), tok(LANES)],
        out_specs=tok(GDN_W),
        out_shape=jax.ShapeDtypeStruct((nb, seq, GDN_W), F32),
        scratch_shapes=[
            pltpu.VMEM((nch, HEAD_DIM, HEAD_DIM), F32),
            pltpu.VMEM((chunks_per_step, nch, CHUNK + HEAD_DIM, HEAD_DIM), BF16),
            pltpu.VMEM((chunks_per_step, nch, CHUNK, HEAD_DIM), F32),
            pltpu.VMEM((chunks_per_step, nch, HEAD_DIM, HEAD_DIM), F32),
            pltpu.VMEM((chunks_per_step, nch, 8, HEAD_DIM), F32),
        ],
        compiler_params=pltpu.CompilerParams(
            dimension_semantics=("arbitrary",), vmem_limit_bytes=VMEM_LIMIT),
        name="gdn",
    )(pq, pk, pv, pgb, q, k, v, gb)


def _mix_out_kernel(x_ref, ya_ref, o_ref, z_ref, gnw_ref, woa_ref, wob_ref, fnw_ref, wr_ref,
                    wrh_ref, br_ref, h_ref, xn_ref, route_ref, assign_ref, counts_ref, cnt_s):
    rows = x_ref.shape[0]
    sub = MIX_SUB_ROWS if rows % MIX_SUB_ROWS == 0 else rows
    blocks = [slice(r, r + sub) for r in range(0, rows, sub)]

    def gated_heads(rs):
        yb = []
        for h in range(HEADS):
            sl = slice(h * HEAD_DIM, (h + 1) * HEAD_DIM)
            oh = o_ref[rs, sl]
            zh = z_ref[rs, sl]
            on = oh * lax.rsqrt(jnp.mean(oh * oh, axis=-1, keepdims=True) + EPS) * gnw_ref[...]
            yb.append((on * (zh * _sigmoid(zh))).astype(BF16))
        return jnp.concatenate(yb, axis=1)

    yb = [gated_heads(rs) for rs in blocks]
    h1 = [x_ref[rs, :] + (_dot(ya_ref[rs, :], woa_ref[...]) + _dot(yb_i, wob_ref[...]))
          for rs, yb_i in zip(blocks, yb)]
    for rs, h1_i in zip(blocks, h1):
        h_ref[rs, :] = h1_i
    xn = [h1_i * lax.rsqrt(jnp.mean(h1_i * h1_i, axis=-1, keepdims=True) + EPS) * fnw_ref[...]
          for h1_i in h1]
    split = [_split_bf16(xn_i) for xn_i in xn]
    for rs, (x_hi, _) in zip(blocks, split):
        xn_ref[rs, :] = _pack_bf16_pairs(x_hi)
    hi_part = [_dot(x_hi, wr_ref[...]) for x_hi, _ in split]
    lo_part = [_dot(x_lo, wrh_ref[...]) for _, x_lo in split]
    logits = [hp[:, :LANES] + hp[:, LANES:] + lp + br_ref[...] for hp, lp in zip(hi_part, lo_part)]

    lane = lax.broadcasted_iota(jnp.int32, (sub, LANES), 1).astype(F32)
    neg = -jnp.inf
    big = float(1 << 20)

    def argmax_first(vals):
        m = jnp.max(vals, axis=-1, keepdims=True)
        idx = jnp.min(jnp.where(vals == m, lane, big), axis=-1, keepdims=True)
        return m, idx

    def top_k(lg):
        grp = jnp.where(lane < N_GROUPS, lg, neg)
        g_max, g_sel = argmax_first(grp)
        p_grp = 1.0 / jnp.sum(jnp.exp(grp - g_max), axis=-1, keepdims=True)
        lo_lane = N_GROUPS + g_sel * EXPERTS_PER_GROUP
        ex = jnp.where((lane >= lo_lane) & (lane < lo_lane + EXPERTS_PER_GROUP), lg, neg)
        m1, i1 = argmax_first(ex)
        m2, i2 = argmax_first(jnp.where(lane == i1, neg, ex))
        e2 = jnp.exp(m2 - m1)
        return i1 - N_GROUPS, i2 - N_GROUPS, 1.0 / (1.0 + e2) * p_grp, e2 / (1.0 + e2) * p_grp

    picks = [top_k(lg) for lg in logits]

    @pl.when(pl.program_id(0) == 0)
    def _():
        cnt_s[...] = jnp.zeros(cnt_s.shape, F32)

    onehots = [((lane == e1).astype(F32), (lane == e2).astype(F32)) for e1, e2, _, _ in picks]
    earlier = (lax.broadcasted_iota(jnp.int32, (sub, sub), 0)
               > lax.broadcasted_iota(jnp.int32, (sub, sub), 1)).astype(BF16)
    within = [_dot(earlier, (oh1 + oh2).astype(BF16)) for oh1, oh2 in onehots]
    counts = cnt_s[...]
    for rs, (e1, e2, w1, w2), (oh1, oh2), within_i in zip(blocks, picks, onehots, within):
        before = within_i + counts
        rank1 = jnp.sum(before * oh1, axis=-1, keepdims=True)
        rank2 = jnp.sum(before * oh2, axis=-1, keepdims=True)
        counts = counts + jnp.sum(oh1 + oh2, axis=0, keepdims=True)
        route = jnp.zeros((sub, LANES), F32)
        for k, val in ((ROUTE_E1, e1), (ROUTE_E2, e2), (ROUTE_W1, w1), (ROUTE_W2, w2),
                       (ROUTE_RANK1, rank1), (ROUTE_RANK2, rank2)):
            route = jnp.where(lane == k, val, route)
        route_ref[rs, :] = route
        assign_ref[:, rs] = route.T[:ASSIGN_ROWS].astype(jnp.int32)
    cnt_s[...] = counts
    counts_ref[...] = counts


def _mix_out_call(x2, ya, o, z, w, rows):
    tokens = x2.shape[0]
    tok = lambda width: pl.BlockSpec((rows, width), lambda i: (i, 0))
    full = lambda a: pl.BlockSpec(a.shape, lambda i: (0,) * a.ndim)
    consts = (w['gdn_norm_w'], w['w_out_a'], w['w_out_b'], w['ffn_norm_w'], w['w_route'],
              w['w_route_hi'], w['b_route'])
    return pl.pallas_call(
        _mix_out_kernel,
        grid=(tokens // rows,),
        in_specs=[tok(D_MODEL), tok(CONV_CH), tok(GDN_W), tok(GDN_W)] + [full(a) for a in consts],
        out_specs=(tok(D_MODEL), tok(D_MODEL // 2), tok(LANES),
                   pl.BlockSpec((ASSIGN_ROWS, rows), lambda i: (0, i)),
                   pl.BlockSpec((1, LANES), lambda i: (0, 0))),
        out_shape=(jax.ShapeDtypeStruct((tokens, D_MODEL), F32),
                   jax.ShapeDtypeStruct((tokens, D_MODEL // 2), jnp.uint32),
                   jax.ShapeDtypeStruct((tokens, LANES), F32),
                   jax.ShapeDtypeStruct((ASSIGN_ROWS, tokens), jnp.int32),
                   jax.ShapeDtypeStruct((1, LANES), F32)),
        scratch_shapes=[pltpu.VMEM((1, LANES), F32)],
        compiler_params=pltpu.CompilerParams(
            dimension_semantics=("arbitrary",), vmem_limit_bytes=VMEM_LIMIT),
        name="mix_out",
    )(x2, ya, o, z, *consts)


def _sc_row_move(table, idx, out_rows, chunk, scatter, name):
    n = idx.shape[0]
    n_src, width = table.shape
    workers = SC_CORES * SC_SUBCORES
    per_w = n // workers
    assert n % workers == 0 and per_w % (2 * chunk) == 0
    assert n_src % per_w == 0 or not scatter
    pairs = per_w // (2 * chunk)
    mesh = plsc.VectorSubcoreMesh(core_axis_name="c", subcore_axis_name="s",
                                  num_cores=SC_CORES, num_subcores=SC_SUBCORES)

    def body(table_hbm, idx_hbm, out_hbm, idx_v, buf_a, buf_b, sem_ra, sem_rb, sem_wa, sem_wb):
        base = (lax.axis_index("s") * SC_CORES + lax.axis_index("c")) * per_w
        src_base = lax.rem(base, n_src)
        pltpu.sync_copy(idx_hbm.at[pl.ds(base, per_w)], idx_v)

        def read(c, buf, sem):
            off = pl.multiple_of(c * chunk, chunk)
            src = (table_hbm.at[pl.ds(src_base + off, chunk)] if scatter
                   else table_hbm.at[idx_v.at[pl.ds(off, chunk)]])
            return pltpu.make_async_copy(src, buf, sem)

        def write(c, buf, sem):
            off = pl.multiple_of(c * chunk, chunk)
            dst = (out_hbm.at[idx_v.at[pl.ds(off, chunk)]] if scatter
                   else out_hbm.at[pl.ds(base + off, chunk)])
            return pltpu.make_async_copy(buf, dst, sem)

        read(0, buf_a, sem_ra).start()

        @pl.loop(0, pairs)
        def _(j):
            ca = 2 * j
            cb = ca + 1
            read(cb, buf_b, sem_rb).start()
            read(ca, buf_a, sem_ra).wait()
            write(ca, buf_a, sem_wa).start()
            read(cb, buf_b, sem_rb).wait()
            write(cb, buf_b, sem_wb).start()
            write(ca, buf_a, sem_wa).wait()

            @pl.when(j + 1 < pairs)
            def _():
                read(ca + 2, buf_a, sem_ra).start()

            write(cb, buf_b, sem_wb).wait()

    return pl.kernel(
        body,
        out_type=jax.ShapeDtypeStruct((out_rows, width), table.dtype),
        mesh=mesh,
        scratch_types=[pltpu.VMEM((per_w,), jnp.int32),
                       pltpu.VMEM((chunk, width), table.dtype),
                       pltpu.VMEM((chunk, width), table.dtype),
                       pltpu.SemaphoreType.DMA, pltpu.SemaphoreType.DMA,
                       pltpu.SemaphoreType.DMA, pltpu.SemaphoreType.DMA],
        name=name,
    )(table, idx)


def _experts_kernel(bexp_ref, bvalid_ref, nused_ref, x_ref, wg_hbm, wu_hbm, wd_hbm, y_ref,
                    wg_s, wu_s, wd_s, stage_g, stage_u, stage_d, sems, seq_s):
    n_used = nused_ref[0]
    n_last = bexp_ref.shape[0] - 1
    rows = x_ref.shape[0] // EXPERT_STEP_BLOCKS
    sub = EXPERT_SUB_ROWS if rows % EXPERT_SUB_ROWS == 0 else rows

    def weight_copies(e, slot):
        return (pltpu.make_async_copy(wg_hbm.at[e], stage_g.at[slot], sems.at[slot, 0]),
                pltpu.make_async_copy(wu_hbm.at[e], stage_u.at[slot], sems.at[slot, 1]),
                pltpu.make_async_copy(wd_hbm.at[e], stage_d.at[slot], sems.at[slot, 2]))

    def block(i, r0):
        expert = bexp_ref[i]
        first_of_expert = (i == 0) | (expert != bexp_ref[jnp.maximum(i - 1, 0)])

        @pl.when((i == 0) & (n_used > 0))
        def _():
            seq_s[0] = 0
            for copy in weight_copies(expert, 0):
                copy.start()

        @pl.when((i < n_used) & first_of_expert)
        def _():
            @pl.when(i > 0)
            def _():
                seq_s[0] = seq_s[0] + 1

            slot = seq_s[0] % 2
            nxt = lax.while_loop(
                lambda j: (j < n_used) & (bexp_ref[jnp.minimum(j, n_last)] == expert),
                lambda j: j + 1, i + 1)

            @pl.when(nxt < n_used)
            def _():
                for copy in weight_copies(bexp_ref[jnp.minimum(nxt, n_last)], 1 - slot):
                    copy.start()

            for copy in weight_copies(expert, slot):
                copy.wait()
            wg_s[...] = stage_g[slot].astype(BF16)
            wu_s[...] = stage_u[slot].astype(BF16)
            wd_s[...] = stage_d[slot].astype(BF16)

        @pl.when(i < n_used)
        def _():
            blocks = [slice(r, r + sub) for r in range(r0, r0 + rows, sub)]
            row = lax.broadcasted_iota(jnp.int32, (sub, x_ref.shape[1]), 0)
            xb = [jnp.concatenate(_unpack_bf16_pairs(
                      jnp.where(row + (rs.start - r0) < bvalid_ref[i], x_ref[rs, :], jnp.uint32(0))),
                      axis=1).astype(BF16) for rs in blocks]
            gate = [_dot(xb_i, wg_s[...]) for xb_i in xb]
            up = [_dot(xb_i, wu_s[...]) for xb_i in xb]
            hid = [((g_i * _sigmoid(g_i)) * u_i).astype(BF16) for g_i, u_i in zip(gate, up)]
            y = [_dot(h_i, wd_s[...]) for h_i in hid]
            for rs, y_i in zip(blocks, y):
                y_ref[rs, :] = _pack_bf16_pairs(y_i.astype(BF16))

        @pl.when(i >= n_used)
        def _():
            y_ref[r0:r0 + rows, :] = jnp.zeros((rows, y_ref.shape[1]), jnp.uint32)

    for j in range(EXPERT_STEP_BLOCKS):
        block(pl.program_id(0) * EXPERT_STEP_BLOCKS + j, j * rows)


def _experts_call(block_expert, block_valid, n_used, x_rows, w_gate, w_up, w_down, rows):
    n_blocks = block_expert.shape[0]
    step_rows = EXPERT_STEP_BLOCKS * rows
    hbm = pl.BlockSpec(memory_space=pl.ANY)
    grid_spec = pltpu.PrefetchScalarGridSpec(
        num_scalar_prefetch=3,
        grid=(n_blocks // EXPERT_STEP_BLOCKS,),
        in_specs=[pl.BlockSpec((step_rows, D_MODEL // 2), lambda i, be, bv, nu: (i, 0)),
                  hbm, hbm, hbm],
        out_specs=pl.BlockSpec((step_rows, D_MODEL // 2), lambda i, be, bv, nu: (i, 0)),
        scratch_shapes=[pltpu.VMEM((D_MODEL, D_EXPERT), BF16),
                        pltpu.VMEM((D_MODEL, D_EXPERT), BF16),
                        pltpu.VMEM((D_EXPERT, D_MODEL), BF16),
                        pltpu.VMEM((2, D_MODEL, D_EXPERT), F32),
                        pltpu.VMEM((2, D_MODEL, D_EXPERT), F32),
                        pltpu.VMEM((2, D_EXPERT, D_MODEL), F32),
                        pltpu.SemaphoreType.DMA((2, 3)),
                        pltpu.SMEM((1,), jnp.int32)],
    )
    return pl.pallas_call(
        _experts_kernel,
        grid_spec=grid_spec,
        out_shape=jax.ShapeDtypeStruct((n_blocks * rows, D_MODEL // 2), jnp.uint32),
        compiler_params=pltpu.CompilerParams(
            dimension_semantics=("arbitrary",), vmem_limit_bytes=VMEM_LIMIT),
        name="experts",
    )(block_expert, block_valid, n_used, x_rows, w_gate, w_up, w_down)


def _combine_kernel(y1_ref, y2_ref, h_ref, route_ref, fw_ref, out_ref):
    route = route_ref[...]
    y1 = jnp.concatenate(_unpack_bf16_pairs(y1_ref[...]), axis=1)
    y2 = jnp.concatenate(_unpack_bf16_pairs(y2_ref[...]), axis=1)
    moe = route[:, ROUTE_W1:ROUTE_W1 + 1] * y1 + route[:, ROUTE_W2:ROUTE_W2 + 1] * y2
    h2 = h_ref[...] + moe
    out_ref[...] = h2 * lax.rsqrt(jnp.mean(h2 * h2, axis=-1, keepdims=True) + EPS) * fw_ref[...]


def _combine_call(y_tok, h1, route, final_w, rows):
    tokens = h1.shape[0]
    steps = tokens // rows
    tok = lambda width: pl.BlockSpec((rows, width), lambda i: (i, 0))
    return pl.pallas_call(
        _combine_kernel,
        grid=(steps,),
        in_specs=[tok(D_MODEL // 2), pl.BlockSpec((rows, D_MODEL // 2), lambda i: (i + steps, 0)),
                  tok(D_MODEL), tok(LANES), pl.BlockSpec((1, D_MODEL), lambda i: (0, 0))],
        out_specs=tok(D_MODEL),
        out_shape=jax.ShapeDtypeStruct((tokens, D_MODEL), F32),
        compiler_params=pltpu.CompilerParams(
            dimension_semantics=("arbitrary",), vmem_limit_bytes=VMEM_LIMIT),
        name="combine",
    )(y_tok, y_tok, h1, route, final_w)


def _dispatch_tables(assign, counts, tokens, rows):
    n_blocks = pl.cdiv((tokens * TOP_K + N_EXPERTS * (rows - 1)) // rows,
                       EXPERT_STEP_BLOCKS) * EXPERT_STEP_BLOCKS
    counts = counts[0, :N_EXPERTS].astype(jnp.int32)
    padded = (counts + rows - 1) // rows * rows
    pad_end = jnp.cumsum(padded)
    pad_start = pad_end - padded
    experts = assign[ROUTE_E1:ROUTE_E2 + 1]
    ranks = assign[ROUTE_RANK1:ROUTE_RANK2 + 1]
    is_expert = experts[..., None] == jnp.arange(N_EXPERTS, dtype=jnp.int32)
    dest = (jnp.sum(jnp.where(is_expert, pad_start, 0), axis=-1) + ranks).reshape(-1)
    block_start = jnp.arange(n_blocks, dtype=jnp.int32) * rows
    in_run = (block_start[:, None] >= pad_start[None, :]) & (block_start[:, None] < pad_end[None, :])
    lookup = lambda table: jnp.sum(jnp.where(in_run, table[None, :], 0), axis=1)
    block_expert = lookup(jnp.arange(N_EXPERTS, dtype=jnp.int32))
    block_valid = jnp.clip(lookup(pad_start + counts) - block_start, 0, rows)
    n_used = (pad_end[-1] // rows).astype(jnp.int32).reshape(1)
    return dest, block_expert, block_valid.astype(jnp.int32), n_used, n_blocks


def _prepare_weights(mix_norm_w, w_in, conv_mix_w, conv_mix_norm_w, qkv_conv_w, a_log, dt_bias,
                     gdn_norm_w, w_out, ffn_norm_w, w_group, b_group, w_router, b_router):
    pad_lanes = lambda v: jnp.pad(v.reshape(1, -1), ((0, 0), (0, LANES - v.size)))
    w_in_bf = w_in.astype(BF16)
    w_ab = jnp.pad(w_in_bf[:, Z_COL0 + GDN_W:], ((0, 0), (0, LANES - 2 * HEADS)))
    grp = jnp.arange(PROJ_COLS) // CONV_GROUP_W
    gmat = jnp.where(grp[:, None] == grp[None, :], 1.0 / CONV_GROUP_W, 0.0).astype(BF16)
    w_route = jnp.concatenate([w_group, w_router.reshape(D_MODEL, N_EXPERTS)], axis=1)
    w_route = jnp.pad(w_route, ((0, 0), (0, LANES - w_route.shape[1])))
    w_route_hi = w_route.astype(BF16)
    return dict(
        mix_norm_w=mix_norm_w.reshape(1, -1),
        w_a=jnp.concatenate([w_in_bf[:, part * CONV_CH + i * PROJ_COLS:][:, :PROJ_COLS]
                             for i in range(CONV_CH // PROJ_COLS) for part in range(3)], axis=1),
        w_in=w_in_bf,
        w_ab=w_ab,
        conv_mix_w=conv_mix_w,
        conv_mix_norm_w=conv_mix_norm_w.reshape(1, -1),
        gmat=gmat,
        qkv_conv_w=qkv_conv_w,
        a_log=pad_lanes(a_log),
        dt_bias=pad_lanes(dt_bias),
        gdn_norm_w=gdn_norm_w.reshape(1, -1),
        w_out_a=w_out[:CONV_CH].astype(BF16),
        w_out_b=w_out[CONV_CH:].astype(BF16),
        ffn_norm_w=ffn_norm_w.reshape(1, -1),
        w_route=jnp.concatenate([w_route_hi, (w_route - w_route_hi.astype(F32)).astype(BF16)], axis=1),
        w_route_hi=w_route_hi,
        b_route=pad_lanes(jnp.concatenate([b_group, b_router.reshape(-1)])),
    )


def _tile(n, preferred):
    return preferred if n % preferred == 0 else n


def kernel(x, meta_tokens, mix_norm_w, w_in, conv_mix_w, conv_mix_norm_w, qkv_conv_w, a_log,
           dt_bias, gdn_norm_w, w_out, ffn_norm_w, w_group, b_group, w_router, b_router, w_gate,
           w_up, w_down, final_norm_w):
    assert mix_norm_w.shape[0] == 1, "single-layer kernel"
    batch, seq, _ = x.shape
    assert seq % CHUNK == 0
    w = _prepare_weights(mix_norm_w[0], w_in[0], conv_mix_w[0], conv_mix_norm_w[0], qkv_conv_w[0],
                         a_log[0], dt_bias[0], gdn_norm_w[0], w_out[0], ffn_norm_w[0], w_group[0],
                         b_group[0], w_router[0], b_router[0])

    prefix = jnp.concatenate([jnp.zeros((CHUNK - N_META, D_MODEL), x.dtype),
                              meta_tokens.astype(x.dtype)], axis=0)[None]
    zero_cu = jnp.zeros((HIST, CONV_CH), F32)
    zero_qkv = jnp.zeros((HIST, 3 * GDN_W), F32)
    _, pq, pk, pv, _, pgb, tail_cu, tail_qkv = _proj_call(prefix, zero_cu, zero_qkv, w, CHUNK)

    ya, q, k, v, z, gb, _, _ = _proj_call(x, tail_cu[0], tail_qkv[0], w, _tile(seq, PROJ_ROWS))
    o = _gdn_call(pq, pk, pv, pgb, q, k, v, gb, _tile(seq // CHUNK, GDN_CHUNKS))

    tokens = batch * seq
    flat = lambda a: a.reshape(tokens, a.shape[-1])
    h1, xn2, route, assign, counts = _mix_out_call(flat(x), flat(ya), flat(o), flat(z), w,
                                           _tile(tokens, MIX_ROWS))

    dest, block_expert, block_valid, n_used, n_blocks = _dispatch_tables(assign, counts, tokens,
                                                                         EXPERT_ROWS)
    x_rows = _sc_row_move(xn2, dest, n_blocks * EXPERT_ROWS, SC_DISPATCH_CHUNK, True,
                          "dispatch_scatter")
    y_rows = _experts_call(block_expert, block_valid, n_used, x_rows, w_gate[0], w_up[0],
                           w_down[0], EXPERT_ROWS)
    y_tok = _sc_row_move(y_rows, dest, TOP_K * tokens, SC_COMBINE_CHUNK, False, "combine_gather")
    out = _combine_call(y_tok, h1, route, final_norm_w.reshape(1, -1), _tile(tokens, COMBINE_ROWS))
    return out.reshape(batch, seq, D_MODEL)
```

```python
import functools

import jax
import jax.numpy as jnp
from jax import lax
from jax.experimental import pallas as pl
from jax.experimental.pallas import tpu as pltpu
from jax.experimental.pallas import tpu_sc as plsc

F32 = jnp.float32
BF16 = jnp.bfloat16
EPS = 1e-6

D_MODEL = 1024
N_META = 16
CONV_CH = 512
CONV_GROUP_W = 64
HEADS = 4
HEAD_DIM = 128
GDN_W = HEADS * HEAD_DIM
QKV_COL0 = 3 * CONV_CH
Z_COL0 = QKV_COL0 + 3 * GDN_W
CHUNK = 64
N_GROUPS = 4
EXPERTS_PER_GROUP = 8
N_EXPERTS = N_GROUPS * EXPERTS_PER_GROUP
TOP_K = 2
D_EXPERT = 512
LANES = 128
ROUTE_E1, ROUTE_E2, ROUTE_RANK1, ROUTE_RANK2, ROUTE_W1, ROUTE_W2 = range(6)
ASSIGN_ROWS = 8
HIST = 8

PROJ_ROWS = 512
PROJ_COLS = 256
PROJ_SUB_ROWS = 256
GDN_CHUNKS = 8
GDN_GROUP = 4
MIX_ROWS = 1024
MIX_SUB_ROWS = 128
EXPERT_ROWS = 256
EXPERT_SUB_ROWS = 128
EXPERT_STEP_BLOCKS = 4
COMBINE_ROWS = 1024
VMEM_LIMIT = 56 * 1024 * 1024
SC_CORES = 2
SC_SUBCORES = 16
SC_DISPATCH_CHUNK = 64
SC_COMBINE_CHUNK = 64


def _dot(a, b):
    return jnp.dot(a, b, preferred_element_type=F32)


def _dot_nt(a, b):
    return lax.dot_general(a, b, (((1,), (1,)), ((), ())), preferred_element_type=F32)


def _dot_tn(a, b):
    return lax.dot_general(a, b, (((0,), (0,)), ((), ())), preferred_element_type=F32)


def _split_bf16(x):
    hi = x.astype(BF16)
    lo = (x - hi.astype(F32)).astype(BF16)
    return hi, lo


def _sigmoid(x):
    return 1.0 / (1.0 + jnp.exp(-x))


def _pack_bf16_pairs(x_bf16):
    bits = pltpu.bitcast(x_bf16.astype(F32), jnp.uint32)
    n = x_bf16.shape[1] // 2
    return (bits[:, :n] >> 16) | (bits[:, n:] & jnp.uint32(0xFFFF0000))


def _unpack_bf16_pairs(packed):
    return (pltpu.bitcast(packed << 16, F32),
            pltpu.bitcast(packed & jnp.uint32(0xFFFF0000), F32))


def _proj_kernel(x_ref, hcu_ref, hqkv_ref, nw_ref, wa_ref, wi_ref, wab_ref, cmw_ref,
                 cmn_ref, gmat_ref, qcw_ref, alog_ref, dtb_ref,
                 ya_ref, q_ref, k_ref, v_ref, z_ref, gb_ref, tcu_ref, tqkv_ref,
                 cu_s, qkv_s, xn_s):
    rows = x_ref.shape[0]
    sub = PROJ_SUB_ROWS if rows % PROJ_SUB_ROWS == 0 else rows
    blocks = [slice(r, r + sub) for r in range(0, rows, sub)]

    @pl.when(pl.program_id(1) == 0)
    def _():
        cu_s[0:HIST, :] = hcu_ref[...]
        qkv_s[0:HIST, :] = hqkv_ref[...]

    x = x_ref[...]
    ms = jnp.mean(x * x, axis=-1, keepdims=True)
    xn_s[...] = (x * lax.rsqrt(ms + EPS) * nw_ref[...]).astype(BF16)

    def causal_conv(buf, cur, w_ref, cols, tail_ref):
        taps = w_ref.shape[0]
        acc = pltpu.roll(cur, taps - 1, axis=0) * w_ref[0:1, cols]
        for j in range(1, taps - 1):
            acc = acc + pltpu.roll(cur, taps - 1 - j, axis=0) * w_ref[j:j + 1, cols]
        acc = acc + cur * w_ref[taps - 1:taps, cols]
        buf[HIST:2 * HIST, cols] = cur[:HIST]
        seam = buf[pl.ds(HIST - taps + 1, HIST), cols] * w_ref[0:1, cols]
        for j in range(1, taps):
            seam = seam + buf[pl.ds(HIST - taps + 1 + j, HIST), cols] * w_ref[j:j + 1, cols]
        tail = cur[sub - HIST:]
        buf[0:HIST, cols] = tail
        tail_ref[:, cols] = tail
        return jnp.concatenate([seam, acc[HIST:]], axis=0)

    def mixer_a_tail(i, rs, pa):
        cols = slice(i * PROJ_COLS, (i + 1) * PROJ_COLS)
        cu = pa[:, PROJ_COLS:2 * PROJ_COLS] * pa[:, 2 * PROJ_COLS:]
        ya = pa[:, :PROJ_COLS] * causal_conv(cu_s, cu, cmw_ref, cols, tcu_ref)
        sq_hi, sq_lo = _split_bf16(ya * ya)
        msg = _dot(sq_hi, gmat_ref[...]) + _dot(sq_lo, gmat_ref[...])
        ya_ref[rs, cols] = (ya * lax.rsqrt(msg + EPS) * cmn_ref[:, cols]).astype(BF16)

    heads_per_chunk = PROJ_COLS // HEAD_DIM

    def qkv_tail(i, rs, pq):
        cols = slice(i * PROJ_COLS, (i + 1) * PROJ_COLS)
        c = causal_conv(qkv_s, pq, qcw_ref, cols, tqkv_ref)
        c = c * _sigmoid(c)
        part, first_head = divmod(i * heads_per_chunk, HEADS)
        for j in range(heads_per_chunk):
            ch = c[:, j * HEAD_DIM:(j + 1) * HEAD_DIM]
            sl = slice((first_head + j) * HEAD_DIM, (first_head + j + 1) * HEAD_DIM)
            if part == 0:
                norm = lax.rsqrt(jnp.sum(ch * ch, axis=-1, keepdims=True) + EPS)
                q_ref[rs, sl] = ch * norm * (HEAD_DIM ** -0.5)
            elif part == 1:
                k_ref[rs, sl] = ch * lax.rsqrt(jnp.sum(ch * ch, axis=-1, keepdims=True) + EPS)
            else:
                v_ref[rs, sl] = ch

    def z_tail(rs, pz):
        z_ref[rs, :] = pz

    def decay_beta_tail(rs, ab):
        sp_in = ab + dtb_ref[...]
        softplus = jnp.maximum(sp_in, 0.0) + jnp.log1p(jnp.exp(-jnp.abs(sp_in)))
        g = -jnp.exp(alog_ref[...]) * softplus
        lane = lax.broadcasted_iota(jnp.int32, ab.shape, 1)
        gb_ref[rs, :] = jnp.where(lane < HEADS, g, jnp.where(lane < 2 * HEADS, _sigmoid(ab), 0.0))

    def matmul(w_ref, w_cols, rs):
        return _dot(xn_s[rs, :], w_ref[:, w_cols])

    stages = []
    for i in range(CONV_CH // PROJ_COLS):
        w_cols = slice(3 * i * PROJ_COLS, 3 * (i + 1) * PROJ_COLS)
        stages += [(functools.partial(matmul, wa_ref, w_cols, rs),
                    functools.partial(mixer_a_tail, i, rs)) for rs in blocks]
    for i in range(3 * GDN_W // PROJ_COLS):
        w_cols = slice(QKV_COL0 + i * PROJ_COLS, QKV_COL0 + (i + 1) * PROJ_COLS)
        stages += [(functools.partial(matmul, wi_ref, w_cols, rs),
                    functools.partial(qkv_tail, i, rs)) for rs in blocks]
    stages += [(functools.partial(matmul, wi_ref, slice(Z_COL0, Z_COL0 + GDN_W), rs),
                functools.partial(z_tail, rs)) for rs in blocks]
    stages += [(functools.partial(matmul, wab_ref, slice(None), rs),
                functools.partial(decay_beta_tail, rs)) for rs in blocks]
    pending = None
    for issue, tail in stages:
        res = issue()
        if pending is not None:
            pending()
        pending = functools.partial(tail, res)
    pending()


def _proj_call(x3, hist_cu, hist_qkv, w, rows):
    nb, seq, _ = x3.shape
    nt = seq // rows
    tok = lambda width: pl.BlockSpec((None, rows, width), lambda b, t: (b, t, 0))
    full = lambda a: pl.BlockSpec(a.shape, lambda b, t: (0,) * a.ndim,
                                  pipeline_mode=pl.Buffered(1))
    tail = lambda width: pl.BlockSpec((None, HIST, width), lambda b, t: (b, 0, 0))
    consts = (hist_cu, hist_qkv, w['mix_norm_w'], w['w_a'], w['w_in'], w['w_ab'],
              w['conv_mix_w'], w['conv_mix_norm_w'], w['gmat'], w['qkv_conv_w'], w['a_log'],
              w['dt_bias'])
    out_shape = (
        jax.ShapeDtypeStruct((nb, seq, CONV_CH), BF16),
        jax.ShapeDtypeStruct((nb, seq, GDN_W), F32),
        jax.ShapeDtypeStruct((nb, seq, GDN_W), F32),
        jax.ShapeDtypeStruct((nb, seq, GDN_W), F32),
        jax.ShapeDtypeStruct((nb, seq, GDN_W), F32),
        jax.ShapeDtypeStruct((nb, seq, LANES), F32),
        jax.ShapeDtypeStruct((nb, HIST, CONV_CH), F32),
        jax.ShapeDtypeStruct((nb, HIST, 3 * GDN_W), F32),
    )
    return pl.pallas_call(
        _proj_kernel,
        grid=(nb, nt),
        in_specs=[tok(D_MODEL)] + [full(a) for a in consts],
        out_specs=(tok(CONV_CH), tok(GDN_W), tok(GDN_W), tok(GDN_W), tok(GDN_W), tok(LANES),
                   tail(CONV_CH), tail(3 * GDN_W)),
        out_shape=out_shape,
        scratch_shapes=[pltpu.VMEM((2 * HIST, CONV_CH), F32),
                        pltpu.VMEM((2 * HIST, 3 * GDN_W), F32),
                        pltpu.VMEM((rows, D_MODEL), BF16)],
        compiler_params=pltpu.CompilerParams(
            dimension_semantics=("arbitrary", "arbitrary"), vmem_limit_bytes=VMEM_LIMIT),
        name="proj",
    )(x3, *consts)


def _chunk_masks():
    row = lax.broadcasted_iota(jnp.int32, (CHUNK, CHUNK), 0)
    col = lax.broadcasted_iota(jnp.int32, (CHUNK, CHUNK), 1)
    incl = row >= col
    strict = row > col
    levels = []
    n = 1
    while n < CHUNK:
        levels.append((row // (2 * n) == col // (2 * n)) & ((row // n) % 2 == 1) & ((col // n) % 2 == 0))
        n *= 2
    return incl, strict, levels


def _chunk_cumsum(gb_blk, incl):
    tri = incl.astype(BF16)
    hi, lo = _split_bf16(gb_blk)
    return _dot(tri, hi) + _dot(tri, lo)


def _chunk_transforms(chains, masks, state_only):
    incl, strict, levels = masks
    eye = (lax.broadcasted_iota(jnp.int32, (CHUNK, CHUNK), 0)
           == lax.broadcasted_iota(jnp.int32, (CHUNK, CHUNK), 1)).astype(F32)
    decay = [jnp.exp(jnp.where(incl, gc_col - gc_row, -jnp.inf))
             for (_, _, _, _, gc_col, gc_row, _) in chains]
    kb = [kh * beta for (_, kh, _, beta, _, _, _) in chains]
    k_bf = [kh.astype(BF16) for (_, kh, _, _, _, _, _) in chains]
    a_mat = [jnp.where(strict, _dot_nt(kb_i.astype(BF16), k_i) * d_i, 0.0)
             for kb_i, k_i, d_i in zip(kb, k_bf, decay)]
    t_inv = [eye - jnp.where(levels[0], a_i, 0.0) for a_i in a_mat]
    for lvl in levels[1:]:
        t_bf = [t_i.astype(BF16) for t_i in t_inv]
        m1 = [_dot(jnp.where(lvl, a_i, 0.0).astype(BF16), t_i) for a_i, t_i in zip(a_mat, t_bf)]
        t_inv = [t_i - _dot(tb_i, m_i.astype(BF16)) for t_i, tb_i, m_i in zip(t_inv, t_bf, m1)]
    rhs = [jnp.concatenate([vh * beta, kb_i * jnp.exp(gc_col)], axis=1)
           for (_, _, vh, beta, gc_col, _, _), kb_i in zip(chains, kb)]
    uw = [_dot(t_i.astype(BF16), r_i.astype(BF16)).astype(BF16)
          for t_i, r_i in zip(t_inv, rhs)]
    kd = [kh * jnp.exp(g_last - gc_col) for (_, kh, _, _, gc_col, _, g_last) in chains]
    pn = [_dot_tn(kd_i.astype(BF16), uw_i) for kd_i, uw_i in zip(kd, uw)]
    if state_only:
        return [pn_i[:, :HEAD_DIM] for pn_i in pn]
    intra = [jnp.where(incl, _dot_nt(qh.astype(BF16), k_i) * d_i, 0.0)
             for (qh, _, _, _, _, _, _), k_i, d_i in zip(chains, k_bf, decay)]
    iuw = [_dot(in_i.astype(BF16), uw_i) for in_i, uw_i in zip(intra, uw)]
    out = []
    for (qh, _, _, _, gc_col, _, g_last), pn_i, iuw_i in zip(chains, pn, iuw):
        q_part = qh * jnp.exp(gc_col) - iuw_i[:, HEAD_DIM:]
        out.append((q_part, pn_i[:, HEAD_DIM:], iuw_i[:, :HEAD_DIM], pn_i[:, :HEAD_DIM],
                    jnp.exp(g_last)))
    return out


def _gdn_kernel(pq_ref, pk_ref, pv_ref, pgb_ref, q_ref, k_ref, v_ref, gb_ref, o_ref,
                s_s, qp_s, op_s, n_s, a_s):
    nb = q_ref.shape[0]
    n_chunks = q_ref.shape[1] // CHUNK
    group = GDN_GROUP if n_chunks % GDN_GROUP == 0 else 1
    masks = _chunk_masks()

    def chains_of(gb_blk, q_blk, k_blk, v_blk):
        gc = _chunk_cumsum(gb_blk, masks[0])
        gc_t = gc.T
        res = []
        for h in range(HEADS):
            sl = slice(h * HEAD_DIM, (h + 1) * HEAD_DIM)
            res.append((q_blk(sl), k_blk(sl), v_blk(sl), gb_blk[:, HEADS + h:HEADS + h + 1],
                        gc[:, h:h + 1], gc_t[h:h + 1, :CHUNK], gc[CHUNK - 1:CHUNK, h:h + 1]))
        return res

    @pl.when(pl.program_id(0) == 0)
    def _():
        chains = chains_of(pgb_ref[...], lambda sl: pq_ref[:, sl], lambda sl: pk_ref[:, sl],
                           lambda sl: pv_ref[:, sl])
        for h, n_mat in enumerate(_chunk_transforms(chains, masks, True)):
            for b in range(nb):
                s_s[b * HEADS + h] = n_mat

    def transform_group(gi, carry):
        chains, where = [], []
        for cc in range(group):
            c = gi * group + cc
            rows = pl.ds(pl.multiple_of(c * CHUNK, CHUNK), CHUNK)
            for b in range(nb):
                chains += chains_of(gb_ref[b, rows, :], lambda sl: q_ref[b, rows, sl],
                                    lambda sl: k_ref[b, rows, sl], lambda sl: v_ref[b, rows, sl])
                where += [(c, b * HEADS + h) for h in range(HEADS)]
        for (c, ch), (q_part, p_mat, o_part, n_mat, a) in zip(
                where, _chunk_transforms(chains, masks, False)):
            qp_s[c, ch, 0:CHUNK, :] = q_part.astype(BF16)
            qp_s[c, ch, CHUNK:, :] = p_mat.astype(BF16)
            op_s[c, ch] = o_part
            n_s[c, ch] = n_mat
            a_s[c, ch] = jnp.broadcast_to(a, (8, HEAD_DIM))
        return carry

    lax.fori_loop(0, n_chunks // group, transform_group, 0)

    def scan_chunk(c, carry):
        r0 = pl.multiple_of(c * CHUNK, CHUNK)
        for b in range(nb):
            for h in range(HEADS):
                ch = b * HEADS + h
                s = s_s[ch]
                r = _dot(qp_s[c, ch], s.astype(BF16))
                o_ref[b, pl.ds(r0, CHUNK), h * HEAD_DIM:(h + 1) * HEAD_DIM] = r[:CHUNK] + op_s[c, ch]
                s_s[ch] = a_s[c, ch][0:1, :] * s - r[CHUNK:] + n_s[c, ch]
        return carry

    lax.fori_loop(0, n_chunks, scan_chunk, 0)


def _gdn_call(pq, pk, pv, pgb, q, k, v, gb, chunks_per_step):
    nb, seq, _ = q.shape
    rows = chunks_per_step * CHUNK
    steps = seq // rows
    tok = lambda width: pl.BlockSpec((nb, rows, width), lambda i: (0, i, 0))
    pre = lambda width: pl.BlockSpec((None, CHUNK, width), lambda i: (0, 0, 0))
    nch = nb * HEADS
    return pl.pallas_call(
        _gdn_kernel,
        grid=(steps,),
        in_specs=[pre(GDN_W), pre(GDN_W), pre(GDN_W), pre(LANES),
                  tok(GDN_W), tok(GDN_W), tok(GDN_W), tok(LANES)],
        out_specs=tok(GDN_W),
        out_shape=jax.ShapeDtypeStruct((nb, seq, GDN_W), F32),
        scratch_shapes=[
            pltpu.VMEM((nch, HEAD_DIM, HEAD_DIM), F32),
            pltpu.VMEM((chunks_per_step, nch, CHUNK + HEAD_DIM, HEAD_DIM), BF16),
            pltpu.VMEM((chunks_per_step, nch, CHUNK, HEAD_DIM), F32),
            pltpu.VMEM((chunks_per_step, nch, HEAD_DIM, HEAD_DIM), F32),
            pltpu.VMEM((chunks_per_step, nch, 8, HEAD_DIM), F32),
        ],
        compiler_params=pltpu.CompilerParams(
            dimension_semantics=("arbitrary",), vmem_limit_bytes=VMEM_LIMIT),
        name="gdn",
    )(pq, pk, pv, pgb, q, k, v, gb)


def _mix_out_kernel(x_ref, ya_ref, o_ref, z_ref, gnw_ref, woa_ref, wob_ref, fnw_ref, wr_ref,
                    br_ref, h_ref, xn_ref, route_ref, assign_ref, counts_ref, cnt_s):
    rows = x_ref.shape[0]
    sub = MIX_SUB_ROWS if rows % MIX_SUB_ROWS == 0 else rows
    blocks = [slice(r, r + sub) for r in range(0, rows, sub)]

    def gated_heads(rs):
        yb = []
        for h in range(HEADS):
            sl = slice(h * HEAD_DIM, (h + 1) * HEAD_DIM)
            oh = o_ref[rs, sl]
            zh = z_ref[rs, sl]
            on = oh * lax.rsqrt(jnp.mean(oh * oh, axis=-1, keepdims=True) + EPS) * gnw_ref[...]
            yb.append((on * (zh * _sigmoid(zh))).astype(BF16))
        return jnp.concatenate(yb, axis=1)

    yb = [gated_heads(rs) for rs in blocks]
    h1 = [x_ref[rs, :] + (_dot(ya_ref[rs, :], woa_ref[...]) + _dot(yb_i, wob_ref[...]))
          for rs, yb_i in zip(blocks, yb)]
    for rs, h1_i in zip(blocks, h1):
        h_ref[rs, :] = h1_i
    xn = [h1_i * lax.rsqrt(jnp.mean(h1_i * h1_i, axis=-1, keepdims=True) + EPS) * fnw_ref[...]
          for h1_i in h1]
    xn_bf = [xn_i.astype(BF16) for xn_i in xn]
    for rs, xn_i in zip(blocks, xn_bf):
        xn_ref[rs, :] = _pack_bf16_pairs(xn_i)
    logits = [_dot(xn_i, wr_ref[...]) + br_ref[...] for xn_i in xn_bf]

    lane = lax.broadcasted_iota(jnp.int32, (sub, LANES), 1).astype(F32)
    neg = -jnp.inf
    big = float(1 << 20)

    def argmax_first(vals):
        m = jnp.max(vals, axis=-1, keepdims=True)
        idx = jnp.min(jnp.where(vals == m, lane, big), axis=-1, keepdims=True)
        return m, idx

    def top_k(lg):
        grp = jnp.where(lane < N_GROUPS, lg, neg)
        g_max, g_sel = argmax_first(grp)
        p_grp = 1.0 / jnp.sum(jnp.exp(grp - g_max), axis=-1, keepdims=True)
        lo_lane = N_GROUPS + g_sel * EXPERTS_PER_GROUP
        ex = jnp.where((lane >= lo_lane) & (lane < lo_lane + EXPERTS_PER_GROUP), lg, neg)
        m1, i1 = argmax_first(ex)
        m2, i2 = argmax_first(jnp.where(lane == i1, neg, ex))
        e2 = jnp.exp(m2 - m1)
        return i1 - N_GROUPS, i2 - N_GROUPS, 1.0 / (1.0 + e2) * p_grp, e2 / (1.0 + e2) * p_grp

    picks = [top_k(lg) for lg in logits]

    @pl.when(pl.program_id(0) == 0)
    def _():
        cnt_s[...] = jnp.zeros(cnt_s.shape, F32)

    onehots = [((lane == e1).astype(F32), (lane == e2).astype(F32)) for e1, e2, _, _ in picks]
    earlier = (lax.broadcasted_iota(jnp.int32, (sub, sub), 0)
               > lax.broadcasted_iota(jnp.int32, (sub, sub), 1)).astype(BF16)
    within = [_dot(earlier, (oh1 + oh2).astype(BF16)) for oh1, oh2 in onehots]
    counts = cnt_s[...]
    for rs, (e1, e2, w1, w2), (oh1, oh2), within_i in zip(blocks, picks, onehots, within):
        before = within_i + counts
        rank1 = jnp.sum(before * oh1, axis=-1, keepdims=True)
        rank2 = jnp.sum(before * oh2, axis=-1, keepdims=True)
        counts = counts + jnp.sum(oh1 + oh2, axis=0, keepdims=True)
        route = jnp.zeros((sub, LANES), F32)
        for k, val in ((ROUTE_E1, e1), (ROUTE_E2, e2), (ROUTE_W1, w1), (ROUTE_W2, w2),
                       (ROUTE_RANK1, rank1), (ROUTE_RANK2, rank2)):
            route = jnp.where(lane == k, val, route)
        route_ref[rs, :] = route
        assign_ref[:, rs] = route.T[:ASSIGN_ROWS].astype(jnp.int32)
    cnt_s[...] = counts
    counts_ref[...] = counts


def _mix_out_call(x2, ya, o, z, w, rows):
    tokens = x2.shape[0]
    tok = lambda width: pl.BlockSpec((rows, width), lambda i: (i, 0))
    full = lambda a: pl.BlockSpec(a.shape, lambda i: (0,) * a.ndim)
    consts = (w['gdn_norm_w'], w['w_out_a'], w['w_out_b'], w['ffn_norm_w'], w['w_route'],
              w['b_route'])
    return pl.pallas_call(
        _mix_out_kernel,
        grid=(tokens // rows,),
        in_specs=[tok(D_MODEL), tok(CONV_CH), tok(GDN_W), tok(GDN_W)] + [full(a) for a in consts],
        out_specs=(tok(D_MODEL), tok(D_MODEL // 2), tok(LANES),
                   pl.BlockSpec((ASSIGN_ROWS, rows), lambda i: (0, i)),
                   pl.BlockSpec((1, LANES), lambda i: (0, 0))),
        out_shape=(jax.ShapeDtypeStruct((tokens, D_MODEL), F32),
                   jax.ShapeDtypeStruct((tokens, D_MODEL // 2), jnp.uint32),
                   jax.ShapeDtypeStruct((tokens, LANES), F32),
                   jax.ShapeDtypeStruct((ASSIGN_ROWS, tokens), jnp.int32),
                   jax.ShapeDtypeStruct((1, LANES), F32)),
        scratch_shapes=[pltpu.VMEM((1, LANES), F32)],
        compiler_params=pltpu.CompilerParams(
            dimension_semantics=("arbitrary",), vmem_limit_bytes=VMEM_LIMIT),
        name="mix_out",
    )(x2, ya, o, z, *consts)


def _sc_row_move(table, idx, out_rows, chunk, scatter, name):
    n = idx.shape[0]
    n_src, width = table.shape
    workers = SC_CORES * SC_SUBCORES
    per_w = n // workers
    assert n % workers == 0 and per_w % (2 * chunk) == 0
    assert n_src % per_w == 0 or not scatter
    pairs = per_w // (2 * chunk)
    mesh = plsc.VectorSubcoreMesh(core_axis_name="c", subcore_axis_name="s",
                                  num_cores=SC_CORES, num_subcores=SC_SUBCORES)

    def body(table_hbm, idx_hbm, out_hbm, idx_v, buf_a, buf_b, sem_ra, sem_rb, sem_wa, sem_wb):
        base = (lax.axis_index("s") * SC_CORES + lax.axis_index("c")) * per_w
        src_base = lax.rem(base, n_src)
        pltpu.sync_copy(idx_hbm.at[pl.ds(base, per_w)], idx_v)

        def read(c, buf, sem):
            off = pl.multiple_of(c * chunk, chunk)
            src = (table_hbm.at[pl.ds(src_base + off, chunk)] if scatter
                   else table_hbm.at[idx_v.at[pl.ds(off, chunk)]])
            return pltpu.make_async_copy(src, buf, sem)

        def write(c, buf, sem):
            off = pl.multiple_of(c * chunk, chunk)
            dst = (out_hbm.at[idx_v.at[pl.ds(off, chunk)]] if scatter
                   else out_hbm.at[pl.ds(base + off, chunk)])
            return pltpu.make_async_copy(buf, dst, sem)

        read(0, buf_a, sem_ra).start()

        @pl.loop(0, pairs)
        def _(j):
            ca = 2 * j
            cb = ca + 1
            read(cb, buf_b, sem_rb).start()
            read(ca, buf_a, sem_ra).wait()
            write(ca, buf_a, sem_wa).start()
            read(cb, buf_b, sem_rb).wait()
            write(cb, buf_b, sem_wb).start()
            write(ca, buf_a, sem_wa).wait()

            @pl.when(j + 1 < pairs)
            def _():
                read(ca + 2, buf_a, sem_ra).start()

            write(cb, buf_b, sem_wb).wait()

    return pl.kernel(
        body,
        out_type=jax.ShapeDtypeStruct((out_rows, width), table.dtype),
        mesh=mesh,
        scratch_types=[pltpu.VMEM((per_w,), jnp.int32),
                       pltpu.VMEM((chunk, width), table.dtype),
                       pltpu.VMEM((chunk, width), table.dtype),
                       pltpu.SemaphoreType.DMA, pltpu.SemaphoreType.DMA,
                       pltpu.SemaphoreType.DMA, pltpu.SemaphoreType.DMA],
        name=name,
    )(table, idx)


def _experts_kernel(bexp_ref, bvalid_ref, nused_ref, x_ref, wg_hbm, wu_hbm, wd_hbm, y_ref,
                    wg_s, wu_s, wd_s, stage_g, stage_u, stage_d, sems, seq_s):
    n_used = nused_ref[0]
    n_last = bexp_ref.shape[0] - 1
    rows = x_ref.shape[0] // EXPERT_STEP_BLOCKS
    sub = EXPERT_SUB_ROWS if rows % EXPERT_SUB_ROWS == 0 else rows

    def weight_copies(e, slot):
        return (pltpu.make_async_copy(wg_hbm.at[e], stage_g.at[slot], sems.at[slot, 0]),
                pltpu.make_async_copy(wu_hbm.at[e], stage_u.at[slot], sems.at[slot, 1]),
                pltpu.make_async_copy(wd_hbm.at[e], stage_d.at[slot], sems.at[slot, 2]))

    def block(i, r0):
        expert = bexp_ref[i]
        first_of_expert = (i == 0) | (expert != bexp_ref[jnp.maximum(i - 1, 0)])

        @pl.when((i == 0) & (n_used > 0))
        def _():
            seq_s[0] = 0
            for copy in weight_copies(expert, 0):
                copy.start()

        @pl.when((i < n_used) & first_of_expert)
        def _():
            @pl.when(i > 0)
            def _():
                seq_s[0] = seq_s[0] + 1

            slot = seq_s[0] % 2
            nxt = lax.while_loop(
                lambda j: (j < n_used) & (bexp_ref[jnp.minimum(j, n_last)] == expert),
                lambda j: j + 1, i + 1)

            @pl.when(nxt < n_used)
            def _():
                for copy in weight_copies(bexp_ref[jnp.minimum(nxt, n_last)], 1 - slot):
                    copy.start()

            for copy in weight_copies(expert, slot):
                copy.wait()
            wg_s[...] = stage_g[slot].astype(BF16)
            wu_s[...] = stage_u[slot].astype(BF16)
            wd_s[...] = stage_d[slot].astype(BF16)

        @pl.when(i < n_used)
        def _():
            blocks = [slice(r, r + sub) for r in range(r0, r0 + rows, sub)]
            row = lax.broadcasted_iota(jnp.int32, (sub, x_ref.shape[1]), 0)
            xb = [jnp.concatenate(_unpack_bf16_pairs(
                      jnp.where(row + (rs.start - r0) < bvalid_ref[i], x_ref[rs, :], jnp.uint32(0))),
                      axis=1).astype(BF16) for rs in blocks]
            gate = [_dot(xb_i, wg_s[...]) for xb_i in xb]
            up = [_dot(xb_i, wu_s[...]) for xb_i in xb]
            hid = [((g_i * _sigmoid(g_i)) * u_i).astype(BF16) for g_i, u_i in zip(gate, up)]
            y = [_dot(h_i, wd_s[...]) for h_i in hid]
            for rs, y_i in zip(blocks, y):
                y_ref[rs, :] = _pack_bf16_pairs(y_i.astype(BF16))

        @pl.when(i >= n_used)
        def _():
            y_ref[r0:r0 + rows, :] = jnp.zeros((rows, y_ref.shape[1]), jnp.uint32)

    for j in range(EXPERT_STEP_BLOCKS):
        block(pl.program_id(0) * EXPERT_STEP_BLOCKS + j, j * rows)


def _experts_call(block_expert, block_valid, n_used, x_rows, w_gate, w_up, w_down, rows):
    n_blocks = block_expert.shape[0]
    step_rows = EXPERT_STEP_BLOCKS * rows
    hbm = pl.BlockSpec(memory_space=pl.ANY)
    grid_spec = pltpu.PrefetchScalarGridSpec(
        num_scalar_prefetch=3,
        grid=(n_blocks // EXPERT_STEP_BLOCKS,),
        in_specs=[pl.BlockSpec((step_rows, D_MODEL // 2), lambda i, be, bv, nu: (i, 0)),
                  hbm, hbm, hbm],
        out_specs=pl.BlockSpec((step_rows, D_MODEL // 2), lambda i, be, bv, nu: (i, 0)),
        scratch_shapes=[pltpu.VMEM((D_MODEL, D_EXPERT), BF16),
                        pltpu.VMEM((D_MODEL, D_EXPERT), BF16),
                        pltpu.VMEM((D_EXPERT, D_MODEL), BF16),
                        pltpu.VMEM((2, D_MODEL, D_EXPERT), F32),
                        pltpu.VMEM((2, D_MODEL, D_EXPERT), F32),
                        pltpu.VMEM((2, D_EXPERT, D_MODEL), F32),
                        pltpu.SemaphoreType.DMA((2, 3)),
                        pltpu.SMEM((1,), jnp.int32)],
    )
    return pl.pallas_call(
        _experts_kernel,
        grid_spec=grid_spec,
        out_shape=jax.ShapeDtypeStruct((n_blocks * rows, D_MODEL // 2), jnp.uint32),
        compiler_params=pltpu.CompilerParams(
            dimension_semantics=("arbitrary",), vmem_limit_bytes=VMEM_LIMIT),
        name="experts",
    )(block_expert, block_valid, n_used, x_rows, w_gate, w_up, w_down)


def _combine_kernel(y1_ref, y2_ref, h_ref, route_ref, fw_ref, out_ref):
    route = route_ref[...]
    y1 = jnp.concatenate(_unpack_bf16_pairs(y1_ref[...]), axis=1)
    y2 = jnp.concatenate(_unpack_bf16_pairs(y2_ref[...]), axis=1)
    moe = route[:, ROUTE_W1:ROUTE_W1 + 1] * y1 + route[:, ROUTE_W2:ROUTE_W2 + 1] * y2
    h2 = h_ref[...] + moe
    out_ref[...] = h2 * lax.rsqrt(jnp.mean(h2 * h2, axis=-1, keepdims=True) + EPS) * fw_ref[...]


def _combine_call(y_tok, h1, route, final_w, rows):
    tokens = h1.shape[0]
    steps = tokens // rows
    tok = lambda width: pl.BlockSpec((rows, width), lambda i: (i, 0))
    return pl.pallas_call(
        _combine_kernel,
        grid=(steps,),
        in_specs=[tok(D_MODEL // 2), pl.BlockSpec((rows, D_MODEL // 2), lambda i: (i + steps, 0)),
                  tok(D_MODEL), tok(LANES), pl.BlockSpec((1, D_MODEL), lambda i: (0, 0))],
        out_specs=tok(D_MODEL),
        out_shape=jax.ShapeDtypeStruct((tokens, D_MODEL), F32),
        compiler_params=pltpu.CompilerParams(
            dimension_semantics=("arbitrary",), vmem_limit_bytes=VMEM_LIMIT),
        name="combine",
    )(y_tok, y_tok, h1, route, final_w)


def _dispatch_tables(assign, counts, tokens, rows):
    n_blocks = pl.cdiv((tokens * TOP_K + N_EXPERTS * (rows - 1)) // rows,
                       EXPERT_STEP_BLOCKS) * EXPERT_STEP_BLOCKS
    counts = counts[0, :N_EXPERTS].astype(jnp.int32)
    padded = (counts + rows - 1) // rows * rows
    pad_end = jnp.cumsum(padded)
    pad_start = pad_end - padded
    experts = assign[ROUTE_E1:ROUTE_E2 + 1]
    ranks = assign[ROUTE_RANK1:ROUTE_RANK2 + 1]
    is_expert = experts[..., None] == jnp.arange(N_EXPERTS, dtype=jnp.int32)
    dest = (jnp.sum(jnp.where(is_expert, pad_start, 0), axis=-1) + ranks).reshape(-1)
    block_start = jnp.arange(n_blocks, dtype=jnp.int32) * rows
    in_run = (block_start[:, None] >= pad_start[None, :]) & (block_start[:, None] < pad_end[None, :])
    lookup = lambda table: jnp.sum(jnp.where(in_run, table[None, :], 0), axis=1)
    block_expert = lookup(jnp.arange(N_EXPERTS, dtype=jnp.int32))
    block_valid = jnp.clip(lookup(pad_start + counts) - block_start, 0, rows)
    n_used = (pad_end[-1] // rows).astype(jnp.int32).reshape(1)
    return dest, block_expert, block_valid.astype(jnp.int32), n_used, n_blocks


def _prepare_weights(mix_norm_w, w_in, conv_mix_w, conv_mix_norm_w, qkv_conv_w, a_log, dt_bias,
                     gdn_norm_w, w_out, ffn_norm_w, w_group, b_group, w_router, b_router):
    pad_lanes = lambda v: jnp.pad(v.reshape(1, -1), ((0, 0), (0, LANES - v.size)))
    w_in_bf = w_in.astype(BF16)
    w_ab = jnp.pad(w_in_bf[:, Z_COL0 + GDN_W:], ((0, 0), (0, LANES - 2 * HEADS)))
    grp = jnp.arange(PROJ_COLS) // CONV_GROUP_W
    gmat = jnp.where(grp[:, None] == grp[None, :], 1.0 / CONV_GROUP_W, 0.0).astype(BF16)
    w_route = jnp.concatenate([w_group, w_router.reshape(D_MODEL, N_EXPERTS)], axis=1)
    w_route = jnp.pad(w_route, ((0, 0), (0, LANES - w_route.shape[1])))
    return dict(
        mix_norm_w=mix_norm_w.reshape(1, -1),
        w_a=jnp.concatenate([w_in_bf[:, part * CONV_CH + i * PROJ_COLS:][:, :PROJ_COLS]
                             for i in range(CONV_CH // PROJ_COLS) for part in range(3)], axis=1),
        w_in=w_in_bf,
        w_ab=w_ab,
        conv_mix_w=conv_mix_w,
        conv_mix_norm_w=conv_mix_norm_w.reshape(1, -1),
        gmat=gmat,
        qkv_conv_w=qkv_conv_w,
        a_log=pad_lanes(a_log),
        dt_bias=pad_lanes(dt_bias),
        gdn_norm_w=gdn_norm_w.reshape(1, -1),
        w_out_a=w_out[:CONV_CH].astype(BF16),
        w_out_b=w_out[CONV_CH:].astype(BF16),
        ffn_norm_w=ffn_norm_w.reshape(1, -1),
        w_route=w_route.astype(BF16),
        b_route=pad_lanes(jnp.concatenate([b_group, b_router.reshape(-1)])),
    )


def _tile(n, preferred):
    return preferred if n % preferred == 0 else n


def kernel(x, meta_tokens, mix_norm_w, w_in, conv_mix_w, conv_mix_norm_w, qkv_conv_w, a_log,
           dt_bias, gdn_norm_w, w_out, ffn_norm_w, w_group, b_group, w_router, b_router, w_gate,
           w_up, w_down, final_norm_w):
    assert mix_norm_w.shape[0] == 1, "single-layer kernel"
    batch, seq, _ = x.shape
    assert seq % CHUNK == 0
    w = _prepare_weights(mix_norm_w[0], w_in[0], conv_mix_w[0], conv_mix_norm_w[0], qkv_conv_w[0],
                         a_log[0], dt_bias[0], gdn_norm_w[0], w_out[0], ffn_norm_w[0], w_group[0],
                         b_group[0], w_router[0], b_router[0])

    prefix = jnp.concatenate([jnp.zeros((CHUNK - N_META, D_MODEL), x.dtype),
                              meta_tokens.astype(x.dtype)], axis=0)[None]
    zero_cu = jnp.zeros((HIST, CONV_CH), F32)
    zero_qkv = jnp.zeros((HIST, 3 * GDN_W), F32)
    _, pq, pk, pv, _, pgb, tail_cu, tail_qkv = _proj_call(prefix, zero_cu, zero_qkv, w, CHUNK)

    ya, q, k, v, z, gb, _, _ = _proj_call(x, tail_cu[0], tail_qkv[0], w, _tile(seq, PROJ_ROWS))
    o = _gdn_call(pq, pk, pv, pgb, q, k, v, gb, _tile(seq // CHUNK, GDN_CHUNKS))

    tokens = batch * seq
    flat = lambda a: a.reshape(tokens, a.shape[-1])
    h1, xn2, route, assign, counts = _mix_out_call(flat(x), flat(ya), flat(o), flat(z), w,
                                           _tile(tokens, MIX_ROWS))

    dest, block_expert, block_valid, n_used, n_blocks = _dispatch_tables(assign, counts, tokens,
                                                                         EXPERT_ROWS)
    x_rows = _sc_row_move(xn2, dest, n_blocks * EXPERT_ROWS, SC_DISPATCH_CHUNK, True,
                          "dispatch_scatter")
    y_rows = _experts_call(block_expert, block_valid, n_used, x_rows, w_gate[0], w_up[0],
                           w_down[0], EXPERT_ROWS)
    y_tok = _sc_row_move(y_rows, dest, TOP_K * tokens, SC_COMBINE_CHUNK, False, "combine_gather")
    out = _combine_call(y_tok, h1, route, final_norm_w.reshape(1, -1), _tile(tokens, COMBINE_ROWS))
    return out.reshape(batch, seq, D_MODEL)
```

```python
import functools

import jax
import jax.numpy as jnp
from jax import lax
from jax.experimental import pallas as pl
from jax.experimental.pallas import tpu as pltpu
from jax.experimental.pallas import tpu_sc as plsc

F32 = jnp.float32
BF16 = jnp.bfloat16
EPS = 1e-6

D_MODEL = 1024
N_META = 16
CONV_CH = 512
CONV_GROUP_W = 64
HEADS = 4
HEAD_DIM = 128
GDN_W = HEADS * HEAD_DIM
QKV_COL0 = 3 * CONV_CH
Z_COL0 = QKV_COL0 + 3 * GDN_W
CHUNK = 64
N_GROUPS = 4
EXPERTS_PER_GROUP = 8
N_EXPERTS = N_GROUPS * EXPERTS_PER_GROUP
TOP_K = 2
D_EXPERT = 512
LANES = 128
ROUTE_E1, ROUTE_E2, ROUTE_RANK1, ROUTE_RANK2, ROUTE_W1, ROUTE_W2 = range(6)
ASSIGN_ROWS = 8
HIST = 8

PROJ_ROWS = 512
PROJ_COLS = 256
PROJ_SUB_ROWS = 128
GDN_CHUNKS = 8
GDN_GROUP = 4
MIX_ROWS = 1024
MIX_SUB_ROWS = 128
EXPERT_ROWS = 256
EXPERT_SUB_ROWS = 128
EXPERT_STEP_BLOCKS = 8
COMBINE_ROWS = 1024
VMEM_LIMIT = 56 * 1024 * 1024
SC_CORES = 2
SC_SUBCORES = 16
SC_DISPATCH_CHUNK = 64
SC_COMBINE_CHUNK = 64


def _dot(a, b):
    return jnp.dot(a, b, preferred_element_type=F32)


def _dot_nt(a, b):
    return lax.dot_general(a, b, (((1,), (1,)), ((), ())), preferred_element_type=F32)


def _dot_tn(a, b):
    return lax.dot_general(a, b, (((0,), (0,)), ((), ())), preferred_element_type=F32)


def _split_bf16(x):
    hi = x.astype(BF16)
    lo = (x - hi.astype(F32)).astype(BF16)
    return hi, lo


def _sigmoid(x):
    return 1.0 / (1.0 + jnp.exp(-x))


def _pack_bf16_pairs(x_bf16):
    bits = pltpu.bitcast(x_bf16.astype(F32), jnp.uint32)
    n = x_bf16.shape[1] // 2
    return (bits[:, :n] >> 16) | (bits[:, n:] & jnp.uint32(0xFFFF0000))


def _unpack_bf16_pairs(packed):
    return (pltpu.bitcast(packed << 16, F32),
            pltpu.bitcast(packed & jnp.uint32(0xFFFF0000), F32))


def _proj_kernel(x_ref, hcu_ref, hqkv_ref, nw_ref, wa_ref, wi_ref, wab_ref, cmw_ref,
                 cmn_ref, gmat_ref, qcw_ref, alog_ref, dtb_ref,
                 ya_ref, q_ref, k_ref, v_ref, z_ref, gb_ref, tcu_ref, tqkv_ref,
                 cu_s, qkv_s, xn_s):
    rows = x_ref.shape[0]
    sub = PROJ_SUB_ROWS if rows % PROJ_SUB_ROWS == 0 else rows
    blocks = [slice(r, r + sub) for r in range(0, rows, sub)]

    @pl.when(pl.program_id(1) == 0)
    def _():
        cu_s[0:HIST, :] = hcu_ref[...]
        qkv_s[0:HIST, :] = hqkv_ref[...]

    x = x_ref[...]
    ms = jnp.mean(x * x, axis=-1, keepdims=True)
    xn_s[...] = (x * lax.rsqrt(ms + EPS) * nw_ref[...]).astype(BF16)

    def causal_conv(buf, cur, w_ref, cols, tail_ref):
        taps = w_ref.shape[0]
        acc = pltpu.roll(cur, taps - 1, axis=0) * w_ref[0:1, cols]
        for j in range(1, taps - 1):
            acc = acc + pltpu.roll(cur, taps - 1 - j, axis=0) * w_ref[j:j + 1, cols]
        acc = acc + cur * w_ref[taps - 1:taps, cols]
        buf[HIST:2 * HIST, cols] = cur[:HIST]
        seam = buf[pl.ds(HIST - taps + 1, HIST), cols] * w_ref[0:1, cols]
        for j in range(1, taps):
            seam = seam + buf[pl.ds(HIST - taps + 1 + j, HIST), cols] * w_ref[j:j + 1, cols]
        tail = cur[sub - HIST:]
        buf[0:HIST, cols] = tail
        tail_ref[:, cols] = tail
        return jnp.concatenate([seam, acc[HIST:]], axis=0)

    def mixer_a_tail(i, rs, pa):
        cols = slice(i * PROJ_COLS, (i + 1) * PROJ_COLS)
        cu = pa[:, PROJ_COLS:2 * PROJ_COLS] * pa[:, 2 * PROJ_COLS:]
        ya = pa[:, :PROJ_COLS] * causal_conv(cu_s, cu, cmw_ref, cols, tcu_ref)
        msg = _dot((ya * ya).astype(BF16), gmat_ref[...])
        ya_ref[rs, cols] = (ya * lax.rsqrt(msg + EPS) * cmn_ref[:, cols]).astype(BF16)

    heads_per_chunk = PROJ_COLS // HEAD_DIM

    def qkv_tail(i, rs, pq):
        cols = slice(i * PROJ_COLS, (i + 1) * PROJ_COLS)
        c = causal_conv(qkv_s, pq, qcw_ref, cols, tqkv_ref)
        c = c * _sigmoid(c)
        part, first_head = divmod(i * heads_per_chunk, HEADS)
        for j in range(heads_per_chunk):
            ch = c[:, j * HEAD_DIM:(j + 1) * HEAD_DIM]
            sl = slice((first_head + j) * HEAD_DIM, (first_head + j + 1) * HEAD_DIM)
            if part == 0:
                norm = lax.rsqrt(jnp.sum(ch * ch, axis=-1, keepdims=True) + EPS)
                q_ref[rs, sl] = ch * norm * (HEAD_DIM ** -0.5)
            elif part == 1:
                k_ref[rs, sl] = ch * lax.rsqrt(jnp.sum(ch * ch, axis=-1, keepdims=True) + EPS)
            else:
                v_ref[rs, sl] = ch

    def z_tail(rs, pz):
        z_ref[rs, :] = pz

    def decay_beta_tail(rs, ab):
        sp_in = ab + dtb_ref[...]
        softplus = jnp.maximum(sp_in, 0.0) + jnp.log1p(jnp.exp(-jnp.abs(sp_in)))
        g = -jnp.exp(alog_ref[...]) * softplus
        lane = lax.broadcasted_iota(jnp.int32, ab.shape, 1)
        gb_ref[rs, :] = jnp.where(lane < HEADS, g, jnp.where(lane < 2 * HEADS, _sigmoid(ab), 0.0))

    def matmul(w_ref, w_cols, rs):
        return _dot(xn_s[rs, :], w_ref[:, w_cols])

    stages = []
    for i in range(CONV_CH // PROJ_COLS):
        w_cols = slice(3 * i * PROJ_COLS, 3 * (i + 1) * PROJ_COLS)
        stages += [(functools.partial(matmul, wa_ref, w_cols, rs),
                    functools.partial(mixer_a_tail, i, rs)) for rs in blocks]
    for i in range(3 * GDN_W // PROJ_COLS):
        w_cols = slice(QKV_COL0 + i * PROJ_COLS, QKV_COL0 + (i + 1) * PROJ_COLS)
        stages += [(functools.partial(matmul, wi_ref, w_cols, rs),
                    functools.partial(qkv_tail, i, rs)) for rs in blocks]
    stages += [(functools.partial(matmul, wi_ref, slice(Z_COL0, Z_COL0 + GDN_W), rs),
                functools.partial(z_tail, rs)) for rs in blocks]
    stages += [(functools.partial(matmul, wab_ref, slice(None), rs),
                functools.partial(decay_beta_tail, rs)) for rs in blocks]
    pending = None
    for issue, tail in stages:
        res = issue()
        if pending is not None:
            pending()
        pending = functools.partial(tail, res)
    pending()


def _proj_call(x3, hist_cu, hist_qkv, w, rows):
    nb, seq, _ = x3.shape
    nt = seq // rows
    tok = lambda width: pl.BlockSpec((None, rows, width), lambda b, t: (b, t, 0))
    full = lambda a: pl.BlockSpec(a.shape, lambda b, t: (0,) * a.ndim,
                                  pipeline_mode=pl.Buffered(1))
    tail = lambda width: pl.BlockSpec((None, HIST, width), lambda b, t: (b, 0, 0))
    consts = (hist_cu, hist_qkv, w['mix_norm_w'], w['w_a'], w['w_in'], w['w_ab'],
              w['conv_mix_w'], w['conv_mix_norm_w'], w['gmat'], w['qkv_conv_w'], w['a_log'],
              w['dt_bias'])
    out_shape = (
        jax.ShapeDtypeStruct((nb, seq, CONV_CH), BF16),
        jax.ShapeDtypeStruct((nb, seq, GDN_W), F32),
        jax.ShapeDtypeStruct((nb, seq, GDN_W), F32),
        jax.ShapeDtypeStruct((nb, seq, GDN_W), F32),
        jax.ShapeDtypeStruct((nb, seq, GDN_W), F32),
        jax.ShapeDtypeStruct((nb, seq, LANES), F32),
        jax.ShapeDtypeStruct((nb, HIST, CONV_CH), F32),
        jax.ShapeDtypeStruct((nb, HIST, 3 * GDN_W), F32),
    )
    return pl.pallas_call(
        _proj_kernel,
        grid=(nb, nt),
        in_specs=[tok(D_MODEL)] + [full(a) for a in consts],
        out_specs=(tok(CONV_CH), tok(GDN_W), tok(GDN_W), tok(GDN_W), tok(GDN_W), tok(LANES),
                   tail(CONV_CH), tail(3 * GDN_W)),
        out_shape=out_shape,
        scratch_shapes=[pltpu.VMEM((2 * HIST, CONV_CH), F32),
                        pltpu.VMEM((2 * HIST, 3 * GDN_W), F32),
                        pltpu.VMEM((rows, D_MODEL), BF16)],
        compiler_params=pltpu.CompilerParams(
            dimension_semantics=("arbitrary", "arbitrary"), vmem_limit_bytes=VMEM_LIMIT),
        name="proj",
    )(x3, *consts)


def _chunk_masks():
    row = lax.broadcasted_iota(jnp.int32, (CHUNK, CHUNK), 0)
    col = lax.broadcasted_iota(jnp.int32, (CHUNK, CHUNK), 1)
    incl = row >= col
    strict = row > col
    levels = []
    n = 1
    while n < CHUNK:
        levels.append((row // (2 * n) == col // (2 * n)) & ((row // n) % 2 == 1) & ((col // n) % 2 == 0))
        n *= 2
    return incl, strict, levels


def _chunk_cumsum(gb_blk, incl):
    tri = incl.astype(BF16)
    hi, lo = _split_bf16(gb_blk)
    return _dot(tri, hi) + _dot(tri, lo)


def _chunk_transforms(chains, masks, state_only):
    incl, strict, levels = masks
    eye = (lax.broadcasted_iota(jnp.int32, (CHUNK, CHUNK), 0)
           == lax.broadcasted_iota(jnp.int32, (CHUNK, CHUNK), 1)).astype(F32)
    decay = [jnp.exp(jnp.where(incl, gc_col - gc_row, -jnp.inf))
             for (_, _, _, _, gc_col, gc_row, _) in chains]
    kb = [kh * beta for (_, kh, _, beta, _, _, _) in chains]
    k_bf = [kh.astype(BF16) for (_, kh, _, _, _, _, _) in chains]
    a_mat = [jnp.where(strict, _dot_nt(kb_i.astype(BF16), k_i) * d_i, 0.0)
             for kb_i, k_i, d_i in zip(kb, k_bf, decay)]
    t_inv = [eye - jnp.where(levels[0], a_i, 0.0) for a_i in a_mat]
    for lvl in levels[1:]:
        t_bf = [t_i.astype(BF16) for t_i in t_inv]
        m1 = [_dot(jnp.where(lvl, a_i, 0.0).astype(BF16), t_i) for a_i, t_i in zip(a_mat, t_bf)]
        t_inv = [t_i - _dot(tb_i, m_i.astype(BF16)) for t_i, tb_i, m_i in zip(t_inv, t_bf, m1)]
    rhs = [jnp.concatenate([vh * beta, kb_i * jnp.exp(gc_col)], axis=1)
           for (_, _, vh, beta, gc_col, _, _), kb_i in zip(chains, kb)]
    uw = [_dot(t_i.astype(BF16), r_i.astype(BF16)).astype(BF16)
          for t_i, r_i in zip(t_inv, rhs)]
    kd = [kh * jnp.exp(g_last - gc_col) for (_, kh, _, _, gc_col, _, g_last) in chains]
    pn = [_dot_tn(kd_i.astype(BF16), uw_i) for kd_i, uw_i in zip(kd, uw)]
    if state_only:
        return [pn_i[:, :HEAD_DIM] for pn_i in pn]
    intra = [jnp.where(incl, _dot_nt(qh.astype(BF16), k_i) * d_i, 0.0)
             for (qh, _, _, _, _, _, _), k_i, d_i in zip(chains, k_bf, decay)]
    iuw = [_dot(in_i.astype(BF16), uw_i) for in_i, uw_i in zip(intra, uw)]
    out = []
    for (qh, _, _, _, gc_col, _, g_last), pn_i, iuw_i in zip(chains, pn, iuw):
        q_part = qh * jnp.exp(gc_col) - iuw_i[:, HEAD_DIM:]
        out.append((q_part, pn_i[:, HEAD_DIM:], iuw_i[:, :HEAD_DIM], pn_i[:, :HEAD_DIM],
                    jnp.exp(g_last)))
    return out


def _gdn_kernel(pq_ref, pk_ref, pv_ref, pgb_ref, q_ref, k_ref, v_ref, gb_ref, o_ref,
                s_s, qp_s, op_s, n_s, a_s):
    nb = q_ref.shape[0]
    n_chunks = q_ref.shape[1] // CHUNK
    group = GDN_GROUP if n_chunks % GDN_GROUP == 0 else 1
    masks = _chunk_masks()

    def chains_of(gb_blk, q_blk, k_blk, v_blk):
        gc = _chunk_cumsum(gb_blk, masks[0])
        gc_t = gc.T
        res = []
        for h in range(HEADS):
            sl = slice(h * HEAD_DIM, (h + 1) * HEAD_DIM)
            res.append((q_blk(sl), k_blk(sl), v_blk(sl), gb_blk[:, HEADS + h:HEADS + h + 1],
                        gc[:, h:h + 1], gc_t[h:h + 1, :CHUNK], gc[CHUNK - 1:CHUNK, h:h + 1]))
        return res

    @pl.when(pl.program_id(0) == 0)
    def _():
        chains = chains_of(pgb_ref[...], lambda sl: pq_ref[:, sl], lambda sl: pk_ref[:, sl],
                           lambda sl: pv_ref[:, sl])
        for h, n_mat in enumerate(_chunk_transforms(chains, masks, True)):
            for b in range(nb):
                s_s[b * HEADS + h] = n_mat

    def transform_group(gi, carry):
        chains, where = [], []
        for cc in range(group):
            c = gi * group + cc
            rows = pl.ds(pl.multiple_of(c * CHUNK, CHUNK), CHUNK)
            for b in range(nb):
                chains += chains_of(gb_ref[b, rows, :], lambda sl: q_ref[b, rows, sl],
                                    lambda sl: k_ref[b, rows, sl], lambda sl: v_ref[b, rows, sl])
                where += [(c, b * HEADS + h) for h in range(HEADS)]
        for (c, ch), (q_part, p_mat, o_part, n_mat, a) in zip(
                where, _chunk_transforms(chains, masks, False)):
            qp_s[c, ch, 0:CHUNK, :] = q_part.astype(BF16)
            qp_s[c, ch, CHUNK:, :] = p_mat.astype(BF16)
            op_s[c, ch] = o_part
            n_s[c, ch] = n_mat
            a_s[c, ch] = jnp.broadcast_to(a, (8, HEAD_DIM))
        return carry

    lax.fori_loop(0, n_chunks // group, transform_group, 0)

    def scan_chunk(c, carry):
        r0 = pl.multiple_of(c * CHUNK, CHUNK)
        for b in range(nb):
            for h in range(HEADS):
                ch = b * HEADS + h
                s = s_s[ch]
                r = _dot(qp_s[c, ch], s.astype(BF16))
                o_ref[b, pl.ds(r0, CHUNK), h * HEAD_DIM:(h + 1) * HEAD_DIM] = r[:CHUNK] + op_s[c, ch]
                s_s[ch] = a_s[c, ch][0:1, :] * s - r[CHUNK:] + n_s[c, ch]
        return carry

    lax.fori_loop(0, n_chunks, scan_chunk, 0)


def _gdn_call(pq, pk, pv, pgb, q, k, v, gb, chunks_per_step):
    nb, seq, _ = q.shape
    rows = chunks_per_step * CHUNK
    steps = seq // rows
    tok = lambda width: pl.BlockSpec((nb, rows, width), lambda i: (0, i, 0))
    pre = lambda width: pl.BlockSpec((None, CHUNK, width), lambda i: (0, 0, 0))
    nch = nb * HEADS
    return pl.pallas_call(
        _gdn_kernel,
        grid=(steps,),
        in_specs=[pre(GDN_W), pre(GDN_W), pre(GDN_W), pre(LANES),
                  tok(GDN_W), tok(GDN_W), tok(GDN_W), tok(LANES)],
        out_specs=tok(GDN_W),
        out_shape=jax.ShapeDtypeStruct((nb, seq, GDN_W), F32),
        scratch_shapes=[
            pltpu.VMEM((nch, HEAD_DIM, HEAD_DIM), F32),
            pltpu.VMEM((chunks_per_step, nch, CHUNK + HEAD_DIM, HEAD_DIM), BF16),
            pltpu.VMEM((chunks_per_step, nch, CHUNK, HEAD_DIM), F32),
            pltpu.VMEM((chunks_per_step, nch, HEAD_DIM, HEAD_DIM), F32),
            pltpu.VMEM((chunks_per_step, nch, 8, HEAD_DIM), F32),
        ],
        compiler_params=pltpu.CompilerParams(
            dimension_semantics=("arbitrary",), vmem_limit_bytes=VMEM_LIMIT),
        name="gdn",
    )(pq, pk, pv, pgb, q, k, v, gb)


def _mix_out_kernel(x_ref, ya_ref, o_ref, z_ref, gnw_ref, woa_ref, wob_ref, fnw_ref, wr_ref,
                    br_ref, h_ref, xn_ref, route_ref, assign_ref, counts_ref, cnt_s):
    rows = x_ref.shape[0]
    sub = MIX_SUB_ROWS if rows % MIX_SUB_ROWS == 0 else rows
    blocks = [slice(r, r + sub) for r in range(0, rows, sub)]

    def gated_heads(rs):
        yb = []
        for h in range(HEADS):
            sl = slice(h * HEAD_DIM, (h + 1) * HEAD_DIM)
            oh = o_ref[rs, sl]
            zh = z_ref[rs, sl]
            on = oh * lax.rsqrt(jnp.mean(oh * oh, axis=-1, keepdims=True) + EPS) * gnw_ref[...]
            yb.append((on * (zh * _sigmoid(zh))).astype(BF16))
        return jnp.concatenate(yb, axis=1)

    yb = [gated_heads(rs) for rs in blocks]
    h1 = [x_ref[rs, :] + (_dot(ya_ref[rs, :], woa_ref[...]) + _dot(yb_i, wob_ref[...]))
          for rs, yb_i in zip(blocks, yb)]
    for rs, h1_i in zip(blocks, h1):
        h_ref[rs, :] = h1_i
    xn = [h1_i * lax.rsqrt(jnp.mean(h1_i * h1_i, axis=-1, keepdims=True) + EPS) * fnw_ref[...]
          for h1_i in h1]
    xn_bf = [xn_i.astype(BF16) for xn_i in xn]
    for rs, xn_i in zip(blocks, xn_bf):
        xn_ref[rs, :] = _pack_bf16_pairs(xn_i)
    logits = [_dot(xn_i, wr_ref[...]) + br_ref[...] for xn_i in xn_bf]

    lane = lax.broadcasted_iota(jnp.int32, (sub, LANES), 1).astype(F32)
    neg = -jnp.inf
    big = float(1 << 20)

    def argmax_first(vals):
        m = jnp.max(vals, axis=-1, keepdims=True)
        idx = jnp.min(jnp.where(vals == m, lane, big), axis=-1, keepdims=True)
        return m, idx

    def top_k(lg):
        grp = jnp.where(lane < N_GROUPS, lg, neg)
        g_max, g_sel = argmax_first(grp)
        p_grp = 1.0 / jnp.sum(jnp.exp(grp - g_max), axis=-1, keepdims=True)
        lo_lane = N_GROUPS + g_sel * EXPERTS_PER_GROUP
        ex = jnp.where((lane >= lo_lane) & (lane < lo_lane + EXPERTS_PER_GROUP), lg, neg)
        m1, i1 = argmax_first(ex)
        m2, i2 = argmax_first(jnp.where(lane == i1, neg, ex))
        e2 = jnp.exp(m2 - m1)
        return i1 - N_GROUPS, i2 - N_GROUPS, 1.0 / (1.0 + e2) * p_grp, e2 / (1.0 + e2) * p_grp

    picks = [top_k(lg) for lg in logits]

    @pl.when(pl.program_id(0) == 0)
    def _():
        cnt_s[...] = jnp.zeros(cnt_s.shape, F32)

    onehots = [((lane == e1).astype(F32), (lane == e2).astype(F32)) for e1, e2, _, _ in picks]
    earlier = (lax.broadcasted_iota(jnp.int32, (sub, sub), 0)
               > lax.broadcasted_iota(jnp.int32, (sub, sub), 1)).astype(BF16)
    within = [_dot(earlier, (oh1 + oh2).astype(BF16)) for oh1, oh2 in onehots]
    counts = cnt_s[...]
    for rs, (e1, e2, w1, w2), (oh1, oh2), within_i in zip(blocks, picks, onehots, within):
        before = within_i + counts
        rank1 = jnp.sum(before * oh1, axis=-1, keepdims=True)
        rank2 = jnp.sum(before * oh2, axis=-1, keepdims=True)
        counts = counts + jnp.sum(oh1 + oh2, axis=0, keepdims=True)
        route = jnp.zeros((sub, LANES), F32)
        for k, val in ((ROUTE_E1, e1), (ROUTE_E2, e2), (ROUTE_W1, w1), (ROUTE_W2, w2),
                       (ROUTE_RANK1, rank1), (ROUTE_RANK2, rank2)):
            route = jnp.where(lane == k, val, route)
        route_ref[rs, :] = route
        assign_ref[:, rs] = route.T[:ASSIGN_ROWS].astype(jnp.int32)
    cnt_s[...] = counts
    counts_ref[...] = counts


def _mix_out_call(x2, ya, o, z, w, rows):
    tokens = x2.shape[0]
    tok = lambda width: pl.BlockSpec((rows, width), lambda i: (i, 0))
    full = lambda a: pl.BlockSpec(a.shape, lambda i: (0,) * a.ndim)
    consts = (w['gdn_norm_w'], w['w_out_a'], w['w_out_b'], w['ffn_norm_w'], w['w_route'],
              w['b_route'])
    return pl.pallas_call(
        _mix_out_kernel,
        grid=(tokens // rows,),
        in_specs=[tok(D_MODEL), tok(CONV_CH), tok(GDN_W), tok(GDN_W)] + [full(a) for a in consts],
        out_specs=(tok(D_MODEL), tok(D_MODEL // 2), tok(LANES),
                   pl.BlockSpec((ASSIGN_ROWS, rows), lambda i: (0, i)),
                   pl.BlockSpec((1, LANES), lambda i: (0, 0))),
        out_shape=(jax.ShapeDtypeStruct((tokens, D_MODEL), F32),
                   jax.ShapeDtypeStruct((tokens, D_MODEL // 2), jnp.uint32),
                   jax.ShapeDtypeStruct((tokens, LANES), F32),
                   jax.ShapeDtypeStruct((ASSIGN_ROWS, tokens), jnp.int32),
                   jax.ShapeDtypeStruct((1, LANES), F32)),
        scratch_shapes=[pltpu.VMEM((1, LANES), F32)],
        compiler_params=pltpu.CompilerParams(
            dimension_semantics=("arbitrary",), vmem_limit_bytes=VMEM_LIMIT),
        name="mix_out",
    )(x2, ya, o, z, *consts)


def _sc_row_move(table, idx, out_rows, chunk, scatter, name):
    n = idx.shape[0]
    n_src, width = table.shape
    workers = SC_CORES * SC_SUBCORES
    per_w = n // workers
    assert n % workers == 0 and per_w % (2 * chunk) == 0
    assert n_src % per_w == 0 or not scatter
    pairs = per_w // (2 * chunk)
    mesh = plsc.VectorSubcoreMesh(core_axis_name="c", subcore_axis_name="s",
                                  num_cores=SC_CORES, num_subcores=SC_SUBCORES)

    def body(table_hbm, idx_hbm, out_hbm, idx_v, buf_a, buf_b, sem_ra, sem_rb, sem_wa, sem_wb):
        base = (lax.axis_index("s") * SC_CORES + lax.axis_index("c")) * per_w
        src_base = lax.rem(base, n_src)
        pltpu.sync_copy(idx_hbm.at[pl.ds(base, per_w)], idx_v)

        def read(c, buf, sem):
            off = pl.multiple_of(c * chunk, chunk)
            src = (table_hbm.at[pl.ds(src_base + off, chunk)] if scatter
                   else table_hbm.at[idx_v.at[pl.ds(off, chunk)]])
            return pltpu.make_async_copy(src, buf, sem)

        def write(c, buf, sem):
            off = pl.multiple_of(c * chunk, chunk)
            dst = (out_hbm.at[idx_v.at[pl.ds(off, chunk)]] if scatter
                   else out_hbm.at[pl.ds(base + off, chunk)])
            return pltpu.make_async_copy(buf, dst, sem)

        read(0, buf_a, sem_ra).start()

        @pl.loop(0, pairs)
        def _(j):
            ca = 2 * j
            cb = ca + 1
            read(cb, buf_b, sem_rb).start()
            read(ca, buf_a, sem_ra).wait()
            write(ca, buf_a, sem_wa).start()
            read(cb, buf_b, sem_rb).wait()
            write(cb, buf_b, sem_wb).start()
            write(ca, buf_a, sem_wa).wait()

            @pl.when(j + 1 < pairs)
            def _():
                read(ca + 2, buf_a, sem_ra).start()

            write(cb, buf_b, sem_wb).wait()

    return pl.kernel(
        body,
        out_type=jax.ShapeDtypeStruct((out_rows, width), table.dtype),
        mesh=mesh,
        scratch_types=[pltpu.VMEM((per_w,), jnp.int32),
                       pltpu.VMEM((chunk, width), table.dtype),
                       pltpu.VMEM((chunk, width), table.dtype),
                       pltpu.SemaphoreType.DMA, pltpu.SemaphoreType.DMA,
                       pltpu.SemaphoreType.DMA, pltpu.SemaphoreType.DMA],
        name=name,
    )(table, idx)


def _experts_kernel(bexp_ref, bvalid_ref, nused_ref, x_ref, wg_hbm, wu_hbm, wd_hbm, y_ref,
                    wg_s, wu_s, wd_s, stage_g, stage_u, stage_d, sems, seq_s):
    n_used = nused_ref[0]
    n_last = bexp_ref.shape[0] - 1
    rows = x_ref.shape[0] // EXPERT_STEP_BLOCKS
    sub = EXPERT_SUB_ROWS if rows % EXPERT_SUB_ROWS == 0 else rows

    def weight_copies(e, slot):
        return (pltpu.make_async_copy(wg_hbm.at[e], stage_g.at[slot], sems.at[slot, 0]),
                pltpu.make_async_copy(wu_hbm.at[e], stage_u.at[slot], sems.at[slot, 1]),
                pltpu.make_async_copy(wd_hbm.at[e], stage_d.at[slot], sems.at[slot, 2]))

    def block(i, r0):
        expert = bexp_ref[i]
        first_of_expert = (i == 0) | (expert != bexp_ref[jnp.maximum(i - 1, 0)])

        @pl.when((i == 0) & (n_used > 0))
        def _():
            seq_s[0] = 0
            for copy in weight_copies(expert, 0):
                copy.start()

        @pl.when((i < n_used) & first_of_expert)
        def _():
            @pl.when(i > 0)
            def _():
                seq_s[0] = seq_s[0] + 1

            slot = seq_s[0] % 2
            nxt = lax.while_loop(
                lambda j: (j < n_used) & (bexp_ref[jnp.minimum(j, n_last)] == expert),
                lambda j: j + 1, i + 1)

            @pl.when(nxt < n_used)
            def _():
                for copy in weight_copies(bexp_ref[jnp.minimum(nxt, n_last)], 1 - slot):
                    copy.start()

            for copy in weight_copies(expert, slot):
                copy.wait()
            wg_s[...] = stage_g[slot].astype(BF16)
            wu_s[...] = stage_u[slot].astype(BF16)
            wd_s[...] = stage_d[slot].astype(BF16)

        valid = bvalid_ref[i]

        def ffn(n_sub):
            blocks = [slice(r0 + k * sub, r0 + (k + 1) * sub) for k in range(n_sub)]
            row = lax.broadcasted_iota(jnp.int32, (sub, x_ref.shape[1]), 0)
            xb = [jnp.concatenate(_unpack_bf16_pairs(
                      jnp.where(row + (rs.start - r0) < valid, x_ref[rs, :], jnp.uint32(0))),
                      axis=1).astype(BF16) for rs in blocks]
            gate = [_dot(xb_i, wg_s[...]) for xb_i in xb]
            up = [_dot(xb_i, wu_s[...]) for xb_i in xb]
            hid = [((g_i * _sigmoid(g_i)) * u_i).astype(BF16) for g_i, u_i in zip(gate, up)]
            y = [_dot(h_i, wd_s[...]) for h_i in hid]
            for rs, y_i in zip(blocks, y):
                y_ref[rs, :] = _pack_bf16_pairs(y_i.astype(BF16))
            rest = rows - n_sub * sub
            if rest:
                y_ref[r0 + n_sub * sub:r0 + rows, :] = jnp.zeros((rest, y_ref.shape[1]), jnp.uint32)

        tail_sub = (rows // sub + 1) // 2
        pl.when((i < n_used) & (valid > tail_sub * sub))(lambda: ffn(rows // sub))
        pl.when((i < n_used) & (valid <= tail_sub * sub))(lambda: ffn(tail_sub))
        pl.when(i >= n_used)(lambda: ffn(0))

    for j in range(EXPERT_STEP_BLOCKS):
        block(pl.program_id(0) * EXPERT_STEP_BLOCKS + j, j * rows)


def _experts_call(block_expert, block_valid, n_used, x_rows, w_gate, w_up, w_down, rows):
    n_blocks = block_expert.shape[0]
    step_rows = EXPERT_STEP_BLOCKS * rows
    hbm = pl.BlockSpec(memory_space=pl.ANY)
    grid_spec = pltpu.PrefetchScalarGridSpec(
        num_scalar_prefetch=3,
        grid=(n_blocks // EXPERT_STEP_BLOCKS,),
        in_specs=[pl.BlockSpec((step_rows, D_MODEL // 2), lambda i, be, bv, nu: (i, 0)),
                  hbm, hbm, hbm],
        out_specs=pl.BlockSpec((step_rows, D_MODEL // 2), lambda i, be, bv, nu: (i, 0)),
        scratch_shapes=[pltpu.VMEM((D_MODEL, D_EXPERT), BF16),
                        pltpu.VMEM((D_MODEL, D_EXPERT), BF16),
                        pltpu.VMEM((D_EXPERT, D_MODEL), BF16),
                        pltpu.VMEM((2, D_MODEL, D_EXPERT), F32),
                        pltpu.VMEM((2, D_MODEL, D_EXPERT), F32),
                        pltpu.VMEM((2, D_EXPERT, D_MODEL), F32),
                        pltpu.SemaphoreType.DMA((2, 3)),
                        pltpu.SMEM((1,), jnp.int32)],
    )
    return pl.pallas_call(
        _experts_kernel,
        grid_spec=grid_spec,
        out_shape=jax.ShapeDtypeStruct((n_blocks * rows, D_MODEL // 2), jnp.uint32),
        compiler_params=pltpu.CompilerParams(
            dimension_semantics=("arbitrary",), vmem_limit_bytes=VMEM_LIMIT),
        name="experts",
    )(block_expert, block_valid, n_used, x_rows, w_gate, w_up, w_down)


def _combine_kernel(y1_ref, y2_ref, h_ref, route_ref, fw_ref, out_ref):
    route = route_ref[...]
    y1 = jnp.concatenate(_unpack_bf16_pairs(y1_ref[...]), axis=1)
    y2 = jnp.concatenate(_unpack_bf16_pairs(y2_ref[...]), axis=1)
    moe = route[:, ROUTE_W1:ROUTE_W1 + 1] * y1 + route[:, ROUTE_W2:ROUTE_W2 + 1] * y2
    h2 = h_ref[...] + moe
    out_ref[...] = h2 * lax.rsqrt(jnp.mean(h2 * h2, axis=-1, keepdims=True) + EPS) * fw_ref[...]


def _combine_call(y_tok, h1, route, final_w, rows):
    tokens = h1.shape[0]
    steps = tokens // rows
    tok = lambda width: pl.BlockSpec((rows, width), lambda i: (i, 0))
    return pl.pallas_call(
        _combine_kernel,
        grid=(steps,),
        in_specs=[tok(D_MODEL // 2), pl.BlockSpec((rows, D_MODEL // 2), lambda i: (i + steps, 0)),
                  tok(D_MODEL), tok(LANES), pl.BlockSpec((1, D_MODEL), lambda i: (0, 0))],
        out_specs=tok(D_MODEL),
        out_shape=jax.ShapeDtypeStruct((tokens, D_MODEL), F32),
        compiler_params=pltpu.CompilerParams(
            dimension_semantics=("arbitrary",), vmem_limit_bytes=VMEM_LIMIT),
        name="combine",
    )(y_tok, y_tok, h1, route, final_w)


def _dispatch_tables(assign, counts, tokens, rows):
    n_blocks = pl.cdiv((tokens * TOP_K + N_EXPERTS * (rows - 1)) // rows,
                       EXPERT_STEP_BLOCKS) * EXPERT_STEP_BLOCKS
    counts = counts[0, :N_EXPERTS].astype(jnp.int32)
    padded = (counts + rows - 1) // rows * rows
    pad_end = jnp.cumsum(padded)
    pad_start = pad_end - padded
    experts = assign[ROUTE_E1:ROUTE_E2 + 1]
    ranks = assign[ROUTE_RANK1:ROUTE_RANK2 + 1]
    is_expert = experts[..., None] == jnp.arange(N_EXPERTS, dtype=jnp.int32)
    dest = (jnp.sum(jnp.where(is_expert, pad_start, 0), axis=-1) + ranks).reshape(-1)
    block_start = jnp.arange(n_blocks, dtype=jnp.int32) * rows
    in_run = (block_start[:, None] >= pad_start[None, :]) & (block_start[:, None] < pad_end[None, :])
    lookup = lambda table: jnp.sum(jnp.where(in_run, table[None, :], 0), axis=1)
    block_expert = lookup(jnp.arange(N_EXPERTS, dtype=jnp.int32))
    block_valid = jnp.clip(lookup(pad_start + counts) - block_start, 0, rows)
    n_used = (pad_end[-1] // rows).astype(jnp.int32).reshape(1)
    return dest, block_expert, block_valid.astype(jnp.int32), n_used, n_blocks


def _prepare_weights(mix_norm_w, w_in, conv_mix_w, conv_mix_norm_w, qkv_conv_w, a_log, dt_bias,
                     gdn_norm_w, w_out, ffn_norm_w, w_group, b_group, w_router, b_router):
    pad_lanes = lambda v: jnp.pad(v.reshape(1, -1), ((0, 0), (0, LANES - v.size)))
    w_in_bf = w_in.astype(BF16)
    w_ab = jnp.pad(w_in_bf[:, Z_COL0 + GDN_W:], ((0, 0), (0, LANES - 2 * HEADS)))
    grp = jnp.arange(PROJ_COLS) // CONV_GROUP_W
    gmat = jnp.where(grp[:, None] == grp[None, :], 1.0 / CONV_GROUP_W, 0.0).astype(BF16)
    w_route = jnp.concatenate([w_group, w_router.reshape(D_MODEL, N_EXPERTS)], axis=1)
    w_route = jnp.pad(w_route, ((0, 0), (0, LANES - w_route.shape[1])))
    return dict(
        mix_norm_w=mix_norm_w.reshape(1, -1),
        w_a=jnp.concatenate([w_in_bf[:, part * CONV_CH + i * PROJ_COLS:][:, :PROJ_COLS]
                             for i in range(CONV_CH // PROJ_COLS) for part in range(3)], axis=1),
        w_in=w_in_bf,
        w_ab=w_ab,
        conv_mix_w=conv_mix_w,
        conv_mix_norm_w=conv_mix_norm_w.reshape(1, -1),
        gmat=gmat,
        qkv_conv_w=qkv_conv_w,
        a_log=pad_lanes(a_log),
        dt_bias=pad_lanes(dt_bias),
        gdn_norm_w=gdn_norm_w.reshape(1, -1),
        w_out_a=w_out[:CONV_CH].astype(BF16),
        w_out_b=w_out[CONV_CH:].astype(BF16),
        ffn_norm_w=ffn_norm_w.reshape(1, -1),
        w_route=w_route.astype(BF16),
        b_route=pad_lanes(jnp.concatenate([b_group, b_router.reshape(-1)])),
    )


def _tile(n, preferred):
    return preferred if n % preferred == 0 else n


def kernel(x, meta_tokens, mix_norm_w, w_in, conv_mix_w, conv_mix_norm_w, qkv_conv_w, a_log,
           dt_bias, gdn_norm_w, w_out, ffn_norm_w, w_group, b_group, w_router, b_router, w_gate,
           w_up, w_down, final_norm_w):
    assert mix_norm_w.shape[0] == 1, "single-layer kernel"
    batch, seq, _ = x.shape
    assert seq % CHUNK == 0
    w = _prepare_weights(mix_norm_w[0], w_in[0], conv_mix_w[0], conv_mix_norm_w[0], qkv_conv_w[0],
                         a_log[0], dt_bias[0], gdn_norm_w[0], w_out[0], ffn_norm_w[0], w_group[0],
                         b_group[0], w_router[0], b_router[0])

    prefix = jnp.concatenate([jnp.zeros((CHUNK - N_META, D_MODEL), x.dtype),
                              meta_tokens.astype(x.dtype)], axis=0)[None]
    zero_cu = jnp.zeros((HIST, CONV_CH), F32)
    zero_qkv = jnp.zeros((HIST, 3 * GDN_W), F32)
    _, pq, pk, pv, _, pgb, tail_cu, tail_qkv = _proj_call(prefix, zero_cu, zero_qkv, w, CHUNK)

    ya, q, k, v, z, gb, _, _ = _proj_call(x, tail_cu[0], tail_qkv[0], w, _tile(seq, PROJ_ROWS))
    o = _gdn_call(pq, pk, pv, pgb, q, k, v, gb, _tile(seq // CHUNK, GDN_CHUNKS))

    tokens = batch * seq
    flat = lambda a: a.reshape(tokens, a.shape[-1])
    h1, xn2, route, assign, counts = _mix_out_call(flat(x), flat(ya), flat(o), flat(z), w,
                                           _tile(tokens, MIX_ROWS))

    dest, block_expert, block_valid, n_used, n_blocks = _dispatch_tables(assign, counts, tokens,
                                                                         EXPERT_ROWS)
    x_rows = _sc_row_move(xn2, dest, n_blocks * EXPERT_ROWS, SC_DISPATCH_CHUNK, True,
                          "dispatch_scatter")
    y_rows = _experts_call(block_expert, block_valid, n_used, x_rows, w_gate[0], w_up[0],
                           w_down[0], EXPERT_ROWS)
    y_tok = _sc_row_move(y_rows, dest, TOP_K * tokens, SC_COMBINE_CHUNK, False, "combine_gather")
    out = _combine_call(y_tok, h1, route, final_norm_w.reshape(1, -1), _tile(tokens, COMBINE_ROWS))
    return out.reshape(batch, seq, D_MODEL)
```

```python
import functools

import jax
import jax.numpy as jnp
from jax import lax
from jax.experimental import pallas as pl
from jax.experimental.pallas import tpu as pltpu
from jax.experimental.pallas import tpu_sc as plsc

F32 = jnp.float32
BF16 = jnp.bfloat16
EPS = 1e-6

D_MODEL = 1024
N_META = 16
CONV_CH = 512
CONV_GROUP_W = 64
HEADS = 4
HEAD_DIM = 128
GDN_W = HEADS * HEAD_DIM
QKV_COL0 = 3 * CONV_CH
Z_COL0 = QKV_COL0 + 3 * GDN_W
CHUNK = 64
N_GROUPS = 4
EXPERTS_PER_GROUP = 8
N_EXPERTS = N_GROUPS * EXPERTS_PER_GROUP
TOP_K = 2
D_EXPERT = 512
LANES = 128
ROUTE_E1, ROUTE_E2, ROUTE_RANK1, ROUTE_RANK2, ROUTE_W1, ROUTE_W2 = range(6)
ASSIGN_ROWS = 8
HIST = 8

PROJ_ROWS = 512
PROJ_COLS = 256
PROJ_SUB_ROWS = 128
GDN_CHUNKS = 8
GDN_GROUP = 4
MIX_ROWS = 1024
MIX_SUB_ROWS = 128
EXPERT_ROWS = 256
EXPERT_SUB_ROWS = 128
EXPERT_STEP_BLOCKS = 4
COMBINE_ROWS = 1024
VMEM_LIMIT = 56 * 1024 * 1024
SC_CORES = 2
SC_SUBCORES = 16
SC_DISPATCH_CHUNK = 64
SC_COMBINE_CHUNK = 64


def _dot(a, b):
    return jnp.dot(a, b, preferred_element_type=F32)


def _dot_nt(a, b):
    return lax.dot_general(a, b, (((1,), (1,)), ((), ())), preferred_element_type=F32)


def _dot_tn(a, b):
    return lax.dot_general(a, b, (((0,), (0,)), ((), ())), preferred_element_type=F32)


def _split_bf16(x):
    hi = x.astype(BF16)
    lo = (x - hi.astype(F32)).astype(BF16)
    return hi, lo


def _sigmoid(x):
    return 1.0 / (1.0 + jnp.exp(-x))


def _pack_bf16_pairs(x_bf16):
    bits = pltpu.bitcast(x_bf16.astype(F32), jnp.uint32)
    n = x_bf16.shape[1] // 2
    return (bits[:, :n] >> 16) | (bits[:, n:] & jnp.uint32(0xFFFF0000))


def _unpack_bf16_pairs(packed):
    return (pltpu.bitcast(packed << 16, F32),
            pltpu.bitcast(packed & jnp.uint32(0xFFFF0000), F32))


def _proj_kernel(x_ref, hcu_ref, hqkv_ref, nw_ref, wa_ref, wi_ref, wab_ref, cmw_ref,
                 cmn_ref, gmat_ref, qcw_ref, alog_ref, dtb_ref,
                 ya_ref, q_ref, k_ref, v_ref, z_ref, gb_ref, tcu_ref, tqkv_ref,
                 cu_s, qkv_s, xn_s):
    rows = x_ref.shape[0]
    sub = PROJ_SUB_ROWS if rows % PROJ_SUB_ROWS == 0 else rows
    blocks = [slice(r, r + sub) for r in range(0, rows, sub)]

    @pl.when(pl.program_id(1) == 0)
    def _():
        cu_s[0:HIST, :] = hcu_ref[...]
        qkv_s[0:HIST, :] = hqkv_ref[...]

    x = x_ref[...]
    ms = jnp.mean(x * x, axis=-1, keepdims=True)
    xn_s[...] = (x * lax.rsqrt(ms + EPS) * nw_ref[...]).astype(BF16)

    def causal_conv(buf, cur, w_ref, cols, tail_ref):
        taps = w_ref.shape[0]
        acc = pltpu.roll(cur, taps - 1, axis=0) * w_ref[0:1, cols]
        for j in range(1, taps - 1):
            acc = acc + pltpu.roll(cur, taps - 1 - j, axis=0) * w_ref[j:j + 1, cols]
        acc = acc + cur * w_ref[taps - 1:taps, cols]
        buf[HIST:2 * HIST, cols] = cur[:HIST]
        seam = buf[pl.ds(HIST - taps + 1, HIST), cols] * w_ref[0:1, cols]
        for j in range(1, taps):
            seam = seam + buf[pl.ds(HIST - taps + 1 + j, HIST), cols] * w_ref[j:j + 1, cols]
        tail = cur[sub - HIST:]
        buf[0:HIST, cols] = tail
        tail_ref[:, cols] = tail
        return jnp.concatenate([seam, acc[HIST:]], axis=0)

    def mixer_a_tail(i, rs, pa):
        cols = slice(i * PROJ_COLS, (i + 1) * PROJ_COLS)
        cu = pa[:, PROJ_COLS:2 * PROJ_COLS] * pa[:, 2 * PROJ_COLS:]
        ya = pa[:, :PROJ_COLS] * causal_conv(cu_s, cu, cmw_ref, cols, tcu_ref)
        msg = _dot((ya * ya).astype(BF16), gmat_ref[...])
        ya_ref[rs, cols] = (ya * lax.rsqrt(msg + EPS) * cmn_ref[:, cols]).astype(BF16)

    heads_per_chunk = PROJ_COLS // HEAD_DIM

    def qkv_tail(i, rs, pq):
        cols = slice(i * PROJ_COLS, (i + 1) * PROJ_COLS)
        c = causal_conv(qkv_s, pq, qcw_ref, cols, tqkv_ref)
        c = c * _sigmoid(c)
        part, first_head = divmod(i * heads_per_chunk, HEADS)
        for j in range(heads_per_chunk):
            ch = c[:, j * HEAD_DIM:(j + 1) * HEAD_DIM]
            sl = slice((first_head + j) * HEAD_DIM, (first_head + j + 1) * HEAD_DIM)
            if part == 0:
                norm = lax.rsqrt(jnp.sum(ch * ch, axis=-1, keepdims=True) + EPS)
                q_ref[rs, sl] = ch * norm * (HEAD_DIM ** -0.5)
            elif part == 1:
                k_ref[rs, sl] = ch * lax.rsqrt(jnp.sum(ch * ch, axis=-1, keepdims=True) + EPS)
            else:
                v_ref[rs, sl] = ch

    def z_tail(rs, pz):
        z_ref[rs, :] = pz

    def decay_beta_tail(rs, ab):
        sp_in = ab + dtb_ref[...]
        softplus = jnp.maximum(sp_in, 0.0) + jnp.log1p(jnp.exp(-jnp.abs(sp_in)))
        g = -jnp.exp(alog_ref[...]) * softplus
        lane = lax.broadcasted_iota(jnp.int32, ab.shape, 1)
        gb_ref[rs, :] = jnp.where(lane < HEADS, g, jnp.where(lane < 2 * HEADS, _sigmoid(ab), 0.0))

    def matmul(w_ref, w_cols, rs):
        return _dot(xn_s[rs, :], w_ref[:, w_cols])

    stages = []
    for i in range(CONV_CH // PROJ_COLS):
        w_cols = slice(3 * i * PROJ_COLS, 3 * (i + 1) * PROJ_COLS)
        stages += [(functools.partial(matmul, wa_ref, w_cols, rs),
                    functools.partial(mixer_a_tail, i, rs)) for rs in blocks]
    for i in range(3 * GDN_W // PROJ_COLS):
        w_cols = slice(QKV_COL0 + i * PROJ_COLS, QKV_COL0 + (i + 1) * PROJ_COLS)
        stages += [(functools.partial(matmul, wi_ref, w_cols, rs),
                    functools.partial(qkv_tail, i, rs)) for rs in blocks]
    stages += [(functools.partial(matmul, wi_ref, slice(Z_COL0, Z_COL0 + GDN_W), rs),
                functools.partial(z_tail, rs)) for rs in blocks]
    stages += [(functools.partial(matmul, wab_ref, slice(None), rs),
                functools.partial(decay_beta_tail, rs)) for rs in blocks]
    pending = None
    for issue, tail in stages:
        res = issue()
        if pending is not None:
            pending()
        pending = functools.partial(tail, res)
    pending()


def _proj_call(x3, hist_cu, hist_qkv, w, rows):
    nb, seq, _ = x3.shape
    nt = seq // rows
    tok = lambda width: pl.BlockSpec((None, rows, width), lambda b, t: (b, t, 0))
    full = lambda a: pl.BlockSpec(a.shape, lambda b, t: (0,) * a.ndim,
                                  pipeline_mode=pl.Buffered(1))
    tail = lambda width: pl.BlockSpec((None, HIST, width), lambda b, t: (b, 0, 0))
    consts = (hist_cu, hist_qkv, w['mix_norm_w'], w['w_a'], w['w_in'], w['w_ab'],
              w['conv_mix_w'], w['conv_mix_norm_w'], w['gmat'], w['qkv_conv_w'], w['a_log'],
              w['dt_bias'])
    out_shape = (
        jax.ShapeDtypeStruct((nb, seq, CONV_CH), BF16),
        jax.ShapeDtypeStruct((nb, seq, GDN_W), F32),
        jax.ShapeDtypeStruct((nb, seq, GDN_W), F32),
        jax.ShapeDtypeStruct((nb, seq, GDN_W), F32),
        jax.ShapeDtypeStruct((nb, seq, GDN_W), F32),
        jax.ShapeDtypeStruct((nb, seq, LANES), F32),
        jax.ShapeDtypeStruct((nb, HIST, CONV_CH), F32),
        jax.ShapeDtypeStruct((nb, HIST, 3 * GDN_W), F32),
    )
    return pl.pallas_call(
        _proj_kernel,
        grid=(nb, nt),
        in_specs=[tok(D_MODEL)] + [full(a) for a in consts],
        out_specs=(tok(CONV_CH), tok(GDN_W), tok(GDN_W), tok(GDN_W), tok(GDN_W), tok(LANES),
                   tail(CONV_CH), tail(3 * GDN_W)),
        out_shape=out_shape,
        scratch_shapes=[pltpu.VMEM((2 * HIST, CONV_CH), F32),
                        pltpu.VMEM((2 * HIST, 3 * GDN_W), F32),
                        pltpu.VMEM((rows, D_MODEL), BF16)],
        compiler_params=pltpu.CompilerParams(
            dimension_semantics=("arbitrary", "arbitrary"), vmem_limit_bytes=VMEM_LIMIT),
        name="proj",
    )(x3, *consts)


def _chunk_masks():
    row = lax.broadcasted_iota(jnp.int32, (CHUNK, CHUNK), 0)
    col = lax.broadcasted_iota(jnp.int32, (CHUNK, CHUNK), 1)
    incl = row >= col
    strict = row > col
    levels = []
    n = 1
    while n < CHUNK:
        levels.append((row // (2 * n) == col // (2 * n)) & ((row // n) % 2 == 1) & ((col // n) % 2 == 0))
        n *= 2
    return incl, strict, levels


def _chunk_cumsum(gb_blk, incl):
    tri = incl.astype(BF16)
    hi, lo = _split_bf16(gb_blk)
    return _dot(tri, hi) + _dot(tri, lo)


def _chunk_transforms(chains, masks, state_only):
    incl, strict, levels = masks
    eye = (lax.broadcasted_iota(jnp.int32, (CHUNK, CHUNK), 0)
           == lax.broadcasted_iota(jnp.int32, (CHUNK, CHUNK), 1)).astype(F32)
    decay = [jnp.exp(jnp.where(incl, gc_col - gc_row, -jnp.inf))
             for (_, _, _, _, gc_col, gc_row, _) in chains]
    kb = [kh * beta for (_, kh, _, beta, _, _, _) in chains]
    k_bf = [kh.astype(BF16) for (_, kh, _, _, _, _, _) in chains]
    a_mat = [jnp.where(strict, _dot_nt(kb_i.astype(BF16), k_i) * d_i, 0.0)
             for kb_i, k_i, d_i in zip(kb, k_bf, decay)]
    t_inv = [eye - jnp.where(levels[0], a_i, 0.0) for a_i in a_mat]
    for lvl in levels[1:]:
        t_bf = [t_i.astype(BF16) for t_i in t_inv]
        m1 = [_dot(jnp.where(lvl, a_i, 0.0).astype(BF16), t_i) for a_i, t_i in zip(a_mat, t_bf)]
        t_inv = [t_i - _dot(tb_i, m_i.astype(BF16)) for t_i, tb_i, m_i in zip(t_inv, t_bf, m1)]
    rhs = [jnp.concatenate([vh * beta, kb_i * jnp.exp(gc_col)], axis=1)
           for (_, _, vh, beta, gc_col, _, _), kb_i in zip(chains, kb)]
    uw = [_dot(t_i.astype(BF16), r_i.astype(BF16)).astype(BF16)
          for t_i, r_i in zip(t_inv, rhs)]
    kd = [kh * jnp.exp(g_last - gc_col) for (_, kh, _, _, gc_col, _, g_last) in chains]
    pn = [_dot_tn(kd_i.astype(BF16), uw_i) for kd_i, uw_i in zip(kd, uw)]
    if state_only:
        return [pn_i[:, :HEAD_DIM] for pn_i in pn]
    intra = [jnp.where(incl, _dot_nt(qh.astype(BF16), k_i) * d_i, 0.0)
             for (qh, _, _, _, _, _, _), k_i, d_i in zip(chains, k_bf, decay)]
    iuw = [_dot(in_i.astype(BF16), uw_i) for in_i, uw_i in zip(intra, uw)]
    out = []
    for (qh, _, _, _, gc_col, _, g_last), pn_i, iuw_i in zip(chains, pn, iuw):
        q_part = qh * jnp.exp(gc_col) - iuw_i[:, HEAD_DIM:]
        out.append((q_part, pn_i[:, HEAD_DIM:], iuw_i[:, :HEAD_DIM], pn_i[:, :HEAD_DIM],
                    jnp.exp(g_last)))
    return out


def _gdn_kernel(pq_ref, pk_ref, pv_ref, pgb_ref, q_ref, k_ref, v_ref, gb_ref, o_ref,
                s_s, qp_s, op_s, n_s, a_s):
    nb = q_ref.shape[0]
    n_chunks = q_ref.shape[1] // CHUNK
    group = GDN_GROUP if n_chunks % GDN_GROUP == 0 else 1
    masks = _chunk_masks()

    def chains_of(gb_blk, q_blk, k_blk, v_blk):
        gc = _chunk_cumsum(gb_blk, masks[0])
        gc_t = gc.T
        res = []
        for h in range(HEADS):
            sl = slice(h * HEAD_DIM, (h + 1) * HEAD_DIM)
            res.append((q_blk(sl), k_blk(sl), v_blk(sl), gb_blk[:, HEADS + h:HEADS + h + 1],
                        gc[:, h:h + 1], gc_t[h:h + 1, :CHUNK], gc[CHUNK - 1:CHUNK, h:h + 1]))
        return res

    @pl.when(pl.program_id(0) == 0)
    def _():
        chains = chains_of(pgb_ref[...], lambda sl: pq_ref[:, sl], lambda sl: pk_ref[:, sl],
                           lambda sl: pv_ref[:, sl])
        for h, n_mat in enumerate(_chunk_transforms(chains, masks, True)):
            for b in range(nb):
                s_s[b * HEADS + h] = n_mat

    def transform_group(gi, carry):
        chains, where = [], []
        for cc in range(group):
            c = gi * group + cc
            rows = pl.ds(pl.multiple_of(c * CHUNK, CHUNK), CHUNK)
            for b in range(nb):
                chains += chains_of(gb_ref[b, rows, :], lambda sl: q_ref[b, rows, sl],
                                    lambda sl: k_ref[b, rows, sl], lambda sl: v_ref[b, rows, sl])
                where += [(c, b * HEADS + h) for h in range(HEADS)]
        for (c, ch), (q_part, p_mat, o_part, n_mat, a) in zip(
                where, _chunk_transforms(chains, masks, False)):
            qp_s[c, ch, 0:CHUNK, :] = q_part.astype(BF16)
            qp_s[c, ch, CHUNK:, :] = p_mat.astype(BF16)
            op_s[c, ch] = o_part
            n_s[c, ch] = n_mat
            a_s[c, ch] = jnp.broadcast_to(a, (8, HEAD_DIM))
        return carry

    lax.fori_loop(0, n_chunks // group, transform_group, 0)

    def scan_chunk(c, carry):
        r0 = pl.multiple_of(c * CHUNK, CHUNK)
        for b in range(nb):
            for h in range(HEADS):
                ch = b * HEADS + h
                s = s_s[ch]
                r = _dot(qp_s[c, ch], s.astype(BF16))
                o_ref[b, pl.ds(r0, CHUNK), h * HEAD_DIM:(h + 1) * HEAD_DIM] = r[:CHUNK] + op_s[c, ch]
                s_s[ch] = a_s[c, ch][0:1, :] * s - r[CHUNK:] + n_s[c, ch]
        return carry

    lax.fori_loop(0, n_chunks, scan_chunk, 0)


def _gdn_call(pq, pk, pv, pgb, q, k, v, gb, chunks_per_step):
    nb, seq, _ = q.shape
    rows = chunks_per_step * CHUNK
    steps = seq // rows
    tok = lambda width: pl.BlockSpec((nb, rows, width), lambda i: (0, i, 0))
    pre = lambda width: pl.BlockSpec((None, CHUNK, width), lambda i: (0, 0, 0))
    nch = nb * HEADS
    return pl.pallas_call(
        _gdn_kernel,
        grid=(steps,),
        in_specs=[pre(GDN_W), pre(GDN_W), pre(GDN_W), pre(LANES),
                  tok(GDN_W), tok(GDN_W), tok(GDN_W), tok(LANES)],
        out_specs=tok(GDN_W),
        out_shape=jax.ShapeDtypeStruct((nb, seq, GDN_W), F32),
        scratch_shapes=[
            pltpu.VMEM((nch, HEAD_DIM, HEAD_DIM), F32),
            pltpu.VMEM((chunks_per_step, nch, CHUNK + HEAD_DIM, HEAD_DIM), BF16),
            pltpu.VMEM((chunks_per_step, nch, CHUNK, HEAD_DIM), F32),
            pltpu.VMEM((chunks_per_step, nch, HEAD_DIM, HEAD_DIM), F32),
            pltpu.VMEM((chunks_per_step, nch, 8, HEAD_DIM), F32),
        ],
        compiler_params=pltpu.CompilerParams(
            dimension_semantics=("arbitrary",), vmem_limit_bytes=VMEM_LIMIT),
        name="gdn",
    )(pq, pk, pv, pgb, q, k, v, gb)


def _mix_out_kernel(x_ref, ya_ref, o_ref, z_ref, gnw_ref, woa_ref, wob_ref, fnw_ref, wr_ref,
                    br_ref, h_ref, xn_ref, route_ref, assign_ref, counts_ref, cnt_s):
    rows = x_ref.shape[0]
    sub = MIX_SUB_ROWS if rows % MIX_SUB_ROWS == 0 else rows
    blocks = [slice(r, r + sub) for r in range(0, rows, sub)]

    def gated_heads(rs):
        yb = []
        for h in range(HEADS):
            sl = slice(h * HEAD_DIM, (h + 1) * HEAD_DIM)
            oh = o_ref[rs, sl]
            zh = z_ref[rs, sl]
            on = oh * lax.rsqrt(jnp.mean(oh * oh, axis=-1, keepdims=True) + EPS) * gnw_ref[...]
            yb.append((on * (zh * _sigmoid(zh))).astype(BF16))
        return jnp.concatenate(yb, axis=1)

    yb = [gated_heads(rs) for rs in blocks]
    h1 = [x_ref[rs, :] + (_dot(ya_ref[rs, :], woa_ref[...]) + _dot(yb_i, wob_ref[...]))
          for rs, yb_i in zip(blocks, yb)]
    for rs, h1_i in zip(blocks, h1):
        h_ref[rs, :] = h1_i
    xn = [h1_i * lax.rsqrt(jnp.mean(h1_i * h1_i, axis=-1, keepdims=True) + EPS) * fnw_ref[...]
          for h1_i in h1]
    xn_bf = [xn_i.astype(BF16) for xn_i in xn]
    for rs, xn_i in zip(blocks, xn_bf):
        xn_ref[rs, :] = _pack_bf16_pairs(xn_i)
    logits = [_dot(xn_i, wr_ref[...]) + br_ref[...] for xn_i in xn_bf]

    lane = lax.broadcasted_iota(jnp.int32, (sub, LANES), 1).astype(F32)
    neg = -jnp.inf
    big = float(1 << 20)

    def argmax_first(vals):
        m = jnp.max(vals, axis=-1, keepdims=True)
        idx = jnp.min(jnp.where(vals == m, lane, big), axis=-1, keepdims=True)
        return m, idx

    def top_k(lg):
        grp = jnp.where(lane < N_GROUPS, lg, neg)
        g_max, g_sel = argmax_first(grp)
        p_grp = 1.0 / jnp.sum(jnp.exp(grp - g_max), axis=-1, keepdims=True)
        lo_lane = N_GROUPS + g_sel * EXPERTS_PER_GROUP
        ex = jnp.where((lane >= lo_lane) & (lane < lo_lane + EXPERTS_PER_GROUP), lg, neg)
        m1, i1 = argmax_first(ex)
        m2, i2 = argmax_first(jnp.where(lane == i1, neg, ex))
        e2 = jnp.exp(m2 - m1)
        return i1 - N_GROUPS, i2 - N_GROUPS, 1.0 / (1.0 + e2) * p_grp, e2 / (1.0 + e2) * p_grp

    picks = [top_k(lg) for lg in logits]

    @pl.when(pl.program_id(0) == 0)
    def _():
        cnt_s[...] = jnp.zeros(cnt_s.shape, F32)

    onehots = [((lane == e1).astype(F32), (lane == e2).astype(F32)) for e1, e2, _, _ in picks]
    earlier = (lax.broadcasted_iota(jnp.int32, (sub, sub), 0)
               > lax.broadcasted_iota(jnp.int32, (sub, sub), 1)).astype(BF16)
    within = [_dot(earlier, (oh1 + oh2).astype(BF16)) for oh1, oh2 in onehots]
    counts = cnt_s[...]
    for rs, (e1, e2, w1, w2), (oh1, oh2), within_i in zip(blocks, picks, onehots, within):
        before = within_i + counts
        rank1 = jnp.sum(before * oh1, axis=-1, keepdims=True)
        rank2 = jnp.sum(before * oh2, axis=-1, keepdims=True)
        counts = counts + jnp.sum(oh1 + oh2, axis=0, keepdims=True)
        route = jnp.zeros((sub, LANES), F32)
        for k, val in ((ROUTE_E1, e1), (ROUTE_E2, e2), (ROUTE_W1, w1), (ROUTE_W2, w2),
                       (ROUTE_RANK1, rank1), (ROUTE_RANK2, rank2)):
            route = jnp.where(lane == k, val, route)
        route_ref[rs, :] = route
        assign_ref[:, rs] = route.T[:ASSIGN_ROWS].astype(jnp.int32)
    cnt_s[...] = counts
    counts_ref[...] = counts


def _mix_out_call(x2, ya, o, z, w, rows):
    tokens = x2.shape[0]
    tok = lambda width: pl.BlockSpec((rows, width), lambda i: (i, 0))
    full = lambda a: pl.BlockSpec(a.shape, lambda i: (0,) * a.ndim)
    consts = (w['gdn_norm_w'], w['w_out_a'], w['w_out_b'], w['ffn_norm_w'], w['w_route'],
              w['b_route'])
    return pl.pallas_call(
        _mix_out_kernel,
        grid=(tokens // rows,),
        in_specs=[tok(D_MODEL), tok(CONV_CH), tok(GDN_W), tok(GDN_W)] + [full(a) for a in consts],
        out_specs=(tok(D_MODEL), tok(D_MODEL // 2), tok(LANES),
                   pl.BlockSpec((ASSIGN_ROWS, rows), lambda i: (0, i)),
                   pl.BlockSpec((1, LANES), lambda i: (0, 0))),
        out_shape=(jax.ShapeDtypeStruct((tokens, D_MODEL), F32),
                   jax.ShapeDtypeStruct((tokens, D_MODEL // 2), jnp.uint32),
                   jax.ShapeDtypeStruct((tokens, LANES), F32),
                   jax.ShapeDtypeStruct((ASSIGN_ROWS, tokens), jnp.int32),
                   jax.ShapeDtypeStruct((1, LANES), F32)),
        scratch_shapes=[pltpu.VMEM((1, LANES), F32)],
        compiler_params=pltpu.CompilerParams(
            dimension_semantics=("arbitrary",), vmem_limit_bytes=VMEM_LIMIT),
        name="mix_out",
    )(x2, ya, o, z, *consts)


def _sc_row_move(table, idx, out_rows, chunk, scatter, name):
    n = idx.shape[0]
    n_src, width = table.shape
    workers = SC_CORES * SC_SUBCORES
    per_w = n // workers
    assert n % workers == 0 and per_w % (2 * chunk) == 0
    assert n_src % per_w == 0 or not scatter
    pairs = per_w // (2 * chunk)
    mesh = plsc.VectorSubcoreMesh(core_axis_name="c", subcore_axis_name="s",
                                  num_cores=SC_CORES, num_subcores=SC_SUBCORES)

    def body(table_hbm, idx_hbm, out_hbm, idx_v, buf_a, buf_b, sem_ra, sem_rb, sem_wa, sem_wb):
        base = (lax.axis_index("s") * SC_CORES + lax.axis_index("c")) * per_w
        src_base = lax.rem(base, n_src)
        pltpu.sync_copy(idx_hbm.at[pl.ds(base, per_w)], idx_v)

        def read(c, buf, sem):
            off = pl.multiple_of(c * chunk, chunk)
            src = (table_hbm.at[pl.ds(src_base + off, chunk)] if scatter
                   else table_hbm.at[idx_v.at[pl.ds(off, chunk)]])
            return pltpu.make_async_copy(src, buf, sem)

        def write(c, buf, sem):
            off = pl.multiple_of(c * chunk, chunk)
            dst = (out_hbm.at[idx_v.at[pl.ds(off, chunk)]] if scatter
                   else out_hbm.at[pl.ds(base + off, chunk)])
            return pltpu.make_async_copy(buf, dst, sem)

        read(0, buf_a, sem_ra).start()

        @pl.loop(0, pairs)
        def _(j):
            ca = 2 * j
            cb = ca + 1
            read(cb, buf_b, sem_rb).start()
            read(ca, buf_a, sem_ra).wait()
            write(ca, buf_a, sem_wa).start()
            read(cb, buf_b, sem_rb).wait()
            write(cb, buf_b, sem_wb).start()
            write(ca, buf_a, sem_wa).wait()

            @pl.when(j + 1 < pairs)
            def _():
                read(ca + 2, buf_a, sem_ra).start()

            write(cb, buf_b, sem_wb).wait()

    return pl.kernel(
        body,
        out_type=jax.ShapeDtypeStruct((out_rows, width), table.dtype),
        mesh=mesh,
        scratch_types=[pltpu.VMEM((per_w,), jnp.int32),
                       pltpu.VMEM((chunk, width), table.dtype),
                       pltpu.VMEM((chunk, width), table.dtype),
                       pltpu.SemaphoreType.DMA, pltpu.SemaphoreType.DMA,
                       pltpu.SemaphoreType.DMA, pltpu.SemaphoreType.DMA],
        name=name,
    )(table, idx)


def _experts_kernel(bexp_ref, bvalid_ref, nused_ref, x_ref, wg_hbm, wu_hbm, wd_hbm, y_ref,
                    wg_s, wu_s, wd_s, stage_g, stage_u, stage_d, sems, seq_s):
    n_used = nused_ref[0]
    n_last = bexp_ref.shape[0] - 1
    rows = x_ref.shape[0] // EXPERT_STEP_BLOCKS
    sub = EXPERT_SUB_ROWS if rows % EXPERT_SUB_ROWS == 0 else rows

    def weight_copies(e, slot):
        return (pltpu.make_async_copy(wg_hbm.at[e], stage_g.at[slot], sems.at[slot, 0]),
                pltpu.make_async_copy(wu_hbm.at[e], stage_u.at[slot], sems.at[slot, 1]),
                pltpu.make_async_copy(wd_hbm.at[e], stage_d.at[slot], sems.at[slot, 2]))

    def block(i, r0):
        expert = bexp_ref[i]
        first_of_expert = (i == 0) | (expert != bexp_ref[jnp.maximum(i - 1, 0)])

        @pl.when((i == 0) & (n_used > 0))
        def _():
            seq_s[0] = 0
            for copy in weight_copies(expert, 0):
                copy.start()

        @pl.when((i < n_used) & first_of_expert)
        def _():
            @pl.when(i > 0)
            def _():
                seq_s[0] = seq_s[0] + 1

            slot = seq_s[0] % 2
            nxt = lax.while_loop(
                lambda j: (j < n_used) & (bexp_ref[jnp.minimum(j, n_last)] == expert),
                lambda j: j + 1, i + 1)

            @pl.when(nxt < n_used)
            def _():
                for copy in weight_copies(bexp_ref[jnp.minimum(nxt, n_last)], 1 - slot):
                    copy.start()

            for copy in weight_copies(expert, slot):
                copy.wait()
            wg_s[...] = stage_g[slot].astype(BF16)
            wu_s[...] = stage_u[slot].astype(BF16)
            wd_s[...] = stage_d[slot].astype(BF16)

        valid = bvalid_ref[i]

        def ffn(n_sub):
            blocks = [slice(r0 + k * sub, r0 + (k + 1) * sub) for k in range(n_sub)]
            row = lax.broadcasted_iota(jnp.int32, (sub, x_ref.shape[1]), 0)
            xb = [jnp.concatenate(_unpack_bf16_pairs(
                      jnp.where(row + (rs.start - r0) < valid, x_ref[rs, :], jnp.uint32(0))),
                      axis=1).astype(BF16) for rs in blocks]
            gate = [_dot(xb_i, wg_s[...]) for xb_i in xb]
            up = [_dot(xb_i, wu_s[...]) for xb_i in xb]
            hid = [((g_i * _sigmoid(g_i)) * u_i).astype(BF16) for g_i, u_i in zip(gate, up)]
            y = [_dot(h_i, wd_s[...]) for h_i in hid]
            for rs, y_i in zip(blocks, y):
                y_ref[rs, :] = _pack_bf16_pairs(y_i.astype(BF16))
            rest = rows - n_sub * sub
            if rest:
                y_ref[r0 + n_sub * sub:r0 + rows, :] = jnp.zeros((rest, y_ref.shape[1]), jnp.uint32)

        tail_sub = (rows // sub + 1) // 2
        pl.when((i < n_used) & (valid > tail_sub * sub))(lambda: ffn(rows // sub))
        pl.when((i < n_used) & (valid <= tail_sub * sub))(lambda: ffn(tail_sub))
        pl.when(i >= n_used)(lambda: ffn(0))

    for j in range(EXPERT_STEP_BLOCKS):
        block(pl.program_id(0) * EXPERT_STEP_BLOCKS + j, j * rows)


def _experts_call(block_expert, block_valid, n_used, x_rows, w_gate, w_up, w_down, rows):
    n_blocks = block_expert.shape[0]
    step_rows = EXPERT_STEP_BLOCKS * rows
    hbm = pl.BlockSpec(memory_space=pl.ANY)
    grid_spec = pltpu.PrefetchScalarGridSpec(
        num_scalar_prefetch=3,
        grid=(n_blocks // EXPERT_STEP_BLOCKS,),
        in_specs=[pl.BlockSpec((step_rows, D_MODEL // 2), lambda i, be, bv, nu: (i, 0)),
                  hbm, hbm, hbm],
        out_specs=pl.BlockSpec((step_rows, D_MODEL // 2), lambda i, be, bv, nu: (i, 0)),
        scratch_shapes=[pltpu.VMEM((D_MODEL, D_EXPERT), BF16),
                        pltpu.VMEM((D_MODEL, D_EXPERT), BF16),
                        pltpu.VMEM((D_EXPERT, D_MODEL), BF16),
                        pltpu.VMEM((2, D_MODEL, D_EXPERT), F32),
                        pltpu.VMEM((2, D_MODEL, D_EXPERT), F32),
                        pltpu.VMEM((2, D_EXPERT, D_MODEL), F32),
                        pltpu.SemaphoreType.DMA((2, 3)),
                        pltpu.SMEM((1,), jnp.int32)],
    )
    return pl.pallas_call(
        _experts_kernel,
        grid_spec=grid_spec,
        out_shape=jax.ShapeDtypeStruct((n_blocks * rows, D_MODEL // 2), jnp.uint32),
        compiler_params=pltpu.CompilerParams(
            dimension_semantics=("arbitrary",), vmem_limit_bytes=VMEM_LIMIT),
        name="experts",
    )(block_expert, block_valid, n_used, x_rows, w_gate, w_up, w_down)


def _combine_kernel(y1_ref, y2_ref, h_ref, route_ref, fw_ref, out_ref):
    route = route_ref[...]
    y1 = jnp.concatenate(_unpack_bf16_pairs(y1_ref[...]), axis=1)
    y2 = jnp.concatenate(_unpack_bf16_pairs(y2_ref[...]), axis=1)
    moe = route[:, ROUTE_W1:ROUTE_W1 + 1] * y1 + route[:, ROUTE_W2:ROUTE_W2 + 1] * y2
    h2 = h_ref[...] + moe
    out_ref[...] = h2 * lax.rsqrt(jnp.mean(h2 * h2, axis=-1, keepdims=True) + EPS) * fw_ref[...]


def _combine_call(y_tok, h1, route, final_w, rows):
    tokens = h1.shape[0]
    steps = tokens // rows
    tok = lambda width: pl.BlockSpec((rows, width), lambda i: (i, 0))
    return pl.pallas_call(
        _combine_kernel,
        grid=(steps,),
        in_specs=[tok(D_MODEL // 2), pl.BlockSpec((rows, D_MODEL // 2), lambda i: (i + steps, 0)),
                  tok(D_MODEL), tok(LANES), pl.BlockSpec((1, D_MODEL), lambda i: (0, 0))],
        out_specs=tok(D_MODEL),
        out_shape=jax.ShapeDtypeStruct((tokens, D_MODEL), F32),
        compiler_params=pltpu.CompilerParams(
            dimension_semantics=("arbitrary",), vmem_limit_bytes=VMEM_LIMIT),
        name="combine",
    )(y_tok, y_tok, h1, route, final_w)


def _dispatch_tables(assign, counts, tokens, rows):
    n_blocks = pl.cdiv((tokens * TOP_K + N_EXPERTS * (rows - 1)) // rows,
                       EXPERT_STEP_BLOCKS) * EXPERT_STEP_BLOCKS
    counts = counts[0, :N_EXPERTS].astype(jnp.int32)
    padded = (counts + rows - 1) // rows * rows
    pad_end = jnp.cumsum(padded)
    pad_start = pad_end - padded
    experts = assign[ROUTE_E1:ROUTE_E2 + 1]
    ranks = assign[ROUTE_RANK1:ROUTE_RANK2 + 1]
    is_expert = experts[..., None] == jnp.arange(N_EXPERTS, dtype=jnp.int32)
    dest = (jnp.sum(jnp.where(is_expert, pad_start, 0), axis=-1) + ranks).reshape(-1)
    block_start = jnp.arange(n_blocks, dtype=jnp.int32) * rows
    in_run = (block_start[:, None] >= pad_start[None, :]) & (block_start[:, None] < pad_end[None, :])
    lookup = lambda table: jnp.sum(jnp.where(in_run, table[None, :], 0), axis=1)
    block_expert = lookup(jnp.arange(N_EXPERTS, dtype=jnp.int32))
    block_valid = jnp.clip(lookup(pad_start + counts) - block_start, 0, rows)
    n_used = (pad_end[-1] // rows).astype(jnp.int32).reshape(1)
    return dest, block_expert, block_valid.astype(jnp.int32), n_used, n_blocks


def _prepare_weights(mix_norm_w, w_in, conv_mix_w, conv_mix_norm_w, qkv_conv_w, a_log, dt_bias,
                     gdn_norm_w, w_out, ffn_norm_w, w_group, b_group, w_router, b_router):
    pad_lanes = lambda v: jnp.pad(v.reshape(1, -1), ((0, 0), (0, LANES - v.size)))
    w_in_bf = w_in.astype(BF16)
    w_ab = jnp.pad(w_in_bf[:, Z_COL0 + GDN_W:], ((0, 0), (0, LANES - 2 * HEADS)))
    grp = jnp.arange(PROJ_COLS) // CONV_GROUP_W
    gmat = jnp.where(grp[:, None] == grp[None, :], 1.0 / CONV_GROUP_W, 0.0).astype(BF16)
    w_route = jnp.concatenate([w_group, w_router.reshape(D_MODEL, N_EXPERTS)], axis=1)
    w_route = jnp.pad(w_route, ((0, 0), (0, LANES - w_route.shape[1])))
    return dict(
        mix_norm_w=mix_norm_w.reshape(1, -1),
        w_a=jnp.concatenate([w_in_bf[:, part * CONV_CH + i * PROJ_COLS:][:, :PROJ_COLS]
                             for i in range(CONV_CH // PROJ_COLS) for part in range(3)], axis=1),
        w_in=w_in_bf,
        w_ab=w_ab,
        conv_mix_w=conv_mix_w,
        conv_mix_norm_w=conv_mix_norm_w.reshape(1, -1),
        gmat=gmat,
        qkv_conv_w=qkv_conv_w,
        a_log=pad_lanes(a_log),
        dt_bias=pad_lanes(dt_bias),
        gdn_norm_w=gdn_norm_w.reshape(1, -1),
        w_out_a=w_out[:CONV_CH].astype(BF16),
        w_out_b=w_out[CONV_CH:].astype(BF16),
        ffn_norm_w=ffn_norm_w.reshape(1, -1),
        w_route=w_route.astype(BF16),
        b_route=pad_lanes(jnp.concatenate([b_group, b_router.reshape(-1)])),
    )


def _tile(n, preferred):
    return preferred if n % preferred == 0 else n


def kernel(x, meta_tokens, mix_norm_w, w_in, conv_mix_w, conv_mix_norm_w, qkv_conv_w, a_log,
           dt_bias, gdn_norm_w, w_out, ffn_norm_w, w_group, b_group, w_router, b_router, w_gate,
           w_up, w_down, final_norm_w):
    assert mix_norm_w.shape[0] == 1, "single-layer kernel"
    batch, seq, _ = x.shape
    assert seq % CHUNK == 0
    w = _prepare_weights(mix_norm_w[0], w_in[0], conv_mix_w[0], conv_mix_norm_w[0], qkv_conv_w[0],
                         a_log[0], dt_bias[0], gdn_norm_w[0], w_out[0], ffn_norm_w[0], w_group[0],
                         b_group[0], w_router[0], b_router[0])

    prefix = jnp.concatenate([jnp.zeros((CHUNK - N_META, D_MODEL), x.dtype),
                              meta_tokens.astype(x.dtype)], axis=0)[None]
    zero_cu = jnp.zeros((HIST, CONV_CH), F32)
    zero_qkv = jnp.zeros((HIST, 3 * GDN_W), F32)
    _, pq, pk, pv, _, pgb, tail_cu, tail_qkv = _proj_call(prefix, zero_cu, zero_qkv, w, CHUNK)

    ya, q, k, v, z, gb, _, _ = _proj_call(x, tail_cu[0], tail_qkv[0], w, _tile(seq, PROJ_ROWS))
    o = _gdn_call(pq, pk, pv, pgb, q, k, v, gb, _tile(seq // CHUNK, GDN_CHUNKS))

    tokens = batch * seq
    flat = lambda a: a.reshape(tokens, a.shape[-1])
    h1, xn2, route, assign, counts = _mix_out_call(flat(x), flat(ya), flat(o), flat(z), w,
                                           _tile(tokens, MIX_ROWS))

    dest, block_expert, block_valid, n_used, n_blocks = _dispatch_tables(assign, counts, tokens,
                                                                         EXPERT_ROWS)
    x_rows = _sc_row_move(xn2, dest, n_blocks * EXPERT_ROWS, SC_DISPATCH_CHUNK, True,
                          "dispatch_scatter")
    y_rows = _experts_call(block_expert, block_valid, n_used, x_rows, w_gate[0], w_up[0],
                           w_down[0], EXPERT_ROWS)
    y_tok = _sc_row_move(y_rows, dest, TOP_K * tokens, SC_COMBINE_CHUNK, False, "combine_gather")
    out = _combine_call(y_tok, h1, route, final_norm_w.reshape(1, -1), _tile(tokens, COMBINE_ROWS))
    return out.reshape(batch, seq, D_MODEL)
```

```python
import functools

import jax
import jax.numpy as jnp
from jax import lax
from jax.experimental import pallas as pl
from jax.experimental.pallas import tpu as pltpu
from jax.experimental.pallas import tpu_sc as plsc

F32 = jnp.float32
BF16 = jnp.bfloat16
EPS = 1e-6

D_MODEL = 1024
N_META = 16
CONV_CH = 512
CONV_GROUP_W = 64
HEADS = 4
HEAD_DIM = 128
GDN_W = HEADS * HEAD_DIM
QKV_COL0 = 3 * CONV_CH
Z_COL0 = QKV_COL0 + 3 * GDN_W
CHUNK = 64
N_GROUPS = 4
EXPERTS_PER_GROUP = 8
N_EXPERTS = N_GROUPS * EXPERTS_PER_GROUP
TOP_K = 2
D_EXPERT = 512
LANES = 128
ROUTE_E1, ROUTE_E2, ROUTE_RANK1, ROUTE_RANK2, ROUTE_W1, ROUTE_W2 = range(6)
ASSIGN_ROWS = 8
HIST = 8

PROJ_ROWS = 512
PROJ_COLS = 256
PROJ_SUB_ROWS = 128
GDN_CHUNKS = 8
GDN_GROUP = 4
MIX_ROWS = 1024
MIX_SUB_ROWS = 128
EXPERT_ROWS = 256
EXPERT_SUB_ROWS = 128
EXPERT_STEP_BLOCKS = 4
COMBINE_ROWS = 1024
COMBINE_PARTS = 2
VMEM_LIMIT = 56 * 1024 * 1024
SC_CORES = 2
SC_SUBCORES = 16
SC_DISPATCH_CHUNK = 64
SC_COMBINE_CHUNK = 64


def _dot(a, b):
    return jnp.dot(a, b, preferred_element_type=F32)


def _dot_nt(a, b):
    return lax.dot_general(a, b, (((1,), (1,)), ((), ())), preferred_element_type=F32)


def _dot_tn(a, b):
    return lax.dot_general(a, b, (((0,), (0,)), ((), ())), preferred_element_type=F32)


def _split_bf16(x):
    hi = x.astype(BF16)
    lo = (x - hi.astype(F32)).astype(BF16)
    return hi, lo


def _sigmoid(x):
    return 1.0 / (1.0 + jnp.exp(-x))


def _pack_bf16_pairs(x_bf16):
    bits = pltpu.bitcast(x_bf16.astype(F32), jnp.uint32)
    n = x_bf16.shape[1] // 2
    return (bits[:, :n] >> 16) | (bits[:, n:] & jnp.uint32(0xFFFF0000))


def _unpack_bf16_pairs(packed):
    return (pltpu.bitcast(packed << 16, F32),
            pltpu.bitcast(packed & jnp.uint32(0xFFFF0000), F32))


def _proj_kernel(x_ref, hcu_ref, hqkv_ref, nw_ref, wa_ref, wi_ref, wab_ref, cmw_ref,
                 cmn_ref, gmat_ref, qcw_ref, alog_ref, dtb_ref,
                 ya_ref, q_ref, k_ref, v_ref, z_ref, gb_ref, tcu_ref, tqkv_ref,
                 cu_s, qkv_s, xn_s):
    rows = x_ref.shape[0]
    sub = PROJ_SUB_ROWS if rows % PROJ_SUB_ROWS == 0 else rows
    blocks = [slice(r, r + sub) for r in range(0, rows, sub)]

    @pl.when(pl.program_id(1) == 0)
    def _():
        cu_s[0:HIST, :] = hcu_ref[...]
        qkv_s[0:HIST, :] = hqkv_ref[...]

    x = x_ref[...]
    ms = jnp.mean(x * x, axis=-1, keepdims=True)
    xn_s[...] = (x * lax.rsqrt(ms + EPS) * nw_ref[...]).astype(BF16)

    def causal_conv(buf, cur, w_ref, cols, tail_ref):
        taps = w_ref.shape[0]
        acc = pltpu.roll(cur, taps - 1, axis=0) * w_ref[0:1, cols]
        for j in range(1, taps - 1):
            acc = acc + pltpu.roll(cur, taps - 1 - j, axis=0) * w_ref[j:j + 1, cols]
        acc = acc + cur * w_ref[taps - 1:taps, cols]
        buf[HIST:2 * HIST, cols] = cur[:HIST]
        seam = buf[pl.ds(HIST - taps + 1, HIST), cols] * w_ref[0:1, cols]
        for j in range(1, taps):
            seam = seam + buf[pl.ds(HIST - taps + 1 + j, HIST), cols] * w_ref[j:j + 1, cols]
        tail = cur[sub - HIST:]
        buf[0:HIST, cols] = tail
        tail_ref[:, cols] = tail
        return jnp.concatenate([seam, acc[HIST:]], axis=0)

    def mixer_a_tail(i, rs, pa):
        cols = slice(i * PROJ_COLS, (i + 1) * PROJ_COLS)
        cu = pa[:, PROJ_COLS:2 * PROJ_COLS] * pa[:, 2 * PROJ_COLS:]
        ya = pa[:, :PROJ_COLS] * causal_conv(cu_s, cu, cmw_ref, cols, tcu_ref)
        msg = _dot((ya * ya).astype(BF16), gmat_ref[...])
        ya_ref[rs, cols] = (ya * lax.rsqrt(msg + EPS) * cmn_ref[:, cols]).astype(BF16)

    heads_per_chunk = PROJ_COLS // HEAD_DIM

    def qkv_tail(i, rs, pq):
        cols = slice(i * PROJ_COLS, (i + 1) * PROJ_COLS)
        c = causal_conv(qkv_s, pq, qcw_ref, cols, tqkv_ref)
        c = c * _sigmoid(c)
        part, first_head = divmod(i * heads_per_chunk, HEADS)
        for j in range(heads_per_chunk):
            ch = c[:, j * HEAD_DIM:(j + 1) * HEAD_DIM]
            sl = slice((first_head + j) * HEAD_DIM, (first_head + j + 1) * HEAD_DIM)
            if part == 0:
                norm = lax.rsqrt(jnp.sum(ch * ch, axis=-1, keepdims=True) + EPS)
                q_ref[rs, sl] = ch * norm * (HEAD_DIM ** -0.5)
            elif part == 1:
                k_ref[rs, sl] = ch * lax.rsqrt(jnp.sum(ch * ch, axis=-1, keepdims=True) + EPS)
            else:
                v_ref[rs, sl] = ch

    def z_tail(rs, pz):
        z_ref[rs, :] = pz

    def decay_beta_tail(rs, ab):
        sp_in = ab + dtb_ref[...]
        softplus = jnp.maximum(sp_in, 0.0) + jnp.log1p(jnp.exp(-jnp.abs(sp_in)))
        g = -jnp.exp(alog_ref[...]) * softplus
        lane = lax.broadcasted_iota(jnp.int32, ab.shape, 1)
        gb_ref[rs, :] = jnp.where(lane < HEADS, g, jnp.where(lane < 2 * HEADS, _sigmoid(ab), 0.0))

    def matmul(w_ref, w_cols, rs):
        return _dot(xn_s[rs, :], w_ref[:, w_cols])

    stages = []
    for i in range(CONV_CH // PROJ_COLS):
        w_cols = slice(3 * i * PROJ_COLS, 3 * (i + 1) * PROJ_COLS)
        stages += [(functools.partial(matmul, wa_ref, w_cols, rs),
                    functools.partial(mixer_a_tail, i, rs)) for rs in blocks]
    for i in range(3 * GDN_W // PROJ_COLS):
        w_cols = slice(QKV_COL0 + i * PROJ_COLS, QKV_COL0 + (i + 1) * PROJ_COLS)
        stages += [(functools.partial(matmul, wi_ref, w_cols, rs),
                    functools.partial(qkv_tail, i, rs)) for rs in blocks]
    stages += [(functools.partial(matmul, wi_ref, slice(Z_COL0, Z_COL0 + GDN_W), rs),
                functools.partial(z_tail, rs)) for rs in blocks]
    stages += [(functools.partial(matmul, wab_ref, slice(None), rs),
                functools.partial(decay_beta_tail, rs)) for rs in blocks]
    pending = None
    for issue, tail in stages:
        res = issue()
        if pending is not None:
            pending()
        pending = functools.partial(tail, res)
    pending()


def _proj_call(x3, hist_cu, hist_qkv, w, rows):
    nb, seq, _ = x3.shape
    nt = seq // rows
    tok = lambda width: pl.BlockSpec((None, rows, width), lambda b, t: (b, t, 0))
    full = lambda a: pl.BlockSpec(a.shape, lambda b, t: (0,) * a.ndim,
                                  pipeline_mode=pl.Buffered(1))
    tail = lambda width: pl.BlockSpec((None, HIST, width), lambda b, t: (b, 0, 0))
    consts = (hist_cu, hist_qkv, w['mix_norm_w'], w['w_a'], w['w_in'], w['w_ab'],
              w['conv_mix_w'], w['conv_mix_norm_w'], w['gmat'], w['qkv_conv_w'], w['a_log'],
              w['dt_bias'])
    out_shape = (
        jax.ShapeDtypeStruct((nb, seq, CONV_CH), BF16),
        jax.ShapeDtypeStruct((nb, seq, GDN_W), F32),
        jax.ShapeDtypeStruct((nb, seq, GDN_W), F32),
        jax.ShapeDtypeStruct((nb, seq, GDN_W), F32),
        jax.ShapeDtypeStruct((nb, seq, GDN_W), F32),
        jax.ShapeDtypeStruct((nb, seq, LANES), F32),
        jax.ShapeDtypeStruct((nb, HIST, CONV_CH), F32),
        jax.ShapeDtypeStruct((nb, HIST, 3 * GDN_W), F32),
    )
    return pl.pallas_call(
        _proj_kernel,
        grid=(nb, nt),
        in_specs=[tok(D_MODEL)] + [full(a) for a in consts],
        out_specs=(tok(CONV_CH), tok(GDN_W), tok(GDN_W), tok(GDN_W), tok(GDN_W), tok(LANES),
                   tail(CONV_CH), tail(3 * GDN_W)),
        out_shape=out_shape,
        scratch_shapes=[pltpu.VMEM((2 * HIST, CONV_CH), F32),
                        pltpu.VMEM((2 * HIST, 3 * GDN_W), F32),
                        pltpu.VMEM((rows, D_MODEL), BF16)],
        compiler_params=pltpu.CompilerParams(
            dimension_semantics=("arbitrary", "arbitrary"), vmem_limit_bytes=VMEM_LIMIT),
        name="proj",
    )(x3, *consts)


def _chunk_masks():
    row = lax.broadcasted_iota(jnp.int32, (CHUNK, CHUNK), 0)
    col = lax.broadcasted_iota(jnp.int32, (CHUNK, CHUNK), 1)
    incl = row >= col
    strict = row > col
    levels = []
    n = 1
    while n < CHUNK:
        levels.append((row // (2 * n) == col // (2 * n)) & ((row // n) % 2 == 1) & ((col // n) % 2 == 0))
        n *= 2
    return incl, strict, levels


def _chunk_cumsum(gb_blk, incl):
    tri = incl.astype(BF16)
    hi, lo = _split_bf16(gb_blk)
    return _dot(tri, hi) + _dot(tri, lo)


def _chunk_transforms(chains, masks, state_only):
    incl, strict, levels = masks
    eye = (lax.broadcasted_iota(jnp.int32, (CHUNK, CHUNK), 0)
           == lax.broadcasted_iota(jnp.int32, (CHUNK, CHUNK), 1)).astype(F32)
    decay = [jnp.exp(jnp.where(incl, gc_col - gc_row, -jnp.inf))
             for (_, _, _, _, gc_col, gc_row, _) in chains]
    kb = [kh * beta for (_, kh, _, beta, _, _, _) in chains]
    k_bf = [kh.astype(BF16) for (_, kh, _, _, _, _, _) in chains]
    a_mat = [jnp.where(strict, _dot_nt(kb_i.astype(BF16), k_i) * d_i, 0.0)
             for kb_i, k_i, d_i in zip(kb, k_bf, decay)]
    t_inv = [eye - jnp.where(levels[0], a_i, 0.0) for a_i in a_mat]
    for lvl in levels[1:]:
        t_bf = [t_i.astype(BF16) for t_i in t_inv]
        m1 = [_dot(jnp.where(lvl, a_i, 0.0).astype(BF16), t_i) for a_i, t_i in zip(a_mat, t_bf)]
        t_inv = [t_i - _dot(tb_i, m_i.astype(BF16)) for t_i, tb_i, m_i in zip(t_inv, t_bf, m1)]
    rhs = [jnp.concatenate([vh * beta, kb_i * jnp.exp(gc_col)], axis=1)
           for (_, _, vh, beta, gc_col, _, _), kb_i in zip(chains, kb)]
    uw = [_dot(t_i.astype(BF16), r_i.astype(BF16)).astype(BF16)
          for t_i, r_i in zip(t_inv, rhs)]
    kd = [kh * jnp.exp(g_last - gc_col) for (_, kh, _, _, gc_col, _, g_last) in chains]
    pn = [_dot_tn(kd_i.astype(BF16), uw_i) for kd_i, uw_i in zip(kd, uw)]
    if state_only:
        return [pn_i[:, :HEAD_DIM] for pn_i in pn]
    intra = [jnp.where(incl, _dot_nt(qh.astype(BF16), k_i) * d_i, 0.0)
             for (qh, _, _, _, _, _, _), k_i, d_i in zip(chains, k_bf, decay)]
    iuw = [_dot(in_i.astype(BF16), uw_i) for in_i, uw_i in zip(intra, uw)]
    out = []
    for (qh, _, _, _, gc_col, _, g_last), pn_i, iuw_i in zip(chains, pn, iuw):
        q_part = qh * jnp.exp(gc_col) - iuw_i[:, HEAD_DIM:]
        out.append((q_part, pn_i[:, HEAD_DIM:], iuw_i[:, :HEAD_DIM], pn_i[:, :HEAD_DIM],
                    jnp.exp(g_last)))
    return out


def _gdn_kernel(pq_ref, pk_ref, pv_ref, pgb_ref, q_ref, k_ref, v_ref, gb_ref, o_ref,
                s_s, qp_s, op_s, n_s, a_s):
    nb = q_ref.shape[0]
    n_chunks = q_ref.shape[1] // CHUNK
    group = GDN_GROUP if n_chunks % GDN_GROUP == 0 else 1
    masks = _chunk_masks()

    def chains_of(gb_blk, q_blk, k_blk, v_blk):
        gc = _chunk_cumsum(gb_blk, masks[0])
        gc_t = gc.T
        res = []
        for h in range(HEADS):
            sl = slice(h * HEAD_DIM, (h + 1) * HEAD_DIM)
            res.append((q_blk(sl), k_blk(sl), v_blk(sl), gb_blk[:, HEADS + h:HEADS + h + 1],
                        gc[:, h:h + 1], gc_t[h:h + 1, :CHUNK], gc[CHUNK - 1:CHUNK, h:h + 1]))
        return res

    @pl.when(pl.program_id(0) == 0)
    def _():
        chains = chains_of(pgb_ref[...], lambda sl: pq_ref[:, sl], lambda sl: pk_ref[:, sl],
                           lambda sl: pv_ref[:, sl])
        for h, n_mat in enumerate(_chunk_transforms(chains, masks, True)):
            for b in range(nb):
                s_s[b * HEADS + h] = n_mat

    def transform_group(gi, carry):
        chains, where = [], []
        for cc in range(group):
            c = gi * group + cc
            rows = pl.ds(pl.multiple_of(c * CHUNK, CHUNK), CHUNK)
            for b in range(nb):
                chains += chains_of(gb_ref[b, rows, :], lambda sl: q_ref[b, rows, sl],
                                    lambda sl: k_ref[b, rows, sl], lambda sl: v_ref[b, rows, sl])
                where += [(c, b * HEADS + h) for h in range(HEADS)]
        for (c, ch), (q_part, p_mat, o_part, n_mat, a) in zip(
                where, _chunk_transforms(chains, masks, False)):
            qp_s[c, ch, 0:CHUNK, :] = q_part.astype(BF16)
            qp_s[c, ch, CHUNK:, :] = p_mat.astype(BF16)
            op_s[c, ch] = o_part
            n_s[c, ch] = n_mat
            a_s[c, ch] = jnp.broadcast_to(a, (8, HEAD_DIM))
        return carry

    lax.fori_loop(0, n_chunks // group, transform_group, 0)

    def scan_chunk(c, carry):
        r0 = pl.multiple_of(c * CHUNK, CHUNK)
        for b in range(nb):
            for h in range(HEADS):
                ch = b * HEADS + h
                s = s_s[ch]
                r = _dot(qp_s[c, ch], s.astype(BF16))
                o_ref[b, pl.ds(r0, CHUNK), h * HEAD_DIM:(h + 1) * HEAD_DIM] = r[:CHUNK] + op_s[c, ch]
                s_s[ch] = a_s[c, ch][0:1, :] * s - r[CHUNK:] + n_s[c, ch]
        return carry

    lax.fori_loop(0, n_chunks, scan_chunk, 0)


def _gdn_call(pq, pk, pv, pgb, q, k, v, gb, chunks_per_step):
    nb, seq, _ = q.shape
    rows = chunks_per_step * CHUNK
    steps = seq // rows
    tok = lambda width: pl.BlockSpec((nb, rows, width), lambda i: (0, i, 0))
    pre = lambda width: pl.BlockSpec((None, CHUNK, width), lambda i: (0, 0, 0))
    nch = nb * HEADS
    return pl.pallas_call(
        _gdn_kernel,
        grid=(steps,),
        in_specs=[pre(GDN_W), pre(GDN_W), pre(GDN_W), pre(LANES),
                  tok(GDN_W), tok(GDN_W), tok(GDN_W), tok(LANES)],
        out_specs=tok(GDN_W),
        out_shape=jax.ShapeDtypeStruct((nb, seq, GDN_W), F32),
        scratch_shapes=[
            pltpu.VMEM((nch, HEAD_DIM, HEAD_DIM), F32),
            pltpu.VMEM((chunks_per_step, nch, CHUNK + HEAD_DIM, HEAD_DIM), BF16),
            pltpu.VMEM((chunks_per_step, nch, CHUNK, HEAD_DIM), F32),
            pltpu.VMEM((chunks_per_step, nch, HEAD_DIM, HEAD_DIM), F32),
            pltpu.VMEM((chunks_per_step, nch, 8, HEAD_DIM), F32),
        ],
        compiler_params=pltpu.CompilerParams(
            dimension_semantics=("arbitrary",), vmem_limit_bytes=VMEM_LIMIT),
        name="gdn",
    )(pq, pk, pv, pgb, q, k, v, gb)


def _mix_out_kernel(x_ref, ya_ref, o_ref, z_ref, gnw_ref, woa_ref, wob_ref, fnw_ref, wr_ref,
                    br_ref, h_ref, xn_ref, route_ref, assign_ref, counts_ref, cnt_s):
    rows = x_ref.shape[0]
    sub = MIX_SUB_ROWS if rows % MIX_SUB_ROWS == 0 else rows
    blocks = [slice(r, r + sub) for r in range(0, rows, sub)]

    def gated_heads(rs):
        yb = []
        for h in range(HEADS):
            sl = slice(h * HEAD_DIM, (h + 1) * HEAD_DIM)
            oh = o_ref[rs, sl]
            zh = z_ref[rs, sl]
            on = oh * lax.rsqrt(jnp.mean(oh * oh, axis=-1, keepdims=True) + EPS) * gnw_ref[...]
            yb.append((on * (zh * _sigmoid(zh))).astype(BF16))
        return jnp.concatenate(yb, axis=1)

    yb = [gated_heads(rs) for rs in blocks]
    h1 = [x_ref[rs, :] + (_dot(ya_ref[rs, :], woa_ref[...]) + _dot(yb_i, wob_ref[...]))
          for rs, yb_i in zip(blocks, yb)]
    for rs, h1_i in zip(blocks, h1):
        h_ref[rs, :] = h1_i
    xn = [h1_i * lax.rsqrt(jnp.mean(h1_i * h1_i, axis=-1, keepdims=True) + EPS) * fnw_ref[...]
          for h1_i in h1]
    xn_bf = [xn_i.astype(BF16) for xn_i in xn]
    for rs, xn_i in zip(blocks, xn_bf):
        xn_ref[rs, :] = _pack_bf16_pairs(xn_i)
    logits = [_dot(xn_i, wr_ref[...]) + br_ref[...] for xn_i in xn_bf]

    lane = lax.broadcasted_iota(jnp.int32, (sub, LANES), 1).astype(F32)
    neg = -jnp.inf
    big = float(1 << 20)

    def argmax_first(vals):
        m = jnp.max(vals, axis=-1, keepdims=True)
        idx = jnp.min(jnp.where(vals == m, lane, big), axis=-1, keepdims=True)
        return m, idx

    def top_k(lg):
        grp = jnp.where(lane < N_GROUPS, lg, neg)
        g_max, g_sel = argmax_first(grp)
        p_grp = 1.0 / jnp.sum(jnp.exp(grp - g_max), axis=-1, keepdims=True)
        lo_lane = N_GROUPS + g_sel * EXPERTS_PER_GROUP
        ex = jnp.where((lane >= lo_lane) & (lane < lo_lane + EXPERTS_PER_GROUP), lg, neg)
        m1, i1 = argmax_first(ex)
        m2, i2 = argmax_first(jnp.where(lane == i1, neg, ex))
        e2 = jnp.exp(m2 - m1)
        return i1 - N_GROUPS, i2 - N_GROUPS, 1.0 / (1.0 + e2) * p_grp, e2 / (1.0 + e2) * p_grp

    picks = [top_k(lg) for lg in logits]

    @pl.when(pl.program_id(0) == 0)
    def _():
        cnt_s[...] = jnp.zeros(cnt_s.shape, F32)

    onehots = [((lane == e1).astype(F32), (lane == e2).astype(F32)) for e1, e2, _, _ in picks]
    earlier = (lax.broadcasted_iota(jnp.int32, (sub, sub), 0)
               > lax.broadcasted_iota(jnp.int32, (sub, sub), 1)).astype(BF16)
    within = [_dot(earlier, (oh1 + oh2).astype(BF16)) for oh1, oh2 in onehots]
    counts = cnt_s[...]
    for rs, (e1, e2, w1, w2), (oh1, oh2), within_i in zip(blocks, picks, onehots, within):
        before = within_i + counts
        rank1 = jnp.sum(before * oh1, axis=-1, keepdims=True)
        rank2 = jnp.sum(before * oh2, axis=-1, keepdims=True)
        counts = counts + jnp.sum(oh1 + oh2, axis=0, keepdims=True)
        route = jnp.zeros((sub, LANES), F32)
        for k, val in ((ROUTE_E1, e1), (ROUTE_E2, e2), (ROUTE_W1, w1), (ROUTE_W2, w2),
                       (ROUTE_RANK1, rank1), (ROUTE_RANK2, rank2)):
            route = jnp.where(lane == k, val, route)
        route_ref[rs, :] = route
        assign_ref[:, rs] = route.T[:ASSIGN_ROWS].astype(jnp.int32)
    cnt_s[...] = counts
    counts_ref[...] = counts


def _mix_out_call(x2, ya, o, z, w, rows):
    tokens = x2.shape[0]
    tok = lambda width: pl.BlockSpec((rows, width), lambda i: (i, 0))
    full = lambda a: pl.BlockSpec(a.shape, lambda i: (0,) * a.ndim)
    consts = (w['gdn_norm_w'], w['w_out_a'], w['w_out_b'], w['ffn_norm_w'], w['w_route'],
              w['b_route'])
    return pl.pallas_call(
        _mix_out_kernel,
        grid=(tokens // rows,),
        in_specs=[tok(D_MODEL), tok(CONV_CH), tok(GDN_W), tok(GDN_W)] + [full(a) for a in consts],
        out_specs=(tok(D_MODEL), tok(D_MODEL // 2), tok(LANES),
                   pl.BlockSpec((ASSIGN_ROWS, rows), lambda i: (0, i)),
                   pl.BlockSpec((1, LANES), lambda i: (0, 0))),
        out_shape=(jax.ShapeDtypeStruct((tokens, D_MODEL), F32),
                   jax.ShapeDtypeStruct((tokens, D_MODEL // 2), jnp.uint32),
                   jax.ShapeDtypeStruct((tokens, LANES), F32),
                   jax.ShapeDtypeStruct((ASSIGN_ROWS, tokens), jnp.int32),
                   jax.ShapeDtypeStruct((1, LANES), F32)),
        scratch_shapes=[pltpu.VMEM((1, LANES), F32)],
        compiler_params=pltpu.CompilerParams(
            dimension_semantics=("arbitrary",), vmem_limit_bytes=VMEM_LIMIT),
        name="mix_out",
    )(x2, ya, o, z, *consts)


def _sc_row_move(table, idx, out_rows, chunk, scatter, name):
    n = idx.shape[0]
    n_src, width = table.shape
    workers = SC_CORES * SC_SUBCORES
    per_w = n // workers
    assert n % workers == 0 and per_w % (2 * chunk) == 0
    assert n_src % per_w == 0 or not scatter
    pairs = per_w // (2 * chunk)
    mesh = plsc.VectorSubcoreMesh(core_axis_name="c", subcore_axis_name="s",
                                  num_cores=SC_CORES, num_subcores=SC_SUBCORES)

    def body(table_hbm, idx_hbm, out_hbm, idx_v, buf_a, buf_b, sem_ra, sem_rb, sem_wa, sem_wb):
        base = (lax.axis_index("s") * SC_CORES + lax.axis_index("c")) * per_w
        src_base = lax.rem(base, n_src)
        pltpu.sync_copy(idx_hbm.at[pl.ds(base, per_w)], idx_v)

        def read(c, buf, sem):
            off = pl.multiple_of(c * chunk, chunk)
            src = (table_hbm.at[pl.ds(src_base + off, chunk)] if scatter
                   else table_hbm.at[idx_v.at[pl.ds(off, chunk)]])
            return pltpu.make_async_copy(src, buf, sem)

        def write(c, buf, sem):
            off = pl.multiple_of(c * chunk, chunk)
            dst = (out_hbm.at[idx_v.at[pl.ds(off, chunk)]] if scatter
                   else out_hbm.at[pl.ds(base + off, chunk)])
            return pltpu.make_async_copy(buf, dst, sem)

        read(0, buf_a, sem_ra).start()

        @pl.loop(0, pairs)
        def _(j):
            ca = 2 * j
            cb = ca + 1
            read(cb, buf_b, sem_rb).start()
            read(ca, buf_a, sem_ra).wait()
            write(ca, buf_a, sem_wa).start()
            read(cb, buf_b, sem_rb).wait()
            write(cb, buf_b, sem_wb).start()
            write(ca, buf_a, sem_wa).wait()

            @pl.when(j + 1 < pairs)
            def _():
                read(ca + 2, buf_a, sem_ra).start()

            write(cb, buf_b, sem_wb).wait()

    return pl.kernel(
        body,
        out_type=jax.ShapeDtypeStruct((out_rows, width), table.dtype),
        mesh=mesh,
        scratch_types=[pltpu.VMEM((per_w,), jnp.int32),
                       pltpu.VMEM((chunk, width), table.dtype),
                       pltpu.VMEM((chunk, width), table.dtype),
                       pltpu.SemaphoreType.DMA, pltpu.SemaphoreType.DMA,
                       pltpu.SemaphoreType.DMA, pltpu.SemaphoreType.DMA],
        name=name,
    )(table, idx)


def _experts_kernel(bexp_ref, bvalid_ref, nused_ref, x_ref, wg_hbm, wu_hbm, wd_hbm, y_ref,
                    wg_s, wu_s, wd_s, stage_g, stage_u, stage_d, sems, seq_s):
    n_used = nused_ref[0]
    n_last = bexp_ref.shape[0] - 1
    rows = x_ref.shape[0] // EXPERT_STEP_BLOCKS
    sub = EXPERT_SUB_ROWS if rows % EXPERT_SUB_ROWS == 0 else rows

    def weight_copies(e, slot):
        return (pltpu.make_async_copy(wg_hbm.at[e], stage_g.at[slot], sems.at[slot, 0]),
                pltpu.make_async_copy(wu_hbm.at[e], stage_u.at[slot], sems.at[slot, 1]),
                pltpu.make_async_copy(wd_hbm.at[e], stage_d.at[slot], sems.at[slot, 2]))

    def block(i, r0):
        expert = bexp_ref[i]
        first_of_expert = (i == 0) | (expert != bexp_ref[jnp.maximum(i - 1, 0)])

        @pl.when((i == 0) & (n_used > 0))
        def _():
            seq_s[0] = 0
            for copy in weight_copies(expert, 0):
                copy.start()

        @pl.when((i < n_used) & first_of_expert)
        def _():
            @pl.when(i > 0)
            def _():
                seq_s[0] = seq_s[0] + 1

            slot = seq_s[0] % 2
            nxt = lax.while_loop(
                lambda j: (j < n_used) & (bexp_ref[jnp.minimum(j, n_last)] == expert),
                lambda j: j + 1, i + 1)

            @pl.when(nxt < n_used)
            def _():
                for copy in weight_copies(bexp_ref[jnp.minimum(nxt, n_last)], 1 - slot):
                    copy.start()

            for copy in weight_copies(expert, slot):
                copy.wait()
            wg_s[...] = stage_g[slot].astype(BF16)
            wu_s[...] = stage_u[slot].astype(BF16)
            wd_s[...] = stage_d[slot].astype(BF16)

        valid = bvalid_ref[i]

        def ffn(n_sub):
            blocks = [slice(r0 + k * sub, r0 + (k + 1) * sub) for k in range(n_sub)]
            row = lax.broadcasted_iota(jnp.int32, (sub, x_ref.shape[1]), 0)
            xb = [jnp.concatenate(_unpack_bf16_pairs(
                      jnp.where(row + (rs.start - r0) < valid, x_ref[rs, :], jnp.uint32(0))),
                      axis=1).astype(BF16) for rs in blocks]
            gate = [_dot(xb_i, wg_s[...]) for xb_i in xb]
            up = [_dot(xb_i, wu_s[...]) for xb_i in xb]
            hid = [((g_i * _sigmoid(g_i)) * u_i).astype(BF16) for g_i, u_i in zip(gate, up)]
            y = [_dot(h_i, wd_s[...]) for h_i in hid]
            for rs, y_i in zip(blocks, y):
                y_ref[rs, :] = _pack_bf16_pairs(y_i.astype(BF16))
            rest = rows - n_sub * sub
            if rest:
                y_ref[r0 + n_sub * sub:r0 + rows, :] = jnp.zeros((rest, y_ref.shape[1]), jnp.uint32)

        tail_sub = (rows // sub + 1) // 2
        pl.when((i < n_used) & (valid > tail_sub * sub))(lambda: ffn(rows // sub))
        pl.when((i < n_used) & (valid <= tail_sub * sub))(lambda: ffn(tail_sub))
        pl.when(i >= n_used)(lambda: ffn(0))

    for j in range(EXPERT_STEP_BLOCKS):
        block(pl.program_id(0) * EXPERT_STEP_BLOCKS + j, j * rows)


def _experts_call(block_expert, block_valid, n_used, x_rows, w_gate, w_up, w_down, rows):
    n_blocks = block_expert.shape[0]
    step_rows = EXPERT_STEP_BLOCKS * rows
    hbm = pl.BlockSpec(memory_space=pl.ANY)
    grid_spec = pltpu.PrefetchScalarGridSpec(
        num_scalar_prefetch=3,
        grid=(n_blocks // EXPERT_STEP_BLOCKS,),
        in_specs=[pl.BlockSpec((step_rows, D_MODEL // 2), lambda i, be, bv, nu: (i, 0)),
                  hbm, hbm, hbm],
        out_specs=pl.BlockSpec((step_rows, D_MODEL // 2), lambda i, be, bv, nu: (i, 0)),
        scratch_shapes=[pltpu.VMEM((D_MODEL, D_EXPERT), BF16),
                        pltpu.VMEM((D_MODEL, D_EXPERT), BF16),
                        pltpu.VMEM((D_EXPERT, D_MODEL), BF16),
                        pltpu.VMEM((2, D_MODEL, D_EXPERT), F32),
                        pltpu.VMEM((2, D_MODEL, D_EXPERT), F32),
                        pltpu.VMEM((2, D_EXPERT, D_MODEL), F32),
                        pltpu.SemaphoreType.DMA((2, 3)),
                        pltpu.SMEM((1,), jnp.int32)],
    )
    return pl.pallas_call(
        _experts_kernel,
        grid_spec=grid_spec,
        out_shape=jax.ShapeDtypeStruct((n_blocks * rows, D_MODEL // 2), jnp.uint32),
        compiler_params=pltpu.CompilerParams(
            dimension_semantics=("arbitrary",), vmem_limit_bytes=VMEM_LIMIT),
        name="experts",
    )(block_expert, block_valid, n_used, x_rows, w_gate, w_up, w_down)


def _combine_kernel(y1_ref, y2_ref, h_ref, route_ref, fw_ref, *rest):
    out_ref = rest[-1]
    route = route_ref[...]
    y1 = jnp.concatenate(_unpack_bf16_pairs(y1_ref[...]), axis=1)
    y2 = jnp.concatenate(_unpack_bf16_pairs(y2_ref[...]), axis=1)
    moe = route[:, ROUTE_W1:ROUTE_W1 + 1] * y1 + route[:, ROUTE_W2:ROUTE_W2 + 1] * y2
    h2 = h_ref[...] + moe
    out_ref[...] = h2 * lax.rsqrt(jnp.mean(h2 * h2, axis=-1, keepdims=True) + EPS) * fw_ref[...]


def _combine_call(y_part, h1, route, final_w, rows, part, n_parts, out_so_far):
    tokens = h1.shape[0]
    steps = tokens // n_parts // rows
    first = part * steps
    tok = lambda width: pl.BlockSpec((rows, width), lambda i: (i + first, 0))
    in_specs = [pl.BlockSpec((rows, D_MODEL // 2), lambda i: (i, 0)),
                pl.BlockSpec((rows, D_MODEL // 2), lambda i: (i + steps, 0)),
                tok(D_MODEL), tok(LANES), pl.BlockSpec((1, D_MODEL), lambda i: (0, 0))]
    args = [y_part, y_part, h1, route, final_w]
    aliases = {}
    if out_so_far is not None:
        in_specs.append(pl.BlockSpec(memory_space=pl.ANY))
        aliases = {len(args): 0}
        args.append(out_so_far)
    return pl.pallas_call(
        _combine_kernel,
        grid=(steps,),
        in_specs=in_specs,
        out_specs=tok(D_MODEL),
        out_shape=jax.ShapeDtypeStruct((tokens, D_MODEL), F32),
        input_output_aliases=aliases,
        compiler_params=pltpu.CompilerParams(
            dimension_semantics=("arbitrary",), vmem_limit_bytes=VMEM_LIMIT),
        name="combine",
    )(*args)


def _dispatch_tables(assign, counts, tokens, rows):
    n_blocks = pl.cdiv((tokens * TOP_K + N_EXPERTS * (rows - 1)) // rows,
                       EXPERT_STEP_BLOCKS) * EXPERT_STEP_BLOCKS
    counts = counts[0, :N_EXPERTS].astype(jnp.int32)
    padded = (counts + rows - 1) // rows * rows
    pad_end = jnp.cumsum(padded)
    pad_start = pad_end - padded
    experts = assign[ROUTE_E1:ROUTE_E2 + 1]
    ranks = assign[ROUTE_RANK1:ROUTE_RANK2 + 1]
    is_expert = experts[..., None] == jnp.arange(N_EXPERTS, dtype=jnp.int32)
    dest = (jnp.sum(jnp.where(is_expert, pad_start, 0), axis=-1) + ranks).reshape(-1)
    block_start = jnp.arange(n_blocks, dtype=jnp.int32) * rows
    in_run = (block_start[:, None] >= pad_start[None, :]) & (block_start[:, None] < pad_end[None, :])
    lookup = lambda table: jnp.sum(jnp.where(in_run, table[None, :], 0), axis=1)
    block_expert = lookup(jnp.arange(N_EXPERTS, dtype=jnp.int32))
    block_valid = jnp.clip(lookup(pad_start + counts) - block_start, 0, rows)
    n_used = (pad_end[-1] // rows).astype(jnp.int32).reshape(1)
    return dest, block_expert, block_valid.astype(jnp.int32), n_used, n_blocks


def _prepare_weights(mix_norm_w, w_in, conv_mix_w, conv_mix_norm_w, qkv_conv_w, a_log, dt_bias,
                     gdn_norm_w, w_out, ffn_norm_w, w_group, b_group, w_router, b_router):
    pad_lanes = lambda v: jnp.pad(v.reshape(1, -1), ((0, 0), (0, LANES - v.size)))
    w_in_bf = w_in.astype(BF16)
    w_ab = jnp.pad(w_in_bf[:, Z_COL0 + GDN_W:], ((0, 0), (0, LANES - 2 * HEADS)))
    grp = jnp.arange(PROJ_COLS) // CONV_GROUP_W
    gmat = jnp.where(grp[:, None] == grp[None, :], 1.0 / CONV_GROUP_W, 0.0).astype(BF16)
    w_route = jnp.concatenate([w_group, w_router.reshape(D_MODEL, N_EXPERTS)], axis=1)
    w_route = jnp.pad(w_route, ((0, 0), (0, LANES - w_route.shape[1])))
    return dict(
        mix_norm_w=mix_norm_w.reshape(1, -1),
        w_a=jnp.concatenate([w_in_bf[:, part * CONV_CH + i * PROJ_COLS:][:, :PROJ_COLS]
                             for i in range(CONV_CH // PROJ_COLS) for part in range(3)], axis=1),
        w_in=w_in_bf,
        w_ab=w_ab,
        conv_mix_w=conv_mix_w,
        conv_mix_norm_w=conv_mix_norm_w.reshape(1, -1),
        gmat=gmat,
        qkv_conv_w=qkv_conv_w,
        a_log=pad_lanes(a_log),
        dt_bias=pad_lanes(dt_bias),
        gdn_norm_w=gdn_norm_w.reshape(1, -1),
        w_out_a=w_out[:CONV_CH].astype(BF16),
        w_out_b=w_out[CONV_CH:].astype(BF16),
        ffn_norm_w=ffn_norm_w.reshape(1, -1),
        w_route=w_route.astype(BF16),
        b_route=pad_lanes(jnp.concatenate([b_group, b_router.reshape(-1)])),
    )


def _tile(n, preferred):
    return preferred if n % preferred == 0 else n


def kernel(x, meta_tokens, mix_norm_w, w_in, conv_mix_w, conv_mix_norm_w, qkv_conv_w, a_log,
           dt_bias, gdn_norm_w, w_out, ffn_norm_w, w_group, b_group, w_router, b_router, w_gate,
           w_up, w_down, final_norm_w):
    assert mix_norm_w.shape[0] == 1, "single-layer kernel"
    batch, seq, _ = x.shape
    assert seq % CHUNK == 0
    w = _prepare_weights(mix_norm_w[0], w_in[0], conv_mix_w[0], conv_mix_norm_w[0], qkv_conv_w[0],
                         a_log[0], dt_bias[0], gdn_norm_w[0], w_out[0], ffn_norm_w[0], w_group[0],
                         b_group[0], w_router[0], b_router[0])

    prefix = jnp.concatenate([jnp.zeros((CHUNK - N_META, D_MODEL), x.dtype),
                              meta_tokens.astype(x.dtype)], axis=0)[None]
    zero_cu = jnp.zeros((HIST, CONV_CH), F32)
    zero_qkv = jnp.zeros((HIST, 3 * GDN_W), F32)
    _, pq, pk, pv, _, pgb, tail_cu, tail_qkv = _proj_call(prefix, zero_cu, zero_qkv, w, CHUNK)

    ya, q, k, v, z, gb, _, _ = _proj_call(x, tail_cu[0], tail_qkv[0], w, _tile(seq, PROJ_ROWS))
    o = _gdn_call(pq, pk, pv, pgb, q, k, v, gb, _tile(seq // CHUNK, GDN_CHUNKS))

    tokens = batch * seq
    flat = lambda a: a.reshape(tokens, a.shape[-1])
    h1, xn2, route, assign, counts = _mix_out_call(flat(x), flat(ya), flat(o), flat(z), w,
                                           _tile(tokens, MIX_ROWS))

    dest, block_expert, block_valid, n_used, n_blocks = _dispatch_tables(assign, counts, tokens,
                                                                         EXPERT_ROWS)
    x_rows = _sc_row_move(xn2, dest, n_blocks * EXPERT_ROWS, SC_DISPATCH_CHUNK, True,
                          "dispatch_scatter")
    y_rows = _experts_call(block_expert, block_valid, n_used, x_rows, w_gate[0], w_up[0],
                           w_down[0], EXPERT_ROWS)
    part_tokens = tokens // COMBINE_PARTS
    dest_parts = dest.reshape(TOP_K, COMBINE_PARTS, part_tokens)
    out = None
    for part in range(COMBINE_PARTS):
        y_part = _sc_row_move(y_rows, dest_parts[:, part].reshape(-1), TOP_K * part_tokens,
                              SC_COMBINE_CHUNK, False, "combine_gather")
        out = _combine_call(y_part, h1, route, final_norm_w.reshape(1, -1),
                            _tile(part_tokens, COMBINE_ROWS), part, COMBINE_PARTS, out)
    return out.reshape(batch, seq, D_MODEL)
```

```python
import functools

import jax
import jax.numpy as jnp
from jax import lax
from jax.experimental import pallas as pl
from jax.experimental.pallas import tpu as pltpu
from jax.experimental.pallas import tpu_sc as plsc

F32 = jnp.float32
BF16 = jnp.bfloat16
EPS = 1e-6

D_MODEL = 1024
N_META = 16
CONV_CH = 512
CONV_GROUP_W = 64
HEADS = 4
HEAD_DIM = 128
GDN_W = HEADS * HEAD_DIM
QKV_COL0 = 3 * CONV_CH
Z_COL0 = QKV_COL0 + 3 * GDN_W
CHUNK = 64
N_GROUPS = 4
EXPERTS_PER_GROUP = 8
N_EXPERTS = N_GROUPS * EXPERTS_PER_GROUP
TOP_K = 2
D_EXPERT = 512
LANES = 128
ROUTE_E1, ROUTE_E2, ROUTE_RANK1, ROUTE_RANK2, ROUTE_W1, ROUTE_W2 = range(6)
ASSIGN_ROWS = 8
HIST = 8

PROJ_ROWS = 512
PROJ_COLS = 256
PROJ_SUB_ROWS = 128
GDN_CHUNKS = 8
GDN_GROUP = 4
MIX_ROWS = 1024
MIX_SUB_ROWS = 128
EXPERT_ROWS = 256
EXPERT_SUB_ROWS = 128
EXPERT_STEP_BLOCKS = 4
COMBINE_ROWS = 1024
VMEM_LIMIT = 56 * 1024 * 1024
SC_CORES = 2
SC_SUBCORES = 16
SC_DISPATCH_CHUNK = 64
SC_COMBINE_CHUNK = 64


def _dot(a, b):
    return jnp.dot(a, b, preferred_element_type=F32)


def _dot_nt(a, b):
    return lax.dot_general(a, b, (((1,), (1,)), ((), ())), preferred_element_type=F32)


def _dot_tn(a, b):
    return lax.dot_general(a, b, (((0,), (0,)), ((), ())), preferred_element_type=F32)


def _split_bf16(x):
    hi = x.astype(BF16)
    lo = (x - hi.astype(F32)).astype(BF16)
    return hi, lo


def _sigmoid(x):
    return 1.0 / (1.0 + jnp.exp(-x))


def _pack_bf16_pairs(x_bf16):
    bits = pltpu.bitcast(x_bf16.astype(F32), jnp.uint32)
    n = x_bf16.shape[1] // 2
    return (bits[:, :n] >> 16) | (bits[:, n:] & jnp.uint32(0xFFFF0000))


def _unpack_bf16_pairs(packed):
    return (pltpu.bitcast(packed << 16, F32),
            pltpu.bitcast(packed & jnp.uint32(0xFFFF0000), F32))


def _proj_kernel(x_ref, hcu_ref, hqkv_ref, nw_ref, wa_ref, wi_ref, wab_ref, cmw_ref,
                 cmn_ref, gmat_ref, qcw_ref, alog_ref, dtb_ref,
                 ya_ref, q_ref, k_ref, v_ref, z_ref, gb_ref, tcu_ref, tqkv_ref,
                 cu_s, qkv_s, xn_s):
    rows = x_ref.shape[0]
    sub = PROJ_SUB_ROWS if rows % PROJ_SUB_ROWS == 0 else rows
    blocks = [slice(r, r + sub) for r in range(0, rows, sub)]

    @pl.when(pl.program_id(1) == 0)
    def _():
        cu_s[0:HIST, :] = hcu_ref[...]
        qkv_s[0:HIST, :] = hqkv_ref[...]

    x = x_ref[...]
    ms = jnp.mean(x * x, axis=-1, keepdims=True)
    xn_s[...] = (x * lax.rsqrt(ms + EPS) * nw_ref[...]).astype(BF16)

    def causal_conv(buf, cur, w_ref, cols, tail_ref):
        taps = w_ref.shape[0]
        acc = pltpu.roll(cur, taps - 1, axis=0) * w_ref[0:1, cols]
        for j in range(1, taps - 1):
            acc = acc + pltpu.roll(cur, taps - 1 - j, axis=0) * w_ref[j:j + 1, cols]
        acc = acc + cur * w_ref[taps - 1:taps, cols]
        buf[HIST:2 * HIST, cols] = cur[:HIST]
        seam = buf[pl.ds(HIST - taps + 1, HIST), cols] * w_ref[0:1, cols]
        for j in range(1, taps):
            seam = seam + buf[pl.ds(HIST - taps + 1 + j, HIST), cols] * w_ref[j:j + 1, cols]
        tail = cur[sub - HIST:]
        buf[0:HIST, cols] = tail
        tail_ref[:, cols] = tail
        return jnp.concatenate([seam, acc[HIST:]], axis=0)

    def mixer_a_tail(i, rs, pa):
        cols = slice(i * PROJ_COLS, (i + 1) * PROJ_COLS)
        cu = pa[:, PROJ_COLS:2 * PROJ_COLS] * pa[:, 2 * PROJ_COLS:]
        ya = pa[:, :PROJ_COLS] * causal_conv(cu_s, cu, cmw_ref, cols, tcu_ref)
        msg = _dot((ya * ya).astype(BF16), gmat_ref[...])
        ya_ref[rs, cols] = (ya * lax.rsqrt(msg + EPS) * cmn_ref[:, cols]).astype(BF16)

    heads_per_chunk = PROJ_COLS // HEAD_DIM

    def qkv_tail(i, rs, pq):
        cols = slice(i * PROJ_COLS, (i + 1) * PROJ_COLS)
        c = causal_conv(qkv_s, pq, qcw_ref, cols, tqkv_ref)
        c = c * _sigmoid(c)
        part, first_head = divmod(i * heads_per_chunk, HEADS)
        for j in range(heads_per_chunk):
            ch = c[:, j * HEAD_DIM:(j + 1) * HEAD_DIM]
            sl = slice((first_head + j) * HEAD_DIM, (first_head + j + 1) * HEAD_DIM)
            if part == 0:
                norm = lax.rsqrt(jnp.sum(ch * ch, axis=-1, keepdims=True) + EPS)
                q_ref[rs, sl] = ch * norm * (HEAD_DIM ** -0.5)
            elif part == 1:
                k_ref[rs, sl] = ch * lax.rsqrt(jnp.sum(ch * ch, axis=-1, keepdims=True) + EPS)
            else:
                v_ref[rs, sl] = ch

    def z_tail(rs, pz):
        z_ref[rs, :] = pz

    def decay_beta_tail(rs, ab):
        sp_in = ab + dtb_ref[...]
        softplus = jnp.maximum(sp_in, 0.0) + jnp.log1p(jnp.exp(-jnp.abs(sp_in)))
        g = -jnp.exp(alog_ref[...]) * softplus
        lane = lax.broadcasted_iota(jnp.int32, ab.shape, 1)
        gb_ref[rs, :] = jnp.where(lane < HEADS, g, jnp.where(lane < 2 * HEADS, _sigmoid(ab), 0.0))

    def matmul(w_ref, w_cols, rs):
        return _dot(xn_s[rs, :], w_ref[:, w_cols])

    stages = []
    for i in range(CONV_CH // PROJ_COLS):
        w_cols = slice(3 * i * PROJ_COLS, 3 * (i + 1) * PROJ_COLS)
        stages += [(functools.partial(matmul, wa_ref, w_cols, rs),
                    functools.partial(mixer_a_tail, i, rs)) for rs in blocks]
    for i in range(3 * GDN_W // PROJ_COLS):
        w_cols = slice(QKV_COL0 + i * PROJ_COLS, QKV_COL0 + (i + 1) * PROJ_COLS)
        stages += [(functools.partial(matmul, wi_ref, w_cols, rs),
                    functools.partial(qkv_tail, i, rs)) for rs in blocks]
    stages += [(functools.partial(matmul, wi_ref, slice(Z_COL0, Z_COL0 + GDN_W), rs),
                functools.partial(z_tail, rs)) for rs in blocks]
    stages += [(functools.partial(matmul, wab_ref, slice(None), rs),
                functools.partial(decay_beta_tail, rs)) for rs in blocks]
    pending = None
    for issue, tail in stages:
        res = issue()
        if pending is not None:
            pending()
        pending = functools.partial(tail, res)
    pending()


def _proj_call(x3, hist_cu, hist_qkv, w, rows):
    nb, seq, _ = x3.shape
    nt = seq // rows
    tok = lambda width: pl.BlockSpec((None, rows, width), lambda b, t: (b, t, 0))
    full = lambda a: pl.BlockSpec(a.shape, lambda b, t: (0,) * a.ndim,
                                  pipeline_mode=pl.Buffered(1))
    tail = lambda width: pl.BlockSpec((None, HIST, width), lambda b, t: (b, 0, 0))
    consts = (hist_cu, hist_qkv, w['mix_norm_w'], w['w_a'], w['w_in'], w['w_ab'],
              w['conv_mix_w'], w['conv_mix_norm_w'], w['gmat'], w['qkv_conv_w'], w['a_log'],
              w['dt_bias'])
    out_shape = (
        jax.ShapeDtypeStruct((nb, seq, CONV_CH), BF16),
        jax.ShapeDtypeStruct((nb, seq, GDN_W), F32),
        jax.ShapeDtypeStruct((nb, seq, GDN_W), F32),
        jax.ShapeDtypeStruct((nb, seq, GDN_W), F32),
        jax.ShapeDtypeStruct((nb, seq, GDN_W), F32),
        jax.ShapeDtypeStruct((nb, seq, LANES), F32),
        jax.ShapeDtypeStruct((nb, HIST, CONV_CH), F32),
        jax.ShapeDtypeStruct((nb, HIST, 3 * GDN_W), F32),
    )
    return pl.pallas_call(
        _proj_kernel,
        grid=(nb, nt),
        in_specs=[tok(D_MODEL)] + [full(a) for a in consts],
        out_specs=(tok(CONV_CH), tok(GDN_W), tok(GDN_W), tok(GDN_W), tok(GDN_W), tok(LANES),
                   tail(CONV_CH), tail(3 * GDN_W)),
        out_shape=out_shape,
        scratch_shapes=[pltpu.VMEM((2 * HIST, CONV_CH), F32),
                        pltpu.VMEM((2 * HIST, 3 * GDN_W), F32),
                        pltpu.VMEM((rows, D_MODEL), BF16)],
        compiler_params=pltpu.CompilerParams(
            dimension_semantics=("arbitrary", "arbitrary"), vmem_limit_bytes=VMEM_LIMIT),
        name="proj",
    )(x3, *consts)


def _chunk_masks():
    row = lax.broadcasted_iota(jnp.int32, (CHUNK, CHUNK), 0)
    col = lax.broadcasted_iota(jnp.int32, (CHUNK, CHUNK), 1)
    incl = row >= col
    strict = row > col
    levels = []
    n = 1
    while n < CHUNK:
        levels.append((row // (2 * n) == col // (2 * n)) & ((row // n) % 2 == 1) & ((col // n) % 2 == 0))
        n *= 2
    return incl, strict, levels


def _chunk_cumsum(gb_blk, incl):
    tri = incl.astype(BF16)
    hi, lo = _split_bf16(gb_blk)
    return _dot(tri, hi) + _dot(tri, lo)


def _chunk_transforms(chains, masks, state_only):
    incl, strict, levels = masks
    eye = (lax.broadcasted_iota(jnp.int32, (CHUNK, CHUNK), 0)
           == lax.broadcasted_iota(jnp.int32, (CHUNK, CHUNK), 1)).astype(F32)
    decay = [jnp.exp(jnp.where(incl, gc_col - gc_row, -jnp.inf))
             for (_, _, _, _, gc_col, gc_row, _) in chains]
    kb = [kh * beta for (_, kh, _, beta, _, _, _) in chains]
    k_bf = [kh.astype(BF16) for (_, kh, _, _, _, _, _) in chains]
    a_mat = [jnp.where(strict, _dot_nt(kb_i.astype(BF16), k_i) * d_i, 0.0)
             for kb_i, k_i, d_i in zip(kb, k_bf, decay)]
    t_inv = [eye - jnp.where(levels[0], a_i, 0.0) for a_i in a_mat]
    for lvl in levels[1:]:
        t_bf = [t_i.astype(BF16) for t_i in t_inv]
        m1 = [_dot(jnp.where(lvl, a_i, 0.0).astype(BF16), t_i) for a_i, t_i in zip(a_mat, t_bf)]
        t_inv = [t_i - _dot(tb_i, m_i.astype(BF16)) for t_i, tb_i, m_i in zip(t_inv, t_bf, m1)]
    rhs = [jnp.concatenate([vh * beta, kb_i * jnp.exp(gc_col)], axis=1)
           for (_, _, vh, beta, gc_col, _, _), kb_i in zip(chains, kb)]
    uw = [_dot(t_i.astype(BF16), r_i.astype(BF16)).astype(BF16)
          for t_i, r_i in zip(t_inv, rhs)]
    kd = [kh * jnp.exp(g_last - gc_col) for (_, kh, _, _, gc_col, _, g_last) in chains]
    pn = [_dot_tn(kd_i.astype(BF16), uw_i) for kd_i, uw_i in zip(kd, uw)]
    if state_only:
        return [pn_i[:, :HEAD_DIM] for pn_i in pn]
    intra = [jnp.where(incl, _dot_nt(qh.astype(BF16), k_i) * d_i, 0.0)
             for (qh, _, _, _, _, _, _), k_i, d_i in zip(chains, k_bf, decay)]
    iuw = [_dot(in_i.astype(BF16), uw_i) for in_i, uw_i in zip(intra, uw)]
    out = []
    for (qh, _, _, _, gc_col, _, g_last), pn_i, iuw_i in zip(chains, pn, iuw):
        q_part = qh * jnp.exp(gc_col) - iuw_i[:, HEAD_DIM:]
        out.append((q_part, pn_i[:, HEAD_DIM:], iuw_i[:, :HEAD_DIM], pn_i[:, :HEAD_DIM],
                    jnp.exp(g_last)))
    return out


def _gdn_kernel(pq_ref, pk_ref, pv_ref, pgb_ref, q_ref, k_ref, v_ref, gb_ref, z_ref, gnw_ref, yb_ref,
                s_s, qp_s, op_s, n_s, a_s):
    nb = q_ref.shape[0]
    n_chunks = q_ref.shape[1] // CHUNK
    group = GDN_GROUP if n_chunks % GDN_GROUP == 0 else 1
    masks = _chunk_masks()

    def chains_of(gb_blk, q_blk, k_blk, v_blk):
        gc = _chunk_cumsum(gb_blk, masks[0])
        gc_t = gc.T
        res = []
        for h in range(HEADS):
            sl = slice(h * HEAD_DIM, (h + 1) * HEAD_DIM)
            res.append((q_blk(sl), k_blk(sl), v_blk(sl), gb_blk[:, HEADS + h:HEADS + h + 1],
                        gc[:, h:h + 1], gc_t[h:h + 1, :CHUNK], gc[CHUNK - 1:CHUNK, h:h + 1]))
        return res

    @pl.when(pl.program_id(0) == 0)
    def _():
        chains = chains_of(pgb_ref[...], lambda sl: pq_ref[:, sl], lambda sl: pk_ref[:, sl],
                           lambda sl: pv_ref[:, sl])
        for h, n_mat in enumerate(_chunk_transforms(chains, masks, True)):
            for b in range(nb):
                s_s[b * HEADS + h] = n_mat

    def transform_group(gi, carry):
        chains, where = [], []
        for cc in range(group):
            c = gi * group + cc
            rows = pl.ds(pl.multiple_of(c * CHUNK, CHUNK), CHUNK)
            for b in range(nb):
                chains += chains_of(gb_ref[b, rows, :], lambda sl: q_ref[b, rows, sl],
                                    lambda sl: k_ref[b, rows, sl], lambda sl: v_ref[b, rows, sl])
                where += [(c, b * HEADS + h) for h in range(HEADS)]
        for (c, ch), (q_part, p_mat, o_part, n_mat, a) in zip(
                where, _chunk_transforms(chains, masks, False)):
            qp_s[c, ch, 0:CHUNK, :] = q_part.astype(BF16)
            qp_s[c, ch, CHUNK:, :] = p_mat.astype(BF16)
            op_s[c, ch] = o_part
            n_s[c, ch] = n_mat
            a_s[c, ch] = jnp.broadcast_to(a, (8, HEAD_DIM))
        return carry

    lax.fori_loop(0, n_chunks // group, transform_group, 0)

    def scan_chunk(c, carry):
        r0 = pl.multiple_of(c * CHUNK, CHUNK)
        for b in range(nb):
            for h in range(HEADS):
                ch = b * HEADS + h
                s = s_s[ch]
                r = _dot(qp_s[c, ch], s.astype(BF16))
                s_s[ch] = a_s[c, ch][0:1, :] * s - r[CHUNK:] + n_s[c, ch]
                rows, cols = pl.ds(r0, CHUNK), slice(h * HEAD_DIM, (h + 1) * HEAD_DIM)
                o = r[:CHUNK] + op_s[c, ch]
                zh = z_ref[b, rows, cols]
                on = o * lax.rsqrt(jnp.mean(o * o, axis=-1, keepdims=True) + EPS) * gnw_ref[...]
                yb_ref[b, rows, cols] = (on * (zh * _sigmoid(zh))).astype(BF16)
        return carry

    lax.fori_loop(0, n_chunks, scan_chunk, 0)


def _gdn_call(pq, pk, pv, pgb, q, k, v, gb, z, gdn_norm_w, chunks_per_step):
    nb, seq, _ = q.shape
    rows = chunks_per_step * CHUNK
    steps = seq // rows
    tok = lambda width: pl.BlockSpec((nb, rows, width), lambda i: (0, i, 0))
    pre = lambda width: pl.BlockSpec((None, CHUNK, width), lambda i: (0, 0, 0))
    nch = nb * HEADS
    return pl.pallas_call(
        _gdn_kernel,
        grid=(steps,),
        in_specs=[pre(GDN_W), pre(GDN_W), pre(GDN_W), pre(LANES),
                  tok(GDN_W), tok(GDN_W), tok(GDN_W), tok(LANES), tok(GDN_W),
                  pl.BlockSpec((1, HEAD_DIM), lambda i: (0, 0))],
        out_specs=tok(GDN_W),
        out_shape=jax.ShapeDtypeStruct((nb, seq, GDN_W), BF16),
        scratch_shapes=[
            pltpu.VMEM((nch, HEAD_DIM, HEAD_DIM), F32),
            pltpu.VMEM((chunks_per_step, nch, CHUNK + HEAD_DIM, HEAD_DIM), BF16),
            pltpu.VMEM((chunks_per_step, nch, CHUNK, HEAD_DIM), F32),
            pltpu.VMEM((chunks_per_step, nch, HEAD_DIM, HEAD_DIM), F32),
            pltpu.VMEM((chunks_per_step, nch, 8, HEAD_DIM), F32),
        ],
        compiler_params=pltpu.CompilerParams(
            dimension_semantics=("arbitrary",), vmem_limit_bytes=VMEM_LIMIT),
        name="gdn",
    )(pq, pk, pv, pgb, q, k, v, gb, z, gdn_norm_w)


def _mix_out_kernel(x_ref, ya_ref, yb_ref, woa_ref, wob_ref, fnw_ref, wr_ref,
                    br_ref, h_ref, xn_ref, route_ref, assign_ref, counts_ref, cnt_s):
    rows = x_ref.shape[0]
    sub = MIX_SUB_ROWS if rows % MIX_SUB_ROWS == 0 else rows
    blocks = [slice(r, r + sub) for r in range(0, rows, sub)]

    h1 = [x_ref[rs, :] + (_dot(ya_ref[rs, :], woa_ref[...]) + _dot(yb_ref[rs, :], wob_ref[...]))
          for rs in blocks]
    for rs, h1_i in zip(blocks, h1):
        h_ref[rs, :] = h1_i
    xn = [h1_i * lax.rsqrt(jnp.mean(h1_i * h1_i, axis=-1, keepdims=True) + EPS) * fnw_ref[...]
          for h1_i in h1]
    xn_bf = [xn_i.astype(BF16) for xn_i in xn]
    for rs, xn_i in zip(blocks, xn_bf):
        xn_ref[rs, :] = _pack_bf16_pairs(xn_i)
    logits = [_dot(xn_i, wr_ref[...]) + br_ref[...] for xn_i in xn_bf]

    lane = lax.broadcasted_iota(jnp.int32, (sub, LANES), 1).astype(F32)
    neg = -jnp.inf
    big = float(1 << 20)

    def argmax_first(vals):
        m = jnp.max(vals, axis=-1, keepdims=True)
        idx = jnp.min(jnp.where(vals == m, lane, big), axis=-1, keepdims=True)
        return m, idx

    def top_k(lg):
        grp = jnp.where(lane < N_GROUPS, lg, neg)
        g_max, g_sel = argmax_first(grp)
        p_grp = 1.0 / jnp.sum(jnp.exp(grp - g_max), axis=-1, keepdims=True)
        lo_lane = N_GROUPS + g_sel * EXPERTS_PER_GROUP
        ex = jnp.where((lane >= lo_lane) & (lane < lo_lane + EXPERTS_PER_GROUP), lg, neg)
        m1, i1 = argmax_first(ex)
        m2, i2 = argmax_first(jnp.where(lane == i1, neg, ex))
        e2 = jnp.exp(m2 - m1)
        return i1 - N_GROUPS, i2 - N_GROUPS, 1.0 / (1.0 + e2) * p_grp, e2 / (1.0 + e2) * p_grp

    picks = [top_k(lg) for lg in logits]

    @pl.when(pl.program_id(0) == 0)
    def _():
        cnt_s[...] = jnp.zeros(cnt_s.shape, F32)

    onehots = [((lane == e1).astype(F32), (lane == e2).astype(F32)) for e1, e2, _, _ in picks]
    earlier = (lax.broadcasted_iota(jnp.int32, (sub, sub), 0)
               > lax.broadcasted_iota(jnp.int32, (sub, sub), 1)).astype(BF16)
    within = [_dot(earlier, (oh1 + oh2).astype(BF16)) for oh1, oh2 in onehots]
    counts = cnt_s[...]
    for rs, (e1, e2, w1, w2), (oh1, oh2), within_i in zip(blocks, picks, onehots, within):
        before = within_i + counts
        rank1 = jnp.sum(before * oh1, axis=-1, keepdims=True)
        rank2 = jnp.sum(before * oh2, axis=-1, keepdims=True)
        counts = counts + jnp.sum(oh1 + oh2, axis=0, keepdims=True)
        route = jnp.zeros((sub, LANES), F32)
        for k, val in ((ROUTE_E1, e1), (ROUTE_E2, e2), (ROUTE_W1, w1), (ROUTE_W2, w2),
                       (ROUTE_RANK1, rank1), (ROUTE_RANK2, rank2)):
            route = jnp.where(lane == k, val, route)
        route_ref[rs, :] = route
        assign_ref[:, rs] = route.T[:ASSIGN_ROWS].astype(jnp.int32)
    cnt_s[...] = counts
    counts_ref[...] = counts


def _mix_out_call(x2, ya, yb, w, rows):
    tokens = x2.shape[0]
    tok = lambda width: pl.BlockSpec((rows, width), lambda i: (i, 0))
    full = lambda a: pl.BlockSpec(a.shape, lambda i: (0,) * a.ndim)
    consts = (w['w_out_a'], w['w_out_b'], w['ffn_norm_w'], w['w_route'], w['b_route'])
    return pl.pallas_call(
        _mix_out_kernel,
        grid=(tokens // rows,),
        in_specs=[tok(D_MODEL), tok(CONV_CH), tok(GDN_W)] + [full(a) for a in consts],
        out_specs=(tok(D_MODEL), tok(D_MODEL // 2), tok(LANES),
                   pl.BlockSpec((ASSIGN_ROWS, rows), lambda i: (0, i)),
                   pl.BlockSpec((1, LANES), lambda i: (0, 0))),
        out_shape=(jax.ShapeDtypeStruct((tokens, D_MODEL), F32),
                   jax.ShapeDtypeStruct((tokens, D_MODEL // 2), jnp.uint32),
                   jax.ShapeDtypeStruct((tokens, LANES), F32),
                   jax.ShapeDtypeStruct((ASSIGN_ROWS, tokens), jnp.int32),
                   jax.ShapeDtypeStruct((1, LANES), F32)),
        scratch_shapes=[pltpu.VMEM((1, LANES), F32)],
        compiler_params=pltpu.CompilerParams(
            dimension_semantics=("arbitrary",), vmem_limit_bytes=VMEM_LIMIT),
        name="mix_out",
    )(x2, ya, yb, *consts)


def _sc_row_move(table, idx, out_rows, chunk, scatter, name):
    n = idx.shape[0]
    n_src, width = table.shape
    workers = SC_CORES * SC_SUBCORES
    per_w = n // workers
    assert n % workers == 0 and per_w % (2 * chunk) == 0
    assert n_src % per_w == 0 or not scatter
    pairs = per_w // (2 * chunk)
    mesh = plsc.VectorSubcoreMesh(core_axis_name="c", subcore_axis_name="s",
                                  num_cores=SC_CORES, num_subcores=SC_SUBCORES)

    def body(table_hbm, idx_hbm, out_hbm, idx_v, buf_a, buf_b, sem_ra, sem_rb, sem_wa, sem_wb):
        base = (lax.axis_index("s") * SC_CORES + lax.axis_index("c")) * per_w
        src_base = lax.rem(base, n_src)
        pltpu.sync_copy(idx_hbm.at[pl.ds(base, per_w)], idx_v)

        def read(c, buf, sem):
            off = pl.multiple_of(c * chunk, chunk)
            src = (table_hbm.at[pl.ds(src_base + off, chunk)] if scatter
                   else table_hbm.at[idx_v.at[pl.ds(off, chunk)]])
            return pltpu.make_async_copy(src, buf, sem)

        def write(c, buf, sem):
            off = pl.multiple_of(c * chunk, chunk)
            dst = (out_hbm.at[idx_v.at[pl.ds(off, chunk)]] if scatter
                   else out_hbm.at[pl.ds(base + off, chunk)])
            return pltpu.make_async_copy(buf, dst, sem)

        read(0, buf_a, sem_ra).start()

        @pl.loop(0, pairs)
        def _(j):
            ca = 2 * j
            cb = ca + 1
            read(cb, buf_b, sem_rb).start()
            read(ca, buf_a, sem_ra).wait()
            write(ca, buf_a, sem_wa).start()
            read(cb, buf_b, sem_rb).wait()
            write(cb, buf_b, sem_wb).start()
            write(ca, buf_a, sem_wa).wait()

            @pl.when(j + 1 < pairs)
            def _():
                read(ca + 2, buf_a, sem_ra).start()

            write(cb, buf_b, sem_wb).wait()

    return pl.kernel(
        body,
        out_type=jax.ShapeDtypeStruct((out_rows, width), table.dtype),
        mesh=mesh,
        scratch_types=[pltpu.VMEM((per_w,), jnp.int32),
                       pltpu.VMEM((chunk, width), table.dtype),
                       pltpu.VMEM((chunk, width), table.dtype),
                       pltpu.SemaphoreType.DMA, pltpu.SemaphoreType.DMA,
                       pltpu.SemaphoreType.DMA, pltpu.SemaphoreType.DMA],
        name=name,
    )(table, idx)


def _experts_kernel(bexp_ref, bvalid_ref, nused_ref, x_ref, wg_hbm, wu_hbm, wd_hbm, y_ref,
                    wg_s, wu_s, wd_s, stage_g, stage_u, stage_d, sems, seq_s):
    n_used = nused_ref[0]
    n_last = bexp_ref.shape[0] - 1
    rows = x_ref.shape[0] // EXPERT_STEP_BLOCKS
    sub = EXPERT_SUB_ROWS if rows % EXPERT_SUB_ROWS == 0 else rows

    def weight_copies(e, slot):
        return (pltpu.make_async_copy(wg_hbm.at[e], stage_g.at[slot], sems.at[slot, 0]),
                pltpu.make_async_copy(wu_hbm.at[e], stage_u.at[slot], sems.at[slot, 1]),
                pltpu.make_async_copy(wd_hbm.at[e], stage_d.at[slot], sems.at[slot, 2]))

    def block(i, r0):
        expert = bexp_ref[i]
        first_of_expert = (i == 0) | (expert != bexp_ref[jnp.maximum(i - 1, 0)])

        @pl.when((i == 0) & (n_used > 0))
        def _():
            seq_s[0] = 0
            for copy in weight_copies(expert, 0):
                copy.start()

        @pl.when((i < n_used) & first_of_expert)
        def _():
            @pl.when(i > 0)
            def _():
                seq_s[0] = seq_s[0] + 1

            slot = seq_s[0] % 2
            nxt = lax.while_loop(
                lambda j: (j < n_used) & (bexp_ref[jnp.minimum(j, n_last)] == expert),
                lambda j: j + 1, i + 1)

            @pl.when(nxt < n_used)
            def _():
                for copy in weight_copies(bexp_ref[jnp.minimum(nxt, n_last)], 1 - slot):
                    copy.start()

            for copy in weight_copies(expert, slot):
                copy.wait()
            wg_s[...] = stage_g[slot].astype(BF16)
            wu_s[...] = stage_u[slot].astype(BF16)
            wd_s[...] = stage_d[slot].astype(BF16)

        valid = bvalid_ref[i]

        def ffn(n_sub):
            blocks = [slice(r0 + k * sub, r0 + (k + 1) * sub) for k in range(n_sub)]
            row = lax.broadcasted_iota(jnp.int32, (sub, x_ref.shape[1]), 0)
            xb = [jnp.concatenate(_unpack_bf16_pairs(
                      jnp.where(row + (rs.start - r0) < valid, x_ref[rs, :], jnp.uint32(0))),
                      axis=1).astype(BF16) for rs in blocks]
            gate = [_dot(xb_i, wg_s[...]) for xb_i in xb]
            up = [_dot(xb_i, wu_s[...]) for xb_i in xb]
            hid = [((g_i * _sigmoid(g_i)) * u_i).astype(BF16) for g_i, u_i in zip(gate, up)]
            y = [_dot(h_i, wd_s[...]) for h_i in hid]
            for rs, y_i in zip(blocks, y):
                y_ref[rs, :] = _pack_bf16_pairs(y_i.astype(BF16))
            rest = rows - n_sub * sub
            if rest:
                y_ref[r0 + n_sub * sub:r0 + rows, :] = jnp.zeros((rest, y_ref.shape[1]), jnp.uint32)

        tail_sub = (rows // sub + 1) // 2
        pl.when((i < n_used) & (valid > tail_sub * sub))(lambda: ffn(rows // sub))
        pl.when((i < n_used) & (valid <= tail_sub * sub))(lambda: ffn(tail_sub))
        pl.when(i >= n_used)(lambda: ffn(0))

    for j in range(EXPERT_STEP_BLOCKS):
        block(pl.program_id(0) * EXPERT_STEP_BLOCKS + j, j * rows)


def _experts_call(block_expert, block_valid, n_used, x_rows, w_gate, w_up, w_down, rows):
    n_blocks = block_expert.shape[0]
    step_rows = EXPERT_STEP_BLOCKS * rows
    hbm = pl.BlockSpec(memory_space=pl.ANY)
    grid_spec = pltpu.PrefetchScalarGridSpec(
        num_scalar_prefetch=3,
        grid=(n_blocks // EXPERT_STEP_BLOCKS,),
        in_specs=[pl.BlockSpec((step_rows, D_MODEL // 2), lambda i, be, bv, nu: (i, 0)),
                  hbm, hbm, hbm],
        out_specs=pl.BlockSpec((step_rows, D_MODEL // 2), lambda i, be, bv, nu: (i, 0)),
        scratch_shapes=[pltpu.VMEM((D_MODEL, D_EXPERT), BF16),
                        pltpu.VMEM((D_MODEL, D_EXPERT), BF16),
                        pltpu.VMEM((D_EXPERT, D_MODEL), BF16),
                        pltpu.VMEM((2, D_MODEL, D_EXPERT), F32),
                        pltpu.VMEM((2, D_MODEL, D_EXPERT), F32),
                        pltpu.VMEM((2, D_EXPERT, D_MODEL), F32),
                        pltpu.SemaphoreType.DMA((2, 3)),
                        pltpu.SMEM((1,), jnp.int32)],
    )
    return pl.pallas_call(
        _experts_kernel,
        grid_spec=grid_spec,
        out_shape=jax.ShapeDtypeStruct((n_blocks * rows, D_MODEL // 2), jnp.uint32),
        compiler_params=pltpu.CompilerParams(
            dimension_semantics=("arbitrary",), vmem_limit_bytes=VMEM_LIMIT),
        name="experts",
    )(block_expert, block_valid, n_used, x_rows, w_gate, w_up, w_down)


def _combine_kernel(y1_ref, y2_ref, h_ref, route_ref, fw_ref, out_ref):
    route = route_ref[...]
    y1 = jnp.concatenate(_unpack_bf16_pairs(y1_ref[...]), axis=1)
    y2 = jnp.concatenate(_unpack_bf16_pairs(y2_ref[...]), axis=1)
    moe = route[:, ROUTE_W1:ROUTE_W1 + 1] * y1 + route[:, ROUTE_W2:ROUTE_W2 + 1] * y2
    h2 = h_ref[...] + moe
    out_ref[...] = h2 * lax.rsqrt(jnp.mean(h2 * h2, axis=-1, keepdims=True) + EPS) * fw_ref[...]


def _combine_call(y_tok, h1, route, final_w, rows):
    tokens = h1.shape[0]
    steps = tokens // rows
    tok = lambda width: pl.BlockSpec((rows, width), lambda i: (i, 0))
    return pl.pallas_call(
        _combine_kernel,
        grid=(steps,),
        in_specs=[tok(D_MODEL // 2), pl.BlockSpec((rows, D_MODEL // 2), lambda i: (i + steps, 0)),
                  tok(D_MODEL), tok(LANES), pl.BlockSpec((1, D_MODEL), lambda i: (0, 0))],
        out_specs=tok(D_MODEL),
        out_shape=jax.ShapeDtypeStruct((tokens, D_MODEL), F32),
        compiler_params=pltpu.CompilerParams(
            dimension_semantics=("arbitrary",), vmem_limit_bytes=VMEM_LIMIT),
        name="combine",
    )(y_tok, y_tok, h1, route, final_w)


def _dispatch_tables(assign, counts, tokens, rows):
    n_blocks = pl.cdiv((tokens * TOP_K + N_EXPERTS * (rows - 1)) // rows,
                       EXPERT_STEP_BLOCKS) * EXPERT_STEP_BLOCKS
    counts = counts[0, :N_EXPERTS].astype(jnp.int32)
    padded = (counts + rows - 1) // rows * rows
    pad_end = jnp.cumsum(padded)
    pad_start = pad_end - padded
    experts = assign[ROUTE_E1:ROUTE_E2 + 1]
    ranks = assign[ROUTE_RANK1:ROUTE_RANK2 + 1]
    is_expert = experts[..., None] == jnp.arange(N_EXPERTS, dtype=jnp.int32)
    dest = (jnp.sum(jnp.where(is_expert, pad_start, 0), axis=-1) + ranks).reshape(-1)
    block_start = jnp.arange(n_blocks, dtype=jnp.int32) * rows
    in_run = (block_start[:, None] >= pad_start[None, :]) & (block_start[:, None] < pad_end[None, :])
    lookup = lambda table: jnp.sum(jnp.where(in_run, table[None, :], 0), axis=1)
    block_expert = lookup(jnp.arange(N_EXPERTS, dtype=jnp.int32))
    block_valid = jnp.clip(lookup(pad_start + counts) - block_start, 0, rows)
    n_used = (pad_end[-1] // rows).astype(jnp.int32).reshape(1)
    return dest, block_expert, block_valid.astype(jnp.int32), n_used, n_blocks


def _prepare_weights(mix_norm_w, w_in, conv_mix_w, conv_mix_norm_w, qkv_conv_w, a_log, dt_bias,
                     gdn_norm_w, w_out, ffn_norm_w, w_group, b_group, w_router, b_router):
    pad_lanes = lambda v: jnp.pad(v.reshape(1, -1), ((0, 0), (0, LANES - v.size)))
    w_in_bf = w_in.astype(BF16)
    w_ab = jnp.pad(w_in_bf[:, Z_COL0 + GDN_W:], ((0, 0), (0, LANES - 2 * HEADS)))
    grp = jnp.arange(PROJ_COLS) // CONV_GROUP_W
    gmat = jnp.where(grp[:, None] == grp[None, :], 1.0 / CONV_GROUP_W, 0.0).astype(BF16)
    w_route = jnp.concatenate([w_group, w_router.reshape(D_MODEL, N_EXPERTS)], axis=1)
    w_route = jnp.pad(w_route, ((0, 0), (0, LANES - w_route.shape[1])))
    return dict(
        mix_norm_w=mix_norm_w.reshape(1, -1),
        w_a=jnp.concatenate([w_in_bf[:, part * CONV_CH + i * PROJ_COLS:][:, :PROJ_COLS]
                             for i in range(CONV_CH // PROJ_COLS) for part in range(3)], axis=1),
        w_in=w_in_bf,
        w_ab=w_ab,
        conv_mix_w=conv_mix_w,
        conv_mix_norm_w=conv_mix_norm_w.reshape(1, -1),
        gmat=gmat,
        qkv_conv_w=qkv_conv_w,
        a_log=pad_lanes(a_log),
        dt_bias=pad_lanes(dt_bias),
        gdn_norm_w=gdn_norm_w.reshape(1, -1),
        w_out_a=w_out[:CONV_CH].astype(BF16),
        w_out_b=w_out[CONV_CH:].astype(BF16),
        ffn_norm_w=ffn_norm_w.reshape(1, -1),
        w_route=w_route.astype(BF16),
        b_route=pad_lanes(jnp.concatenate([b_group, b_router.reshape(-1)])),
    )


def _tile(n, preferred):
    return preferred if n % preferred == 0 else n


def kernel(x, meta_tokens, mix_norm_w, w_in, conv_mix_w, conv_mix_norm_w, qkv_conv_w, a_log,
           dt_bias, gdn_norm_w, w_out, ffn_norm_w, w_group, b_group, w_router, b_router, w_gate,
           w_up, w_down, final_norm_w):
    assert mix_norm_w.shape[0] == 1, "single-layer kernel"
    batch, seq, _ = x.shape
    assert seq % CHUNK == 0
    w = _prepare_weights(mix_norm_w[0], w_in[0], conv_mix_w[0], conv_mix_norm_w[0], qkv_conv_w[0],
                         a_log[0], dt_bias[0], gdn_norm_w[0], w_out[0], ffn_norm_w[0], w_group[0],
                         b_group[0], w_router[0], b_router[0])

    prefix = jnp.concatenate([jnp.zeros((CHUNK - N_META, D_MODEL), x.dtype),
                              meta_tokens.astype(x.dtype)], axis=0)[None]
    zero_cu = jnp.zeros((HIST, CONV_CH), F32)
    zero_qkv = jnp.zeros((HIST, 3 * GDN_W), F32)
    _, pq, pk, pv, _, pgb, tail_cu, tail_qkv = _proj_call(prefix, zero_cu, zero_qkv, w, CHUNK)

    ya, q, k, v, z, gb, _, _ = _proj_call(x, tail_cu[0], tail_qkv[0], w, _tile(seq, PROJ_ROWS))
    yb = _gdn_call(pq, pk, pv, pgb, q, k, v, gb, z, w['gdn_norm_w'], _tile(seq // CHUNK, GDN_CHUNKS))

    tokens = batch * seq
    flat = lambda a: a.reshape(tokens, a.shape[-1])
    h1, xn2, route, assign, counts = _mix_out_call(flat(x), flat(ya), flat(yb), w,
                                           _tile(tokens, MIX_ROWS))

    dest, block_expert, block_valid, n_used, n_blocks = _dispatch_tables(assign, counts, tokens,
                                                                         EXPERT_ROWS)
    x_rows = _sc_row_move(xn2, dest, n_blocks * EXPERT_ROWS, SC_DISPATCH_CHUNK, True,
                          "dispatch_scatter")
    y_rows = _experts_call(block_expert, block_valid, n_used, x_rows, w_gate[0], w_up[0],
                           w_down[0], EXPERT_ROWS)
    y_tok = _sc_row_move(y_rows, dest, TOP_K * tokens, SC_COMBINE_CHUNK, False, "combine_gather")
    out = _combine_call(y_tok, h1, route, final_norm_w.reshape(1, -1), _tile(tokens, COMBINE_ROWS))
    return out.reshape(batch, seq, D_MODEL)
```

```python
import functools

import jax
import jax.numpy as jnp
from jax import lax
from jax.experimental import pallas as pl
from jax.experimental.pallas import tpu as pltpu
from jax.experimental.pallas import tpu_sc as plsc

F32 = jnp.float32
BF16 = jnp.bfloat16
EPS = 1e-6

D_MODEL = 1024
N_META = 16
CONV_CH = 512
CONV_GROUP_W = 64
HEADS = 4
HEAD_DIM = 128
GDN_W = HEADS * HEAD_DIM
QKV_COL0 = 3 * CONV_CH
Z_COL0 = QKV_COL0 + 3 * GDN_W
CHUNK = 64
N_GROUPS = 4
EXPERTS_PER_GROUP = 8
N_EXPERTS = N_GROUPS * EXPERTS_PER_GROUP
TOP_K = 2
D_EXPERT = 512
LANES = 128
ROUTE_E1, ROUTE_E2, ROUTE_RANK1, ROUTE_RANK2, ROUTE_W1, ROUTE_W2 = range(6)
ASSIGN_ROWS = 8
HIST = 8

PROJ_ROWS = 512
PROJ_COLS = 256
PROJ_SUB_ROWS = 128
GDN_CHUNKS = 8
GDN_GROUP = 4
MIX_ROWS = 1024
MIX_SUB_ROWS = 128
EXPERT_ROWS = 256
EXPERT_SUB_ROWS = 128
EXPERT_STEP_BLOCKS = 4
COMBINE_ROWS = 1024
VMEM_LIMIT = 56 * 1024 * 1024
SC_CORES = 2
SC_SUBCORES = 16
SC_DISPATCH_CHUNK = 64
SC_COMBINE_CHUNK = 64


def _dot(a, b):
    return jnp.dot(a, b, preferred_element_type=F32)


def _dot_nt(a, b):
    return lax.dot_general(a, b, (((1,), (1,)), ((), ())), preferred_element_type=F32)


def _dot_tn(a, b):
    return lax.dot_general(a, b, (((0,), (0,)), ((), ())), preferred_element_type=F32)


def _split_bf16(x):
    hi = x.astype(BF16)
    lo = (x - hi.astype(F32)).astype(BF16)
    return hi, lo


def _sigmoid(x):
    return 1.0 / (1.0 + jnp.exp(-x))


def _pack_bf16_pairs(x_bf16):
    bits = pltpu.bitcast(x_bf16.astype(F32), jnp.uint32)
    n = x_bf16.shape[1] // 2
    return (bits[:, :n] >> 16) | (bits[:, n:] & jnp.uint32(0xFFFF0000))


def _unpack_bf16_pairs(packed):
    return (pltpu.bitcast(packed << 16, F32),
            pltpu.bitcast(packed & jnp.uint32(0xFFFF0000), F32))


def _proj_kernel(x_ref, hcu_ref, hqkv_ref, nw_ref, wa_ref, wi_ref, wab_ref, cmw_ref,
                 cmn_ref, gmat_ref, qcw_ref, alog_ref, dtb_ref,
                 ya_ref, q_ref, k_ref, v_ref, z_ref, gb_ref, tcu_ref, tqkv_ref,
                 cu_s, qkv_s, xn_s):
    rows = x_ref.shape[0]
    sub = PROJ_SUB_ROWS if rows % PROJ_SUB_ROWS == 0 else rows
    blocks = [slice(r, r + sub) for r in range(0, rows, sub)]

    @pl.when(pl.program_id(1) == 0)
    def _():
        cu_s[0:HIST, :] = hcu_ref[...]
        qkv_s[0:HIST, :] = hqkv_ref[...]

    x = x_ref[...]
    ms = jnp.mean(x * x, axis=-1, keepdims=True)
    xn_s[...] = (x * lax.rsqrt(ms + EPS) * nw_ref[...]).astype(BF16)

    def causal_conv(buf, cur, w_ref, cols, tail_ref):
        taps = w_ref.shape[0]
        acc = pltpu.roll(cur, taps - 1, axis=0) * w_ref[0:1, cols]
        for j in range(1, taps - 1):
            acc = acc + pltpu.roll(cur, taps - 1 - j, axis=0) * w_ref[j:j + 1, cols]
        acc = acc + cur * w_ref[taps - 1:taps, cols]
        buf[HIST:2 * HIST, cols] = cur[:HIST]
        seam = buf[pl.ds(HIST - taps + 1, HIST), cols] * w_ref[0:1, cols]
        for j in range(1, taps):
            seam = seam + buf[pl.ds(HIST - taps + 1 + j, HIST), cols] * w_ref[j:j + 1, cols]
        tail = cur[sub - HIST:]
        buf[0:HIST, cols] = tail
        tail_ref[:, cols] = tail
        return jnp.concatenate([seam, acc[HIST:]], axis=0)

    def mixer_a_tail(i, rs, pa):
        cols = slice(i * PROJ_COLS, (i + 1) * PROJ_COLS)
        cu = pa[:, PROJ_COLS:2 * PROJ_COLS] * pa[:, 2 * PROJ_COLS:]
        ya = pa[:, :PROJ_COLS] * causal_conv(cu_s, cu, cmw_ref, cols, tcu_ref)
        msg = _dot((ya * ya).astype(BF16), gmat_ref[...])
        ya_ref[rs, cols] = (ya * lax.rsqrt(msg + EPS) * cmn_ref[:, cols]).astype(BF16)

    heads_per_chunk = PROJ_COLS // HEAD_DIM

    def qkv_tail(i, rs, pq):
        cols = slice(i * PROJ_COLS, (i + 1) * PROJ_COLS)
        c = causal_conv(qkv_s, pq, qcw_ref, cols, tqkv_ref)
        c = c * _sigmoid(c)
        part, first_head = divmod(i * heads_per_chunk, HEADS)
        for j in range(heads_per_chunk):
            ch = c[:, j * HEAD_DIM:(j + 1) * HEAD_DIM]
            sl = slice((first_head + j) * HEAD_DIM, (first_head + j + 1) * HEAD_DIM)
            if part == 0:
                norm = lax.rsqrt(jnp.sum(ch * ch, axis=-1, keepdims=True) + EPS)
                q_ref[rs, sl] = ch * norm * (HEAD_DIM ** -0.5)
            elif part == 1:
                k_ref[rs, sl] = ch * lax.rsqrt(jnp.sum(ch * ch, axis=-1, keepdims=True) + EPS)
            else:
                v_ref[rs, sl] = ch

    def z_tail(rs, pz):
        z_ref[rs, :] = pz

    def decay_beta_tail(rs, ab):
        sp_in = ab + dtb_ref[...]
        softplus = jnp.maximum(sp_in, 0.0) + jnp.log1p(jnp.exp(-jnp.abs(sp_in)))
        g = -jnp.exp(alog_ref[...]) * softplus
        lane = lax.broadcasted_iota(jnp.int32, ab.shape, 1)
        gb_ref[rs, :] = jnp.where(lane < HEADS, g, jnp.where(lane < 2 * HEADS, _sigmoid(ab), 0.0))

    def matmul(w_ref, w_cols, rs):
        return _dot(xn_s[rs, :], w_ref[:, w_cols])

    stages = []
    for i in range(CONV_CH // PROJ_COLS):
        w_cols = slice(3 * i * PROJ_COLS, 3 * (i + 1) * PROJ_COLS)
        stages += [(functools.partial(matmul, wa_ref, w_cols, rs),
                    functools.partial(mixer_a_tail, i, rs)) for rs in blocks]
    for i in range(3 * GDN_W // PROJ_COLS):
        w_cols = slice(QKV_COL0 + i * PROJ_COLS, QKV_COL0 + (i + 1) * PROJ_COLS)
        stages += [(functools.partial(matmul, wi_ref, w_cols, rs),
                    functools.partial(qkv_tail, i, rs)) for rs in blocks]
    stages += [(functools.partial(matmul, wi_ref, slice(Z_COL0, Z_COL0 + GDN_W), rs),
                functools.partial(z_tail, rs)) for rs in blocks]
    stages += [(functools.partial(matmul, wab_ref, slice(None), rs),
                functools.partial(decay_beta_tail, rs)) for rs in blocks]
    pending = None
    for issue, tail in stages:
        res = issue()
        if pending is not None:
            pending()
        pending = functools.partial(tail, res)
    pending()


def _proj_call(x3, hist_cu, hist_qkv, w, rows):
    nb, seq, _ = x3.shape
    nt = seq // rows
    tok = lambda width: pl.BlockSpec((None, rows, width), lambda b, t: (b, t, 0))
    full = lambda a: pl.BlockSpec(a.shape, lambda b, t: (0,) * a.ndim,
                                  pipeline_mode=pl.Buffered(1))
    tail = lambda width: pl.BlockSpec((None, HIST, width), lambda b, t: (b, 0, 0))
    consts = (hist_cu, hist_qkv, w['mix_norm_w'], w['w_a'], w['w_in'], w['w_ab'],
              w['conv_mix_w'], w['conv_mix_norm_w'], w['gmat'], w['qkv_conv_w'], w['a_log'],
              w['dt_bias'])
    out_shape = (
        jax.ShapeDtypeStruct((nb, seq, CONV_CH), BF16),
        jax.ShapeDtypeStruct((nb, seq, GDN_W), F32),
        jax.ShapeDtypeStruct((nb, seq, GDN_W), F32),
        jax.ShapeDtypeStruct((nb, seq, GDN_W), F32),
        jax.ShapeDtypeStruct((nb, seq, GDN_W), F32),
        jax.ShapeDtypeStruct((nb, seq, LANES), F32),
        jax.ShapeDtypeStruct((nb, HIST, CONV_CH), F32),
        jax.ShapeDtypeStruct((nb, HIST, 3 * GDN_W), F32),
    )
    return pl.pallas_call(
        _proj_kernel,
        grid=(nb, nt),
        in_specs=[tok(D_MODEL)] + [full(a) for a in consts],
        out_specs=(tok(CONV_CH), tok(GDN_W), tok(GDN_W), tok(GDN_W), tok(GDN_W), tok(LANES),
                   tail(CONV_CH), tail(3 * GDN_W)),
        out_shape=out_shape,
        scratch_shapes=[pltpu.VMEM((2 * HIST, CONV_CH), F32),
                        pltpu.VMEM((2 * HIST, 3 * GDN_W), F32),
                        pltpu.VMEM((rows, D_MODEL), BF16)],
        compiler_params=pltpu.CompilerParams(
            dimension_semantics=("arbitrary", "arbitrary"), vmem_limit_bytes=VMEM_LIMIT),
        name="proj",
    )(x3, *consts)


def _chunk_masks():
    row = lax.broadcasted_iota(jnp.int32, (CHUNK, CHUNK), 0)
    col = lax.broadcasted_iota(jnp.int32, (CHUNK, CHUNK), 1)
    incl = row >= col
    strict = row > col
    levels = []
    n = 1
    while n < CHUNK:
        levels.append((row // (2 * n) == col // (2 * n)) & ((row // n) % 2 == 1) & ((col // n) % 2 == 0))
        n *= 2
    return incl, strict, levels


def _chunk_cumsum(gb_blk, incl):
    tri = incl.astype(BF16)
    hi, lo = _split_bf16(gb_blk)
    return _dot(tri, hi) + _dot(tri, lo)


def _chunk_transforms(chains, masks, state_only):
    incl, strict, levels = masks
    eye = (lax.broadcasted_iota(jnp.int32, (CHUNK, CHUNK), 0)
           == lax.broadcasted_iota(jnp.int32, (CHUNK, CHUNK), 1)).astype(F32)
    decay = [jnp.exp(jnp.where(incl, gc_col - gc_row, -jnp.inf))
             for (_, _, _, _, gc_col, gc_row, _) in chains]
    kb = [kh * beta for (_, kh, _, beta, _, _, _) in chains]
    k_bf = [kh.astype(BF16) for (_, kh, _, _, _, _, _) in chains]
    a_mat = [jnp.where(strict, _dot_nt(kb_i.astype(BF16), k_i) * d_i, 0.0)
             for kb_i, k_i, d_i in zip(kb, k_bf, decay)]
    t_inv = [eye - jnp.where(levels[0], a_i, 0.0) for a_i in a_mat]
    for lvl in levels[1:]:
        t_bf = [t_i.astype(BF16) for t_i in t_inv]
        m1 = [_dot(jnp.where(lvl, a_i, 0.0).astype(BF16), t_i) for a_i, t_i in zip(a_mat, t_bf)]
        t_inv = [t_i - _dot(tb_i, m_i.astype(BF16)) for t_i, tb_i, m_i in zip(t_inv, t_bf, m1)]
    rhs = [jnp.concatenate([vh * beta, kb_i * jnp.exp(gc_col)], axis=1)
           for (_, _, vh, beta, gc_col, _, _), kb_i in zip(chains, kb)]
    uw = [_dot(t_i.astype(BF16), r_i.astype(BF16)).astype(BF16)
          for t_i, r_i in zip(t_inv, rhs)]
    kd = [kh * jnp.exp(g_last - gc_col) for (_, kh, _, _, gc_col, _, g_last) in chains]
    pn = [_dot_tn(kd_i.astype(BF16), uw_i) for kd_i, uw_i in zip(kd, uw)]
    if state_only:
        return [pn_i[:, :HEAD_DIM] for pn_i in pn]
    intra = [jnp.where(incl, _dot_nt(qh.astype(BF16), k_i) * d_i, 0.0)
             for (qh, _, _, _, _, _, _), k_i, d_i in zip(chains, k_bf, decay)]
    iuw = [_dot(in_i.astype(BF16), uw_i) for in_i, uw_i in zip(intra, uw)]
    out = []
    for (qh, _, _, _, gc_col, _, g_last), pn_i, iuw_i in zip(chains, pn, iuw):
        q_part = qh * jnp.exp(gc_col) - iuw_i[:, HEAD_DIM:]
        out.append((q_part, pn_i[:, HEAD_DIM:], iuw_i[:, :HEAD_DIM], pn_i[:, :HEAD_DIM],
                    jnp.exp(g_last)))
    return out


def _gdn_kernel(pq_ref, pk_ref, pv_ref, pgb_ref, q_ref, k_ref, v_ref, gb_ref, z_ref, gnw_ref, yb_ref,
                s_s, qp_s, op_s, n_s, a_s):
    nb = q_ref.shape[0]
    n_chunks = q_ref.shape[1] // CHUNK
    group = GDN_GROUP if n_chunks % GDN_GROUP == 0 else 1
    masks = _chunk_masks()

    def chains_of(gb_blk, q_blk, k_blk, v_blk):
        gc = _chunk_cumsum(gb_blk, masks[0])
        gc_t = gc.T
        res = []
        for h in range(HEADS):
            sl = slice(h * HEAD_DIM, (h + 1) * HEAD_DIM)
            res.append((q_blk(sl), k_blk(sl), v_blk(sl), gb_blk[:, HEADS + h:HEADS + h + 1],
                        gc[:, h:h + 1], gc_t[h:h + 1, :CHUNK], gc[CHUNK - 1:CHUNK, h:h + 1]))
        return res

    @pl.when(pl.program_id(0) == 0)
    def _():
        chains = chains_of(pgb_ref[...], lambda sl: pq_ref[:, sl], lambda sl: pk_ref[:, sl],
                           lambda sl: pv_ref[:, sl])
        for h, n_mat in enumerate(_chunk_transforms(chains, masks, True)):
            for b in range(nb):
                s_s[b * HEADS + h] = n_mat

    def transform_group(gi, carry):
        chains, where = [], []
        for cc in range(group):
            c = gi * group + cc
            rows = pl.ds(pl.multiple_of(c * CHUNK, CHUNK), CHUNK)
            for b in range(nb):
                chains += chains_of(gb_ref[b, rows, :], lambda sl: q_ref[b, rows, sl],
                                    lambda sl: k_ref[b, rows, sl], lambda sl: v_ref[b, rows, sl])
                where += [(c, b * HEADS + h) for h in range(HEADS)]
        for (c, ch), (q_part, p_mat, o_part, n_mat, a) in zip(
                where, _chunk_transforms(chains, masks, False)):
            qp_s[c, ch, 0:CHUNK, :] = q_part.astype(BF16)
            qp_s[c, ch, CHUNK:, :] = p_mat.astype(BF16)
            op_s[c, ch] = o_part
            n_s[c, ch] = n_mat
            a_s[c, ch] = jnp.broadcast_to(a, (8, HEAD_DIM))
        return carry

    lax.fori_loop(0, n_chunks // group, transform_group, 0)

    def scan_chunk(c, carry):
        r0 = pl.multiple_of(c * CHUNK, CHUNK)
        for b in range(nb):
            for h in range(HEADS):
                ch = b * HEADS + h
                s = s_s[ch]
                r = _dot(qp_s[c, ch], s.astype(BF16))
                s_s[ch] = a_s[c, ch][0:1, :] * s - r[CHUNK:] + n_s[c, ch]
                rows, cols = pl.ds(r0, CHUNK), slice(h * HEAD_DIM, (h + 1) * HEAD_DIM)
                o = r[:CHUNK] + op_s[c, ch]
                zh = z_ref[b, rows, cols]
                on = o * lax.rsqrt(jnp.mean(o * o, axis=-1, keepdims=True) + EPS) * gnw_ref[...]
                yb_ref[b, rows, cols] = (on * (zh * _sigmoid(zh))).astype(BF16)
        return carry

    lax.fori_loop(0, n_chunks, scan_chunk, 0, unroll=4 if n_chunks % 4 == 0 else 1)


def _gdn_call(pq, pk, pv, pgb, q, k, v, gb, z, gdn_norm_w, chunks_per_step):
    nb, seq, _ = q.shape
    rows = chunks_per_step * CHUNK
    steps = seq // rows
    tok = lambda width: pl.BlockSpec((nb, rows, width), lambda i: (0, i, 0))
    pre = lambda width: pl.BlockSpec((None, CHUNK, width), lambda i: (0, 0, 0))
    nch = nb * HEADS
    return pl.pallas_call(
        _gdn_kernel,
        grid=(steps,),
        in_specs=[pre(GDN_W), pre(GDN_W), pre(GDN_W), pre(LANES),
                  tok(GDN_W), tok(GDN_W), tok(GDN_W), tok(LANES), tok(GDN_W),
                  pl.BlockSpec((1, HEAD_DIM), lambda i: (0, 0))],
        out_specs=tok(GDN_W),
        out_shape=jax.ShapeDtypeStruct((nb, seq, GDN_W), BF16),
        scratch_shapes=[
            pltpu.VMEM((nch, HEAD_DIM, HEAD_DIM), F32),
            pltpu.VMEM((chunks_per_step, nch, CHUNK + HEAD_DIM, HEAD_DIM), BF16),
            pltpu.VMEM((chunks_per_step, nch, CHUNK, HEAD_DIM), F32),
            pltpu.VMEM((chunks_per_step, nch, HEAD_DIM, HEAD_DIM), F32),
            pltpu.VMEM((chunks_per_step, nch, 8, HEAD_DIM), F32),
        ],
        compiler_params=pltpu.CompilerParams(
            dimension_semantics=("arbitrary",), vmem_limit_bytes=VMEM_LIMIT),
        name="gdn",
    )(pq, pk, pv, pgb, q, k, v, gb, z, gdn_norm_w)


def _mix_out_kernel(x_ref, ya_ref, yb_ref, woa_ref, wob_ref, fnw_ref, wr_ref,
                    br_ref, h_ref, xn_ref, route_ref, assign_ref, counts_ref, cnt_s):
    rows = x_ref.shape[0]
    sub = MIX_SUB_ROWS if rows % MIX_SUB_ROWS == 0 else rows
    blocks = [slice(r, r + sub) for r in range(0, rows, sub)]

    h1 = [x_ref[rs, :] + (_dot(ya_ref[rs, :], woa_ref[...]) + _dot(yb_ref[rs, :], wob_ref[...]))
          for rs in blocks]
    for rs, h1_i in zip(blocks, h1):
        h_ref[rs, :] = h1_i
    xn = [h1_i * lax.rsqrt(jnp.mean(h1_i * h1_i, axis=-1, keepdims=True) + EPS) * fnw_ref[...]
          for h1_i in h1]
    xn_bf = [xn_i.astype(BF16) for xn_i in xn]
    for rs, xn_i in zip(blocks, xn_bf):
        xn_ref[rs, :] = _pack_bf16_pairs(xn_i)
    logits = [_dot(xn_i, wr_ref[...]) + br_ref[...] for xn_i in xn_bf]

    lane = lax.broadcasted_iota(jnp.int32, (sub, LANES), 1).astype(F32)
    neg = -jnp.inf
    big = float(1 << 20)

    def argmax_first(vals):
        m = jnp.max(vals, axis=-1, keepdims=True)
        idx = jnp.min(jnp.where(vals == m, lane, big), axis=-1, keepdims=True)
        return m, idx

    def top_k(lg):
        grp = jnp.where(lane < N_GROUPS, lg, neg)
        g_max, g_sel = argmax_first(grp)
        p_grp = 1.0 / jnp.sum(jnp.exp(grp - g_max), axis=-1, keepdims=True)
        lo_lane = N_GROUPS + g_sel * EXPERTS_PER_GROUP
        ex = jnp.where((lane >= lo_lane) & (lane < lo_lane + EXPERTS_PER_GROUP), lg, neg)
        m1, i1 = argmax_first(ex)
        m2, i2 = argmax_first(jnp.where(lane == i1, neg, ex))
        e2 = jnp.exp(m2 - m1)
        return i1 - N_GROUPS, i2 - N_GROUPS, 1.0 / (1.0 + e2) * p_grp, e2 / (1.0 + e2) * p_grp

    picks = [top_k(lg) for lg in logits]

    @pl.when(pl.program_id(0) == 0)
    def _():
        cnt_s[...] = jnp.zeros(cnt_s.shape, F32)

    onehots = [((lane == e1).astype(F32), (lane == e2).astype(F32)) for e1, e2, _, _ in picks]
    earlier = (lax.broadcasted_iota(jnp.int32, (sub, sub), 0)
               > lax.broadcasted_iota(jnp.int32, (sub, sub), 1)).astype(BF16)
    within = [_dot(earlier, (oh1 + oh2).astype(BF16)) for oh1, oh2 in onehots]
    counts = cnt_s[...]
    for rs, (e1, e2, w1, w2), (oh1, oh2), within_i in zip(blocks, picks, onehots, within):
        before = within_i + counts
        rank1 = jnp.sum(before * oh1, axis=-1, keepdims=True)
        rank2 = jnp.sum(before * oh2, axis=-1, keepdims=True)
        counts = counts + jnp.sum(oh1 + oh2, axis=0, keepdims=True)
        route = jnp.zeros((sub, LANES), F32)
        for k, val in ((ROUTE_E1, e1), (ROUTE_E2, e2), (ROUTE_W1, w1), (ROUTE_W2, w2),
                       (ROUTE_RANK1, rank1), (ROUTE_RANK2, rank2)):
            route = jnp.where(lane == k, val, route)
        route_ref[rs, :] = route
        assign_ref[:, rs] = route.T[:ASSIGN_ROWS].astype(jnp.int32)
    cnt_s[...] = counts
    counts_ref[...] = counts


def _mix_out_call(x2, ya, yb, w, rows):
    tokens = x2.shape[0]
    tok = lambda width: pl.BlockSpec((rows, width), lambda i: (i, 0))
    full = lambda a: pl.BlockSpec(a.shape, lambda i: (0,) * a.ndim)
    consts = (w['w_out_a'], w['w_out_b'], w['ffn_norm_w'], w['w_route'], w['b_route'])
    return pl.pallas_call(
        _mix_out_kernel,
        grid=(tokens // rows,),
        in_specs=[tok(D_MODEL), tok(CONV_CH), tok(GDN_W)] + [full(a) for a in consts],
        out_specs=(tok(D_MODEL), tok(D_MODEL // 2), tok(LANES),
                   pl.BlockSpec((ASSIGN_ROWS, rows), lambda i: (0, i)),
                   pl.BlockSpec((1, LANES), lambda i: (0, 0))),
        out_shape=(jax.ShapeDtypeStruct((tokens, D_MODEL), F32),
                   jax.ShapeDtypeStruct((tokens, D_MODEL // 2), jnp.uint32),
                   jax.ShapeDtypeStruct((tokens, LANES), F32),
                   jax.ShapeDtypeStruct((ASSIGN_ROWS, tokens), jnp.int32),
                   jax.ShapeDtypeStruct((1, LANES), F32)),
        scratch_shapes=[pltpu.VMEM((1, LANES), F32)],
        compiler_params=pltpu.CompilerParams(
            dimension_semantics=("arbitrary",), vmem_limit_bytes=VMEM_LIMIT),
        name="mix_out",
    )(x2, ya, yb, *consts)


def _sc_row_move(table, idx, out_rows, chunk, scatter, name):
    n = idx.shape[0]
    n_src, width = table.shape
    workers = SC_CORES * SC_SUBCORES
    per_w = n // workers
    assert n % workers == 0 and per_w % (2 * chunk) == 0
    assert n_src % per_w == 0 or not scatter
    pairs = per_w // (2 * chunk)
    mesh = plsc.VectorSubcoreMesh(core_axis_name="c", subcore_axis_name="s",
                                  num_cores=SC_CORES, num_subcores=SC_SUBCORES)

    def body(table_hbm, idx_hbm, out_hbm, idx_v, buf_a, buf_b, sem_ra, sem_rb, sem_wa, sem_wb):
        base = (lax.axis_index("s") * SC_CORES + lax.axis_index("c")) * per_w
        src_base = lax.rem(base, n_src)
        pltpu.sync_copy(idx_hbm.at[pl.ds(base, per_w)], idx_v)

        def read(c, buf, sem):
            off = pl.multiple_of(c * chunk, chunk)
            src = (table_hbm.at[pl.ds(src_base + off, chunk)] if scatter
                   else table_hbm.at[idx_v.at[pl.ds(off, chunk)]])
            return pltpu.make_async_copy(src, buf, sem)

        def write(c, buf, sem):
            off = pl.multiple_of(c * chunk, chunk)
            dst = (out_hbm.at[idx_v.at[pl.ds(off, chunk)]] if scatter
                   else out_hbm.at[pl.ds(base + off, chunk)])
            return pltpu.make_async_copy(buf, dst, sem)

        read(0, buf_a, sem_ra).start()

        @pl.loop(0, pairs)
        def _(j):
            ca = 2 * j
            cb = ca + 1
            read(cb, buf_b, sem_rb).start()
            read(ca, buf_a, sem_ra).wait()
            write(ca, buf_a, sem_wa).start()
            read(cb, buf_b, sem_rb).wait()
            write(cb, buf_b, sem_wb).start()
            write(ca, buf_a, sem_wa).wait()

            @pl.when(j + 1 < pairs)
            def _():
                read(ca + 2, buf_a, sem_ra).start()

            write(cb, buf_b, sem_wb).wait()

    return pl.kernel(
        body,
        out_type=jax.ShapeDtypeStruct((out_rows, width), table.dtype),
        mesh=mesh,
        scratch_types=[pltpu.VMEM((per_w,), jnp.int32),
                       pltpu.VMEM((chunk, width), table.dtype),
                       pltpu.VMEM((chunk, width), table.dtype),
                       pltpu.SemaphoreType.DMA, pltpu.SemaphoreType.DMA,
                       pltpu.SemaphoreType.DMA, pltpu.SemaphoreType.DMA],
        name=name,
    )(table, idx)


def _experts_kernel(bexp_ref, bvalid_ref, nused_ref, x_ref, wg_hbm, wu_hbm, wd_hbm, y_ref,
                    wg_s, wu_s, wd_s, stage_g, stage_u, stage_d, sems, seq_s):
    n_used = nused_ref[0]
    n_last = bexp_ref.shape[0] - 1
    rows = x_ref.shape[0] // EXPERT_STEP_BLOCKS
    sub = EXPERT_SUB_ROWS if rows % EXPERT_SUB_ROWS == 0 else rows

    def weight_copies(e, slot):
        return (pltpu.make_async_copy(wg_hbm.at[e], stage_g.at[slot], sems.at[slot, 0]),
                pltpu.make_async_copy(wu_hbm.at[e], stage_u.at[slot], sems.at[slot, 1]),
                pltpu.make_async_copy(wd_hbm.at[e], stage_d.at[slot], sems.at[slot, 2]))

    def block(i, r0):
        expert = bexp_ref[i]
        first_of_expert = (i == 0) | (expert != bexp_ref[jnp.maximum(i - 1, 0)])

        @pl.when((i == 0) & (n_used > 0))
        def _():
            seq_s[0] = 0
            for copy in weight_copies(expert, 0):
                copy.start()

        @pl.when((i < n_used) & first_of_expert)
        def _():
            @pl.when(i > 0)
            def _():
                seq_s[0] = seq_s[0] + 1

            slot = seq_s[0] % 2
            nxt = lax.while_loop(
                lambda j: (j < n_used) & (bexp_ref[jnp.minimum(j, n_last)] == expert),
                lambda j: j + 1, i + 1)

            @pl.when(nxt < n_used)
            def _():
                for copy in weight_copies(bexp_ref[jnp.minimum(nxt, n_last)], 1 - slot):
                    copy.start()

            for copy in weight_copies(expert, slot):
                copy.wait()
            wg_s[...] = stage_g[slot].astype(BF16)
            wu_s[...] = stage_u[slot].astype(BF16)
            wd_s[...] = stage_d[slot].astype(BF16)

        valid = bvalid_ref[i]

        def ffn(n_sub):
            blocks = [slice(r0 + k * sub, r0 + (k + 1) * sub) for k in range(n_sub)]
            row = lax.broadcasted_iota(jnp.int32, (sub, x_ref.shape[1]), 0)
            xb = [jnp.concatenate(_unpack_bf16_pairs(
                      jnp.where(row + (rs.start - r0) < valid, x_ref[rs, :], jnp.uint32(0))),
                      axis=1).astype(BF16) for rs in blocks]
            gate = [_dot(xb_i, wg_s[...]) for xb_i in xb]
            up = [_dot(xb_i, wu_s[...]) for xb_i in xb]
            hid = [((g_i * _sigmoid(g_i)) * u_i).astype(BF16) for g_i, u_i in zip(gate, up)]
            y = [_dot(h_i, wd_s[...]) for h_i in hid]
            for rs, y_i in zip(blocks, y):
                y_ref[rs, :] = _pack_bf16_pairs(y_i.astype(BF16))
            rest = rows - n_sub * sub
            if rest:
                y_ref[r0 + n_sub * sub:r0 + rows, :] = jnp.zeros((rest, y_ref.shape[1]), jnp.uint32)

        tail_sub = (rows // sub + 1) // 2
        pl.when((i < n_used) & (valid > tail_sub * sub))(lambda: ffn(rows // sub))
        pl.when((i < n_used) & (valid <= tail_sub * sub))(lambda: ffn(tail_sub))
        pl.when(i >= n_used)(lambda: ffn(0))

    for j in range(EXPERT_STEP_BLOCKS):
        block(pl.program_id(0) * EXPERT_STEP_BLOCKS + j, j * rows)


def _experts_call(block_expert, block_valid, n_used, x_rows, w_gate, w_up, w_down, rows):
    n_blocks = block_expert.shape[0]
    step_rows = EXPERT_STEP_BLOCKS * rows
    hbm = pl.BlockSpec(memory_space=pl.ANY)
    grid_spec = pltpu.PrefetchScalarGridSpec(
        num_scalar_prefetch=3,
        grid=(n_blocks // EXPERT_STEP_BLOCKS,),
        in_specs=[pl.BlockSpec((step_rows, D_MODEL // 2), lambda i, be, bv, nu: (i, 0)),
                  hbm, hbm, hbm],
        out_specs=pl.BlockSpec((step_rows, D_MODEL // 2), lambda i, be, bv, nu: (i, 0)),
        scratch_shapes=[pltpu.VMEM((D_MODEL, D_EXPERT), BF16),
                        pltpu.VMEM((D_MODEL, D_EXPERT), BF16),
                        pltpu.VMEM((D_EXPERT, D_MODEL), BF16),
                        pltpu.VMEM((2, D_MODEL, D_EXPERT), F32),
                        pltpu.VMEM((2, D_MODEL, D_EXPERT), F32),
                        pltpu.VMEM((2, D_EXPERT, D_MODEL), F32),
                        pltpu.SemaphoreType.DMA((2, 3)),
                        pltpu.SMEM((1,), jnp.int32)],
    )
    return pl.pallas_call(
        _experts_kernel,
        grid_spec=grid_spec,
        out_shape=jax.ShapeDtypeStruct((n_blocks * rows, D_MODEL // 2), jnp.uint32),
        compiler_params=pltpu.CompilerParams(
            dimension_semantics=("arbitrary",), vmem_limit_bytes=VMEM_LIMIT),
        name="experts",
    )(block_expert, block_valid, n_used, x_rows, w_gate, w_up, w_down)


def _combine_kernel(y1_ref, y2_ref, h_ref, route_ref, fw_ref, out_ref):
    route = route_ref[...]
    y1 = jnp.concatenate(_unpack_bf16_pairs(y1_ref[...]), axis=1)
    y2 = jnp.concatenate(_unpack_bf16_pairs(y2_ref[...]), axis=1)
    moe = route[:, ROUTE_W1:ROUTE_W1 + 1] * y1 + route[:, ROUTE_W2:ROUTE_W2 + 1] * y2
    h2 = h_ref[...] + moe
    out_ref[...] = h2 * lax.rsqrt(jnp.mean(h2 * h2, axis=-1, keepdims=True) + EPS) * fw_ref[...]


def _combine_call(y_tok, h1, route, final_w, rows):
    tokens = h1.shape[0]
    steps = tokens // rows
    tok = lambda width: pl.BlockSpec((rows, width), lambda i: (i, 0))
    return pl.pallas_call(
        _combine_kernel,
        grid=(steps,),
        in_specs=[tok(D_MODEL // 2), pl.BlockSpec((rows, D_MODEL // 2), lambda i: (i + steps, 0)),
                  tok(D_MODEL), tok(LANES), pl.BlockSpec((1, D_MODEL), lambda i: (0, 0))],
        out_specs=tok(D_MODEL),
        out_shape=jax.ShapeDtypeStruct((tokens, D_MODEL), F32),
        compiler_params=pltpu.CompilerParams(
            dimension_semantics=("arbitrary",), vmem_limit_bytes=VMEM_LIMIT),
        name="combine",
    )(y_tok, y_tok, h1, route, final_w)


def _dispatch_tables(assign, counts, tokens, rows):
    n_blocks = pl.cdiv((tokens * TOP_K + N_EXPERTS * (rows - 1)) // rows,
                       EXPERT_STEP_BLOCKS) * EXPERT_STEP_BLOCKS
    counts = counts[0, :N_EXPERTS].astype(jnp.int32)
    padded = (counts + rows - 1) // rows * rows
    pad_end = jnp.cumsum(padded)
    pad_start = pad_end - padded
    experts = assign[ROUTE_E1:ROUTE_E2 + 1]
    ranks = assign[ROUTE_RANK1:ROUTE_RANK2 + 1]
    is_expert = experts[..., None] == jnp.arange(N_EXPERTS, dtype=jnp.int32)
    dest = (jnp.sum(jnp.where(is_expert, pad_start, 0), axis=-1) + ranks).reshape(-1)
    block_start = jnp.arange(n_blocks, dtype=jnp.int32) * rows
    in_run = (block_start[:, None] >= pad_start[None, :]) & (block_start[:, None] < pad_end[None, :])
    lookup = lambda table: jnp.sum(jnp.where(in_run, table[None, :], 0), axis=1)
    block_expert = lookup(jnp.arange(N_EXPERTS, dtype=jnp.int32))
    block_valid = jnp.clip(lookup(pad_start + counts) - block_start, 0, rows)
    n_used = (pad_end[-1] // rows).astype(jnp.int32).reshape(1)
    return dest, block_expert, block_valid.astype(jnp.int32), n_used, n_blocks


def _prepare_weights(mix_norm_w, w_in, conv_mix_w, conv_mix_norm_w, qkv_conv_w, a_log, dt_bias,
                     gdn_norm_w, w_out, ffn_norm_w, w_group, b_group, w_router, b_router):
    pad_lanes = lambda v: jnp.pad(v.reshape(1, -1), ((0, 0), (0, LANES - v.size)))
    w_in_bf = w_in.astype(BF16)
    w_ab = jnp.pad(w_in_bf[:, Z_COL0 + GDN_W:], ((0, 0), (0, LANES - 2 * HEADS)))
    grp = jnp.arange(PROJ_COLS) // CONV_GROUP_W
    gmat = jnp.where(grp[:, None] == grp[None, :], 1.0 / CONV_GROUP_W, 0.0).astype(BF16)
    w_route = jnp.concatenate([w_group, w_router.reshape(D_MODEL, N_EXPERTS)], axis=1)
    w_route = jnp.pad(w_route, ((0, 0), (0, LANES - w_route.shape[1])))
    return dict(
        mix_norm_w=mix_norm_w.reshape(1, -1),
        w_a=jnp.concatenate([w_in_bf[:, part * CONV_CH + i * PROJ_COLS:][:, :PROJ_COLS]
                             for i in range(CONV_CH // PROJ_COLS) for part in range(3)], axis=1),
        w_in=w_in_bf,
        w_ab=w_ab,
        conv_mix_w=conv_mix_w,
        conv_mix_norm_w=conv_mix_norm_w.reshape(1, -1),
        gmat=gmat,
        qkv_conv_w=qkv_conv_w,
        a_log=pad_lanes(a_log),
        dt_bias=pad_lanes(dt_bias),
        gdn_norm_w=gdn_norm_w.reshape(1, -1),
        w_out_a=w_out[:CONV_CH].astype(BF16),
        w_out_b=w_out[CONV_CH:].astype(BF16),
        ffn_norm_w=ffn_norm_w.reshape(1, -1),
        w_route=w_route.astype(BF16),
        b_route=pad_lanes(jnp.concatenate([b_group, b_router.reshape(-1)])),
    )


def _tile(n, preferred):
    return preferred if n % preferred == 0 else n


def kernel(x, meta_tokens, mix_norm_w, w_in, conv_mix_w, conv_mix_norm_w, qkv_conv_w, a_log,
           dt_bias, gdn_norm_w, w_out, ffn_norm_w, w_group, b_group, w_router, b_router, w_gate,
           w_up, w_down, final_norm_w):
    assert mix_norm_w.shape[0] == 1, "single-layer kernel"
    batch, seq, _ = x.shape
    assert seq % CHUNK == 0
    w = _prepare_weights(mix_norm_w[0], w_in[0], conv_mix_w[0], conv_mix_norm_w[0], qkv_conv_w[0],
                         a_log[0], dt_bias[0], gdn_norm_w[0], w_out[0], ffn_norm_w[0], w_group[0],
                         b_group[0], w_router[0], b_router[0])

    prefix = jnp.concatenate([jnp.zeros((CHUNK - N_META, D_MODEL), x.dtype),
                              meta_tokens.astype(x.dtype)], axis=0)[None]
    zero_cu = jnp.zeros((HIST, CONV_CH), F32)
    zero_qkv = jnp.zeros((HIST, 3 * GDN_W), F32)
    _, pq, pk, pv, _, pgb, tail_cu, tail_qkv = _proj_call(prefix, zero_cu, zero_qkv, w, CHUNK)

    ya, q, k, v, z, gb, _, _ = _proj_call(x, tail_cu[0], tail_qkv[0], w, _tile(seq, PROJ_ROWS))
    yb = _gdn_call(pq, pk, pv, pgb, q, k, v, gb, z, w['gdn_norm_w'], _tile(seq // CHUNK, GDN_CHUNKS))

    tokens = batch * seq
    flat = lambda a: a.reshape(tokens, a.shape[-1])
    h1, xn2, route, assign, counts = _mix_out_call(flat(x), flat(ya), flat(yb), w,
                                           _tile(tokens, MIX_ROWS))

    dest, block_expert, block_valid, n_used, n_blocks = _dispatch_tables(assign, counts, tokens,
                                                                         EXPERT_ROWS)
    x_rows = _sc_row_move(xn2, dest, n_blocks * EXPERT_ROWS, SC_DISPATCH_CHUNK, True,
                          "dispatch_scatter")
    y_rows = _experts_call(block_expert, block_valid, n_used, x_rows, w_gate[0], w_up[0],
                           w_down[0], EXPERT_ROWS)
    y_tok = _sc_row_move(y_rows, dest, TOP_K * tokens, SC_COMBINE_CHUNK, False, "combine_gather")
    out = _combine_call(y_tok, h1, route, final_norm_w.reshape(1, -1), _tile(tokens, COMBINE_ROWS))
    return out.reshape(batch, seq, D_MODEL)
```

```python
import functools

import jax
import jax.numpy as jnp
from jax import lax
from jax.experimental import pallas as pl
from jax.experimental.pallas import tpu as pltpu
from jax.experimental.pallas import tpu_sc as plsc

F32 = jnp.float32
BF16 = jnp.bfloat16
EPS = 1e-6

D_MODEL = 1024
N_META = 16
CONV_CH = 512
CONV_GROUP_W = 64
HEADS = 4
HEAD_DIM = 128
GDN_W = HEADS * HEAD_DIM
QKV_COL0 = 3 * CONV_CH
Z_COL0 = QKV_COL0 + 3 * GDN_W
CHUNK = 64
N_GROUPS = 4
EXPERTS_PER_GROUP = 8
N_EXPERTS = N_GROUPS * EXPERTS_PER_GROUP
TOP_K = 2
D_EXPERT = 512
LANES = 128
ROUTE_E1, ROUTE_E2, ROUTE_RANK1, ROUTE_RANK2, ROUTE_W1, ROUTE_W2 = range(6)
ASSIGN_ROWS = 8
HIST = 8

PROJ_ROWS = 512
PROJ_COLS = 256
PROJ_SUB_ROWS = 128
GDN_CHUNKS = 8
GDN_GROUP = 4
MIX_ROWS = 1024
MIX_SUB_ROWS = 128
EXPERT_ROWS = 256
EXPERT_SUB_ROWS = 128
EXPERT_STEP_BLOCKS = 4
COMBINE_ROWS = 1024
VMEM_LIMIT = 56 * 1024 * 1024
SC_CORES = 2
SC_SUBCORES = 16
SC_DISPATCH_CHUNK = 64
SC_COMBINE_CHUNK = 64


def _dot(a, b):
    return jnp.dot(a, b, preferred_element_type=F32)


def _dot_nt(a, b):
    return lax.dot_general(a, b, (((1,), (1,)), ((), ())), preferred_element_type=F32)


def _dot_tn(a, b):
    return lax.dot_general(a, b, (((0,), (0,)), ((), ())), preferred_element_type=F32)


def _split_bf16(x):
    hi = x.astype(BF16)
    lo = (x - hi.astype(F32)).astype(BF16)
    return hi, lo


def _sigmoid(x):
    return 1.0 / (1.0 + jnp.exp(-x))


def _pack_bf16_pairs(x_bf16):
    bits = pltpu.bitcast(x_bf16.astype(F32), jnp.uint32)
    n = x_bf16.shape[1] // 2
    return (bits[:, :n] >> 16) | (bits[:, n:] & jnp.uint32(0xFFFF0000))


def _unpack_bf16_pairs(packed):
    return (pltpu.bitcast(packed << 16, F32),
            pltpu.bitcast(packed & jnp.uint32(0xFFFF0000), F32))


def _proj_kernel(x_ref, hcu_ref, hqkv_ref, nw_ref, wa_ref, wi_ref, wab_ref, cmw_ref,
                 cmn_ref, gmat_ref, qcw_ref, alog_ref, dtb_ref,
                 ya_ref, q_ref, k_ref, v_ref, z_ref, gb_ref, tcu_ref, tqkv_ref,
                 cu_s, qkv_s, xn_s):
    rows = x_ref.shape[0]
    sub = PROJ_SUB_ROWS if rows % PROJ_SUB_ROWS == 0 else rows
    blocks = [slice(r, r + sub) for r in range(0, rows, sub)]

    @pl.when(pl.program_id(1) == 0)
    def _():
        cu_s[0:HIST, :] = hcu_ref[...]
        qkv_s[0:HIST, :] = hqkv_ref[...]

    x = x_ref[...]
    ms = jnp.mean(x * x, axis=-1, keepdims=True)
    xn_s[...] = (x * lax.rsqrt(ms + EPS) * nw_ref[...]).astype(BF16)

    def causal_conv(buf, cur, w_ref, cols, tail_ref):
        taps = w_ref.shape[0]
        acc = pltpu.roll(cur, taps - 1, axis=0) * w_ref[0:1, cols]
        for j in range(1, taps - 1):
            acc = acc + pltpu.roll(cur, taps - 1 - j, axis=0) * w_ref[j:j + 1, cols]
        acc = acc + cur * w_ref[taps - 1:taps, cols]
        buf[HIST:2 * HIST, cols] = cur[:HIST]
        seam = buf[pl.ds(HIST - taps + 1, HIST), cols] * w_ref[0:1, cols]
        for j in range(1, taps):
            seam = seam + buf[pl.ds(HIST - taps + 1 + j, HIST), cols] * w_ref[j:j + 1, cols]
        tail = cur[sub - HIST:]
        buf[0:HIST, cols] = tail
        tail_ref[:, cols] = tail
        return jnp.concatenate([seam, acc[HIST:]], axis=0)

    def mixer_a_tail(i, rs, pa):
        cols = slice(i * PROJ_COLS, (i + 1) * PROJ_COLS)
        cu = pa[:, PROJ_COLS:2 * PROJ_COLS] * pa[:, 2 * PROJ_COLS:]
        ya = pa[:, :PROJ_COLS] * causal_conv(cu_s, cu, cmw_ref, cols, tcu_ref)
        msg = _dot((ya * ya).astype(BF16), gmat_ref[...])
        ya_ref[rs, cols] = (ya * lax.rsqrt(msg + EPS) * cmn_ref[:, cols]).astype(BF16)

    heads_per_chunk = PROJ_COLS // HEAD_DIM

    def qkv_tail(i, rs, pq):
        cols = slice(i * PROJ_COLS, (i + 1) * PROJ_COLS)
        c = causal_conv(qkv_s, pq, qcw_ref, cols, tqkv_ref)
        c = c * _sigmoid(c)
        part, first_head = divmod(i * heads_per_chunk, HEADS)
        for j in range(heads_per_chunk):
            ch = c[:, j * HEAD_DIM:(j + 1) * HEAD_DIM]
            sl = slice((first_head + j) * HEAD_DIM, (first_head + j + 1) * HEAD_DIM)
            if part == 0:
                norm = lax.rsqrt(jnp.sum(ch * ch, axis=-1, keepdims=True) + EPS)
                q_ref[rs, sl] = ch * norm * (HEAD_DIM ** -0.5)
            elif part == 1:
                k_ref[rs, sl] = ch * lax.rsqrt(jnp.sum(ch * ch, axis=-1, keepdims=True) + EPS)
            else:
                v_ref[rs, sl] = ch

    def z_tail(rs, pz):
        z_ref[rs, :] = pz

    def decay_beta_tail(rs, ab):
        sp_in = ab + dtb_ref[...]
        softplus = jnp.maximum(sp_in, 0.0) + jnp.log1p(jnp.exp(-jnp.abs(sp_in)))
        g = -jnp.exp(alog_ref[...]) * softplus
        lane = lax.broadcasted_iota(jnp.int32, ab.shape, 1)
        gb_ref[rs, :] = jnp.where(lane < HEADS, g, jnp.where(lane < 2 * HEADS, _sigmoid(ab), 0.0))

    def matmul(w_ref, w_cols, rs):
        return _dot(xn_s[rs, :], w_ref[:, w_cols])

    stages = []
    for i in range(CONV_CH // PROJ_COLS):
        w_cols = slice(3 * i * PROJ_COLS, 3 * (i + 1) * PROJ_COLS)
        stages += [(functools.partial(matmul, wa_ref, w_cols, rs),
                    functools.partial(mixer_a_tail, i, rs)) for rs in blocks]
    for i in range(3 * GDN_W // PROJ_COLS):
        w_cols = slice(QKV_COL0 + i * PROJ_COLS, QKV_COL0 + (i + 1) * PROJ_COLS)
        stages += [(functools.partial(matmul, wi_ref, w_cols, rs),
                    functools.partial(qkv_tail, i, rs)) for rs in blocks]
    stages += [(functools.partial(matmul, wi_ref, slice(Z_COL0, Z_COL0 + GDN_W), rs),
                functools.partial(z_tail, rs)) for rs in blocks]
    stages += [(functools.partial(matmul, wab_ref, slice(None), rs),
                functools.partial(decay_beta_tail, rs)) for rs in blocks]
    pending = None
    for issue, tail in stages:
        res = issue()
        if pending is not None:
            pending()
        pending = functools.partial(tail, res)
    pending()


def _proj_prefix_kernel(x_ref, hcu_ref, hqkv_ref, nw_ref, w32_ref, cmw_ref, cmn_ref, gmat_ref, qcw_ref,
                        alog_ref, dtb_ref,
                        ya_ref, q_ref, k_ref, v_ref, z_ref, gb_ref, tcu_ref, tqkv_ref,
                        wa_ref, wi_ref, wab_ref, cu_s, qkv_s, xn_s):
    for i in range(CONV_CH // PROJ_COLS):
        for part in range(3):
            src = part * CONV_CH + i * PROJ_COLS
            dst = (3 * i + part) * PROJ_COLS
            wa_ref[:, dst:dst + PROJ_COLS] = w32_ref[:, src:src + PROJ_COLS].astype(BF16)
    for c in range(0, Z_COL0 + GDN_W, PROJ_COLS):
        wi_ref[:, c:c + PROJ_COLS] = w32_ref[:, c:c + PROJ_COLS].astype(BF16)
    wab_ref[...] = jnp.zeros(wab_ref.shape, BF16)
    wab_ref[:, :2 * HEADS] = w32_ref[:, Z_COL0 + GDN_W:].astype(BF16)
    _proj_kernel(x_ref, hcu_ref, hqkv_ref, nw_ref, wa_ref, wi_ref, wab_ref, cmw_ref, cmn_ref,
                 gmat_ref, qcw_ref, alog_ref, dtb_ref, ya_ref, q_ref, k_ref, v_ref, z_ref, gb_ref,
                 tcu_ref, tqkv_ref, cu_s, qkv_s, xn_s)


def _proj_call(x3, hist_cu, hist_qkv, w, rows, prefix=False):
    nb, seq, _ = x3.shape
    nt = seq // rows
    tok = lambda width: pl.BlockSpec((None, rows, width), lambda b, t: (b, t, 0))
    full = lambda a: pl.BlockSpec(a.shape, lambda b, t: (0,) * a.ndim,
                                  pipeline_mode=pl.Buffered(1))
    tail = lambda width: pl.BlockSpec((None, HIST, width), lambda b, t: (b, 0, 0))
    weights = (w['w_in'],) if prefix else (w['w_a'], w['w_in_bf'], w['w_ab'])
    consts = (hist_cu, hist_qkv, w['mix_norm_w']) + weights + (
        w['conv_mix_w'], w['conv_mix_norm_w'], w['gmat'], w['qkv_conv_w'], w['a_log'], w['dt_bias'])
    out_shape = [
        jax.ShapeDtypeStruct((nb, seq, CONV_CH), BF16),
        jax.ShapeDtypeStruct((nb, seq, GDN_W), F32),
        jax.ShapeDtypeStruct((nb, seq, GDN_W), F32),
        jax.ShapeDtypeStruct((nb, seq, GDN_W), F32),
        jax.ShapeDtypeStruct((nb, seq, GDN_W), F32),
        jax.ShapeDtypeStruct((nb, seq, LANES), F32),
        jax.ShapeDtypeStruct((nb, HIST, CONV_CH), F32),
        jax.ShapeDtypeStruct((nb, HIST, 3 * GDN_W), F32),
    ]
    out_specs = [tok(CONV_CH), tok(GDN_W), tok(GDN_W), tok(GDN_W), tok(GDN_W), tok(LANES),
                 tail(CONV_CH), tail(3 * GDN_W)]
    if prefix:
        assert (nb, nt) == (1, 1)
        prepared = [jax.ShapeDtypeStruct((D_MODEL, QKV_COL0), BF16),
                    jax.ShapeDtypeStruct((D_MODEL, Z_COL0 + GDN_W), BF16),
                    jax.ShapeDtypeStruct((D_MODEL, LANES), BF16)]
        out_shape += prepared
        out_specs += [pl.BlockSpec(p.shape, lambda b, t: (0, 0)) for p in prepared]
    return pl.pallas_call(
        _proj_prefix_kernel if prefix else _proj_kernel,
        grid=(nb, nt),
        in_specs=[tok(D_MODEL)] + [full(a) for a in consts],
        out_specs=tuple(out_specs),
        out_shape=tuple(out_shape),
        scratch_shapes=[pltpu.VMEM((2 * HIST, CONV_CH), F32),
                        pltpu.VMEM((2 * HIST, 3 * GDN_W), F32),
                        pltpu.VMEM((rows, D_MODEL), BF16)],
        compiler_params=pltpu.CompilerParams(
            dimension_semantics=("arbitrary", "arbitrary"), vmem_limit_bytes=VMEM_LIMIT),
        name="proj_prefix" if prefix else "proj",
    )(x3, *consts)


def _chunk_masks():
    row = lax.broadcasted_iota(jnp.int32, (CHUNK, CHUNK), 0)
    col = lax.broadcasted_iota(jnp.int32, (CHUNK, CHUNK), 1)
    incl = row >= col
    strict = row > col
    levels = []
    n = 1
    while n < CHUNK:
        levels.append((row // (2 * n) == col // (2 * n)) & ((row // n) % 2 == 1) & ((col // n) % 2 == 0))
        n *= 2
    return incl, strict, levels


def _chunk_cumsum(gb_blk, incl):
    tri = incl.astype(BF16)
    hi, lo = _split_bf16(gb_blk)
    return _dot(tri, hi) + _dot(tri, lo)


def _chunk_transforms(chains, masks, state_only):
    incl, strict, levels = masks
    eye = (lax.broadcasted_iota(jnp.int32, (CHUNK, CHUNK), 0)
           == lax.broadcasted_iota(jnp.int32, (CHUNK, CHUNK), 1)).astype(F32)
    decay = [jnp.exp(jnp.where(incl, gc_col - gc_row, -jnp.inf))
             for (_, _, _, _, gc_col, gc_row, _) in chains]
    kb = [kh * beta for (_, kh, _, beta, _, _, _) in chains]
    k_bf = [kh.astype(BF16) for (_, kh, _, _, _, _, _) in chains]
    a_mat = [jnp.where(strict, _dot_nt(kb_i.astype(BF16), k_i) * d_i, 0.0)
             for kb_i, k_i, d_i in zip(kb, k_bf, decay)]
    t_inv = [eye - jnp.where(levels[0], a_i, 0.0) for a_i in a_mat]
    for lvl in levels[1:]:
        t_bf = [t_i.astype(BF16) for t_i in t_inv]
        m1 = [_dot(jnp.where(lvl, a_i, 0.0).astype(BF16), t_i) for a_i, t_i in zip(a_mat, t_bf)]
        t_inv = [t_i - _dot(tb_i, m_i.astype(BF16)) for t_i, tb_i, m_i in zip(t_inv, t_bf, m1)]
    rhs = [jnp.concatenate([vh * beta, kb_i * jnp.exp(gc_col)], axis=1)
           for (_, _, vh, beta, gc_col, _, _), kb_i in zip(chains, kb)]
    uw = [_dot(t_i.astype(BF16), r_i.astype(BF16)).astype(BF16)
          for t_i, r_i in zip(t_inv, rhs)]
    kd = [kh * jnp.exp(g_last - gc_col) for (_, kh, _, _, gc_col, _, g_last) in chains]
    pn = [_dot_tn(kd_i.astype(BF16), uw_i) for kd_i, uw_i in zip(kd, uw)]
    if state_only:
        return [pn_i[:, :HEAD_DIM] for pn_i in pn]
    intra = [jnp.where(incl, _dot_nt(qh.astype(BF16), k_i) * d_i, 0.0)
             for (qh, _, _, _, _, _, _), k_i, d_i in zip(chains, k_bf, decay)]
    iuw = [_dot(in_i.astype(BF16), uw_i) for in_i, uw_i in zip(intra, uw)]
    out = []
    for (qh, _, _, _, gc_col, _, g_last), pn_i, iuw_i in zip(chains, pn, iuw):
        q_part = qh * jnp.exp(gc_col) - iuw_i[:, HEAD_DIM:]
        out.append((q_part, pn_i[:, HEAD_DIM:], iuw_i[:, :HEAD_DIM], pn_i[:, :HEAD_DIM],
                    jnp.exp(g_last)))
    return out


def _gdn_kernel(pq_ref, pk_ref, pv_ref, pgb_ref, q_ref, k_ref, v_ref, gb_ref, z_ref, gnw_ref, yb_ref,
                s_s, qp_s, op_s, n_s, a_s):
    nb = q_ref.shape[0]
    n_chunks = q_ref.shape[1] // CHUNK
    group = GDN_GROUP if n_chunks % GDN_GROUP == 0 else 1
    masks = _chunk_masks()

    def chains_of(gb_blk, q_blk, k_blk, v_blk):
        gc = _chunk_cumsum(gb_blk, masks[0])
        gc_t = gc.T
        res = []
        for h in range(HEADS):
            sl = slice(h * HEAD_DIM, (h + 1) * HEAD_DIM)
            res.append((q_blk(sl), k_blk(sl), v_blk(sl), gb_blk[:, HEADS + h:HEADS + h + 1],
                        gc[:, h:h + 1], gc_t[h:h + 1, :CHUNK], gc[CHUNK - 1:CHUNK, h:h + 1]))
        return res

    @pl.when(pl.program_id(0) == 0)
    def _():
        chains = chains_of(pgb_ref[...], lambda sl: pq_ref[:, sl], lambda sl: pk_ref[:, sl],
                           lambda sl: pv_ref[:, sl])
        for h, n_mat in enumerate(_chunk_transforms(chains, masks, True)):
            for b in range(nb):
                s_s[b * HEADS + h] = n_mat

    def transform_group(gi, carry):
        chains, where = [], []
        for cc in range(group):
            c = gi * group + cc
            rows = pl.ds(pl.multiple_of(c * CHUNK, CHUNK), CHUNK)
            for b in range(nb):
                chains += chains_of(gb_ref[b, rows, :], lambda sl: q_ref[b, rows, sl],
                                    lambda sl: k_ref[b, rows, sl], lambda sl: v_ref[b, rows, sl])
                where += [(c, b * HEADS + h) for h in range(HEADS)]
        for (c, ch), (q_part, p_mat, o_part, n_mat, a) in zip(
                where, _chunk_transforms(chains, masks, False)):
            qp_s[c, ch, 0:CHUNK, :] = q_part.astype(BF16)
            qp_s[c, ch, CHUNK:, :] = p_mat.astype(BF16)
            op_s[c, ch] = o_part
            n_s[c, ch] = n_mat
            a_s[c, ch] = jnp.broadcast_to(a, (8, HEAD_DIM))
        return carry

    lax.fori_loop(0, n_chunks // group, transform_group, 0)

    def scan_chunk(c, carry):
        r0 = pl.multiple_of(c * CHUNK, CHUNK)
        for b in range(nb):
            for h in range(HEADS):
                ch = b * HEADS + h
                s = s_s[ch]
                r = _dot(qp_s[c, ch], s.astype(BF16))
                s_s[ch] = a_s[c, ch][0:1, :] * s - r[CHUNK:] + n_s[c, ch]
                rows, cols = pl.ds(r0, CHUNK), slice(h * HEAD_DIM, (h + 1) * HEAD_DIM)
                o = r[:CHUNK] + op_s[c, ch]
                zh = z_ref[b, rows, cols]
                on = o * lax.rsqrt(jnp.mean(o * o, axis=-1, keepdims=True) + EPS) * gnw_ref[...]
                yb_ref[b, rows, cols] = (on * (zh * _sigmoid(zh))).astype(BF16)
        return carry

    lax.fori_loop(0, n_chunks, scan_chunk, 0, unroll=4 if n_chunks % 4 == 0 else 1)


def _gdn_call(pq, pk, pv, pgb, q, k, v, gb, z, gdn_norm_w, chunks_per_step):
    nb, seq, _ = q.shape
    rows = chunks_per_step * CHUNK
    steps = seq // rows
    tok = lambda width: pl.BlockSpec((nb, rows, width), lambda i: (0, i, 0))
    pre = lambda width: pl.BlockSpec((None, CHUNK, width), lambda i: (0, 0, 0))
    nch = nb * HEADS
    return pl.pallas_call(
        _gdn_kernel,
        grid=(steps,),
        in_specs=[pre(GDN_W), pre(GDN_W), pre(GDN_W), pre(LANES),
                  tok(GDN_W), tok(GDN_W), tok(GDN_W), tok(LANES), tok(GDN_W),
                  pl.BlockSpec((1, HEAD_DIM), lambda i: (0, 0))],
        out_specs=tok(GDN_W),
        out_shape=jax.ShapeDtypeStruct((nb, seq, GDN_W), BF16),
        scratch_shapes=[
            pltpu.VMEM((nch, HEAD_DIM, HEAD_DIM), F32),
            pltpu.VMEM((chunks_per_step, nch, CHUNK + HEAD_DIM, HEAD_DIM), BF16),
            pltpu.VMEM((chunks_per_step, nch, CHUNK, HEAD_DIM), F32),
            pltpu.VMEM((chunks_per_step, nch, HEAD_DIM, HEAD_DIM), F32),
            pltpu.VMEM((chunks_per_step, nch, 8, HEAD_DIM), F32),
        ],
        compiler_params=pltpu.CompilerParams(
            dimension_semantics=("arbitrary",), vmem_limit_bytes=VMEM_LIMIT),
        name="gdn",
    )(pq, pk, pv, pgb, q, k, v, gb, z, gdn_norm_w)


def _mix_out_kernel(x_ref, ya_ref, yb_ref, woa_ref, wob_ref, fnw_ref, wr_ref,
                    br_ref, h_ref, xn_ref, route_ref, assign_ref, counts_ref, cnt_s):
    rows = x_ref.shape[0]
    sub = MIX_SUB_ROWS if rows % MIX_SUB_ROWS == 0 else rows
    blocks = [slice(r, r + sub) for r in range(0, rows, sub)]

    h1 = [x_ref[rs, :] + (_dot(ya_ref[rs, :], woa_ref[...]) + _dot(yb_ref[rs, :], wob_ref[...]))
          for rs in blocks]
    for rs, h1_i in zip(blocks, h1):
        h_ref[rs, :] = h1_i
    xn = [h1_i * lax.rsqrt(jnp.mean(h1_i * h1_i, axis=-1, keepdims=True) + EPS) * fnw_ref[...]
          for h1_i in h1]
    xn_bf = [xn_i.astype(BF16) for xn_i in xn]
    for rs, xn_i in zip(blocks, xn_bf):
        xn_ref[rs, :] = _pack_bf16_pairs(xn_i)
    logits = [_dot(xn_i, wr_ref[...]) + br_ref[...] for xn_i in xn_bf]

    lane = lax.broadcasted_iota(jnp.int32, (sub, LANES), 1).astype(F32)
    neg = -jnp.inf
    big = float(1 << 20)

    def argmax_first(vals):
        m = jnp.max(vals, axis=-1, keepdims=True)
        idx = jnp.min(jnp.where(vals == m, lane, big), axis=-1, keepdims=True)
        return m, idx

    def top_k(lg):
        grp = jnp.where(lane < N_GROUPS, lg, neg)
        g_max, g_sel = argmax_first(grp)
        p_grp = 1.0 / jnp.sum(jnp.exp(grp - g_max), axis=-1, keepdims=True)
        lo_lane = N_GROUPS + g_sel * EXPERTS_PER_GROUP
        ex = jnp.where((lane >= lo_lane) & (lane < lo_lane + EXPERTS_PER_GROUP), lg, neg)
        m1, i1 = argmax_first(ex)
        m2, i2 = argmax_first(jnp.where(lane == i1, neg, ex))
        e2 = jnp.exp(m2 - m1)
        return i1 - N_GROUPS, i2 - N_GROUPS, 1.0 / (1.0 + e2) * p_grp, e2 / (1.0 + e2) * p_grp

    picks = [top_k(lg) for lg in logits]

    @pl.when(pl.program_id(0) == 0)
    def _():
        cnt_s[...] = jnp.zeros(cnt_s.shape, F32)

    onehots = [((lane == e1).astype(F32), (lane == e2).astype(F32)) for e1, e2, _, _ in picks]
    earlier = (lax.broadcasted_iota(jnp.int32, (sub, sub), 0)
               > lax.broadcasted_iota(jnp.int32, (sub, sub), 1)).astype(BF16)
    within = [_dot(earlier, (oh1 + oh2).astype(BF16)) for oh1, oh2 in onehots]
    counts = cnt_s[...]
    for rs, (e1, e2, w1, w2), (oh1, oh2), within_i in zip(blocks, picks, onehots, within):
        before = within_i + counts
        rank1 = jnp.sum(before * oh1, axis=-1, keepdims=True)
        rank2 = jnp.sum(before * oh2, axis=-1, keepdims=True)
        counts = counts + jnp.sum(oh1 + oh2, axis=0, keepdims=True)
        route = jnp.zeros((sub, LANES), F32)
        for k, val in ((ROUTE_E1, e1), (ROUTE_E2, e2), (ROUTE_W1, w1), (ROUTE_W2, w2),
                       (ROUTE_RANK1, rank1), (ROUTE_RANK2, rank2)):
            route = jnp.where(lane == k, val, route)
        route_ref[rs, :] = route
        assign_ref[:, rs] = route.T[:ASSIGN_ROWS].astype(jnp.int32)
    cnt_s[...] = counts
    counts_ref[...] = counts


def _mix_out_call(x2, ya, yb, w, rows):
    tokens = x2.shape[0]
    tok = lambda width: pl.BlockSpec((rows, width), lambda i: (i, 0))
    full = lambda a: pl.BlockSpec(a.shape, lambda i: (0,) * a.ndim)
    consts = (w['w_out_a'], w['w_out_b'], w['ffn_norm_w'], w['w_route'], w['b_route'])
    return pl.pallas_call(
        _mix_out_kernel,
        grid=(tokens // rows,),
        in_specs=[tok(D_MODEL), tok(CONV_CH), tok(GDN_W)] + [full(a) for a in consts],
        out_specs=(tok(D_MODEL), tok(D_MODEL // 2), tok(LANES),
                   pl.BlockSpec((ASSIGN_ROWS, rows), lambda i: (0, i)),
                   pl.BlockSpec((1, LANES), lambda i: (0, 0))),
        out_shape=(jax.ShapeDtypeStruct((tokens, D_MODEL), F32),
                   jax.ShapeDtypeStruct((tokens, D_MODEL // 2), jnp.uint32),
                   jax.ShapeDtypeStruct((tokens, LANES), F32),
                   jax.ShapeDtypeStruct((ASSIGN_ROWS, tokens), jnp.int32),
                   jax.ShapeDtypeStruct((1, LANES), F32)),
        scratch_shapes=[pltpu.VMEM((1, LANES), F32)],
        compiler_params=pltpu.CompilerParams(
            dimension_semantics=("arbitrary",), vmem_limit_bytes=VMEM_LIMIT),
        name="mix_out",
    )(x2, ya, yb, *consts)


def _sc_row_move(table, idx, out_rows, chunk, scatter, name):
    n = idx.shape[0]
    n_src, width = table.shape
    workers = SC_CORES * SC_SUBCORES
    per_w = n // workers
    assert n % workers == 0 and per_w % (2 * chunk) == 0
    assert n_src % per_w == 0 or not scatter
    pairs = per_w // (2 * chunk)
    mesh = plsc.VectorSubcoreMesh(core_axis_name="c", subcore_axis_name="s",
                                  num_cores=SC_CORES, num_subcores=SC_SUBCORES)

    def body(table_hbm, idx_hbm, out_hbm, idx_v, buf_a, buf_b, sem_ra, sem_rb, sem_wa, sem_wb):
        base = (lax.axis_index("s") * SC_CORES + lax.axis_index("c")) * per_w
        src_base = lax.rem(base, n_src)
        pltpu.sync_copy(idx_hbm.at[pl.ds(base, per_w)], idx_v)

        def read(c, buf, sem):
            off = pl.multiple_of(c * chunk, chunk)
            src = (table_hbm.at[pl.ds(src_base + off, chunk)] if scatter
                   else table_hbm.at[idx_v.at[pl.ds(off, chunk)]])
            return pltpu.make_async_copy(src, buf, sem)

        def write(c, buf, sem):
            off = pl.multiple_of(c * chunk, chunk)
            dst = (out_hbm.at[idx_v.at[pl.ds(off, chunk)]] if scatter
                   else out_hbm.at[pl.ds(base + off, chunk)])
            return pltpu.make_async_copy(buf, dst, sem)

        read(0, buf_a, sem_ra).start()

        @pl.loop(0, pairs)
        def _(j):
            ca = 2 * j
            cb = ca + 1
            read(cb, buf_b, sem_rb).start()
            read(ca, buf_a, sem_ra).wait()
            write(ca, buf_a, sem_wa).start()
            read(cb, buf_b, sem_rb).wait()
            write(cb, buf_b, sem_wb).start()
            write(ca, buf_a, sem_wa).wait()

            @pl.when(j + 1 < pairs)
            def _():
                read(ca + 2, buf_a, sem_ra).start()

            write(cb, buf_b, sem_wb).wait()

    return pl.kernel(
        body,
        out_type=jax.ShapeDtypeStruct((out_rows, width), table.dtype),
        mesh=mesh,
        scratch_types=[pltpu.VMEM((per_w,), jnp.int32),
                       pltpu.VMEM((chunk, width), table.dtype),
                       pltpu.VMEM((chunk, width), table.dtype),
                       pltpu.SemaphoreType.DMA, pltpu.SemaphoreType.DMA,
                       pltpu.SemaphoreType.DMA, pltpu.SemaphoreType.DMA],
        name=name,
    )(table, idx)


def _experts_kernel(bexp_ref, bvalid_ref, nused_ref, x_ref, wg_hbm, wu_hbm, wd_hbm, y_ref,
                    wg_s, wu_s, wd_s, stage_g, stage_u, stage_d, sems, seq_s):
    n_used = nused_ref[0]
    n_last = bexp_ref.shape[0] - 1
    rows = x_ref.shape[0] // EXPERT_STEP_BLOCKS
    sub = EXPERT_SUB_ROWS if rows % EXPERT_SUB_ROWS == 0 else rows

    def weight_copies(e, slot):
        return (pltpu.make_async_copy(wg_hbm.at[e], stage_g.at[slot], sems.at[slot, 0]),
                pltpu.make_async_copy(wu_hbm.at[e], stage_u.at[slot], sems.at[slot, 1]),
                pltpu.make_async_copy(wd_hbm.at[e], stage_d.at[slot], sems.at[slot, 2]))

    def block(i, r0):
        expert = bexp_ref[i]
        first_of_expert = (i == 0) | (expert != bexp_ref[jnp.maximum(i - 1, 0)])

        @pl.when((i == 0) & (n_used > 0))
        def _():
            seq_s[0] = 0
            for copy in weight_copies(expert, 0):
                copy.start()

        @pl.when((i < n_used) & first_of_expert)
        def _():
            @pl.when(i > 0)
            def _():
                seq_s[0] = seq_s[0] + 1

            slot = seq_s[0] % 2
            nxt = lax.while_loop(
                lambda j: (j < n_used) & (bexp_ref[jnp.minimum(j, n_last)] == expert),
                lambda j: j + 1, i + 1)

            @pl.when(nxt < n_used)
            def _():
                for copy in weight_copies(bexp_ref[jnp.minimum(nxt, n_last)], 1 - slot):
                    copy.start()

            for copy in weight_copies(expert, slot):
                copy.wait()
            wg_s[...] = stage_g[slot].astype(BF16)
            wu_s[...] = stage_u[slot].astype(BF16)
            wd_s[...] = stage_d[slot].astype(BF16)

        valid = bvalid_ref[i]

        def ffn(n_sub):
            blocks = [slice(r0 + k * sub, r0 + (k + 1) * sub) for k in range(n_sub)]
            row = lax.broadcasted_iota(jnp.int32, (sub, x_ref.shape[1]), 0)
            xb = [jnp.concatenate(_unpack_bf16_pairs(
                      jnp.where(row + (rs.start - r0) < valid, x_ref[rs, :], jnp.uint32(0))),
                      axis=1).astype(BF16) for rs in blocks]
            gate = [_dot(xb_i, wg_s[...]) for xb_i in xb]
            up = [_dot(xb_i, wu_s[...]) for xb_i in xb]
            hid = [((g_i * _sigmoid(g_i)) * u_i).astype(BF16) for g_i, u_i in zip(gate, up)]
            y = [_dot(h_i, wd_s[...]) for h_i in hid]
            for rs, y_i in zip(blocks, y):
                y_ref[rs, :] = _pack_bf16_pairs(y_i.astype(BF16))
            rest = rows - n_sub * sub
            if rest:
                y_ref[r0 + n_sub * sub:r0 + rows, :] = jnp.zeros((rest, y_ref.shape[1]), jnp.uint32)

        tail_sub = (rows // sub + 1) // 2
        pl.when((i < n_used) & (valid > tail_sub * sub))(lambda: ffn(rows // sub))
        pl.when((i < n_used) & (valid <= tail_sub * sub))(lambda: ffn(tail_sub))
        pl.when(i >= n_used)(lambda: ffn(0))

    for j in range(EXPERT_STEP_BLOCKS):
        block(pl.program_id(0) * EXPERT_STEP_BLOCKS + j, j * rows)


def _experts_call(block_expert, block_valid, n_used, x_rows, w_gate, w_up, w_down, rows):
    n_blocks = block_expert.shape[0]
    step_rows = EXPERT_STEP_BLOCKS * rows
    hbm = pl.BlockSpec(memory_space=pl.ANY)
    grid_spec = pltpu.PrefetchScalarGridSpec(
        num_scalar_prefetch=3,
        grid=(n_blocks // EXPERT_STEP_BLOCKS,),
        in_specs=[pl.BlockSpec((step_rows, D_MODEL // 2), lambda i, be, bv, nu: (i, 0)),
                  hbm, hbm, hbm],
        out_specs=pl.BlockSpec((step_rows, D_MODEL // 2), lambda i, be, bv, nu: (i, 0)),
        scratch_shapes=[pltpu.VMEM((D_MODEL, D_EXPERT), BF16),
                        pltpu.VMEM((D_MODEL, D_EXPERT), BF16),
                        pltpu.VMEM((D_EXPERT, D_MODEL), BF16),
                        pltpu.VMEM((2, D_MODEL, D_EXPERT), F32),
                        pltpu.VMEM((2, D_MODEL, D_EXPERT), F32),
                        pltpu.VMEM((2, D_EXPERT, D_MODEL), F32),
                        pltpu.SemaphoreType.DMA((2, 3)),
                        pltpu.SMEM((1,), jnp.int32)],
    )
    return pl.pallas_call(
        _experts_kernel,
        grid_spec=grid_spec,
        out_shape=jax.ShapeDtypeStruct((n_blocks * rows, D_MODEL // 2), jnp.uint32),
        compiler_params=pltpu.CompilerParams(
            dimension_semantics=("arbitrary",), vmem_limit_bytes=VMEM_LIMIT),
        name="experts",
    )(block_expert, block_valid, n_used, x_rows, w_gate, w_up, w_down)


def _combine_kernel(y1_ref, y2_ref, h_ref, route_ref, fw_ref, out_ref):
    route = route_ref[...]
    y1 = jnp.concatenate(_unpack_bf16_pairs(y1_ref[...]), axis=1)
    y2 = jnp.concatenate(_unpack_bf16_pairs(y2_ref[...]), axis=1)
    moe = route[:, ROUTE_W1:ROUTE_W1 + 1] * y1 + route[:, ROUTE_W2:ROUTE_W2 + 1] * y2
    h2 = h_ref[...] + moe
    out_ref[...] = h2 * lax.rsqrt(jnp.mean(h2 * h2, axis=-1, keepdims=True) + EPS) * fw_ref[...]


def _combine_call(y_tok, h1, route, final_w, rows):
    tokens = h1.shape[0]
    steps = tokens // rows
    tok = lambda width: pl.BlockSpec((rows, width), lambda i: (i, 0))
    return pl.pallas_call(
        _combine_kernel,
        grid=(steps,),
        in_specs=[tok(D_MODEL // 2), pl.BlockSpec((rows, D_MODEL // 2), lambda i: (i + steps, 0)),
                  tok(D_MODEL), tok(LANES), pl.BlockSpec((1, D_MODEL), lambda i: (0, 0))],
        out_specs=tok(D_MODEL),
        out_shape=jax.ShapeDtypeStruct((tokens, D_MODEL), F32),
        compiler_params=pltpu.CompilerParams(
            dimension_semantics=("arbitrary",), vmem_limit_bytes=VMEM_LIMIT),
        name="combine",
    )(y_tok, y_tok, h1, route, final_w)


def _dispatch_tables(assign, counts, tokens, rows):
    n_blocks = pl.cdiv((tokens * TOP_K + N_EXPERTS * (rows - 1)) // rows,
                       EXPERT_STEP_BLOCKS) * EXPERT_STEP_BLOCKS
    counts = counts[0, :N_EXPERTS].astype(jnp.int32)
    padded = (counts + rows - 1) // rows * rows
    pad_end = jnp.cumsum(padded)
    pad_start = pad_end - padded
    experts = assign[ROUTE_E1:ROUTE_E2 + 1]
    ranks = assign[ROUTE_RANK1:ROUTE_RANK2 + 1]
    is_expert = experts[..., None] == jnp.arange(N_EXPERTS, dtype=jnp.int32)
    dest = (jnp.sum(jnp.where(is_expert, pad_start, 0), axis=-1) + ranks).reshape(-1)
    block_start = jnp.arange(n_blocks, dtype=jnp.int32) * rows
    in_run = (block_start[:, None] >= pad_start[None, :]) & (block_start[:, None] < pad_end[None, :])
    lookup = lambda table: jnp.sum(jnp.where(in_run, table[None, :], 0), axis=1)
    block_expert = lookup(jnp.arange(N_EXPERTS, dtype=jnp.int32))
    block_valid = jnp.clip(lookup(pad_start + counts) - block_start, 0, rows)
    n_used = (pad_end[-1] // rows).astype(jnp.int32).reshape(1)
    return dest, block_expert, block_valid.astype(jnp.int32), n_used, n_blocks


def _prepare_weights(mix_norm_w, w_in, conv_mix_w, conv_mix_norm_w, qkv_conv_w, a_log, dt_bias,
                     gdn_norm_w, w_out, ffn_norm_w, w_group, b_group, w_router, b_router):
    pad_lanes = lambda v: jnp.pad(v.reshape(1, -1), ((0, 0), (0, LANES - v.size)))
    grp = jnp.arange(PROJ_COLS) // CONV_GROUP_W
    gmat = jnp.where(grp[:, None] == grp[None, :], 1.0 / CONV_GROUP_W, 0.0).astype(BF16)
    w_route = jnp.concatenate([w_group, w_router.reshape(D_MODEL, N_EXPERTS)], axis=1)
    w_route = jnp.pad(w_route, ((0, 0), (0, LANES - w_route.shape[1])))
    return dict(
        mix_norm_w=mix_norm_w.reshape(1, -1),
        w_in=w_in,
        conv_mix_w=conv_mix_w,
        conv_mix_norm_w=conv_mix_norm_w.reshape(1, -1),
        gmat=gmat,
        qkv_conv_w=qkv_conv_w,
        a_log=pad_lanes(a_log),
        dt_bias=pad_lanes(dt_bias),
        gdn_norm_w=gdn_norm_w.reshape(1, -1),
        w_out_a=w_out[:CONV_CH].astype(BF16),
        w_out_b=w_out[CONV_CH:].astype(BF16),
        ffn_norm_w=ffn_norm_w.reshape(1, -1),
        w_route=w_route.astype(BF16),
        b_route=pad_lanes(jnp.concatenate([b_group, b_router.reshape(-1)])),
    )


def _tile(n, preferred):
    return preferred if n % preferred == 0 else n


def kernel(x, meta_tokens, mix_norm_w, w_in, conv_mix_w, conv_mix_norm_w, qkv_conv_w, a_log,
           dt_bias, gdn_norm_w, w_out, ffn_norm_w, w_group, b_group, w_router, b_router, w_gate,
           w_up, w_down, final_norm_w):
    assert mix_norm_w.shape[0] == 1, "single-layer kernel"
    batch, seq, _ = x.shape
    assert seq % CHUNK == 0
    w = _prepare_weights(mix_norm_w[0], w_in[0], conv_mix_w[0], conv_mix_norm_w[0], qkv_conv_w[0],
                         a_log[0], dt_bias[0], gdn_norm_w[0], w_out[0], ffn_norm_w[0], w_group[0],
                         b_group[0], w_router[0], b_router[0])

    prefix = jnp.concatenate([jnp.zeros((CHUNK - N_META, D_MODEL), x.dtype),
                              meta_tokens.astype(x.dtype)], axis=0)[None]
    zero_cu = jnp.zeros((HIST, CONV_CH), F32)
    zero_qkv = jnp.zeros((HIST, 3 * GDN_W), F32)
    (_, pq, pk, pv, _, pgb, tail_cu, tail_qkv,
     w['w_a'], w['w_in_bf'], w['w_ab']) = _proj_call(prefix, zero_cu, zero_qkv, w, CHUNK, prefix=True)

    ya, q, k, v, z, gb, _, _ = _proj_call(x, tail_cu[0], tail_qkv[0], w, _tile(seq, PROJ_ROWS))
    yb = _gdn_call(pq, pk, pv, pgb, q, k, v, gb, z, w['gdn_norm_w'], _tile(seq // CHUNK, GDN_CHUNKS))

    tokens = batch * seq
    flat = lambda a: a.reshape(tokens, a.shape[-1])
    h1, xn2, route, assign, counts = _mix_out_call(flat(x), flat(ya), flat(yb), w,
                                           _tile(tokens, MIX_ROWS))

    dest, block_expert, block_valid, n_used, n_blocks = _dispatch_tables(assign, counts, tokens,
                                                                         EXPERT_ROWS)
    x_rows = _sc_row_move(xn2, dest, n_blocks * EXPERT_ROWS, SC_DISPATCH_CHUNK, True,
                          "dispatch_scatter")
    y_rows = _experts_call(block_expert, block_valid, n_used, x_rows, w_gate[0], w_up[0],
                           w_down[0], EXPERT_ROWS)
    y_tok = _sc_row_move(y_rows, dest, TOP_K * tokens, SC_COMBINE_CHUNK, False, "combine_gather")
    out = _combine_call(y_tok, h1, route, final_norm_w.reshape(1, -1), _tile(tokens, COMBINE_ROWS))
    return out.reshape(batch, seq, D_MODEL)
```

```python
import functools

import jax
import jax.numpy as jnp
from jax import lax
from jax.experimental import pallas as pl
from jax.experimental.pallas import tpu as pltpu
from jax.experimental.pallas import tpu_sc as plsc

F32 = jnp.float32
BF16 = jnp.bfloat16
EPS = 1e-6

D_MODEL = 1024
N_META = 16
CONV_CH = 512
CONV_GROUP_W = 64
HEADS = 4
HEAD_DIM = 128
GDN_W = HEADS * HEAD_DIM
QKV_COL0 = 3 * CONV_CH
Z_COL0 = QKV_COL0 + 3 * GDN_W
CHUNK = 64
N_GROUPS = 4
EXPERTS_PER_GROUP = 8
N_EXPERTS = N_GROUPS * EXPERTS_PER_GROUP
TOP_K = 2
D_EXPERT = 512
LANES = 128
ROUTE_E1, ROUTE_E2, ROUTE_RANK1, ROUTE_RANK2, ROUTE_W1, ROUTE_W2 = range(6)
ASSIGN_ROWS = 8
HIST = 8

PROJ_ROWS = 512
PROJ_COLS = 256
PROJ_SUB_ROWS = 128
GDN_CHUNKS = 8
GDN_GROUP = 4
MIX_ROWS = 1024
MIX_SUB_ROWS = 128
EXPERT_ROWS = 256
EXPERT_SUB_ROWS = 128
EXPERT_STEP_BLOCKS = 4
COMBINE_ROWS = 1024
VMEM_LIMIT = 56 * 1024 * 1024
SC_CORES = 2
SC_SUBCORES = 16
SC_DISPATCH_CHUNK = 64
SC_COMBINE_CHUNK = 64


def _dot(a, b):
    return jnp.dot(a, b, preferred_element_type=F32)


def _dot_nt(a, b):
    return lax.dot_general(a, b, (((1,), (1,)), ((), ())), preferred_element_type=F32)


def _dot_tn(a, b):
    return lax.dot_general(a, b, (((0,), (0,)), ((), ())), preferred_element_type=F32)


def _split_bf16(x):
    hi = x.astype(BF16)
    lo = (x - hi.astype(F32)).astype(BF16)
    return hi, lo


def _sigmoid(x):
    return 1.0 / (1.0 + jnp.exp(-x))


def _pack_bf16_pairs(x_bf16):
    bits = pltpu.bitcast(x_bf16.astype(F32), jnp.uint32)
    n = x_bf16.shape[1] // 2
    return (bits[:, :n] >> 16) | (bits[:, n:] & jnp.uint32(0xFFFF0000))


def _unpack_bf16_pairs(packed):
    return (pltpu.bitcast(packed << 16, F32),
            pltpu.bitcast(packed & jnp.uint32(0xFFFF0000), F32))


def _proj_kernel(x_ref, hcu_ref, hqkv_ref, nw_ref, wa_ref, wi_ref, wab_ref, cmw_ref,
                 cmn_ref, gmat_ref, qcw_ref, alog_ref, dtb_ref,
                 ya_ref, q_ref, k_ref, v_ref, z_ref, gb_ref, tcu_ref, tqkv_ref,
                 cu_s, qkv_s, xn_s):
    rows = x_ref.shape[0]
    sub = PROJ_SUB_ROWS if rows % PROJ_SUB_ROWS == 0 else rows
    blocks = [slice(r, r + sub) for r in range(0, rows, sub)]

    @pl.when(pl.program_id(1) == 0)
    def _():
        cu_s[0:HIST, :] = hcu_ref[...]
        qkv_s[0:HIST, :] = hqkv_ref[...]

    x = x_ref[...]
    ms = jnp.mean(x * x, axis=-1, keepdims=True)
    xn_s[...] = (x * lax.rsqrt(ms + EPS) * nw_ref[...]).astype(BF16)

    def causal_conv(buf, cur, w_ref, cols, tail_ref):
        taps = w_ref.shape[0]
        acc = pltpu.roll(cur, taps - 1, axis=0) * w_ref[0:1, cols]
        for j in range(1, taps - 1):
            acc = acc + pltpu.roll(cur, taps - 1 - j, axis=0) * w_ref[j:j + 1, cols]
        acc = acc + cur * w_ref[taps - 1:taps, cols]
        buf[HIST:2 * HIST, cols] = cur[:HIST]
        seam = buf[pl.ds(HIST - taps + 1, HIST), cols] * w_ref[0:1, cols]
        for j in range(1, taps):
            seam = seam + buf[pl.ds(HIST - taps + 1 + j, HIST), cols] * w_ref[j:j + 1, cols]
        tail = cur[sub - HIST:]
        buf[0:HIST, cols] = tail
        tail_ref[:, cols] = tail
        return jnp.concatenate([seam, acc[HIST:]], axis=0)

    def mixer_a_tail(i, rs, pa):
        cols = slice(i * PROJ_COLS, (i + 1) * PROJ_COLS)
        cu = pa[:, PROJ_COLS:2 * PROJ_COLS] * pa[:, 2 * PROJ_COLS:]
        ya = pa[:, :PROJ_COLS] * causal_conv(cu_s, cu, cmw_ref, cols, tcu_ref)
        msg = _dot((ya * ya).astype(BF16), gmat_ref[...])
        ya_ref[rs, cols] = (ya * lax.rsqrt(msg + EPS) * cmn_ref[:, cols]).astype(BF16)

    heads_per_chunk = PROJ_COLS // HEAD_DIM

    def qkv_tail(i, rs, pq):
        cols = slice(i * PROJ_COLS, (i + 1) * PROJ_COLS)
        c = causal_conv(qkv_s, pq, qcw_ref, cols, tqkv_ref)
        c = c * _sigmoid(c)
        part, first_head = divmod(i * heads_per_chunk, HEADS)
        for j in range(heads_per_chunk):
            ch = c[:, j * HEAD_DIM:(j + 1) * HEAD_DIM]
            sl = slice((first_head + j) * HEAD_DIM, (first_head + j + 1) * HEAD_DIM)
            if part == 0:
                norm = lax.rsqrt(jnp.sum(ch * ch, axis=-1, keepdims=True) + EPS)
                q_ref[rs, sl] = ch * norm * (HEAD_DIM ** -0.5)
            elif part == 1:
                k_ref[rs, sl] = ch * lax.rsqrt(jnp.sum(ch * ch, axis=-1, keepdims=True) + EPS)
            else:
                v_ref[rs, sl] = ch

    def z_tail(rs, pz):
        z_ref[rs, :] = pz

    def decay_beta_tail(rs, ab):
        sp_in = ab + dtb_ref[...]
        softplus = jnp.maximum(sp_in, 0.0) + jnp.log1p(jnp.exp(-jnp.abs(sp_in)))
        g = -jnp.exp(alog_ref[...]) * softplus
        lane = lax.broadcasted_iota(jnp.int32, ab.shape, 1)
        gb_ref[rs, :] = jnp.where(lane < HEADS, g, jnp.where(lane < 2 * HEADS, _sigmoid(ab), 0.0))

    def matmul(w_ref, w_cols, rs):
        return _dot(xn_s[rs, :], w_ref[:, w_cols])

    stages = []
    for i in range(CONV_CH // PROJ_COLS):
        w_cols = slice(3 * i * PROJ_COLS, 3 * (i + 1) * PROJ_COLS)
        stages += [(functools.partial(matmul, wa_ref, w_cols, rs),
                    functools.partial(mixer_a_tail, i, rs)) for rs in blocks]
    for i in range(3 * GDN_W // PROJ_COLS):
        w_cols = slice(QKV_COL0 + i * PROJ_COLS, QKV_COL0 + (i + 1) * PROJ_COLS)
        stages += [(functools.partial(matmul, wi_ref, w_cols, rs),
                    functools.partial(qkv_tail, i, rs)) for rs in blocks]
    stages += [(functools.partial(matmul, wi_ref, slice(Z_COL0, Z_COL0 + GDN_W), rs),
                functools.partial(z_tail, rs)) for rs in blocks]
    stages += [(functools.partial(matmul, wab_ref, slice(None), rs),
                functools.partial(decay_beta_tail, rs)) for rs in blocks]
    pending = None
    for issue, tail in stages:
        res = issue()
        if pending is not None:
            pending()
        pending = functools.partial(tail, res)
    pending()


def _proj_prefix_kernel(x_ref, hcu_ref, hqkv_ref, nw_ref, wi_ref, cmw_ref, cmn_ref, gmat_ref, qcw_ref,
                        alog_ref, dtb_ref,
                        ya_ref, q_ref, k_ref, v_ref, z_ref, gb_ref, tcu_ref, tqkv_ref,
                        wa_ref, wab_ref, cu_s, qkv_s, xn_s):
    for i in range(CONV_CH // PROJ_COLS):
        for part in range(3):
            src = part * CONV_CH + i * PROJ_COLS
            dst = (3 * i + part) * PROJ_COLS
            wa_ref[:, dst:dst + PROJ_COLS] = wi_ref[:, src:src + PROJ_COLS]
    wab_ref[...] = jnp.zeros(wab_ref.shape, BF16)
    wab_ref[:, :2 * HEADS] = wi_ref[:, Z_COL0 + GDN_W:]
    _proj_kernel(x_ref, hcu_ref, hqkv_ref, nw_ref, wa_ref, wi_ref, wab_ref, cmw_ref, cmn_ref,
                 gmat_ref, qcw_ref, alog_ref, dtb_ref, ya_ref, q_ref, k_ref, v_ref, z_ref, gb_ref,
                 tcu_ref, tqkv_ref, cu_s, qkv_s, xn_s)


def _proj_call(x3, hist_cu, hist_qkv, w, rows, prefix=False):
    nb, seq, _ = x3.shape
    nt = seq // rows
    tok = lambda width: pl.BlockSpec((None, rows, width), lambda b, t: (b, t, 0))
    full = lambda a: pl.BlockSpec(a.shape, lambda b, t: (0,) * a.ndim,
                                  pipeline_mode=pl.Buffered(1))
    tail = lambda width: pl.BlockSpec((None, HIST, width), lambda b, t: (b, 0, 0))
    weights = (w['w_in'],) if prefix else (w['w_a'], w['w_in'], w['w_ab'])
    consts = (hist_cu, hist_qkv, w['mix_norm_w']) + weights + (
        w['conv_mix_w'], w['conv_mix_norm_w'], w['gmat'], w['qkv_conv_w'], w['a_log'], w['dt_bias'])
    out_shape = [
        jax.ShapeDtypeStruct((nb, seq, CONV_CH), BF16),
        jax.ShapeDtypeStruct((nb, seq, GDN_W), F32),
        jax.ShapeDtypeStruct((nb, seq, GDN_W), F32),
        jax.ShapeDtypeStruct((nb, seq, GDN_W), F32),
        jax.ShapeDtypeStruct((nb, seq, GDN_W), F32),
        jax.ShapeDtypeStruct((nb, seq, LANES), F32),
        jax.ShapeDtypeStruct((nb, HIST, CONV_CH), F32),
        jax.ShapeDtypeStruct((nb, HIST, 3 * GDN_W), F32),
    ]
    out_specs = [tok(CONV_CH), tok(GDN_W), tok(GDN_W), tok(GDN_W), tok(GDN_W), tok(LANES),
                 tail(CONV_CH), tail(3 * GDN_W)]
    if prefix:
        assert (nb, nt) == (1, 1)
        prepared = [jax.ShapeDtypeStruct((D_MODEL, QKV_COL0), BF16),
                    jax.ShapeDtypeStruct((D_MODEL, LANES), BF16)]
        out_shape += prepared
        out_specs += [pl.BlockSpec(p.shape, lambda b, t: (0, 0)) for p in prepared]
    return pl.pallas_call(
        _proj_prefix_kernel if prefix else _proj_kernel,
        grid=(nb, nt),
        in_specs=[tok(D_MODEL)] + [full(a) for a in consts],
        out_specs=tuple(out_specs),
        out_shape=tuple(out_shape),
        scratch_shapes=[pltpu.VMEM((2 * HIST, CONV_CH), F32),
                        pltpu.VMEM((2 * HIST, 3 * GDN_W), F32),
                        pltpu.VMEM((rows, D_MODEL), BF16)],
        compiler_params=pltpu.CompilerParams(
            dimension_semantics=("arbitrary", "arbitrary"), vmem_limit_bytes=VMEM_LIMIT),
        name="proj_prefix" if prefix else "proj",
    )(x3, *consts)


def _chunk_masks():
    row = lax.broadcasted_iota(jnp.int32, (CHUNK, CHUNK), 0)
    col = lax.broadcasted_iota(jnp.int32, (CHUNK, CHUNK), 1)
    incl = row >= col
    strict = row > col
    levels = []
    n = 1
    while n < CHUNK:
        levels.append((row // (2 * n) == col // (2 * n)) & ((row // n) % 2 == 1) & ((col // n) % 2 == 0))
        n *= 2
    return incl, strict, levels


def _chunk_cumsum(gb_blk, incl):
    tri = incl.astype(BF16)
    hi, lo = _split_bf16(gb_blk)
    return _dot(tri, hi) + _dot(tri, lo)


def _chunk_transforms(chains, masks, state_only):
    incl, strict, levels = masks
    eye = (lax.broadcasted_iota(jnp.int32, (CHUNK, CHUNK), 0)
           == lax.broadcasted_iota(jnp.int32, (CHUNK, CHUNK), 1)).astype(F32)
    decay = [jnp.exp(jnp.where(incl, gc_col - gc_row, -jnp.inf))
             for (_, _, _, _, gc_col, gc_row, _) in chains]
    kb = [kh * beta for (_, kh, _, beta, _, _, _) in chains]
    k_bf = [kh.astype(BF16) for (_, kh, _, _, _, _, _) in chains]
    a_mat = [jnp.where(strict, _dot_nt(kb_i.astype(BF16), k_i) * d_i, 0.0)
             for kb_i, k_i, d_i in zip(kb, k_bf, decay)]
    t_inv = [eye - jnp.where(levels[0], a_i, 0.0) for a_i in a_mat]
    for lvl in levels[1:]:
        t_bf = [t_i.astype(BF16) for t_i in t_inv]
        m1 = [_dot(jnp.where(lvl, a_i, 0.0).astype(BF16), t_i) for a_i, t_i in zip(a_mat, t_bf)]
        t_inv = [t_i - _dot(tb_i, m_i.astype(BF16)) for t_i, tb_i, m_i in zip(t_inv, t_bf, m1)]
    rhs = [jnp.concatenate([vh * beta, kb_i * jnp.exp(gc_col)], axis=1)
           for (_, _, vh, beta, gc_col, _, _), kb_i in zip(chains, kb)]
    uw = [_dot(t_i.astype(BF16), r_i.astype(BF16)).astype(BF16)
          for t_i, r_i in zip(t_inv, rhs)]
    kd = [kh * jnp.exp(g_last - gc_col) for (_, kh, _, _, gc_col, _, g_last) in chains]
    pn = [_dot_tn(kd_i.astype(BF16), uw_i) for kd_i, uw_i in zip(kd, uw)]
    if state_only:
        return [pn_i[:, :HEAD_DIM] for pn_i in pn]
    intra = [jnp.where(incl, _dot_nt(qh.astype(BF16), k_i) * d_i, 0.0)
             for (qh, _, _, _, _, _, _), k_i, d_i in zip(chains, k_bf, decay)]
    iuw = [_dot(in_i.astype(BF16), uw_i) for in_i, uw_i in zip(intra, uw)]
    out = []
    for (qh, _, _, _, gc_col, _, g_last), pn_i, iuw_i in zip(chains, pn, iuw):
        q_part = qh * jnp.exp(gc_col) - iuw_i[:, HEAD_DIM:]
        out.append((q_part, pn_i[:, HEAD_DIM:], iuw_i[:, :HEAD_DIM], pn_i[:, :HEAD_DIM],
                    jnp.exp(g_last)))
    return out


def _gdn_kernel(pq_ref, pk_ref, pv_ref, pgb_ref, q_ref, k_ref, v_ref, gb_ref, z_ref, gnw_ref, yb_ref,
                s_s, qp_s, op_s, n_s, a_s):
    nb = q_ref.shape[0]
    n_chunks = q_ref.shape[1] // CHUNK
    group = GDN_GROUP if n_chunks % GDN_GROUP == 0 else 1
    masks = _chunk_masks()

    def chains_of(gb_blk, q_blk, k_blk, v_blk):
        gc = _chunk_cumsum(gb_blk, masks[0])
        gc_t = gc.T
        res = []
        for h in range(HEADS):
            sl = slice(h * HEAD_DIM, (h + 1) * HEAD_DIM)
            res.append((q_blk(sl), k_blk(sl), v_blk(sl), gb_blk[:, HEADS + h:HEADS + h + 1],
                        gc[:, h:h + 1], gc_t[h:h + 1, :CHUNK], gc[CHUNK - 1:CHUNK, h:h + 1]))
        return res

    @pl.when(pl.program_id(0) == 0)
    def _():
        chains = chains_of(pgb_ref[...], lambda sl: pq_ref[:, sl], lambda sl: pk_ref[:, sl],
                           lambda sl: pv_ref[:, sl])
        for h, n_mat in enumerate(_chunk_transforms(chains, masks, True)):
            for b in range(nb):
                s_s[b * HEADS + h] = n_mat

    def transform_group(gi, carry):
        chains, where = [], []
        for cc in range(group):
            c = gi * group + cc
            rows = pl.ds(pl.multiple_of(c * CHUNK, CHUNK), CHUNK)
            for b in range(nb):
                chains += chains_of(gb_ref[b, rows, :], lambda sl: q_ref[b, rows, sl],
                                    lambda sl: k_ref[b, rows, sl], lambda sl: v_ref[b, rows, sl])
                where += [(c, b * HEADS + h) for h in range(HEADS)]
        for (c, ch), (q_part, p_mat, o_part, n_mat, a) in zip(
                where, _chunk_transforms(chains, masks, False)):
            qp_s[c, ch, 0:CHUNK, :] = q_part.astype(BF16)
            qp_s[c, ch, CHUNK:, :] = p_mat.astype(BF16)
            op_s[c, ch] = o_part
            n_s[c, ch] = n_mat
            a_s[c, ch] = jnp.broadcast_to(a, (8, HEAD_DIM))
        return carry

    lax.fori_loop(0, n_chunks // group, transform_group, 0)

    def scan_chunk(c, carry):
        r0 = pl.multiple_of(c * CHUNK, CHUNK)
        for b in range(nb):
            for h in range(HEADS):
                ch = b * HEADS + h
                s = s_s[ch]
                r = _dot(qp_s[c, ch], s.astype(BF16))
                s_s[ch] = a_s[c, ch][0:1, :] * s - r[CHUNK:] + n_s[c, ch]
                rows, cols = pl.ds(r0, CHUNK), slice(h * HEAD_DIM, (h + 1) * HEAD_DIM)
                o = r[:CHUNK] + op_s[c, ch]
                zh = z_ref[b, rows, cols]
                on = o * lax.rsqrt(jnp.mean(o * o, axis=-1, keepdims=True) + EPS) * gnw_ref[...]
                yb_ref[b, rows, cols] = (on * (zh * _sigmoid(zh))).astype(BF16)
        return carry

    lax.fori_loop(0, n_chunks, scan_chunk, 0, unroll=4 if n_chunks % 4 == 0 else 1)


def _gdn_call(pq, pk, pv, pgb, q, k, v, gb, z, gdn_norm_w, chunks_per_step):
    nb, seq, _ = q.shape
    rows = chunks_per_step * CHUNK
    steps = seq // rows
    tok = lambda width: pl.BlockSpec((nb, rows, width), lambda i: (0, i, 0))
    pre = lambda width: pl.BlockSpec((None, CHUNK, width), lambda i: (0, 0, 0))
    nch = nb * HEADS
    return pl.pallas_call(
        _gdn_kernel,
        grid=(steps,),
        in_specs=[pre(GDN_W), pre(GDN_W), pre(GDN_W), pre(LANES),
                  tok(GDN_W), tok(GDN_W), tok(GDN_W), tok(LANES), tok(GDN_W),
                  pl.BlockSpec((1, HEAD_DIM), lambda i: (0, 0))],
        out_specs=tok(GDN_W),
        out_shape=jax.ShapeDtypeStruct((nb, seq, GDN_W), BF16),
        scratch_shapes=[
            pltpu.VMEM((nch, HEAD_DIM, HEAD_DIM), F32),
            pltpu.VMEM((chunks_per_step, nch, CHUNK + HEAD_DIM, HEAD_DIM), BF16),
            pltpu.VMEM((chunks_per_step, nch, CHUNK, HEAD_DIM), F32),
            pltpu.VMEM((chunks_per_step, nch, HEAD_DIM, HEAD_DIM), F32),
            pltpu.VMEM((chunks_per_step, nch, 8, HEAD_DIM), F32),
        ],
        compiler_params=pltpu.CompilerParams(
            dimension_semantics=("arbitrary",), vmem_limit_bytes=VMEM_LIMIT),
        name="gdn",
    )(pq, pk, pv, pgb, q, k, v, gb, z, gdn_norm_w)


def _mix_out_kernel(x_ref, ya_ref, yb_ref, woa_ref, wob_ref, fnw_ref, wr_ref,
                    br_ref, h_ref, xn_ref, route_ref, assign_ref, counts_ref, cnt_s):
    rows = x_ref.shape[0]
    sub = MIX_SUB_ROWS if rows % MIX_SUB_ROWS == 0 else rows
    blocks = [slice(r, r + sub) for r in range(0, rows, sub)]

    h1 = [x_ref[rs, :] + (_dot(ya_ref[rs, :], woa_ref[...]) + _dot(yb_ref[rs, :], wob_ref[...]))
          for rs in blocks]
    for rs, h1_i in zip(blocks, h1):
        h_ref[rs, :] = h1_i
    xn = [h1_i * lax.rsqrt(jnp.mean(h1_i * h1_i, axis=-1, keepdims=True) + EPS) * fnw_ref[...]
          for h1_i in h1]
    xn_bf = [xn_i.astype(BF16) for xn_i in xn]
    for rs, xn_i in zip(blocks, xn_bf):
        xn_ref[rs, :] = _pack_bf16_pairs(xn_i)
    logits = [_dot(xn_i, wr_ref[...]) + br_ref[...] for xn_i in xn_bf]

    lane = lax.broadcasted_iota(jnp.int32, (sub, LANES), 1).astype(F32)
    neg = -jnp.inf
    big = float(1 << 20)

    def argmax_first(vals):
        m = jnp.max(vals, axis=-1, keepdims=True)
        idx = jnp.min(jnp.where(vals == m, lane, big), axis=-1, keepdims=True)
        return m, idx

    def top_k(lg):
        grp = jnp.where(lane < N_GROUPS, lg, neg)
        g_max, g_sel = argmax_first(grp)
        p_grp = 1.0 / jnp.sum(jnp.exp(grp - g_max), axis=-1, keepdims=True)
        lo_lane = N_GROUPS + g_sel * EXPERTS_PER_GROUP
        ex = jnp.where((lane >= lo_lane) & (lane < lo_lane + EXPERTS_PER_GROUP), lg, neg)
        m1, i1 = argmax_first(ex)
        m2, i2 = argmax_first(jnp.where(lane == i1, neg, ex))
        e2 = jnp.exp(m2 - m1)
        return i1 - N_GROUPS, i2 - N_GROUPS, 1.0 / (1.0 + e2) * p_grp, e2 / (1.0 + e2) * p_grp

    picks = [top_k(lg) for lg in logits]

    @pl.when(pl.program_id(0) == 0)
    def _():
        cnt_s[...] = jnp.zeros(cnt_s.shape, F32)

    onehots = [((lane == e1).astype(F32), (lane == e2).astype(F32)) for e1, e2, _, _ in picks]
    earlier = (lax.broadcasted_iota(jnp.int32, (sub, sub), 0)
               > lax.broadcasted_iota(jnp.int32, (sub, sub), 1)).astype(BF16)
    within = [_dot(earlier, (oh1 + oh2).astype(BF16)) for oh1, oh2 in onehots]
    counts = cnt_s[...]
    for rs, (e1, e2, w1, w2), (oh1, oh2), within_i in zip(blocks, picks, onehots, within):
        before = within_i + counts
        rank1 = jnp.sum(before * oh1, axis=-1, keepdims=True)
        rank2 = jnp.sum(before * oh2, axis=-1, keepdims=True)
        counts = counts + jnp.sum(oh1 + oh2, axis=0, keepdims=True)
        route = jnp.zeros((sub, LANES), F32)
        for k, val in ((ROUTE_E1, e1), (ROUTE_E2, e2), (ROUTE_W1, w1), (ROUTE_W2, w2),
                       (ROUTE_RANK1, rank1), (ROUTE_RANK2, rank2)):
            route = jnp.where(lane == k, val, route)
        route_ref[rs, :] = route
        assign_ref[:, rs] = route.T[:ASSIGN_ROWS].astype(jnp.int32)
    cnt_s[...] = counts
    counts_ref[...] = counts


def _mix_out_call(x2, ya, yb, w, rows):
    tokens = x2.shape[0]
    tok = lambda width: pl.BlockSpec((rows, width), lambda i: (i, 0))
    full = lambda a: pl.BlockSpec(a.shape, lambda i: (0,) * a.ndim)
    consts = (w['w_out_a'], w['w_out_b'], w['ffn_norm_w'], w['w_route'], w['b_route'])
    return pl.pallas_call(
        _mix_out_kernel,
        grid=(tokens // rows,),
        in_specs=[tok(D_MODEL), tok(CONV_CH), tok(GDN_W)] + [full(a) for a in consts],
        out_specs=(tok(D_MODEL), tok(D_MODEL // 2), tok(LANES),
                   pl.BlockSpec((ASSIGN_ROWS, rows), lambda i: (0, i)),
                   pl.BlockSpec((1, LANES), lambda i: (0, 0))),
        out_shape=(jax.ShapeDtypeStruct((tokens, D_MODEL), F32),
                   jax.ShapeDtypeStruct((tokens, D_MODEL // 2), jnp.uint32),
                   jax.ShapeDtypeStruct((tokens, LANES), F32),
                   jax.ShapeDtypeStruct((ASSIGN_ROWS, tokens), jnp.int32),
                   jax.ShapeDtypeStruct((1, LANES), F32)),
        scratch_shapes=[pltpu.VMEM((1, LANES), F32)],
        compiler_params=pltpu.CompilerParams(
            dimension_semantics=("arbitrary",), vmem_limit_bytes=VMEM_LIMIT),
        name="mix_out",
    )(x2, ya, yb, *consts)


def _sc_row_move(table, idx, out_rows, chunk, scatter, name):
    n = idx.shape[0]
    n_src, width = table.shape
    workers = SC_CORES * SC_SUBCORES
    per_w = n // workers
    assert n % workers == 0 and per_w % (2 * chunk) == 0
    assert n_src % per_w == 0 or not scatter
    pairs = per_w // (2 * chunk)
    mesh = plsc.VectorSubcoreMesh(core_axis_name="c", subcore_axis_name="s",
                                  num_cores=SC_CORES, num_subcores=SC_SUBCORES)

    def body(table_hbm, idx_hbm, out_hbm, idx_v, buf_a, buf_b, sem_ra, sem_rb, sem_wa, sem_wb):
        base = (lax.axis_index("s") * SC_CORES + lax.axis_index("c")) * per_w
        src_base = lax.rem(base, n_src)
        pltpu.sync_copy(idx_hbm.at[pl.ds(base, per_w)], idx_v)

        def read(c, buf, sem):
            off = pl.multiple_of(c * chunk, chunk)
            src = (table_hbm.at[pl.ds(src_base + off, chunk)] if scatter
                   else table_hbm.at[idx_v.at[pl.ds(off, chunk)]])
            return pltpu.make_async_copy(src, buf, sem)

        def write(c, buf, sem):
            off = pl.multiple_of(c * chunk, chunk)
            dst = (out_hbm.at[idx_v.at[pl.ds(off, chunk)]] if scatter
                   else out_hbm.at[pl.ds(base + off, chunk)])
            return pltpu.make_async_copy(buf, dst, sem)

        read(0, buf_a, sem_ra).start()

        @pl.loop(0, pairs)
        def _(j):
            ca = 2 * j
            cb = ca + 1
            read(cb, buf_b, sem_rb).start()
            read(ca, buf_a, sem_ra).wait()
            write(ca, buf_a, sem_wa).start()
            read(cb, buf_b, sem_rb).wait()
            write(cb, buf_b, sem_wb).start()
            write(ca, buf_a, sem_wa).wait()

            @pl.when(j + 1 < pairs)
            def _():
                read(ca + 2, buf_a, sem_ra).start()

            write(cb, buf_b, sem_wb).wait()

    return pl.kernel(
        body,
        out_type=jax.ShapeDtypeStruct((out_rows, width), table.dtype),
        mesh=mesh,
        scratch_types=[pltpu.VMEM((per_w,), jnp.int32),
                       pltpu.VMEM((chunk, width), table.dtype),
                       pltpu.VMEM((chunk, width), table.dtype),
                       pltpu.SemaphoreType.DMA, pltpu.SemaphoreType.DMA,
                       pltpu.SemaphoreType.DMA, pltpu.SemaphoreType.DMA],
        name=name,
    )(table, idx)


def _experts_kernel(bexp_ref, bvalid_ref, nused_ref, x_ref, wg_hbm, wu_hbm, wd_hbm, y_ref,
                    wg_s, wu_s, wd_s, stage_g, stage_u, stage_d, sems, seq_s):
    n_used = nused_ref[0]
    n_last = bexp_ref.shape[0] - 1
    rows = x_ref.shape[0] // EXPERT_STEP_BLOCKS
    sub = EXPERT_SUB_ROWS if rows % EXPERT_SUB_ROWS == 0 else rows

    def weight_copies(e, slot):
        return (pltpu.make_async_copy(wg_hbm.at[e], stage_g.at[slot], sems.at[slot, 0]),
                pltpu.make_async_copy(wu_hbm.at[e], stage_u.at[slot], sems.at[slot, 1]),
                pltpu.make_async_copy(wd_hbm.at[e], stage_d.at[slot], sems.at[slot, 2]))

    def block(i, r0):
        expert = bexp_ref[i]
        first_of_expert = (i == 0) | (expert != bexp_ref[jnp.maximum(i - 1, 0)])

        @pl.when((i == 0) & (n_used > 0))
        def _():
            seq_s[0] = 0
            for copy in weight_copies(expert, 0):
                copy.start()

        @pl.when((i < n_used) & first_of_expert)
        def _():
            @pl.when(i > 0)
            def _():
                seq_s[0] = seq_s[0] + 1

            slot = seq_s[0] % 2
            nxt = lax.while_loop(
                lambda j: (j < n_used) & (bexp_ref[jnp.minimum(j, n_last)] == expert),
                lambda j: j + 1, i + 1)

            @pl.when(nxt < n_used)
            def _():
                for copy in weight_copies(bexp_ref[jnp.minimum(nxt, n_last)], 1 - slot):
                    copy.start()

            for copy in weight_copies(expert, slot):
                copy.wait()
            wg_s[...] = stage_g[slot].astype(BF16)
            wu_s[...] = stage_u[slot].astype(BF16)
            wd_s[...] = stage_d[slot].astype(BF16)

        valid = bvalid_ref[i]

        def ffn(n_sub):
            blocks = [slice(r0 + k * sub, r0 + (k + 1) * sub) for k in range(n_sub)]
            row = lax.broadcasted_iota(jnp.int32, (sub, x_ref.shape[1]), 0)
            xb = [jnp.concatenate(_unpack_bf16_pairs(
                      jnp.where(row + (rs.start - r0) < valid, x_ref[rs, :], jnp.uint32(0))),
                      axis=1).astype(BF16) for rs in blocks]
            gate = [_dot(xb_i, wg_s[...]) for xb_i in xb]
            up = [_dot(xb_i, wu_s[...]) for xb_i in xb]
            hid = [((g_i * _sigmoid(g_i)) * u_i).astype(BF16) for g_i, u_i in zip(gate, up)]
            y = [_dot(h_i, wd_s[...]) for h_i in hid]
            for rs, y_i in zip(blocks, y):
                y_ref[rs, :] = _pack_bf16_pairs(y_i.astype(BF16))
            rest = rows - n_sub * sub
            if rest:
                y_ref[r0 + n_sub * sub:r0 + rows, :] = jnp.zeros((rest, y_ref.shape[1]), jnp.uint32)

        tail_sub = (rows // sub + 1) // 2
        pl.when((i < n_used) & (valid > tail_sub * sub))(lambda: ffn(rows // sub))
        pl.when((i < n_used) & (valid <= tail_sub * sub))(lambda: ffn(tail_sub))
        pl.when(i >= n_used)(lambda: ffn(0))

    for j in range(EXPERT_STEP_BLOCKS):
        block(pl.program_id(0) * EXPERT_STEP_BLOCKS + j, j * rows)


def _experts_call(block_expert, block_valid, n_used, x_rows, w_gate, w_up, w_down, rows):
    n_blocks = block_expert.shape[0]
    step_rows = EXPERT_STEP_BLOCKS * rows
    hbm = pl.BlockSpec(memory_space=pl.ANY)
    grid_spec = pltpu.PrefetchScalarGridSpec(
        num_scalar_prefetch=3,
        grid=(n_blocks // EXPERT_STEP_BLOCKS,),
        in_specs=[pl.BlockSpec((step_rows, D_MODEL // 2), lambda i, be, bv, nu: (i, 0)),
                  hbm, hbm, hbm],
        out_specs=pl.BlockSpec((step_rows, D_MODEL // 2), lambda i, be, bv, nu: (i, 0)),
        scratch_shapes=[pltpu.VMEM((D_MODEL, D_EXPERT), BF16),
                        pltpu.VMEM((D_MODEL, D_EXPERT), BF16),
                        pltpu.VMEM((D_EXPERT, D_MODEL), BF16),
                        pltpu.VMEM((2, D_MODEL, D_EXPERT), F32),
                        pltpu.VMEM((2, D_MODEL, D_EXPERT), F32),
                        pltpu.VMEM((2, D_EXPERT, D_MODEL), F32),
                        pltpu.SemaphoreType.DMA((2, 3)),
                        pltpu.SMEM((1,), jnp.int32)],
    )
    return pl.pallas_call(
        _experts_kernel,
        grid_spec=grid_spec,
        out_shape=jax.ShapeDtypeStruct((n_blocks * rows, D_MODEL // 2), jnp.uint32),
        compiler_params=pltpu.CompilerParams(
            dimension_semantics=("arbitrary",), vmem_limit_bytes=VMEM_LIMIT),
        name="experts",
    )(block_expert, block_valid, n_used, x_rows, w_gate, w_up, w_down)


def _combine_kernel(y1_ref, y2_ref, h_ref, route_ref, fw_ref, out_ref):
    route = route_ref[...]
    y1 = jnp.concatenate(_unpack_bf16_pairs(y1_ref[...]), axis=1)
    y2 = jnp.concatenate(_unpack_bf16_pairs(y2_ref[...]), axis=1)
    moe = route[:, ROUTE_W1:ROUTE_W1 + 1] * y1 + route[:, ROUTE_W2:ROUTE_W2 + 1] * y2
    h2 = h_ref[...] + moe
    out_ref[...] = h2 * lax.rsqrt(jnp.mean(h2 * h2, axis=-1, keepdims=True) + EPS) * fw_ref[...]


def _combine_call(y_tok, h1, route, final_w, rows):
    tokens = h1.shape[0]
    steps = tokens // rows
    tok = lambda width: pl.BlockSpec((rows, width), lambda i: (i, 0))
    return pl.pallas_call(
        _combine_kernel,
        grid=(steps,),
        in_specs=[tok(D_MODEL // 2), pl.BlockSpec((rows, D_MODEL // 2), lambda i: (i + steps, 0)),
                  tok(D_MODEL), tok(LANES), pl.BlockSpec((1, D_MODEL), lambda i: (0, 0))],
        out_specs=tok(D_MODEL),
        out_shape=jax.ShapeDtypeStruct((tokens, D_MODEL), F32),
        compiler_params=pltpu.CompilerParams(
            dimension_semantics=("arbitrary",), vmem_limit_bytes=VMEM_LIMIT),
        name="combine",
    )(y_tok, y_tok, h1, route, final_w)


def _dispatch_tables(assign, counts, tokens, rows):
    n_blocks = pl.cdiv((tokens * TOP_K + N_EXPERTS * (rows - 1)) // rows,
                       EXPERT_STEP_BLOCKS) * EXPERT_STEP_BLOCKS
    counts = counts[0, :N_EXPERTS].astype(jnp.int32)
    padded = (counts + rows - 1) // rows * rows
    pad_end = jnp.cumsum(padded)
    pad_start = pad_end - padded
    experts = assign[ROUTE_E1:ROUTE_E2 + 1]
    ranks = assign[ROUTE_RANK1:ROUTE_RANK2 + 1]
    is_expert = experts[..., None] == jnp.arange(N_EXPERTS, dtype=jnp.int32)
    dest = (jnp.sum(jnp.where(is_expert, pad_start, 0), axis=-1) + ranks).reshape(-1)
    block_start = jnp.arange(n_blocks, dtype=jnp.int32) * rows
    in_run = (block_start[:, None] >= pad_start[None, :]) & (block_start[:, None] < pad_end[None, :])
    lookup = lambda table: jnp.sum(jnp.where(in_run, table[None, :], 0), axis=1)
    block_expert = lookup(jnp.arange(N_EXPERTS, dtype=jnp.int32))
    block_valid = jnp.clip(lookup(pad_start + counts) - block_start, 0, rows)
    n_used = (pad_end[-1] // rows).astype(jnp.int32).reshape(1)
    return dest, block_expert, block_valid.astype(jnp.int32), n_used, n_blocks


def _prepare_weights(mix_norm_w, w_in, conv_mix_w, conv_mix_norm_w, qkv_conv_w, a_log, dt_bias,
                     gdn_norm_w, w_out, ffn_norm_w, w_group, b_group, w_router, b_router):
    pad_lanes = lambda v: jnp.pad(v.reshape(1, -1), ((0, 0), (0, LANES - v.size)))
    grp = jnp.arange(PROJ_COLS) // CONV_GROUP_W
    gmat = jnp.where(grp[:, None] == grp[None, :], 1.0 / CONV_GROUP_W, 0.0).astype(BF16)
    w_route = jnp.concatenate([w_group, w_router.reshape(D_MODEL, N_EXPERTS)], axis=1)
    w_route = jnp.pad(w_route, ((0, 0), (0, LANES - w_route.shape[1])))
    return dict(
        mix_norm_w=mix_norm_w.reshape(1, -1),
        w_in=w_in.astype(BF16),
        conv_mix_w=conv_mix_w,
        conv_mix_norm_w=conv_mix_norm_w.reshape(1, -1),
        gmat=gmat,
        qkv_conv_w=qkv_conv_w,
        a_log=pad_lanes(a_log),
        dt_bias=pad_lanes(dt_bias),
        gdn_norm_w=gdn_norm_w.reshape(1, -1),
        w_out_a=w_out[:CONV_CH].astype(BF16),
        w_out_b=w_out[CONV_CH:].astype(BF16),
        ffn_norm_w=ffn_norm_w.reshape(1, -1),
        w_route=w_route.astype(BF16),
        b_route=pad_lanes(jnp.concatenate([b_group, b_router.reshape(-1)])),
    )


def _tile(n, preferred):
    return preferred if n % preferred == 0 else n


def kernel(x, meta_tokens, mix_norm_w, w_in, conv_mix_w, conv_mix_norm_w, qkv_conv_w, a_log,
           dt_bias, gdn_norm_w, w_out, ffn_norm_w, w_group, b_group, w_router, b_router, w_gate,
           w_up, w_down, final_norm_w):
    assert mix_norm_w.shape[0] == 1, "single-layer kernel"
    batch, seq, _ = x.shape
    assert seq % CHUNK == 0
    w = _prepare_weights(mix_norm_w[0], w_in[0], conv_mix_w[0], conv_mix_norm_w[0], qkv_conv_w[0],
                         a_log[0], dt_bias[0], gdn_norm_w[0], w_out[0], ffn_norm_w[0], w_group[0],
                         b_group[0], w_router[0], b_router[0])

    prefix = jnp.concatenate([jnp.zeros((CHUNK - N_META, D_MODEL), x.dtype),
                              meta_tokens.astype(x.dtype)], axis=0)[None]
    zero_cu = jnp.zeros((HIST, CONV_CH), F32)
    zero_qkv = jnp.zeros((HIST, 3 * GDN_W), F32)
    (_, pq, pk, pv, _, pgb, tail_cu, tail_qkv,
     w['w_a'], w['w_ab']) = _proj_call(prefix, zero_cu, zero_qkv, w, CHUNK, prefix=True)

    ya, q, k, v, z, gb, _, _ = _proj_call(x, tail_cu[0], tail_qkv[0], w, _tile(seq, PROJ_ROWS))
    yb = _gdn_call(pq, pk, pv, pgb, q, k, v, gb, z, w['gdn_norm_w'], _tile(seq // CHUNK, GDN_CHUNKS))

    tokens = batch * seq
    flat = lambda a: a.reshape(tokens, a.shape[-1])
    h1, xn2, route, assign, counts = _mix_out_call(flat(x), flat(ya), flat(yb), w,
                                           _tile(tokens, MIX_ROWS))

    dest, block_expert, block_valid, n_used, n_blocks = _dispatch_tables(assign, counts, tokens,
                                                                         EXPERT_ROWS)
    x_rows = _sc_row_move(xn2, dest, n_blocks * EXPERT_ROWS, SC_DISPATCH_CHUNK, True,
                          "dispatch_scatter")
    y_rows = _experts_call(block_expert, block_valid, n_used, x_rows, w_gate[0], w_up[0],
                           w_down[0], EXPERT_ROWS)
    y_tok = _sc_row_move(y_rows, dest, TOP_K * tokens, SC_COMBINE_CHUNK, False, "combine_gather")
    out = _combine_call(y_tok, h1, route, final_norm_w.reshape(1, -1), _tile(tokens, COMBINE_ROWS))
    return out.reshape(batch, seq, D_MODEL)
```

```python
import functools

import jax
import jax.numpy as jnp
from jax import lax
from jax.experimental import pallas as pl
from jax.experimental.pallas import tpu as pltpu
from jax.experimental.pallas import tpu_sc as plsc

F32 = jnp.float32
BF16 = jnp.bfloat16
EPS = 1e-6

D_MODEL = 1024
N_META = 16
CONV_CH = 512
CONV_GROUP_W = 64
HEADS = 4
HEAD_DIM = 128
GDN_W = HEADS * HEAD_DIM
QKV_COL0 = 3 * CONV_CH
Z_COL0 = QKV_COL0 + 3 * GDN_W
CHUNK = 64
N_GROUPS = 4
EXPERTS_PER_GROUP = 8
N_EXPERTS = N_GROUPS * EXPERTS_PER_GROUP
TOP_K = 2
D_EXPERT = 512
LANES = 128
SMALL_A_LOG, SMALL_DT_BIAS, SMALL_B_ROUTE, SMALL_GDN_NORM_W = range(4)
ROUTE_E1, ROUTE_E2, ROUTE_RANK1, ROUTE_RANK2, ROUTE_W1, ROUTE_W2 = range(6)
ASSIGN_ROWS = 8
HIST = 8

PROJ_ROWS = 512
PROJ_COLS = 256
PROJ_SUB_ROWS = 128
GDN_CHUNKS = 8
GDN_GROUP = 4
MIX_ROWS = 1024
MIX_SUB_ROWS = 128
EXPERT_ROWS = 256
EXPERT_SUB_ROWS = 128
EXPERT_STEP_BLOCKS = 4
COMBINE_ROWS = 1024
VMEM_LIMIT = 56 * 1024 * 1024
SC_CORES = 2
SC_SUBCORES = 16
SC_DISPATCH_CHUNK = 64
SC_COMBINE_CHUNK = 64


def _dot(a, b):
    return jnp.dot(a, b, preferred_element_type=F32)


def _dot_nt(a, b):
    return lax.dot_general(a, b, (((1,), (1,)), ((), ())), preferred_element_type=F32)


def _dot_tn(a, b):
    return lax.dot_general(a, b, (((0,), (0,)), ((), ())), preferred_element_type=F32)


def _split_bf16(x):
    hi = x.astype(BF16)
    lo = (x - hi.astype(F32)).astype(BF16)
    return hi, lo


def _sigmoid(x):
    return 1.0 / (1.0 + jnp.exp(-x))


def _pack_bf16_pairs(x_bf16):
    bits = pltpu.bitcast(x_bf16.astype(F32), jnp.uint32)
    n = x_bf16.shape[1] // 2
    return (bits[:, :n] >> 16) | (bits[:, n:] & jnp.uint32(0xFFFF0000))


def _unpack_bf16_pairs(packed):
    return (pltpu.bitcast(packed << 16, F32),
            pltpu.bitcast(packed & jnp.uint32(0xFFFF0000), F32))


def _proj_kernel(x_ref, hcu_ref, hqkv_ref, nw_ref, wa_ref, wi_ref, wab_ref, cmw_ref,
                 cmn_ref, gmat_ref, qcw_ref, small_ref,
                 ya_ref, q_ref, k_ref, v_ref, z_ref, gb_ref, tcu_ref, tqkv_ref,
                 cu_s, qkv_s, xn_s):
    rows = x_ref.shape[0]
    sub = PROJ_SUB_ROWS if rows % PROJ_SUB_ROWS == 0 else rows
    blocks = [slice(r, r + sub) for r in range(0, rows, sub)]

    if hcu_ref is not None:
        @pl.when(pl.program_id(1) == 0)
        def _():
            cu_s[0:HIST, :] = hcu_ref[...]
            qkv_s[0:HIST, :] = hqkv_ref[...]

    x = x_ref[...]
    ms = jnp.mean(x * x, axis=-1, keepdims=True)
    xn_s[...] = (x * lax.rsqrt(ms + EPS) * nw_ref[...]).astype(BF16)

    def causal_conv(buf, cur, w_ref, cols, tail_ref):
        taps = w_ref.shape[0]
        acc = pltpu.roll(cur, taps - 1, axis=0) * w_ref[0:1, cols]
        for j in range(1, taps - 1):
            acc = acc + pltpu.roll(cur, taps - 1 - j, axis=0) * w_ref[j:j + 1, cols]
        acc = acc + cur * w_ref[taps - 1:taps, cols]
        buf[HIST:2 * HIST, cols] = cur[:HIST]
        seam = buf[pl.ds(HIST - taps + 1, HIST), cols] * w_ref[0:1, cols]
        for j in range(1, taps):
            seam = seam + buf[pl.ds(HIST - taps + 1 + j, HIST), cols] * w_ref[j:j + 1, cols]
        tail = cur[sub - HIST:]
        buf[0:HIST, cols] = tail
        tail_ref[:, cols] = tail
        return jnp.concatenate([seam, acc[HIST:]], axis=0)

    def mixer_a_tail(i, rs, pa):
        cols = slice(i * PROJ_COLS, (i + 1) * PROJ_COLS)
        cu = pa[:, PROJ_COLS:2 * PROJ_COLS] * pa[:, 2 * PROJ_COLS:]
        ya = pa[:, :PROJ_COLS] * causal_conv(cu_s, cu, cmw_ref, cols, tcu_ref)
        msg = _dot((ya * ya).astype(BF16), gmat_ref[...])
        ya_ref[rs, cols] = (ya * lax.rsqrt(msg + EPS) * cmn_ref[:, cols]).astype(BF16)

    heads_per_chunk = PROJ_COLS // HEAD_DIM

    def qkv_tail(i, rs, pq):
        cols = slice(i * PROJ_COLS, (i + 1) * PROJ_COLS)
        c = causal_conv(qkv_s, pq, qcw_ref, cols, tqkv_ref)
        c = c * _sigmoid(c)
        part, first_head = divmod(i * heads_per_chunk, HEADS)
        for j in range(heads_per_chunk):
            ch = c[:, j * HEAD_DIM:(j + 1) * HEAD_DIM]
            sl = slice((first_head + j) * HEAD_DIM, (first_head + j + 1) * HEAD_DIM)
            if part == 0:
                norm = lax.rsqrt(jnp.sum(ch * ch, axis=-1, keepdims=True) + EPS)
                q_ref[rs, sl] = ch * norm * (HEAD_DIM ** -0.5)
            elif part == 1:
                k_ref[rs, sl] = ch * lax.rsqrt(jnp.sum(ch * ch, axis=-1, keepdims=True) + EPS)
            else:
                v_ref[rs, sl] = ch

    def z_tail(rs, pz):
        z_ref[rs, :] = pz

    def decay_beta_tail(rs, ab):
        sp_in = ab + small_ref[SMALL_DT_BIAS:SMALL_DT_BIAS + 1, :]
        softplus = jnp.maximum(sp_in, 0.0) + jnp.log1p(jnp.exp(-jnp.abs(sp_in)))
        g = -jnp.exp(small_ref[SMALL_A_LOG:SMALL_A_LOG + 1, :]) * softplus
        lane = lax.broadcasted_iota(jnp.int32, ab.shape, 1)
        gb_ref[rs, :] = jnp.where(lane < HEADS, g, jnp.where(lane < 2 * HEADS, _sigmoid(ab), 0.0))

    def matmul(w_ref, w_cols, rs):
        return _dot(xn_s[rs, :], w_ref[:, w_cols])

    stages = []
    for i in range(CONV_CH // PROJ_COLS):
        w_cols = slice(3 * i * PROJ_COLS, 3 * (i + 1) * PROJ_COLS)
        stages += [(functools.partial(matmul, wa_ref, w_cols, rs),
                    functools.partial(mixer_a_tail, i, rs)) for rs in blocks]
    for i in range(3 * GDN_W // PROJ_COLS):
        w_cols = slice(QKV_COL0 + i * PROJ_COLS, QKV_COL0 + (i + 1) * PROJ_COLS)
        stages += [(functools.partial(matmul, wi_ref, w_cols, rs),
                    functools.partial(qkv_tail, i, rs)) for rs in blocks]
    stages += [(functools.partial(matmul, wi_ref, slice(Z_COL0, Z_COL0 + GDN_W), rs),
                functools.partial(z_tail, rs)) for rs in blocks]
    stages += [(functools.partial(matmul, wab_ref, slice(None), rs),
                functools.partial(decay_beta_tail, rs)) for rs in blocks]
    pending = None
    for issue, tail in stages:
        res = issue()
        if pending is not None:
            pending()
        pending = functools.partial(tail, res)
    pending()


def _proj_prefix_kernel(meta_ref, nw_ref, wi_ref, cmw_ref, cmn_ref, gmat_ref, qcw_ref, small_ref,
                        ya_ref, q_ref, k_ref, v_ref, z_ref, gb_ref, tcu_ref, tqkv_ref,
                        wa_ref, wab_ref, cu_s, qkv_s, xn_s, x_s):
    x_s[...] = jnp.zeros(x_s.shape, F32)
    x_s[CHUNK - N_META:, :] = meta_ref[...]
    cu_s[0:HIST, :] = jnp.zeros((HIST, cu_s.shape[1]), F32)
    qkv_s[0:HIST, :] = jnp.zeros((HIST, qkv_s.shape[1]), F32)
    for i in range(CONV_CH // PROJ_COLS):
        for part in range(3):
            src = part * CONV_CH + i * PROJ_COLS
            dst = (3 * i + part) * PROJ_COLS
            wa_ref[:, dst:dst + PROJ_COLS] = wi_ref[:, src:src + PROJ_COLS]
    wab_ref[...] = jnp.zeros(wab_ref.shape, BF16)
    wab_ref[:, :2 * HEADS] = wi_ref[:, Z_COL0 + GDN_W:]
    _proj_kernel(x_s, None, None, nw_ref, wa_ref, wi_ref, wab_ref, cmw_ref, cmn_ref,
                 gmat_ref, qcw_ref, small_ref, ya_ref, q_ref, k_ref, v_ref, z_ref, gb_ref,
                 tcu_ref, tqkv_ref, cu_s, qkv_s, xn_s)


def _proj_call(x3, hist_cu, hist_qkv, w, rows, prefix=False):
    nb, seq = (1, CHUNK) if prefix else x3.shape[:2]
    nt = seq // rows
    tok = lambda width: pl.BlockSpec((None, rows, width), lambda b, t: (b, t, 0))
    full = lambda a: pl.BlockSpec(a.shape, lambda b, t: (0,) * a.ndim,
                                  pipeline_mode=pl.Buffered(1))
    tail = lambda width: pl.BlockSpec((None, HIST, width), lambda b, t: (b, 0, 0))
    weights = (w['w_in'],) if prefix else (hist_cu, hist_qkv, w['mix_norm_w'], w['w_a'], w['w_in'],
                                           w['w_ab'])
    consts = ((w['mix_norm_w'],) if prefix else ()) + weights + (
        w['conv_mix_w'], w['conv_mix_norm_w'], w['gmat'], w['qkv_conv_w'], w['small'])
    out_shape = [
        jax.ShapeDtypeStruct((nb, seq, CONV_CH), BF16),
        jax.ShapeDtypeStruct((nb, seq, GDN_W), F32),
        jax.ShapeDtypeStruct((nb, seq, GDN_W), F32),
        jax.ShapeDtypeStruct((nb, seq, GDN_W), F32),
        jax.ShapeDtypeStruct((nb, seq, GDN_W), F32),
        jax.ShapeDtypeStruct((nb, seq, LANES), F32),
        jax.ShapeDtypeStruct((nb, HIST, CONV_CH), F32),
        jax.ShapeDtypeStruct((nb, HIST, 3 * GDN_W), F32),
    ]
    out_specs = [tok(CONV_CH), tok(GDN_W), tok(GDN_W), tok(GDN_W), tok(GDN_W), tok(LANES),
                 tail(CONV_CH), tail(3 * GDN_W)]
    if prefix:
        assert (nb, nt) == (1, 1)
        prepared = [jax.ShapeDtypeStruct((D_MODEL, QKV_COL0), BF16),
                    jax.ShapeDtypeStruct((D_MODEL, LANES), BF16)]
        out_shape += prepared
        out_specs += [pl.BlockSpec(p.shape, lambda b, t: (0, 0)) for p in prepared]
    return pl.pallas_call(
        _proj_prefix_kernel if prefix else _proj_kernel,
        grid=(nb, nt),
        in_specs=[full(x3) if prefix else tok(D_MODEL)] + [full(a) for a in consts],
        out_specs=tuple(out_specs),
        out_shape=tuple(out_shape),
        scratch_shapes=[pltpu.VMEM((2 * HIST, CONV_CH), F32),
                        pltpu.VMEM((2 * HIST, 3 * GDN_W), F32),
                        pltpu.VMEM((rows, D_MODEL), BF16)]
        + ([pltpu.VMEM((rows, D_MODEL), F32)] if prefix else []),
        compiler_params=pltpu.CompilerParams(
            dimension_semantics=("arbitrary", "arbitrary"), vmem_limit_bytes=VMEM_LIMIT),
        name="proj_prefix" if prefix else "proj",
    )(x3, *consts)


def _chunk_masks():
    row = lax.broadcasted_iota(jnp.int32, (CHUNK, CHUNK), 0)
    col = lax.broadcasted_iota(jnp.int32, (CHUNK, CHUNK), 1)
    incl = row >= col
    strict = row > col
    levels = []
    n = 1
    while n < CHUNK:
        levels.append((row // (2 * n) == col // (2 * n)) & ((row // n) % 2 == 1) & ((col // n) % 2 == 0))
        n *= 2
    return incl, strict, levels


def _chunk_cumsum(gb_blk, incl):
    tri = incl.astype(BF16)
    hi, lo = _split_bf16(gb_blk)
    return _dot(tri, hi) + _dot(tri, lo)


def _chunk_transforms(chains, masks, state_only):
    incl, strict, levels = masks
    eye = (lax.broadcasted_iota(jnp.int32, (CHUNK, CHUNK), 0)
           == lax.broadcasted_iota(jnp.int32, (CHUNK, CHUNK), 1)).astype(F32)
    decay = [jnp.exp(jnp.where(incl, gc_col - gc_row, -jnp.inf))
             for (_, _, _, _, gc_col, gc_row, _) in chains]
    kb = [kh * beta for (_, kh, _, beta, _, _, _) in chains]
    k_bf = [kh.astype(BF16) for (_, kh, _, _, _, _, _) in chains]
    a_mat = [jnp.where(strict, _dot_nt(kb_i.astype(BF16), k_i) * d_i, 0.0)
             for kb_i, k_i, d_i in zip(kb, k_bf, decay)]
    t_inv = [eye - jnp.where(levels[0], a_i, 0.0) for a_i in a_mat]
    for lvl in levels[1:]:
        t_bf = [t_i.astype(BF16) for t_i in t_inv]
        m1 = [_dot(jnp.where(lvl, a_i, 0.0).astype(BF16), t_i) for a_i, t_i in zip(a_mat, t_bf)]
        t_inv = [t_i - _dot(tb_i, m_i.astype(BF16)) for t_i, tb_i, m_i in zip(t_inv, t_bf, m1)]
    rhs = [jnp.concatenate([vh * beta, kb_i * jnp.exp(gc_col)], axis=1)
           for (_, _, vh, beta, gc_col, _, _), kb_i in zip(chains, kb)]
    uw = [_dot(t_i.astype(BF16), r_i.astype(BF16)).astype(BF16)
          for t_i, r_i in zip(t_inv, rhs)]
    kd = [kh * jnp.exp(g_last - gc_col) for (_, kh, _, _, gc_col, _, g_last) in chains]
    pn = [_dot_tn(kd_i.astype(BF16), uw_i) for kd_i, uw_i in zip(kd, uw)]
    if state_only:
        return [pn_i[:, :HEAD_DIM] for pn_i in pn]
    intra = [jnp.where(incl, _dot_nt(qh.astype(BF16), k_i) * d_i, 0.0)
             for (qh, _, _, _, _, _, _), k_i, d_i in zip(chains, k_bf, decay)]
    iuw = [_dot(in_i.astype(BF16), uw_i) for in_i, uw_i in zip(intra, uw)]
    out = []
    for (qh, _, _, _, gc_col, _, g_last), pn_i, iuw_i in zip(chains, pn, iuw):
        q_part = qh * jnp.exp(gc_col) - iuw_i[:, HEAD_DIM:]
        out.append((q_part, pn_i[:, HEAD_DIM:], iuw_i[:, :HEAD_DIM], pn_i[:, :HEAD_DIM],
                    jnp.exp(g_last)))
    return out


def _gdn_kernel(pq_ref, pk_ref, pv_ref, pgb_ref, q_ref, k_ref, v_ref, gb_ref, z_ref, small_ref, yb_ref,
                s_s, qp_s, op_s, n_s, a_s):
    nb = q_ref.shape[0]
    n_chunks = q_ref.shape[1] // CHUNK
    group = GDN_GROUP if n_chunks % GDN_GROUP == 0 else 1
    masks = _chunk_masks()

    def chains_of(gb_blk, q_blk, k_blk, v_blk):
        gc = _chunk_cumsum(gb_blk, masks[0])
        gc_t = gc.T
        res = []
        for h in range(HEADS):
            sl = slice(h * HEAD_DIM, (h + 1) * HEAD_DIM)
            res.append((q_blk(sl), k_blk(sl), v_blk(sl), gb_blk[:, HEADS + h:HEADS + h + 1],
                        gc[:, h:h + 1], gc_t[h:h + 1, :CHUNK], gc[CHUNK - 1:CHUNK, h:h + 1]))
        return res

    @pl.when(pl.program_id(0) == 0)
    def _():
        chains = chains_of(pgb_ref[...], lambda sl: pq_ref[:, sl], lambda sl: pk_ref[:, sl],
                           lambda sl: pv_ref[:, sl])
        for h, n_mat in enumerate(_chunk_transforms(chains, masks, True)):
            for b in range(nb):
                s_s[b * HEADS + h] = n_mat

    def transform_group(gi, carry):
        chains, where = [], []
        for cc in range(group):
            c = gi * group + cc
            rows = pl.ds(pl.multiple_of(c * CHUNK, CHUNK), CHUNK)
            for b in range(nb):
                chains += chains_of(gb_ref[b, rows, :], lambda sl: q_ref[b, rows, sl],
                                    lambda sl: k_ref[b, rows, sl], lambda sl: v_ref[b, rows, sl])
                where += [(c, b * HEADS + h) for h in range(HEADS)]
        for (c, ch), (q_part, p_mat, o_part, n_mat, a) in zip(
                where, _chunk_transforms(chains, masks, False)):
            qp_s[c, ch, 0:CHUNK, :] = q_part.astype(BF16)
            qp_s[c, ch, CHUNK:, :] = p_mat.astype(BF16)
            op_s[c, ch] = o_part
            n_s[c, ch] = n_mat
            a_s[c, ch] = jnp.broadcast_to(a, (8, HEAD_DIM))
        return carry

    lax.fori_loop(0, n_chunks // group, transform_group, 0)

    def scan_chunk(c, carry):
        r0 = pl.multiple_of(c * CHUNK, CHUNK)
        for b in range(nb):
            for h in range(HEADS):
                ch = b * HEADS + h
                s = s_s[ch]
                r = _dot(qp_s[c, ch], s.astype(BF16))
                s_s[ch] = a_s[c, ch][0:1, :] * s - r[CHUNK:] + n_s[c, ch]
                rows, cols = pl.ds(r0, CHUNK), slice(h * HEAD_DIM, (h + 1) * HEAD_DIM)
                o = r[:CHUNK] + op_s[c, ch]
                zh = z_ref[b, rows, cols]
                on = (o * lax.rsqrt(jnp.mean(o * o, axis=-1, keepdims=True) + EPS)
                      * small_ref[SMALL_GDN_NORM_W:SMALL_GDN_NORM_W + 1, :])
                yb_ref[b, rows, cols] = (on * (zh * _sigmoid(zh))).astype(BF16)
        return carry

    lax.fori_loop(0, n_chunks, scan_chunk, 0, unroll=4 if n_chunks % 4 == 0 else 1)


def _gdn_call(pq, pk, pv, pgb, q, k, v, gb, z, small, chunks_per_step):
    nb, seq, _ = q.shape
    rows = chunks_per_step * CHUNK
    steps = seq // rows
    tok = lambda width: pl.BlockSpec((nb, rows, width), lambda i: (0, i, 0))
    pre = lambda width: pl.BlockSpec((None, CHUNK, width), lambda i: (0, 0, 0))
    nch = nb * HEADS
    return pl.pallas_call(
        _gdn_kernel,
        grid=(steps,),
        in_specs=[pre(GDN_W), pre(GDN_W), pre(GDN_W), pre(LANES),
                  tok(GDN_W), tok(GDN_W), tok(GDN_W), tok(LANES), tok(GDN_W),
                  pl.BlockSpec(small.shape, lambda i: (0, 0))],
        out_specs=tok(GDN_W),
        out_shape=jax.ShapeDtypeStruct((nb, seq, GDN_W), BF16),
        scratch_shapes=[
            pltpu.VMEM((nch, HEAD_DIM, HEAD_DIM), F32),
            pltpu.VMEM((chunks_per_step, nch, CHUNK + HEAD_DIM, HEAD_DIM), BF16),
            pltpu.VMEM((chunks_per_step, nch, CHUNK, HEAD_DIM), F32),
            pltpu.VMEM((chunks_per_step, nch, HEAD_DIM, HEAD_DIM), F32),
            pltpu.VMEM((chunks_per_step, nch, 8, HEAD_DIM), F32),
        ],
        compiler_params=pltpu.CompilerParams(
            dimension_semantics=("arbitrary",), vmem_limit_bytes=VMEM_LIMIT),
        name="gdn",
    )(pq, pk, pv, pgb, q, k, v, gb, z, small)


def _mix_out_kernel(x_ref, ya_ref, yb_ref, woa_ref, wob_ref, fnw_ref, wr_ref,
                    small_ref, h_ref, xn_ref, route_ref, assign_ref, counts_ref, cnt_s):
    rows = x_ref.shape[0]
    sub = MIX_SUB_ROWS if rows % MIX_SUB_ROWS == 0 else rows
    blocks = [slice(r, r + sub) for r in range(0, rows, sub)]

    h1 = [x_ref[rs, :] + (_dot(ya_ref[rs, :], woa_ref[...]) + _dot(yb_ref[rs, :], wob_ref[...]))
          for rs in blocks]
    for rs, h1_i in zip(blocks, h1):
        h_ref[rs, :] = h1_i
    xn = [h1_i * lax.rsqrt(jnp.mean(h1_i * h1_i, axis=-1, keepdims=True) + EPS) * fnw_ref[...]
          for h1_i in h1]
    xn_bf = [xn_i.astype(BF16) for xn_i in xn]
    for rs, xn_i in zip(blocks, xn_bf):
        xn_ref[rs, :] = _pack_bf16_pairs(xn_i)
    logits = [_dot(xn_i, wr_ref[...]) + small_ref[SMALL_B_ROUTE:SMALL_B_ROUTE + 1, :] for xn_i in xn_bf]

    lane = lax.broadcasted_iota(jnp.int32, (sub, LANES), 1).astype(F32)
    neg = -jnp.inf
    big = float(1 << 20)

    def argmax_first(vals):
        m = jnp.max(vals, axis=-1, keepdims=True)
        idx = jnp.min(jnp.where(vals == m, lane, big), axis=-1, keepdims=True)
        return m, idx

    def top_k(lg):
        grp = jnp.where(lane < N_GROUPS, lg, neg)
        g_max, g_sel = argmax_first(grp)
        p_grp = 1.0 / jnp.sum(jnp.exp(grp - g_max), axis=-1, keepdims=True)
        lo_lane = N_GROUPS + g_sel * EXPERTS_PER_GROUP
        ex = jnp.where((lane >= lo_lane) & (lane < lo_lane + EXPERTS_PER_GROUP), lg, neg)
        m1, i1 = argmax_first(ex)
        m2, i2 = argmax_first(jnp.where(lane == i1, neg, ex))
        e2 = jnp.exp(m2 - m1)
        return i1 - N_GROUPS, i2 - N_GROUPS, 1.0 / (1.0 + e2) * p_grp, e2 / (1.0 + e2) * p_grp

    picks = [top_k(lg) for lg in logits]

    @pl.when(pl.program_id(0) == 0)
    def _():
        cnt_s[...] = jnp.zeros(cnt_s.shape, F32)

    onehots = [((lane == e1).astype(F32), (lane == e2).astype(F32)) for e1, e2, _, _ in picks]
    earlier = (lax.broadcasted_iota(jnp.int32, (sub, sub), 0)
               > lax.broadcasted_iota(jnp.int32, (sub, sub), 1)).astype(BF16)
    within = [_dot(earlier, (oh1 + oh2).astype(BF16)) for oh1, oh2 in onehots]
    counts = cnt_s[...]
    for rs, (e1, e2, w1, w2), (oh1, oh2), within_i in zip(blocks, picks, onehots, within):
        before = within_i + counts
        rank1 = jnp.sum(before * oh1, axis=-1, keepdims=True)
        rank2 = jnp.sum(before * oh2, axis=-1, keepdims=True)
        counts = counts + jnp.sum(oh1 + oh2, axis=0, keepdims=True)
        route = jnp.zeros((sub, LANES), F32)
        for k, val in ((ROUTE_E1, e1), (ROUTE_E2, e2), (ROUTE_W1, w1), (ROUTE_W2, w2),
                       (ROUTE_RANK1, rank1), (ROUTE_RANK2, rank2)):
            route = jnp.where(lane == k, val, route)
        route_ref[rs, :] = route
        assign_ref[:, rs] = route.T[:ASSIGN_ROWS].astype(jnp.int32)
    cnt_s[...] = counts
    counts_ref[...] = counts


def _mix_out_call(x2, ya, yb, w, rows):
    tokens = x2.shape[0]
    tok = lambda width: pl.BlockSpec((rows, width), lambda i: (i, 0))
    full = lambda a: pl.BlockSpec(a.shape, lambda i: (0,) * a.ndim)
    consts = (w['w_out_a'], w['w_out_b'], w['ffn_norm_w'], w['w_route'], w['small'])
    return pl.pallas_call(
        _mix_out_kernel,
        grid=(tokens // rows,),
        in_specs=[tok(D_MODEL), tok(CONV_CH), tok(GDN_W)] + [full(a) for a in consts],
        out_specs=(tok(D_MODEL), tok(D_MODEL // 2), tok(LANES),
                   pl.BlockSpec((ASSIGN_ROWS, rows), lambda i: (0, i)),
                   pl.BlockSpec((1, LANES), lambda i: (0, 0))),
        out_shape=(jax.ShapeDtypeStruct((tokens, D_MODEL), F32),
                   jax.ShapeDtypeStruct((tokens, D_MODEL // 2), jnp.uint32),
                   jax.ShapeDtypeStruct((tokens, LANES), F32),
                   jax.ShapeDtypeStruct((ASSIGN_ROWS, tokens), jnp.int32),
                   jax.ShapeDtypeStruct((1, LANES), F32)),
        scratch_shapes=[pltpu.VMEM((1, LANES), F32)],
        compiler_params=pltpu.CompilerParams(
            dimension_semantics=("arbitrary",), vmem_limit_bytes=VMEM_LIMIT),
        name="mix_out",
    )(x2, ya, yb, *consts)


def _sc_row_move(table, idx, out_rows, chunk, scatter, name):
    n = idx.shape[0]
    n_src, width = table.shape
    workers = SC_CORES * SC_SUBCORES
    per_w = n // workers
    assert n % workers == 0 and per_w % (2 * chunk) == 0
    assert n_src % per_w == 0 or not scatter
    pairs = per_w // (2 * chunk)
    mesh = plsc.VectorSubcoreMesh(core_axis_name="c", subcore_axis_name="s",
                                  num_cores=SC_CORES, num_subcores=SC_SUBCORES)

    def body(table_hbm, idx_hbm, out_hbm, idx_v, buf_a, buf_b, sem_ra, sem_rb, sem_wa, sem_wb):
        base = (lax.axis_index("s") * SC_CORES + lax.axis_index("c")) * per_w
        src_base = lax.rem(base, n_src)
        pltpu.sync_copy(idx_hbm.at[pl.ds(base, per_w)], idx_v)

        def read(c, buf, sem):
            off = pl.multiple_of(c * chunk, chunk)
            src = (table_hbm.at[pl.ds(src_base + off, chunk)] if scatter
                   else table_hbm.at[idx_v.at[pl.ds(off, chunk)]])
            return pltpu.make_async_copy(src, buf, sem)

        def write(c, buf, sem):
            off = pl.multiple_of(c * chunk, chunk)
            dst = (out_hbm.at[idx_v.at[pl.ds(off, chunk)]] if scatter
                   else out_hbm.at[pl.ds(base + off, chunk)])
            return pltpu.make_async_copy(buf, dst, sem)

        read(0, buf_a, sem_ra).start()

        @pl.loop(0, pairs)
        def _(j):
            ca = 2 * j
            cb = ca + 1
            read(cb, buf_b, sem_rb).start()
            read(ca, buf_a, sem_ra).wait()
            write(ca, buf_a, sem_wa).start()
            read(cb, buf_b, sem_rb).wait()
            write(cb, buf_b, sem_wb).start()
            write(ca, buf_a, sem_wa).wait()

            @pl.when(j + 1 < pairs)
            def _():
                read(ca + 2, buf_a, sem_ra).start()

            write(cb, buf_b, sem_wb).wait()

    return pl.kernel(
        body,
        out_type=jax.ShapeDtypeStruct((out_rows, width), table.dtype),
        mesh=mesh,
        scratch_types=[pltpu.VMEM((per_w,), jnp.int32),
                       pltpu.VMEM((chunk, width), table.dtype),
                       pltpu.VMEM((chunk, width), table.dtype),
                       pltpu.SemaphoreType.DMA, pltpu.SemaphoreType.DMA,
                       pltpu.SemaphoreType.DMA, pltpu.SemaphoreType.DMA],
        name=name,
    )(table, idx)


def _experts_kernel(bexp_ref, bvalid_ref, nused_ref, x_ref, wg_hbm, wu_hbm, wd_hbm, y_ref,
                    wg_s, wu_s, wd_s, stage_g, stage_u, stage_d, sems, seq_s):
    n_used = nused_ref[0]
    n_last = bexp_ref.shape[0] - 1
    rows = x_ref.shape[0] // EXPERT_STEP_BLOCKS
    sub = EXPERT_SUB_ROWS if rows % EXPERT_SUB_ROWS == 0 else rows

    def weight_copies(e, slot):
        return (pltpu.make_async_copy(wg_hbm.at[e], stage_g.at[slot], sems.at[slot, 0]),
                pltpu.make_async_copy(wu_hbm.at[e], stage_u.at[slot], sems.at[slot, 1]),
                pltpu.make_async_copy(wd_hbm.at[e], stage_d.at[slot], sems.at[slot, 2]))

    def block(i, r0):
        expert = bexp_ref[i]
        first_of_expert = (i == 0) | (expert != bexp_ref[jnp.maximum(i - 1, 0)])

        @pl.when((i == 0) & (n_used > 0))
        def _():
            seq_s[0] = 0
            for copy in weight_copies(expert, 0):
                copy.start()

        @pl.when((i < n_used) & first_of_expert)
        def _():
            @pl.when(i > 0)
            def _():
                seq_s[0] = seq_s[0] + 1

            slot = seq_s[0] % 2
            nxt = lax.while_loop(
                lambda j: (j < n_used) & (bexp_ref[jnp.minimum(j, n_last)] == expert),
                lambda j: j + 1, i + 1)

            @pl.when(nxt < n_used)
            def _():
                for copy in weight_copies(bexp_ref[jnp.minimum(nxt, n_last)], 1 - slot):
                    copy.start()

            for copy in weight_copies(expert, slot):
                copy.wait()
            wg_s[...] = stage_g[slot].astype(BF16)
            wu_s[...] = stage_u[slot].astype(BF16)
            wd_s[...] = stage_d[slot].astype(BF16)

        valid = bvalid_ref[i]

        def ffn(n_sub):
            blocks = [slice(r0 + k * sub, r0 + (k + 1) * sub) for k in range(n_sub)]
            row = lax.broadcasted_iota(jnp.int32, (sub, x_ref.shape[1]), 0)
            xb = [jnp.concatenate(_unpack_bf16_pairs(
                      jnp.where(row + (rs.start - r0) < valid, x_ref[rs, :], jnp.uint32(0))),
                      axis=1).astype(BF16) for rs in blocks]
            gate = [_dot(xb_i, wg_s[...]) for xb_i in xb]
            up = [_dot(xb_i, wu_s[...]) for xb_i in xb]
            hid = [((g_i * _sigmoid(g_i)) * u_i).astype(BF16) for g_i, u_i in zip(gate, up)]
            y = [_dot(h_i, wd_s[...]) for h_i in hid]
            for rs, y_i in zip(blocks, y):
                y_ref[rs, :] = _pack_bf16_pairs(y_i.astype(BF16))
            rest = rows - n_sub * sub
            if rest:
                y_ref[r0 + n_sub * sub:r0 + rows, :] = jnp.zeros((rest, y_ref.shape[1]), jnp.uint32)

        tail_sub = (rows // sub + 1) // 2
        pl.when((i < n_used) & (valid > tail_sub * sub))(lambda: ffn(rows // sub))
        pl.when((i < n_used) & (valid <= tail_sub * sub))(lambda: ffn(tail_sub))
        pl.when(i >= n_used)(lambda: ffn(0))

    for j in range(EXPERT_STEP_BLOCKS):
        block(pl.program_id(0) * EXPERT_STEP_BLOCKS + j, j * rows)


def _experts_call(block_expert, block_valid, n_used, x_rows, w_gate, w_up, w_down, rows):
    n_blocks = block_expert.shape[0]
    step_rows = EXPERT_STEP_BLOCKS * rows
    hbm = pl.BlockSpec(memory_space=pl.ANY)
    grid_spec = pltpu.PrefetchScalarGridSpec(
        num_scalar_prefetch=3,
        grid=(n_blocks // EXPERT_STEP_BLOCKS,),
        in_specs=[pl.BlockSpec((step_rows, D_MODEL // 2), lambda i, be, bv, nu: (i, 0)),
                  hbm, hbm, hbm],
        out_specs=pl.BlockSpec((step_rows, D_MODEL // 2), lambda i, be, bv, nu: (i, 0)),
        scratch_shapes=[pltpu.VMEM((D_MODEL, D_EXPERT), BF16),
                        pltpu.VMEM((D_MODEL, D_EXPERT), BF16),
                        pltpu.VMEM((D_EXPERT, D_MODEL), BF16),
                        pltpu.VMEM((2, D_MODEL, D_EXPERT), F32),
                        pltpu.VMEM((2, D_MODEL, D_EXPERT), F32),
                        pltpu.VMEM((2, D_EXPERT, D_MODEL), F32),
                        pltpu.SemaphoreType.DMA((2, 3)),
                        pltpu.SMEM((1,), jnp.int32)],
    )
    return pl.pallas_call(
        _experts_kernel,
        grid_spec=grid_spec,
        out_shape=jax.ShapeDtypeStruct((n_blocks * rows, D_MODEL // 2), jnp.uint32),
        compiler_params=pltpu.CompilerParams(
            dimension_semantics=("arbitrary",), vmem_limit_bytes=VMEM_LIMIT),
        name="experts",
    )(block_expert, block_valid, n_used, x_rows, w_gate, w_up, w_down)


def _combine_kernel(y1_ref, y2_ref, h_ref, route_ref, fw_ref, out_ref):
    route = route_ref[...]
    y1 = jnp.concatenate(_unpack_bf16_pairs(y1_ref[...]), axis=1)
    y2 = jnp.concatenate(_unpack_bf16_pairs(y2_ref[...]), axis=1)
    moe = route[:, ROUTE_W1:ROUTE_W1 + 1] * y1 + route[:, ROUTE_W2:ROUTE_W2 + 1] * y2
    h2 = h_ref[...] + moe
    out_ref[...] = h2 * lax.rsqrt(jnp.mean(h2 * h2, axis=-1, keepdims=True) + EPS) * fw_ref[...]


def _combine_call(y_tok, h1, route, final_w, rows):
    tokens = h1.shape[0]
    steps = tokens // rows
    tok = lambda width: pl.BlockSpec((rows, width), lambda i: (i, 0))
    return pl.pallas_call(
        _combine_kernel,
        grid=(steps,),
        in_specs=[tok(D_MODEL // 2), pl.BlockSpec((rows, D_MODEL // 2), lambda i: (i + steps, 0)),
                  tok(D_MODEL), tok(LANES), pl.BlockSpec((1, D_MODEL), lambda i: (0, 0))],
        out_specs=tok(D_MODEL),
        out_shape=jax.ShapeDtypeStruct((tokens, D_MODEL), F32),
        compiler_params=pltpu.CompilerParams(
            dimension_semantics=("arbitrary",), vmem_limit_bytes=VMEM_LIMIT),
        name="combine",
    )(y_tok, y_tok, h1, route, final_w)


def _dispatch_tables(assign, counts, tokens, rows):
    n_blocks = pl.cdiv((tokens * TOP_K + N_EXPERTS * (rows - 1)) // rows,
                       EXPERT_STEP_BLOCKS) * EXPERT_STEP_BLOCKS
    counts = counts[0, :N_EXPERTS].astype(jnp.int32)
    padded = (counts + rows - 1) // rows * rows
    pad_end = jnp.cumsum(padded)
    pad_start = pad_end - padded
    experts = assign[ROUTE_E1:ROUTE_E2 + 1]
    ranks = assign[ROUTE_RANK1:ROUTE_RANK2 + 1]
    is_expert = experts[..., None] == jnp.arange(N_EXPERTS, dtype=jnp.int32)
    dest = (jnp.sum(jnp.where(is_expert, pad_start, 0), axis=-1) + ranks).reshape(-1)
    block_start = jnp.arange(n_blocks, dtype=jnp.int32) * rows
    in_run = (block_start[:, None] >= pad_start[None, :]) & (block_start[:, None] < pad_end[None, :])
    lookup = lambda table: jnp.sum(jnp.where(in_run, table[None, :], 0), axis=1)
    block_expert = lookup(jnp.arange(N_EXPERTS, dtype=jnp.int32))
    block_valid = jnp.clip(lookup(pad_start + counts) - block_start, 0, rows)
    n_used = (pad_end[-1] // rows).astype(jnp.int32).reshape(1)
    return dest, block_expert, block_valid.astype(jnp.int32), n_used, n_blocks


def _prepare_weights(mix_norm_w, w_in, conv_mix_w, conv_mix_norm_w, qkv_conv_w, a_log, dt_bias,
                     gdn_norm_w, w_out, ffn_norm_w, w_group, b_group, w_router, b_router):
    pad_lanes = lambda v: jnp.pad(v.reshape(1, -1), ((0, 0), (0, LANES - v.size)))
    small_rows = [pad_lanes(a_log), pad_lanes(dt_bias),
                  pad_lanes(jnp.concatenate([b_group, b_router.reshape(-1)])), pad_lanes(gdn_norm_w)]
    grp = jnp.arange(PROJ_COLS) // CONV_GROUP_W
    gmat = jnp.where(grp[:, None] == grp[None, :], 1.0 / CONV_GROUP_W, 0.0).astype(BF16)
    w_route = jnp.concatenate([w_group, w_router.reshape(D_MODEL, N_EXPERTS)], axis=1)
    w_route = jnp.pad(w_route, ((0, 0), (0, LANES - w_route.shape[1])))
    return dict(
        mix_norm_w=mix_norm_w.reshape(1, -1),
        w_in=w_in.astype(BF16),
        conv_mix_w=conv_mix_w,
        conv_mix_norm_w=conv_mix_norm_w.reshape(1, -1),
        gmat=gmat,
        qkv_conv_w=qkv_conv_w,
        small=jnp.concatenate(small_rows + [jnp.zeros((8 - len(small_rows), LANES), F32)], axis=0),
        w_out_a=w_out[:CONV_CH].astype(BF16),
        w_out_b=w_out[CONV_CH:].astype(BF16),
        ffn_norm_w=ffn_norm_w.reshape(1, -1),
        w_route=w_route.astype(BF16),
    )


def _tile(n, preferred):
    return preferred if n % preferred == 0 else n


def kernel(x, meta_tokens, mix_norm_w, w_in, conv_mix_w, conv_mix_norm_w, qkv_conv_w, a_log,
           dt_bias, gdn_norm_w, w_out, ffn_norm_w, w_group, b_group, w_router, b_router, w_gate,
           w_up, w_down, final_norm_w):
    assert mix_norm_w.shape[0] == 1, "single-layer kernel"
    batch, seq, _ = x.shape
    assert seq % CHUNK == 0
    w = _prepare_weights(mix_norm_w[0], w_in[0], conv_mix_w[0], conv_mix_norm_w[0], qkv_conv_w[0],
                         a_log[0], dt_bias[0], gdn_norm_w[0], w_out[0], ffn_norm_w[0], w_group[0],
                         b_group[0], w_router[0], b_router[0])

    (_, pq, pk, pv, _, pgb, tail_cu, tail_qkv,
     w['w_a'], w['w_ab']) = _proj_call(meta_tokens.astype(x.dtype), None, None, w, CHUNK, prefix=True)

    ya, q, k, v, z, gb, _, _ = _proj_call(x, tail_cu[0], tail_qkv[0], w, _tile(seq, PROJ_ROWS))
    yb = _gdn_call(pq, pk, pv, pgb, q, k, v, gb, z, w['small'], _tile(seq // CHUNK, GDN_CHUNKS))

    tokens = batch * seq
    flat = lambda a: a.reshape(tokens, a.shape[-1])
    h1, xn2, route, assign, counts = _mix_out_call(flat(x), flat(ya), flat(yb), w,
                                           _tile(tokens, MIX_ROWS))

    dest, block_expert, block_valid, n_used, n_blocks = _dispatch_tables(assign, counts, tokens,
                                                                         EXPERT_ROWS)
    x_rows = _sc_row_move(xn2, dest, n_blocks * EXPERT_ROWS, SC_DISPATCH_CHUNK, True,
                          "dispatch_scatter")
    y_rows = _experts_call(block_expert, block_valid, n_used, x_rows, w_gate[0], w_up[0],
                           w_down[0], EXPERT_ROWS)
    y_tok = _sc_row_move(y_rows, dest, TOP_K * tokens, SC_COMBINE_CHUNK, False, "combine_gather")
    out = _combine_call(y_tok, h1, route, final_norm_w.reshape(1, -1), _tile(tokens, COMBINE_ROWS))
    return out.reshape(batch, seq, D_MODEL)
```
